```python
import math
import jax, jax.numpy as jnp
from jax import lax
import numpy as np

D_MODEL = 1024
BATCH = 8
SEQ = 8192
DEPTH = 1

CHUNK = 64
N_META = 16
Q_BLOCK = 128
EPS = 1e-6
D_RNN = 1280
RNN_BLOCKS = 10
RNN_BLOCK_DIM = D_RNN // RNN_BLOCKS
CONV_WIDTH = 4
LRU_C = 8.0
N_HEADS = 8
QK_NOPE = 128
QK_ROPE = 64
V_DIM = 128
Q_RANK = 384
KV_RANK = 256
ROPE_THETA = 10000.0
ATTN_SCALE = 1.0 / math.sqrt(QK_NOPE + QK_ROPE)
N_BRANCH = 2
D_FF = ((8 * D_MODEL // 3 + 255) // 256) * 256
IN_SPLITS = (D_RNN, D_RNN, Q_RANK, KV_RANK, QK_ROPE, N_BRANCH * D_MODEL)
D_IN = sum(IN_SPLITS)
D_BRANCH_IN = D_RNN + N_HEADS * V_DIM
PAD_CHUNK = 2 ** 30
NEG = -1e30

kernel_name = "hybrid_rglru_mla_gated_block"


def rmsnorm(x, g):
    xf = x.astype(jnp.float32)
    y = xf * lax.rsqrt(jnp.mean(xf * xf, axis=-1, keepdims=True) + EPS)
    return (y * g.astype(jnp.float32)).astype(x.dtype)


def apply_rope(x, cos, sin):
    x1, x2 = jnp.split(x.astype(jnp.float32), 2, axis=-1)
    return jnp.concatenate([x1 * cos - x2 * sin, x2 * cos + x1 * sin], axis=-1).astype(x.dtype)


def rglru_branch(u_x, u_gate, conv_w, conv_b, w_a, b_a, w_i, b_i, lam):
    B, L, _ = u_x.shape
    xp = jnp.pad(u_x, ((0, 0), (CONV_WIDTH - 1, 0), (0, 0)))
    xc = conv_b + xp[:, 0:L] * conv_w[0]
    for k in range(1, CONV_WIDTH):
        xc = xc + xp[:, k:k + L] * conv_w[k]
    xb = xc.reshape(B, L, RNN_BLOCKS, RNN_BLOCK_DIM)
    r = jax.nn.sigmoid(jnp.einsum('blhi,hij->blhj', xb, w_a).reshape(B, L, D_RNN) + b_a)
    i = jax.nn.sigmoid(jnp.einsum('blhi,hij->blhj', xb, w_i).reshape(B, L, D_RNN) + b_i)
    log_a = LRU_C * r.astype(jnp.float32) * jax.nn.log_sigmoid(lam.astype(jnp.float32))
    a = jnp.exp(log_a)
    b = jnp.sqrt(-jnp.expm1(2.0 * log_a)) * (i * xc).astype(jnp.float32)

    def combine(left, right):
        a_l, b_l = left
        a_r, b_r = right
        return a_l * a_r, a_r * b_l + b_r

    _, h = lax.associative_scan(combine, (a, b), axis=1)
    return h.astype(u_x.dtype) * jax.nn.gelu(u_gate)


def mla_branch(u_q, u_kv, u_kr, q_norm_g, w_uq, kv_norm_g, w_ukv, cos, sin, chunk_id):
    B, L, _ = u_q.shape
    nb = L // Q_BLOCK
    q = (rmsnorm(u_q, q_norm_g) @ w_uq).reshape(B, L, N_HEADS, QK_NOPE + QK_ROPE)
    q_nope, q_rope = q[..., :QK_NOPE], q[..., QK_NOPE:]
    q_rope = apply_rope(q_rope, cos[:, None, :], sin[:, None, :])
    kv = (rmsnorm(u_kv, kv_norm_g) @ w_ukv).reshape(B, L, N_HEADS, QK_NOPE + V_DIM)
    k_nope, v = kv[..., :QK_NOPE], kv[..., QK_NOPE:]
    k_rope = apply_rope(u_kr, cos, sin)

    def to_blocks(t):
        t = t.reshape((B, nb, Q_BLOCK) + t.shape[2:])
        return jnp.moveaxis(t, 1, 0)

    def attend(args):
        qn, qr, qc = args
        s = (jnp.einsum('bqhd,bkhd->bhqk', qn, k_nope)
             + jnp.einsum('bqhr,bkr->bhqk', qr, k_rope)).astype(jnp.float32) * ATTN_SCALE
        mask = chunk_id[None, :] <= qc[:, None]
        s = jnp.where(mask[None, None], s, NEG)
        p = jax.nn.softmax(s, axis=-1).astype(v.dtype)
        return jnp.einsum('bhqk,bkhd->bqhd', p, v)

    o = lax.map(attend, (to_blocks(q_nope), to_blocks(q_rope), chunk_id.reshape(nb, Q_BLOCK)))
    return jnp.moveaxis(o, 0, 1).reshape(B, L, N_HEADS * V_DIM)


def hybrid_layer(h, cos, sin, chunk_id, norm_mix_g, w_in, b_gate, conv_w, conv_b, w_rec_a, b_rec_a,
                 w_rec_i, b_rec_i, lru_lambda, q_norm_g, w_uq, kv_norm_g, w_ukv, w_branch, w_out,
                 norm_ffn_g, w_ffn_in, w_ffn_out):
    B, L, D = h.shape
    z = rmsnorm(h, norm_mix_g)
    u = z @ w_in
    u_x, u_g, u_q, u_kv, u_kr, u_m = jnp.split(u, np.cumsum(IN_SPLITS)[:-1].tolist(), axis=-1)
    y_rnn = rglru_branch(u_x, u_g, conv_w, conv_b, w_rec_a, b_rec_a, w_rec_i, b_rec_i, lru_lambda)
    y_att = mla_branch(u_q, u_kv, u_kr, q_norm_g, w_uq, kv_norm_g, w_ukv, cos, sin, chunk_id)
    p_rnn = y_rnn @ w_branch[:D_RNN]
    p_att = y_att @ w_branch[D_RNN:]
    gates = jax.nn.sigmoid(u_m + b_gate.reshape(-1)).reshape(B, L, N_BRANCH, D)
    mixed = gates[:, :, 0] * p_rnn + gates[:, :, 1] * p_att
    h = h + mixed @ w_out
    zf = rmsnorm(h, norm_ffn_g)
    gate, up = jnp.split(zf @ w_ffn_in, 2, axis=-1)
    return h + (jax.nn.silu(gate) * up) @ w_ffn_out


def _fwd_setup_inputs(seed: int = 0) -> dict:
    key = jax.random.key(seed)
    ks = jax.random.split(key, 24)
    f32 = jnp.float32

    def nrm(k, shape, scale):
        return jax.random.normal(k, shape, f32) * scale

    a0 = jax.random.uniform(ks[11], (DEPTH, D_RNN), f32, 0.9, 0.999)
    return {
        "x": nrm(ks[0], (BATCH, SEQ, D_MODEL), 1.0),
        "meta_tokens": nrm(ks[1], (N_META, D_MODEL), 1.0),
        "norm_mix_g": 1.0 + nrm(ks[2], (DEPTH, D_MODEL), 0.02),
        "w_in": nrm(ks[3], (DEPTH, D_MODEL, D_IN), D_MODEL ** -0.5),
        "b_gate": nrm(ks[4], (DEPTH, N_BRANCH, D_MODEL), 0.02),
        "conv_w": nrm(ks[5], (DEPTH, CONV_WIDTH, D_RNN), CONV_WIDTH ** -0.5),
        "conv_b": nrm(ks[6], (DEPTH, D_RNN), 0.02),
        "w_rec_a": nrm(ks[7], (DEPTH, RNN_BLOCKS, RNN_BLOCK_DIM, RNN_BLOCK_DIM), RNN_BLOCK_DIM ** -0.5),
        "b_rec_a": nrm(ks[8], (DEPTH, D_RNN), 0.02),
        "w_rec_i": nrm(ks[9], (DEPTH, RNN_BLOCKS, RNN_BLOCK_DIM, RNN_BLOCK_DIM), RNN_BLOCK_DIM ** -0.5),
        "b_rec_i": nrm(ks[10], (DEPTH, D_RNN), 0.02),
        "lru_lambda": jnp.log(a0) - jnp.log1p(-a0),
        "q_norm_g": 1.0 + nrm(ks[12], (DEPTH, Q_RANK), 0.02),
        "w_uq": nrm(ks[13], (DEPTH, Q_RANK, N_HEADS * (QK_NOPE + QK_ROPE)), Q_RANK ** -0.5),
        "kv_norm_g": 1.0 + nrm(ks[14], (DEPTH, KV_RANK), 0.02),
        "w_ukv": nrm(ks[15], (DEPTH, KV_RANK, N_HEADS * (QK_NOPE + V_DIM)), KV_RANK ** -0.5),
        "w_branch": nrm(ks[16], (DEPTH, D_BRANCH_IN, D_MODEL), 1024 ** -0.5),
        "w_out": nrm(ks[17], (DEPTH, D_MODEL, D_MODEL), D_MODEL ** -0.5),
        "norm_ffn_g": 1.0 + nrm(ks[18], (DEPTH, D_MODEL), 0.02),
        "w_ffn_in": nrm(ks[19], (DEPTH, D_MODEL, 2 * D_FF), D_MODEL ** -0.5),
        "w_ffn_out": nrm(ks[20], (DEPTH, D_FF, D_MODEL), D_FF ** -0.5),
        "final_norm_g": 1.0 + nrm(ks[21], (D_MODEL,), 0.02),
    }


def _fwd_reference(x, meta_tokens, norm_mix_g, w_in, b_gate, conv_w, conv_b, w_rec_a, b_rec_a, w_rec_i,
              b_rec_i, lru_lambda, q_norm_g, w_uq, kv_norm_g, w_ukv, w_branch, w_out, norm_ffn_g,
              w_ffn_in, w_ffn_out, final_norm_g):
    B, S, D = x.shape
    L = N_META + S
    Lp = ((L + Q_BLOCK - 1) // Q_BLOCK) * Q_BLOCK
    meta = jnp.broadcast_to(meta_tokens.astype(x.dtype)[None], (B, N_META, D))
    h = jnp.pad(jnp.concatenate([meta, x], axis=1), ((0, 0), (0, Lp - L), (0, 0)))
    idx = jnp.arange(Lp, dtype=jnp.int32)
    chunk_id = jnp.where(idx < N_META, 0, (idx - N_META) // CHUNK + 1)
    chunk_id = jnp.where(idx >= L, PAD_CHUNK, chunk_id)
    inv_freq = ROPE_THETA ** (-jnp.arange(0, QK_ROPE, 2, dtype=jnp.float32) / QK_ROPE)
    ang = idx.astype(jnp.float32)[:, None] * inv_freq[None, :]
    cos, sin = jnp.cos(ang), jnp.sin(ang)
    for l in range(DEPTH):
        h = hybrid_layer(h, cos, sin, chunk_id, norm_mix_g[l], w_in[l], b_gate[l], conv_w[l], conv_b[l],
                         w_rec_a[l], b_rec_a[l], w_rec_i[l], b_rec_i[l], lru_lambda[l], q_norm_g[l],
                         w_uq[l], kv_norm_g[l], w_ukv[l], w_branch[l], w_out[l], norm_ffn_g[l],
                         w_ffn_in[l], w_ffn_out[l])
    h = rmsnorm(h, final_norm_g)
    return h[:, N_META:L]


import jax as _jax
import jax.numpy as _jnp

TWIN_FORMAT = 'train_step'
FWD_PARAMS = ['x', 'meta_tokens', 'norm_mix_g', 'w_in', 'b_gate', 'conv_w', 'conv_b', 'w_rec_a', 'b_rec_a', 'w_rec_i', 'b_rec_i', 'lru_lambda', 'q_norm_g', 'w_uq', 'kv_norm_g', 'w_ukv', 'w_branch', 'w_out', 'norm_ffn_g', 'w_ffn_in', 'w_ffn_out', 'final_norm_g']
TWIN_WEIGHTS = ['meta_tokens', 'norm_mix_g', 'w_in', 'b_gate', 'conv_w', 'conv_b', 'w_rec_a', 'b_rec_a', 'w_rec_i', 'b_rec_i', 'lru_lambda', 'q_norm_g', 'w_uq', 'kv_norm_g', 'w_ukv', 'w_branch', 'w_out', 'norm_ffn_g', 'w_ffn_in', 'w_ffn_out', 'final_norm_g']
TWIN_DIFF_INPUT = 'x'
TWIN_INPUTS = ['x', 'meta_tokens', 'norm_mix_g', 'w_in', 'b_gate', 'conv_w', 'conv_b', 'w_rec_a', 'b_rec_a', 'w_rec_i', 'b_rec_i', 'lru_lambda', 'q_norm_g', 'w_uq', 'kv_norm_g', 'w_ukv', 'w_branch', 'w_out', 'norm_ffn_g', 'w_ffn_in', 'w_ffn_out', 'final_norm_g', 'loss_target', 'm_meta_tokens', 'm_norm_mix_g', 'm_w_in', 'm_b_gate', 'm_conv_w', 'm_conv_b', 'm_w_rec_a', 'm_b_rec_a', 'm_w_rec_i', 'm_b_rec_i', 'm_lru_lambda', 'm_q_norm_g', 'm_w_uq', 'm_kv_norm_g', 'm_w_ukv', 'm_w_branch', 'm_w_out', 'm_norm_ffn_g', 'm_w_ffn_in', 'm_w_ffn_out', 'm_final_norm_g', 'v_meta_tokens', 'v_norm_mix_g', 'v_w_in', 'v_b_gate', 'v_conv_w', 'v_conv_b', 'v_w_rec_a', 'v_b_rec_a', 'v_w_rec_i', 'v_b_rec_i', 'v_lru_lambda', 'v_q_norm_g', 'v_w_uq', 'v_kv_norm_g', 'v_w_ukv', 'v_w_branch', 'v_w_out', 'v_norm_ffn_g', 'v_w_ffn_in', 'v_w_ffn_out', 'v_final_norm_g']
TWIN_OUTPUTS = ['loss', 'grad_x', 'grad_meta_tokens', 'grad_norm_mix_g', 'grad_w_in', 'grad_b_gate', 'grad_conv_w', 'grad_conv_b', 'grad_w_rec_a', 'grad_b_rec_a', 'grad_w_rec_i', 'grad_b_rec_i', 'grad_lru_lambda', 'grad_q_norm_g', 'grad_w_uq', 'grad_kv_norm_g', 'grad_w_ukv', 'grad_w_branch', 'grad_w_out', 'grad_norm_ffn_g', 'grad_w_ffn_in', 'grad_w_ffn_out', 'grad_final_norm_g', 'delta_meta_tokens', 'delta_norm_mix_g', 'delta_w_in', 'delta_b_gate', 'delta_conv_w', 'delta_conv_b', 'delta_w_rec_a', 'delta_b_rec_a', 'delta_w_rec_i', 'delta_b_rec_i', 'delta_lru_lambda', 'delta_q_norm_g', 'delta_w_uq', 'delta_kv_norm_g', 'delta_w_ukv', 'delta_w_branch', 'delta_w_out', 'delta_norm_ffn_g', 'delta_w_ffn_in', 'delta_w_ffn_out', 'delta_final_norm_g', 'new_m_meta_tokens', 'new_m_norm_mix_g', 'new_m_w_in', 'new_m_b_gate', 'new_m_conv_w', 'new_m_conv_b', 'new_m_w_rec_a', 'new_m_b_rec_a', 'new_m_w_rec_i', 'new_m_b_rec_i', 'new_m_lru_lambda', 'new_m_q_norm_g', 'new_m_w_uq', 'new_m_kv_norm_g', 'new_m_w_ukv', 'new_m_w_branch', 'new_m_w_out', 'new_m_norm_ffn_g', 'new_m_w_ffn_in', 'new_m_w_ffn_out', 'new_m_final_norm_g', 'new_v_meta_tokens', 'new_v_norm_mix_g', 'new_v_w_in', 'new_v_b_gate', 'new_v_conv_w', 'new_v_conv_b', 'new_v_w_rec_a', 'new_v_b_rec_a', 'new_v_w_rec_i', 'new_v_b_rec_i', 'new_v_lru_lambda', 'new_v_q_norm_g', 'new_v_w_uq', 'new_v_kv_norm_g', 'new_v_w_ukv', 'new_v_w_branch', 'new_v_w_out', 'new_v_norm_ffn_g', 'new_v_w_ffn_in', 'new_v_w_ffn_out', 'new_v_final_norm_g']
TWIN_LEAF_KINDS = {'loss': 'loss', 'grad_x': 'grad_x', 'grad_meta_tokens': 'grad_w', 'grad_norm_mix_g': 'grad_w', 'grad_w_in': 'grad_w', 'grad_b_gate': 'grad_w', 'grad_conv_w': 'grad_w', 'grad_conv_b': 'grad_w', 'grad_w_rec_a': 'grad_w', 'grad_b_rec_a': 'grad_w', 'grad_w_rec_i': 'grad_w', 'grad_b_rec_i': 'grad_w', 'grad_lru_lambda': 'grad_w', 'grad_q_norm_g': 'grad_w', 'grad_w_uq': 'grad_w', 'grad_kv_norm_g': 'grad_w', 'grad_w_ukv': 'grad_w', 'grad_w_branch': 'grad_w', 'grad_w_out': 'grad_w', 'grad_norm_ffn_g': 'grad_w', 'grad_w_ffn_in': 'grad_w', 'grad_w_ffn_out': 'grad_w', 'grad_final_norm_g': 'grad_w', 'delta_meta_tokens': 'delta_w', 'delta_norm_mix_g': 'delta_w', 'delta_w_in': 'delta_w', 'delta_b_gate': 'delta_w', 'delta_conv_w': 'delta_w', 'delta_conv_b': 'delta_w', 'delta_w_rec_a': 'delta_w', 'delta_b_rec_a': 'delta_w', 'delta_w_rec_i': 'delta_w', 'delta_b_rec_i': 'delta_w', 'delta_lru_lambda': 'delta_w', 'delta_q_norm_g': 'delta_w', 'delta_w_uq': 'delta_w', 'delta_kv_norm_g': 'delta_w', 'delta_w_ukv': 'delta_w', 'delta_w_branch': 'delta_w', 'delta_w_out': 'delta_w', 'delta_norm_ffn_g': 'delta_w', 'delta_w_ffn_in': 'delta_w', 'delta_w_ffn_out': 'delta_w', 'delta_final_norm_g': 'delta_w', 'new_m_meta_tokens': 'new_m', 'new_m_norm_mix_g': 'new_m', 'new_m_w_in': 'new_m', 'new_m_b_gate': 'new_m', 'new_m_conv_w': 'new_m', 'new_m_conv_b': 'new_m', 'new_m_w_rec_a': 'new_m', 'new_m_b_rec_a': 'new_m', 'new_m_w_rec_i': 'new_m', 'new_m_b_rec_i': 'new_m', 'new_m_lru_lambda': 'new_m', 'new_m_q_norm_g': 'new_m', 'new_m_w_uq': 'new_m', 'new_m_kv_norm_g': 'new_m', 'new_m_w_ukv': 'new_m', 'new_m_w_branch': 'new_m', 'new_m_w_out': 'new_m', 'new_m_norm_ffn_g': 'new_m', 'new_m_w_ffn_in': 'new_m', 'new_m_w_ffn_out': 'new_m', 'new_m_final_norm_g': 'new_m', 'new_v_meta_tokens': 'new_v', 'new_v_norm_mix_g': 'new_v', 'new_v_w_in': 'new_v', 'new_v_b_gate': 'new_v', 'new_v_conv_w': 'new_v', 'new_v_conv_b': 'new_v', 'new_v_w_rec_a': 'new_v', 'new_v_b_rec_a': 'new_v', 'new_v_w_rec_i': 'new_v', 'new_v_b_rec_i': 'new_v', 'new_v_lru_lambda': 'new_v', 'new_v_q_norm_g': 'new_v', 'new_v_w_uq': 'new_v', 'new_v_kv_norm_g': 'new_v', 'new_v_w_ukv': 'new_v', 'new_v_w_branch': 'new_v', 'new_v_w_out': 'new_v', 'new_v_norm_ffn_g': 'new_v', 'new_v_w_ffn_in': 'new_v', 'new_v_w_ffn_out': 'new_v', 'new_v_final_norm_g': 'new_v'}


def _forward(args):
    return _fwd_reference(*[args[k] for k in FWD_PARAMS])


def _output_shape():
    def fwd():
        inp = _fwd_setup_inputs(0)
        return _fwd_reference(*[inp[k] for k in FWD_PARAMS])
    out = _jax.eval_shape(fwd)
    return out.shape, out.dtype

N_MICROBATCH = 1
ADAM_LR = 0.001
ADAM_B1 = 0.9
ADAM_B2 = 0.999
ADAM_EPS = 1e-08
ADAM_WD = 0.01
ADAM_STEP = 10
PER_EXAMPLE_BATCH_AXIS = {'x': 0, 'loss_target': 0}
SHARED_INPUTS = []
_WEIGHT_DTYPES = {'meta_tokens': _jnp.float32, 'norm_mix_g': _jnp.float32, 'w_in': _jnp.float32, 'b_gate': _jnp.float32, 'conv_w': _jnp.float32, 'conv_b': _jnp.float32, 'w_rec_a': _jnp.float32, 'b_rec_a': _jnp.float32, 'w_rec_i': _jnp.float32, 'b_rec_i': _jnp.float32, 'lru_lambda': _jnp.float32, 'q_norm_g': _jnp.float32, 'w_uq': _jnp.float32, 'kv_norm_g': _jnp.float32, 'w_ukv': _jnp.float32, 'w_branch': _jnp.float32, 'w_out': _jnp.float32, 'norm_ffn_g': _jnp.float32, 'w_ffn_in': _jnp.float32, 'w_ffn_out': _jnp.float32, 'final_norm_g': _jnp.float32}
MOMENT_SCALE = {'meta_tokens': 3.373016e-03, 'norm_mix_g': 1.002230e-01, 'w_in': 4.378823e-02, 'b_gate': 1.772716e-02, 'conv_w': 6.039204e-02, 'conv_b': 2.523223e-01, 'w_rec_a': 1.372689e-02, 'b_rec_a': 1.255807e-02, 'w_rec_i': 2.367297e-02, 'b_rec_i': 2.574864e-02, 'lru_lambda': 2.663122e-02, 'q_norm_g': 3.566234e-02, 'w_uq': 1.729962e-02, 'kv_norm_g': 6.432462e-02, 'w_ukv': 2.002234e-02, 'w_branch': 4.264162e-02, 'w_out': 6.406200e-02, 'norm_ffn_g': 1.845109e-01, 'w_ffn_in': 7.730499e-02, 'w_ffn_out': 1.266691e-01, 'final_norm_g': 6.404175e+01}


def _to_microbatches(a, axis):
    t = _jnp.moveaxis(a, axis, 0)
    t = t.reshape((N_MICROBATCH, t.shape[0] // N_MICROBATCH) + t.shape[1:])
    return _jnp.moveaxis(t, 1, axis + 1)


def setup_inputs(seed: int = 0) -> dict:
    inp = _fwd_setup_inputs(seed)
    key = _jax.random.fold_in(_jax.random.key(seed), 7919)
    shape, _ = _output_shape()
    out = dict(inp)
    out["loss_target"] = _jax.random.normal(_jax.random.fold_in(key, 0), shape, _jnp.float32)
    for i, name in enumerate(TWIN_WEIGHTS):
        w = inp[name].astype(_jnp.float32)
        if MOMENT_SCALE is None:
            s = _jnp.sqrt(_jnp.mean(_jnp.square(w)) + 1e-30)
        else:
            s = MOMENT_SCALE[name]
        km, kv = _jax.random.split(_jax.random.fold_in(key, i + 1))
        out[name] = w
        out["m_" + name] = s * _jax.random.normal(km, w.shape, _jnp.float32)
        out["v_" + name] = (s * s) * _jax.random.uniform(kv, w.shape, _jnp.float32, 0.5, 1.5)
    if N_MICROBATCH > 1:
        for name, axis in PER_EXAMPLE_BATCH_AXIS.items():
            out[name] = _to_microbatches(out[name], axis)
    return {'x': out['x'], 'meta_tokens': out['meta_tokens'], 'norm_mix_g': out['norm_mix_g'], 'w_in': out['w_in'], 'b_gate': out['b_gate'], 'conv_w': out['conv_w'], 'conv_b': out['conv_b'], 'w_rec_a': out['w_rec_a'], 'b_rec_a': out['b_rec_a'], 'w_rec_i': out['w_rec_i'], 'b_rec_i': out['b_rec_i'], 'lru_lambda': out['lru_lambda'], 'q_norm_g': out['q_norm_g'], 'w_uq': out['w_uq'], 'kv_norm_g': out['kv_norm_g'], 'w_ukv': out['w_ukv'], 'w_branch': out['w_branch'], 'w_out': out['w_out'], 'norm_ffn_g': out['norm_ffn_g'], 'w_ffn_in': out['w_ffn_in'], 'w_ffn_out': out['w_ffn_out'], 'final_norm_g': out['final_norm_g'], 'loss_target': out['loss_target'], 'm_meta_tokens': out['m_meta_tokens'], 'm_norm_mix_g': out['m_norm_mix_g'], 'm_w_in': out['m_w_in'], 'm_b_gate': out['m_b_gate'], 'm_conv_w': out['m_conv_w'], 'm_conv_b': out['m_conv_b'], 'm_w_rec_a': out['m_w_rec_a'], 'm_b_rec_a': out['m_b_rec_a'], 'm_w_rec_i': out['m_w_rec_i'], 'm_b_rec_i': out['m_b_rec_i'], 'm_lru_lambda': out['m_lru_lambda'], 'm_q_norm_g': out['m_q_norm_g'], 'm_w_uq': out['m_w_uq'], 'm_kv_norm_g': out['m_kv_norm_g'], 'm_w_ukv': out['m_w_ukv'], 'm_w_branch': out['m_w_branch'], 'm_w_out': out['m_w_out'], 'm_norm_ffn_g': out['m_norm_ffn_g'], 'm_w_ffn_in': out['m_w_ffn_in'], 'm_w_ffn_out': out['m_w_ffn_out'], 'm_final_norm_g': out['m_final_norm_g'], 'v_meta_tokens': out['v_meta_tokens'], 'v_norm_mix_g': out['v_norm_mix_g'], 'v_w_in': out['v_w_in'], 'v_b_gate': out['v_b_gate'], 'v_conv_w': out['v_conv_w'], 'v_conv_b': out['v_conv_b'], 'v_w_rec_a': out['v_w_rec_a'], 'v_b_rec_a': out['v_b_rec_a'], 'v_w_rec_i': out['v_w_rec_i'], 'v_b_rec_i': out['v_b_rec_i'], 'v_lru_lambda': out['v_lru_lambda'], 'v_q_norm_g': out['v_q_norm_g'], 'v_w_uq': out['v_w_uq'], 'v_kv_norm_g': out['v_kv_norm_g'], 'v_w_ukv': out['v_w_ukv'], 'v_w_branch': out['v_w_branch'], 'v_w_out': out['v_w_out'], 'v_norm_ffn_g': out['v_norm_ffn_g'], 'v_w_ffn_in': out['v_w_ffn_in'], 'v_w_ffn_out': out['v_w_ffn_out'], 'v_final_norm_g': out['v_final_norm_g']}


def _loss(weights, diff, rest, loss_target):
    with _jax.named_scope("forward"):
        args = {**rest, TWIN_DIFF_INPUT: diff, **{k: w.astype(_WEIGHT_DTYPES[k]) for k, w in weights.items()}}
        y = _forward(args)
    with _jax.named_scope("loss_head"):
        err = _jnp.square(y.astype(_jnp.float32) - loss_target)
        return 0.5 * _jnp.sum(_jnp.mean(err, axis=-1)) if err.ndim else 0.5 * err


def _adamw(w, g, m, v):
    m = ADAM_B1 * m + (1.0 - ADAM_B1) * g
    v = ADAM_B2 * v + (1.0 - ADAM_B2) * _jnp.square(g)
    m_hat = m / (1.0 - ADAM_B1 ** ADAM_STEP)
    v_hat = v / (1.0 - ADAM_B2 ** ADAM_STEP)
    delta = -ADAM_LR * (m_hat / (_jnp.sqrt(v_hat) + ADAM_EPS) + ADAM_WD * w)
    return delta, m, v


def reference(x, meta_tokens, norm_mix_g, w_in, b_gate, conv_w, conv_b, w_rec_a, b_rec_a, w_rec_i, b_rec_i, lru_lambda, q_norm_g, w_uq, kv_norm_g, w_ukv, w_branch, w_out, norm_ffn_g, w_ffn_in, w_ffn_out, final_norm_g, loss_target, m_meta_tokens, m_norm_mix_g, m_w_in, m_b_gate, m_conv_w, m_conv_b, m_w_rec_a, m_b_rec_a, m_w_rec_i, m_b_rec_i, m_lru_lambda, m_q_norm_g, m_w_uq, m_kv_norm_g, m_w_ukv, m_w_branch, m_w_out, m_norm_ffn_g, m_w_ffn_in, m_w_ffn_out, m_final_norm_g, v_meta_tokens, v_norm_mix_g, v_w_in, v_b_gate, v_conv_w, v_conv_b, v_w_rec_a, v_b_rec_a, v_w_rec_i, v_b_rec_i, v_lru_lambda, v_q_norm_g, v_w_uq, v_kv_norm_g, v_w_ukv, v_w_branch, v_w_out, v_norm_ffn_g, v_w_ffn_in, v_w_ffn_out, v_final_norm_g):
    given = dict(x=x, meta_tokens=meta_tokens, norm_mix_g=norm_mix_g, w_in=w_in, b_gate=b_gate, conv_w=conv_w, conv_b=conv_b, w_rec_a=w_rec_a, b_rec_a=b_rec_a, w_rec_i=w_rec_i, b_rec_i=b_rec_i, lru_lambda=lru_lambda, q_norm_g=q_norm_g, w_uq=w_uq, kv_norm_g=kv_norm_g, w_ukv=w_ukv, w_branch=w_branch, w_out=w_out, norm_ffn_g=norm_ffn_g, w_ffn_in=w_ffn_in, w_ffn_out=w_ffn_out, final_norm_g=final_norm_g, loss_target=loss_target, m_meta_tokens=m_meta_tokens, m_norm_mix_g=m_norm_mix_g, m_w_in=m_w_in, m_b_gate=m_b_gate, m_conv_w=m_conv_w, m_conv_b=m_conv_b, m_w_rec_a=m_w_rec_a, m_b_rec_a=m_b_rec_a, m_w_rec_i=m_w_rec_i, m_b_rec_i=m_b_rec_i, m_lru_lambda=m_lru_lambda, m_q_norm_g=m_q_norm_g, m_w_uq=m_w_uq, m_kv_norm_g=m_kv_norm_g, m_w_ukv=m_w_ukv, m_w_branch=m_w_branch, m_w_out=m_w_out, m_norm_ffn_g=m_norm_ffn_g, m_w_ffn_in=m_w_ffn_in, m_w_ffn_out=m_w_ffn_out, m_final_norm_g=m_final_norm_g, v_meta_tokens=v_meta_tokens, v_norm_mix_g=v_norm_mix_g, v_w_in=v_w_in, v_b_gate=v_b_gate, v_conv_w=v_conv_w, v_conv_b=v_conv_b, v_w_rec_a=v_w_rec_a, v_b_rec_a=v_b_rec_a, v_w_rec_i=v_w_rec_i, v_b_rec_i=v_b_rec_i, v_lru_lambda=v_lru_lambda, v_q_norm_g=v_q_norm_g, v_w_uq=v_w_uq, v_kv_norm_g=v_kv_norm_g, v_w_ukv=v_w_ukv, v_w_branch=v_w_branch, v_w_out=v_w_out, v_norm_ffn_g=v_norm_ffn_g, v_w_ffn_in=v_w_ffn_in, v_w_ffn_out=v_w_ffn_out, v_final_norm_g=v_final_norm_g)
    weights = {n: given[n] for n in TWIN_WEIGHTS}
    shared = {n: given[n] for n in SHARED_INPUTS}
    per_example = {n: given[n] for n in ['x']}
    grad_fn = _jax.value_and_grad(_loss, argnums=(0, 1))

    def one_microbatch(ex, loss_target):
        ex = dict(ex)
        diff = ex.pop(TWIN_DIFF_INPUT)
        return grad_fn(weights, diff, {**shared, **ex}, loss_target)

    if N_MICROBATCH == 1:
        loss, (grad_w, grad_x) = one_microbatch(per_example, given["loss_target"])
    else:
        def body(carry, xs):
            loss_sum, grad_sum = carry
            l_k, (gw_k, gx_k) = one_microbatch(xs[0], xs[1])
            with _jax.named_scope("update"):
                return (loss_sum + l_k, _jax.tree.map(_jnp.add, grad_sum, gw_k)), gx_k

        init = (_jnp.zeros((), _jnp.float32), _jax.tree.map(_jnp.zeros_like, weights))
        (loss, grad_w), grad_x = _jax.lax.scan(body, init, (per_example, given["loss_target"]))
    with _jax.named_scope("update"):
        delta_w, new_m, new_v = {}, {}, {}
        for n in TWIN_WEIGHTS:
            delta_w[n], new_m[n], new_v[n] = _adamw(weights[n], grad_w[n], given["m_" + n], given["v_" + n])
    return (loss, grad_x, *[grad_w[n] for n in TWIN_WEIGHTS], *[delta_w[n] for n in TWIN_WEIGHTS],
            *[new_m[n] for n in TWIN_WEIGHTS], *[new_v[n] for n in TWIN_WEIGHTS])
```

```python
import functools
import math

import jax
import jax.numpy as jnp
from jax import lax
from jax.experimental import pallas as pl
from jax.experimental.pallas import tpu as pltpu

F32 = jnp.float32
BF16 = jnp.bfloat16

D = 1024
DR = 1280
NBLK = 10
RB = 128
CW = 4
NH = 8
NOPE = 128
ROPE = 64
VD = 128
QR = 384
KVR = 256
DFF = 2816
NMETA = 16
EPS = 1e-6
LRU_C = 8.0
ROPE_THETA = 10000.0
SCALE = 1.0 / math.sqrt(NOPE + ROPE)
NEG = -1e30
FRONT = 128
PAD = FRONT - NMETA
QW = 2 * NOPE
LANES = 128
SUB = 128
VMEM_LIMIT = 52 * 1024 * 1024

ADAM_LR = 0.001
ADAM_B1 = 0.9
ADAM_B2 = 0.999
ADAM_EPS = 1e-08
ADAM_WD = 0.01
ADAM_STEP = 10

MESH = pl.DeviceIdType.MESH


def _cparams(sem):
    return pltpu.CompilerParams(dimension_semantics=sem, vmem_limit_bytes=VMEM_LIMIT)


def _sigmoid(x):
    return 1.0 / (1.0 + jnp.exp(-x))


def _gelu_parts(x):
    c = math.sqrt(2.0 / math.pi)
    inner = c * (x + 0.044715 * x * x * x)
    t = jnp.tanh(inner)
    g = 0.5 * x * (1.0 + t)
    dg = 0.5 * (1.0 + t) + 0.5 * x * (1.0 - t * t) * c * (1.0 + 3.0 * 0.044715 * x * x)
    return g, dg


def _mm(a, b, *, name, tm=640, tn=512, out_dtype=F32, res=None):
    M, K = a.shape
    N = b.shape[1]
    tm = min(tm, M)
    tn = min(tn, N)
    assert M % tm == 0 and N % tn == 0, (name, M, N, tm, tn)
    has_res = res is not None

    def body(*refs):
        if has_res:
            a_ref, b_ref, r_ref, o_ref = refs
        else:
            a_ref, b_ref, o_ref = refs
        acc = jnp.dot(a_ref[...].astype(BF16), b_ref[...].astype(BF16), preferred_element_type=F32)
        if has_res:
            acc = acc + r_ref[...].astype(F32)
        o_ref[...] = acc.astype(o_ref.dtype)

    a_bytes = M * K * a.dtype.itemsize
    b_bytes = K * N * b.dtype.itemsize
    rows_outer = a_bytes + (M // tm) * b_bytes <= b_bytes + (N // tn) * a_bytes
    if rows_outer:
        grid = (M // tm, N // tn)
        ia, ib, io = (lambda i, j: (i, 0)), (lambda i, j: (0, j)), (lambda i, j: (i, j))
    else:
        grid = (N // tn, M // tm)
        ia, ib, io = (lambda j, i: (i, 0)), (lambda j, i: (0, j)), (lambda j, i: (i, j))
    in_specs = [pl.BlockSpec((tm, K), ia), pl.BlockSpec((K, tn), ib)]
    args = [a, b]
    if has_res:
        in_specs.append(pl.BlockSpec((tm, tn), io))
        args.append(res)
    return pl.pallas_call(
        body, name=name, grid=grid, in_specs=in_specs,
        out_specs=pl.BlockSpec((tm, tn), io),
        out_shape=jax.ShapeDtypeStruct((M, N), out_dtype),
        compiler_params=_cparams(("parallel", "parallel")),
    )(*args)


def _mm_tn(a, b, *, name, tk=1024, tn=1024, tt=640):
    T, K1 = a.shape
    N = b.shape[1]
    tk = min(tk, K1)
    tn = min(tn, N)
    assert T % tt == 0 and K1 % tk == 0 and N % tn == 0, (name, T, K1, N)

    def body(a_ref, b_ref, o_ref):
        @pl.when(pl.program_id(2) == 0)
        def _():
            o_ref[...] = jnp.zeros_like(o_ref)

        o_ref[...] += lax.dot_general(a_ref[...].astype(BF16), b_ref[...].astype(BF16),
                                      (((0,), (0,)), ((), ())), preferred_element_type=F32)

    return pl.pallas_call(
        body, name=name, grid=(K1 // tk, N // tn, T // tt),
        in_specs=[pl.BlockSpec((tt, tk), lambda i, j, t: (t, i)),
                  pl.BlockSpec((tt, tn), lambda i, j, t: (t, j))],
        out_specs=pl.BlockSpec((tk, tn), lambda i, j, t: (i, j)),
        out_shape=jax.ShapeDtypeStruct((K1, N), F32),
        compiler_params=_cparams(("parallel", "parallel", "arbitrary")),
    )(a, b)


def _rows(tm, w, cb=0):
    return pl.BlockSpec((tm, w), lambda i: (i, cb))


def _const(shape):
    n = len(shape)
    return pl.BlockSpec(shape, lambda i: (0,) * n)


def _rmsnorm_fwd(x, g, *, name, tm=640):
    T, C = x.shape

    def body(x_ref, g_ref, o_ref):
        xv = x_ref[...]
        r = lax.rsqrt(jnp.mean(xv * xv, axis=-1, keepdims=True) + EPS)
        o_ref[...] = ((xv * r) * g_ref[...]).astype(BF16)

    return pl.pallas_call(
        body, name=name, grid=(T // tm,),
        in_specs=[_rows(tm, C), _const((1, C))],
        out_specs=_rows(tm, C),
        out_shape=jax.ShapeDtypeStruct((T, C), BF16),
        compiler_params=_cparams(("parallel",)),
    )(x, g)


def _rmsnorm_bwd(x, g, dy, res, *, name, tm=640, want_f32=True, want_bf16=True):
    T, C = x.shape
    has_res = res is not None

    def body(*refs):
        refs = list(refs)
        x_ref, g_ref, dy_ref = refs[:3]
        refs = refs[3:]
        r_ref = refs.pop(0) if has_res else None
        o32 = refs.pop(0) if want_f32 else None
        o16 = refs.pop(0) if want_bf16 else None
        dg_ref = refs.pop(0)

        @pl.when(pl.program_id(0) == 0)
        def _():
            dg_ref[...] = jnp.zeros_like(dg_ref)

        xv = x_ref[...]
        dyv = dy_ref[...].astype(F32)
        r = lax.rsqrt(jnp.mean(xv * xv, axis=-1, keepdims=True) + EPS)
        xn = xv * r
        dg_ref[...] += jnp.sum(dyv * xn, axis=0, keepdims=True)
        dxn = dyv * g_ref[...]
        dx = r * (dxn - xn * jnp.mean(dxn * xn, axis=-1, keepdims=True))
        if has_res:
            dx = dx + r_ref[...]
        if want_f32:
            o32[...] = dx
        if want_bf16:
            o16[...] = dx.astype(BF16)

    in_specs = [_rows(tm, C), _const((1, C)), _rows(tm, C)]
    args = [x, g, dy]
    if has_res:
        in_specs.append(_rows(tm, C))
        args.append(res)
    out_specs, out_shape = [], []
    if want_f32:
        out_specs.append(_rows(tm, C))
        out_shape.append(jax.ShapeDtypeStruct((T, C), F32))
    if want_bf16:
        out_specs.append(_rows(tm, C))
        out_shape.append(jax.ShapeDtypeStruct((T, C), BF16))
    out_specs.append(_const((1, C)))
    out_shape.append(jax.ShapeDtypeStruct((1, C), F32))
    return pl.pallas_call(
        body, name=name, grid=(T // tm,), in_specs=in_specs, out_specs=out_specs,
        out_shape=out_shape, compiler_params=_cparams(("arbitrary",)),
    )(*args)


def _gate_mix_fwd(um, bg, p_rnn, p_att, *, tm=320):
    T = um.shape[0]

    def body(um_ref, bg_ref, pr_ref, pa_ref, o_ref):
        g = _sigmoid(um_ref[...] + bg_ref[...])
        o_ref[...] = (g[:, :D] * pr_ref[...] + g[:, D:] * pa_ref[...]).astype(BF16)

    return pl.pallas_call(
        body, name="gate_mix_fwd", grid=(T // tm,),
        in_specs=[_rows(tm, 2 * D), _const((1, 2 * D)), _rows(tm, D), _rows(tm, D)],
        out_specs=_rows(tm, D),
        out_shape=jax.ShapeDtypeStruct((T, D), BF16),
        compiler_params=_cparams(("parallel",)),
    )(um, bg, p_rnn, p_att)


def _gate_mix_bwd(um, bg, p_rnn, p_att, dmixed, *, tm=320):
    T = um.shape[0]

    def body(um_ref, bg_ref, pr_ref, pa_ref, dm_ref, dpr_ref, dpa_ref, dum_ref, dbg_ref):
        @pl.when(pl.program_id(0) == 0)
        def _():
            dbg_ref[...] = jnp.zeros_like(dbg_ref)

        g = _sigmoid(um_ref[...] + bg_ref[...])
        g0, g1 = g[:, :D], g[:, D:]
        dm = dm_ref[...]
        dpr_ref[...] = (dm * g0).astype(BF16)
        dpa_ref[...] = (dm * g1).astype(BF16)
        d0 = dm * pr_ref[...] * g0 * (1.0 - g0)
        d1 = dm * pa_ref[...] * g1 * (1.0 - g1)
        dum_ref[:, :D] = d0.astype(BF16)
        dum_ref[:, D:] = d1.astype(BF16)
        dbg_ref[:, :D] += jnp.sum(d0, axis=0, keepdims=True)
        dbg_ref[:, D:] += jnp.sum(d1, axis=0, keepdims=True)

    return pl.pallas_call(
        body, name="gate_mix_bwd", grid=(T // tm,),
        in_specs=[_rows(tm, 2 * D), _const((1, 2 * D)), _rows(tm, D), _rows(tm, D), _rows(tm, D)],
        out_specs=[_rows(tm, D), _rows(tm, D), _rows(tm, 2 * D), _const((1, 2 * D))],
        out_shape=[jax.ShapeDtypeStruct((T, D), BF16), jax.ShapeDtypeStruct((T, D), BF16),
                   jax.ShapeDtypeStruct((T, 2 * D), BF16), jax.ShapeDtypeStruct((1, 2 * D), F32)],
        compiler_params=_cparams(("arbitrary",)),
    )(um, bg, p_rnn, p_att, dmixed)


def _swiglu_fwd(ff, *, tm=320):
    T = ff.shape[0]

    def body(g_ref, u_ref, o_ref):
        gv = g_ref[...]
        o_ref[...] = (gv * _sigmoid(gv) * u_ref[...]).astype(BF16)

    return pl.pallas_call(
        body, name="swiglu_fwd", grid=(T // tm,),
        in_specs=[_rows(tm, DFF, 0), _rows(tm, DFF, 1)],
        out_specs=_rows(tm, DFF),
        out_shape=jax.ShapeDtypeStruct((T, DFF), BF16),
        compiler_params=_cparams(("parallel",)),
    )(ff, ff)


def _swiglu_bwd(ff, dact, *, tm=320):
    T = ff.shape[0]

    def body(g_ref, u_ref, da_ref, o_ref):
        gv = g_ref[...]
        s = _sigmoid(gv)
        da = da_ref[...]
        o_ref[:, :DFF] = (da * u_ref[...] * s * (1.0 + gv * (1.0 - s))).astype(BF16)
        o_ref[:, DFF:] = (da * gv * s).astype(BF16)

    return pl.pallas_call(
        body, name="swiglu_bwd", grid=(T // tm,),
        in_specs=[_rows(tm, DFF, 0), _rows(tm, DFF, 1), _rows(tm, DFF)],
        out_specs=_rows(tm, 2 * DFF),
        out_shape=jax.ShapeDtypeStruct((T, 2 * DFF), BF16),
        compiler_params=_cparams(("parallel",)),
    )(ff, ff, dact)


def _loss_head(h2, tgt, g, *, tm=FRONT):
    T = h2.shape[0]
    front_blocks = FRONT // tm

    def body(h_ref, t_ref, g_ref, d32_ref, d16_ref, dg_ref, ls_ref):
        i = pl.program_id(0)

        @pl.when(i == 0)
        def _():
            dg_ref[...] = jnp.zeros_like(dg_ref)
            ls_ref[...] = jnp.zeros_like(ls_ref)

        xv = h_ref[...]
        r = lax.rsqrt(jnp.mean(xv * xv, axis=-1, keepdims=True) + EPS)
        xn = xv * r
        gv = g_ref[...]
        e = jnp.where(i >= front_blocks, xn * gv - t_ref[...], 0.0)
        ls_ref[...] += jnp.sum(e * e, axis=0, keepdims=True)
        dy = e * (1.0 / D)
        dg_ref[...] += jnp.sum(dy * xn, axis=0, keepdims=True)
        dxn = dy * gv
        dx = r * (dxn - xn * jnp.mean(dxn * xn, axis=-1, keepdims=True))
        d32_ref[...] = dx
        d16_ref[...] = dx.astype(BF16)

    return pl.pallas_call(
        body, name="loss_head", grid=(T // tm,),
        in_specs=[_rows(tm, D), pl.BlockSpec((tm, D), lambda i: (jnp.maximum(i - front_blocks, 0), 0)),
                  _const((1, D))],
        out_specs=[_rows(tm, D), _rows(tm, D), _const((1, D)), _const((1, D))],
        out_shape=[jax.ShapeDtypeStruct((T, D), F32), jax.ShapeDtypeStruct((T, D), BF16),
                   jax.ShapeDtypeStruct((1, D), F32), jax.ShapeDtypeStruct((1, D), F32)],
        compiler_params=_cparams(("arbitrary",)),
    )(h2, tgt, g)


def _scan_fwd(a, b, h_in):
    n = a.shape[0]
    row = lax.broadcasted_iota(jnp.int32, a.shape, 0)
    s = 1
    while s < n:
        a_sh = jnp.where(row >= s, pltpu.roll(a, s, 0), 1.0)
        b_sh = jnp.where(row >= s, pltpu.roll(b, s, 0), 0.0)
        b = a * b_sh + b
        a = a * a_sh
        s *= 2
    return b + a * h_in


def _scan_rev(a, b, g_in):
    n = a.shape[0]
    row = lax.broadcasted_iota(jnp.int32, a.shape, 0)
    s = 1
    while s < n:
        a_sh = jnp.where(row < n - s, pltpu.roll(a, n - s, 0), 1.0)
        b_sh = jnp.where(row < n - s, pltpu.roll(b, n - s, 0), 0.0)
        b = a * b_sh + b
        a = a * a_sh
        s *= 2
    return b + a * g_in


def _lru_gates(xc, wa, ba, wi, bi, lam):
    xcb = xc.astype(BF16)
    r = _sigmoid(jnp.dot(xcb, wa, preferred_element_type=F32) + ba)
    ig = _sigmoid(jnp.dot(xcb, wi, preferred_element_type=F32) + bi)
    log_sig = jnp.minimum(lam, 0.0) - jnp.log(1.0 + jnp.exp(-jnp.abs(lam)))
    log_a = LRU_C * r * log_sig
    a = jnp.exp(log_a)
    z = 2.0 * log_a
    poly = -z * (1.0 + z * (0.5 + z * (1.0 / 6.0 + z * (1.0 / 24.0))))
    m2 = jnp.where(z > -0.03, poly, 1.0 - jnp.exp(z))
    return r, ig, log_sig, a, jnp.sqrt(m2)


def _rnn_specs(tc, nblk_t, rev):
    def tmap(k):
        return (nblk_t - 1 - k) if rev else k

    hb = tc // 8
    blk = lambda off: pl.BlockSpec((tc, RB), lambda c, k: (tmap(k), c + off))
    halo = lambda off: pl.BlockSpec((8, RB), lambda c, k: (jnp.maximum(tmap(k) * hb - 1, 0), c + off))
    vec = pl.BlockSpec((1, RB), lambda c, k: (0, c))
    cwv = pl.BlockSpec((CW, RB), lambda c, k: (0, c))
    mat = pl.BlockSpec((None, RB, RB), lambda c, k: (c, 0, 0))
    return blk, halo, vec, cwv, mat


def _rnn_fwd(uxg, cw, cb, wa, ba, wi, bi, lam, *, tc=640):
    T = uxg.shape[0]
    nt = T // tc
    nsub = tc // SUB
    blk, halo, vec, cwv, mat = _rnn_specs(tc, nt, False)

    def body(x_ref, xh_ref, ug_ref, cw_ref, cb_ref, wa_ref, ba_ref, wi_ref, bi_ref, lam_ref,
             h_ref, y_ref, xb, hc):
        k = pl.program_id(1)

        @pl.when(k == 0)
        def _():
            hc[...] = jnp.zeros_like(hc)

        xb[0:8, :] = jnp.where(k > 0, xh_ref[...], 0.0)
        xb[8:, :] = x_ref[...]
        cwv_, cbv = cw_ref[...], cb_ref[...]
        wav, wiv = wa_ref[...], wi_ref[...]
        bav, biv, lamv = ba_ref[...], bi_ref[...], lam_ref[...]
        h_in = hc[0:1, :]
        for sc in range(nsub):
            r0 = sc * SUB
            xc = cbv + cwv_[0:1, :] * xb[pl.ds(5 + r0, SUB), :]
            for j in range(1, CW):
                xc = xc + cwv_[j:j + 1, :] * xb[pl.ds(5 + j + r0, SUB), :]
            r, ig, _, a, mm = _lru_gates(xc, wav, bav, wiv, biv, lamv)
            rows = k * tc + r0 + lax.broadcasted_iota(jnp.int32, (SUB, RB), 0)
            b = jnp.where(rows >= PAD, mm * (ig * xc), 0.0)
            h = _scan_fwd(a, b, h_in)
            h_in = h[SUB - 1:SUB, :]
            h_ref[pl.ds(r0, SUB), :] = h
            gl, _ = _gelu_parts(ug_ref[pl.ds(r0, SUB), :])
            y_ref[pl.ds(r0, SUB), :] = (h * gl).astype(BF16)
        hc[0:1, :] = h_in

    return pl.pallas_call(
        body, name="rnn_fwd", grid=(NBLK, nt),
        in_specs=[blk(0), halo(0), blk(NBLK), cwv, vec, mat, vec, mat, vec, vec],
        out_specs=[blk(0), blk(0)],
        out_shape=[jax.ShapeDtypeStruct((T, DR), F32), jax.ShapeDtypeStruct((T, DR), BF16)],
        scratch_shapes=[pltpu.VMEM((tc + 8, RB), F32), pltpu.VMEM((8, RB), F32)],
        compiler_params=_cparams(("parallel", "arbitrary")),
    )(uxg, uxg, uxg, cw, cb, wa, ba, wi, bi, lam)


def _rnn_bwd(uxg, hs, dy, cw, cb, wa, ba, wi, bi, lam, wat, wit, *, tc=640):
    T = uxg.shape[0]
    nt = T // tc
    nsub = tc // SUB
    blk, halo, vec, cwv, mat = _rnn_specs(tc, nt, True)

    def body(x_ref, xh_ref, ug_ref, h_ref, hh_ref, dy_ref, cw_ref, cb_ref, wa_ref, ba_ref, wi_ref,
             bi_ref, lam_ref, wat_ref, wit_ref,
             dux_ref, dug_ref, dcw_ref, dcb_ref, dwa_ref, dba_ref, dwi_ref, dbi_ref, dlam_ref,
             xb, hb, ab, dxb, xcs, rs, igs, mms, dgas, dgis, carry):
        k = pl.program_id(1)
        kt = nt - 1 - k

        @pl.when(k == 0)
        def _():
            carry[...] = jnp.zeros_like(carry)
            for ref in (dcw_ref, dcb_ref, dwa_ref, dba_ref, dwi_ref, dbi_ref, dlam_ref):
                ref[...] = jnp.zeros_like(ref)

        xb[0:8, :] = jnp.where(kt > 0, xh_ref[...], 0.0)
        xb[8:, :] = x_ref[...]
        hb[0:8, :] = jnp.where(kt > 0, hh_ref[...], 0.0)
        hb[8:, :] = h_ref[...]
        cwv_, cbv = cw_ref[...], cb_ref[...]
        wav, wiv = wa_ref[...], wi_ref[...]
        bav, biv, lamv = ba_ref[...], bi_ref[...], lam_ref[...]
        ab[tc:tc + 8, :] = jnp.broadcast_to(carry[1:2, :], (8, RB))
        dxb[tc:tc + 8, :] = carry[8:16, :]
        log_sig = None
        for sc in range(nsub):
            r0 = sc * SUB
            xc = cbv + cwv_[0:1, :] * xb[pl.ds(5 + r0, SUB), :]
            for j in range(1, CW):
                xc = xc + cwv_[j:j + 1, :] * xb[pl.ds(5 + j + r0, SUB), :]
            r, ig, log_sig, a, mm = _lru_gates(xc, wav, bav, wiv, biv, lamv)
            xcs[pl.ds(r0, SUB), :] = xc
            rs[pl.ds(r0, SUB), :] = r
            igs[pl.ds(r0, SUB), :] = ig
            mms[pl.ds(r0, SUB), :] = mm
            ab[pl.ds(r0, SUB), :] = a
        sig_neg = _sigmoid(-lamv)
        g_in = carry[0:1, :]
        dlam_acc = jnp.zeros((1, RB), F32)
        for sc in reversed(range(nsub)):
            r0 = sc * SUB
            xc, r, ig, mm = xcs[pl.ds(r0, SUB), :], rs[pl.ds(r0, SUB), :], igs[pl.ds(r0, SUB), :], mms[pl.ds(r0, SUB), :]
            a = ab[pl.ds(r0, SUB), :]
            a_next = ab[pl.ds(r0 + 1, SUB), :]
            hv = hb[pl.ds(8 + r0, SUB), :]
            hprev = hb[pl.ds(7 + r0, SUB), :]
            dyv = dy_ref[pl.ds(r0, SUB), :]
            gl, dgl = _gelu_parts(ug_ref[pl.ds(r0, SUB), :])
            dug_ref[pl.ds(r0, SUB), :] = (dyv * hv * dgl).astype(BF16)
            G = _scan_rev(a_next, dyv * gl, g_in)
            g_in = G[0:1, :]
            rows = kt * tc + r0 + lax.broadcasted_iota(jnp.int32, (SUB, RB), 0)
            db = jnp.where(rows >= PAD, G, 0.0)
            da = G * hprev
            dmm = db * (ig * xc)
            di = db * (mm * xc)
            dxc = db * (mm * ig)
            dlog_a = da * a - dmm * (a * a) / jnp.maximum(mm, 1e-30)
            dr = dlog_a * (LRU_C * log_sig)
            dlam_acc = dlam_acc + jnp.sum(dlog_a * (LRU_C * r), axis=0, keepdims=True)
            dga = dr * r * (1.0 - r)
            dgi = di * ig * (1.0 - ig)
            dgab, dgib = dga.astype(BF16), dgi.astype(BF16)
            dgas[pl.ds(r0, SUB), :] = dgab
            dgis[pl.ds(r0, SUB), :] = dgib
            dba_ref[...] += jnp.sum(dga, axis=0, keepdims=True)
            dbi_ref[...] += jnp.sum(dgi, axis=0, keepdims=True)
            dxc = dxc + jnp.dot(dgab, wat_ref[...], preferred_element_type=F32) \
                + jnp.dot(dgib, wit_ref[...], preferred_element_type=F32)
            dxb[pl.ds(r0, SUB), :] = dxc
        dlam_ref[...] += dlam_acc * sig_neg
        xcb = xcs[...].astype(BF16)
        tn = (((0,), (0,)), ((), ()))
        dwa_ref[...] += lax.dot_general(xcb, dgas[...], tn, preferred_element_type=F32)
        dwi_ref[...] += lax.dot_general(xcb, dgis[...], tn, preferred_element_type=F32)
        dxc_all = dxb[0:tc, :]
        dcb_ref[...] += jnp.sum(dxc_all, axis=0, keepdims=True)
        rows_all = kt * tc + lax.broadcasted_iota(jnp.int32, (tc, RB), 0)
        dux = jnp.zeros((tc, RB), F32)
        for j in range(CW):
            dcw_ref[j:j + 1, :] += jnp.sum(dxc_all * xb[pl.ds(5 + j, tc), :], axis=0, keepdims=True)
            dux = dux + cwv_[j:j + 1, :] * dxb[pl.ds(CW - 1 - j, tc), :]
        dux_ref[...] = jnp.where(rows_all >= PAD, dux, 0.0).astype(BF16)
        carry[0:1, :] = g_in
        carry[1:2, :] = ab[0:1, :]
        carry[8:16, :] = dxb[0:8, :]

    vec_out = pl.BlockSpec((1, RB), lambda c, k: (0, c))
    return pl.pallas_call(
        body, name="rnn_bwd", grid=(NBLK, nt),
        in_specs=[blk(0), halo(0), blk(NBLK), blk(0), halo(0), blk(0), cwv, vec, mat, vec, mat, vec, vec, mat, mat],
        out_specs=[blk(0), blk(0), cwv, vec_out, mat, vec_out, mat, vec_out, vec_out],
        out_shape=[jax.ShapeDtypeStruct((T, DR), BF16), jax.ShapeDtypeStruct((T, DR), BF16),
                   jax.ShapeDtypeStruct((CW, DR), F32), jax.ShapeDtypeStruct((1, DR), F32),
                   jax.ShapeDtypeStruct((NBLK, RB, RB), F32), jax.ShapeDtypeStruct((1, DR), F32),
                   jax.ShapeDtypeStruct((NBLK, RB, RB), F32), jax.ShapeDtypeStruct((1, DR), F32),
                   jax.ShapeDtypeStruct((1, DR), F32)],
        scratch_shapes=[pltpu.VMEM((tc + 8, RB), F32), pltpu.VMEM((tc + 8, RB), F32),
                        pltpu.VMEM((tc + 8, RB), F32), pltpu.VMEM((tc + 8, RB), F32),
                        pltpu.VMEM((tc, RB), F32), pltpu.VMEM((tc, RB), F32), pltpu.VMEM((tc, RB), F32),
                        pltpu.VMEM((tc, RB), F32), pltpu.VMEM((tc, RB), BF16), pltpu.VMEM((tc, RB), BF16),
                        pltpu.VMEM((16, RB), F32)],
        compiler_params=_cparams(("parallel", "arbitrary")),
    )(uxg, uxg, uxg, hs, hs, dy, cw, cb, wa, ba, wi, bi, lam, wat, wit)


def _attn_prep(q_all, kv_all, ukr, tab, *, tm=320):
    T = q_all.shape[0]

    def body(q_ref, kv_ref, kr_ref, tab_ref, qo_ref, ko_ref, vo_ref):
        tab_v = tab_ref[...]
        lane = lax.broadcasted_iota(jnp.int32, (tm, LANES), 1)
        t1 = kr_ref[...] * tab_v
        kro = jnp.where(lane < ROPE, t1 + pltpu.roll(t1, ROPE, 1), 0.0).astype(BF16)
        for h in range(NH):
            c0 = h * QW
            qo_ref[h, :, 0:NOPE] = (q_ref[:, c0:c0 + NOPE] * SCALE).astype(BF16)
            t2 = q_ref[:, c0 + NOPE:c0 + QW] * tab_v
            qo_ref[h, :, NOPE:QW] = ((t2 + pltpu.roll(t2, ROPE, 1)) * SCALE).astype(BF16)
            ko_ref[h, :, 0:NOPE] = kv_ref[:, c0:c0 + NOPE].astype(BF16)
            ko_ref[h, :, NOPE:QW] = kro
            vo_ref[h, :, :] = kv_ref[:, c0 + NOPE:c0 + QW].astype(BF16)

    return pl.pallas_call(
        body, name="attn_prep", grid=(T // tm,),
        in_specs=[_rows(tm, NH * QW), _rows(tm, NH * QW), _rows(tm, LANES), _rows(tm, LANES)],
        out_specs=[pl.BlockSpec((NH, tm, QW), lambda i: (0, i, 0)), pl.BlockSpec((NH, tm, QW), lambda i: (0, i, 0)),
                   pl.BlockSpec((NH, tm, VD), lambda i: (0, i, 0))],
        out_shape=[jax.ShapeDtypeStruct((NH, T, QW), BF16), jax.ShapeDtypeStruct((NH, T, QW), BF16),
                   jax.ShapeDtypeStruct((NH, T, VD), BF16)],
        compiler_params=_cparams(("parallel",)),
    )(q_all, kv_all, ukr, tab)


def _attn_prep_bwd(dq, dk, dv, tab, *, tm=320):
    T = dq.shape[1]

    def body(dq_ref, dk_ref, dv_ref, tab_ref, dqa_ref, dkva_ref, dkr_ref):
        tab_v = tab_ref[...]
        lane = lax.broadcasted_iota(jnp.int32, (tm, LANES), 1)
        dkro = jnp.zeros((tm, LANES), F32)
        for h in range(NH):
            c0 = h * QW
            dqa_ref[:, c0:c0 + NOPE] = (dq_ref[h, :, 0:NOPE] * SCALE).astype(BF16)
            d2 = dq_ref[h, :, NOPE:QW]
            dqa_ref[:, c0 + NOPE:c0 + QW] = ((d2 + pltpu.roll(d2, ROPE, 1)) * tab_v * SCALE).astype(BF16)
            dkva_ref[:, c0:c0 + NOPE] = dk_ref[h, :, 0:NOPE].astype(BF16)
            dkva_ref[:, c0 + NOPE:c0 + QW] = dv_ref[h, :, :].astype(BF16)
            dkro = dkro + dk_ref[h, :, NOPE:QW]
        dkro = jnp.where(lane < ROPE, dkro, 0.0)
        dkr_ref[...] = ((dkro + pltpu.roll(dkro, ROPE, 1)) * tab_v).astype(BF16)

    return pl.pallas_call(
        body, name="attn_prep_bwd", grid=(T // tm,),
        in_specs=[pl.BlockSpec((NH, tm, QW), lambda i: (0, i, 0)), pl.BlockSpec((NH, tm, QW), lambda i: (0, i, 0)),
                  pl.BlockSpec((NH, tm, VD), lambda i: (0, i, 0)), _rows(tm, LANES)],
        out_specs=[_rows(tm, NH * QW), _rows(tm, NH * QW), _rows(tm, LANES)],
        out_shape=[jax.ShapeDtypeStruct((T, NH * QW), BF16), jax.ShapeDtypeStruct((T, NH * QW), BF16),
                   jax.ShapeDtypeStruct((T, LANES), BF16)],
        compiler_params=_cparams(("parallel",)),
    )(dq, dk, dv, tab)


def _visible(q0, k0, nq, nk):
    rows = q0 + lax.broadcasted_iota(jnp.int32, (nq, nk), 0)
    cols = k0 + lax.broadcasted_iota(jnp.int32, (nq, nk), 1)
    return ((cols >> 6) <= (rows >> 6)) & (cols >= PAD)


def _visible_t(q0, k0, nq, nk):
    cols = k0 + lax.broadcasted_iota(jnp.int32, (nk, nq), 0)
    rows = q0 + lax.broadcasted_iota(jnp.int32, (nk, nq), 1)
    return ((cols >> 6) <= (rows >> 6)) & (cols >= PAD)


_NT = (((1,), (1,)), ((), ()))


def _flash_fwd(q, k, v, *, bq=640):
    T = q.shape[1]
    nq = T // bq

    def body(q_ref, k_ref, v_ref, o_ref, lse_ref, m_s, l_s, acc_s):
        i = pl.program_id(1)
        qv = q_ref[...]
        m_s[...] = jnp.full_like(m_s, NEG)
        l_s[...] = jnp.zeros_like(l_s)
        acc_s[...] = jnp.zeros_like(acc_s)

        def step(j, masked):
            off = pl.multiple_of(j * bq, bq)
            kv_ = k_ref[pl.ds(off, bq), :]
            vv = v_ref[pl.ds(off, bq), :]
            s = lax.dot_general(qv, kv_, _NT, preferred_element_type=F32)
            if masked:
                s = jnp.where(_visible(i * bq, j * bq, bq, bq), s, NEG)
            m_prev = m_s[...]
            m_new = jnp.maximum(m_prev, jnp.max(s, axis=-1, keepdims=True))
            p = jnp.exp(s - m_new)
            alpha = jnp.exp(m_prev - m_new)
            l_s[...] = alpha * l_s[...] + jnp.sum(p, axis=-1, keepdims=True)
            acc_s[...] = alpha * acc_s[...] + jnp.dot(p.astype(BF16), vv, preferred_element_type=F32)
            m_s[...] = m_new

        step(0, True)

        def loop(j, c):
            step(j, False)
            return c

        lax.fori_loop(1, i, loop, 0)

        @pl.when(i > 0)
        def _():
            step(i, True)

        o_ref[...] = (acc_s[...] / l_s[...]).astype(BF16)
        lse_ref[...] = m_s[...] + jnp.log(l_s[...])

    return pl.pallas_call(
        body, name="flash_fwd", grid=(NH, nq),
        in_specs=[pl.BlockSpec((None, bq, QW), lambda h, i: (h, i, 0)),
                  pl.BlockSpec((None, T, QW), lambda h, i: (h, 0, 0)),
                  pl.BlockSpec((None, T, VD), lambda h, i: (h, 0, 0))],
        out_specs=[pl.BlockSpec((bq, VD), lambda h, i: (i, h)),
                   pl.BlockSpec((None, bq, 1), lambda h, i: (h, i, 0))],
        out_shape=[jax.ShapeDtypeStruct((T, NH * VD), BF16), jax.ShapeDtypeStruct((NH, T, 1), F32)],
        scratch_shapes=[pltpu.VMEM((bq, 1), F32), pltpu.VMEM((bq, 1), F32), pltpu.VMEM((bq, VD), F32)],
        compiler_params=_cparams(("parallel", "parallel")),
    )(q, k, v)


def _attn_delta(o, do, *, tm=640):
    T = o.shape[0]

    def body(o_ref, do_ref, d_ref):
        prod = o_ref[...].astype(F32) * do_ref[...].astype(F32)
        for h in range(NH):
            d_ref[h, :, :] = jnp.sum(prod[:, h * VD:(h + 1) * VD], axis=-1, keepdims=True)

    return pl.pallas_call(
        body, name="attn_delta", grid=(T // tm,),
        in_specs=[_rows(tm, NH * VD), _rows(tm, NH * VD)],
        out_specs=pl.BlockSpec((NH, tm, 1), lambda i: (0, i, 0)),
        out_shape=jax.ShapeDtypeStruct((NH, T, 1), F32),
        compiler_params=_cparams(("parallel",)),
    )(o, do)


def _flash_bwd_dq(q, k, v, do, lse, delta, *, bq=640):
    T = q.shape[1]
    nq = T // bq

    def body(q_ref, k_ref, v_ref, do_ref, lse_ref, dl_ref, dq_ref):
        i = pl.program_id(1)
        qv = q_ref[...]
        dov = do_ref[...]
        lse_v = lse_ref[...]
        dl_v = dl_ref[...]
        dq_ref[...] = jnp.zeros_like(dq_ref)

        def step(j, masked):
            off = pl.multiple_of(j * bq, bq)
            kv_ = k_ref[pl.ds(off, bq), :]
            vv = v_ref[pl.ds(off, bq), :]
            s = lax.dot_general(qv, kv_, _NT, preferred_element_type=F32)
            if masked:
                s = jnp.where(_visible(i * bq, j * bq, bq, bq), s, NEG)
            p = jnp.exp(s - lse_v)
            dp = lax.dot_general(dov, vv, _NT, preferred_element_type=F32)
            ds = (p * (dp - dl_v)).astype(BF16)
            dq_ref[...] += jnp.dot(ds, kv_, preferred_element_type=F32)

        step(0, True)

        def loop(j, c):
            step(j, False)
            return c

        lax.fori_loop(1, i, loop, 0)

        @pl.when(i > 0)
        def _():
            step(i, True)

    return pl.pallas_call(
        body, name="flash_bwd_dq", grid=(NH, nq),
        in_specs=[pl.BlockSpec((None, bq, QW), lambda h, i: (h, i, 0)),
                  pl.BlockSpec((None, T, QW), lambda h, i: (h, 0, 0)),
                  pl.BlockSpec((None, T, VD), lambda h, i: (h, 0, 0)),
                  pl.BlockSpec((bq, VD), lambda h, i: (i, h)),
                  pl.BlockSpec((None, bq, 1), lambda h, i: (h, i, 0)),
                  pl.BlockSpec((None, bq, 1), lambda h, i: (h, i, 0))],
        out_specs=pl.BlockSpec((None, bq, QW), lambda h, i: (h, i, 0)),
        out_shape=jax.ShapeDtypeStruct((NH, T, QW), F32),
        compiler_params=_cparams(("parallel", "parallel")),
    )(q, k, v, do, lse, delta)


def _flash_bwd_dkv(q, k, v, do, lse_row, delta_row, *, bq=640):
    T = q.shape[1]
    nq = T // bq

    def body(q_ref, k_ref, v_ref, do_ref, lse_ref, dl_ref, dk_ref, dv_ref):
        j = pl.program_id(1)
        kv_ = k_ref[...]
        vv = v_ref[...]
        dk_ref[...] = jnp.zeros_like(dk_ref)
        dv_ref[...] = jnp.zeros_like(dv_ref)

        def step(i, masked):
            off = pl.multiple_of(i * bq, bq)
            qv = q_ref[pl.ds(off, bq), :]
            dov = do_ref[pl.ds(off, bq), :]
            lse_v = lse_ref[:, pl.ds(off, bq)]
            dl_v = dl_ref[:, pl.ds(off, bq)]
            st = lax.dot_general(kv_, qv, _NT, preferred_element_type=F32)
            if masked:
                st = jnp.where(_visible_t(i * bq, j * bq, bq, bq), st, NEG)
            pt = jnp.exp(st - lse_v)
            dv_ref[...] += jnp.dot(pt.astype(BF16), dov, preferred_element_type=F32)
            dpt = lax.dot_general(vv, dov, _NT, preferred_element_type=F32)
            dst = (pt * (dpt - dl_v)).astype(BF16)
            dk_ref[...] += jnp.dot(dst, qv, preferred_element_type=F32)

        step(j, True)

        @pl.when(j == 0)
        def _():
            def loop(i, c):
                step(i, True)
                return c
            lax.fori_loop(1, nq, loop, 0)

        @pl.when(j > 0)
        def _():
            def loop(i, c):
                step(i, False)
                return c
            lax.fori_loop(j + 1, nq, loop, 0)

    return pl.pallas_call(
        body, name="flash_bwd_dkv", grid=(NH, nq),
        in_specs=[pl.BlockSpec((None, T, QW), lambda h, j: (h, 0, 0)),
                  pl.BlockSpec((None, bq, QW), lambda h, j: (h, j, 0)),
                  pl.BlockSpec((None, bq, VD), lambda h, j: (h, j, 0)),
                  pl.BlockSpec((T, VD), lambda h, j: (0, h)),
                  pl.BlockSpec((None, 1, T), lambda h, j: (h, 0, 0)),
                  pl.BlockSpec((None, 1, T), lambda h, j: (h, 0, 0))],
        out_specs=[pl.BlockSpec((None, bq, QW), lambda h, j: (h, j, 0)),
                   pl.BlockSpec((None, bq, VD), lambda h, j: (h, j, 0))],
        out_shape=[jax.ShapeDtypeStruct((NH, T, QW), F32), jax.ShapeDtypeStruct((NH, T, VD), F32)],
        compiler_params=_cparams(("parallel", "parallel")),
    )(q, k, v, do, lse_row, delta_row)


def _rope_table(T):
    pos = (jnp.arange(T, dtype=jnp.int32) - PAD).astype(F32)
    inv_freq = ROPE_THETA ** (-jnp.arange(0, ROPE, 2, dtype=F32) / ROPE)
    ang = pos[:, None] * inv_freq[None, :]
    cos, sin = jnp.cos(ang), jnp.sin(ang)
    return jnp.concatenate([cos, cos, -sin, sin], axis=1)


def _swap_halves(w):
    return jnp.concatenate([w[..., ROPE // 2:], w[..., :ROPE // 2]], axis=-1)


O_UX, O_UG, O_UQ, O_UKV, O_UKR, O_UM = 0, DR, 2 * DR, 2 * DR + QR, 2 * DR + QR + KVR, 2 * DR + QR + KVR + ROPE


def _prep_weights(w):
    b = lambda a: a.astype(BF16)
    w_in = w["w_in"]
    kr = w_in[:, O_UKR:O_UM]
    p = {
        "w_xg": b(w_in[:, :O_UQ]),
        "w_q": b(w_in[:, O_UQ:O_UKV]),
        "w_kv": b(w_in[:, O_UKV:O_UKR]),
        "w_kr": b(jnp.concatenate([kr, _swap_halves(kr)], axis=1)),
        "w_m": b(w_in[:, O_UM:]),
    }
    wq = w["w_uq"].reshape(QR, NH, NOPE + ROPE)
    p["w_uq"] = b(jnp.concatenate([wq, _swap_halves(wq[..., NOPE:])], axis=-1).reshape(QR, NH * QW))
    p["w_ukv"] = b(w["w_ukv"])
    p["w_br"] = b(w["w_branch"][:DR])
    p["w_ba"] = b(w["w_branch"][DR:])
    p["w_out"] = b(w["w_out"])
    p["w_fi"] = b(w["w_ffn_in"])
    p["w_fo"] = b(w["w_ffn_out"])
    for n in ("w_xg", "w_q", "w_kv", "w_kr", "w_m", "w_uq", "w_ukv", "w_br", "w_ba", "w_out", "w_fi", "w_fo"):
        p[n + "_t"] = p[n].T
    p["wa"] = b(w["w_rec_a"])
    p["wi"] = b(w["w_rec_i"])
    p["wa_t"] = jnp.swapaxes(p["wa"], 1, 2)
    p["wi_t"] = jnp.swapaxes(p["wi"], 1, 2)
    return p


def _local_step(x, tgt, w):
    S = x.shape[0]
    T = FRONT + S
    p = _prep_weights(w)
    tab = _rope_table(T)
    h0 = jnp.concatenate([jnp.zeros((PAD, D), F32), w["meta_tokens"], x], axis=0)
    row = lambda v: v.reshape(1, -1)

    z = _rmsnorm_fwd(h0, row(w["norm_mix_g"]), name="norm_mix")
    uxg = _mm(z, p["w_xg"], name="mm_uxg", tn=640)
    uq = _mm(z, p["w_q"], name="mm_uq")
    ukv = _mm(z, p["w_kv"], name="mm_ukv")
    ukr = _mm(z, p["w_kr"], name="mm_ukr")
    um = _mm(z, p["w_m"], name="mm_um")
    rnn_w = (w["conv_w"], row(w["conv_b"]), p["wa"], row(w["b_rec_a"]), p["wi"], row(w["b_rec_i"]),
             row(w["lru_lambda"]))
    hs, y_rnn = _rnn_fwd(uxg, *rnn_w)
    qn = _rmsnorm_fwd(uq, row(w["q_norm_g"]), name="norm_q")
    kvn = _rmsnorm_fwd(ukv, row(w["kv_norm_g"]), name="norm_kv")
    q_all = _mm(qn, p["w_uq"], name="mm_q")
    kv_all = _mm(kvn, p["w_ukv"], name="mm_kv")
    qh, kh, vh = _attn_prep(q_all, kv_all, ukr, tab)
    y_att, lse = _flash_fwd(qh, kh, vh)
    p_rnn = _mm(y_rnn, p["w_br"], name="mm_prnn")
    p_att = _mm(y_att, p["w_ba"], name="mm_patt")
    bg = row(w["b_gate"])
    mixed = _gate_mix_fwd(um, bg, p_rnn, p_att)
    h1 = _mm(mixed, p["w_out"], name="mm_out", res=h0)
    zf = _rmsnorm_fwd(h1, row(w["norm_ffn_g"]), name="norm_ffn")
    ff = _mm(zf, p["w_fi"], name="mm_ffn_in")
    act = _swiglu_fwd(ff)
    h2 = _mm(act, p["w_fo"], name="mm_ffn_out", res=h1)

    g = {}
    dh2, dh2b, dg_fin, lsum = _loss_head(h2, tgt, row(w["final_norm_g"]))
    loss = 0.5 * jnp.sum(lsum) / D
    g["final_norm_g"] = dg_fin.reshape(-1)
    dact = _mm(dh2b, p["w_fo_t"], name="mm_dact", tn=1408)
    g["w_ffn_out"] = _mm_tn(act, dh2b, name="mm_dw_ffn_out", tk=1408)
    dff = _swiglu_bwd(ff, dact)
    dzf = _mm(dff, p["w_fi_t"], name="mm_dzf")
    g["w_ffn_in"] = _mm_tn(zf, dff, name="mm_dw_ffn_in", tn=1408)
    dh1, dh1b, dg = _rmsnorm_bwd(h1, row(w["norm_ffn_g"]), dzf, dh2, name="norm_ffn_bwd")
    g["norm_ffn_g"] = dg
    dmixed = _mm(dh1b, p["w_out_t"], name="mm_dmixed")
    g["w_out"] = _mm_tn(mixed, dh1b, name="mm_dw_out")
    dp_rnn, dp_att, dum, dbg = _gate_mix_bwd(um, bg, p_rnn, p_att, dmixed)
    g["b_gate"] = dbg.reshape(2, D)
    dy_rnn = _mm(dp_rnn, p["w_br_t"], name="mm_dy_rnn", tn=640)
    dy_att = _mm(dp_att, p["w_ba_t"], name="mm_dy_att", out_dtype=BF16)
    g["w_branch"] = jnp.concatenate([_mm_tn(y_rnn, dp_rnn, name="mm_dw_br", tk=640),
                                     _mm_tn(y_att, dp_att, name="mm_dw_ba")], axis=0)
    delta = _attn_delta(y_att, dy_att)
    dq = _flash_bwd_dq(qh, kh, vh, dy_att, lse, delta)
    dk, dv = _flash_bwd_dkv(qh, kh, vh, dy_att, lse.reshape(NH, 1, T), delta.reshape(NH, 1, T))
    dq_all, dkv_all, dukr = _attn_prep_bwd(dq, dk, dv, tab)
    dqn = _mm(dq_all, p["w_uq_t"], name="mm_dqn")
    dkvn = _mm(dkv_all, p["w_ukv_t"], name="mm_dkvn")
    dwq = _mm_tn(qn, dq_all, name="mm_dw_uq", tk=QR).reshape(QR, NH, QW)
    dwq_rope = dwq[..., NOPE:NOPE + ROPE] + _swap_halves(dwq[..., NOPE + ROPE:])
    g["w_uq"] = jnp.concatenate([dwq[..., :NOPE], dwq_rope], axis=-1).reshape(QR, NH * (NOPE + ROPE))
    g["w_ukv"] = _mm_tn(kvn, dkv_all, name="mm_dw_ukv", tk=KVR)
    duq, dg = _rmsnorm_bwd(uq, row(w["q_norm_g"]), dqn, None, name="norm_q_bwd", want_f32=False)
    g["q_norm_g"] = dg
    dukv, dg = _rmsnorm_bwd(ukv, row(w["kv_norm_g"]), dkvn, None, name="norm_kv_bwd", want_f32=False)
    g["kv_norm_g"] = dg
    (dux, dug, g["conv_w"], g["conv_b"], g["w_rec_a"], g["b_rec_a"], g["w_rec_i"], g["b_rec_i"],
     g["lru_lambda"]) = _rnn_bwd(uxg, hs, dy_rnn, *rnn_w, p["wa_t"], p["wi_t"])
    dz = _mm(dux, p["w_xg_t"][:DR], name="mm_dz_x")
    dz = _mm(dug, p["w_xg_t"][DR:], name="mm_dz_g", res=dz)
    dz = _mm(duq, p["w_q_t"], name="mm_dz_q", res=dz)
    dz = _mm(dukv, p["w_kv_t"], name="mm_dz_kv", res=dz)
    dz = _mm(dukr, p["w_kr_t"], name="mm_dz_kr", res=dz)
    dz = _mm(dum, p["w_m_t"], name="mm_dz_m", res=dz)
    dwkr = _mm_tn(z, dukr, name="mm_dw_kr")
    g["w_in"] = jnp.concatenate([
        _mm_tn(z, dux, name="mm_dw_x", tn=640), _mm_tn(z, dug, name="mm_dw_g", tn=640),
        _mm_tn(z, duq, name="mm_dw_q"), _mm_tn(z, dukv, name="mm_dw_kv"),
        dwkr[:, :ROPE] + _swap_halves(dwkr[:, ROPE:]),
        _mm_tn(z, dum, name="mm_dw_m")], axis=1)
    dh0, dg = _rmsnorm_bwd(h0, row(w["norm_mix_g"]), dz, dh1, name="norm_mix_bwd", want_bf16=False)
    g["norm_mix_g"] = dg
    g["meta_tokens"] = dh0[PAD:FRONT]
    return loss, dh0[FRONT:], g


HBM = pl.BlockSpec(memory_space=pltpu.HBM)
CHIP_FLIPS = ((1, 0), (0, 1), (1, 1))


def _place():
    return lax.axis_index("x"), lax.axis_index("y"), lax.axis_index("c")


def _flip(v, f):
    return 1 - v if f else v


def _allgather_chips(src, *, name):
    def body(src_ref, out_ref, send_sems, recv_sems, local_sem):
        x, y, c = _place()
        me = 2 * x + y
        mine = pltpu.make_async_copy(src_ref, out_ref.at[me], local_sem)
        mine.start()
        copies = []
        for k, (fx, fy) in enumerate(CHIP_FLIPS):
            cp = pltpu.make_async_remote_copy(
                src_ref=src_ref, dst_ref=out_ref.at[me], send_sem=send_sems.at[k], recv_sem=recv_sems.at[k],
                device_id=(_flip(x, fx), _flip(y, fy), c), device_id_type=MESH)
            cp.start()
            copies.append(cp)
        for cp in copies:
            cp.wait()
        mine.wait()

    return pl.pallas_call(
        body, name=name, in_specs=[HBM], out_specs=HBM,
        out_shape=jax.ShapeDtypeStruct((4,) + src.shape, src.dtype),
        scratch_shapes=[pltpu.SemaphoreType.DMA((3,)), pltpu.SemaphoreType.DMA((3,)), pltpu.SemaphoreType.DMA],
    )(src)


def _scatter_chips(src, *, name):
    def body(src_ref, out_ref, send_sems, recv_sems, local_sem):
        x, y, c = _place()
        me = 2 * x + y
        mine = pltpu.make_async_copy(src_ref.at[me], out_ref.at[me], local_sem)
        mine.start()
        copies = []
        for k, (fx, fy) in enumerate(CHIP_FLIPS):
            px, py = _flip(x, fx), _flip(y, fy)
            cp = pltpu.make_async_remote_copy(
                src_ref=src_ref.at[2 * px + py], dst_ref=out_ref.at[me], send_sem=send_sems.at[k],
                recv_sem=recv_sems.at[k], device_id=(px, py, c), device_id_type=MESH)
            cp.start()
            copies.append(cp)
        for cp in copies:
            cp.wait()
        mine.wait()

    return pl.pallas_call(
        body, name=name, in_specs=[HBM], out_specs=HBM,
        out_shape=jax.ShapeDtypeStruct(src.shape, src.dtype),
        scratch_shapes=[pltpu.SemaphoreType.DMA((3,)), pltpu.SemaphoreType.DMA((3,)), pltpu.SemaphoreType.DMA],
    )(src)


def _sibling_take(src, *, name):
    def body(src_ref, out_ref, send_sem, recv_sem):
        x, y, c = _place()
        cp = pltpu.make_async_remote_copy(
            src_ref=src_ref.at[1 - c], dst_ref=out_ref, send_sem=send_sem, recv_sem=recv_sem,
            device_id=(x, y, 1 - c), device_id_type=MESH)
        cp.start()
        cp.wait()

    return pl.pallas_call(
        body, name=name, in_specs=[HBM], out_specs=HBM,
        out_shape=jax.ShapeDtypeStruct(src.shape[1:], src.dtype),
        scratch_shapes=[pltpu.SemaphoreType.DMA, pltpu.SemaphoreType.DMA],
    )(src)


def _sibling_pair(src, *, name):
    def body(src_ref, out_ref, send_sem, recv_sem, local_sem):
        x, y, c = _place()
        mine = pltpu.make_async_copy(src_ref, out_ref.at[c], local_sem)
        mine.start()
        cp = pltpu.make_async_remote_copy(
            src_ref=src_ref, dst_ref=out_ref.at[c], send_sem=send_sem, recv_sem=recv_sem,
            device_id=(x, y, 1 - c), device_id_type=MESH)
        cp.start()
        cp.wait()
        mine.wait()

    return pl.pallas_call(
        body, name=name, in_specs=[HBM], out_specs=HBM,
        out_shape=jax.ShapeDtypeStruct((2,) + src.shape, src.dtype),
        scratch_shapes=[pltpu.SemaphoreType.DMA, pltpu.SemaphoreType.DMA, pltpu.SemaphoreType.DMA],
    )(src)


def _add_rows(a, b, *, name, tm=336):
    R, C = a.shape

    def body(a_ref, b_ref, o_ref):
        o_ref[...] = a_ref[...] + b_ref[...]

    return pl.pallas_call(
        body, name=name, grid=(R // tm,), in_specs=[_rows(tm, C), _rows(tm, C)], out_specs=_rows(tm, C),
        out_shape=jax.ShapeDtypeStruct((R, C), F32), compiler_params=_cparams(("parallel",)),
    )(a, b)


def _sum4(a, *, name, tm=336):
    _, R, C = a.shape

    def body(a_ref, o_ref):
        o_ref[...] = ((a_ref[0] + a_ref[1]) + a_ref[2]) + a_ref[3]

    return pl.pallas_call(
        body, name=name, grid=(R // tm,), in_specs=[pl.BlockSpec((4, tm, C), lambda i: (0, i, 0))],
        out_specs=_rows(tm, C), out_shape=jax.ShapeDtypeStruct((R, C), F32),
        compiler_params=_cparams(("parallel",)),
    )(a)


def _adamw(g, w, m, v, *, tm=336):
    R, C = g.shape
    c1 = 1.0 / (1.0 - ADAM_B1 ** ADAM_STEP)
    c2 = 1.0 / (1.0 - ADAM_B2 ** ADAM_STEP)

    def body(g_ref, w_ref, m_ref, v_ref, d_ref, nm_ref, nv_ref):
        gv = g_ref[...]
        nm = ADAM_B1 * m_ref[...] + (1.0 - ADAM_B1) * gv
        nv = ADAM_B2 * v_ref[...] + (1.0 - ADAM_B2) * (gv * gv)
        nm_ref[...] = nm
        nv_ref[...] = nv
        d_ref[...] = -ADAM_LR * ((nm * c1) / (jnp.sqrt(nv * c2) + ADAM_EPS) + ADAM_WD * w_ref[...])

    spec = _rows(tm, C)
    shape = jax.ShapeDtypeStruct((R, C), F32)
    return pl.pallas_call(
        body, name="adamw", grid=(R // tm,), in_specs=[spec] * 4, out_specs=[spec] * 3,
        out_shape=[shape] * 3, compiler_params=_cparams(("parallel",)),
    )(g, w, m, v)


BIG = (("w_in", (D, 1328), 1), ("w_uq", (QR, 384), 1), ("w_ukv", (KVR, 512), 1), ("w_branch", (576, D), 0),
       ("w_out", (256, D), 0), ("w_ffn_in", (D, 1408), 1), ("w_ffn_out", (704, D), 0))
SMALL = (("meta_tokens", (NMETA, 256), 1), ("b_gate", (2, 256), 1), ("conv_w", (CW, 320), 1))
REPL = (("norm_mix_g", (D,)), ("conv_b", (DR,)), ("w_rec_a", (NBLK, RB, RB)), ("b_rec_a", (DR,)),
        ("w_rec_i", (NBLK, RB, RB)), ("b_rec_i", (DR,)), ("lru_lambda", (DR,)), ("q_norm_g", (QR,)),
        ("kv_norm_g", (KVR,)), ("norm_ffn_g", (D,)), ("final_norm_g", (D,)))
WEIGHTS = ("meta_tokens", "norm_mix_g", "w_in", "b_gate", "conv_w", "conv_b", "w_rec_a", "b_rec_a", "w_rec_i",
           "b_rec_i", "lru_lambda", "q_norm_g", "w_uq", "kv_norm_g", "w_ukv", "w_branch", "w_out", "norm_ffn_g",
           "w_ffn_in", "w_ffn_out", "final_norm_g")
W = 1024
BIG_ROWS = sum(math.prod(s) for _, s, _ in BIG) // W
SMALL_N = sum(math.prod(s) for _, s, _ in SMALL)
SMALL_ROWS = 8
SHARD_N = BIG_ROWS * W + SMALL_N
SHARD_ROWS = 4560
REPL_N = sum(math.prod(s) for _, s in REPL)
QUART_ROWS = 144
FLAT_ROWS = SHARD_ROWS + QUART_ROWS
HALF_ROWS = FLAT_ROWS // 2


def _flat_pad(parts, rows):
    v = jnp.concatenate([p.reshape(-1) for p in parts])
    return jnp.pad(v, (0, rows * W - v.shape[0])).reshape(rows, W)


def _shards_of(full, shard_shape, axis):
    if axis == 0:
        return full.reshape(4, -1)
    r, cs = shard_shape
    return full.reshape(r, 4, cs).transpose(1, 0, 2).reshape(4, -1)


def _unshard(stack, shard_shape, axis):
    r, cs = shard_shape
    if axis == 0:
        return stack.reshape(4 * r, cs)
    return stack.reshape(4, r, cs).transpose(1, 0, 2).reshape(r, 4 * cs)


def _split(flat, table):
    out, off = {}, 0
    for name, shape, *_ in table:
        n = math.prod(shape)
        out[name] = flat[..., off:off + n].reshape(flat.shape[:-1] + tuple(shape))
        off += n
    return out


def _flat_state(args, prefix, quarter):
    sharded = jnp.concatenate([args[prefix + n].reshape(-1) for n, _, _ in BIG + SMALL])
    sharded = jnp.pad(sharded, (0, SHARD_ROWS * W - SHARD_N))
    repl = jnp.concatenate([args[prefix + n].reshape(-1) for n, _ in REPL])
    repl = jnp.pad(repl, (0, 4 * QUART_ROWS * W - REPL_N))
    mine = lax.dynamic_slice_in_dim(repl, quarter * (QUART_ROWS * W), QUART_ROWS * W)
    return jnp.concatenate([sharded, mine]).reshape(FLAT_ROWS, W)


def kernel(x, meta_tokens, norm_mix_g, w_in, b_gate, conv_w, conv_b, w_rec_a, b_rec_a, w_rec_i, b_rec_i, lru_lambda, q_norm_g, w_uq, kv_norm_g, w_ukv, w_branch, w_out, norm_ffn_g, w_ffn_in, w_ffn_out, final_norm_g, loss_target, m_meta_tokens, m_norm_mix_g, m_w_in, m_b_gate, m_conv_w, m_conv_b, m_w_rec_a, m_b_rec_a, m_w_rec_i, m_b_rec_i, m_lru_lambda, m_q_norm_g, m_w_uq, m_kv_norm_g, m_w_ukv, m_w_branch, m_w_out, m_norm_ffn_g, m_w_ffn_in, m_w_ffn_out, m_final_norm_g, v_meta_tokens, v_norm_mix_g, v_w_in, v_b_gate, v_conv_w, v_conv_b, v_w_rec_a, v_b_rec_a, v_w_rec_i, v_b_rec_i, v_lru_lambda, v_q_norm_g, v_w_uq, v_kv_norm_g, v_w_ukv, v_w_branch, v_w_out, v_norm_ffn_g, v_w_ffn_in, v_w_ffn_out, v_final_norm_g):
    args = dict(locals())
    chip = 2 * lax.axis_index("x") + lax.axis_index("y")
    core = lax.axis_index("c")

    big = jnp.concatenate([args[n].reshape(-1) for n, _, _ in BIG]).astype(BF16).reshape(BIG_ROWS, W)
    small = _flat_pad([args[n] for n, _, _ in SMALL], SMALL_ROWS)
    big_all = _allgather_chips(big, name="gather_big").reshape(4, BIG_ROWS * W)
    small_all = _allgather_chips(small, name="gather_small").reshape(4, SMALL_ROWS * W)
    w = {}
    for table, stack in ((BIG, big_all), (SMALL, small_all)):
        parts = _split(stack, table)
        for name, shape, axis in table:
            w[name] = _unshard(parts[name].reshape(4, -1), shape, axis)
    for name, shape in REPL:
        w[name] = args[name].reshape(shape)

    loss, grad_x, g = _local_step(x[0], loss_target[0], w)
    loss = lax.psum(loss, ("x", "y", "c"))

    sharded = jnp.concatenate([_shards_of(g[n], s, a) for n, s, a in BIG + SMALL], axis=1)
    sharded = jnp.pad(sharded, ((0, 0), (0, SHARD_ROWS * W - SHARD_N)))
    repl = jnp.concatenate([g[n].reshape(-1) for n, _ in REPL])
    repl = jnp.pad(repl, (0, 4 * QUART_ROWS * W - REPL_N)).reshape(4, QUART_ROWS * W)
    gflat = jnp.concatenate([sharded, repl], axis=1).reshape(4, 2, HALF_ROWS, W).transpose(1, 0, 2, 3)

    theirs = _sibling_take(gflat, name="reduce_sibling")
    mine = lax.dynamic_index_in_dim(gflat, core, 0, keepdims=False)
    part = _add_rows(mine.reshape(4 * HALF_ROWS, W), theirs.reshape(4 * HALF_ROWS, W), name="add_sibling")
    parts4 = _scatter_chips(part.reshape(4, HALF_ROWS, W), name="reduce_chips")
    half = _sum4(parts4, name="sum_chips")
    gred = _sibling_pair(half, name="share_sibling").reshape(FLAT_ROWS, W)

    delta, new_m, new_v = _adamw(gred, _flat_state(args, "", chip), _flat_state(args, "m_", chip),
                                 _flat_state(args, "v_", chip))

    quarters = jnp.stack([t[SHARD_ROWS:] for t in (gred, delta, new_m, new_v)])
    repl_all = _allgather_chips(quarters, name="gather_repl")
    repl_all = repl_all.transpose(1, 0, 2, 3).reshape(4, 4 * QUART_ROWS * W)

    outs = []
    for k, flat in enumerate((gred, delta, new_m, new_v)):
        sh = _split(flat.reshape(-1)[:SHARD_N], BIG + SMALL)
        rp = _split(repl_all[k, :REPL_N], REPL)
        for name in WEIGHTS:
            val = sh[name] if name in sh else rp[name]
            outs.append(val.reshape(args[name].shape))
    return (loss, grad_x[None], *outs)
```

```python
import functools
import math

import jax
import jax.numpy as jnp
from jax import lax
from jax.experimental import pallas as pl
from jax.experimental.pallas import tpu as pltpu

F32 = jnp.float32
BF16 = jnp.bfloat16

D = 1024
DR = 1280
NBLK = 10
RB = 128
CW = 4
NH = 8
NOPE = 128
ROPE = 64
VD = 128
QR = 384
KVR = 256
DFF = 2816
NMETA = 16
EPS = 1e-6
LRU_C = 8.0
ROPE_THETA = 10000.0
SCALE = 1.0 / math.sqrt(NOPE + ROPE)
NEG = -1e30
FRONT = 128
PAD = FRONT - NMETA
QW = 2 * NOPE
LANES = 128
SUB = 128
CHAINS = 2
VMEM_LIMIT = 52 * 1024 * 1024

ADAM_LR = 0.001
ADAM_B1 = 0.9
ADAM_B2 = 0.999
ADAM_EPS = 1e-08
ADAM_WD = 0.01
ADAM_STEP = 10

MESH = pl.DeviceIdType.MESH


def _cparams(sem):
    return pltpu.CompilerParams(dimension_semantics=sem, vmem_limit_bytes=VMEM_LIMIT)


def _sigmoid(x):
    return 1.0 / (1.0 + jnp.exp(-x))


def _gelu_parts(x):
    c = math.sqrt(2.0 / math.pi)
    inner = c * (x + 0.044715 * x * x * x)
    t = jnp.tanh(inner)
    g = 0.5 * x * (1.0 + t)
    dg = 0.5 * (1.0 + t) + 0.5 * x * (1.0 - t * t) * c * (1.0 + 3.0 * 0.044715 * x * x)
    return g, dg


def _mm(a, b, *, name, tm=640, tn=512, out_dtype=F32, res=None):
    M, K = a.shape
    N = b.shape[1]
    tm = min(tm, M)
    tn = min(tn, N)
    assert M % tm == 0 and N % tn == 0, (name, M, N, tm, tn)
    has_res = res is not None

    def body(*refs):
        if has_res:
            a_ref, b_ref, r_ref, o_ref = refs
        else:
            a_ref, b_ref, o_ref = refs
        acc = jnp.dot(a_ref[...].astype(BF16), b_ref[...].astype(BF16), preferred_element_type=F32)
        if has_res:
            acc = acc + r_ref[...].astype(F32)
        o_ref[...] = acc.astype(o_ref.dtype)

    a_bytes = M * K * a.dtype.itemsize
    b_bytes = K * N * b.dtype.itemsize
    rows_outer = a_bytes + (M // tm) * b_bytes <= b_bytes + (N // tn) * a_bytes
    if rows_outer:
        grid = (M // tm, N // tn)
        ia, ib, io = (lambda i, j: (i, 0)), (lambda i, j: (0, j)), (lambda i, j: (i, j))
    else:
        grid = (N // tn, M // tm)
        ia, ib, io = (lambda j, i: (i, 0)), (lambda j, i: (0, j)), (lambda j, i: (i, j))
    in_specs = [pl.BlockSpec((tm, K), ia), pl.BlockSpec((K, tn), ib)]
    args = [a, b]
    if has_res:
        in_specs.append(pl.BlockSpec((tm, tn), io))
        args.append(res)
    return pl.pallas_call(
        body, name=name, grid=grid, in_specs=in_specs,
        out_specs=pl.BlockSpec((tm, tn), io),
        out_shape=jax.ShapeDtypeStruct((M, N), out_dtype),
        compiler_params=_cparams(("parallel", "parallel")),
    )(*args)


def _mm_tn(a, b, *, name, tk=1024, tn=1024, tt=640):
    T, K1 = a.shape
    N = b.shape[1]
    tk = min(tk, K1)
    tn = min(tn, N)
    assert T % tt == 0 and K1 % tk == 0 and N % tn == 0, (name, T, K1, N)

    def body(a_ref, b_ref, o_ref):
        @pl.when(pl.program_id(2) == 0)
        def _():
            o_ref[...] = jnp.zeros_like(o_ref)

        o_ref[...] += lax.dot_general(a_ref[...].astype(BF16), b_ref[...].astype(BF16),
                                      (((0,), (0,)), ((), ())), preferred_element_type=F32)

    return pl.pallas_call(
        body, name=name, grid=(K1 // tk, N // tn, T // tt),
        in_specs=[pl.BlockSpec((tt, tk), lambda i, j, t: (t, i)),
                  pl.BlockSpec((tt, tn), lambda i, j, t: (t, j))],
        out_specs=pl.BlockSpec((tk, tn), lambda i, j, t: (i, j)),
        out_shape=jax.ShapeDtypeStruct((K1, N), F32),
        compiler_params=_cparams(("parallel", "parallel", "arbitrary")),
    )(a, b)


def _rows(tm, w, cb=0):
    return pl.BlockSpec((tm, w), lambda i: (i, cb))


def _const(shape):
    n = len(shape)
    return pl.BlockSpec(shape, lambda i: (0,) * n)


def _rmsnorm_fwd(x, g, *, name, tm=640):
    T, C = x.shape

    def body(x_ref, g_ref, o_ref):
        xv = x_ref[...]
        r = lax.rsqrt(jnp.mean(xv * xv, axis=-1, keepdims=True) + EPS)
        o_ref[...] = ((xv * r) * g_ref[...]).astype(BF16)

    return pl.pallas_call(
        body, name=name, grid=(T // tm,),
        in_specs=[_rows(tm, C), _const((1, C))],
        out_specs=_rows(tm, C),
        out_shape=jax.ShapeDtypeStruct((T, C), BF16),
        compiler_params=_cparams(("parallel",)),
    )(x, g)


def _rmsnorm_bwd(x, g, dy, res, *, name, tm=640, want_f32=True, want_bf16=True):
    T, C = x.shape
    has_res = res is not None

    def body(*refs):
        refs = list(refs)
        x_ref, g_ref, dy_ref = refs[:3]
        refs = refs[3:]
        r_ref = refs.pop(0) if has_res else None
        o32 = refs.pop(0) if want_f32 else None
        o16 = refs.pop(0) if want_bf16 else None
        dg_ref = refs.pop(0)

        @pl.when(pl.program_id(0) == 0)
        def _():
            dg_ref[...] = jnp.zeros_like(dg_ref)

        xv = x_ref[...]
        dyv = dy_ref[...].astype(F32)
        r = lax.rsqrt(jnp.mean(xv * xv, axis=-1, keepdims=True) + EPS)
        xn = xv * r
        dg_ref[...] += jnp.sum(dyv * xn, axis=0, keepdims=True)
        dxn = dyv * g_ref[...]
        dx = r * (dxn - xn * jnp.mean(dxn * xn, axis=-1, keepdims=True))
        if has_res:
            dx = dx + r_ref[...]
        if want_f32:
            o32[...] = dx
        if want_bf16:
            o16[...] = dx.astype(BF16)

    in_specs = [_rows(tm, C), _const((1, C)), _rows(tm, C)]
    args = [x, g, dy]
    if has_res:
        in_specs.append(_rows(tm, C))
        args.append(res)
    out_specs, out_shape = [], []
    if want_f32:
        out_specs.append(_rows(tm, C))
        out_shape.append(jax.ShapeDtypeStruct((T, C), F32))
    if want_bf16:
        out_specs.append(_rows(tm, C))
        out_shape.append(jax.ShapeDtypeStruct((T, C), BF16))
    out_specs.append(_const((1, C)))
    out_shape.append(jax.ShapeDtypeStruct((1, C), F32))
    return pl.pallas_call(
        body, name=name, grid=(T // tm,), in_specs=in_specs, out_specs=out_specs,
        out_shape=out_shape, compiler_params=_cparams(("arbitrary",)),
    )(*args)


def _gate_mix_fwd(um, bg, p_rnn, p_att, *, tm=320):
    T = um.shape[0]

    def body(um_ref, bg_ref, pr_ref, pa_ref, o_ref):
        g = _sigmoid(um_ref[...] + bg_ref[...])
        o_ref[...] = (g[:, :D] * pr_ref[...] + g[:, D:] * pa_ref[...]).astype(BF16)

    return pl.pallas_call(
        body, name="gate_mix_fwd", grid=(T // tm,),
        in_specs=[_rows(tm, 2 * D), _const((1, 2 * D)), _rows(tm, D), _rows(tm, D)],
        out_specs=_rows(tm, D),
        out_shape=jax.ShapeDtypeStruct((T, D), BF16),
        compiler_params=_cparams(("parallel",)),
    )(um, bg, p_rnn, p_att)


def _gate_mix_bwd(um, bg, p_rnn, p_att, dmixed, *, tm=320):
    T = um.shape[0]

    def body(um_ref, bg_ref, pr_ref, pa_ref, dm_ref, dpr_ref, dpa_ref, dum_ref, dbg_ref):
        @pl.when(pl.program_id(0) == 0)
        def _():
            dbg_ref[...] = jnp.zeros_like(dbg_ref)

        g = _sigmoid(um_ref[...] + bg_ref[...])
        g0, g1 = g[:, :D], g[:, D:]
        dm = dm_ref[...]
        dpr_ref[...] = (dm * g0).astype(BF16)
        dpa_ref[...] = (dm * g1).astype(BF16)
        d0 = dm * pr_ref[...] * g0 * (1.0 - g0)
        d1 = dm * pa_ref[...] * g1 * (1.0 - g1)
        dum_ref[:, :D] = d0.astype(BF16)
        dum_ref[:, D:] = d1.astype(BF16)
        dbg_ref[:, :D] += jnp.sum(d0, axis=0, keepdims=True)
        dbg_ref[:, D:] += jnp.sum(d1, axis=0, keepdims=True)

    return pl.pallas_call(
        body, name="gate_mix_bwd", grid=(T // tm,),
        in_specs=[_rows(tm, 2 * D), _const((1, 2 * D)), _rows(tm, D), _rows(tm, D), _rows(tm, D)],
        out_specs=[_rows(tm, D), _rows(tm, D), _rows(tm, 2 * D), _const((1, 2 * D))],
        out_shape=[jax.ShapeDtypeStruct((T, D), BF16), jax.ShapeDtypeStruct((T, D), BF16),
                   jax.ShapeDtypeStruct((T, 2 * D), BF16), jax.ShapeDtypeStruct((1, 2 * D), F32)],
        compiler_params=_cparams(("arbitrary",)),
    )(um, bg, p_rnn, p_att, dmixed)


def _swiglu_fwd(ff, *, tm=320):
    T = ff.shape[0]

    def body(g_ref, u_ref, o_ref):
        gv = g_ref[...]
        o_ref[...] = (gv * _sigmoid(gv) * u_ref[...]).astype(BF16)

    return pl.pallas_call(
        body, name="swiglu_fwd", grid=(T // tm,),
        in_specs=[_rows(tm, DFF, 0), _rows(tm, DFF, 1)],
        out_specs=_rows(tm, DFF),
        out_shape=jax.ShapeDtypeStruct((T, DFF), BF16),
        compiler_params=_cparams(("parallel",)),
    )(ff, ff)


def _swiglu_bwd(ff, dact, *, tm=320):
    T = ff.shape[0]

    def body(g_ref, u_ref, da_ref, o_ref):
        gv = g_ref[...]
        s = _sigmoid(gv)
        da = da_ref[...]
        o_ref[:, :DFF] = (da * u_ref[...] * s * (1.0 + gv * (1.0 - s))).astype(BF16)
        o_ref[:, DFF:] = (da * gv * s).astype(BF16)

    return pl.pallas_call(
        body, name="swiglu_bwd", grid=(T // tm,),
        in_specs=[_rows(tm, DFF, 0), _rows(tm, DFF, 1), _rows(tm, DFF)],
        out_specs=_rows(tm, 2 * DFF),
        out_shape=jax.ShapeDtypeStruct((T, 2 * DFF), BF16),
        compiler_params=_cparams(("parallel",)),
    )(ff, ff, dact)


def _loss_head(h2, tgt, g, *, tm=FRONT):
    T = h2.shape[0]
    front_blocks = FRONT // tm

    def body(h_ref, t_ref, g_ref, d32_ref, d16_ref, dg_ref, ls_ref):
        i = pl.program_id(0)

        @pl.when(i == 0)
        def _():
            dg_ref[...] = jnp.zeros_like(dg_ref)
            ls_ref[...] = jnp.zeros_like(ls_ref)

        xv = h_ref[...]
        r = lax.rsqrt(jnp.mean(xv * xv, axis=-1, keepdims=True) + EPS)
        xn = xv * r
        gv = g_ref[...]
        e = jnp.where(i >= front_blocks, xn * gv - t_ref[...], 0.0)
        ls_ref[...] += jnp.sum(e * e, axis=0, keepdims=True)
        dy = e * (1.0 / D)
        dg_ref[...] += jnp.sum(dy * xn, axis=0, keepdims=True)
        dxn = dy * gv
        dx = r * (dxn - xn * jnp.mean(dxn * xn, axis=-1, keepdims=True))
        d32_ref[...] = dx
        d16_ref[...] = dx.astype(BF16)

    return pl.pallas_call(
        body, name="loss_head", grid=(T // tm,),
        in_specs=[_rows(tm, D), pl.BlockSpec((tm, D), lambda i: (jnp.maximum(i - front_blocks, 0), 0)),
                  _const((1, D))],
        out_specs=[_rows(tm, D), _rows(tm, D), _const((1, D)), _const((1, D))],
        out_shape=[jax.ShapeDtypeStruct((T, D), F32), jax.ShapeDtypeStruct((T, D), BF16),
                   jax.ShapeDtypeStruct((1, D), F32), jax.ShapeDtypeStruct((1, D), F32)],
        compiler_params=_cparams(("arbitrary",)),
    )(h2, tgt, g)


def _scan_fwd(a, b, h_in):
    n = a.shape[0]
    row = lax.broadcasted_iota(jnp.int32, a.shape, 0)
    s = 1
    while s < n:
        a_sh = jnp.where(row >= s, pltpu.roll(a, s, 0), 1.0)
        b_sh = jnp.where(row >= s, pltpu.roll(b, s, 0), 0.0)
        b = a * b_sh + b
        a = a * a_sh
        s *= 2
    return b + a * h_in


def _scan_rev(a, b, g_in):
    n = a.shape[0]
    row = lax.broadcasted_iota(jnp.int32, a.shape, 0)
    s = 1
    while s < n:
        a_sh = jnp.where(row < n - s, pltpu.roll(a, n - s, 0), 1.0)
        b_sh = jnp.where(row < n - s, pltpu.roll(b, n - s, 0), 0.0)
        b = a * b_sh + b
        a = a * a_sh
        s *= 2
    return b + a * g_in


def _lru_gates(xc, wa, ba, wi, bi, lam):
    xcb = xc.astype(BF16)
    r = _sigmoid(jnp.dot(xcb, wa, preferred_element_type=F32) + ba)
    ig = _sigmoid(jnp.dot(xcb, wi, preferred_element_type=F32) + bi)
    log_sig = jnp.minimum(lam, 0.0) - jnp.log(1.0 + jnp.exp(-jnp.abs(lam)))
    log_a = LRU_C * r * log_sig
    a = jnp.exp(log_a)
    z = 2.0 * log_a
    poly = -z * (1.0 + z * (0.5 + z * (1.0 / 6.0 + z * (1.0 / 24.0))))
    m2 = jnp.where(z > -0.03, poly, 1.0 - jnp.exp(z))
    return r, ig, log_sig, a, jnp.sqrt(m2)


def _rnn_specs(tc, nblk_t, rev):
    def tmap(k):
        return (nblk_t - 1 - k) if rev else k

    hb = tc // 8
    blk = lambda off: pl.BlockSpec((tc, RB), lambda c, k: (tmap(k), c + off))
    halo = lambda off: pl.BlockSpec((8, RB), lambda c, k: (jnp.maximum(tmap(k) * hb - 1, 0), c + off))
    vec = pl.BlockSpec((1, RB), lambda c, k: (0, c))
    cwv = pl.BlockSpec((CW, RB), lambda c, k: (0, c))
    mat = pl.BlockSpec((None, RB, RB), lambda c, k: (c, 0, 0))
    return blk, halo, vec, cwv, mat


def _rnn_fwd(uxg, cw, cb, wa, ba, wi, bi, lam, *, tc=640):
    T = uxg.shape[0]
    nt = T // tc
    nsub = tc // SUB
    blk, halo, vec, cwv, mat = _rnn_specs(tc, nt, False)

    def body(x_ref, xh_ref, ug_ref, cw_ref, cb_ref, wa_ref, ba_ref, wi_ref, bi_ref, lam_ref,
             h_ref, y_ref, xb, hc):
        k = pl.program_id(1)

        @pl.when(k == 0)
        def _():
            hc[...] = jnp.zeros_like(hc)

        xb[0:8, :] = jnp.where(k > 0, xh_ref[...], 0.0)
        xb[8:, :] = x_ref[...]
        cwv_, cbv = cw_ref[...], cb_ref[...]
        wav, wiv = wa_ref[...], wi_ref[...]
        bav, biv, lamv = ba_ref[...], bi_ref[...], lam_ref[...]
        h_in = hc[0:1, :]
        for sc in range(nsub):
            r0 = sc * SUB
            xc = cbv + cwv_[0:1, :] * xb[pl.ds(5 + r0, SUB), :]
            for j in range(1, CW):
                xc = xc + cwv_[j:j + 1, :] * xb[pl.ds(5 + j + r0, SUB), :]
            r, ig, _, a, mm = _lru_gates(xc, wav, bav, wiv, biv, lamv)
            rows = k * tc + r0 + lax.broadcasted_iota(jnp.int32, (SUB, RB), 0)
            b = jnp.where(rows >= PAD, mm * (ig * xc), 0.0)
            h = _scan_fwd(a, b, h_in)
            h_in = h[SUB - 1:SUB, :]
            h_ref[pl.ds(r0, SUB), :] = h
            gl, _ = _gelu_parts(ug_ref[pl.ds(r0, SUB), :])
            y_ref[pl.ds(r0, SUB), :] = (h * gl).astype(BF16)
        hc[0:1, :] = h_in

    return pl.pallas_call(
        body, name="rnn_fwd", grid=(NBLK, nt),
        in_specs=[blk(0), halo(0), blk(NBLK), cwv, vec, mat, vec, mat, vec, vec],
        out_specs=[blk(0), blk(0)],
        out_shape=[jax.ShapeDtypeStruct((T, DR), F32), jax.ShapeDtypeStruct((T, DR), BF16)],
        scratch_shapes=[pltpu.VMEM((tc + 8, RB), F32), pltpu.VMEM((8, RB), F32)],
        compiler_params=_cparams(("parallel", "arbitrary")),
    )(uxg, uxg, uxg, cw, cb, wa, ba, wi, bi, lam)


def _rnn_bwd(uxg, hs, dy, cw, cb, wa, ba, wi, bi, lam, wat, wit, *, tc=640):
    T = uxg.shape[0]
    nt = T // tc
    nsub = tc // SUB
    blk, halo, vec, cwv, mat = _rnn_specs(tc, nt, True)

    def body(x_ref, xh_ref, ug_ref, h_ref, hh_ref, dy_ref, cw_ref, cb_ref, wa_ref, ba_ref, wi_ref,
             bi_ref, lam_ref, wat_ref, wit_ref,
             dux_ref, dug_ref, dcw_ref, dcb_ref, dwa_ref, dba_ref, dwi_ref, dbi_ref, dlam_ref,
             xb, hb, ab, dxb, xcs, rs, igs, mms, dgas, dgis, carry):
        k = pl.program_id(1)
        kt = nt - 1 - k

        @pl.when(k == 0)
        def _():
            carry[...] = jnp.zeros_like(carry)
            for ref in (dcw_ref, dcb_ref, dwa_ref, dba_ref, dwi_ref, dbi_ref, dlam_ref):
                ref[...] = jnp.zeros_like(ref)

        xb[0:8, :] = jnp.where(kt > 0, xh_ref[...], 0.0)
        xb[8:, :] = x_ref[...]
        hb[0:8, :] = jnp.where(kt > 0, hh_ref[...], 0.0)
        hb[8:, :] = h_ref[...]
        cwv_, cbv = cw_ref[...], cb_ref[...]
        wav, wiv = wa_ref[...], wi_ref[...]
        bav, biv, lamv = ba_ref[...], bi_ref[...], lam_ref[...]
        ab[tc:tc + 8, :] = jnp.broadcast_to(carry[1:2, :], (8, RB))
        dxb[tc:tc + 8, :] = carry[8:16, :]
        log_sig = None
        for sc in range(nsub):
            r0 = sc * SUB
            xc = cbv + cwv_[0:1, :] * xb[pl.ds(5 + r0, SUB), :]
            for j in range(1, CW):
                xc = xc + cwv_[j:j + 1, :] * xb[pl.ds(5 + j + r0, SUB), :]
            r, ig, log_sig, a, mm = _lru_gates(xc, wav, bav, wiv, biv, lamv)
            xcs[pl.ds(r0, SUB), :] = xc
            rs[pl.ds(r0, SUB), :] = r
            igs[pl.ds(r0, SUB), :] = ig
            mms[pl.ds(r0, SUB), :] = mm
            ab[pl.ds(r0, SUB), :] = a
        sig_neg = _sigmoid(-lamv)
        g_in = carry[0:1, :]
        dlam_acc = jnp.zeros((1, RB), F32)
        for sc in reversed(range(nsub)):
            r0 = sc * SUB
            xc, r, ig, mm = xcs[pl.ds(r0, SUB), :], rs[pl.ds(r0, SUB), :], igs[pl.ds(r0, SUB), :], mms[pl.ds(r0, SUB), :]
            a = ab[pl.ds(r0, SUB), :]
            a_next = ab[pl.ds(r0 + 1, SUB), :]
            hv = hb[pl.ds(8 + r0, SUB), :]
            hprev = hb[pl.ds(7 + r0, SUB), :]
            dyv = dy_ref[pl.ds(r0, SUB), :]
            gl, dgl = _gelu_parts(ug_ref[pl.ds(r0, SUB), :])
            dug_ref[pl.ds(r0, SUB), :] = (dyv * hv * dgl).astype(BF16)
            G = _scan_rev(a_next, dyv * gl, g_in)
            g_in = G[0:1, :]
            rows = kt * tc + r0 + lax.broadcasted_iota(jnp.int32, (SUB, RB), 0)
            db = jnp.where(rows >= PAD, G, 0.0)
            da = G * hprev
            dmm = db * (ig * xc)
            di = db * (mm * xc)
            dxc = db * (mm * ig)
            dlog_a = da * a - dmm * (a * a) / jnp.maximum(mm, 1e-30)
            dr = dlog_a * (LRU_C * log_sig)
            dlam_acc = dlam_acc + jnp.sum(dlog_a * (LRU_C * r), axis=0, keepdims=True)
            dga = dr * r * (1.0 - r)
            dgi = di * ig * (1.0 - ig)
            dgab, dgib = dga.astype(BF16), dgi.astype(BF16)
            dgas[pl.ds(r0, SUB), :] = dgab
            dgis[pl.ds(r0, SUB), :] = dgib
            dba_ref[...] += jnp.sum(dga, axis=0, keepdims=True)
            dbi_ref[...] += jnp.sum(dgi, axis=0, keepdims=True)
            dxc = dxc + jnp.dot(dgab, wat_ref[...], preferred_element_type=F32) \
                + jnp.dot(dgib, wit_ref[...], preferred_element_type=F32)
            dxb[pl.ds(r0, SUB), :] = dxc
        dlam_ref[...] += dlam_acc * sig_neg
        xcb = xcs[...].astype(BF16)
        tn = (((0,), (0,)), ((), ()))
        dwa_ref[...] += lax.dot_general(xcb, dgas[...], tn, preferred_element_type=F32)
        dwi_ref[...] += lax.dot_general(xcb, dgis[...], tn, preferred_element_type=F32)
        dxc_all = dxb[0:tc, :]
        dcb_ref[...] += jnp.sum(dxc_all, axis=0, keepdims=True)
        rows_all = kt * tc + lax.broadcasted_iota(jnp.int32, (tc, RB), 0)
        dux = jnp.zeros((tc, RB), F32)
        for j in range(CW):
            dcw_ref[j:j + 1, :] += jnp.sum(dxc_all * xb[pl.ds(5 + j, tc), :], axis=0, keepdims=True)
            dux = dux + cwv_[j:j + 1, :] * dxb[pl.ds(CW - 1 - j, tc), :]
        dux_ref[...] = jnp.where(rows_all >= PAD, dux, 0.0).astype(BF16)
        carry[0:1, :] = g_in
        carry[1:2, :] = ab[0:1, :]
        carry[8:16, :] = dxb[0:8, :]

    vec_out = pl.BlockSpec((1, RB), lambda c, k: (0, c))
    return pl.pallas_call(
        body, name="rnn_bwd", grid=(NBLK, nt),
        in_specs=[blk(0), halo(0), blk(NBLK), blk(0), halo(0), blk(0), cwv, vec, mat, vec, mat, vec, vec, mat, mat],
        out_specs=[blk(0), blk(0), cwv, vec_out, mat, vec_out, mat, vec_out, vec_out],
        out_shape=[jax.ShapeDtypeStruct((T, DR), BF16), jax.ShapeDtypeStruct((T, DR), BF16),
                   jax.ShapeDtypeStruct((CW, DR), F32), jax.ShapeDtypeStruct((1, DR), F32),
                   jax.ShapeDtypeStruct((NBLK, RB, RB), F32), jax.ShapeDtypeStruct((1, DR), F32),
                   jax.ShapeDtypeStruct((NBLK, RB, RB), F32), jax.ShapeDtypeStruct((1, DR), F32),
                   jax.ShapeDtypeStruct((1, DR), F32)],
        scratch_shapes=[pltpu.VMEM((tc + 8, RB), F32), pltpu.VMEM((tc + 8, RB), F32),
                        pltpu.VMEM((tc + 8, RB), F32), pltpu.VMEM((tc + 8, RB), F32),
                        pltpu.VMEM((tc, RB), F32), pltpu.VMEM((tc, RB), F32), pltpu.VMEM((tc, RB), F32),
                        pltpu.VMEM((tc, RB), F32), pltpu.VMEM((tc, RB), BF16), pltpu.VMEM((tc, RB), BF16),
                        pltpu.VMEM((16, RB), F32)],
        compiler_params=_cparams(("parallel", "arbitrary")),
    )(uxg, uxg, uxg, hs, hs, dy, cw, cb, wa, ba, wi, bi, lam, wat, wit)


def _attn_prep(q_all, kv_all, ukr, tab, *, tm=320):
    T = q_all.shape[0]

    def body(q_ref, kv_ref, kr_ref, tab_ref, qo_ref, ko_ref, vo_ref):
        tab_v = tab_ref[...]
        lane = lax.broadcasted_iota(jnp.int32, (tm, LANES), 1)
        t1 = kr_ref[...] * tab_v
        kro = jnp.where(lane < ROPE, t1 + pltpu.roll(t1, ROPE, 1), 0.0).astype(BF16)
        for h in range(NH):
            c0 = h * QW
            qo_ref[h, :, 0:NOPE] = (q_ref[:, c0:c0 + NOPE] * SCALE).astype(BF16)
            t2 = q_ref[:, c0 + NOPE:c0 + QW] * tab_v
            qo_ref[h, :, NOPE:QW] = ((t2 + pltpu.roll(t2, ROPE, 1)) * SCALE).astype(BF16)
            ko_ref[h, :, 0:NOPE] = kv_ref[:, c0:c0 + NOPE].astype(BF16)
            ko_ref[h, :, NOPE:QW] = kro
            vo_ref[h, :, :] = kv_ref[:, c0 + NOPE:c0 + QW].astype(BF16)

    return pl.pallas_call(
        body, name="attn_prep", grid=(T // tm,),
        in_specs=[_rows(tm, NH * QW), _rows(tm, NH * QW), _rows(tm, LANES), _rows(tm, LANES)],
        out_specs=[pl.BlockSpec((NH, tm, QW), lambda i: (0, i, 0)), pl.BlockSpec((NH, tm, QW), lambda i: (0, i, 0)),
                   pl.BlockSpec((NH, tm, VD), lambda i: (0, i, 0))],
        out_shape=[jax.ShapeDtypeStruct((NH, T, QW), BF16), jax.ShapeDtypeStruct((NH, T, QW), BF16),
                   jax.ShapeDtypeStruct((NH, T, VD), BF16)],
        compiler_params=_cparams(("parallel",)),
    )(q_all, kv_all, ukr, tab)


def _attn_prep_bwd(dq, dk, dv, tab, *, tm=320):
    T = dq.shape[1]

    def body(dq_ref, dk_ref, dv_ref, tab_ref, dqa_ref, dkva_ref, dkr_ref):
        tab_v = tab_ref[...]
        lane = lax.broadcasted_iota(jnp.int32, (tm, LANES), 1)
        dkro = jnp.zeros((tm, LANES), F32)
        for h in range(NH):
            c0 = h * QW
            dqa_ref[:, c0:c0 + NOPE] = (dq_ref[h, :, 0:NOPE] * SCALE).astype(BF16)
            d2 = dq_ref[h, :, NOPE:QW]
            dqa_ref[:, c0 + NOPE:c0 + QW] = ((d2 + pltpu.roll(d2, ROPE, 1)) * tab_v * SCALE).astype(BF16)
            dkva_ref[:, c0:c0 + NOPE] = dk_ref[h, :, 0:NOPE].astype(BF16)
            dkva_ref[:, c0 + NOPE:c0 + QW] = dv_ref[h, :, :].astype(BF16)
            dkro = dkro + dk_ref[h, :, NOPE:QW]
        dkro = jnp.where(lane < ROPE, dkro, 0.0)
        dkr_ref[...] = ((dkro + pltpu.roll(dkro, ROPE, 1)) * tab_v).astype(BF16)

    return pl.pallas_call(
        body, name="attn_prep_bwd", grid=(T // tm,),
        in_specs=[pl.BlockSpec((NH, tm, QW), lambda i: (0, i, 0)), pl.BlockSpec((NH, tm, QW), lambda i: (0, i, 0)),
                  pl.BlockSpec((NH, tm, VD), lambda i: (0, i, 0)), _rows(tm, LANES)],
        out_specs=[_rows(tm, NH * QW), _rows(tm, NH * QW), _rows(tm, LANES)],
        out_shape=[jax.ShapeDtypeStruct((T, NH * QW), BF16), jax.ShapeDtypeStruct((T, NH * QW), BF16),
                   jax.ShapeDtypeStruct((T, LANES), BF16)],
        compiler_params=_cparams(("parallel",)),
    )(dq, dk, dv, tab)


def _visible(q0, k0, nq, nk):
    rows = q0 + lax.broadcasted_iota(jnp.int32, (nq, nk), 0)
    cols = k0 + lax.broadcasted_iota(jnp.int32, (nq, nk), 1)
    return ((cols >> 6) <= (rows >> 6)) & (cols >= PAD)


def _visible_t(q0, k0, nq, nk):
    cols = k0 + lax.broadcasted_iota(jnp.int32, (nk, nq), 0)
    rows = q0 + lax.broadcasted_iota(jnp.int32, (nk, nq), 1)
    return ((cols >> 6) <= (rows >> 6)) & (cols >= PAD)


_NT = (((1,), (1,)), ((), ()))


def _flash_fwd(q, k, v, *, bq=640):
    T = q.shape[1]
    nq = T // bq
    rs = bq // CHAINS

    def body(q_ref, k_ref, v_ref, o_ref, lse_ref, m_s, l_s, acc_s):
        i = pl.program_id(1)
        m_s[...] = jnp.full_like(m_s, NEG)
        l_s[...] = jnp.zeros_like(l_s)
        acc_s[...] = jnp.zeros_like(acc_s)

        def step(j, masked):
            off = pl.multiple_of(j * bq, bq)
            kv_ = k_ref[pl.ds(off, bq), :]
            vv = v_ref[pl.ds(off, bq), :]
            for r in range(CHAINS):
                rows = pl.ds(r * rs, rs)
                s = lax.dot_general(q_ref[rows, :], kv_, _NT, preferred_element_type=F32)
                if masked:
                    s = jnp.where(_visible(i * bq + r * rs, j * bq, rs, bq), s, NEG)
                m_prev = m_s[rows, :]
                m_new = jnp.maximum(m_prev, jnp.max(s, axis=-1, keepdims=True))
                p = jnp.exp(s - m_new)
                alpha = jnp.exp(m_prev - m_new)
                l_s[rows, :] = alpha * l_s[rows, :] + jnp.sum(p, axis=-1, keepdims=True)
                acc_s[rows, :] = alpha * acc_s[rows, :] + jnp.dot(p.astype(BF16), vv, preferred_element_type=F32)
                m_s[rows, :] = m_new

        step(0, True)

        def loop(j, c):
            step(j, False)
            return c

        lax.fori_loop(1, i, loop, 0)

        @pl.when(i > 0)
        def _():
            step(i, True)

        o_ref[...] = (acc_s[...] / l_s[...]).astype(BF16)
        lse_ref[...] = m_s[...] + jnp.log(l_s[...])

    return pl.pallas_call(
        body, name="flash_fwd", grid=(NH, nq),
        in_specs=[pl.BlockSpec((None, bq, QW), lambda h, i: (h, i, 0)),
                  pl.BlockSpec((None, T, QW), lambda h, i: (h, 0, 0)),
                  pl.BlockSpec((None, T, VD), lambda h, i: (h, 0, 0))],
        out_specs=[pl.BlockSpec((bq, VD), lambda h, i: (i, h)),
                   pl.BlockSpec((None, bq, 1), lambda h, i: (h, i, 0))],
        out_shape=[jax.ShapeDtypeStruct((T, NH * VD), BF16), jax.ShapeDtypeStruct((NH, T, 1), F32)],
        scratch_shapes=[pltpu.VMEM((bq, 1), F32), pltpu.VMEM((bq, 1), F32), pltpu.VMEM((bq, VD), F32)],
        compiler_params=_cparams(("parallel", "parallel")),
    )(q, k, v)


def _attn_delta(o, do, *, tm=640):
    T = o.shape[0]

    def body(o_ref, do_ref, d_ref):
        prod = o_ref[...].astype(F32) * do_ref[...].astype(F32)
        for h in range(NH):
            d_ref[h, :, :] = jnp.sum(prod[:, h * VD:(h + 1) * VD], axis=-1, keepdims=True)

    return pl.pallas_call(
        body, name="attn_delta", grid=(T // tm,),
        in_specs=[_rows(tm, NH * VD), _rows(tm, NH * VD)],
        out_specs=pl.BlockSpec((NH, tm, 1), lambda i: (0, i, 0)),
        out_shape=jax.ShapeDtypeStruct((NH, T, 1), F32),
        compiler_params=_cparams(("parallel",)),
    )(o, do)


def _flash_bwd_dq(q, k, v, do, lse, delta, *, bq=640):
    T = q.shape[1]
    nq = T // bq
    rs = bq // CHAINS

    def body(q_ref, k_ref, v_ref, do_ref, lse_ref, dl_ref, dq_ref):
        i = pl.program_id(1)
        dq_ref[...] = jnp.zeros_like(dq_ref)

        def step(j, masked):
            off = pl.multiple_of(j * bq, bq)
            kv_ = k_ref[pl.ds(off, bq), :]
            vv = v_ref[pl.ds(off, bq), :]
            for r in range(CHAINS):
                rows = pl.ds(r * rs, rs)
                s = lax.dot_general(q_ref[rows, :], kv_, _NT, preferred_element_type=F32)
                if masked:
                    s = jnp.where(_visible(i * bq + r * rs, j * bq, rs, bq), s, NEG)
                p = jnp.exp(s - lse_ref[rows, :])
                dp = lax.dot_general(do_ref[rows, :], vv, _NT, preferred_element_type=F32)
                ds = (p * (dp - dl_ref[rows, :])).astype(BF16)
                dq_ref[rows, :] += jnp.dot(ds, kv_, preferred_element_type=F32)

        step(0, True)

        def loop(j, c):
            step(j, False)
            return c

        lax.fori_loop(1, i, loop, 0)

        @pl.when(i > 0)
        def _():
            step(i, True)

    return pl.pallas_call(
        body, name="flash_bwd_dq", grid=(NH, nq),
        in_specs=[pl.BlockSpec((None, bq, QW), lambda h, i: (h, i, 0)),
                  pl.BlockSpec((None, T, QW), lambda h, i: (h, 0, 0)),
                  pl.BlockSpec((None, T, VD), lambda h, i: (h, 0, 0)),
                  pl.BlockSpec((bq, VD), lambda h, i: (i, h)),
                  pl.BlockSpec((None, bq, 1), lambda h, i: (h, i, 0)),
                  pl.BlockSpec((None, bq, 1), lambda h, i: (h, i, 0))],
        out_specs=pl.BlockSpec((None, bq, QW), lambda h, i: (h, i, 0)),
        out_shape=jax.ShapeDtypeStruct((NH, T, QW), F32),
        compiler_params=_cparams(("parallel", "parallel")),
    )(q, k, v, do, lse, delta)


def _flash_bwd_dkv(q, k, v, do, lse_row, delta_row, *, bq=640):
    T = q.shape[1]
    nq = T // bq
    rs = bq // CHAINS

    def body(q_ref, k_ref, v_ref, do_ref, lse_ref, dl_ref, dk_ref, dv_ref):
        j = pl.program_id(1)
        dk_ref[...] = jnp.zeros_like(dk_ref)
        dv_ref[...] = jnp.zeros_like(dv_ref)

        def step(i, masked):
            off = pl.multiple_of(i * bq, bq)
            qv = q_ref[pl.ds(off, bq), :]
            dov = do_ref[pl.ds(off, bq), :]
            lse_v = lse_ref[:, pl.ds(off, bq)]
            dl_v = dl_ref[:, pl.ds(off, bq)]
            for r in range(CHAINS):
                rows = pl.ds(r * rs, rs)
                st = lax.dot_general(k_ref[rows, :], qv, _NT, preferred_element_type=F32)
                if masked:
                    st = jnp.where(_visible_t(i * bq, j * bq + r * rs, bq, rs), st, NEG)
                pt = jnp.exp(st - lse_v)
                dv_ref[rows, :] += jnp.dot(pt.astype(BF16), dov, preferred_element_type=F32)
                dpt = lax.dot_general(v_ref[rows, :], dov, _NT, preferred_element_type=F32)
                dst = (pt * (dpt - dl_v)).astype(BF16)
                dk_ref[rows, :] += jnp.dot(dst, qv, preferred_element_type=F32)

        step(j, True)

        @pl.when(j == 0)
        def _():
            def loop(i, c):
                step(i, True)
                return c
            lax.fori_loop(1, nq, loop, 0)

        @pl.when(j > 0)
        def _():
            def loop(i, c):
                step(i, False)
                return c
            lax.fori_loop(j + 1, nq, loop, 0)

    return pl.pallas_call(
        body, name="flash_bwd_dkv", grid=(NH, nq),
        in_specs=[pl.BlockSpec((None, T, QW), lambda h, j: (h, 0, 0)),
                  pl.BlockSpec((None, bq, QW), lambda h, j: (h, j, 0)),
                  pl.BlockSpec((None, bq, VD), lambda h, j: (h, j, 0)),
                  pl.BlockSpec((T, VD), lambda h, j: (0, h)),
                  pl.BlockSpec((None, 1, T), lambda h, j: (h, 0, 0)),
                  pl.BlockSpec((None, 1, T), lambda h, j: (h, 0, 0))],
        out_specs=[pl.BlockSpec((None, bq, QW), lambda h, j: (h, j, 0)),
                   pl.BlockSpec((None, bq, VD), lambda h, j: (h, j, 0))],
        out_shape=[jax.ShapeDtypeStruct((NH, T, QW), F32), jax.ShapeDtypeStruct((NH, T, VD), F32)],
        compiler_params=_cparams(("parallel", "parallel")),
    )(q, k, v, do, lse_row, delta_row)


def _rope_table(T):
    pos = (jnp.arange(T, dtype=jnp.int32) - PAD).astype(F32)
    inv_freq = ROPE_THETA ** (-jnp.arange(0, ROPE, 2, dtype=F32) / ROPE)
    ang = pos[:, None] * inv_freq[None, :]
    cos, sin = jnp.cos(ang), jnp.sin(ang)
    return jnp.concatenate([cos, cos, -sin, sin], axis=1)


def _swap_halves(w):
    return jnp.concatenate([w[..., ROPE // 2:], w[..., :ROPE // 2]], axis=-1)


O_UX, O_UG, O_UQ, O_UKV, O_UKR, O_UM = 0, DR, 2 * DR, 2 * DR + QR, 2 * DR + QR + KVR, 2 * DR + QR + KVR + ROPE


def _prep_weights(w):
    b = lambda a: a.astype(BF16)
    w_in = w["w_in"]
    kr = w_in[:, O_UKR:O_UM]
    p = {
        "w_xg": b(w_in[:, :O_UQ]),
        "w_q": b(w_in[:, O_UQ:O_UKV]),
        "w_kv": b(w_in[:, O_UKV:O_UKR]),
        "w_kr": b(jnp.concatenate([kr, _swap_halves(kr)], axis=1)),
        "w_m": b(w_in[:, O_UM:]),
    }
    wq = w["w_uq"].reshape(QR, NH, NOPE + ROPE)
    p["w_uq"] = b(jnp.concatenate([wq, _swap_halves(wq[..., NOPE:])], axis=-1).reshape(QR, NH * QW))
    p["w_ukv"] = b(w["w_ukv"])
    p["w_br"] = b(w["w_branch"][:DR])
    p["w_ba"] = b(w["w_branch"][DR:])
    p["w_out"] = b(w["w_out"])
    p["w_fi"] = b(w["w_ffn_in"])
    p["w_fo"] = b(w["w_ffn_out"])
    for n in ("w_xg", "w_q", "w_kv", "w_kr", "w_m", "w_uq", "w_ukv", "w_br", "w_ba", "w_out", "w_fi", "w_fo"):
        p[n + "_t"] = p[n].T
    p["wa"] = b(w["w_rec_a"])
    p["wi"] = b(w["w_rec_i"])
    p["wa_t"] = jnp.swapaxes(p["wa"], 1, 2)
    p["wi_t"] = jnp.swapaxes(p["wi"], 1, 2)
    return p


def _local_step(x, tgt, w):
    S = x.shape[0]
    T = FRONT + S
    p = _prep_weights(w)
    tab = _rope_table(T)
    h0 = jnp.concatenate([jnp.zeros((PAD, D), F32), w["meta_tokens"], x], axis=0)
    row = lambda v: v.reshape(1, -1)

    z = _rmsnorm_fwd(h0, row(w["norm_mix_g"]), name="norm_mix")
    uxg = _mm(z, p["w_xg"], name="mm_uxg", tn=640)
    uq = _mm(z, p["w_q"], name="mm_uq")
    ukv = _mm(z, p["w_kv"], name="mm_ukv")
    ukr = _mm(z, p["w_kr"], name="mm_ukr")
    um = _mm(z, p["w_m"], name="mm_um")
    rnn_w = (w["conv_w"], row(w["conv_b"]), p["wa"], row(w["b_rec_a"]), p["wi"], row(w["b_rec_i"]),
             row(w["lru_lambda"]))
    hs, y_rnn = _rnn_fwd(uxg, *rnn_w)
    qn = _rmsnorm_fwd(uq, row(w["q_norm_g"]), name="norm_q")
    kvn = _rmsnorm_fwd(ukv, row(w["kv_norm_g"]), name="norm_kv")
    q_all = _mm(qn, p["w_uq"], name="mm_q")
    kv_all = _mm(kvn, p["w_ukv"], name="mm_kv")
    qh, kh, vh = _attn_prep(q_all, kv_all, ukr, tab)
    y_att, lse = _flash_fwd(qh, kh, vh)
    p_rnn = _mm(y_rnn, p["w_br"], name="mm_prnn")
    p_att = _mm(y_att, p["w_ba"], name="mm_patt")
    bg = row(w["b_gate"])
    mixed = _gate_mix_fwd(um, bg, p_rnn, p_att)
    h1 = _mm(mixed, p["w_out"], name="mm_out", res=h0)
    zf = _rmsnorm_fwd(h1, row(w["norm_ffn_g"]), name="norm_ffn")
    ff = _mm(zf, p["w_fi"], name="mm_ffn_in")
    act = _swiglu_fwd(ff)
    h2 = _mm(act, p["w_fo"], name="mm_ffn_out", res=h1)

    g = {}
    dh2, dh2b, dg_fin, lsum = _loss_head(h2, tgt, row(w["final_norm_g"]))
    loss = 0.5 * jnp.sum(lsum) / D
    g["final_norm_g"] = dg_fin.reshape(-1)
    dact = _mm(dh2b, p["w_fo_t"], name="mm_dact", tn=1408)
    g["w_ffn_out"] = _mm_tn(act, dh2b, name="mm_dw_ffn_out", tk=1408)
    dff = _swiglu_bwd(ff, dact)
    dzf = _mm(dff, p["w_fi_t"], name="mm_dzf")
    g["w_ffn_in"] = _mm_tn(zf, dff, name="mm_dw_ffn_in", tn=1408)
    dh1, dh1b, dg = _rmsnorm_bwd(h1, row(w["norm_ffn_g"]), dzf, dh2, name="norm_ffn_bwd")
    g["norm_ffn_g"] = dg
    dmixed = _mm(dh1b, p["w_out_t"], name="mm_dmixed")
    g["w_out"] = _mm_tn(mixed, dh1b, name="mm_dw_out")
    dp_rnn, dp_att, dum, dbg = _gate_mix_bwd(um, bg, p_rnn, p_att, dmixed)
    g["b_gate"] = dbg.reshape(2, D)
    dy_rnn = _mm(dp_rnn, p["w_br_t"], name="mm_dy_rnn", tn=640)
    dy_att = _mm(dp_att, p["w_ba_t"], name="mm_dy_att", out_dtype=BF16)
    g["w_branch"] = jnp.concatenate([_mm_tn(y_rnn, dp_rnn, name="mm_dw_br", tk=640),
                                     _mm_tn(y_att, dp_att, name="mm_dw_ba")], axis=0)
    delta = _attn_delta(y_att, dy_att)
    dq = _flash_bwd_dq(qh, kh, vh, dy_att, lse, delta)
    dk, dv = _flash_bwd_dkv(qh, kh, vh, dy_att, lse.reshape(NH, 1, T), delta.reshape(NH, 1, T))
    dq_all, dkv_all, dukr = _attn_prep_bwd(dq, dk, dv, tab)
    dqn = _mm(dq_all, p["w_uq_t"], name="mm_dqn")
    dkvn = _mm(dkv_all, p["w_ukv_t"], name="mm_dkvn")
    dwq = _mm_tn(qn, dq_all, name="mm_dw_uq", tk=QR).reshape(QR, NH, QW)
    dwq_rope = dwq[..., NOPE:NOPE + ROPE] + _swap_halves(dwq[..., NOPE + ROPE:])
    g["w_uq"] = jnp.concatenate([dwq[..., :NOPE], dwq_rope], axis=-1).reshape(QR, NH * (NOPE + ROPE))
    g["w_ukv"] = _mm_tn(kvn, dkv_all, name="mm_dw_ukv", tk=KVR)
    duq, dg = _rmsnorm_bwd(uq, row(w["q_norm_g"]), dqn, None, name="norm_q_bwd", want_f32=False)
    g["q_norm_g"] = dg
    dukv, dg = _rmsnorm_bwd(ukv, row(w["kv_norm_g"]), dkvn, None, name="norm_kv_bwd", want_f32=False)
    g["kv_norm_g"] = dg
    (dux, dug, g["conv_w"], g["conv_b"], g["w_rec_a"], g["b_rec_a"], g["w_rec_i"], g["b_rec_i"],
     g["lru_lambda"]) = _rnn_bwd(uxg, hs, dy_rnn, *rnn_w, p["wa_t"], p["wi_t"])
    dz = _mm(dux, p["w_xg_t"][:DR], name="mm_dz_x")
    dz = _mm(dug, p["w_xg_t"][DR:], name="mm_dz_g", res=dz)
    dz = _mm(duq, p["w_q_t"], name="mm_dz_q", res=dz)
    dz = _mm(dukv, p["w_kv_t"], name="mm_dz_kv", res=dz)
    dz = _mm(dukr, p["w_kr_t"], name="mm_dz_kr", res=dz)
    dz = _mm(dum, p["w_m_t"], name="mm_dz_m", res=dz)
    dwkr = _mm_tn(z, dukr, name="mm_dw_kr")
    g["w_in"] = jnp.concatenate([
        _mm_tn(z, dux, name="mm_dw_x", tn=640), _mm_tn(z, dug, name="mm_dw_g", tn=640),
        _mm_tn(z, duq, name="mm_dw_q"), _mm_tn(z, dukv, name="mm_dw_kv"),
        dwkr[:, :ROPE] + _swap_halves(dwkr[:, ROPE:]),
        _mm_tn(z, dum, name="mm_dw_m")], axis=1)
    dh0, dg = _rmsnorm_bwd(h0, row(w["norm_mix_g"]), dz, dh1, name="norm_mix_bwd", want_bf16=False)
    g["norm_mix_g"] = dg
    g["meta_tokens"] = dh0[PAD:FRONT]
    return loss, dh0[FRONT:], g


HBM = pl.BlockSpec(memory_space=pltpu.HBM)
CHIP_FLIPS = ((1, 0), (0, 1), (1, 1))


def _place():
    return lax.axis_index("x"), lax.axis_index("y"), lax.axis_index("c")


def _flip(v, f):
    return 1 - v if f else v


def _dma_sems(n):
    return pltpu.SemaphoreType.DMA((n,))


def _allgather_chips(srcs, *, name):
    n = len(srcs)

    def body(*refs):
        src_refs, out_refs = refs[:n], refs[n:2 * n]
        send_sems, recv_sems, local_sems = refs[2 * n:]
        x, y, c = _place()
        me = 2 * x + y
        copies = []
        for a in range(n):
            cp = pltpu.make_async_copy(src_refs[a], out_refs[a].at[me], local_sems.at[a])
            cp.start()
            copies.append(cp)
            for k, (fx, fy) in enumerate(CHIP_FLIPS):
                cp = pltpu.make_async_remote_copy(
                    src_ref=src_refs[a], dst_ref=out_refs[a].at[me], send_sem=send_sems.at[3 * a + k],
                    recv_sem=recv_sems.at[3 * a + k], device_id=(_flip(x, fx), _flip(y, fy), c),
                    device_id_type=MESH)
                cp.start()
                copies.append(cp)
        for cp in copies:
            cp.wait()

    return pl.pallas_call(
        body, name=name, in_specs=[HBM] * n, out_specs=[HBM] * n,
        out_shape=[jax.ShapeDtypeStruct((4,) + s.shape, s.dtype) for s in srcs],
        scratch_shapes=[_dma_sems(3 * n), _dma_sems(3 * n), _dma_sems(n)],
    )(*srcs)


def _scatter_chips(srcs, *, name):
    n = len(srcs)

    def body(*refs):
        src_refs, out_refs = refs[:n], refs[n:2 * n]
        send_sems, recv_sems, local_sems = refs[2 * n:]
        x, y, c = _place()
        me = 2 * x + y
        copies = []
        for a in range(n):
            cp = pltpu.make_async_copy(src_refs[a].at[me], out_refs[a].at[me], local_sems.at[a])
            cp.start()
            copies.append(cp)
            for k, (fx, fy) in enumerate(CHIP_FLIPS):
                px, py = _flip(x, fx), _flip(y, fy)
                cp = pltpu.make_async_remote_copy(
                    src_ref=src_refs[a].at[2 * px + py], dst_ref=out_refs[a].at[me],
                    send_sem=send_sems.at[3 * a + k], recv_sem=recv_sems.at[3 * a + k],
                    device_id=(px, py, c), device_id_type=MESH)
                cp.start()
                copies.append(cp)
        for cp in copies:
            cp.wait()

    return pl.pallas_call(
        body, name=name, in_specs=[HBM] * n, out_specs=[HBM] * n,
        out_shape=[jax.ShapeDtypeStruct(s.shape, s.dtype) for s in srcs],
        scratch_shapes=[_dma_sems(3 * n), _dma_sems(3 * n), _dma_sems(n)],
    )(*srcs)


def _sibling_take(srcs, *, name):
    n = len(srcs)

    def body(*refs):
        src_refs, out_refs = refs[:n], refs[n:2 * n]
        send_sems, recv_sems = refs[2 * n:]
        x, y, c = _place()
        copies = []
        for a in range(n):
            h = srcs[a].shape[1] // 2
            theirs = pl.ds(pl.multiple_of((1 - c) * h, 8), h)
            cp = pltpu.make_async_remote_copy(
                src_ref=src_refs[a].at[:, theirs, :], dst_ref=out_refs[a], send_sem=send_sems.at[a],
                recv_sem=recv_sems.at[a], device_id=(x, y, 1 - c), device_id_type=MESH)
            cp.start()
            copies.append(cp)
        for cp in copies:
            cp.wait()

    return pl.pallas_call(
        body, name=name, in_specs=[HBM] * n, out_specs=[HBM] * n,
        out_shape=[jax.ShapeDtypeStruct((4, s.shape[1] // 2, s.shape[2]), s.dtype) for s in srcs],
        scratch_shapes=[_dma_sems(n), _dma_sems(n)],
    )(*srcs)


def _sibling_pair(srcs, *, name):
    n = len(srcs)

    def body(*refs):
        src_refs, out_refs = refs[:n], refs[n:2 * n]
        send_sems, recv_sems, local_sems = refs[2 * n:]
        x, y, c = _place()
        copies = []
        for a in range(n):
            cp = pltpu.make_async_copy(src_refs[a], out_refs[a].at[c], local_sems.at[a])
            cp.start()
            copies.append(cp)
            cp = pltpu.make_async_remote_copy(
                src_ref=src_refs[a], dst_ref=out_refs[a].at[c], send_sem=send_sems.at[a],
                recv_sem=recv_sems.at[a], device_id=(x, y, 1 - c), device_id_type=MESH)
            cp.start()
            copies.append(cp)
        for cp in copies:
            cp.wait()

    return pl.pallas_call(
        body, name=name, in_specs=[HBM] * n, out_specs=[HBM] * n,
        out_shape=[jax.ShapeDtypeStruct((2,) + s.shape, s.dtype) for s in srcs],
        scratch_shapes=[_dma_sems(n), _dma_sems(n), _dma_sems(n)],
    )(*srcs)


def _row_tile(rows, cols, n_arrays):
    budget = 24 * 1024 * 1024 // (2 * 4 * n_arrays * cols)
    best = 16
    for t in range(16, rows + 1, 16):
        if rows % t == 0 and t <= budget:
            best = t
    assert rows % best == 0, (rows, cols)
    return best


def _add_halves(mine, theirs, wire, *, name):
    _, h, c = mine.shape
    tm = _row_tile(h, c, 3)
    spec = pl.BlockSpec((None, tm, c), lambda s, i: (s, i, 0))

    def body(a_ref, b_ref, o_ref):
        o_ref[...] = (a_ref[...] + b_ref[...]).astype(wire)

    return pl.pallas_call(
        body, name=name, grid=(4, h // tm), in_specs=[spec, spec], out_specs=spec,
        out_shape=jax.ShapeDtypeStruct(mine.shape, wire), compiler_params=_cparams(("parallel", "parallel")),
    )(mine, theirs)


def _sum4(a, *, name):
    _, h, c = a.shape
    tm = _row_tile(h, c, 5)

    def body(a_ref, o_ref):
        f = lambda s: a_ref[s].astype(F32)
        o_ref[...] = ((f(0) + f(1)) + f(2)) + f(3)

    return pl.pallas_call(
        body, name=name, grid=(h // tm,), in_specs=[pl.BlockSpec((4, tm, c), lambda i: (0, i, 0))],
        out_specs=_rows(tm, c), out_shape=jax.ShapeDtypeStruct((h, c), F32),
        compiler_params=_cparams(("parallel",)),
    )(a)


def _adamw(g, w, m, v, *, name):
    r, c = g.shape
    tm = _row_tile(r, c, 7)
    c1 = 1.0 / (1.0 - ADAM_B1 ** ADAM_STEP)
    c2 = 1.0 / (1.0 - ADAM_B2 ** ADAM_STEP)

    def body(g_ref, w_ref, m_ref, v_ref, d_ref, nm_ref, nv_ref):
        gv = g_ref[...]
        nm = ADAM_B1 * m_ref[...] + (1.0 - ADAM_B1) * gv
        nv = ADAM_B2 * v_ref[...] + (1.0 - ADAM_B2) * (gv * gv)
        nm_ref[...] = nm
        nv_ref[...] = nv
        d_ref[...] = -ADAM_LR * ((nm * c1) / (jnp.sqrt(nv * c2) + ADAM_EPS) + ADAM_WD * w_ref[...])

    spec = _rows(tm, c)
    shape = jax.ShapeDtypeStruct((r, c), F32)
    return pl.pallas_call(
        body, name=name, grid=(r // tm,), in_specs=[spec] * 4, out_specs=[spec] * 3,
        out_shape=[shape] * 3, compiler_params=_cparams(("parallel",)),
    )(g, w, m, v)


BIG = (("w_in", (D, 1328), 1), ("w_uq", (QR, 384), 1), ("w_ukv", (KVR, 512), 1), ("w_branch", (576, D), 0),
       ("w_out", (256, D), 0), ("w_ffn_in", (D, 1408), 1), ("w_ffn_out", (704, D), 0))
SMALL = (("meta_tokens", (NMETA, 256), 1), ("b_gate", (2, 256), 1), ("conv_w", (CW, 320), 1))
REPL = (("norm_mix_g", (D,)), ("conv_b", (DR,)), ("w_rec_a", (NBLK, RB, RB)), ("b_rec_a", (DR,)),
        ("w_rec_i", (NBLK, RB, RB)), ("b_rec_i", (DR,)), ("lru_lambda", (DR,)), ("q_norm_g", (QR,)),
        ("kv_norm_g", (KVR,)), ("norm_ffn_g", (D,)), ("final_norm_g", (D,)))
WEIGHTS = ("meta_tokens", "norm_mix_g", "w_in", "b_gate", "conv_w", "conv_b", "w_rec_a", "b_rec_a", "w_rec_i",
           "b_rec_i", "lru_lambda", "q_norm_g", "w_uq", "kv_norm_g", "w_ukv", "w_branch", "w_out", "norm_ffn_g",
           "w_ffn_in", "w_ffn_out", "final_norm_g")
W = 1024
SMALL_N = sum(math.prod(s) for _, s, _ in SMALL)
SMALL_ROWS = 8
REPL_N = sum(math.prod(s) for _, s in REPL)
QUART_ROWS = 88
assert SMALL_N <= SMALL_ROWS * W and REPL_N <= 4 * QUART_ROWS * W


def _flat_pad(parts, rows):
    v = jnp.concatenate([p.reshape(-1) for p in parts])
    return jnp.pad(v, (0, rows * W - v.shape[0])).reshape(rows, W)


def _shard_stack(full, shard_shape, axis):
    r, cs = shard_shape
    if axis == 0:
        return full.reshape(4, r, cs)
    return jnp.stack([full[:, s * cs:(s + 1) * cs] for s in range(4)])


def _unshard(stack, axis):
    if axis == 0:
        return stack.reshape(4 * stack.shape[1], stack.shape[2])
    return jnp.concatenate([stack[s] for s in range(4)], axis=1)


def _split(flat, table):
    out, off = {}, 0
    for name, shape, *_ in table:
        n = math.prod(shape)
        out[name] = flat[..., off:off + n].reshape(flat.shape[:-1] + tuple(shape))
        off += n
    return out


def _misc_state(args, prefix, quarter):
    small = _flat_pad([args[prefix + n] for n, _, _ in SMALL], SMALL_ROWS)
    repl = _flat_pad([args[prefix + n] for n, _ in REPL], 4 * QUART_ROWS)
    mine = lax.dynamic_slice_in_dim(repl, quarter * QUART_ROWS, QUART_ROWS, axis=0)
    return jnp.concatenate([small, mine], axis=0)


def kernel(x, meta_tokens, norm_mix_g, w_in, b_gate, conv_w, conv_b, w_rec_a, b_rec_a, w_rec_i, b_rec_i, lru_lambda, q_norm_g, w_uq, kv_norm_g, w_ukv, w_branch, w_out, norm_ffn_g, w_ffn_in, w_ffn_out, final_norm_g, loss_target, m_meta_tokens, m_norm_mix_g, m_w_in, m_b_gate, m_conv_w, m_conv_b, m_w_rec_a, m_b_rec_a, m_w_rec_i, m_b_rec_i, m_lru_lambda, m_q_norm_g, m_w_uq, m_kv_norm_g, m_w_ukv, m_w_branch, m_w_out, m_norm_ffn_g, m_w_ffn_in, m_w_ffn_out, m_final_norm_g, v_meta_tokens, v_norm_mix_g, v_w_in, v_b_gate, v_conv_w, v_conv_b, v_w_rec_a, v_b_rec_a, v_w_rec_i, v_b_rec_i, v_lru_lambda, v_q_norm_g, v_w_uq, v_kv_norm_g, v_w_ukv, v_w_branch, v_w_out, v_norm_ffn_g, v_w_ffn_in, v_w_ffn_out, v_final_norm_g):
    args = dict(locals())
    chip = 2 * lax.axis_index("x") + lax.axis_index("y")
    core = lax.axis_index("c")

    shards = [args[n].reshape(s).astype(BF16) for n, s, _ in BIG]
    small = _flat_pad([args[n] for n, _, _ in SMALL], SMALL_ROWS)
    gathered = _allgather_chips(shards + [small], name="gather_weights")
    w = {}
    for (name, _, axis), stack in zip(BIG, gathered):
        w[name] = _unshard(stack, axis)
    small_parts = _split(gathered[-1].reshape(4, SMALL_ROWS * W), SMALL)
    for name, _, axis in SMALL:
        w[name] = _unshard(small_parts[name], axis)
    for name, shape in REPL:
        w[name] = args[name].reshape(shape)

    loss, grad_x, g = _local_step(x[0], loss_target[0], w)
    loss = lax.psum(loss, ("x", "y", "c"))

    red = [_shard_stack(g[n], s, a) for n, s, a in BIG]
    small_g = jnp.concatenate([_shard_stack(g[n], s, a).reshape(4, -1) for n, s, a in SMALL], axis=1)
    small_g = jnp.pad(small_g, ((0, 0), (0, SMALL_ROWS * W - SMALL_N))).reshape(4, SMALL_ROWS, W)
    repl_g = _flat_pad([g[n] for n, _ in REPL], 4 * QUART_ROWS).reshape(4, QUART_ROWS, W)
    red.append(jnp.concatenate([small_g, repl_g], axis=1))

    theirs = _sibling_take(red, name="reduce_sibling")
    parts = []
    for k, (a, t) in enumerate(zip(red, theirs)):
        h = a.shape[1] // 2
        mine = lax.dynamic_slice_in_dim(a, core * h, h, axis=1)
        wire = F32 if k == len(red) - 1 else BF16
        parts.append(_add_halves(mine, t, wire, name=f"add_sibling_{k}"))
    parts4 = _scatter_chips(parts, name="reduce_chips")
    halves = [_sum4(p, name=f"sum_chips_{k}") for k, p in enumerate(parts4)]
    pairs = _sibling_pair(halves, name="share_sibling")
    gred = [p.reshape(2 * p.shape[1], p.shape[2]) for p in pairs]

    results = {}
    for (name, shape, _), gr in zip(BIG, gred):
        d, nm, nv = _adamw(gr, args[name].reshape(shape), args["m_" + name].reshape(shape),
                           args["v_" + name].reshape(shape), name="adamw_" + name)
        results[name] = (gr, d, nm, nv)
    d, nm, nv = _adamw(gred[-1], _misc_state(args, "", chip), _misc_state(args, "m_", chip),
                       _misc_state(args, "v_", chip), name="adamw_misc")
    misc = (gred[-1], d, nm, nv)

    quarters = jnp.stack([t[SMALL_ROWS:] for t in misc])
    repl_all = _allgather_chips([quarters], name="gather_repl")[0]
    repl_all = repl_all.transpose(1, 0, 2, 3).reshape(4, 4 * QUART_ROWS * W)

    outs = []
    for k in range(4):
        sm = _split(misc[k][:SMALL_ROWS].reshape(-1), SMALL)
        rp = _split(repl_all[k], REPL)
        for name in WEIGHTS:
            val = results[name][k] if name in results else (sm[name] if name in sm else rp[name])
            outs.append(val.reshape(args[name].shape))
    return (loss, grad_x[None], *outs)
```

```python
import functools
import math

import jax
import jax.numpy as jnp
from jax import lax
from jax.experimental import pallas as pl
from jax.experimental.pallas import tpu as pltpu

F32 = jnp.float32
BF16 = jnp.bfloat16

D = 1024
DR = 1280
NBLK = 10
RB = 128
CW = 4
NH = 8
NOPE = 128
ROPE = 64
VD = 128
QR = 384
KVR = 256
DFF = 2816
NMETA = 16
EPS = 1e-6
LRU_C = 8.0
ROPE_THETA = 10000.0
SCALE = 1.0 / math.sqrt(NOPE + ROPE)
NEG = -1e30
FRONT = 128
PAD = FRONT - NMETA
QW = 2 * NOPE
LANES = 128
SUB = 128
CHAINS = 2
VMEM_LIMIT = 52 * 1024 * 1024

ADAM_LR = 0.001
ADAM_B1 = 0.9
ADAM_B2 = 0.999
ADAM_EPS = 1e-08
ADAM_WD = 0.01
ADAM_STEP = 10

MESH = pl.DeviceIdType.MESH


def _cparams(sem):
    return pltpu.CompilerParams(dimension_semantics=sem, vmem_limit_bytes=VMEM_LIMIT)


def _sigmoid(x):
    return 1.0 / (1.0 + jnp.exp(-x))


def _gelu_parts(x):
    c = math.sqrt(2.0 / math.pi)
    inner = c * (x + 0.044715 * x * x * x)
    t = jnp.tanh(inner)
    g = 0.5 * x * (1.0 + t)
    dg = 0.5 * (1.0 + t) + 0.5 * x * (1.0 - t * t) * c * (1.0 + 3.0 * 0.044715 * x * x)
    return g, dg


def _divisors(n, step, cap):
    return [d for d in range(step, min(n, cap) + 1, step) if n % d == 0] or [n]


MM_VMEM_BUDGET = 40 * 1024 * 1024
MM_MAX_ROWS = 1664
MM_MAX_COLS = 1408


def _mm_tiles(M, K, N, a_item, out_item, has_res):
    best = None
    for tn in _divisors(N, LANES, MM_MAX_COLS):
        for tm in _divisors(M, 16, MM_MAX_ROWS):
            need = 2 * (tm * K * a_item + K * tn * 2 + tm * tn * (out_item + (4 if has_res else 0)))
            if need <= MM_VMEM_BUDGET and (best is None or tm * tn > best[0] * best[1]):
                best = (tm, tn)
    assert best is not None, (M, K, N)
    return best


def _mm(a, b, *, name, out_dtype=F32, res=None):
    M, K = a.shape
    N = b.shape[1]
    has_res = res is not None
    tm, tn = _mm_tiles(M, K, N, a.dtype.itemsize, jnp.dtype(out_dtype).itemsize, has_res)

    def body(*refs):
        if has_res:
            a_ref, b_ref, r_ref, o_ref = refs
        else:
            a_ref, b_ref, o_ref = refs
        acc = jnp.dot(a_ref[...].astype(BF16), b_ref[...].astype(BF16), preferred_element_type=F32)
        if has_res:
            acc = acc + r_ref[...].astype(F32)
        o_ref[...] = acc.astype(o_ref.dtype)

    a_bytes = M * K * a.dtype.itemsize
    b_bytes = K * N * b.dtype.itemsize
    rows_outer = a_bytes + (M // tm) * b_bytes <= b_bytes + (N // tn) * a_bytes
    if rows_outer:
        grid = (M // tm, N // tn)
        ia, ib, io = (lambda i, j: (i, 0)), (lambda i, j: (0, j)), (lambda i, j: (i, j))
    else:
        grid = (N // tn, M // tm)
        ia, ib, io = (lambda j, i: (i, 0)), (lambda j, i: (0, j)), (lambda j, i: (i, j))
    in_specs = [pl.BlockSpec((tm, K), ia), pl.BlockSpec((K, tn), ib)]
    args = [a, b]
    if has_res:
        in_specs.append(pl.BlockSpec((tm, tn), io))
        args.append(res)
    return pl.pallas_call(
        body, name=name, grid=grid, in_specs=in_specs,
        out_specs=pl.BlockSpec((tm, tn), io),
        out_shape=jax.ShapeDtypeStruct((M, N), out_dtype),
        compiler_params=_cparams(("parallel", "parallel")),
    )(*args)


def _mm_tn(a, b, *, name):
    T, K1 = a.shape
    N = b.shape[1]
    tt = _divisors(T, 16, MM_MAX_ROWS)[-1]
    tk = _divisors(K1, LANES, MM_MAX_COLS)[-1]
    tn = _divisors(N, LANES, MM_MAX_COLS)[-1]

    def body(a_ref, b_ref, o_ref):
        @pl.when(pl.program_id(2) == 0)
        def _():
            o_ref[...] = jnp.zeros_like(o_ref)

        o_ref[...] += lax.dot_general(a_ref[...].astype(BF16), b_ref[...].astype(BF16),
                                      (((0,), (0,)), ((), ())), preferred_element_type=F32)

    return pl.pallas_call(
        body, name=name, grid=(K1 // tk, N // tn, T // tt),
        in_specs=[pl.BlockSpec((tt, tk), lambda i, j, t: (t, i)),
                  pl.BlockSpec((tt, tn), lambda i, j, t: (t, j))],
        out_specs=pl.BlockSpec((tk, tn), lambda i, j, t: (i, j)),
        out_shape=jax.ShapeDtypeStruct((K1, N), F32),
        compiler_params=_cparams(("parallel", "parallel", "arbitrary")),
    )(a, b)


def _rows(tm, w, cb=0):
    return pl.BlockSpec((tm, w), lambda i: (i, cb))


def _const(shape):
    n = len(shape)
    return pl.BlockSpec(shape, lambda i: (0,) * n)


def _rmsnorm_fwd(x, g, *, name, tm=640):
    T, C = x.shape

    def body(x_ref, g_ref, o_ref):
        xv = x_ref[...]
        r = lax.rsqrt(jnp.mean(xv * xv, axis=-1, keepdims=True) + EPS)
        o_ref[...] = ((xv * r) * g_ref[...]).astype(BF16)

    return pl.pallas_call(
        body, name=name, grid=(T // tm,),
        in_specs=[_rows(tm, C), _const((1, C))],
        out_specs=_rows(tm, C),
        out_shape=jax.ShapeDtypeStruct((T, C), BF16),
        compiler_params=_cparams(("parallel",)),
    )(x, g)


def _rmsnorm_bwd(x, g, dy, res, *, name, tm=640, want_f32=True, want_bf16=True):
    T, C = x.shape
    has_res = res is not None

    def body(*refs):
        refs = list(refs)
        x_ref, g_ref, dy_ref = refs[:3]
        refs = refs[3:]
        r_ref = refs.pop(0) if has_res else None
        o32 = refs.pop(0) if want_f32 else None
        o16 = refs.pop(0) if want_bf16 else None
        dg_ref = refs.pop(0)

        @pl.when(pl.program_id(0) == 0)
        def _():
            dg_ref[...] = jnp.zeros_like(dg_ref)

        xv = x_ref[...]
        dyv = dy_ref[...].astype(F32)
        r = lax.rsqrt(jnp.mean(xv * xv, axis=-1, keepdims=True) + EPS)
        xn = xv * r
        dg_ref[...] += jnp.sum(dyv * xn, axis=0, keepdims=True)
        dxn = dyv * g_ref[...]
        dx = r * (dxn - xn * jnp.mean(dxn * xn, axis=-1, keepdims=True))
        if has_res:
            dx = dx + r_ref[...]
        if want_f32:
            o32[...] = dx
        if want_bf16:
            o16[...] = dx.astype(BF16)

    in_specs = [_rows(tm, C), _const((1, C)), _rows(tm, C)]
    args = [x, g, dy]
    if has_res:
        in_specs.append(_rows(tm, C))
        args.append(res)
    out_specs, out_shape = [], []
    if want_f32:
        out_specs.append(_rows(tm, C))
        out_shape.append(jax.ShapeDtypeStruct((T, C), F32))
    if want_bf16:
        out_specs.append(_rows(tm, C))
        out_shape.append(jax.ShapeDtypeStruct((T, C), BF16))
    out_specs.append(_const((1, C)))
    out_shape.append(jax.ShapeDtypeStruct((1, C), F32))
    return pl.pallas_call(
        body, name=name, grid=(T // tm,), in_specs=in_specs, out_specs=out_specs,
        out_shape=out_shape, compiler_params=_cparams(("arbitrary",)),
    )(*args)


def _gate_mix_fwd(um, bg, p_rnn, p_att, *, tm=320):
    T = um.shape[0]

    def body(um_ref, bg_ref, pr_ref, pa_ref, o_ref):
        g = _sigmoid(um_ref[...] + bg_ref[...])
        o_ref[...] = (g[:, :D] * pr_ref[...] + g[:, D:] * pa_ref[...]).astype(BF16)

    return pl.pallas_call(
        body, name="gate_mix_fwd", grid=(T // tm,),
        in_specs=[_rows(tm, 2 * D), _const((1, 2 * D)), _rows(tm, D), _rows(tm, D)],
        out_specs=_rows(tm, D),
        out_shape=jax.ShapeDtypeStruct((T, D), BF16),
        compiler_params=_cparams(("parallel",)),
    )(um, bg, p_rnn, p_att)


def _gate_mix_bwd(um, bg, p_rnn, p_att, dmixed, *, tm=320):
    T = um.shape[0]

    def body(um_ref, bg_ref, pr_ref, pa_ref, dm_ref, dpr_ref, dpa_ref, dum_ref, dbg_ref):
        @pl.when(pl.program_id(0) == 0)
        def _():
            dbg_ref[...] = jnp.zeros_like(dbg_ref)

        g = _sigmoid(um_ref[...] + bg_ref[...])
        g0, g1 = g[:, :D], g[:, D:]
        dm = dm_ref[...]
        dpr_ref[...] = (dm * g0).astype(BF16)
        dpa_ref[...] = (dm * g1).astype(BF16)
        d0 = dm * pr_ref[...] * g0 * (1.0 - g0)
        d1 = dm * pa_ref[...] * g1 * (1.0 - g1)
        dum_ref[:, :D] = d0.astype(BF16)
        dum_ref[:, D:] = d1.astype(BF16)
        dbg_ref[:, :D] += jnp.sum(d0, axis=0, keepdims=True)
        dbg_ref[:, D:] += jnp.sum(d1, axis=0, keepdims=True)

    return pl.pallas_call(
        body, name="gate_mix_bwd", grid=(T // tm,),
        in_specs=[_rows(tm, 2 * D), _const((1, 2 * D)), _rows(tm, D), _rows(tm, D), _rows(tm, D)],
        out_specs=[_rows(tm, D), _rows(tm, D), _rows(tm, 2 * D), _const((1, 2 * D))],
        out_shape=[jax.ShapeDtypeStruct((T, D), BF16), jax.ShapeDtypeStruct((T, D), BF16),
                   jax.ShapeDtypeStruct((T, 2 * D), BF16), jax.ShapeDtypeStruct((1, 2 * D), F32)],
        compiler_params=_cparams(("arbitrary",)),
    )(um, bg, p_rnn, p_att, dmixed)


def _swiglu_fwd(ff, *, tm=320):
    T = ff.shape[0]

    def body(g_ref, u_ref, o_ref):
        gv = g_ref[...]
        o_ref[...] = (gv * _sigmoid(gv) * u_ref[...]).astype(BF16)

    return pl.pallas_call(
        body, name="swiglu_fwd", grid=(T // tm,),
        in_specs=[_rows(tm, DFF, 0), _rows(tm, DFF, 1)],
        out_specs=_rows(tm, DFF),
        out_shape=jax.ShapeDtypeStruct((T, DFF), BF16),
        compiler_params=_cparams(("parallel",)),
    )(ff, ff)


def _swiglu_bwd(ff, dact, *, tm=320):
    T = ff.shape[0]

    def body(g_ref, u_ref, da_ref, o_ref):
        gv = g_ref[...]
        s = _sigmoid(gv)
        da = da_ref[...]
        o_ref[:, :DFF] = (da * u_ref[...] * s * (1.0 + gv * (1.0 - s))).astype(BF16)
        o_ref[:, DFF:] = (da * gv * s).astype(BF16)

    return pl.pallas_call(
        body, name="swiglu_bwd", grid=(T // tm,),
        in_specs=[_rows(tm, DFF, 0), _rows(tm, DFF, 1), _rows(tm, DFF)],
        out_specs=_rows(tm, 2 * DFF),
        out_shape=jax.ShapeDtypeStruct((T, 2 * DFF), BF16),
        compiler_params=_cparams(("parallel",)),
    )(ff, ff, dact)


def _loss_head(h2, tgt, g, *, tm=FRONT):
    T = h2.shape[0]
    front_blocks = FRONT // tm

    def body(h_ref, t_ref, g_ref, d32_ref, d16_ref, dg_ref, ls_ref):
        i = pl.program_id(0)

        @pl.when(i == 0)
        def _():
            dg_ref[...] = jnp.zeros_like(dg_ref)
            ls_ref[...] = jnp.zeros_like(ls_ref)

        xv = h_ref[...]
        r = lax.rsqrt(jnp.mean(xv * xv, axis=-1, keepdims=True) + EPS)
        xn = xv * r
        gv = g_ref[...]
        e = jnp.where(i >= front_blocks, xn * gv - t_ref[...], 0.0)
        ls_ref[...] += jnp.sum(e * e, axis=0, keepdims=True)
        dy = e * (1.0 / D)
        dg_ref[...] += jnp.sum(dy * xn, axis=0, keepdims=True)
        dxn = dy * gv
        dx = r * (dxn - xn * jnp.mean(dxn * xn, axis=-1, keepdims=True))
        d32_ref[...] = dx
        d16_ref[...] = dx.astype(BF16)

    return pl.pallas_call(
        body, name="loss_head", grid=(T // tm,),
        in_specs=[_rows(tm, D), pl.BlockSpec((tm, D), lambda i: (jnp.maximum(i - front_blocks, 0), 0)),
                  _const((1, D))],
        out_specs=[_rows(tm, D), _rows(tm, D), _const((1, D)), _const((1, D))],
        out_shape=[jax.ShapeDtypeStruct((T, D), F32), jax.ShapeDtypeStruct((T, D), BF16),
                   jax.ShapeDtypeStruct((1, D), F32), jax.ShapeDtypeStruct((1, D), F32)],
        compiler_params=_cparams(("arbitrary",)),
    )(h2, tgt, g)


def _scan_fwd(a, b, h_in):
    n = a.shape[0]
    row = lax.broadcasted_iota(jnp.int32, a.shape, 0)
    s = 1
    while s < n:
        a_sh = jnp.where(row >= s, pltpu.roll(a, s, 0), 1.0)
        b_sh = jnp.where(row >= s, pltpu.roll(b, s, 0), 0.0)
        b = a * b_sh + b
        a = a * a_sh
        s *= 2
    return b + a * h_in


def _scan_rev(a, b, g_in):
    n = a.shape[0]
    row = lax.broadcasted_iota(jnp.int32, a.shape, 0)
    s = 1
    while s < n:
        a_sh = jnp.where(row < n - s, pltpu.roll(a, n - s, 0), 1.0)
        b_sh = jnp.where(row < n - s, pltpu.roll(b, n - s, 0), 0.0)
        b = a * b_sh + b
        a = a * a_sh
        s *= 2
    return b + a * g_in


def _lru_gates(xc, wa, ba, wi, bi, lam):
    xcb = xc.astype(BF16)
    r = _sigmoid(jnp.dot(xcb, wa, preferred_element_type=F32) + ba)
    ig = _sigmoid(jnp.dot(xcb, wi, preferred_element_type=F32) + bi)
    log_sig = jnp.minimum(lam, 0.0) - jnp.log(1.0 + jnp.exp(-jnp.abs(lam)))
    log_a = LRU_C * r * log_sig
    a = jnp.exp(log_a)
    z = 2.0 * log_a
    poly = -z * (1.0 + z * (0.5 + z * (1.0 / 6.0 + z * (1.0 / 24.0))))
    m2 = jnp.where(z > -0.03, poly, 1.0 - jnp.exp(z))
    return r, ig, log_sig, a, jnp.sqrt(m2)


def _rnn_specs(tc, nblk_t, rev):
    def tmap(k):
        return (nblk_t - 1 - k) if rev else k

    hb = tc // 8
    blk = lambda off: pl.BlockSpec((tc, RB), lambda c, k: (tmap(k), c + off))
    halo = lambda off: pl.BlockSpec((8, RB), lambda c, k: (jnp.maximum(tmap(k) * hb - 1, 0), c + off))
    vec = pl.BlockSpec((1, RB), lambda c, k: (0, c))
    cwv = pl.BlockSpec((CW, RB), lambda c, k: (0, c))
    mat = pl.BlockSpec((None, RB, RB), lambda c, k: (c, 0, 0))
    return blk, halo, vec, cwv, mat


def _rnn_fwd(uxg, cw, cb, wa, ba, wi, bi, lam, *, tc=640):
    T = uxg.shape[0]
    nt = T // tc
    nsub = tc // SUB
    blk, halo, vec, cwv, mat = _rnn_specs(tc, nt, False)

    def body(x_ref, xh_ref, ug_ref, cw_ref, cb_ref, wa_ref, ba_ref, wi_ref, bi_ref, lam_ref,
             h_ref, y_ref, xb, hc):
        k = pl.program_id(1)

        @pl.when(k == 0)
        def _():
            hc[...] = jnp.zeros_like(hc)

        xb[0:8, :] = jnp.where(k > 0, xh_ref[...], 0.0)
        xb[8:, :] = x_ref[...]
        cwv_, cbv = cw_ref[...], cb_ref[...]
        wav, wiv = wa_ref[...], wi_ref[...]
        bav, biv, lamv = ba_ref[...], bi_ref[...], lam_ref[...]
        h_in = hc[0:1, :]
        for sc in range(nsub):
            r0 = sc * SUB
            xc = cbv + cwv_[0:1, :] * xb[pl.ds(5 + r0, SUB), :]
            for j in range(1, CW):
                xc = xc + cwv_[j:j + 1, :] * xb[pl.ds(5 + j + r0, SUB), :]
            r, ig, _, a, mm = _lru_gates(xc, wav, bav, wiv, biv, lamv)
            rows = k * tc + r0 + lax.broadcasted_iota(jnp.int32, (SUB, RB), 0)
            b = jnp.where(rows >= PAD, mm * (ig * xc), 0.0)
            h = _scan_fwd(a, b, h_in)
            h_in = h[SUB - 1:SUB, :]
            h_ref[pl.ds(r0, SUB), :] = h
            gl, _ = _gelu_parts(ug_ref[pl.ds(r0, SUB), :])
            y_ref[pl.ds(r0, SUB), :] = (h * gl).astype(BF16)
        hc[0:1, :] = h_in

    return pl.pallas_call(
        body, name="rnn_fwd", grid=(NBLK, nt),
        in_specs=[blk(0), halo(0), blk(NBLK), cwv, vec, mat, vec, mat, vec, vec],
        out_specs=[blk(0), blk(0)],
        out_shape=[jax.ShapeDtypeStruct((T, DR), F32), jax.ShapeDtypeStruct((T, DR), BF16)],
        scratch_shapes=[pltpu.VMEM((tc + 8, RB), F32), pltpu.VMEM((8, RB), F32)],
        compiler_params=_cparams(("parallel", "arbitrary")),
    )(uxg, uxg, uxg, cw, cb, wa, ba, wi, bi, lam)


def _rnn_bwd(uxg, hs, dy, cw, cb, wa, ba, wi, bi, lam, wat, wit, *, tc=640):
    T = uxg.shape[0]
    nt = T // tc
    nsub = tc // SUB
    blk, halo, vec, cwv, mat = _rnn_specs(tc, nt, True)

    def body(x_ref, xh_ref, ug_ref, h_ref, hh_ref, dy_ref, cw_ref, cb_ref, wa_ref, ba_ref, wi_ref,
             bi_ref, lam_ref, wat_ref, wit_ref,
             dux_ref, dug_ref, dcw_ref, dcb_ref, dwa_ref, dba_ref, dwi_ref, dbi_ref, dlam_ref,
             xb, hb, ab, dxb, xcs, rs, igs, mms, dgas, dgis, carry):
        k = pl.program_id(1)
        kt = nt - 1 - k

        @pl.when(k == 0)
        def _():
            carry[...] = jnp.zeros_like(carry)
            for ref in (dcw_ref, dcb_ref, dwa_ref, dba_ref, dwi_ref, dbi_ref, dlam_ref):
                ref[...] = jnp.zeros_like(ref)

        xb[0:8, :] = jnp.where(kt > 0, xh_ref[...], 0.0)
        xb[8:, :] = x_ref[...]
        hb[0:8, :] = jnp.where(kt > 0, hh_ref[...], 0.0)
        hb[8:, :] = h_ref[...]
        cwv_, cbv = cw_ref[...], cb_ref[...]
        wav, wiv = wa_ref[...], wi_ref[...]
        bav, biv, lamv = ba_ref[...], bi_ref[...], lam_ref[...]
        ab[tc:tc + 8, :] = jnp.broadcast_to(carry[1:2, :], (8, RB))
        dxb[tc:tc + 8, :] = carry[8:16, :]
        log_sig = None
        for sc in range(nsub):
            r0 = sc * SUB
            xc = cbv + cwv_[0:1, :] * xb[pl.ds(5 + r0, SUB), :]
            for j in range(1, CW):
                xc = xc + cwv_[j:j + 1, :] * xb[pl.ds(5 + j + r0, SUB), :]
            r, ig, log_sig, a, mm = _lru_gates(xc, wav, bav, wiv, biv, lamv)
            xcs[pl.ds(r0, SUB), :] = xc
            rs[pl.ds(r0, SUB), :] = r
            igs[pl.ds(r0, SUB), :] = ig
            mms[pl.ds(r0, SUB), :] = mm
            ab[pl.ds(r0, SUB), :] = a
        sig_neg = _sigmoid(-lamv)
        g_in = carry[0:1, :]
        dlam_acc = jnp.zeros((1, RB), F32)
        for sc in reversed(range(nsub)):
            r0 = sc * SUB
            xc, r, ig, mm = xcs[pl.ds(r0, SUB), :], rs[pl.ds(r0, SUB), :], igs[pl.ds(r0, SUB), :], mms[pl.ds(r0, SUB), :]
            a = ab[pl.ds(r0, SUB), :]
            a_next = ab[pl.ds(r0 + 1, SUB), :]
            hv = hb[pl.ds(8 + r0, SUB), :]
            hprev = hb[pl.ds(7 + r0, SUB), :]
            dyv = dy_ref[pl.ds(r0, SUB), :]
            gl, dgl = _gelu_parts(ug_ref[pl.ds(r0, SUB), :])
            dug_ref[pl.ds(r0, SUB), :] = (dyv * hv * dgl).astype(BF16)
            G = _scan_rev(a_next, dyv * gl, g_in)
            g_in = G[0:1, :]
            rows = kt * tc + r0 + lax.broadcasted_iota(jnp.int32, (SUB, RB), 0)
            db = jnp.where(rows >= PAD, G, 0.0)
            da = G * hprev
            dmm = db * (ig * xc)
            di = db * (mm * xc)
            dxc = db * (mm * ig)
            dlog_a = da * a - dmm * (a * a) / jnp.maximum(mm, 1e-30)
            dr = dlog_a * (LRU_C * log_sig)
            dlam_acc = dlam_acc + jnp.sum(dlog_a * (LRU_C * r), axis=0, keepdims=True)
            dga = dr * r * (1.0 - r)
            dgi = di * ig * (1.0 - ig)
            dgab, dgib = dga.astype(BF16), dgi.astype(BF16)
            dgas[pl.ds(r0, SUB), :] = dgab
            dgis[pl.ds(r0, SUB), :] = dgib
            dba_ref[...] += jnp.sum(dga, axis=0, keepdims=True)
            dbi_ref[...] += jnp.sum(dgi, axis=0, keepdims=True)
            dxc = dxc + jnp.dot(dgab, wat_ref[...], preferred_element_type=F32) \
                + jnp.dot(dgib, wit_ref[...], preferred_element_type=F32)
            dxb[pl.ds(r0, SUB), :] = dxc
        dlam_ref[...] += dlam_acc * sig_neg
        xcb = xcs[...].astype(BF16)
        tn = (((0,), (0,)), ((), ()))
        dwa_ref[...] += lax.dot_general(xcb, dgas[...], tn, preferred_element_type=F32)
        dwi_ref[...] += lax.dot_general(xcb, dgis[...], tn, preferred_element_type=F32)
        dxc_all = dxb[0:tc, :]
        dcb_ref[...] += jnp.sum(dxc_all, axis=0, keepdims=True)
        rows_all = kt * tc + lax.broadcasted_iota(jnp.int32, (tc, RB), 0)
        dux = jnp.zeros((tc, RB), F32)
        for j in range(CW):
            dcw_ref[j:j + 1, :] += jnp.sum(dxc_all * xb[pl.ds(5 + j, tc), :], axis=0, keepdims=True)
            dux = dux + cwv_[j:j + 1, :] * dxb[pl.ds(CW - 1 - j, tc), :]
        dux_ref[...] = jnp.where(rows_all >= PAD, dux, 0.0).astype(BF16)
        carry[0:1, :] = g_in
        carry[1:2, :] = ab[0:1, :]
        carry[8:16, :] = dxb[0:8, :]

    vec_out = pl.BlockSpec((1, RB), lambda c, k: (0, c))
    return pl.pallas_call(
        body, name="rnn_bwd", grid=(NBLK, nt),
        in_specs=[blk(0), halo(0), blk(NBLK), blk(0), halo(0), blk(0), cwv, vec, mat, vec, mat, vec, vec, mat, mat],
        out_specs=[blk(0), blk(0), cwv, vec_out, mat, vec_out, mat, vec_out, vec_out],
        out_shape=[jax.ShapeDtypeStruct((T, DR), BF16), jax.ShapeDtypeStruct((T, DR), BF16),
                   jax.ShapeDtypeStruct((CW, DR), F32), jax.ShapeDtypeStruct((1, DR), F32),
                   jax.ShapeDtypeStruct((NBLK, RB, RB), F32), jax.ShapeDtypeStruct((1, DR), F32),
                   jax.ShapeDtypeStruct((NBLK, RB, RB), F32), jax.ShapeDtypeStruct((1, DR), F32),
                   jax.ShapeDtypeStruct((1, DR), F32)],
        scratch_shapes=[pltpu.VMEM((tc + 8, RB), F32), pltpu.VMEM((tc + 8, RB), F32),
                        pltpu.VMEM((tc + 8, RB), F32), pltpu.VMEM((tc + 8, RB), F32),
                        pltpu.VMEM((tc, RB), F32), pltpu.VMEM((tc, RB), F32), pltpu.VMEM((tc, RB), F32),
                        pltpu.VMEM((tc, RB), F32), pltpu.VMEM((tc, RB), BF16), pltpu.VMEM((tc, RB), BF16),
                        pltpu.VMEM((16, RB), F32)],
        compiler_params=_cparams(("parallel", "arbitrary")),
    )(uxg, uxg, uxg, hs, hs, dy, cw, cb, wa, ba, wi, bi, lam, wat, wit)


def _attn_prep(q_all, kv_all, ukr, tab, *, tm=320):
    T = q_all.shape[0]

    def body(q_ref, kv_ref, kr_ref, tab_ref, qo_ref, ko_ref, vo_ref):
        tab_v = tab_ref[...]
        lane = lax.broadcasted_iota(jnp.int32, (tm, LANES), 1)
        t1 = kr_ref[...] * tab_v
        kro = jnp.where(lane < ROPE, t1 + pltpu.roll(t1, ROPE, 1), 0.0).astype(BF16)
        for h in range(NH):
            c0 = h * QW
            qo_ref[h, :, 0:NOPE] = (q_ref[:, c0:c0 + NOPE] * SCALE).astype(BF16)
            t2 = q_ref[:, c0 + NOPE:c0 + QW] * tab_v
            qo_ref[h, :, NOPE:QW] = ((t2 + pltpu.roll(t2, ROPE, 1)) * SCALE).astype(BF16)
            ko_ref[h, :, 0:NOPE] = kv_ref[:, c0:c0 + NOPE].astype(BF16)
            ko_ref[h, :, NOPE:QW] = kro
            vo_ref[h, :, :] = kv_ref[:, c0 + NOPE:c0 + QW].astype(BF16)

    return pl.pallas_call(
        body, name="attn_prep", grid=(T // tm,),
        in_specs=[_rows(tm, NH * QW), _rows(tm, NH * QW), _rows(tm, LANES), _rows(tm, LANES)],
        out_specs=[pl.BlockSpec((NH, tm, QW), lambda i: (0, i, 0)), pl.BlockSpec((NH, tm, QW), lambda i: (0, i, 0)),
                   pl.BlockSpec((NH, tm, VD), lambda i: (0, i, 0))],
        out_shape=[jax.ShapeDtypeStruct((NH, T, QW), BF16), jax.ShapeDtypeStruct((NH, T, QW), BF16),
                   jax.ShapeDtypeStruct((NH, T, VD), BF16)],
        compiler_params=_cparams(("parallel",)),
    )(q_all, kv_all, ukr, tab)


def _attn_prep_bwd(dq, dk, dv, tab, *, tm=320):
    T = dq.shape[1]

    def body(dq_ref, dk_ref, dv_ref, tab_ref, dqa_ref, dkva_ref, dkr_ref):
        tab_v = tab_ref[...]
        lane = lax.broadcasted_iota(jnp.int32, (tm, LANES), 1)
        dkro = jnp.zeros((tm, LANES), F32)
        for h in range(NH):
            c0 = h * QW
            dqa_ref[:, c0:c0 + NOPE] = (dq_ref[h, :, 0:NOPE] * SCALE).astype(BF16)
            d2 = dq_ref[h, :, NOPE:QW]
            dqa_ref[:, c0 + NOPE:c0 + QW] = ((d2 + pltpu.roll(d2, ROPE, 1)) * tab_v * SCALE).astype(BF16)
            dkva_ref[:, c0:c0 + NOPE] = dk_ref[h, :, 0:NOPE].astype(BF16)
            dkva_ref[:, c0 + NOPE:c0 + QW] = dv_ref[h, :, :].astype(BF16)
            dkro = dkro + dk_ref[h, :, NOPE:QW]
        dkro = jnp.where(lane < ROPE, dkro, 0.0)
        dkr_ref[...] = ((dkro + pltpu.roll(dkro, ROPE, 1)) * tab_v).astype(BF16)

    return pl.pallas_call(
        body, name="attn_prep_bwd", grid=(T // tm,),
        in_specs=[pl.BlockSpec((NH, tm, QW), lambda i: (0, i, 0)), pl.BlockSpec((NH, tm, QW), lambda i: (0, i, 0)),
                  pl.BlockSpec((NH, tm, VD), lambda i: (0, i, 0)), _rows(tm, LANES)],
        out_specs=[_rows(tm, NH * QW), _rows(tm, NH * QW), _rows(tm, LANES)],
        out_shape=[jax.ShapeDtypeStruct((T, NH * QW), BF16), jax.ShapeDtypeStruct((T, NH * QW), BF16),
                   jax.ShapeDtypeStruct((T, LANES), BF16)],
        compiler_params=_cparams(("parallel",)),
    )(dq, dk, dv, tab)


def _visible(q0, k0, nq, nk):
    rows = q0 + lax.broadcasted_iota(jnp.int32, (nq, nk), 0)
    cols = k0 + lax.broadcasted_iota(jnp.int32, (nq, nk), 1)
    return ((cols >> 6) <= (rows >> 6)) & (cols >= PAD)


def _visible_t(q0, k0, nq, nk):
    cols = k0 + lax.broadcasted_iota(jnp.int32, (nk, nq), 0)
    rows = q0 + lax.broadcasted_iota(jnp.int32, (nk, nq), 1)
    return ((cols >> 6) <= (rows >> 6)) & (cols >= PAD)


_NT = (((1,), (1,)), ((), ()))


def _flash_fwd(q, k, v, *, bq=640):
    T = q.shape[1]
    nq = T // bq
    rs = bq // CHAINS

    def body(q_ref, k_ref, v_ref, o_ref, lse_ref, *scr):
        i = pl.program_id(1)
        m_s, l_s, acc_s = scr[:CHAINS], scr[CHAINS:2 * CHAINS], scr[2 * CHAINS:]
        for r in range(CHAINS):
            m_s[r][...] = jnp.full_like(m_s[r], NEG)
            l_s[r][...] = jnp.zeros_like(l_s[r])
            acc_s[r][...] = jnp.zeros_like(acc_s[r])

        def step(j, masked):
            off = pl.multiple_of(j * bq, bq)
            kv_ = k_ref[pl.ds(off, bq), :]
            vv = v_ref[pl.ds(off, bq), :]
            for r in range(CHAINS):
                rows = pl.ds(r * rs, rs)
                s = lax.dot_general(q_ref[rows, :], kv_, _NT, preferred_element_type=F32)
                if masked:
                    s = jnp.where(_visible(i * bq + r * rs, j * bq, rs, bq), s, NEG)
                m_prev = m_s[r][...]
                m_new = jnp.maximum(m_prev, jnp.max(s, axis=-1, keepdims=True))
                p = jnp.exp(s - m_new)
                alpha = jnp.exp(m_prev - m_new)
                l_s[r][...] = alpha * l_s[r][...] + jnp.sum(p, axis=-1, keepdims=True)
                acc_s[r][...] = alpha * acc_s[r][...] + jnp.dot(p.astype(BF16), vv, preferred_element_type=F32)
                m_s[r][...] = m_new

        step(0, True)

        def loop(j, c):
            step(j, False)
            return c

        lax.fori_loop(1, i, loop, 0)

        @pl.when(i > 0)
        def _():
            step(i, True)

        for r in range(CHAINS):
            rows = pl.ds(r * rs, rs)
            o_ref[rows, :] = (acc_s[r][...] / l_s[r][...]).astype(BF16)
            lse_ref[rows, :] = m_s[r][...] + jnp.log(l_s[r][...])

    return pl.pallas_call(
        body, name="flash_fwd", grid=(NH, nq),
        in_specs=[pl.BlockSpec((None, bq, QW), lambda h, i: (h, i, 0)),
                  pl.BlockSpec((None, T, QW), lambda h, i: (h, 0, 0)),
                  pl.BlockSpec((None, T, VD), lambda h, i: (h, 0, 0))],
        out_specs=[pl.BlockSpec((bq, VD), lambda h, i: (i, h)),
                   pl.BlockSpec((None, bq, 1), lambda h, i: (h, i, 0))],
        out_shape=[jax.ShapeDtypeStruct((T, NH * VD), BF16), jax.ShapeDtypeStruct((NH, T, 1), F32)],
        scratch_shapes=[pltpu.VMEM((rs, 1), F32)] * (2 * CHAINS) + [pltpu.VMEM((rs, VD), F32)] * CHAINS,
        compiler_params=_cparams(("parallel", "parallel")),
    )(q, k, v)


def _attn_delta(o, do, *, tm=640):
    T = o.shape[0]

    def body(o_ref, do_ref, d_ref):
        prod = o_ref[...].astype(F32) * do_ref[...].astype(F32)
        for h in range(NH):
            d_ref[h, :, :] = jnp.sum(prod[:, h * VD:(h + 1) * VD], axis=-1, keepdims=True)

    return pl.pallas_call(
        body, name="attn_delta", grid=(T // tm,),
        in_specs=[_rows(tm, NH * VD), _rows(tm, NH * VD)],
        out_specs=pl.BlockSpec((NH, tm, 1), lambda i: (0, i, 0)),
        out_shape=jax.ShapeDtypeStruct((NH, T, 1), F32),
        compiler_params=_cparams(("parallel",)),
    )(o, do)


def _flash_bwd_dq(q, k, v, do, lse, delta, *, bq=640):
    T = q.shape[1]
    nq = T // bq
    rs = bq // CHAINS

    def body(q_ref, k_ref, v_ref, do_ref, lse_ref, dl_ref, dq_ref):
        i = pl.program_id(1)
        dq_ref[...] = jnp.zeros_like(dq_ref)

        def step(j, masked):
            off = pl.multiple_of(j * bq, bq)
            kv_ = k_ref[pl.ds(off, bq), :]
            vv = v_ref[pl.ds(off, bq), :]
            for r in range(CHAINS):
                rows = pl.ds(r * rs, rs)
                s = lax.dot_general(q_ref[rows, :], kv_, _NT, preferred_element_type=F32)
                if masked:
                    s = jnp.where(_visible(i * bq + r * rs, j * bq, rs, bq), s, NEG)
                p = jnp.exp(s - lse_ref[rows, :])
                dp = lax.dot_general(do_ref[rows, :], vv, _NT, preferred_element_type=F32)
                ds = (p * (dp - dl_ref[rows, :])).astype(BF16)
                dq_ref[rows, :] += jnp.dot(ds, kv_, preferred_element_type=F32)

        step(0, True)

        def loop(j, c):
            step(j, False)
            return c

        lax.fori_loop(1, i, loop, 0)

        @pl.when(i > 0)
        def _():
            step(i, True)

    return pl.pallas_call(
        body, name="flash_bwd_dq", grid=(NH, nq),
        in_specs=[pl.BlockSpec((None, bq, QW), lambda h, i: (h, i, 0)),
                  pl.BlockSpec((None, T, QW), lambda h, i: (h, 0, 0)),
                  pl.BlockSpec((None, T, VD), lambda h, i: (h, 0, 0)),
                  pl.BlockSpec((bq, VD), lambda h, i: (i, h)),
                  pl.BlockSpec((None, bq, 1), lambda h, i: (h, i, 0)),
                  pl.BlockSpec((None, bq, 1), lambda h, i: (h, i, 0))],
        out_specs=pl.BlockSpec((None, bq, QW), lambda h, i: (h, i, 0)),
        out_shape=jax.ShapeDtypeStruct((NH, T, QW), F32),
        compiler_params=_cparams(("parallel", "parallel")),
    )(q, k, v, do, lse, delta)


def _flash_bwd_dkv(q, k, v, do, lse_row, delta_row, *, bq=640):
    T = q.shape[1]
    nq = T // bq
    rs = bq // CHAINS

    def body(q_ref, k_ref, v_ref, do_ref, lse_ref, dl_ref, dk_ref, dv_ref):
        j = pl.program_id(1)
        dk_ref[...] = jnp.zeros_like(dk_ref)
        dv_ref[...] = jnp.zeros_like(dv_ref)

        def step(i, masked):
            off = pl.multiple_of(i * bq, bq)
            qv = q_ref[pl.ds(off, bq), :]
            dov = do_ref[pl.ds(off, bq), :]
            lse_v = lse_ref[:, pl.ds(off, bq)]
            dl_v = dl_ref[:, pl.ds(off, bq)]
            for r in range(CHAINS):
                rows = pl.ds(r * rs, rs)
                st = lax.dot_general(k_ref[rows, :], qv, _NT, preferred_element_type=F32)
                if masked:
                    st = jnp.where(_visible_t(i * bq, j * bq + r * rs, bq, rs), st, NEG)
                pt = jnp.exp(st - lse_v)
                dv_ref[rows, :] += jnp.dot(pt.astype(BF16), dov, preferred_element_type=F32)
                dpt = lax.dot_general(v_ref[rows, :], dov, _NT, preferred_element_type=F32)
                dst = (pt * (dpt - dl_v)).astype(BF16)
                dk_ref[rows, :] += jnp.dot(dst, qv, preferred_element_type=F32)

        step(j, True)

        @pl.when(j == 0)
        def _():
            def loop(i, c):
                step(i, True)
                return c
            lax.fori_loop(1, nq, loop, 0)

        @pl.when(j > 0)
        def _():
            def loop(i, c):
                step(i, False)
                return c
            lax.fori_loop(j + 1, nq, loop, 0)

    return pl.pallas_call(
        body, name="flash_bwd_dkv", grid=(NH, nq),
        in_specs=[pl.BlockSpec((None, T, QW), lambda h, j: (h, 0, 0)),
                  pl.BlockSpec((None, bq, QW), lambda h, j: (h, j, 0)),
                  pl.BlockSpec((None, bq, VD), lambda h, j: (h, j, 0)),
                  pl.BlockSpec((T, VD), lambda h, j: (0, h)),
                  pl.BlockSpec((None, 1, T), lambda h, j: (h, 0, 0)),
                  pl.BlockSpec((None, 1, T), lambda h, j: (h, 0, 0))],
        out_specs=[pl.BlockSpec((None, bq, QW), lambda h, j: (h, j, 0)),
                   pl.BlockSpec((None, bq, VD), lambda h, j: (h, j, 0))],
        out_shape=[jax.ShapeDtypeStruct((NH, T, QW), F32), jax.ShapeDtypeStruct((NH, T, VD), F32)],
        compiler_params=_cparams(("parallel", "parallel")),
    )(q, k, v, do, lse_row, delta_row)


def _rope_table(T):
    pos = (jnp.arange(T, dtype=jnp.int32) - PAD).astype(F32)
    inv_freq = ROPE_THETA ** (-jnp.arange(0, ROPE, 2, dtype=F32) / ROPE)
    ang = pos[:, None] * inv_freq[None, :]
    cos, sin = jnp.cos(ang), jnp.sin(ang)
    return jnp.concatenate([cos, cos, -sin, sin], axis=1)


def _swap_halves(w):
    return jnp.concatenate([w[..., ROPE // 2:], w[..., :ROPE // 2]], axis=-1)


O_UX, O_UG, O_UQ, O_UKV, O_UKR, O_UM = 0, DR, 2 * DR, 2 * DR + QR, 2 * DR + QR + KVR, 2 * DR + QR + KVR + ROPE


def _prep_weights(w):
    b = lambda a: a.astype(BF16)
    w_in = w["w_in"]
    kr = w_in[:, O_UKR:O_UM]
    p = {
        "w_xg": b(w_in[:, :O_UQ]),
        "w_q": b(w_in[:, O_UQ:O_UKV]),
        "w_kv": b(w_in[:, O_UKV:O_UKR]),
        "w_kr": b(jnp.concatenate([kr, _swap_halves(kr)], axis=1)),
        "w_m": b(w_in[:, O_UM:]),
    }
    wq = w["w_uq"].reshape(QR, NH, NOPE + ROPE)
    p["w_uq"] = b(jnp.concatenate([wq, _swap_halves(wq[..., NOPE:])], axis=-1).reshape(QR, NH * QW))
    p["w_ukv"] = b(w["w_ukv"])
    p["w_br"] = b(w["w_branch"][:DR])
    p["w_ba"] = b(w["w_branch"][DR:])
    p["w_out"] = b(w["w_out"])
    p["w_fi"] = b(w["w_ffn_in"])
    p["w_fo"] = b(w["w_ffn_out"])
    for n in ("w_xg", "w_q", "w_kv", "w_kr", "w_m", "w_uq", "w_ukv", "w_br", "w_ba", "w_out", "w_fi", "w_fo"):
        p[n + "_t"] = p[n].T
    p["wa"] = b(w["w_rec_a"])
    p["wi"] = b(w["w_rec_i"])
    p["wa_t"] = jnp.swapaxes(p["wa"], 1, 2)
    p["wi_t"] = jnp.swapaxes(p["wi"], 1, 2)
    return p


def _local_step(x, tgt, w):
    S = x.shape[0]
    T = FRONT + S
    p = _prep_weights(w)
    tab = _rope_table(T)
    h0 = jnp.concatenate([jnp.zeros((PAD, D), F32), w["meta_tokens"], x], axis=0)
    row = lambda v: v.reshape(1, -1)

    z = _rmsnorm_fwd(h0, row(w["norm_mix_g"]), name="norm_mix")
    uxg = _mm(z, p["w_xg"], name="mm_uxg")
    uq = _mm(z, p["w_q"], name="mm_uq")
    ukv = _mm(z, p["w_kv"], name="mm_ukv")
    ukr = _mm(z, p["w_kr"], name="mm_ukr")
    um = _mm(z, p["w_m"], name="mm_um")
    rnn_w = (w["conv_w"], row(w["conv_b"]), p["wa"], row(w["b_rec_a"]), p["wi"], row(w["b_rec_i"]),
             row(w["lru_lambda"]))
    hs, y_rnn = _rnn_fwd(uxg, *rnn_w)
    qn = _rmsnorm_fwd(uq, row(w["q_norm_g"]), name="norm_q")
    kvn = _rmsnorm_fwd(ukv, row(w["kv_norm_g"]), name="norm_kv")
    q_all = _mm(qn, p["w_uq"], name="mm_q")
    kv_all = _mm(kvn, p["w_ukv"], name="mm_kv")
    qh, kh, vh = _attn_prep(q_all, kv_all, ukr, tab)
    y_att, lse = _flash_fwd(qh, kh, vh)
    p_rnn = _mm(y_rnn, p["w_br"], name="mm_prnn")
    p_att = _mm(y_att, p["w_ba"], name="mm_patt")
    bg = row(w["b_gate"])
    mixed = _gate_mix_fwd(um, bg, p_rnn, p_att)
    h1 = _mm(mixed, p["w_out"], name="mm_out", res=h0)
    zf = _rmsnorm_fwd(h1, row(w["norm_ffn_g"]), name="norm_ffn")
    ff = _mm(zf, p["w_fi"], name="mm_ffn_in")
    act = _swiglu_fwd(ff)
    h2 = _mm(act, p["w_fo"], name="mm_ffn_out", res=h1)

    g = {}
    dh2, dh2b, dg_fin, lsum = _loss_head(h2, tgt, row(w["final_norm_g"]))
    loss = 0.5 * jnp.sum(lsum) / D
    g["final_norm_g"] = dg_fin.reshape(-1)
    dact = _mm(dh2b, p["w_fo_t"], name="mm_dact")
    g["w_ffn_out"] = _mm_tn(act, dh2b, name="mm_dw_ffn_out")
    dff = _swiglu_bwd(ff, dact)
    dzf = _mm(dff, p["w_fi_t"], name="mm_dzf")
    g["w_ffn_in"] = _mm_tn(zf, dff, name="mm_dw_ffn_in")
    dh1, dh1b, dg = _rmsnorm_bwd(h1, row(w["norm_ffn_g"]), dzf, dh2, name="norm_ffn_bwd")
    g["norm_ffn_g"] = dg
    dmixed = _mm(dh1b, p["w_out_t"], name="mm_dmixed")
    g["w_out"] = _mm_tn(mixed, dh1b, name="mm_dw_out")
    dp_rnn, dp_att, dum, dbg = _gate_mix_bwd(um, bg, p_rnn, p_att, dmixed)
    g["b_gate"] = dbg.reshape(2, D)
    dy_rnn = _mm(dp_rnn, p["w_br_t"], name="mm_dy_rnn")
    dy_att = _mm(dp_att, p["w_ba_t"], name="mm_dy_att", out_dtype=BF16)
    g["w_branch"] = jnp.concatenate([_mm_tn(y_rnn, dp_rnn, name="mm_dw_br"),
                                     _mm_tn(y_att, dp_att, name="mm_dw_ba")], axis=0)
    delta = _attn_delta(y_att, dy_att)
    dq = _flash_bwd_dq(qh, kh, vh, dy_att, lse, delta)
    dk, dv = _flash_bwd_dkv(qh, kh, vh, dy_att, lse.reshape(NH, 1, T), delta.reshape(NH, 1, T))
    dq_all, dkv_all, dukr = _attn_prep_bwd(dq, dk, dv, tab)
    dqn = _mm(dq_all, p["w_uq_t"], name="mm_dqn")
    dkvn = _mm(dkv_all, p["w_ukv_t"], name="mm_dkvn")
    dwq = _mm_tn(qn, dq_all, name="mm_dw_uq").reshape(QR, NH, QW)
    dwq_rope = dwq[..., NOPE:NOPE + ROPE] + _swap_halves(dwq[..., NOPE + ROPE:])
    g["w_uq"] = jnp.concatenate([dwq[..., :NOPE], dwq_rope], axis=-1).reshape(QR, NH * (NOPE + ROPE))
    g["w_ukv"] = _mm_tn(kvn, dkv_all, name="mm_dw_ukv")
    duq, dg = _rmsnorm_bwd(uq, row(w["q_norm_g"]), dqn, None, name="norm_q_bwd", want_f32=False)
    g["q_norm_g"] = dg
    dukv, dg = _rmsnorm_bwd(ukv, row(w["kv_norm_g"]), dkvn, None, name="norm_kv_bwd", want_f32=False)
    g["kv_norm_g"] = dg
    (dux, dug, g["conv_w"], g["conv_b"], g["w_rec_a"], g["b_rec_a"], g["w_rec_i"], g["b_rec_i"],
     g["lru_lambda"]) = _rnn_bwd(uxg, hs, dy_rnn, *rnn_w, p["wa_t"], p["wi_t"])
    dz = _mm(dux, p["w_xg_t"][:DR], name="mm_dz_x")
    dz = _mm(dug, p["w_xg_t"][DR:], name="mm_dz_g", res=dz)
    dz = _mm(duq, p["w_q_t"], name="mm_dz_q", res=dz)
    dz = _mm(dukv, p["w_kv_t"], name="mm_dz_kv", res=dz)
    dz = _mm(dukr, p["w_kr_t"], name="mm_dz_kr", res=dz)
    dz = _mm(dum, p["w_m_t"], name="mm_dz_m", res=dz)
    dwkr = _mm_tn(z, dukr, name="mm_dw_kr")
    g["w_in"] = jnp.concatenate([
        _mm_tn(z, dux, name="mm_dw_x"), _mm_tn(z, dug, name="mm_dw_g"),
        _mm_tn(z, duq, name="mm_dw_q"), _mm_tn(z, dukv, name="mm_dw_kv"),
        dwkr[:, :ROPE] + _swap_halves(dwkr[:, ROPE:]),
        _mm_tn(z, dum, name="mm_dw_m")], axis=1)
    dh0, dg = _rmsnorm_bwd(h0, row(w["norm_mix_g"]), dz, dh1, name="norm_mix_bwd", want_bf16=False)
    g["norm_mix_g"] = dg
    g["meta_tokens"] = dh0[PAD:FRONT]
    return loss, dh0[FRONT:], g


HBM = pl.BlockSpec(memory_space=pltpu.HBM)
CHIP_FLIPS = ((1, 0), (0, 1), (1, 1))


def _place():
    return lax.axis_index("x"), lax.axis_index("y"), lax.axis_index("c")


def _flip(v, f):
    return 1 - v if f else v


def _dma_sems(n):
    return pltpu.SemaphoreType.DMA((n,))


def _allgather_chips(srcs, *, name):
    n = len(srcs)

    def body(*refs):
        src_refs, out_refs, stage = refs[:n], refs[n:2 * n], refs[2 * n:3 * n]
        send_sems, recv_sems, in_sems, local_sems = refs[3 * n:]
        x, y, c = _place()
        me = 2 * x + y
        loads = [pltpu.make_async_copy(src_refs[a], stage[a], in_sems.at[a]) for a in range(n)]
        for cp in loads:
            cp.start()
        copies = []
        for a in range(n):
            loads[a].wait()
            for k, (fx, fy) in enumerate(CHIP_FLIPS):
                cp = pltpu.make_async_remote_copy(
                    src_ref=stage[a], dst_ref=out_refs[a].at[me], send_sem=send_sems.at[3 * a + k],
                    recv_sem=recv_sems.at[3 * a + k], device_id=(_flip(x, fx), _flip(y, fy), c),
                    device_id_type=MESH)
                cp.start()
                copies.append(cp)
            cp = pltpu.make_async_copy(stage[a], out_refs[a].at[me], local_sems.at[a])
            cp.start()
            copies.append(cp)
        for cp in copies:
            cp.wait()

    return pl.pallas_call(
        body, name=name, in_specs=[HBM] * n, out_specs=[HBM] * n,
        out_shape=[jax.ShapeDtypeStruct((4,) + s.shape, s.dtype) for s in srcs],
        scratch_shapes=[pltpu.VMEM(s.shape, s.dtype) for s in srcs]
        + [_dma_sems(3 * n), _dma_sems(3 * n), _dma_sems(n), _dma_sems(n)],
        compiler_params=pltpu.CompilerParams(vmem_limit_bytes=VMEM_LIMIT),
    )(*srcs)


def _scatter_chips(srcs, *, name):
    n = len(srcs)

    def body(*refs):
        src_refs, out_refs = refs[:n], refs[n:2 * n]
        send_sems, recv_sems = refs[2 * n:]
        x, y, c = _place()
        copies = []
        for a in range(n):
            for k, (fx, fy) in enumerate(CHIP_FLIPS):
                px, py = _flip(x, fx), _flip(y, fy)
                cp = pltpu.make_async_remote_copy(
                    src_ref=src_refs[a].at[2 * px + py], dst_ref=out_refs[a].at[k],
                    send_sem=send_sems.at[3 * a + k], recv_sem=recv_sems.at[3 * a + k],
                    device_id=(px, py, c), device_id_type=MESH)
                cp.start()
                copies.append(cp)
        for cp in copies:
            cp.wait()

    return pl.pallas_call(
        body, name=name, in_specs=[HBM] * n, out_specs=[HBM] * n,
        out_shape=[jax.ShapeDtypeStruct((3,) + s.shape[1:], s.dtype) for s in srcs],
        scratch_shapes=[_dma_sems(3 * n), _dma_sems(3 * n)],
    )(*srcs)


def _sibling_take(srcs, *, name):
    n = len(srcs)

    def body(*refs):
        src_refs, out_refs = refs[:n], refs[n:2 * n]
        send_sems, recv_sems = refs[2 * n:]
        x, y, c = _place()
        copies = []
        for a in range(n):
            h = srcs[a].shape[1] // 2
            theirs = pl.ds(pl.multiple_of((1 - c) * h, 8), h)
            cp = pltpu.make_async_remote_copy(
                src_ref=src_refs[a].at[:, theirs, :], dst_ref=out_refs[a], send_sem=send_sems.at[a],
                recv_sem=recv_sems.at[a], device_id=(x, y, 1 - c), device_id_type=MESH)
            cp.start()
            copies.append(cp)
        for cp in copies:
            cp.wait()

    return pl.pallas_call(
        body, name=name, in_specs=[HBM] * n, out_specs=[HBM] * n,
        out_shape=[jax.ShapeDtypeStruct((4, s.shape[1] // 2, s.shape[2]), s.dtype) for s in srcs],
        scratch_shapes=[_dma_sems(n), _dma_sems(n)],
    )(*srcs)


def _sibling_swap(srcs, *, name):
    n = len(srcs)

    def body(*refs):
        src_refs, out_refs = refs[:n], refs[n:2 * n]
        send_sems, recv_sems = refs[2 * n:]
        x, y, c = _place()
        copies = []
        for a in range(n):
            cp = pltpu.make_async_remote_copy(
                src_ref=src_refs[a], dst_ref=out_refs[a], send_sem=send_sems.at[a],
                recv_sem=recv_sems.at[a], device_id=(x, y, 1 - c), device_id_type=MESH)
            cp.start()
            copies.append(cp)
        for cp in copies:
            cp.wait()

    return pl.pallas_call(
        body, name=name, in_specs=[HBM] * n, out_specs=[HBM] * n,
        out_shape=[jax.ShapeDtypeStruct(s.shape, s.dtype) for s in srcs],
        scratch_shapes=[_dma_sems(n), _dma_sems(n)],
    )(*srcs)


def _row_tile(rows, cols, n_arrays):
    budget = 24 * 1024 * 1024 // (2 * 4 * n_arrays * cols)
    best = 16
    for t in range(16, rows + 1, 16):
        if rows % t == 0 and t <= budget:
            best = t
    assert rows % best == 0, (rows, cols)
    return best


def _add_halves(mine, theirs, wire, *, name):
    _, h, c = mine.shape
    tm = _row_tile(h, c, 3)
    spec = pl.BlockSpec((None, tm, c), lambda s, i: (s, i, 0))

    def body(a_ref, b_ref, o_ref):
        o_ref[...] = (a_ref[...] + b_ref[...]).astype(wire)

    return pl.pallas_call(
        body, name=name, grid=(4, h // tm), in_specs=[spec, spec], out_specs=spec,
        out_shape=jax.ShapeDtypeStruct(mine.shape, wire), compiler_params=_cparams(("parallel", "parallel")),
    )(mine, theirs)


def _sum4(own, recv, *, name):
    h, c = own.shape
    tm = _row_tile(h, c, 5)

    def body(o_ref, r_ref, out_ref):
        f = lambda k: r_ref[k].astype(F32)
        out_ref[...] = ((o_ref[...].astype(F32) + f(0)) + f(1)) + f(2)

    return pl.pallas_call(
        body, name=name, grid=(h // tm,),
        in_specs=[_rows(tm, c), pl.BlockSpec((3, tm, c), lambda i: (0, i, 0))],
        out_specs=_rows(tm, c), out_shape=jax.ShapeDtypeStruct((h, c), F32),
        compiler_params=_cparams(("parallel",)),
    )(own, recv)


def _adamw(g, w, m, v, *, name):
    r, c = g.shape
    tm = _row_tile(r, c, 7)
    c1 = 1.0 / (1.0 - ADAM_B1 ** ADAM_STEP)
    c2 = 1.0 / (1.0 - ADAM_B2 ** ADAM_STEP)

    def body(g_ref, w_ref, m_ref, v_ref, d_ref, nm_ref, nv_ref):
        gv = g_ref[...]
        nm = ADAM_B1 * m_ref[...] + (1.0 - ADAM_B1) * gv
        nv = ADAM_B2 * v_ref[...] + (1.0 - ADAM_B2) * (gv * gv)
        nm_ref[...] = nm
        nv_ref[...] = nv
        d_ref[...] = -ADAM_LR * ((nm * c1) / (jnp.sqrt(nv * c2) + ADAM_EPS) + ADAM_WD * w_ref[...])

    spec = _rows(tm, c)
    shape = jax.ShapeDtypeStruct((r, c), F32)
    return pl.pallas_call(
        body, name=name, grid=(r // tm,), in_specs=[spec] * 4, out_specs=[spec] * 3,
        out_shape=[shape] * 3, compiler_params=_cparams(("parallel",)),
    )(g, w, m, v)


BIG = (("w_in", (D, 1328), 1), ("w_uq", (QR, 384), 1), ("w_ukv", (KVR, 512), 1), ("w_branch", (576, D), 0),
       ("w_out", (256, D), 0), ("w_ffn_in", (D, 1408), 1), ("w_ffn_out", (704, D), 0))
SMALL = (("meta_tokens", (NMETA, 256), 1), ("b_gate", (2, 256), 1), ("conv_w", (CW, 320), 1))
REPL = (("norm_mix_g", (D,)), ("conv_b", (DR,)), ("w_rec_a", (NBLK, RB, RB)), ("b_rec_a", (DR,)),
        ("w_rec_i", (NBLK, RB, RB)), ("b_rec_i", (DR,)), ("lru_lambda", (DR,)), ("q_norm_g", (QR,)),
        ("kv_norm_g", (KVR,)), ("norm_ffn_g", (D,)), ("final_norm_g", (D,)))
WEIGHTS = ("meta_tokens", "norm_mix_g", "w_in", "b_gate", "conv_w", "conv_b", "w_rec_a", "b_rec_a", "w_rec_i",
           "b_rec_i", "lru_lambda", "q_norm_g", "w_uq", "kv_norm_g", "w_ukv", "w_branch", "w_out", "norm_ffn_g",
           "w_ffn_in", "w_ffn_out", "final_norm_g")
W = 1024
SMALL_N = sum(math.prod(s) for _, s, _ in SMALL)
SMALL_ROWS = 8
REPL_N = sum(math.prod(s) for _, s in REPL)
QUART_ROWS = 88
assert SMALL_N <= SMALL_ROWS * W and REPL_N <= 4 * QUART_ROWS * W


def _flat_pad(parts, rows):
    v = jnp.concatenate([p.reshape(-1) for p in parts])
    return jnp.pad(v, (0, rows * W - v.shape[0])).reshape(rows, W)


def _shard_stack(full, shard_shape, axis):
    r, cs = shard_shape
    if axis == 0:
        return full.reshape(4, r, cs)
    return jnp.stack([full[:, s * cs:(s + 1) * cs] for s in range(4)])


def _unshard(stack, axis):
    if axis == 0:
        return stack.reshape(4 * stack.shape[1], stack.shape[2])
    return jnp.concatenate([stack[s] for s in range(4)], axis=1)


def _split(flat, table):
    out, off = {}, 0
    for name, shape, *_ in table:
        n = math.prod(shape)
        out[name] = flat[..., off:off + n].reshape(flat.shape[:-1] + tuple(shape))
        off += n
    return out


def _misc_state(args, prefix, quarter):
    small = _flat_pad([args[prefix + n] for n, _, _ in SMALL], SMALL_ROWS)
    repl = _flat_pad([args[prefix + n] for n, _ in REPL], 4 * QUART_ROWS)
    mine = lax.dynamic_slice_in_dim(repl, quarter * QUART_ROWS, QUART_ROWS, axis=0)
    return jnp.concatenate([small, mine], axis=0)


def kernel(x, meta_tokens, norm_mix_g, w_in, b_gate, conv_w, conv_b, w_rec_a, b_rec_a, w_rec_i, b_rec_i, lru_lambda, q_norm_g, w_uq, kv_norm_g, w_ukv, w_branch, w_out, norm_ffn_g, w_ffn_in, w_ffn_out, final_norm_g, loss_target, m_meta_tokens, m_norm_mix_g, m_w_in, m_b_gate, m_conv_w, m_conv_b, m_w_rec_a, m_b_rec_a, m_w_rec_i, m_b_rec_i, m_lru_lambda, m_q_norm_g, m_w_uq, m_kv_norm_g, m_w_ukv, m_w_branch, m_w_out, m_norm_ffn_g, m_w_ffn_in, m_w_ffn_out, m_final_norm_g, v_meta_tokens, v_norm_mix_g, v_w_in, v_b_gate, v_conv_w, v_conv_b, v_w_rec_a, v_b_rec_a, v_w_rec_i, v_b_rec_i, v_lru_lambda, v_q_norm_g, v_w_uq, v_kv_norm_g, v_w_ukv, v_w_branch, v_w_out, v_norm_ffn_g, v_w_ffn_in, v_w_ffn_out, v_final_norm_g):
    args = dict(locals())
    chip = 2 * lax.axis_index("x") + lax.axis_index("y")
    core = lax.axis_index("c")

    shards = [args[n].reshape(s).astype(BF16) for n, s, _ in BIG]
    small = _flat_pad([args[n] for n, _, _ in SMALL], SMALL_ROWS)
    gathered = _allgather_chips(shards + [small], name="gather_weights")
    w = {}
    for (name, _, axis), stack in zip(BIG, gathered):
        w[name] = _unshard(stack, axis)
    small_parts = _split(gathered[-1].reshape(4, SMALL_ROWS * W), SMALL)
    for name, _, axis in SMALL:
        w[name] = _unshard(small_parts[name], axis)
    for name, shape in REPL:
        w[name] = args[name].reshape(shape)

    loss, grad_x, g = _local_step(x[0], loss_target[0], w)
    loss = lax.psum(loss, ("x", "y", "c"))

    red = [_shard_stack(g[n], s, a) for n, s, a in BIG]
    small_g = jnp.concatenate([_shard_stack(g[n], s, a).reshape(4, -1) for n, s, a in SMALL], axis=1)
    small_g = jnp.pad(small_g, ((0, 0), (0, SMALL_ROWS * W - SMALL_N))).reshape(4, SMALL_ROWS, W)
    repl_g = _flat_pad([g[n] for n, _ in REPL], 4 * QUART_ROWS).reshape(4, QUART_ROWS, W)
    red.append(jnp.concatenate([small_g, repl_g], axis=1))

    theirs = _sibling_take(red, name="reduce_sibling")
    parts = []
    for k, (a, t) in enumerate(zip(red, theirs)):
        h = a.shape[1] // 2
        mine = lax.dynamic_slice_in_dim(a, core * h, h, axis=1)
        wire = F32 if k == len(red) - 1 else BF16
        parts.append(_add_halves(mine, t, wire, name=f"add_sibling_{k}"))
    recv = _scatter_chips(parts, name="reduce_chips")
    halves = [_sum4(lax.dynamic_index_in_dim(p, chip, 0, keepdims=False), r, name=f"sum_chips_{k}")
              for k, (p, r) in enumerate(zip(parts, recv))]
    others = _sibling_swap(halves, name="share_sibling")
    gred = [jnp.where(core == 0, jnp.concatenate([a, b], axis=0), jnp.concatenate([b, a], axis=0))
            for a, b in zip(halves, others)]

    results = {}
    for (name, shape, _), gr in zip(BIG, gred):
        d, nm, nv = _adamw(gr, args[name].reshape(shape), args["m_" + name].reshape(shape),
                           args["v_" + name].reshape(shape), name="adamw_" + name)
        results[name] = (gr, d, nm, nv)
    d, nm, nv = _adamw(gred[-1], _misc_state(args, "", chip), _misc_state(args, "m_", chip),
                       _misc_state(args, "v_", chip), name="adamw_misc")
    misc = (gred[-1], d, nm, nv)

    quarters = jnp.stack([t[SMALL_ROWS:] for t in misc])
    repl_all = _allgather_chips([quarters], name="gather_repl")[0]
    repl_all = repl_all.transpose(1, 0, 2, 3).reshape(4, 4 * QUART_ROWS * W)

    outs = []
    for k in range(4):
        sm = _split(misc[k][:SMALL_ROWS].reshape(-1), SMALL)
        rp = _split(repl_all[k], REPL)
        for name in WEIGHTS:
            val = results[name][k] if name in results else (sm[name] if name in sm else rp[name])
            outs.append(val.reshape(args[name].shape))
    return (loss, grad_x[None], *outs)
```

```python
import functools
import math

import jax
import jax.numpy as jnp
from jax import lax
from jax.experimental import pallas as pl
from jax.experimental.pallas import tpu as pltpu

F32 = jnp.float32
BF16 = jnp.bfloat16

D = 1024
DR = 1280
NBLK = 10
RB = 128
CW = 4
NH = 8
NOPE = 128
ROPE = 64
VD = 128
QR = 384
KVR = 256
DFF = 2816
NMETA = 16
EPS = 1e-6
LRU_C = 8.0
ROPE_THETA = 10000.0
SCALE = 1.0 / math.sqrt(NOPE + ROPE)
NEG = -1e30
FRONT = 128
PAD = FRONT - NMETA
QW = 2 * NOPE
LANES = 128
SUB = 128
CHAINS = 4
VMEM_LIMIT = 52 * 1024 * 1024

ADAM_LR = 0.001
ADAM_B1 = 0.9
ADAM_B2 = 0.999
ADAM_EPS = 1e-08
ADAM_WD = 0.01
ADAM_STEP = 10

MESH = pl.DeviceIdType.MESH


def _cparams(sem):
    return pltpu.CompilerParams(dimension_semantics=sem, vmem_limit_bytes=VMEM_LIMIT)


def _sigmoid(x):
    return 1.0 / (1.0 + jnp.exp(-x))


def _gelu_parts(x):
    c = math.sqrt(2.0 / math.pi)
    inner = c * (x + 0.044715 * x * x * x)
    t = jnp.tanh(inner)
    g = 0.5 * x * (1.0 + t)
    dg = 0.5 * (1.0 + t) + 0.5 * x * (1.0 - t * t) * c * (1.0 + 3.0 * 0.044715 * x * x)
    return g, dg


def _divisors(n, step, cap):
    return [d for d in range(step, min(n, cap) + 1, step) if n % d == 0] or [n]


MM_VMEM_BUDGET = 40 * 1024 * 1024
MM_MAX_ROWS = 1664
MM_MAX_COLS = 1408


def _mm_tiles(M, K, N, a_item, out_item, has_res):
    best = None
    for tn in _divisors(N, LANES, MM_MAX_COLS):
        for tm in _divisors(M, 16, MM_MAX_ROWS):
            need = 2 * (tm * K * a_item + K * tn * 2 + tm * tn * (out_item + (4 if has_res else 0)))
            if need <= MM_VMEM_BUDGET and (best is None or tm * tn > best[0] * best[1]):
                best = (tm, tn)
    assert best is not None, (M, K, N)
    return best


def _mm(a, b, *, name, out_dtype=F32, res=None):
    M, K = a.shape
    N = b.shape[1]
    has_res = res is not None
    tm, tn = _mm_tiles(M, K, N, a.dtype.itemsize, jnp.dtype(out_dtype).itemsize, has_res)

    def body(*refs):
        if has_res:
            a_ref, b_ref, r_ref, o_ref = refs
        else:
            a_ref, b_ref, o_ref = refs
        acc = jnp.dot(a_ref[...].astype(BF16), b_ref[...].astype(BF16), preferred_element_type=F32)
        if has_res:
            acc = acc + r_ref[...].astype(F32)
        o_ref[...] = acc.astype(o_ref.dtype)

    a_bytes = M * K * a.dtype.itemsize
    b_bytes = K * N * b.dtype.itemsize
    rows_outer = a_bytes + (M // tm) * b_bytes <= b_bytes + (N // tn) * a_bytes
    if rows_outer:
        grid = (M // tm, N // tn)
        ia, ib, io = (lambda i, j: (i, 0)), (lambda i, j: (0, j)), (lambda i, j: (i, j))
    else:
        grid = (N // tn, M // tm)
        ia, ib, io = (lambda j, i: (i, 0)), (lambda j, i: (0, j)), (lambda j, i: (i, j))
    in_specs = [pl.BlockSpec((tm, K), ia), pl.BlockSpec((K, tn), ib)]
    args = [a, b]
    if has_res:
        in_specs.append(pl.BlockSpec((tm, tn), io))
        args.append(res)
    return pl.pallas_call(
        body, name=name, grid=grid, in_specs=in_specs,
        out_specs=pl.BlockSpec((tm, tn), io),
        out_shape=jax.ShapeDtypeStruct((M, N), out_dtype),
        compiler_params=_cparams(("parallel", "parallel")),
    )(*args)


def _mm_tn(a, b, *, name):
    T, K1 = a.shape
    N = b.shape[1]
    tt = _divisors(T, 16, MM_MAX_ROWS)[-1]
    tk = _divisors(K1, LANES, MM_MAX_COLS)[-1]
    tn = _divisors(N, LANES, MM_MAX_COLS)[-1]

    def body(a_ref, b_ref, o_ref):
        @pl.when(pl.program_id(2) == 0)
        def _():
            o_ref[...] = jnp.zeros_like(o_ref)

        o_ref[...] += lax.dot_general(a_ref[...].astype(BF16), b_ref[...].astype(BF16),
                                      (((0,), (0,)), ((), ())), preferred_element_type=F32)

    return pl.pallas_call(
        body, name=name, grid=(K1 // tk, N // tn, T // tt),
        in_specs=[pl.BlockSpec((tt, tk), lambda i, j, t: (t, i)),
                  pl.BlockSpec((tt, tn), lambda i, j, t: (t, j))],
        out_specs=pl.BlockSpec((tk, tn), lambda i, j, t: (i, j)),
        out_shape=jax.ShapeDtypeStruct((K1, N), F32),
        compiler_params=_cparams(("parallel", "parallel", "arbitrary")),
    )(a, b)


def _rows(tm, w, cb=0):
    return pl.BlockSpec((tm, w), lambda i: (i, cb))


def _const(shape):
    n = len(shape)
    return pl.BlockSpec(shape, lambda i: (0,) * n)


def _rmsnorm_fwd(x, g, *, name, tm=640):
    T, C = x.shape

    def body(x_ref, g_ref, o_ref):
        xv = x_ref[...]
        r = lax.rsqrt(jnp.mean(xv * xv, axis=-1, keepdims=True) + EPS)
        o_ref[...] = ((xv * r) * g_ref[...]).astype(BF16)

    return pl.pallas_call(
        body, name=name, grid=(T // tm,),
        in_specs=[_rows(tm, C), _const((1, C))],
        out_specs=_rows(tm, C),
        out_shape=jax.ShapeDtypeStruct((T, C), BF16),
        compiler_params=_cparams(("parallel",)),
    )(x, g)


def _rmsnorm_bwd(x, g, dy, res, *, name, tm=640, want_f32=True, want_bf16=True):
    T, C = x.shape
    has_res = res is not None

    def body(*refs):
        refs = list(refs)
        x_ref, g_ref, dy_ref = refs[:3]
        refs = refs[3:]
        r_ref = refs.pop(0) if has_res else None
        o32 = refs.pop(0) if want_f32 else None
        o16 = refs.pop(0) if want_bf16 else None
        dg_ref = refs.pop(0)

        @pl.when(pl.program_id(0) == 0)
        def _():
            dg_ref[...] = jnp.zeros_like(dg_ref)

        xv = x_ref[...]
        dyv = dy_ref[...].astype(F32)
        r = lax.rsqrt(jnp.mean(xv * xv, axis=-1, keepdims=True) + EPS)
        xn = xv * r
        dg_ref[...] += jnp.sum(dyv * xn, axis=0, keepdims=True)
        dxn = dyv * g_ref[...]
        dx = r * (dxn - xn * jnp.mean(dxn * xn, axis=-1, keepdims=True))
        if has_res:
            dx = dx + r_ref[...]
        if want_f32:
            o32[...] = dx
        if want_bf16:
            o16[...] = dx.astype(BF16)

    in_specs = [_rows(tm, C), _const((1, C)), _rows(tm, C)]
    args = [x, g, dy]
    if has_res:
        in_specs.append(_rows(tm, C))
        args.append(res)
    out_specs, out_shape = [], []
    if want_f32:
        out_specs.append(_rows(tm, C))
        out_shape.append(jax.ShapeDtypeStruct((T, C), F32))
    if want_bf16:
        out_specs.append(_rows(tm, C))
        out_shape.append(jax.ShapeDtypeStruct((T, C), BF16))
    out_specs.append(_const((1, C)))
    out_shape.append(jax.ShapeDtypeStruct((1, C), F32))
    return pl.pallas_call(
        body, name=name, grid=(T // tm,), in_specs=in_specs, out_specs=out_specs,
        out_shape=out_shape, compiler_params=_cparams(("arbitrary",)),
    )(*args)


def _gate_mix_fwd(um, bg, p_rnn, p_att, *, tm=320):
    T = um.shape[0]

    def body(um_ref, bg_ref, pr_ref, pa_ref, o_ref):
        g = _sigmoid(um_ref[...] + bg_ref[...])
        o_ref[...] = (g[:, :D] * pr_ref[...] + g[:, D:] * pa_ref[...]).astype(BF16)

    return pl.pallas_call(
        body, name="gate_mix_fwd", grid=(T // tm,),
        in_specs=[_rows(tm, 2 * D), _const((1, 2 * D)), _rows(tm, D), _rows(tm, D)],
        out_specs=_rows(tm, D),
        out_shape=jax.ShapeDtypeStruct((T, D), BF16),
        compiler_params=_cparams(("parallel",)),
    )(um, bg, p_rnn, p_att)


def _gate_mix_bwd(um, bg, p_rnn, p_att, dmixed, *, tm=320):
    T = um.shape[0]

    def body(um_ref, bg_ref, pr_ref, pa_ref, dm_ref, dpr_ref, dpa_ref, dum_ref, dbg_ref):
        @pl.when(pl.program_id(0) == 0)
        def _():
            dbg_ref[...] = jnp.zeros_like(dbg_ref)

        g = _sigmoid(um_ref[...] + bg_ref[...])
        g0, g1 = g[:, :D], g[:, D:]
        dm = dm_ref[...]
        dpr_ref[...] = (dm * g0).astype(BF16)
        dpa_ref[...] = (dm * g1).astype(BF16)
        d0 = dm * pr_ref[...] * g0 * (1.0 - g0)
        d1 = dm * pa_ref[...] * g1 * (1.0 - g1)
        dum_ref[:, :D] = d0.astype(BF16)
        dum_ref[:, D:] = d1.astype(BF16)
        dbg_ref[:, :D] += jnp.sum(d0, axis=0, keepdims=True)
        dbg_ref[:, D:] += jnp.sum(d1, axis=0, keepdims=True)

    return pl.pallas_call(
        body, name="gate_mix_bwd", grid=(T // tm,),
        in_specs=[_rows(tm, 2 * D), _const((1, 2 * D)), _rows(tm, D), _rows(tm, D), _rows(tm, D)],
        out_specs=[_rows(tm, D), _rows(tm, D), _rows(tm, 2 * D), _const((1, 2 * D))],
        out_shape=[jax.ShapeDtypeStruct((T, D), BF16), jax.ShapeDtypeStruct((T, D), BF16),
                   jax.ShapeDtypeStruct((T, 2 * D), BF16), jax.ShapeDtypeStruct((1, 2 * D), F32)],
        compiler_params=_cparams(("arbitrary",)),
    )(um, bg, p_rnn, p_att, dmixed)


def _swiglu_fwd(ff, *, tm=320):
    T = ff.shape[0]

    def body(g_ref, u_ref, o_ref):
        gv = g_ref[...]
        o_ref[...] = (gv * _sigmoid(gv) * u_ref[...]).astype(BF16)

    return pl.pallas_call(
        body, name="swiglu_fwd", grid=(T // tm,),
        in_specs=[_rows(tm, DFF, 0), _rows(tm, DFF, 1)],
        out_specs=_rows(tm, DFF),
        out_shape=jax.ShapeDtypeStruct((T, DFF), BF16),
        compiler_params=_cparams(("parallel",)),
    )(ff, ff)


def _swiglu_bwd(ff, dact, *, tm=320):
    T = ff.shape[0]

    def body(g_ref, u_ref, da_ref, o_ref):
        gv = g_ref[...]
        s = _sigmoid(gv)
        da = da_ref[...]
        o_ref[:, :DFF] = (da * u_ref[...] * s * (1.0 + gv * (1.0 - s))).astype(BF16)
        o_ref[:, DFF:] = (da * gv * s).astype(BF16)

    return pl.pallas_call(
        body, name="swiglu_bwd", grid=(T // tm,),
        in_specs=[_rows(tm, DFF, 0), _rows(tm, DFF, 1), _rows(tm, DFF)],
        out_specs=_rows(tm, 2 * DFF),
        out_shape=jax.ShapeDtypeStruct((T, 2 * DFF), BF16),
        compiler_params=_cparams(("parallel",)),
    )(ff, ff, dact)


def _loss_head(h2, tgt, g, *, tm=FRONT):
    T = h2.shape[0]
    front_blocks = FRONT // tm

    def body(h_ref, t_ref, g_ref, d32_ref, d16_ref, dg_ref, ls_ref):
        i = pl.program_id(0)

        @pl.when(i == 0)
        def _():
            dg_ref[...] = jnp.zeros_like(dg_ref)
            ls_ref[...] = jnp.zeros_like(ls_ref)

        xv = h_ref[...]
        r = lax.rsqrt(jnp.mean(xv * xv, axis=-1, keepdims=True) + EPS)
        xn = xv * r
        gv = g_ref[...]
        e = jnp.where(i >= front_blocks, xn * gv - t_ref[...], 0.0)
        ls_ref[...] += jnp.sum(e * e, axis=0, keepdims=True)
        dy = e * (1.0 / D)
        dg_ref[...] += jnp.sum(dy * xn, axis=0, keepdims=True)
        dxn = dy * gv
        dx = r * (dxn - xn * jnp.mean(dxn * xn, axis=-1, keepdims=True))
        d32_ref[...] = dx
        d16_ref[...] = dx.astype(BF16)

    return pl.pallas_call(
        body, name="loss_head", grid=(T // tm,),
        in_specs=[_rows(tm, D), pl.BlockSpec((tm, D), lambda i: (jnp.maximum(i - front_blocks, 0), 0)),
                  _const((1, D))],
        out_specs=[_rows(tm, D), _rows(tm, D), _const((1, D)), _const((1, D))],
        out_shape=[jax.ShapeDtypeStruct((T, D), F32), jax.ShapeDtypeStruct((T, D), BF16),
                   jax.ShapeDtypeStruct((1, D), F32), jax.ShapeDtypeStruct((1, D), F32)],
        compiler_params=_cparams(("arbitrary",)),
    )(h2, tgt, g)


def _scan_fwd(a, b, h_in):
    n = a.shape[0]
    row = lax.broadcasted_iota(jnp.int32, a.shape, 0)
    s = 1
    while s < n:
        a_sh = jnp.where(row >= s, pltpu.roll(a, s, 0), 1.0)
        b_sh = jnp.where(row >= s, pltpu.roll(b, s, 0), 0.0)
        b = a * b_sh + b
        a = a * a_sh
        s *= 2
    return b + a * h_in


def _scan_rev(a, b, g_in):
    n = a.shape[0]
    row = lax.broadcasted_iota(jnp.int32, a.shape, 0)
    s = 1
    while s < n:
        a_sh = jnp.where(row < n - s, pltpu.roll(a, n - s, 0), 1.0)
        b_sh = jnp.where(row < n - s, pltpu.roll(b, n - s, 0), 0.0)
        b = a * b_sh + b
        a = a * a_sh
        s *= 2
    return b + a * g_in


def _lru_gates(xc, wa, ba, wi, bi, lam):
    xcb = xc.astype(BF16)
    r = _sigmoid(jnp.dot(xcb, wa, preferred_element_type=F32) + ba)
    ig = _sigmoid(jnp.dot(xcb, wi, preferred_element_type=F32) + bi)
    log_sig = jnp.minimum(lam, 0.0) - jnp.log(1.0 + jnp.exp(-jnp.abs(lam)))
    log_a = LRU_C * r * log_sig
    a = jnp.exp(log_a)
    z = 2.0 * log_a
    poly = -z * (1.0 + z * (0.5 + z * (1.0 / 6.0 + z * (1.0 / 24.0))))
    m2 = jnp.where(z > -0.03, poly, 1.0 - jnp.exp(z))
    return r, ig, log_sig, a, jnp.sqrt(m2)


def _rnn_specs(tc, nblk_t, rev):
    def tmap(k):
        return (nblk_t - 1 - k) if rev else k

    hb = tc // 8
    blk = lambda off: pl.BlockSpec((tc, RB), lambda c, k: (tmap(k), c + off))
    halo = lambda off: pl.BlockSpec((8, RB), lambda c, k: (jnp.maximum(tmap(k) * hb - 1, 0), c + off))
    vec = pl.BlockSpec((1, RB), lambda c, k: (0, c))
    cwv = pl.BlockSpec((CW, RB), lambda c, k: (0, c))
    mat = pl.BlockSpec((None, RB, RB), lambda c, k: (c, 0, 0))
    return blk, halo, vec, cwv, mat


def _rnn_fwd(uxg, cw, cb, wa, ba, wi, bi, lam, *, tc=640):
    T = uxg.shape[0]
    nt = T // tc
    nsub = tc // SUB
    blk, halo, vec, cwv, mat = _rnn_specs(tc, nt, False)

    def body(x_ref, xh_ref, ug_ref, cw_ref, cb_ref, wa_ref, ba_ref, wi_ref, bi_ref, lam_ref,
             h_ref, y_ref, xb, hc):
        k = pl.program_id(1)

        @pl.when(k == 0)
        def _():
            hc[...] = jnp.zeros_like(hc)

        xb[0:8, :] = jnp.where(k > 0, xh_ref[...], 0.0)
        xb[8:, :] = x_ref[...]
        cwv_, cbv = cw_ref[...], cb_ref[...]
        wav, wiv = wa_ref[...], wi_ref[...]
        bav, biv, lamv = ba_ref[...], bi_ref[...], lam_ref[...]
        h_in = hc[0:1, :]
        for sc in range(nsub):
            r0 = sc * SUB
            xc = cbv + cwv_[0:1, :] * xb[pl.ds(5 + r0, SUB), :]
            for j in range(1, CW):
                xc = xc + cwv_[j:j + 1, :] * xb[pl.ds(5 + j + r0, SUB), :]
            r, ig, _, a, mm = _lru_gates(xc, wav, bav, wiv, biv, lamv)
            rows = k * tc + r0 + lax.broadcasted_iota(jnp.int32, (SUB, RB), 0)
            b = jnp.where(rows >= PAD, mm * (ig * xc), 0.0)
            h = _scan_fwd(a, b, h_in)
            h_in = h[SUB - 1:SUB, :]
            h_ref[pl.ds(r0, SUB), :] = h
            gl, _ = _gelu_parts(ug_ref[pl.ds(r0, SUB), :])
            y_ref[pl.ds(r0, SUB), :] = (h * gl).astype(BF16)
        hc[0:1, :] = h_in

    return pl.pallas_call(
        body, name="rnn_fwd", grid=(NBLK, nt),
        in_specs=[blk(0), halo(0), blk(NBLK), cwv, vec, mat, vec, mat, vec, vec],
        out_specs=[blk(0), blk(0)],
        out_shape=[jax.ShapeDtypeStruct((T, DR), F32), jax.ShapeDtypeStruct((T, DR), BF16)],
        scratch_shapes=[pltpu.VMEM((tc + 8, RB), F32), pltpu.VMEM((8, RB), F32)],
        compiler_params=_cparams(("parallel", "arbitrary")),
    )(uxg, uxg, uxg, cw, cb, wa, ba, wi, bi, lam)


def _rnn_bwd(uxg, hs, dy, cw, cb, wa, ba, wi, bi, lam, wat, wit, *, tc=640):
    T = uxg.shape[0]
    nt = T // tc
    nsub = tc // SUB
    blk, halo, vec, cwv, mat = _rnn_specs(tc, nt, True)

    def body(x_ref, xh_ref, ug_ref, h_ref, hh_ref, dy_ref, cw_ref, cb_ref, wa_ref, ba_ref, wi_ref,
             bi_ref, lam_ref, wat_ref, wit_ref,
             dux_ref, dug_ref, dcw_ref, dcb_ref, dwa_ref, dba_ref, dwi_ref, dbi_ref, dlam_ref,
             xb, hb, ab, dxb, xcs, rs, igs, mms, dgas, dgis, carry):
        k = pl.program_id(1)
        kt = nt - 1 - k

        @pl.when(k == 0)
        def _():
            carry[...] = jnp.zeros_like(carry)
            for ref in (dcw_ref, dcb_ref, dwa_ref, dba_ref, dwi_ref, dbi_ref, dlam_ref):
                ref[...] = jnp.zeros_like(ref)

        xb[0:8, :] = jnp.where(kt > 0, xh_ref[...], 0.0)
        xb[8:, :] = x_ref[...]
        hb[0:8, :] = jnp.where(kt > 0, hh_ref[...], 0.0)
        hb[8:, :] = h_ref[...]
        cwv_, cbv = cw_ref[...], cb_ref[...]
        wav, wiv = wa_ref[...], wi_ref[...]
        bav, biv, lamv = ba_ref[...], bi_ref[...], lam_ref[...]
        ab[tc:tc + 8, :] = jnp.broadcast_to(carry[1:2, :], (8, RB))
        dxb[tc:tc + 8, :] = carry[8:16, :]
        log_sig = None
        for sc in range(nsub):
            r0 = sc * SUB
            xc = cbv + cwv_[0:1, :] * xb[pl.ds(5 + r0, SUB), :]
            for j in range(1, CW):
                xc = xc + cwv_[j:j + 1, :] * xb[pl.ds(5 + j + r0, SUB), :]
            r, ig, log_sig, a, mm = _lru_gates(xc, wav, bav, wiv, biv, lamv)
            xcs[pl.ds(r0, SUB), :] = xc
            rs[pl.ds(r0, SUB), :] = r
            igs[pl.ds(r0, SUB), :] = ig
            mms[pl.ds(r0, SUB), :] = mm
            ab[pl.ds(r0, SUB), :] = a
        sig_neg = _sigmoid(-lamv)
        g_in = carry[0:1, :]
        dlam_acc = jnp.zeros((1, RB), F32)
        for sc in reversed(range(nsub)):
            r0 = sc * SUB
            xc, r, ig, mm = xcs[pl.ds(r0, SUB), :], rs[pl.ds(r0, SUB), :], igs[pl.ds(r0, SUB), :], mms[pl.ds(r0, SUB), :]
            a = ab[pl.ds(r0, SUB), :]
            a_next = ab[pl.ds(r0 + 1, SUB), :]
            hv = hb[pl.ds(8 + r0, SUB), :]
            hprev = hb[pl.ds(7 + r0, SUB), :]
            dyv = dy_ref[pl.ds(r0, SUB), :]
            gl, dgl = _gelu_parts(ug_ref[pl.ds(r0, SUB), :])
            dug_ref[pl.ds(r0, SUB), :] = (dyv * hv * dgl).astype(BF16)
            G = _scan_rev(a_next, dyv * gl, g_in)
            g_in = G[0:1, :]
            rows = kt * tc + r0 + lax.broadcasted_iota(jnp.int32, (SUB, RB), 0)
            db = jnp.where(rows >= PAD, G, 0.0)
            da = G * hprev
            dmm = db * (ig * xc)
            di = db * (mm * xc)
            dxc = db * (mm * ig)
            dlog_a = da * a - dmm * (a * a) / jnp.maximum(mm, 1e-30)
            dr = dlog_a * (LRU_C * log_sig)
            dlam_acc = dlam_acc + jnp.sum(dlog_a * (LRU_C * r), axis=0, keepdims=True)
            dga = dr * r * (1.0 - r)
            dgi = di * ig * (1.0 - ig)
            dgab, dgib = dga.astype(BF16), dgi.astype(BF16)
            dgas[pl.ds(r0, SUB), :] = dgab
            dgis[pl.ds(r0, SUB), :] = dgib
            dba_ref[...] += jnp.sum(dga, axis=0, keepdims=True)
            dbi_ref[...] += jnp.sum(dgi, axis=0, keepdims=True)
            dxc = dxc + jnp.dot(dgab, wat_ref[...], preferred_element_type=F32) \
                + jnp.dot(dgib, wit_ref[...], preferred_element_type=F32)
            dxb[pl.ds(r0, SUB), :] = dxc
        dlam_ref[...] += dlam_acc * sig_neg
        xcb = xcs[...].astype(BF16)
        tn = (((0,), (0,)), ((), ()))
        dwa_ref[...] += lax.dot_general(xcb, dgas[...], tn, preferred_element_type=F32)
        dwi_ref[...] += lax.dot_general(xcb, dgis[...], tn, preferred_element_type=F32)
        dxc_all = dxb[0:tc, :]
        dcb_ref[...] += jnp.sum(dxc_all, axis=0, keepdims=True)
        rows_all = kt * tc + lax.broadcasted_iota(jnp.int32, (tc, RB), 0)
        dux = jnp.zeros((tc, RB), F32)
        for j in range(CW):
            dcw_ref[j:j + 1, :] += jnp.sum(dxc_all * xb[pl.ds(5 + j, tc), :], axis=0, keepdims=True)
            dux = dux + cwv_[j:j + 1, :] * dxb[pl.ds(CW - 1 - j, tc), :]
        dux_ref[...] = jnp.where(rows_all >= PAD, dux, 0.0).astype(BF16)
        carry[0:1, :] = g_in
        carry[1:2, :] = ab[0:1, :]
        carry[8:16, :] = dxb[0:8, :]

    vec_out = pl.BlockSpec((1, RB), lambda c, k: (0, c))
    return pl.pallas_call(
        body, name="rnn_bwd", grid=(NBLK, nt),
        in_specs=[blk(0), halo(0), blk(NBLK), blk(0), halo(0), blk(0), cwv, vec, mat, vec, mat, vec, vec, mat, mat],
        out_specs=[blk(0), blk(0), cwv, vec_out, mat, vec_out, mat, vec_out, vec_out],
        out_shape=[jax.ShapeDtypeStruct((T, DR), BF16), jax.ShapeDtypeStruct((T, DR), BF16),
                   jax.ShapeDtypeStruct((CW, DR), F32), jax.ShapeDtypeStruct((1, DR), F32),
                   jax.ShapeDtypeStruct((NBLK, RB, RB), F32), jax.ShapeDtypeStruct((1, DR), F32),
                   jax.ShapeDtypeStruct((NBLK, RB, RB), F32), jax.ShapeDtypeStruct((1, DR), F32),
                   jax.ShapeDtypeStruct((1, DR), F32)],
        scratch_shapes=[pltpu.VMEM((tc + 8, RB), F32), pltpu.VMEM((tc + 8, RB), F32),
                        pltpu.VMEM((tc + 8, RB), F32), pltpu.VMEM((tc + 8, RB), F32),
                        pltpu.VMEM((tc, RB), F32), pltpu.VMEM((tc, RB), F32), pltpu.VMEM((tc, RB), F32),
                        pltpu.VMEM((tc, RB), F32), pltpu.VMEM((tc, RB), BF16), pltpu.VMEM((tc, RB), BF16),
                        pltpu.VMEM((16, RB), F32)],
        compiler_params=_cparams(("parallel", "arbitrary")),
    )(uxg, uxg, uxg, hs, hs, dy, cw, cb, wa, ba, wi, bi, lam, wat, wit)


def _attn_prep(q_all, kv_all, ukr, tab, *, tm=320):
    T = q_all.shape[0]

    def body(q_ref, kv_ref, kr_ref, tab_ref, qo_ref, ko_ref, vo_ref):
        tab_v = tab_ref[...]
        lane = lax.broadcasted_iota(jnp.int32, (tm, LANES), 1)
        t1 = kr_ref[...] * tab_v
        kro = jnp.where(lane < ROPE, t1 + pltpu.roll(t1, ROPE, 1), 0.0).astype(BF16)
        for h in range(NH):
            c0 = h * QW
            qo_ref[h, :, 0:NOPE] = (q_ref[:, c0:c0 + NOPE] * SCALE).astype(BF16)
            t2 = q_ref[:, c0 + NOPE:c0 + QW] * tab_v
            qo_ref[h, :, NOPE:QW] = ((t2 + pltpu.roll(t2, ROPE, 1)) * SCALE).astype(BF16)
            ko_ref[h, :, 0:NOPE] = kv_ref[:, c0:c0 + NOPE].astype(BF16)
            ko_ref[h, :, NOPE:QW] = kro
            vo_ref[h, :, :] = kv_ref[:, c0 + NOPE:c0 + QW].astype(BF16)

    return pl.pallas_call(
        body, name="attn_prep", grid=(T // tm,),
        in_specs=[_rows(tm, NH * QW), _rows(tm, NH * QW), _rows(tm, LANES), _rows(tm, LANES)],
        out_specs=[pl.BlockSpec((NH, tm, QW), lambda i: (0, i, 0)), pl.BlockSpec((NH, tm, QW), lambda i: (0, i, 0)),
                   pl.BlockSpec((NH, tm, VD), lambda i: (0, i, 0))],
        out_shape=[jax.ShapeDtypeStruct((NH, T, QW), BF16), jax.ShapeDtypeStruct((NH, T, QW), BF16),
                   jax.ShapeDtypeStruct((NH, T, VD), BF16)],
        compiler_params=_cparams(("parallel",)),
    )(q_all, kv_all, ukr, tab)


def _attn_prep_bwd(dq, dk, dv, tab, *, tm=320):
    T = dq.shape[1]

    def body(dq_ref, dk_ref, dv_ref, tab_ref, dqa_ref, dkva_ref, dkr_ref):
        tab_v = tab_ref[...]
        lane = lax.broadcasted_iota(jnp.int32, (tm, LANES), 1)
        dkro = jnp.zeros((tm, LANES), F32)
        for h in range(NH):
            c0 = h * QW
            dqa_ref[:, c0:c0 + NOPE] = (dq_ref[h, :, 0:NOPE] * SCALE).astype(BF16)
            d2 = dq_ref[h, :, NOPE:QW]
            dqa_ref[:, c0 + NOPE:c0 + QW] = ((d2 + pltpu.roll(d2, ROPE, 1)) * tab_v * SCALE).astype(BF16)
            dkva_ref[:, c0:c0 + NOPE] = dk_ref[h, :, 0:NOPE].astype(BF16)
            dkva_ref[:, c0 + NOPE:c0 + QW] = dv_ref[h, :, :].astype(BF16)
            dkro = dkro + dk_ref[h, :, NOPE:QW]
        dkro = jnp.where(lane < ROPE, dkro, 0.0)
        dkr_ref[...] = ((dkro + pltpu.roll(dkro, ROPE, 1)) * tab_v).astype(BF16)

    return pl.pallas_call(
        body, name="attn_prep_bwd", grid=(T // tm,),
        in_specs=[pl.BlockSpec((NH, tm, QW), lambda i: (0, i, 0)), pl.BlockSpec((NH, tm, QW), lambda i: (0, i, 0)),
                  pl.BlockSpec((NH, tm, VD), lambda i: (0, i, 0)), _rows(tm, LANES)],
        out_specs=[_rows(tm, NH * QW), _rows(tm, NH * QW), _rows(tm, LANES)],
        out_shape=[jax.ShapeDtypeStruct((T, NH * QW), BF16), jax.ShapeDtypeStruct((T, NH * QW), BF16),
                   jax.ShapeDtypeStruct((T, LANES), BF16)],
        compiler_params=_cparams(("parallel",)),
    )(dq, dk, dv, tab)


def _visible(q0, k0, nq, nk):
    rows = q0 + lax.broadcasted_iota(jnp.int32, (nq, nk), 0)
    cols = k0 + lax.broadcasted_iota(jnp.int32, (nq, nk), 1)
    return ((cols >> 6) <= (rows >> 6)) & (cols >= PAD)


def _visible_t(q0, k0, nq, nk):
    cols = k0 + lax.broadcasted_iota(jnp.int32, (nk, nq), 0)
    rows = q0 + lax.broadcasted_iota(jnp.int32, (nk, nq), 1)
    return ((cols >> 6) <= (rows >> 6)) & (cols >= PAD)


_NT = (((1,), (1,)), ((), ()))
ATTN_BLOCK = 1664


def _attn_block(T):
    return ATTN_BLOCK if T % ATTN_BLOCK == 0 else 640


def _flash_fwd(q, k, v, *, bq=None):
    T = q.shape[1]
    bq = bq or _attn_block(T)
    nq = T // bq
    rs = bq // CHAINS

    def body(q_ref, k_ref, v_ref, o_ref, lse_ref, *scr):
        i = pl.program_id(1)
        m_s, l_s, acc_s = scr[:CHAINS], scr[CHAINS:2 * CHAINS], scr[2 * CHAINS:]
        for r in range(CHAINS):
            m_s[r][...] = jnp.full_like(m_s[r], NEG)
            l_s[r][...] = jnp.zeros_like(l_s[r])
            acc_s[r][...] = jnp.zeros_like(acc_s[r])

        def step(j, masked):
            off = pl.multiple_of(j * bq, bq)
            kv_ = k_ref[pl.ds(off, bq), :]
            vv = v_ref[pl.ds(off, bq), :]
            for r in range(CHAINS):
                rows = pl.ds(r * rs, rs)
                s = lax.dot_general(q_ref[rows, :], kv_, _NT, preferred_element_type=F32)
                if masked:
                    s = jnp.where(_visible(i * bq + r * rs, j * bq, rs, bq), s, NEG)
                m_prev = m_s[r][...]
                m_new = jnp.maximum(m_prev, jnp.max(s, axis=-1, keepdims=True))
                p = jnp.exp(s - m_new)
                alpha = jnp.exp(m_prev - m_new)
                l_s[r][...] = alpha * l_s[r][...] + jnp.sum(p, axis=-1, keepdims=True)
                acc_s[r][...] = alpha * acc_s[r][...] + jnp.dot(p.astype(BF16), vv, preferred_element_type=F32)
                m_s[r][...] = m_new

        step(0, True)

        def loop(j, c):
            step(j, False)
            return c

        lax.fori_loop(1, i, loop, 0)

        @pl.when(i > 0)
        def _():
            step(i, True)

        for r in range(CHAINS):
            rows = pl.ds(r * rs, rs)
            o_ref[rows, :] = (acc_s[r][...] / l_s[r][...]).astype(BF16)
            lse_ref[rows, :] = m_s[r][...] + jnp.log(l_s[r][...])

    return pl.pallas_call(
        body, name="flash_fwd", grid=(NH, nq),
        in_specs=[pl.BlockSpec((None, bq, QW), lambda h, i: (h, i, 0)),
                  pl.BlockSpec((None, T, QW), lambda h, i: (h, 0, 0)),
                  pl.BlockSpec((None, T, VD), lambda h, i: (h, 0, 0))],
        out_specs=[pl.BlockSpec((bq, VD), lambda h, i: (i, h)),
                   pl.BlockSpec((None, bq, 1), lambda h, i: (h, i, 0))],
        out_shape=[jax.ShapeDtypeStruct((T, NH * VD), BF16), jax.ShapeDtypeStruct((NH, T, 1), F32)],
        scratch_shapes=[pltpu.VMEM((rs, 1), F32)] * (2 * CHAINS) + [pltpu.VMEM((rs, VD), F32)] * CHAINS,
        compiler_params=_cparams(("parallel", "parallel")),
    )(q, k, v)


def _attn_delta(o, do, *, tm=640):
    T = o.shape[0]

    def body(o_ref, do_ref, d_ref):
        prod = o_ref[...].astype(F32) * do_ref[...].astype(F32)
        for h in range(NH):
            d_ref[h, :, :] = jnp.sum(prod[:, h * VD:(h + 1) * VD], axis=-1, keepdims=True)

    return pl.pallas_call(
        body, name="attn_delta", grid=(T // tm,),
        in_specs=[_rows(tm, NH * VD), _rows(tm, NH * VD)],
        out_specs=pl.BlockSpec((NH, tm, 1), lambda i: (0, i, 0)),
        out_shape=jax.ShapeDtypeStruct((NH, T, 1), F32),
        compiler_params=_cparams(("parallel",)),
    )(o, do)


def _flash_bwd_dq(q, k, v, do, lse, delta, *, bq=None):
    T = q.shape[1]
    bq = bq or _attn_block(T)
    nq = T // bq
    rs = bq // CHAINS

    def body(q_ref, k_ref, v_ref, do_ref, lse_ref, dl_ref, dq_ref):
        i = pl.program_id(1)
        dq_ref[...] = jnp.zeros_like(dq_ref)

        def step(j, masked):
            off = pl.multiple_of(j * bq, bq)
            kv_ = k_ref[pl.ds(off, bq), :]
            vv = v_ref[pl.ds(off, bq), :]
            for r in range(CHAINS):
                rows = pl.ds(r * rs, rs)
                s = lax.dot_general(q_ref[rows, :], kv_, _NT, preferred_element_type=F32)
                if masked:
                    s = jnp.where(_visible(i * bq + r * rs, j * bq, rs, bq), s, NEG)
                p = jnp.exp(s - lse_ref[rows, :])
                dp = lax.dot_general(do_ref[rows, :], vv, _NT, preferred_element_type=F32)
                ds = (p * (dp - dl_ref[rows, :])).astype(BF16)
                dq_ref[rows, :] += jnp.dot(ds, kv_, preferred_element_type=F32)

        step(0, True)

        def loop(j, c):
            step(j, False)
            return c

        lax.fori_loop(1, i, loop, 0)

        @pl.when(i > 0)
        def _():
            step(i, True)

    return pl.pallas_call(
        body, name="flash_bwd_dq", grid=(NH, nq),
        in_specs=[pl.BlockSpec((None, bq, QW), lambda h, i: (h, i, 0)),
                  pl.BlockSpec((None, T, QW), lambda h, i: (h, 0, 0)),
                  pl.BlockSpec((None, T, VD), lambda h, i: (h, 0, 0)),
                  pl.BlockSpec((bq, VD), lambda h, i: (i, h)),
                  pl.BlockSpec((None, bq, 1), lambda h, i: (h, i, 0)),
                  pl.BlockSpec((None, bq, 1), lambda h, i: (h, i, 0))],
        out_specs=pl.BlockSpec((None, bq, QW), lambda h, i: (h, i, 0)),
        out_shape=jax.ShapeDtypeStruct((NH, T, QW), F32),
        compiler_params=_cparams(("parallel", "parallel")),
    )(q, k, v, do, lse, delta)


def _flash_bwd_dkv(q, k, v, do, lse_row, delta_row, *, bq=None):
    T = q.shape[1]
    bq = bq or _attn_block(T)
    nq = T // bq
    rs = bq // CHAINS

    def body(q_ref, k_ref, v_ref, do_ref, lse_ref, dl_ref, dk_ref, dv_ref):
        j = pl.program_id(1)
        dk_ref[...] = jnp.zeros_like(dk_ref)
        dv_ref[...] = jnp.zeros_like(dv_ref)

        def step(i, masked):
            off = pl.multiple_of(i * bq, bq)
            qv = q_ref[pl.ds(off, bq), :]
            dov = do_ref[pl.ds(off, bq), :]
            lse_v = lse_ref[:, pl.ds(off, bq)]
            dl_v = dl_ref[:, pl.ds(off, bq)]
            for r in range(CHAINS):
                rows = pl.ds(r * rs, rs)
                st = lax.dot_general(k_ref[rows, :], qv, _NT, preferred_element_type=F32)
                if masked:
                    st = jnp.where(_visible_t(i * bq, j * bq + r * rs, bq, rs), st, NEG)
                pt = jnp.exp(st - lse_v)
                dv_ref[rows, :] += jnp.dot(pt.astype(BF16), dov, preferred_element_type=F32)
                dpt = lax.dot_general(v_ref[rows, :], dov, _NT, preferred_element_type=F32)
                dst = (pt * (dpt - dl_v)).astype(BF16)
                dk_ref[rows, :] += jnp.dot(dst, qv, preferred_element_type=F32)

        step(j, True)

        @pl.when(j == 0)
        def _():
            def loop(i, c):
                step(i, True)
                return c
            lax.fori_loop(1, nq, loop, 0)

        @pl.when(j > 0)
        def _():
            def loop(i, c):
                step(i, False)
                return c
            lax.fori_loop(j + 1, nq, loop, 0)

    return pl.pallas_call(
        body, name="flash_bwd_dkv", grid=(NH, nq),
        in_specs=[pl.BlockSpec((None, T, QW), lambda h, j: (h, 0, 0)),
                  pl.BlockSpec((None, bq, QW), lambda h, j: (h, j, 0)),
                  pl.BlockSpec((None, bq, VD), lambda h, j: (h, j, 0)),
                  pl.BlockSpec((T, VD), lambda h, j: (0, h)),
                  pl.BlockSpec((None, 1, T), lambda h, j: (h, 0, 0)),
                  pl.BlockSpec((None, 1, T), lambda h, j: (h, 0, 0))],
        out_specs=[pl.BlockSpec((None, bq, QW), lambda h, j: (h, j, 0)),
                   pl.BlockSpec((None, bq, VD), lambda h, j: (h, j, 0))],
        out_shape=[jax.ShapeDtypeStruct((NH, T, QW), F32), jax.ShapeDtypeStruct((NH, T, VD), F32)],
        compiler_params=_cparams(("parallel", "parallel")),
    )(q, k, v, do, lse_row, delta_row)


def _rope_table(T):
    pos = (jnp.arange(T, dtype=jnp.int32) - PAD).astype(F32)
    inv_freq = ROPE_THETA ** (-jnp.arange(0, ROPE, 2, dtype=F32) / ROPE)
    ang = pos[:, None] * inv_freq[None, :]
    cos, sin = jnp.cos(ang), jnp.sin(ang)
    return jnp.concatenate([cos, cos, -sin, sin], axis=1)


def _swap_halves(w):
    return jnp.concatenate([w[..., ROPE // 2:], w[..., :ROPE // 2]], axis=-1)


O_UX, O_UG, O_UQ, O_UKV, O_UKR, O_UM = 0, DR, 2 * DR, 2 * DR + QR, 2 * DR + QR + KVR, 2 * DR + QR + KVR + ROPE


def _prep_weights(w):
    b = lambda a: a.astype(BF16)
    w_in = w["w_in"]
    kr = w_in[:, O_UKR:O_UM]
    p = {
        "w_xg": b(w_in[:, :O_UQ]),
        "w_q": b(w_in[:, O_UQ:O_UKV]),
        "w_kv": b(w_in[:, O_UKV:O_UKR]),
        "w_kr": b(jnp.concatenate([kr, _swap_halves(kr)], axis=1)),
        "w_m": b(w_in[:, O_UM:]),
    }
    wq = w["w_uq"].reshape(QR, NH, NOPE + ROPE)
    p["w_uq"] = b(jnp.concatenate([wq, _swap_halves(wq[..., NOPE:])], axis=-1).reshape(QR, NH * QW))
    p["w_ukv"] = b(w["w_ukv"])
    p["w_br"] = b(w["w_branch"][:DR])
    p["w_ba"] = b(w["w_branch"][DR:])
    p["w_out"] = b(w["w_out"])
    p["w_fi"] = b(w["w_ffn_in"])
    p["w_fo"] = b(w["w_ffn_out"])
    for n in ("w_xg", "w_q", "w_kv", "w_kr", "w_m", "w_uq", "w_ukv", "w_br", "w_ba", "w_out", "w_fi", "w_fo"):
        p[n + "_t"] = p[n].T
    p["wa"] = b(w["w_rec_a"])
    p["wi"] = b(w["w_rec_i"])
    p["wa_t"] = jnp.swapaxes(p["wa"], 1, 2)
    p["wi_t"] = jnp.swapaxes(p["wi"], 1, 2)
    return p


def _local_step(x, tgt, w):
    S = x.shape[0]
    T = FRONT + S
    p = _prep_weights(w)
    tab = _rope_table(T)
    h0 = jnp.concatenate([jnp.zeros((PAD, D), F32), w["meta_tokens"], x], axis=0)
    row = lambda v: v.reshape(1, -1)

    z = _rmsnorm_fwd(h0, row(w["norm_mix_g"]), name="norm_mix")
    uxg = _mm(z, p["w_xg"], name="mm_uxg")
    uq = _mm(z, p["w_q"], name="mm_uq")
    ukv = _mm(z, p["w_kv"], name="mm_ukv")
    ukr = _mm(z, p["w_kr"], name="mm_ukr")
    um = _mm(z, p["w_m"], name="mm_um")
    rnn_w = (w["conv_w"], row(w["conv_b"]), p["wa"], row(w["b_rec_a"]), p["wi"], row(w["b_rec_i"]),
             row(w["lru_lambda"]))
    hs, y_rnn = _rnn_fwd(uxg, *rnn_w)
    qn = _rmsnorm_fwd(uq, row(w["q_norm_g"]), name="norm_q")
    kvn = _rmsnorm_fwd(ukv, row(w["kv_norm_g"]), name="norm_kv")
    q_all = _mm(qn, p["w_uq"], name="mm_q")
    kv_all = _mm(kvn, p["w_ukv"], name="mm_kv")
    qh, kh, vh = _attn_prep(q_all, kv_all, ukr, tab)
    y_att, lse = _flash_fwd(qh, kh, vh)
    p_rnn = _mm(y_rnn, p["w_br"], name="mm_prnn")
    p_att = _mm(y_att, p["w_ba"], name="mm_patt")
    bg = row(w["b_gate"])
    mixed = _gate_mix_fwd(um, bg, p_rnn, p_att)
    h1 = _mm(mixed, p["w_out"], name="mm_out", res=h0)
    zf = _rmsnorm_fwd(h1, row(w["norm_ffn_g"]), name="norm_ffn")
    ff = _mm(zf, p["w_fi"], name="mm_ffn_in")
    act = _swiglu_fwd(ff)
    h2 = _mm(act, p["w_fo"], name="mm_ffn_out", res=h1)

    g = {}
    dh2, dh2b, dg_fin, lsum = _loss_head(h2, tgt, row(w["final_norm_g"]))
    loss = 0.5 * jnp.sum(lsum) / D
    g["final_norm_g"] = dg_fin.reshape(-1)
    dact = _mm(dh2b, p["w_fo_t"], name="mm_dact")
    g["w_ffn_out"] = _mm_tn(act, dh2b, name="mm_dw_ffn_out")
    dff = _swiglu_bwd(ff, dact)
    dzf = _mm(dff, p["w_fi_t"], name="mm_dzf")
    g["w_ffn_in"] = _mm_tn(zf, dff, name="mm_dw_ffn_in")
    dh1, dh1b, dg = _rmsnorm_bwd(h1, row(w["norm_ffn_g"]), dzf, dh2, name="norm_ffn_bwd")
    g["norm_ffn_g"] = dg
    dmixed = _mm(dh1b, p["w_out_t"], name="mm_dmixed")
    g["w_out"] = _mm_tn(mixed, dh1b, name="mm_dw_out")
    dp_rnn, dp_att, dum, dbg = _gate_mix_bwd(um, bg, p_rnn, p_att, dmixed)
    g["b_gate"] = dbg.reshape(2, D)
    dy_rnn = _mm(dp_rnn, p["w_br_t"], name="mm_dy_rnn")
    dy_att = _mm(dp_att, p["w_ba_t"], name="mm_dy_att", out_dtype=BF16)
    g["w_branch"] = jnp.concatenate([_mm_tn(y_rnn, dp_rnn, name="mm_dw_br"),
                                     _mm_tn(y_att, dp_att, name="mm_dw_ba")], axis=0)
    delta = _attn_delta(y_att, dy_att)
    dq = _flash_bwd_dq(qh, kh, vh, dy_att, lse, delta)
    dk, dv = _flash_bwd_dkv(qh, kh, vh, dy_att, lse.reshape(NH, 1, T), delta.reshape(NH, 1, T))
    dq_all, dkv_all, dukr = _attn_prep_bwd(dq, dk, dv, tab)
    dqn = _mm(dq_all, p["w_uq_t"], name="mm_dqn")
    dkvn = _mm(dkv_all, p["w_ukv_t"], name="mm_dkvn")
    dwq = _mm_tn(qn, dq_all, name="mm_dw_uq").reshape(QR, NH, QW)
    dwq_rope = dwq[..., NOPE:NOPE + ROPE] + _swap_halves(dwq[..., NOPE + ROPE:])
    g["w_uq"] = jnp.concatenate([dwq[..., :NOPE], dwq_rope], axis=-1).reshape(QR, NH * (NOPE + ROPE))
    g["w_ukv"] = _mm_tn(kvn, dkv_all, name="mm_dw_ukv")
    duq, dg = _rmsnorm_bwd(uq, row(w["q_norm_g"]), dqn, None, name="norm_q_bwd", want_f32=False)
    g["q_norm_g"] = dg
    dukv, dg = _rmsnorm_bwd(ukv, row(w["kv_norm_g"]), dkvn, None, name="norm_kv_bwd", want_f32=False)
    g["kv_norm_g"] = dg
    (dux, dug, g["conv_w"], g["conv_b"], g["w_rec_a"], g["b_rec_a"], g["w_rec_i"], g["b_rec_i"],
     g["lru_lambda"]) = _rnn_bwd(uxg, hs, dy_rnn, *rnn_w, p["wa_t"], p["wi_t"])
    dz = _mm(dux, p["w_xg_t"][:DR], name="mm_dz_x")
    dz = _mm(dug, p["w_xg_t"][DR:], name="mm_dz_g", res=dz)
    dz = _mm(duq, p["w_q_t"], name="mm_dz_q", res=dz)
    dz = _mm(dukv, p["w_kv_t"], name="mm_dz_kv", res=dz)
    dz = _mm(dukr, p["w_kr_t"], name="mm_dz_kr", res=dz)
    dz = _mm(dum, p["w_m_t"], name="mm_dz_m", res=dz)
    dwkr = _mm_tn(z, dukr, name="mm_dw_kr")
    g["w_in"] = jnp.concatenate([
        _mm_tn(z, dux, name="mm_dw_x"), _mm_tn(z, dug, name="mm_dw_g"),
        _mm_tn(z, duq, name="mm_dw_q"), _mm_tn(z, dukv, name="mm_dw_kv"),
        dwkr[:, :ROPE] + _swap_halves(dwkr[:, ROPE:]),
        _mm_tn(z, dum, name="mm_dw_m")], axis=1)
    dh0, dg = _rmsnorm_bwd(h0, row(w["norm_mix_g"]), dz, dh1, name="norm_mix_bwd", want_bf16=False)
    g["norm_mix_g"] = dg
    g["meta_tokens"] = dh0[PAD:FRONT]
    return loss, dh0[FRONT:], g


HBM = pl.BlockSpec(memory_space=pltpu.HBM)
CHIP_FLIPS = ((1, 0), (0, 1), (1, 1))


def _place():
    return lax.axis_index("x"), lax.axis_index("y"), lax.axis_index("c")


def _flip(v, f):
    return 1 - v if f else v


def _dma_sems(n):
    return pltpu.SemaphoreType.DMA((n,))


def _allgather_chips(srcs, *, name):
    n = len(srcs)

    def body(*refs):
        src_refs, out_refs, stage = refs[:n], refs[n:2 * n], refs[2 * n:3 * n]
        send_sems, recv_sems, in_sems, local_sems = refs[3 * n:]
        x, y, c = _place()
        me = 2 * x + y
        loads = [pltpu.make_async_copy(src_refs[a], stage[a], in_sems.at[a]) for a in range(n)]
        for cp in loads:
            cp.start()
        copies = []
        for a in range(n):
            loads[a].wait()
            for k, (fx, fy) in enumerate(CHIP_FLIPS):
                cp = pltpu.make_async_remote_copy(
                    src_ref=stage[a], dst_ref=out_refs[a].at[me], send_sem=send_sems.at[3 * a + k],
                    recv_sem=recv_sems.at[3 * a + k], device_id=(_flip(x, fx), _flip(y, fy), c),
                    device_id_type=MESH)
                cp.start()
                copies.append(cp)
            cp = pltpu.make_async_copy(stage[a], out_refs[a].at[me], local_sems.at[a])
            cp.start()
            copies.append(cp)
        for cp in copies:
            cp.wait()

    return pl.pallas_call(
        body, name=name, in_specs=[HBM] * n, out_specs=[HBM] * n,
        out_shape=[jax.ShapeDtypeStruct((4,) + s.shape, s.dtype) for s in srcs],
        scratch_shapes=[pltpu.VMEM(s.shape, s.dtype) for s in srcs]
        + [_dma_sems(3 * n), _dma_sems(3 * n), _dma_sems(n), _dma_sems(n)],
        compiler_params=pltpu.CompilerParams(vmem_limit_bytes=VMEM_LIMIT),
    )(*srcs)


def _scatter_chips(srcs, *, name):
    n = len(srcs)

    def body(*refs):
        src_refs, out_refs = refs[:n], refs[n:2 * n]
        send_sems, recv_sems = refs[2 * n:]
        x, y, c = _place()
        copies = []
        for a in range(n):
            for k, (fx, fy) in enumerate(CHIP_FLIPS):
                px, py = _flip(x, fx), _flip(y, fy)
                cp = pltpu.make_async_remote_copy(
                    src_ref=src_refs[a].at[2 * px + py], dst_ref=out_refs[a].at[k],
                    send_sem=send_sems.at[3 * a + k], recv_sem=recv_sems.at[3 * a + k],
                    device_id=(px, py, c), device_id_type=MESH)
                cp.start()
                copies.append(cp)
        for cp in copies:
            cp.wait()

    return pl.pallas_call(
        body, name=name, in_specs=[HBM] * n, out_specs=[HBM] * n,
        out_shape=[jax.ShapeDtypeStruct((3,) + s.shape[1:], s.dtype) for s in srcs],
        scratch_shapes=[_dma_sems(3 * n), _dma_sems(3 * n)],
    )(*srcs)


def _sibling_take(srcs, *, name):
    n = len(srcs)

    def body(*refs):
        src_refs, out_refs = refs[:n], refs[n:2 * n]
        send_sems, recv_sems = refs[2 * n:]
        x, y, c = _place()
        copies = []
        for a in range(n):
            h = srcs[a].shape[1] // 2
            theirs = pl.ds(pl.multiple_of((1 - c) * h, 8), h)
            cp = pltpu.make_async_remote_copy(
                src_ref=src_refs[a].at[:, theirs, :], dst_ref=out_refs[a], send_sem=send_sems.at[a],
                recv_sem=recv_sems.at[a], device_id=(x, y, 1 - c), device_id_type=MESH)
            cp.start()
            copies.append(cp)
        for cp in copies:
            cp.wait()

    return pl.pallas_call(
        body, name=name, in_specs=[HBM] * n, out_specs=[HBM] * n,
        out_shape=[jax.ShapeDtypeStruct((4, s.shape[1] // 2, s.shape[2]), s.dtype) for s in srcs],
        scratch_shapes=[_dma_sems(n), _dma_sems(n)],
    )(*srcs)


def _sibling_swap(srcs, *, name):
    n = len(srcs)

    def body(*refs):
        src_refs, out_refs = refs[:n], refs[n:2 * n]
        send_sems, recv_sems = refs[2 * n:]
        x, y, c = _place()
        copies = []
        for a in range(n):
            cp = pltpu.make_async_remote_copy(
                src_ref=src_refs[a], dst_ref=out_refs[a], send_sem=send_sems.at[a],
                recv_sem=recv_sems.at[a], device_id=(x, y, 1 - c), device_id_type=MESH)
            cp.start()
            copies.append(cp)
        for cp in copies:
            cp.wait()

    return pl.pallas_call(
        body, name=name, in_specs=[HBM] * n, out_specs=[HBM] * n,
        out_shape=[jax.ShapeDtypeStruct(s.shape, s.dtype) for s in srcs],
        scratch_shapes=[_dma_sems(n), _dma_sems(n)],
    )(*srcs)


def _row_tile(rows, cols, n_arrays):
    budget = 24 * 1024 * 1024 // (2 * 4 * n_arrays * cols)
    best = 16
    for t in range(16, rows + 1, 16):
        if rows % t == 0 and t <= budget:
            best = t
    assert rows % best == 0, (rows, cols)
    return best


def _add_halves(mine, theirs, wire, *, name):
    _, h, c = mine.shape
    tm = _row_tile(h, c, 3)
    spec = pl.BlockSpec((None, tm, c), lambda s, i: (s, i, 0))

    def body(a_ref, b_ref, o_ref):
        o_ref[...] = (a_ref[...] + b_ref[...]).astype(wire)

    return pl.pallas_call(
        body, name=name, grid=(4, h // tm), in_specs=[spec, spec], out_specs=spec,
        out_shape=jax.ShapeDtypeStruct(mine.shape, wire), compiler_params=_cparams(("parallel", "parallel")),
    )(mine, theirs)


def _sum4(own, recv, *, name):
    h, c = own.shape
    tm = _row_tile(h, c, 5)

    def body(o_ref, r_ref, out_ref):
        f = lambda k: r_ref[k].astype(F32)
        out_ref[...] = ((o_ref[...].astype(F32) + f(0)) + f(1)) + f(2)

    return pl.pallas_call(
        body, name=name, grid=(h // tm,),
        in_specs=[_rows(tm, c), pl.BlockSpec((3, tm, c), lambda i: (0, i, 0))],
        out_specs=_rows(tm, c), out_shape=jax.ShapeDtypeStruct((h, c), F32),
        compiler_params=_cparams(("parallel",)),
    )(own, recv)


def _adamw(g, w, m, v, *, name):
    r, c = g.shape
    tm = _row_tile(r, c, 7)
    c1 = 1.0 / (1.0 - ADAM_B1 ** ADAM_STEP)
    c2 = 1.0 / (1.0 - ADAM_B2 ** ADAM_STEP)

    def body(g_ref, w_ref, m_ref, v_ref, d_ref, nm_ref, nv_ref):
        gv = g_ref[...]
        nm = ADAM_B1 * m_ref[...] + (1.0 - ADAM_B1) * gv
        nv = ADAM_B2 * v_ref[...] + (1.0 - ADAM_B2) * (gv * gv)
        nm_ref[...] = nm
        nv_ref[...] = nv
        d_ref[...] = -ADAM_LR * ((nm * c1) / (jnp.sqrt(nv * c2) + ADAM_EPS) + ADAM_WD * w_ref[...])

    spec = _rows(tm, c)
    shape = jax.ShapeDtypeStruct((r, c), F32)
    return pl.pallas_call(
        body, name=name, grid=(r // tm,), in_specs=[spec] * 4, out_specs=[spec] * 3,
        out_shape=[shape] * 3, compiler_params=_cparams(("parallel",)),
    )(g, w, m, v)


BIG = (("w_in", (D, 1328), 1), ("w_uq", (QR, 384), 1), ("w_ukv", (KVR, 512), 1), ("w_branch", (576, D), 0),
       ("w_out", (256, D), 0), ("w_ffn_in", (D, 1408), 1), ("w_ffn_out", (704, D), 0))
SMALL = (("meta_tokens", (NMETA, 256), 1), ("b_gate", (2, 256), 1), ("conv_w", (CW, 320), 1))
REPL = (("norm_mix_g", (D,)), ("conv_b", (DR,)), ("w_rec_a", (NBLK, RB, RB)), ("b_rec_a", (DR,)),
        ("w_rec_i", (NBLK, RB, RB)), ("b_rec_i", (DR,)), ("lru_lambda", (DR,)), ("q_norm_g", (QR,)),
        ("kv_norm_g", (KVR,)), ("norm_ffn_g", (D,)), ("final_norm_g", (D,)))
WEIGHTS = ("meta_tokens", "norm_mix_g", "w_in", "b_gate", "conv_w", "conv_b", "w_rec_a", "b_rec_a", "w_rec_i",
           "b_rec_i", "lru_lambda", "q_norm_g", "w_uq", "kv_norm_g", "w_ukv", "w_branch", "w_out", "norm_ffn_g",
           "w_ffn_in", "w_ffn_out", "final_norm_g")
W = 1024
SMALL_N = sum(math.prod(s) for _, s, _ in SMALL)
SMALL_ROWS = 8
REPL_N = sum(math.prod(s) for _, s in REPL)
QUART_ROWS = 88
assert SMALL_N <= SMALL_ROWS * W and REPL_N <= 4 * QUART_ROWS * W


def _flat_pad(parts, rows):
    v = jnp.concatenate([p.reshape(-1) for p in parts])
    return jnp.pad(v, (0, rows * W - v.shape[0])).reshape(rows, W)


def _shard_stack(full, shard_shape, axis):
    r, cs = shard_shape
    if axis == 0:
        return full.reshape(4, r, cs)
    return jnp.stack([full[:, s * cs:(s + 1) * cs] for s in range(4)])


def _unshard(stack, axis):
    if axis == 0:
        return stack.reshape(4 * stack.shape[1], stack.shape[2])
    return jnp.concatenate([stack[s] for s in range(4)], axis=1)


def _split(flat, table):
    out, off = {}, 0
    for name, shape, *_ in table:
        n = math.prod(shape)
        out[name] = flat[..., off:off + n].reshape(flat.shape[:-1] + tuple(shape))
        off += n
    return out


def _misc_state(args, prefix, quarter):
    small = _flat_pad([args[prefix + n] for n, _, _ in SMALL], SMALL_ROWS)
    repl = _flat_pad([args[prefix + n] for n, _ in REPL], 4 * QUART_ROWS)
    mine = lax.dynamic_slice_in_dim(repl, quarter * QUART_ROWS, QUART_ROWS, axis=0)
    return jnp.concatenate([small, mine], axis=0)


def kernel(x, meta_tokens, norm_mix_g, w_in, b_gate, conv_w, conv_b, w_rec_a, b_rec_a, w_rec_i, b_rec_i, lru_lambda, q_norm_g, w_uq, kv_norm_g, w_ukv, w_branch, w_out, norm_ffn_g, w_ffn_in, w_ffn_out, final_norm_g, loss_target, m_meta_tokens, m_norm_mix_g, m_w_in, m_b_gate, m_conv_w, m_conv_b, m_w_rec_a, m_b_rec_a, m_w_rec_i, m_b_rec_i, m_lru_lambda, m_q_norm_g, m_w_uq, m_kv_norm_g, m_w_ukv, m_w_branch, m_w_out, m_norm_ffn_g, m_w_ffn_in, m_w_ffn_out, m_final_norm_g, v_meta_tokens, v_norm_mix_g, v_w_in, v_b_gate, v_conv_w, v_conv_b, v_w_rec_a, v_b_rec_a, v_w_rec_i, v_b_rec_i, v_lru_lambda, v_q_norm_g, v_w_uq, v_kv_norm_g, v_w_ukv, v_w_branch, v_w_out, v_norm_ffn_g, v_w_ffn_in, v_w_ffn_out, v_final_norm_g):
    args = dict(locals())
    chip = 2 * lax.axis_index("x") + lax.axis_index("y")
    core = lax.axis_index("c")

    shards = [args[n].reshape(s).astype(BF16) for n, s, _ in BIG]
    small = _flat_pad([args[n] for n, _, _ in SMALL], SMALL_ROWS)
    gathered = _allgather_chips(shards + [small], name="gather_weights")
    w = {}
    for (name, _, axis), stack in zip(BIG, gathered):
        w[name] = _unshard(stack, axis)
    small_parts = _split(gathered[-1].reshape(4, SMALL_ROWS * W), SMALL)
    for name, _, axis in SMALL:
        w[name] = _unshard(small_parts[name], axis)
    for name, shape in REPL:
        w[name] = args[name].reshape(shape)

    loss, grad_x, g = _local_step(x[0], loss_target[0], w)
    loss = lax.psum(loss, ("x", "y", "c"))

    red = [_shard_stack(g[n], s, a) for n, s, a in BIG]
    small_g = jnp.concatenate([_shard_stack(g[n], s, a).reshape(4, -1) for n, s, a in SMALL], axis=1)
    small_g = jnp.pad(small_g, ((0, 0), (0, SMALL_ROWS * W - SMALL_N))).reshape(4, SMALL_ROWS, W)
    repl_g = _flat_pad([g[n] for n, _ in REPL], 4 * QUART_ROWS).reshape(4, QUART_ROWS, W)
    red.append(jnp.concatenate([small_g, repl_g], axis=1))

    theirs = _sibling_take(red, name="reduce_sibling")
    parts = []
    for k, (a, t) in enumerate(zip(red, theirs)):
        h = a.shape[1] // 2
        mine = lax.dynamic_slice_in_dim(a, core * h, h, axis=1)
        wire = F32 if k == len(red) - 1 else BF16
        parts.append(_add_halves(mine, t, wire, name=f"add_sibling_{k}"))
    recv = _scatter_chips(parts, name="reduce_chips")
    halves = [_sum4(lax.dynamic_index_in_dim(p, chip, 0, keepdims=False), r, name=f"sum_chips_{k}")
              for k, (p, r) in enumerate(zip(parts, recv))]
    others = _sibling_swap(halves, name="share_sibling")
    gred = [jnp.where(core == 0, jnp.concatenate([a, b], axis=0), jnp.concatenate([b, a], axis=0))
            for a, b in zip(halves, others)]

    results = {}
    for (name, shape, _), gr in zip(BIG, gred):
        d, nm, nv = _adamw(gr, args[name].reshape(shape), args["m_" + name].reshape(shape),
                           args["v_" + name].reshape(shape), name="adamw_" + name)
        results[name] = (gr, d, nm, nv)
    d, nm, nv = _adamw(gred[-1], _misc_state(args, "", chip), _misc_state(args, "m_", chip),
                       _misc_state(args, "v_", chip), name="adamw_misc")
    misc = (gred[-1], d, nm, nv)

    quarters = jnp.stack([t[SMALL_ROWS:] for t in misc])
    repl_all = _allgather_chips([quarters], name="gather_repl")[0]
    repl_all = repl_all.transpose(1, 0, 2, 3).reshape(4, 4 * QUART_ROWS * W)

    outs = []
    for k in range(4):
        sm = _split(misc[k][:SMALL_ROWS].reshape(-1), SMALL)
        rp = _split(repl_all[k], REPL)
        for name in WEIGHTS:
            val = results[name][k] if name in results else (sm[name] if name in sm else rp[name])
            outs.append(val.reshape(args[name].shape))
    return (loss, grad_x[None], *outs)
```

```python
import functools
import math

import jax
import jax.numpy as jnp
from jax import lax
from jax.experimental import pallas as pl
from jax.experimental.pallas import tpu as pltpu

F32 = jnp.float32
BF16 = jnp.bfloat16

D = 1024
DR = 1280
NBLK = 10
RB = 128
CW = 4
NH = 8
NOPE = 128
ROPE = 64
VD = 128
QR = 384
KVR = 256
DFF = 2816
NMETA = 16
EPS = 1e-6
LRU_C = 8.0
ROPE_THETA = 10000.0
SCALE = 1.0 / math.sqrt(NOPE + ROPE)
NEG = -1e30
FRONT = 128
PAD = FRONT - NMETA
QW = 2 * NOPE
LANES = 128
SUB = 128
CHAINS = 4
VMEM_LIMIT = 52 * 1024 * 1024

ADAM_LR = 0.001
ADAM_B1 = 0.9
ADAM_B2 = 0.999
ADAM_EPS = 1e-08
ADAM_WD = 0.01
ADAM_STEP = 10

MESH = pl.DeviceIdType.MESH


def _cparams(sem):
    return pltpu.CompilerParams(dimension_semantics=sem, vmem_limit_bytes=VMEM_LIMIT)


def _sigmoid(x):
    return 1.0 / (1.0 + jnp.exp(-x))


def _gelu_parts(x):
    c = math.sqrt(2.0 / math.pi)
    inner = c * (x + 0.044715 * x * x * x)
    t = jnp.tanh(inner)
    g = 0.5 * x * (1.0 + t)
    dg = 0.5 * (1.0 + t) + 0.5 * x * (1.0 - t * t) * c * (1.0 + 3.0 * 0.044715 * x * x)
    return g, dg


def _divisors(n, step, cap):
    return [d for d in range(step, min(n, cap) + 1, step) if n % d == 0] or [n]


MM_VMEM_BUDGET = 40 * 1024 * 1024
MM_MAX_ROWS = 1664
MM_MAX_COLS = 1408


def _mm_tiles(M, K, N, a_item, out_item, has_res):
    best = None
    for tn in _divisors(N, LANES, MM_MAX_COLS):
        for tm in _divisors(M, 16, MM_MAX_ROWS):
            need = 2 * (tm * K * a_item + K * tn * 2 + tm * tn * (out_item + (4 if has_res else 0)))
            if need <= MM_VMEM_BUDGET and (best is None or tm * tn > best[0] * best[1]):
                best = (tm, tn)
    assert best is not None, (M, K, N)
    return best


def _mm(a, b, *, name, out_dtype=F32, res=None):
    M, K = a.shape
    N = b.shape[1]
    has_res = res is not None
    tm, tn = _mm_tiles(M, K, N, a.dtype.itemsize, jnp.dtype(out_dtype).itemsize, has_res)

    def body(*refs):
        if has_res:
            a_ref, b_ref, r_ref, o_ref = refs
        else:
            a_ref, b_ref, o_ref = refs
        acc = jnp.dot(a_ref[...].astype(BF16), b_ref[...].astype(BF16), preferred_element_type=F32)
        if has_res:
            acc = acc + r_ref[...].astype(F32)
        o_ref[...] = acc.astype(o_ref.dtype)

    a_bytes = M * K * a.dtype.itemsize
    b_bytes = K * N * b.dtype.itemsize
    rows_outer = a_bytes + (M // tm) * b_bytes <= b_bytes + (N // tn) * a_bytes
    if rows_outer:
        grid = (M // tm, N // tn)
        ia, ib, io = (lambda i, j: (i, 0)), (lambda i, j: (0, j)), (lambda i, j: (i, j))
    else:
        grid = (N // tn, M // tm)
        ia, ib, io = (lambda j, i: (i, 0)), (lambda j, i: (0, j)), (lambda j, i: (i, j))
    in_specs = [pl.BlockSpec((tm, K), ia), pl.BlockSpec((K, tn), ib)]
    args = [a, b]
    if has_res:
        in_specs.append(pl.BlockSpec((tm, tn), io))
        args.append(res)
    return pl.pallas_call(
        body, name=name, grid=grid, in_specs=in_specs,
        out_specs=pl.BlockSpec((tm, tn), io),
        out_shape=jax.ShapeDtypeStruct((M, N), out_dtype),
        compiler_params=_cparams(("parallel", "parallel")),
    )(*args)


def _mm_tn(a, b, *, name):
    T, K1 = a.shape
    N = b.shape[1]
    tt = _divisors(T, 16, MM_MAX_ROWS)[-1]
    tk = _divisors(K1, LANES, MM_MAX_COLS)[-1]
    tn = _divisors(N, LANES, MM_MAX_COLS)[-1]

    def body(a_ref, b_ref, o_ref):
        @pl.when(pl.program_id(2) == 0)
        def _():
            o_ref[...] = jnp.zeros_like(o_ref)

        o_ref[...] += lax.dot_general(a_ref[...].astype(BF16), b_ref[...].astype(BF16),
                                      (((0,), (0,)), ((), ())), preferred_element_type=F32)

    return pl.pallas_call(
        body, name=name, grid=(K1 // tk, N // tn, T // tt),
        in_specs=[pl.BlockSpec((tt, tk), lambda i, j, t: (t, i)),
                  pl.BlockSpec((tt, tn), lambda i, j, t: (t, j))],
        out_specs=pl.BlockSpec((tk, tn), lambda i, j, t: (i, j)),
        out_shape=jax.ShapeDtypeStruct((K1, N), F32),
        compiler_params=_cparams(("parallel", "parallel", "arbitrary")),
    )(a, b)


def _rows(tm, w, cb=0):
    return pl.BlockSpec((tm, w), lambda i: (i, cb))


def _const(shape):
    n = len(shape)
    return pl.BlockSpec(shape, lambda i: (0,) * n)


def _rmsnorm_fwd(x, g, *, name, tm=640):
    T, C = x.shape

    def body(x_ref, g_ref, o_ref):
        xv = x_ref[...]
        r = lax.rsqrt(jnp.mean(xv * xv, axis=-1, keepdims=True) + EPS)
        o_ref[...] = ((xv * r) * g_ref[...]).astype(BF16)

    return pl.pallas_call(
        body, name=name, grid=(T // tm,),
        in_specs=[_rows(tm, C), _const((1, C))],
        out_specs=_rows(tm, C),
        out_shape=jax.ShapeDtypeStruct((T, C), BF16),
        compiler_params=_cparams(("parallel",)),
    )(x, g)


def _rmsnorm_bwd(x, g, dy, res, *, name, tm=640, want_f32=True, want_bf16=True):
    T, C = x.shape
    has_res = res is not None

    def body(*refs):
        refs = list(refs)
        x_ref, g_ref, dy_ref = refs[:3]
        refs = refs[3:]
        r_ref = refs.pop(0) if has_res else None
        o32 = refs.pop(0) if want_f32 else None
        o16 = refs.pop(0) if want_bf16 else None
        dg_ref = refs.pop(0)

        @pl.when(pl.program_id(0) == 0)
        def _():
            dg_ref[...] = jnp.zeros_like(dg_ref)

        xv = x_ref[...]
        dyv = dy_ref[...].astype(F32)
        r = lax.rsqrt(jnp.mean(xv * xv, axis=-1, keepdims=True) + EPS)
        xn = xv * r
        dg_ref[...] += jnp.sum(dyv * xn, axis=0, keepdims=True)
        dxn = dyv * g_ref[...]
        dx = r * (dxn - xn * jnp.mean(dxn * xn, axis=-1, keepdims=True))
        if has_res:
            dx = dx + r_ref[...]
        if want_f32:
            o32[...] = dx
        if want_bf16:
            o16[...] = dx.astype(BF16)

    in_specs = [_rows(tm, C), _const((1, C)), _rows(tm, C)]
    args = [x, g, dy]
    if has_res:
        in_specs.append(_rows(tm, C))
        args.append(res)
    out_specs, out_shape = [], []
    if want_f32:
        out_specs.append(_rows(tm, C))
        out_shape.append(jax.ShapeDtypeStruct((T, C), F32))
    if want_bf16:
        out_specs.append(_rows(tm, C))
        out_shape.append(jax.ShapeDtypeStruct((T, C), BF16))
    out_specs.append(_const((1, C)))
    out_shape.append(jax.ShapeDtypeStruct((1, C), F32))
    return pl.pallas_call(
        body, name=name, grid=(T // tm,), in_specs=in_specs, out_specs=out_specs,
        out_shape=out_shape, compiler_params=_cparams(("arbitrary",)),
    )(*args)


def _gate_mix_fwd(um, bg, p_rnn, p_att, *, tm=320):
    T = um.shape[0]

    def body(um_ref, bg_ref, pr_ref, pa_ref, o_ref):
        g = _sigmoid(um_ref[...] + bg_ref[...])
        o_ref[...] = (g[:, :D] * pr_ref[...] + g[:, D:] * pa_ref[...]).astype(BF16)

    return pl.pallas_call(
        body, name="gate_mix_fwd", grid=(T // tm,),
        in_specs=[_rows(tm, 2 * D), _const((1, 2 * D)), _rows(tm, D), _rows(tm, D)],
        out_specs=_rows(tm, D),
        out_shape=jax.ShapeDtypeStruct((T, D), BF16),
        compiler_params=_cparams(("parallel",)),
    )(um, bg, p_rnn, p_att)


def _gate_mix_bwd(um, bg, p_rnn, p_att, dmixed, *, tm=320):
    T = um.shape[0]

    def body(um_ref, bg_ref, pr_ref, pa_ref, dm_ref, dpr_ref, dpa_ref, dum_ref, dbg_ref):
        @pl.when(pl.program_id(0) == 0)
        def _():
            dbg_ref[...] = jnp.zeros_like(dbg_ref)

        g = _sigmoid(um_ref[...] + bg_ref[...])
        g0, g1 = g[:, :D], g[:, D:]
        dm = dm_ref[...]
        dpr_ref[...] = (dm * g0).astype(BF16)
        dpa_ref[...] = (dm * g1).astype(BF16)
        d0 = dm * pr_ref[...] * g0 * (1.0 - g0)
        d1 = dm * pa_ref[...] * g1 * (1.0 - g1)
        dum_ref[:, :D] = d0.astype(BF16)
        dum_ref[:, D:] = d1.astype(BF16)
        dbg_ref[:, :D] += jnp.sum(d0, axis=0, keepdims=True)
        dbg_ref[:, D:] += jnp.sum(d1, axis=0, keepdims=True)

    return pl.pallas_call(
        body, name="gate_mix_bwd", grid=(T // tm,),
        in_specs=[_rows(tm, 2 * D), _const((1, 2 * D)), _rows(tm, D), _rows(tm, D), _rows(tm, D)],
        out_specs=[_rows(tm, D), _rows(tm, D), _rows(tm, 2 * D), _const((1, 2 * D))],
        out_shape=[jax.ShapeDtypeStruct((T, D), BF16), jax.ShapeDtypeStruct((T, D), BF16),
                   jax.ShapeDtypeStruct((T, 2 * D), BF16), jax.ShapeDtypeStruct((1, 2 * D), F32)],
        compiler_params=_cparams(("arbitrary",)),
    )(um, bg, p_rnn, p_att, dmixed)


def _swiglu_fwd(ff, *, tm=320):
    T = ff.shape[0]

    def body(g_ref, u_ref, o_ref):
        gv = g_ref[...]
        o_ref[...] = (gv * _sigmoid(gv) * u_ref[...]).astype(BF16)

    return pl.pallas_call(
        body, name="swiglu_fwd", grid=(T // tm,),
        in_specs=[_rows(tm, DFF, 0), _rows(tm, DFF, 1)],
        out_specs=_rows(tm, DFF),
        out_shape=jax.ShapeDtypeStruct((T, DFF), BF16),
        compiler_params=_cparams(("parallel",)),
    )(ff, ff)


def _swiglu_bwd(ff, dact, *, tm=320):
    T = ff.shape[0]

    def body(g_ref, u_ref, da_ref, o_ref):
        gv = g_ref[...]
        s = _sigmoid(gv)
        da = da_ref[...]
        o_ref[:, :DFF] = (da * u_ref[...] * s * (1.0 + gv * (1.0 - s))).astype(BF16)
        o_ref[:, DFF:] = (da * gv * s).astype(BF16)

    return pl.pallas_call(
        body, name="swiglu_bwd", grid=(T // tm,),
        in_specs=[_rows(tm, DFF, 0), _rows(tm, DFF, 1), _rows(tm, DFF)],
        out_specs=_rows(tm, 2 * DFF),
        out_shape=jax.ShapeDtypeStruct((T, 2 * DFF), BF16),
        compiler_params=_cparams(("parallel",)),
    )(ff, ff, dact)


def _loss_head(h2, tgt, g, *, tm=FRONT):
    T = h2.shape[0]
    front_blocks = FRONT // tm

    def body(h_ref, t_ref, g_ref, d32_ref, d16_ref, dg_ref, ls_ref):
        i = pl.program_id(0)

        @pl.when(i == 0)
        def _():
            dg_ref[...] = jnp.zeros_like(dg_ref)
            ls_ref[...] = jnp.zeros_like(ls_ref)

        xv = h_ref[...]
        r = lax.rsqrt(jnp.mean(xv * xv, axis=-1, keepdims=True) + EPS)
        xn = xv * r
        gv = g_ref[...]
        e = jnp.where(i >= front_blocks, xn * gv - t_ref[...], 0.0)
        ls_ref[...] += jnp.sum(e * e, axis=0, keepdims=True)
        dy = e * (1.0 / D)
        dg_ref[...] += jnp.sum(dy * xn, axis=0, keepdims=True)
        dxn = dy * gv
        dx = r * (dxn - xn * jnp.mean(dxn * xn, axis=-1, keepdims=True))
        d32_ref[...] = dx
        d16_ref[...] = dx.astype(BF16)

    return pl.pallas_call(
        body, name="loss_head", grid=(T // tm,),
        in_specs=[_rows(tm, D), pl.BlockSpec((tm, D), lambda i: (jnp.maximum(i - front_blocks, 0), 0)),
                  _const((1, D))],
        out_specs=[_rows(tm, D), _rows(tm, D), _const((1, D)), _const((1, D))],
        out_shape=[jax.ShapeDtypeStruct((T, D), F32), jax.ShapeDtypeStruct((T, D), BF16),
                   jax.ShapeDtypeStruct((1, D), F32), jax.ShapeDtypeStruct((1, D), F32)],
        compiler_params=_cparams(("arbitrary",)),
    )(h2, tgt, g)


def _scan_fwd(a, b, h_in):
    n = a.shape[0]
    row = lax.broadcasted_iota(jnp.int32, a.shape, 0)
    s = 1
    while s < n:
        a_sh = jnp.where(row >= s, pltpu.roll(a, s, 0), 1.0)
        b_sh = jnp.where(row >= s, pltpu.roll(b, s, 0), 0.0)
        b = a * b_sh + b
        a = a * a_sh
        s *= 2
    return b + a * h_in


def _scan_rev(a, b, g_in):
    n = a.shape[0]
    row = lax.broadcasted_iota(jnp.int32, a.shape, 0)
    s = 1
    while s < n:
        a_sh = jnp.where(row < n - s, pltpu.roll(a, n - s, 0), 1.0)
        b_sh = jnp.where(row < n - s, pltpu.roll(b, n - s, 0), 0.0)
        b = a * b_sh + b
        a = a * a_sh
        s *= 2
    return b + a * g_in


def _lru_gates(xc, wa, ba, wi, bi, lam):
    xcb = xc.astype(BF16)
    r = _sigmoid(jnp.dot(xcb, wa, preferred_element_type=F32) + ba)
    ig = _sigmoid(jnp.dot(xcb, wi, preferred_element_type=F32) + bi)
    log_sig = jnp.minimum(lam, 0.0) - jnp.log(1.0 + jnp.exp(-jnp.abs(lam)))
    log_a = LRU_C * r * log_sig
    a = jnp.exp(log_a)
    z = 2.0 * log_a
    poly = -z * (1.0 + z * (0.5 + z * (1.0 / 6.0 + z * (1.0 / 24.0))))
    m2 = jnp.where(z > -0.03, poly, 1.0 - jnp.exp(z))
    return r, ig, log_sig, a, jnp.sqrt(m2)


def _rnn_specs(tc, nblk_t, rev):
    def tmap(k):
        return (nblk_t - 1 - k) if rev else k

    hb = tc // 8
    blk = lambda off: pl.BlockSpec((tc, RB), lambda c, k: (tmap(k), c + off))
    halo = lambda off: pl.BlockSpec((8, RB), lambda c, k: (jnp.maximum(tmap(k) * hb - 1, 0), c + off))
    vec = pl.BlockSpec((1, RB), lambda c, k: (0, c))
    cwv = pl.BlockSpec((CW, RB), lambda c, k: (0, c))
    mat = pl.BlockSpec((None, RB, RB), lambda c, k: (c, 0, 0))
    return blk, halo, vec, cwv, mat


def _rnn_fwd(uxg, cw, cb, wa, ba, wi, bi, lam, *, tc=640):
    T = uxg.shape[0]
    nt = T // tc
    nsub = tc // SUB
    blk, halo, vec, cwv, mat = _rnn_specs(tc, nt, False)

    def body(x_ref, xh_ref, ug_ref, cw_ref, cb_ref, wa_ref, ba_ref, wi_ref, bi_ref, lam_ref,
             h_ref, y_ref, xb, hc):
        k = pl.program_id(1)

        @pl.when(k == 0)
        def _():
            hc[...] = jnp.zeros_like(hc)

        xb[0:8, :] = jnp.where(k > 0, xh_ref[...], 0.0)
        xb[8:, :] = x_ref[...]
        cwv_, cbv = cw_ref[...], cb_ref[...]
        wav, wiv = wa_ref[...], wi_ref[...]
        bav, biv, lamv = ba_ref[...], bi_ref[...], lam_ref[...]
        h_in = hc[0:1, :]
        for sc in range(nsub):
            r0 = sc * SUB
            xc = cbv + cwv_[0:1, :] * xb[pl.ds(5 + r0, SUB), :]
            for j in range(1, CW):
                xc = xc + cwv_[j:j + 1, :] * xb[pl.ds(5 + j + r0, SUB), :]
            r, ig, _, a, mm = _lru_gates(xc, wav, bav, wiv, biv, lamv)
            rows = k * tc + r0 + lax.broadcasted_iota(jnp.int32, (SUB, RB), 0)
            b = jnp.where(rows >= PAD, mm * (ig * xc), 0.0)
            h = _scan_fwd(a, b, h_in)
            h_in = h[SUB - 1:SUB, :]
            h_ref[pl.ds(r0, SUB), :] = h
            gl, _ = _gelu_parts(ug_ref[pl.ds(r0, SUB), :])
            y_ref[pl.ds(r0, SUB), :] = (h * gl).astype(BF16)
        hc[0:1, :] = h_in

    return pl.pallas_call(
        body, name="rnn_fwd", grid=(NBLK, nt),
        in_specs=[blk(0), halo(0), blk(NBLK), cwv, vec, mat, vec, mat, vec, vec],
        out_specs=[blk(0), blk(0)],
        out_shape=[jax.ShapeDtypeStruct((T, DR), F32), jax.ShapeDtypeStruct((T, DR), BF16)],
        scratch_shapes=[pltpu.VMEM((tc + 8, RB), F32), pltpu.VMEM((8, RB), F32)],
        compiler_params=_cparams(("parallel", "arbitrary")),
    )(uxg, uxg, uxg, cw, cb, wa, ba, wi, bi, lam)


def _rnn_bwd(uxg, hs, dy, cw, cb, wa, ba, wi, bi, lam, wat, wit, *, tc=640):
    T = uxg.shape[0]
    nt = T // tc
    nsub = tc // SUB
    blk, halo, vec, cwv, mat = _rnn_specs(tc, nt, True)

    def body(x_ref, xh_ref, ug_ref, h_ref, hh_ref, dy_ref, cw_ref, cb_ref, wa_ref, ba_ref, wi_ref,
             bi_ref, lam_ref, wat_ref, wit_ref,
             dux_ref, dug_ref, dcw_ref, dcb_ref, dwa_ref, dba_ref, dwi_ref, dbi_ref, dlam_ref,
             xb, hb, ab, dxb, xcs, rs, igs, mms, dgas, dgis, carry):
        k = pl.program_id(1)
        kt = nt - 1 - k

        @pl.when(k == 0)
        def _():
            carry[...] = jnp.zeros_like(carry)
            for ref in (dcw_ref, dcb_ref, dwa_ref, dba_ref, dwi_ref, dbi_ref, dlam_ref):
                ref[...] = jnp.zeros_like(ref)

        xb[0:8, :] = jnp.where(kt > 0, xh_ref[...], 0.0)
        xb[8:, :] = x_ref[...]
        hb[0:8, :] = jnp.where(kt > 0, hh_ref[...], 0.0)
        hb[8:, :] = h_ref[...]
        cwv_, cbv = cw_ref[...], cb_ref[...]
        wav, wiv = wa_ref[...], wi_ref[...]
        bav, biv, lamv = ba_ref[...], bi_ref[...], lam_ref[...]
        ab[tc:tc + 8, :] = jnp.broadcast_to(carry[1:2, :], (8, RB))
        dxb[tc:tc + 8, :] = carry[8:16, :]
        log_sig = None
        for sc in range(nsub):
            r0 = sc * SUB
            xc = cbv + cwv_[0:1, :] * xb[pl.ds(5 + r0, SUB), :]
            for j in range(1, CW):
                xc = xc + cwv_[j:j + 1, :] * xb[pl.ds(5 + j + r0, SUB), :]
            r, ig, log_sig, a, mm = _lru_gates(xc, wav, bav, wiv, biv, lamv)
            xcs[pl.ds(r0, SUB), :] = xc
            rs[pl.ds(r0, SUB), :] = r
            igs[pl.ds(r0, SUB), :] = ig
            mms[pl.ds(r0, SUB), :] = mm
            ab[pl.ds(r0, SUB), :] = a
        sig_neg = _sigmoid(-lamv)
        g_in = carry[0:1, :]
        dlam_acc = jnp.zeros((1, RB), F32)
        for sc in reversed(range(nsub)):
            r0 = sc * SUB
            xc, r, ig, mm = xcs[pl.ds(r0, SUB), :], rs[pl.ds(r0, SUB), :], igs[pl.ds(r0, SUB), :], mms[pl.ds(r0, SUB), :]
            a = ab[pl.ds(r0, SUB), :]
            a_next = ab[pl.ds(r0 + 1, SUB), :]
            hv = hb[pl.ds(8 + r0, SUB), :]
            hprev = hb[pl.ds(7 + r0, SUB), :]
            dyv = dy_ref[pl.ds(r0, SUB), :]
            gl, dgl = _gelu_parts(ug_ref[pl.ds(r0, SUB), :])
            dug_ref[pl.ds(r0, SUB), :] = (dyv * hv * dgl).astype(BF16)
            G = _scan_rev(a_next, dyv * gl, g_in)
            g_in = G[0:1, :]
            rows = kt * tc + r0 + lax.broadcasted_iota(jnp.int32, (SUB, RB), 0)
            db = jnp.where(rows >= PAD, G, 0.0)
            da = G * hprev
            dmm = db * (ig * xc)
            di = db * (mm * xc)
            dxc = db * (mm * ig)
            dlog_a = da * a - dmm * (a * a) / jnp.maximum(mm, 1e-30)
            dr = dlog_a * (LRU_C * log_sig)
            dlam_acc = dlam_acc + jnp.sum(dlog_a * (LRU_C * r), axis=0, keepdims=True)
            dga = dr * r * (1.0 - r)
            dgi = di * ig * (1.0 - ig)
            dgab, dgib = dga.astype(BF16), dgi.astype(BF16)
            dgas[pl.ds(r0, SUB), :] = dgab
            dgis[pl.ds(r0, SUB), :] = dgib
            dba_ref[...] += jnp.sum(dga, axis=0, keepdims=True)
            dbi_ref[...] += jnp.sum(dgi, axis=0, keepdims=True)
            dxc = dxc + jnp.dot(dgab, wat_ref[...], preferred_element_type=F32) \
                + jnp.dot(dgib, wit_ref[...], preferred_element_type=F32)
            dxb[pl.ds(r0, SUB), :] = dxc
        dlam_ref[...] += dlam_acc * sig_neg
        xcb = xcs[...].astype(BF16)
        tn = (((0,), (0,)), ((), ()))
        dwa_ref[...] += lax.dot_general(xcb, dgas[...], tn, preferred_element_type=F32)
        dwi_ref[...] += lax.dot_general(xcb, dgis[...], tn, preferred_element_type=F32)
        dxc_all = dxb[0:tc, :]
        dcb_ref[...] += jnp.sum(dxc_all, axis=0, keepdims=True)
        rows_all = kt * tc + lax.broadcasted_iota(jnp.int32, (tc, RB), 0)
        dux = jnp.zeros((tc, RB), F32)
        for j in range(CW):
            dcw_ref[j:j + 1, :] += jnp.sum(dxc_all * xb[pl.ds(5 + j, tc), :], axis=0, keepdims=True)
            dux = dux + cwv_[j:j + 1, :] * dxb[pl.ds(CW - 1 - j, tc), :]
        dux_ref[...] = jnp.where(rows_all >= PAD, dux, 0.0).astype(BF16)
        carry[0:1, :] = g_in
        carry[1:2, :] = ab[0:1, :]
        carry[8:16, :] = dxb[0:8, :]

    vec_out = pl.BlockSpec((1, RB), lambda c, k: (0, c))
    return pl.pallas_call(
        body, name="rnn_bwd", grid=(NBLK, nt),
        in_specs=[blk(0), halo(0), blk(NBLK), blk(0), halo(0), blk(0), cwv, vec, mat, vec, mat, vec, vec, mat, mat],
        out_specs=[blk(0), blk(0), cwv, vec_out, mat, vec_out, mat, vec_out, vec_out],
        out_shape=[jax.ShapeDtypeStruct((T, DR), BF16), jax.ShapeDtypeStruct((T, DR), BF16),
                   jax.ShapeDtypeStruct((CW, DR), F32), jax.ShapeDtypeStruct((1, DR), F32),
                   jax.ShapeDtypeStruct((NBLK, RB, RB), F32), jax.ShapeDtypeStruct((1, DR), F32),
                   jax.ShapeDtypeStruct((NBLK, RB, RB), F32), jax.ShapeDtypeStruct((1, DR), F32),
                   jax.ShapeDtypeStruct((1, DR), F32)],
        scratch_shapes=[pltpu.VMEM((tc + 8, RB), F32), pltpu.VMEM((tc + 8, RB), F32),
                        pltpu.VMEM((tc + 8, RB), F32), pltpu.VMEM((tc + 8, RB), F32),
                        pltpu.VMEM((tc, RB), F32), pltpu.VMEM((tc, RB), F32), pltpu.VMEM((tc, RB), F32),
                        pltpu.VMEM((tc, RB), F32), pltpu.VMEM((tc, RB), BF16), pltpu.VMEM((tc, RB), BF16),
                        pltpu.VMEM((16, RB), F32)],
        compiler_params=_cparams(("parallel", "arbitrary")),
    )(uxg, uxg, uxg, hs, hs, dy, cw, cb, wa, ba, wi, bi, lam, wat, wit)


def _attn_prep(q_all, kv_all, ukr, tab, *, tm=320):
    T = q_all.shape[0]

    def body(q_ref, kv_ref, kr_ref, tab_ref, qo_ref, ko_ref, vo_ref):
        tab_v = tab_ref[...]
        lane = lax.broadcasted_iota(jnp.int32, (tm, LANES), 1)
        t1 = kr_ref[...] * tab_v
        kro = jnp.where(lane < ROPE, t1 + pltpu.roll(t1, ROPE, 1), 0.0).astype(BF16)
        for h in range(NH):
            c0 = h * QW
            qo_ref[h, :, 0:NOPE] = (q_ref[:, c0:c0 + NOPE] * SCALE).astype(BF16)
            t2 = q_ref[:, c0 + NOPE:c0 + QW] * tab_v
            qo_ref[h, :, NOPE:QW] = ((t2 + pltpu.roll(t2, ROPE, 1)) * SCALE).astype(BF16)
            ko_ref[h, :, 0:NOPE] = kv_ref[:, c0:c0 + NOPE].astype(BF16)
            ko_ref[h, :, NOPE:QW] = kro
            vo_ref[h, :, :] = kv_ref[:, c0 + NOPE:c0 + QW].astype(BF16)

    return pl.pallas_call(
        body, name="attn_prep", grid=(T // tm,),
        in_specs=[_rows(tm, NH * QW), _rows(tm, NH * QW), _rows(tm, LANES), _rows(tm, LANES)],
        out_specs=[pl.BlockSpec((NH, tm, QW), lambda i: (0, i, 0)), pl.BlockSpec((NH, tm, QW), lambda i: (0, i, 0)),
                   pl.BlockSpec((NH, tm, VD), lambda i: (0, i, 0))],
        out_shape=[jax.ShapeDtypeStruct((NH, T, QW), BF16), jax.ShapeDtypeStruct((NH, T, QW), BF16),
                   jax.ShapeDtypeStruct((NH, T, VD), BF16)],
        compiler_params=_cparams(("parallel",)),
    )(q_all, kv_all, ukr, tab)


def _attn_prep_bwd(dq, dk, dv, tab, *, tm=320):
    T = dq.shape[1]

    def body(dq_ref, dk_ref, dv_ref, tab_ref, dqa_ref, dkva_ref, dkr_ref):
        tab_v = tab_ref[...]
        lane = lax.broadcasted_iota(jnp.int32, (tm, LANES), 1)
        dkro = jnp.zeros((tm, LANES), F32)
        for h in range(NH):
            c0 = h * QW
            dqa_ref[:, c0:c0 + NOPE] = (dq_ref[h, :, 0:NOPE] * SCALE).astype(BF16)
            d2 = dq_ref[h, :, NOPE:QW]
            dqa_ref[:, c0 + NOPE:c0 + QW] = ((d2 + pltpu.roll(d2, ROPE, 1)) * tab_v * SCALE).astype(BF16)
            dkva_ref[:, c0:c0 + NOPE] = dk_ref[h, :, 0:NOPE].astype(BF16)
            dkva_ref[:, c0 + NOPE:c0 + QW] = dv_ref[h, :, :].astype(BF16)
            dkro = dkro + dk_ref[h, :, NOPE:QW]
        dkro = jnp.where(lane < ROPE, dkro, 0.0)
        dkr_ref[...] = ((dkro + pltpu.roll(dkro, ROPE, 1)) * tab_v).astype(BF16)

    return pl.pallas_call(
        body, name="attn_prep_bwd", grid=(T // tm,),
        in_specs=[pl.BlockSpec((NH, tm, QW), lambda i: (0, i, 0)), pl.BlockSpec((NH, tm, QW), lambda i: (0, i, 0)),
                  pl.BlockSpec((NH, tm, VD), lambda i: (0, i, 0)), _rows(tm, LANES)],
        out_specs=[_rows(tm, NH * QW), _rows(tm, NH * QW), _rows(tm, LANES)],
        out_shape=[jax.ShapeDtypeStruct((T, NH * QW), BF16), jax.ShapeDtypeStruct((T, NH * QW), BF16),
                   jax.ShapeDtypeStruct((T, LANES), BF16)],
        compiler_params=_cparams(("parallel",)),
    )(dq, dk, dv, tab)


def _visible(q0, k0, nq, nk):
    rows = q0 + lax.broadcasted_iota(jnp.int32, (nq, nk), 0)
    cols = k0 + lax.broadcasted_iota(jnp.int32, (nq, nk), 1)
    return ((cols >> 6) <= (rows >> 6)) & (cols >= PAD)


def _visible_t(q0, k0, nq, nk):
    cols = k0 + lax.broadcasted_iota(jnp.int32, (nk, nq), 0)
    rows = q0 + lax.broadcasted_iota(jnp.int32, (nk, nq), 1)
    return ((cols >> 6) <= (rows >> 6)) & (cols >= PAD)


_NT = (((1,), (1,)), ((), ()))
ATTN_BLOCK = 1664


def _attn_block(T):
    return ATTN_BLOCK if T % ATTN_BLOCK == 0 else 640


def _round_up(n, m):
    return -(-n // m) * m


def _flash_fwd(q, k, v, *, gather=(), bq=None):
    T = q.shape[1]
    bq = bq or _attn_block(T)
    nq = T // bq
    rs = bq // CHAINS
    n = len(gather)

    def body(*refs):
        q_ref, k_ref, v_ref = refs[:3]
        g_src = refs[3:3 + n]
        o_ref, lse_ref = refs[3 + n:5 + n]
        g_out = refs[5 + n:5 + 2 * n]
        scr = refs[5 + 2 * n:]
        m_s, l_s, acc_s = scr[:CHAINS], scr[CHAINS:2 * CHAINS], scr[2 * CHAINS:3 * CHAINS]
        g_scr = scr[3 * CHAINS:]
        h = pl.program_id(0)
        i = pl.program_id(1)
        if n:
            @pl.when((h == 0) & (i == 0))
            def _():
                _gather_start(_gather_descs(g_src, g_out, g_scr))

        for r in range(CHAINS):
            m_s[r][...] = jnp.full_like(m_s[r], NEG)
            l_s[r][...] = jnp.zeros_like(l_s[r])
            acc_s[r][...] = jnp.zeros_like(acc_s[r])

        def step(j, masked, diag):
            off = pl.multiple_of(j * bq, bq)
            for r in range(CHAINS):
                rows = pl.ds(r * rs, rs)
                kw = min(bq, _round_up((r + 1) * rs, LANES)) if diag else bq
                kv_ = k_ref[pl.ds(off, kw), :]
                vv = v_ref[pl.ds(off, kw), :]
                s = lax.dot_general(q_ref[rows, :], kv_, _NT, preferred_element_type=F32)
                if masked:
                    s = jnp.where(_visible(i * bq + r * rs, j * bq, rs, kw), s, NEG)
                m_prev = m_s[r][...]
                m_new = jnp.maximum(m_prev, jnp.max(s, axis=-1, keepdims=True))
                p = jnp.exp(s - m_new)
                alpha = jnp.exp(m_prev - m_new)
                l_s[r][...] = alpha * l_s[r][...] + jnp.sum(p, axis=-1, keepdims=True)
                acc_s[r][...] = alpha * acc_s[r][...] + jnp.dot(p.astype(BF16), vv, preferred_element_type=F32)
                m_s[r][...] = m_new

        @pl.when(i == 0)
        def _():
            step(0, True, True)

        @pl.when(i > 0)
        def _():
            step(0, True, False)

            def loop(j, c):
                step(j, False, False)
                return c

            lax.fori_loop(1, i, loop, 0)
            step(i, True, True)

        for r in range(CHAINS):
            rows = pl.ds(r * rs, rs)
            o_ref[rows, :] = (acc_s[r][...] / l_s[r][...]).astype(BF16)
            lse_ref[rows, :] = m_s[r][...] + jnp.log(l_s[r][...])

        if n:
            @pl.when((h == NH - 1) & (i == nq - 1))
            def _():
                _gather_wait(_gather_descs(g_src, g_out, g_scr))

    return pl.pallas_call(
        body, name="flash_fwd", grid=(NH, nq),
        in_specs=[pl.BlockSpec((None, bq, QW), lambda h, i: (h, i, 0)),
                  pl.BlockSpec((None, T, QW), lambda h, i: (h, 0, 0)),
                  pl.BlockSpec((None, T, VD), lambda h, i: (h, 0, 0))] + [HBM] * n,
        out_specs=[pl.BlockSpec((bq, VD), lambda h, i: (i, h)),
                   pl.BlockSpec((None, bq, 1), lambda h, i: (h, i, 0))] + [HBM] * n,
        out_shape=[jax.ShapeDtypeStruct((T, NH * VD), BF16), jax.ShapeDtypeStruct((NH, T, 1), F32)]
        + [jax.ShapeDtypeStruct((4,) + g.shape, g.dtype) for g in gather],
        scratch_shapes=[pltpu.VMEM((rs, 1), F32)] * (2 * CHAINS) + [pltpu.VMEM((rs, VD), F32)] * CHAINS
        + (_gather_scratch(gather) if n else []),
        compiler_params=_cparams(("arbitrary", "arbitrary")),
    )(q, k, v, *gather)


def _attn_delta(o, do, *, tm=640):
    T = o.shape[0]

    def body(o_ref, do_ref, d_ref):
        prod = o_ref[...].astype(F32) * do_ref[...].astype(F32)
        for h in range(NH):
            d_ref[h, :, :] = jnp.sum(prod[:, h * VD:(h + 1) * VD], axis=-1, keepdims=True)

    return pl.pallas_call(
        body, name="attn_delta", grid=(T // tm,),
        in_specs=[_rows(tm, NH * VD), _rows(tm, NH * VD)],
        out_specs=pl.BlockSpec((NH, tm, 1), lambda i: (0, i, 0)),
        out_shape=jax.ShapeDtypeStruct((NH, T, 1), F32),
        compiler_params=_cparams(("parallel",)),
    )(o, do)


def _flash_bwd_dq(q, k, v, do, lse, delta, *, scatter=(), bq=None):
    T = q.shape[1]
    bq = bq or _attn_block(T)
    nq = T // bq
    rs = bq // CHAINS
    n = len(scatter)

    def body(*refs):
        q_ref, k_ref, v_ref, do_ref, lse_ref, dl_ref = refs[:6]
        s_src = refs[6:6 + n]
        dq_ref = refs[6 + n]
        s_out = refs[7 + n:7 + 2 * n]
        s_scr = refs[7 + 2 * n:]
        h = pl.program_id(0)
        i = pl.program_id(1)
        if n:
            @pl.when((h == 0) & (i == 0))
            def _():
                for cp in _scatter_descs(s_src, s_out, s_scr):
                    cp.start()

        dq_ref[...] = jnp.zeros_like(dq_ref)

        def step(j, masked, diag):
            off = pl.multiple_of(j * bq, bq)
            for r in range(CHAINS):
                rows = pl.ds(r * rs, rs)
                kw = min(bq, _round_up((r + 1) * rs, LANES)) if diag else bq
                kv_ = k_ref[pl.ds(off, kw), :]
                vv = v_ref[pl.ds(off, kw), :]
                s = lax.dot_general(q_ref[rows, :], kv_, _NT, preferred_element_type=F32)
                if masked:
                    s = jnp.where(_visible(i * bq + r * rs, j * bq, rs, kw), s, NEG)
                p = jnp.exp(s - lse_ref[rows, :])
                dp = lax.dot_general(do_ref[rows, :], vv, _NT, preferred_element_type=F32)
                ds = (p * (dp - dl_ref[rows, :])).astype(BF16)
                dq_ref[rows, :] += jnp.dot(ds, kv_, preferred_element_type=F32)

        @pl.when(i == 0)
        def _():
            step(0, True, True)

        @pl.when(i > 0)
        def _():
            step(0, True, False)

            def loop(j, c):
                step(j, False, False)
                return c

            lax.fori_loop(1, i, loop, 0)
            step(i, True, True)

        if n:
            @pl.when((h == NH - 1) & (i == nq - 1))
            def _():
                for cp in _scatter_descs(s_src, s_out, s_scr):
                    cp.wait()

    return pl.pallas_call(
        body, name="flash_bwd_dq", grid=(NH, nq),
        in_specs=[pl.BlockSpec((None, bq, QW), lambda h, i: (h, i, 0)),
                  pl.BlockSpec((None, T, QW), lambda h, i: (h, 0, 0)),
                  pl.BlockSpec((None, T, VD), lambda h, i: (h, 0, 0)),
                  pl.BlockSpec((bq, VD), lambda h, i: (i, h)),
                  pl.BlockSpec((None, bq, 1), lambda h, i: (h, i, 0)),
                  pl.BlockSpec((None, bq, 1), lambda h, i: (h, i, 0))] + [HBM] * n,
        out_specs=[pl.BlockSpec((None, bq, QW), lambda h, i: (h, i, 0))] + [HBM] * n,
        out_shape=[jax.ShapeDtypeStruct((NH, T, QW), F32)]
        + [jax.ShapeDtypeStruct((3,) + s.shape[1:], s.dtype) for s in scatter],
        scratch_shapes=[_dma_sems(3 * n), _dma_sems(3 * n)] if n else [],
        compiler_params=_cparams(("arbitrary", "arbitrary")),
    )(q, k, v, do, lse, delta, *scatter)


def _flash_bwd_dkv(q, k, v, do, lse_row, delta_row, *, bq=None):
    T = q.shape[1]
    bq = bq or _attn_block(T)
    nq = T // bq
    rs = bq // CHAINS

    def body(q_ref, k_ref, v_ref, do_ref, lse_ref, dl_ref, dk_ref, dv_ref):
        j = pl.program_id(1)
        dk_ref[...] = jnp.zeros_like(dk_ref)
        dv_ref[...] = jnp.zeros_like(dv_ref)

        def step(i, masked, diag):
            for r in range(CHAINS):
                rows = pl.ds(r * rs, rs)
                q0 = (r * rs) // LANES * LANES if diag else 0
                qn = bq - q0
                off = pl.multiple_of(i * bq + q0, LANES)
                qv = q_ref[pl.ds(off, qn), :]
                dov = do_ref[pl.ds(off, qn), :]
                lse_v = lse_ref[:, pl.ds(off, qn)]
                dl_v = dl_ref[:, pl.ds(off, qn)]
                st = lax.dot_general(k_ref[rows, :], qv, _NT, preferred_element_type=F32)
                if masked:
                    st = jnp.where(_visible_t(i * bq + q0, j * bq + r * rs, qn, rs), st, NEG)
                pt = jnp.exp(st - lse_v)
                dv_ref[rows, :] += jnp.dot(pt.astype(BF16), dov, preferred_element_type=F32)
                dpt = lax.dot_general(v_ref[rows, :], dov, _NT, preferred_element_type=F32)
                dst = (pt * (dpt - dl_v)).astype(BF16)
                dk_ref[rows, :] += jnp.dot(dst, qv, preferred_element_type=F32)

        step(j, True, True)

        @pl.when(j == 0)
        def _():
            def loop(i, c):
                step(i, True, False)
                return c
            lax.fori_loop(1, nq, loop, 0)

        @pl.when(j > 0)
        def _():
            def loop(i, c):
                step(i, False, False)
                return c
            lax.fori_loop(j + 1, nq, loop, 0)

    return pl.pallas_call(
        body, name="flash_bwd_dkv", grid=(NH, nq),
        in_specs=[pl.BlockSpec((None, T, QW), lambda h, j: (h, 0, 0)),
                  pl.BlockSpec((None, bq, QW), lambda h, j: (h, j, 0)),
                  pl.BlockSpec((None, bq, VD), lambda h, j: (h, j, 0)),
                  pl.BlockSpec((T, VD), lambda h, j: (0, h)),
                  pl.BlockSpec((None, 1, T), lambda h, j: (h, 0, 0)),
                  pl.BlockSpec((None, 1, T), lambda h, j: (h, 0, 0))],
        out_specs=[pl.BlockSpec((None, bq, QW), lambda h, j: (h, j, 0)),
                   pl.BlockSpec((None, bq, VD), lambda h, j: (h, j, 0))],
        out_shape=[jax.ShapeDtypeStruct((NH, T, QW), F32), jax.ShapeDtypeStruct((NH, T, VD), F32)],
        compiler_params=_cparams(("parallel", "parallel")),
    )(q, k, v, do, lse_row, delta_row)


def _rope_table(T):
    pos = (jnp.arange(T, dtype=jnp.int32) - PAD).astype(F32)
    inv_freq = ROPE_THETA ** (-jnp.arange(0, ROPE, 2, dtype=F32) / ROPE)
    ang = pos[:, None] * inv_freq[None, :]
    cos, sin = jnp.cos(ang), jnp.sin(ang)
    return jnp.concatenate([cos, cos, -sin, sin], axis=1)


def _swap_halves(w):
    return jnp.concatenate([w[..., ROPE // 2:], w[..., :ROPE // 2]], axis=-1)


O_UX, O_UG, O_UQ, O_UKV, O_UKR, O_UM = 0, DR, 2 * DR, 2 * DR + QR, 2 * DR + QR + KVR, 2 * DR + QR + KVR + ROPE


def _prep_weights(w):
    b = lambda a: a.astype(BF16)
    w_in = w["w_in"]
    kr = w_in[:, O_UKR:O_UM]
    p = {
        "w_xg": b(w_in[:, :O_UQ]),
        "w_q": b(w_in[:, O_UQ:O_UKV]),
        "w_kv": b(w_in[:, O_UKV:O_UKR]),
        "w_kr": b(jnp.concatenate([kr, _swap_halves(kr)], axis=1)),
        "w_m": b(w_in[:, O_UM:]),
    }
    wq = w["w_uq"].reshape(QR, NH, NOPE + ROPE)
    p["w_uq"] = b(jnp.concatenate([wq, _swap_halves(wq[..., NOPE:])], axis=-1).reshape(QR, NH * QW))
    p["w_ukv"] = b(w["w_ukv"])
    for n in ("w_xg", "w_q", "w_kv", "w_kr", "w_m", "w_uq", "w_ukv"):
        p[n + "_t"] = p[n].T
    p["wa"] = b(w["w_rec_a"])
    p["wi"] = b(w["w_rec_i"])
    p["wa_t"] = jnp.swapaxes(p["wa"], 1, 2)
    p["wi_t"] = jnp.swapaxes(p["wi"], 1, 2)
    return p


def _prep_late_weights(w):
    b = lambda a: a.astype(BF16)
    p = {"w_br": b(w["w_branch"][:DR]), "w_ba": b(w["w_branch"][DR:]), "w_out": b(w["w_out"]),
         "w_fi": b(w["w_ffn_in"]), "w_fo": b(w["w_ffn_out"])}
    for n in tuple(p):
        p[n + "_t"] = p[n].T
    return p


LATE = ("w_branch", "w_out", "w_ffn_in", "w_ffn_out")


def _local_step(x, tgt, w, late=None, reduce_first=None):
    S = x.shape[0]
    T = FRONT + S
    p = _prep_weights(w)
    tab = _rope_table(T)
    h0 = jnp.concatenate([jnp.zeros((PAD, D), F32), w["meta_tokens"], x], axis=0)
    row = lambda v: v.reshape(1, -1)

    z = _rmsnorm_fwd(h0, row(w["norm_mix_g"]), name="norm_mix")
    uxg = _mm(z, p["w_xg"], name="mm_uxg")
    uq = _mm(z, p["w_q"], name="mm_uq")
    ukv = _mm(z, p["w_kv"], name="mm_ukv")
    ukr = _mm(z, p["w_kr"], name="mm_ukr")
    um = _mm(z, p["w_m"], name="mm_um")
    rnn_w = (w["conv_w"], row(w["conv_b"]), p["wa"], row(w["b_rec_a"]), p["wi"], row(w["b_rec_i"]),
             row(w["lru_lambda"]))
    hs, y_rnn = _rnn_fwd(uxg, *rnn_w)
    qn = _rmsnorm_fwd(uq, row(w["q_norm_g"]), name="norm_q")
    kvn = _rmsnorm_fwd(ukv, row(w["kv_norm_g"]), name="norm_kv")
    q_all = _mm(qn, p["w_uq"], name="mm_q")
    kv_all = _mm(kvn, p["w_ukv"], name="mm_kv")
    qh, kh, vh = _attn_prep(q_all, kv_all, ukr, tab)
    y_att, lse, *stacks = _flash_fwd(qh, kh, vh, gather=late[0] if late else ())
    if late:
        w = {**w, **late[1](stacks)}
    p.update(_prep_late_weights(w))
    p_rnn = _mm(y_rnn, p["w_br"], name="mm_prnn")
    p_att = _mm(y_att, p["w_ba"], name="mm_patt")
    bg = row(w["b_gate"])
    mixed = _gate_mix_fwd(um, bg, p_rnn, p_att)
    h1 = _mm(mixed, p["w_out"], name="mm_out", res=h0)
    zf = _rmsnorm_fwd(h1, row(w["norm_ffn_g"]), name="norm_ffn")
    ff = _mm(zf, p["w_fi"], name="mm_ffn_in")
    act = _swiglu_fwd(ff)
    h2 = _mm(act, p["w_fo"], name="mm_ffn_out", res=h1)

    g = {}
    dh2, dh2b, dg_fin, lsum = _loss_head(h2, tgt, row(w["final_norm_g"]))
    loss = 0.5 * jnp.sum(lsum) / D
    g["final_norm_g"] = dg_fin.reshape(-1)
    dact = _mm(dh2b, p["w_fo_t"], name="mm_dact")
    g["w_ffn_out"] = _mm_tn(act, dh2b, name="mm_dw_ffn_out")
    dff = _swiglu_bwd(ff, dact)
    dzf = _mm(dff, p["w_fi_t"], name="mm_dzf")
    g["w_ffn_in"] = _mm_tn(zf, dff, name="mm_dw_ffn_in")
    dh1, dh1b, dg = _rmsnorm_bwd(h1, row(w["norm_ffn_g"]), dzf, dh2, name="norm_ffn_bwd")
    g["norm_ffn_g"] = dg
    dmixed = _mm(dh1b, p["w_out_t"], name="mm_dmixed")
    g["w_out"] = _mm_tn(mixed, dh1b, name="mm_dw_out")
    dp_rnn, dp_att, dum, dbg = _gate_mix_bwd(um, bg, p_rnn, p_att, dmixed)
    g["b_gate"] = dbg.reshape(2, D)
    dy_rnn = _mm(dp_rnn, p["w_br_t"], name="mm_dy_rnn")
    dy_att = _mm(dp_att, p["w_ba_t"], name="mm_dy_att", out_dtype=BF16)
    g["w_branch"] = jnp.concatenate([_mm_tn(y_rnn, dp_rnn, name="mm_dw_br"),
                                     _mm_tn(y_att, dp_att, name="mm_dw_ba")], axis=0)
    delta = _attn_delta(y_att, dy_att)
    first = reduce_first({n: g[n] for n in LATE}) if reduce_first else ()
    dq, *received = _flash_bwd_dq(qh, kh, vh, dy_att, lse, delta, scatter=first)
    dk, dv = _flash_bwd_dkv(qh, kh, vh, dy_att, lse.reshape(NH, 1, T), delta.reshape(NH, 1, T))
    dq_all, dkv_all, dukr = _attn_prep_bwd(dq, dk, dv, tab)
    dqn = _mm(dq_all, p["w_uq_t"], name="mm_dqn")
    dkvn = _mm(dkv_all, p["w_ukv_t"], name="mm_dkvn")
    dwq = _mm_tn(qn, dq_all, name="mm_dw_uq").reshape(QR, NH, QW)
    dwq_rope = dwq[..., NOPE:NOPE + ROPE] + _swap_halves(dwq[..., NOPE + ROPE:])
    g["w_uq"] = jnp.concatenate([dwq[..., :NOPE], dwq_rope], axis=-1).reshape(QR, NH * (NOPE + ROPE))
    g["w_ukv"] = _mm_tn(kvn, dkv_all, name="mm_dw_ukv")
    duq, dg = _rmsnorm_bwd(uq, row(w["q_norm_g"]), dqn, None, name="norm_q_bwd", want_f32=False)
    g["q_norm_g"] = dg
    dukv, dg = _rmsnorm_bwd(ukv, row(w["kv_norm_g"]), dkvn, None, name="norm_kv_bwd", want_f32=False)
    g["kv_norm_g"] = dg
    (dux, dug, g["conv_w"], g["conv_b"], g["w_rec_a"], g["b_rec_a"], g["w_rec_i"], g["b_rec_i"],
     g["lru_lambda"]) = _rnn_bwd(uxg, hs, dy_rnn, *rnn_w, p["wa_t"], p["wi_t"])
    dz = _mm(dux, p["w_xg_t"][:DR], name="mm_dz_x")
    dz = _mm(dug, p["w_xg_t"][DR:], name="mm_dz_g", res=dz)
    dz = _mm(duq, p["w_q_t"], name="mm_dz_q", res=dz)
    dz = _mm(dukv, p["w_kv_t"], name="mm_dz_kv", res=dz)
    dz = _mm(dukr, p["w_kr_t"], name="mm_dz_kr", res=dz)
    dz = _mm(dum, p["w_m_t"], name="mm_dz_m", res=dz)
    dwkr = _mm_tn(z, dukr, name="mm_dw_kr")
    g["w_in"] = jnp.concatenate([
        _mm_tn(z, dux, name="mm_dw_x"), _mm_tn(z, dug, name="mm_dw_g"),
        _mm_tn(z, duq, name="mm_dw_q"), _mm_tn(z, dukv, name="mm_dw_kv"),
        dwkr[:, :ROPE] + _swap_halves(dwkr[:, ROPE:]),
        _mm_tn(z, dum, name="mm_dw_m")], axis=1)
    dh0, dg = _rmsnorm_bwd(h0, row(w["norm_mix_g"]), dz, dh1, name="norm_mix_bwd", want_bf16=False)
    g["norm_mix_g"] = dg
    g["meta_tokens"] = dh0[PAD:FRONT]
    return loss, dh0[FRONT:], g, (first, received)


HBM = pl.BlockSpec(memory_space=pltpu.HBM)
CHIP_FLIPS = ((1, 0), (0, 1), (1, 1))


def _place():
    return lax.axis_index("x"), lax.axis_index("y"), lax.axis_index("c")


def _flip(v, f):
    return 1 - v if f else v


def _dma_sems(n):
    return pltpu.SemaphoreType.DMA((n,))


def _gather_scratch(srcs):
    n = len(srcs)
    return [pltpu.VMEM(s.shape, s.dtype) for s in srcs] + [_dma_sems(3 * n), _dma_sems(3 * n), _dma_sems(n),
                                                            _dma_sems(n)]


def _gather_descs(src_refs, out_refs, scr):
    n = len(src_refs)
    stage = scr[:n]
    send_sems, recv_sems, in_sems, local_sems = scr[n:]
    x, y, c = _place()
    me = 2 * x + y
    loads, sends, local = [], [], []
    for a in range(n):
        loads.append(pltpu.make_async_copy(src_refs[a], stage[a], in_sems.at[a]))
        for k, (fx, fy) in enumerate(CHIP_FLIPS):
            sends.append(pltpu.make_async_remote_copy(
                src_ref=stage[a], dst_ref=out_refs[a].at[me], send_sem=send_sems.at[3 * a + k],
                recv_sem=recv_sems.at[3 * a + k], device_id=(_flip(x, fx), _flip(y, fy), c),
                device_id_type=MESH))
        local.append(pltpu.make_async_copy(stage[a], out_refs[a].at[me], local_sems.at[a]))
    return loads, sends, local


def _gather_start(descs):
    loads, sends, local = descs
    for cp in loads:
        cp.start()
    for a, cp in enumerate(loads):
        cp.wait()
        for s in sends[3 * a:3 * a + 3]:
            s.start()
        local[a].start()


def _gather_wait(descs):
    _, sends, local = descs
    for cp in sends + local:
        cp.wait()


def _allgather_chips(srcs, *, name):
    n = len(srcs)

    def body(*refs):
        descs = _gather_descs(refs[:n], refs[n:2 * n], refs[2 * n:])
        _gather_start(descs)
        _gather_wait(descs)

    return pl.pallas_call(
        body, name=name, in_specs=[HBM] * n, out_specs=[HBM] * n,
        out_shape=[jax.ShapeDtypeStruct((4,) + s.shape, s.dtype) for s in srcs],
        scratch_shapes=_gather_scratch(srcs),
        compiler_params=pltpu.CompilerParams(vmem_limit_bytes=VMEM_LIMIT),
    )(*srcs)


def _scatter_descs(src_refs, out_refs, scr):
    send_sems, recv_sems = scr
    x, y, c = _place()
    copies = []
    for a in range(len(src_refs)):
        for k, (fx, fy) in enumerate(CHIP_FLIPS):
            px, py = _flip(x, fx), _flip(y, fy)
            copies.append(pltpu.make_async_remote_copy(
                src_ref=src_refs[a].at[2 * px + py], dst_ref=out_refs[a].at[k],
                send_sem=send_sems.at[3 * a + k], recv_sem=recv_sems.at[3 * a + k],
                device_id=(px, py, c), device_id_type=MESH))
    return copies


def _scatter_chips(srcs, *, name):
    n = len(srcs)

    def body(*refs):
        copies = _scatter_descs(refs[:n], refs[n:2 * n], refs[2 * n:])
        for cp in copies:
            cp.start()
        for cp in copies:
            cp.wait()

    return pl.pallas_call(
        body, name=name, in_specs=[HBM] * n, out_specs=[HBM] * n,
        out_shape=[jax.ShapeDtypeStruct((3,) + s.shape[1:], s.dtype) for s in srcs],
        scratch_shapes=[_dma_sems(3 * n), _dma_sems(3 * n)],
    )(*srcs)


def _sibling_take(srcs, *, name):
    n = len(srcs)

    def body(*refs):
        src_refs, out_refs = refs[:n], refs[n:2 * n]
        send_sems, recv_sems = refs[2 * n:]
        x, y, c = _place()
        copies = []
        for a in range(n):
            h = srcs[a].shape[1] // 2
            theirs = pl.ds(pl.multiple_of((1 - c) * h, 8), h)
            cp = pltpu.make_async_remote_copy(
                src_ref=src_refs[a].at[:, theirs, :], dst_ref=out_refs[a], send_sem=send_sems.at[a],
                recv_sem=recv_sems.at[a], device_id=(x, y, 1 - c), device_id_type=MESH)
            cp.start()
            copies.append(cp)
        for cp in copies:
            cp.wait()

    return pl.pallas_call(
        body, name=name, in_specs=[HBM] * n, out_specs=[HBM] * n,
        out_shape=[jax.ShapeDtypeStruct((4, s.shape[1] // 2, s.shape[2]), s.dtype) for s in srcs],
        scratch_shapes=[_dma_sems(n), _dma_sems(n)],
    )(*srcs)


def _sibling_swap(srcs, *, name):
    n = len(srcs)

    def body(*refs):
        src_refs, out_refs = refs[:n], refs[n:2 * n]
        send_sems, recv_sems = refs[2 * n:]
        x, y, c = _place()
        copies = []
        for a in range(n):
            cp = pltpu.make_async_remote_copy(
                src_ref=src_refs[a], dst_ref=out_refs[a], send_sem=send_sems.at[a],
                recv_sem=recv_sems.at[a], device_id=(x, y, 1 - c), device_id_type=MESH)
            cp.start()
            copies.append(cp)
        for cp in copies:
            cp.wait()

    return pl.pallas_call(
        body, name=name, in_specs=[HBM] * n, out_specs=[HBM] * n,
        out_shape=[jax.ShapeDtypeStruct(s.shape, s.dtype) for s in srcs],
        scratch_shapes=[_dma_sems(n), _dma_sems(n)],
    )(*srcs)


def _row_tile(rows, cols, n_arrays):
    budget = 24 * 1024 * 1024 // (2 * 4 * n_arrays * cols)
    best = 16
    for t in range(16, rows + 1, 16):
        if rows % t == 0 and t <= budget:
            best = t
    assert rows % best == 0, (rows, cols)
    return best


def _add_halves(mine, theirs, wire, *, name):
    _, h, c = mine.shape
    tm = _row_tile(h, c, 3)
    spec = pl.BlockSpec((None, tm, c), lambda s, i: (s, i, 0))

    def body(a_ref, b_ref, o_ref):
        o_ref[...] = (a_ref[...] + b_ref[...]).astype(wire)

    return pl.pallas_call(
        body, name=name, grid=(4, h // tm), in_specs=[spec, spec], out_specs=spec,
        out_shape=jax.ShapeDtypeStruct(mine.shape, wire), compiler_params=_cparams(("parallel", "parallel")),
    )(mine, theirs)


def _sum4(own, recv, *, name):
    h, c = own.shape
    tm = _row_tile(h, c, 5)

    def body(o_ref, r_ref, out_ref):
        f = lambda k: r_ref[k].astype(F32)
        out_ref[...] = ((o_ref[...].astype(F32) + f(0)) + f(1)) + f(2)

    return pl.pallas_call(
        body, name=name, grid=(h // tm,),
        in_specs=[_rows(tm, c), pl.BlockSpec((3, tm, c), lambda i: (0, i, 0))],
        out_specs=_rows(tm, c), out_shape=jax.ShapeDtypeStruct((h, c), F32),
        compiler_params=_cparams(("parallel",)),
    )(own, recv)


def _adamw(g, w, m, v, *, name):
    r, c = g.shape
    tm = _row_tile(r, c, 7)
    c1 = 1.0 / (1.0 - ADAM_B1 ** ADAM_STEP)
    c2 = 1.0 / (1.0 - ADAM_B2 ** ADAM_STEP)

    def body(g_ref, w_ref, m_ref, v_ref, d_ref, nm_ref, nv_ref):
        gv = g_ref[...]
        nm = ADAM_B1 * m_ref[...] + (1.0 - ADAM_B1) * gv
        nv = ADAM_B2 * v_ref[...] + (1.0 - ADAM_B2) * (gv * gv)
        nm_ref[...] = nm
        nv_ref[...] = nv
        d_ref[...] = -ADAM_LR * ((nm * c1) / (jnp.sqrt(nv * c2) + ADAM_EPS) + ADAM_WD * w_ref[...])

    spec = _rows(tm, c)
    shape = jax.ShapeDtypeStruct((r, c), F32)
    return pl.pallas_call(
        body, name=name, grid=(r // tm,), in_specs=[spec] * 4, out_specs=[spec] * 3,
        out_shape=[shape] * 3, compiler_params=_cparams(("parallel",)),
    )(g, w, m, v)


BIG = (("w_in", (D, 1328), 1), ("w_uq", (QR, 384), 1), ("w_ukv", (KVR, 512), 1), ("w_branch", (576, D), 0),
       ("w_out", (256, D), 0), ("w_ffn_in", (D, 1408), 1), ("w_ffn_out", (704, D), 0))
SMALL = (("meta_tokens", (NMETA, 256), 1), ("b_gate", (2, 256), 1), ("conv_w", (CW, 320), 1))
REPL = (("norm_mix_g", (D,)), ("conv_b", (DR,)), ("w_rec_a", (NBLK, RB, RB)), ("b_rec_a", (DR,)),
        ("w_rec_i", (NBLK, RB, RB)), ("b_rec_i", (DR,)), ("lru_lambda", (DR,)), ("q_norm_g", (QR,)),
        ("kv_norm_g", (KVR,)), ("norm_ffn_g", (D,)), ("final_norm_g", (D,)))
WEIGHTS = ("meta_tokens", "norm_mix_g", "w_in", "b_gate", "conv_w", "conv_b", "w_rec_a", "b_rec_a", "w_rec_i",
           "b_rec_i", "lru_lambda", "q_norm_g", "w_uq", "kv_norm_g", "w_ukv", "w_branch", "w_out", "norm_ffn_g",
           "w_ffn_in", "w_ffn_out", "final_norm_g")
W = 1024
SMALL_N = sum(math.prod(s) for _, s, _ in SMALL)
SMALL_ROWS = 8
REPL_N = sum(math.prod(s) for _, s in REPL)
QUART_ROWS = 88
assert SMALL_N <= SMALL_ROWS * W and REPL_N <= 4 * QUART_ROWS * W


def _flat_pad(parts, rows):
    v = jnp.concatenate([p.reshape(-1) for p in parts])
    return jnp.pad(v, (0, rows * W - v.shape[0])).reshape(rows, W)


def _shard_stack(full, shard_shape, axis):
    r, cs = shard_shape
    if axis == 0:
        return full.reshape(4, r, cs)
    return jnp.stack([full[:, s * cs:(s + 1) * cs] for s in range(4)])


def _unshard(stack, axis):
    if axis == 0:
        return stack.reshape(4 * stack.shape[1], stack.shape[2])
    return jnp.concatenate([stack[s] for s in range(4)], axis=1)


def _split(flat, table):
    out, off = {}, 0
    for name, shape, *_ in table:
        n = math.prod(shape)
        out[name] = flat[..., off:off + n].reshape(flat.shape[:-1] + tuple(shape))
        off += n
    return out


def _misc_state(args, prefix, quarter):
    small = _flat_pad([args[prefix + n] for n, _, _ in SMALL], SMALL_ROWS)
    repl = _flat_pad([args[prefix + n] for n, _ in REPL], 4 * QUART_ROWS)
    mine = lax.dynamic_slice_in_dim(repl, quarter * QUART_ROWS, QUART_ROWS, axis=0)
    return jnp.concatenate([small, mine], axis=0)


def kernel(x, meta_tokens, norm_mix_g, w_in, b_gate, conv_w, conv_b, w_rec_a, b_rec_a, w_rec_i, b_rec_i, lru_lambda, q_norm_g, w_uq, kv_norm_g, w_ukv, w_branch, w_out, norm_ffn_g, w_ffn_in, w_ffn_out, final_norm_g, loss_target, m_meta_tokens, m_norm_mix_g, m_w_in, m_b_gate, m_conv_w, m_conv_b, m_w_rec_a, m_b_rec_a, m_w_rec_i, m_b_rec_i, m_lru_lambda, m_q_norm_g, m_w_uq, m_kv_norm_g, m_w_ukv, m_w_branch, m_w_out, m_norm_ffn_g, m_w_ffn_in, m_w_ffn_out, m_final_norm_g, v_meta_tokens, v_norm_mix_g, v_w_in, v_b_gate, v_conv_w, v_conv_b, v_w_rec_a, v_b_rec_a, v_w_rec_i, v_b_rec_i, v_lru_lambda, v_q_norm_g, v_w_uq, v_kv_norm_g, v_w_ukv, v_w_branch, v_w_out, v_norm_ffn_g, v_w_ffn_in, v_w_ffn_out, v_final_norm_g):
    args = dict(locals())
    chip = 2 * lax.axis_index("x") + lax.axis_index("y")
    core = lax.axis_index("c")

    first_big = [b for b in BIG if b[0] not in LATE]
    late_big = [b for b in BIG if b[0] in LATE]
    bf16_shard = lambda n, s: args[n].reshape(s).astype(BF16)
    small = _flat_pad([args[n] for n, _, _ in SMALL], SMALL_ROWS)
    gathered = _allgather_chips([bf16_shard(n, s) for n, s, _ in first_big] + [small], name="gather_weights")
    w = {}
    for (name, _, axis), stack in zip(first_big, gathered):
        w[name] = _unshard(stack, axis)
    small_parts = _split(gathered[-1].reshape(4, SMALL_ROWS * W), SMALL)
    for name, _, axis in SMALL:
        w[name] = _unshard(small_parts[name], axis)
    for name, shape in REPL:
        w[name] = args[name].reshape(shape)
    finish_late = lambda stacks: {name: _unshard(st, axis) for (name, _, axis), st in zip(late_big, stacks)}

    def to_wire(red, tag, wires):
        theirs = _sibling_take(red, name="reduce_sibling_" + tag)
        parts = []
        for k, (a, t) in enumerate(zip(red, theirs)):
            h = a.shape[1] // 2
            mine = lax.dynamic_slice_in_dim(a, core * h, h, axis=1)
            parts.append(_add_halves(mine, t, wires[k], name=f"add_sibling_{tag}{k}"))
        return parts

    reduce_first = lambda gl: to_wire([_shard_stack(gl[n], s, a) for n, s, a in late_big], "a",
                                      [BF16] * len(late_big))
    loss, grad_x, g, (parts_a, recv_a) = _local_step(
        x[0], loss_target[0], w, late=([bf16_shard(n, s) for n, s, _ in late_big], finish_late),
        reduce_first=reduce_first)
    loss = lax.psum(loss, ("x", "y", "c"))

    red = [_shard_stack(g[n], s, a) for n, s, a in first_big]
    small_g = jnp.concatenate([_shard_stack(g[n], s, a).reshape(4, -1) for n, s, a in SMALL], axis=1)
    small_g = jnp.pad(small_g, ((0, 0), (0, SMALL_ROWS * W - SMALL_N))).reshape(4, SMALL_ROWS, W)
    repl_g = _flat_pad([g[n] for n, _ in REPL], 4 * QUART_ROWS).reshape(4, QUART_ROWS, W)
    red.append(jnp.concatenate([small_g, repl_g], axis=1))
    parts_b = to_wire(red, "b", [BF16] * len(first_big) + [F32])
    recv_b = _scatter_chips(parts_b, name="reduce_chips")
    order = [b[0] for b in late_big] + [b[0] for b in first_big] + ["misc"]
    halves = [_sum4(lax.dynamic_index_in_dim(p, chip, 0, keepdims=False), r, name="sum_chips_" + n)
              for n, p, r in zip(order, list(parts_a) + parts_b, list(recv_a) + list(recv_b))]
    others = _sibling_swap(halves, name="share_sibling")
    gred = [jnp.where(core == 0, jnp.concatenate([a, b], axis=0), jnp.concatenate([b, a], axis=0))
            for a, b in zip(halves, others)]

    results = {}
    shape_of = {name: shape for name, shape, _ in BIG}
    for name, gr in zip(order[:-1], gred):
        shape = shape_of[name]
        d, nm, nv = _adamw(gr, args[name].reshape(shape), args["m_" + name].reshape(shape),
                           args["v_" + name].reshape(shape), name="adamw_" + name)
        results[name] = (gr, d, nm, nv)
    d, nm, nv = _adamw(gred[-1], _misc_state(args, "", chip), _misc_state(args, "m_", chip),
                       _misc_state(args, "v_", chip), name="adamw_misc")
    misc = (gred[-1], d, nm, nv)

    quarters = jnp.stack([t[SMALL_ROWS:] for t in misc])
    repl_all = _allgather_chips([quarters], name="gather_repl")[0]
    repl_all = repl_all.transpose(1, 0, 2, 3).reshape(4, 4 * QUART_ROWS * W)

    outs = []
    for k in range(4):
        sm = _split(misc[k][:SMALL_ROWS].reshape(-1), SMALL)
        rp = _split(repl_all[k], REPL)
        for name in WEIGHTS:
            val = results[name][k] if name in results else (sm[name] if name in sm else rp[name])
            outs.append(val.reshape(args[name].shape))
    return (loss, grad_x[None], *outs)
```

```python
import functools
import math

import jax
import jax.numpy as jnp
from jax import lax
from jax.experimental import pallas as pl
from jax.experimental.pallas import tpu as pltpu

F32 = jnp.float32
BF16 = jnp.bfloat16

D = 1024
DR = 1280
NBLK = 10
RB = 128
CW = 4
NH = 8
NOPE = 128
ROPE = 64
VD = 128
QR = 384
KVR = 256
DFF = 2816
NMETA = 16
EPS = 1e-6
LRU_C = 8.0
ROPE_THETA = 10000.0
SCALE = 1.0 / math.sqrt(NOPE + ROPE)
NEG = -1e30
FRONT = 128
PAD = FRONT - NMETA
QW = 2 * NOPE
LANES = 128
SUB = 128
CHAINS = 4
VMEM_LIMIT = 52 * 1024 * 1024

ADAM_LR = 0.001
ADAM_B1 = 0.9
ADAM_B2 = 0.999
ADAM_EPS = 1e-08
ADAM_WD = 0.01
ADAM_STEP = 10

MESH = pl.DeviceIdType.MESH


def _cparams(sem):
    return pltpu.CompilerParams(dimension_semantics=sem, vmem_limit_bytes=VMEM_LIMIT)


def _sigmoid(x):
    return 1.0 / (1.0 + jnp.exp(-x))


def _gelu_parts(x):
    c = math.sqrt(2.0 / math.pi)
    inner = c * (x + 0.044715 * x * x * x)
    t = jnp.tanh(inner)
    g = 0.5 * x * (1.0 + t)
    dg = 0.5 * (1.0 + t) + 0.5 * x * (1.0 - t * t) * c * (1.0 + 3.0 * 0.044715 * x * x)
    return g, dg


def _divisors(n, step, cap):
    return [d for d in range(step, min(n, cap) + 1, step) if n % d == 0] or [n]


MM_VMEM_BUDGET = 40 * 1024 * 1024
MM_MAX_ROWS = 1664
MM_MAX_COLS = 1408


def _mm_tiles(M, K, N, a_item, out_item, has_res):
    best = None
    for tn in _divisors(N, LANES, MM_MAX_COLS):
        for tm in _divisors(M, 16, MM_MAX_ROWS):
            need = 2 * (tm * K * a_item + K * tn * 2 + tm * tn * (out_item + (4 if has_res else 0)))
            if need <= MM_VMEM_BUDGET and (best is None or tm * tn > best[0] * best[1]):
                best = (tm, tn)
    assert best is not None, (M, K, N)
    return best


def _mm(a, b, *, name, out_dtype=F32, res=None):
    M, K = a.shape
    N = b.shape[1]
    has_res = res is not None
    tm, tn = _mm_tiles(M, K, N, a.dtype.itemsize, jnp.dtype(out_dtype).itemsize, has_res)

    def body(*refs):
        if has_res:
            a_ref, b_ref, r_ref, o_ref = refs
        else:
            a_ref, b_ref, o_ref = refs
        acc = jnp.dot(a_ref[...].astype(BF16), b_ref[...].astype(BF16), preferred_element_type=F32)
        if has_res:
            acc = acc + r_ref[...].astype(F32)
        o_ref[...] = acc.astype(o_ref.dtype)

    a_bytes = M * K * a.dtype.itemsize
    b_bytes = K * N * b.dtype.itemsize
    rows_outer = a_bytes + (M // tm) * b_bytes <= b_bytes + (N // tn) * a_bytes
    if rows_outer:
        grid = (M // tm, N // tn)
        ia, ib, io = (lambda i, j: (i, 0)), (lambda i, j: (0, j)), (lambda i, j: (i, j))
    else:
        grid = (N // tn, M // tm)
        ia, ib, io = (lambda j, i: (i, 0)), (lambda j, i: (0, j)), (lambda j, i: (i, j))
    in_specs = [pl.BlockSpec((tm, K), ia), pl.BlockSpec((K, tn), ib)]
    args = [a, b]
    if has_res:
        in_specs.append(pl.BlockSpec((tm, tn), io))
        args.append(res)
    return pl.pallas_call(
        body, name=name, grid=grid, in_specs=in_specs,
        out_specs=pl.BlockSpec((tm, tn), io),
        out_shape=jax.ShapeDtypeStruct((M, N), out_dtype),
        compiler_params=_cparams(("parallel", "parallel")),
    )(*args)


def _mm_tn(a, b, *, name):
    T, K1 = a.shape
    N = b.shape[1]
    tt = _divisors(T, 16, MM_MAX_ROWS)[-1]
    tk = _divisors(K1, LANES, MM_MAX_COLS)[-1]
    tn = _divisors(N, LANES, MM_MAX_COLS)[-1]

    def body(a_ref, b_ref, o_ref):
        @pl.when(pl.program_id(2) == 0)
        def _():
            o_ref[...] = jnp.zeros_like(o_ref)

        o_ref[...] += lax.dot_general(a_ref[...].astype(BF16), b_ref[...].astype(BF16),
                                      (((0,), (0,)), ((), ())), preferred_element_type=F32)

    return pl.pallas_call(
        body, name=name, grid=(K1 // tk, N // tn, T // tt),
        in_specs=[pl.BlockSpec((tt, tk), lambda i, j, t: (t, i)),
                  pl.BlockSpec((tt, tn), lambda i, j, t: (t, j))],
        out_specs=pl.BlockSpec((tk, tn), lambda i, j, t: (i, j)),
        out_shape=jax.ShapeDtypeStruct((K1, N), F32),
        compiler_params=_cparams(("parallel", "parallel", "arbitrary")),
    )(a, b)


def _rows(tm, w, cb=0):
    return pl.BlockSpec((tm, w), lambda i: (i, cb))


def _const(shape):
    n = len(shape)
    return pl.BlockSpec(shape, lambda i: (0,) * n)


def _rmsnorm_fwd(x, g, *, name, tm=640):
    T, C = x.shape

    def body(x_ref, g_ref, o_ref):
        xv = x_ref[...]
        r = lax.rsqrt(jnp.mean(xv * xv, axis=-1, keepdims=True) + EPS)
        o_ref[...] = ((xv * r) * g_ref[...]).astype(BF16)

    return pl.pallas_call(
        body, name=name, grid=(T // tm,),
        in_specs=[_rows(tm, C), _const((1, C))],
        out_specs=_rows(tm, C),
        out_shape=jax.ShapeDtypeStruct((T, C), BF16),
        compiler_params=_cparams(("parallel",)),
    )(x, g)


def _rmsnorm_bwd(x, g, dy, res, *, name, tm=640, want_f32=True, want_bf16=True):
    T, C = x.shape
    has_res = res is not None

    def body(*refs):
        refs = list(refs)
        x_ref, g_ref, dy_ref = refs[:3]
        refs = refs[3:]
        r_ref = refs.pop(0) if has_res else None
        o32 = refs.pop(0) if want_f32 else None
        o16 = refs.pop(0) if want_bf16 else None
        dg_ref = refs.pop(0)

        @pl.when(pl.program_id(0) == 0)
        def _():
            dg_ref[...] = jnp.zeros_like(dg_ref)

        xv = x_ref[...]
        dyv = dy_ref[...].astype(F32)
        r = lax.rsqrt(jnp.mean(xv * xv, axis=-1, keepdims=True) + EPS)
        xn = xv * r
        dg_ref[...] += jnp.sum(dyv * xn, axis=0, keepdims=True)
        dxn = dyv * g_ref[...]
        dx = r * (dxn - xn * jnp.mean(dxn * xn, axis=-1, keepdims=True))
        if has_res:
            dx = dx + r_ref[...]
        if want_f32:
            o32[...] = dx
        if want_bf16:
            o16[...] = dx.astype(BF16)

    in_specs = [_rows(tm, C), _const((1, C)), _rows(tm, C)]
    args = [x, g, dy]
    if has_res:
        in_specs.append(_rows(tm, C))
        args.append(res)
    out_specs, out_shape = [], []
    if want_f32:
        out_specs.append(_rows(tm, C))
        out_shape.append(jax.ShapeDtypeStruct((T, C), F32))
    if want_bf16:
        out_specs.append(_rows(tm, C))
        out_shape.append(jax.ShapeDtypeStruct((T, C), BF16))
    out_specs.append(_const((1, C)))
    out_shape.append(jax.ShapeDtypeStruct((1, C), F32))
    return pl.pallas_call(
        body, name=name, grid=(T // tm,), in_specs=in_specs, out_specs=out_specs,
        out_shape=out_shape, compiler_params=_cparams(("arbitrary",)),
    )(*args)


def _gate_mix_fwd(um, bg, p_rnn, p_att, *, tm=320):
    T = um.shape[0]

    def body(um_ref, bg_ref, pr_ref, pa_ref, o_ref):
        g = _sigmoid(um_ref[...] + bg_ref[...])
        o_ref[...] = (g[:, :D] * pr_ref[...] + g[:, D:] * pa_ref[...]).astype(BF16)

    return pl.pallas_call(
        body, name="gate_mix_fwd", grid=(T // tm,),
        in_specs=[_rows(tm, 2 * D), _const((1, 2 * D)), _rows(tm, D), _rows(tm, D)],
        out_specs=_rows(tm, D),
        out_shape=jax.ShapeDtypeStruct((T, D), BF16),
        compiler_params=_cparams(("parallel",)),
    )(um, bg, p_rnn, p_att)


def _gate_mix_bwd(um, bg, p_rnn, p_att, dmixed, *, tm=320):
    T = um.shape[0]

    def body(um_ref, bg_ref, pr_ref, pa_ref, dm_ref, dpr_ref, dpa_ref, dum_ref, dbg_ref):
        @pl.when(pl.program_id(0) == 0)
        def _():
            dbg_ref[...] = jnp.zeros_like(dbg_ref)

        g = _sigmoid(um_ref[...] + bg_ref[...])
        g0, g1 = g[:, :D], g[:, D:]
        dm = dm_ref[...]
        dpr_ref[...] = (dm * g0).astype(BF16)
        dpa_ref[...] = (dm * g1).astype(BF16)
        d0 = dm * pr_ref[...] * g0 * (1.0 - g0)
        d1 = dm * pa_ref[...] * g1 * (1.0 - g1)
        dum_ref[:, :D] = d0.astype(BF16)
        dum_ref[:, D:] = d1.astype(BF16)
        dbg_ref[:, :D] += jnp.sum(d0, axis=0, keepdims=True)
        dbg_ref[:, D:] += jnp.sum(d1, axis=0, keepdims=True)

    return pl.pallas_call(
        body, name="gate_mix_bwd", grid=(T // tm,),
        in_specs=[_rows(tm, 2 * D), _const((1, 2 * D)), _rows(tm, D), _rows(tm, D), _rows(tm, D)],
        out_specs=[_rows(tm, D), _rows(tm, D), _rows(tm, 2 * D), _const((1, 2 * D))],
        out_shape=[jax.ShapeDtypeStruct((T, D), BF16), jax.ShapeDtypeStruct((T, D), BF16),
                   jax.ShapeDtypeStruct((T, 2 * D), BF16), jax.ShapeDtypeStruct((1, 2 * D), F32)],
        compiler_params=_cparams(("arbitrary",)),
    )(um, bg, p_rnn, p_att, dmixed)


def _swiglu_fwd(ff, *, tm=320):
    T = ff.shape[0]

    def body(g_ref, u_ref, o_ref):
        gv = g_ref[...]
        o_ref[...] = (gv * _sigmoid(gv) * u_ref[...]).astype(BF16)

    return pl.pallas_call(
        body, name="swiglu_fwd", grid=(T // tm,),
        in_specs=[_rows(tm, DFF, 0), _rows(tm, DFF, 1)],
        out_specs=_rows(tm, DFF),
        out_shape=jax.ShapeDtypeStruct((T, DFF), BF16),
        compiler_params=_cparams(("parallel",)),
    )(ff, ff)


def _swiglu_bwd(ff, dact, *, tm=320):
    T = ff.shape[0]

    def body(g_ref, u_ref, da_ref, o_ref):
        gv = g_ref[...]
        s = _sigmoid(gv)
        da = da_ref[...]
        o_ref[:, :DFF] = (da * u_ref[...] * s * (1.0 + gv * (1.0 - s))).astype(BF16)
        o_ref[:, DFF:] = (da * gv * s).astype(BF16)

    return pl.pallas_call(
        body, name="swiglu_bwd", grid=(T // tm,),
        in_specs=[_rows(tm, DFF, 0), _rows(tm, DFF, 1), _rows(tm, DFF)],
        out_specs=_rows(tm, 2 * DFF),
        out_shape=jax.ShapeDtypeStruct((T, 2 * DFF), BF16),
        compiler_params=_cparams(("parallel",)),
    )(ff, ff, dact)


def _loss_head(h2, tgt, g, *, tm=FRONT):
    T = h2.shape[0]
    front_blocks = FRONT // tm

    def body(h_ref, t_ref, g_ref, d32_ref, d16_ref, dg_ref, ls_ref):
        i = pl.program_id(0)

        @pl.when(i == 0)
        def _():
            dg_ref[...] = jnp.zeros_like(dg_ref)
            ls_ref[...] = jnp.zeros_like(ls_ref)

        xv = h_ref[...]
        r = lax.rsqrt(jnp.mean(xv * xv, axis=-1, keepdims=True) + EPS)
        xn = xv * r
        gv = g_ref[...]
        e = jnp.where(i >= front_blocks, xn * gv - t_ref[...], 0.0)
        ls_ref[...] += jnp.sum(e * e, axis=0, keepdims=True)
        dy = e * (1.0 / D)
        dg_ref[...] += jnp.sum(dy * xn, axis=0, keepdims=True)
        dxn = dy * gv
        dx = r * (dxn - xn * jnp.mean(dxn * xn, axis=-1, keepdims=True))
        d32_ref[...] = dx
        d16_ref[...] = dx.astype(BF16)

    return pl.pallas_call(
        body, name="loss_head", grid=(T // tm,),
        in_specs=[_rows(tm, D), pl.BlockSpec((tm, D), lambda i: (jnp.maximum(i - front_blocks, 0), 0)),
                  _const((1, D))],
        out_specs=[_rows(tm, D), _rows(tm, D), _const((1, D)), _const((1, D))],
        out_shape=[jax.ShapeDtypeStruct((T, D), F32), jax.ShapeDtypeStruct((T, D), BF16),
                   jax.ShapeDtypeStruct((1, D), F32), jax.ShapeDtypeStruct((1, D), F32)],
        compiler_params=_cparams(("arbitrary",)),
    )(h2, tgt, g)


def _scan_fwd(a, b, h_in):
    n = a.shape[0]
    row = lax.broadcasted_iota(jnp.int32, a.shape, 0)
    s = 1
    while s < n:
        a_sh = jnp.where(row >= s, pltpu.roll(a, s, 0), 1.0)
        b_sh = jnp.where(row >= s, pltpu.roll(b, s, 0), 0.0)
        b = a * b_sh + b
        a = a * a_sh
        s *= 2
    return b + a * h_in


def _scan_rev(a, b, g_in):
    n = a.shape[0]
    row = lax.broadcasted_iota(jnp.int32, a.shape, 0)
    s = 1
    while s < n:
        a_sh = jnp.where(row < n - s, pltpu.roll(a, n - s, 0), 1.0)
        b_sh = jnp.where(row < n - s, pltpu.roll(b, n - s, 0), 0.0)
        b = a * b_sh + b
        a = a * a_sh
        s *= 2
    return b + a * g_in


def _lru_gates(xc, wa, ba, wi, bi, lam):
    xcb = xc.astype(BF16)
    r = _sigmoid(jnp.dot(xcb, wa, preferred_element_type=F32) + ba)
    ig = _sigmoid(jnp.dot(xcb, wi, preferred_element_type=F32) + bi)
    log_sig = jnp.minimum(lam, 0.0) - jnp.log(1.0 + jnp.exp(-jnp.abs(lam)))
    log_a = LRU_C * r * log_sig
    a = jnp.exp(log_a)
    z = 2.0 * log_a
    poly = -z * (1.0 + z * (0.5 + z * (1.0 / 6.0 + z * (1.0 / 24.0))))
    m2 = jnp.where(z > -0.03, poly, 1.0 - jnp.exp(z))
    return r, ig, log_sig, a, jnp.sqrt(m2)


def _rnn_specs(tc, nblk_t, rev):
    def tmap(k):
        return (nblk_t - 1 - k) if rev else k

    hb = tc // 8
    blk = lambda off: pl.BlockSpec((tc, RB), lambda c, k: (tmap(k), c + off))
    halo = lambda off: pl.BlockSpec((8, RB), lambda c, k: (jnp.maximum(tmap(k) * hb - 1, 0), c + off))
    vec = pl.BlockSpec((1, RB), lambda c, k: (0, c))
    cwv = pl.BlockSpec((CW, RB), lambda c, k: (0, c))
    mat = pl.BlockSpec((None, RB, RB), lambda c, k: (c, 0, 0))
    return blk, halo, vec, cwv, mat


def _rnn_fwd(uxg, cw, cb, wa, ba, wi, bi, lam, *, tc=640):
    T = uxg.shape[0]
    nt = T // tc
    nsub = tc // SUB
    blk, halo, vec, cwv, mat = _rnn_specs(tc, nt, False)

    def body(x_ref, xh_ref, ug_ref, cw_ref, cb_ref, wa_ref, ba_ref, wi_ref, bi_ref, lam_ref,
             h_ref, y_ref, xb, hc):
        k = pl.program_id(1)

        @pl.when(k == 0)
        def _():
            hc[...] = jnp.zeros_like(hc)

        xb[0:8, :] = jnp.where(k > 0, xh_ref[...], 0.0)
        xb[8:, :] = x_ref[...]
        cwv_, cbv = cw_ref[...], cb_ref[...]
        wav, wiv = wa_ref[...], wi_ref[...]
        bav, biv, lamv = ba_ref[...], bi_ref[...], lam_ref[...]
        h_in = hc[0:1, :]
        for sc in range(nsub):
            r0 = sc * SUB
            xc = cbv + cwv_[0:1, :] * xb[pl.ds(5 + r0, SUB), :]
            for j in range(1, CW):
                xc = xc + cwv_[j:j + 1, :] * xb[pl.ds(5 + j + r0, SUB), :]
            r, ig, _, a, mm = _lru_gates(xc, wav, bav, wiv, biv, lamv)
            rows = k * tc + r0 + lax.broadcasted_iota(jnp.int32, (SUB, RB), 0)
            b = jnp.where(rows >= PAD, mm * (ig * xc), 0.0)
            h = _scan_fwd(a, b, h_in)
            h_in = h[SUB - 1:SUB, :]
            h_ref[pl.ds(r0, SUB), :] = h
            gl, _ = _gelu_parts(ug_ref[pl.ds(r0, SUB), :])
            y_ref[pl.ds(r0, SUB), :] = (h * gl).astype(BF16)
        hc[0:1, :] = h_in

    return pl.pallas_call(
        body, name="rnn_fwd", grid=(NBLK, nt),
        in_specs=[blk(0), halo(0), blk(NBLK), cwv, vec, mat, vec, mat, vec, vec],
        out_specs=[blk(0), blk(0)],
        out_shape=[jax.ShapeDtypeStruct((T, DR), F32), jax.ShapeDtypeStruct((T, DR), BF16)],
        scratch_shapes=[pltpu.VMEM((tc + 8, RB), F32), pltpu.VMEM((8, RB), F32)],
        compiler_params=_cparams(("parallel", "arbitrary")),
    )(uxg, uxg, uxg, cw, cb, wa, ba, wi, bi, lam)


def _rnn_bwd(uxg, hs, dy, cw, cb, wa, ba, wi, bi, lam, wat, wit, *, tc=640):
    T = uxg.shape[0]
    nt = T // tc
    nsub = tc // SUB
    blk, halo, vec, cwv, mat = _rnn_specs(tc, nt, True)

    def body(x_ref, xh_ref, ug_ref, h_ref, hh_ref, dy_ref, cw_ref, cb_ref, wa_ref, ba_ref, wi_ref,
             bi_ref, lam_ref, wat_ref, wit_ref,
             dux_ref, dug_ref, dcw_ref, dcb_ref, dwa_ref, dba_ref, dwi_ref, dbi_ref, dlam_ref,
             xb, hb, ab, dxb, xcs, rs, igs, mms, dgas, dgis, carry):
        k = pl.program_id(1)
        kt = nt - 1 - k

        @pl.when(k == 0)
        def _():
            carry[...] = jnp.zeros_like(carry)
            for ref in (dcw_ref, dcb_ref, dwa_ref, dba_ref, dwi_ref, dbi_ref, dlam_ref):
                ref[...] = jnp.zeros_like(ref)

        xb[0:8, :] = jnp.where(kt > 0, xh_ref[...], 0.0)
        xb[8:, :] = x_ref[...]
        hb[0:8, :] = jnp.where(kt > 0, hh_ref[...], 0.0)
        hb[8:, :] = h_ref[...]
        cwv_, cbv = cw_ref[...], cb_ref[...]
        wav, wiv = wa_ref[...], wi_ref[...]
        bav, biv, lamv = ba_ref[...], bi_ref[...], lam_ref[...]
        ab[tc:tc + 8, :] = jnp.broadcast_to(carry[1:2, :], (8, RB))
        dxb[tc:tc + 8, :] = carry[8:16, :]
        log_sig = None
        for sc in range(nsub):
            r0 = sc * SUB
            xc = cbv + cwv_[0:1, :] * xb[pl.ds(5 + r0, SUB), :]
            for j in range(1, CW):
                xc = xc + cwv_[j:j + 1, :] * xb[pl.ds(5 + j + r0, SUB), :]
            r, ig, log_sig, a, mm = _lru_gates(xc, wav, bav, wiv, biv, lamv)
            xcs[pl.ds(r0, SUB), :] = xc
            rs[pl.ds(r0, SUB), :] = r
            igs[pl.ds(r0, SUB), :] = ig
            mms[pl.ds(r0, SUB), :] = mm
            ab[pl.ds(r0, SUB), :] = a
        sig_neg = _sigmoid(-lamv)
        g_in = carry[0:1, :]
        dlam_acc = jnp.zeros((1, RB), F32)
        for sc in reversed(range(nsub)):
            r0 = sc * SUB
            xc, r, ig, mm = xcs[pl.ds(r0, SUB), :], rs[pl.ds(r0, SUB), :], igs[pl.ds(r0, SUB), :], mms[pl.ds(r0, SUB), :]
            a = ab[pl.ds(r0, SUB), :]
            a_next = ab[pl.ds(r0 + 1, SUB), :]
            hv = hb[pl.ds(8 + r0, SUB), :]
            hprev = hb[pl.ds(7 + r0, SUB), :]
            dyv = dy_ref[pl.ds(r0, SUB), :]
            gl, dgl = _gelu_parts(ug_ref[pl.ds(r0, SUB), :])
            dug_ref[pl.ds(r0, SUB), :] = (dyv * hv * dgl).astype(BF16)
            G = _scan_rev(a_next, dyv * gl, g_in)
            g_in = G[0:1, :]
            rows = kt * tc + r0 + lax.broadcasted_iota(jnp.int32, (SUB, RB), 0)
            db = jnp.where(rows >= PAD, G, 0.0)
            da = G * hprev
            dmm = db * (ig * xc)
            di = db * (mm * xc)
            dxc = db * (mm * ig)
            dlog_a = da * a - dmm * (a * a) / jnp.maximum(mm, 1e-30)
            dr = dlog_a * (LRU_C * log_sig)
            dlam_acc = dlam_acc + jnp.sum(dlog_a * (LRU_C * r), axis=0, keepdims=True)
            dga = dr * r * (1.0 - r)
            dgi = di * ig * (1.0 - ig)
            dgab, dgib = dga.astype(BF16), dgi.astype(BF16)
            dgas[pl.ds(r0, SUB), :] = dgab
            dgis[pl.ds(r0, SUB), :] = dgib
            dba_ref[...] += jnp.sum(dga, axis=0, keepdims=True)
            dbi_ref[...] += jnp.sum(dgi, axis=0, keepdims=True)
            dxc = dxc + jnp.dot(dgab, wat_ref[...], preferred_element_type=F32) \
                + jnp.dot(dgib, wit_ref[...], preferred_element_type=F32)
            dxb[pl.ds(r0, SUB), :] = dxc
        dlam_ref[...] += dlam_acc * sig_neg
        xcb = xcs[...].astype(BF16)
        tn = (((0,), (0,)), ((), ()))
        dwa_ref[...] += lax.dot_general(xcb, dgas[...], tn, preferred_element_type=F32)
        dwi_ref[...] += lax.dot_general(xcb, dgis[...], tn, preferred_element_type=F32)
        dxc_all = dxb[0:tc, :]
        dcb_ref[...] += jnp.sum(dxc_all, axis=0, keepdims=True)
        rows_all = kt * tc + lax.broadcasted_iota(jnp.int32, (tc, RB), 0)
        dux = jnp.zeros((tc, RB), F32)
        for j in range(CW):
            dcw_ref[j:j + 1, :] += jnp.sum(dxc_all * xb[pl.ds(5 + j, tc), :], axis=0, keepdims=True)
            dux = dux + cwv_[j:j + 1, :] * dxb[pl.ds(CW - 1 - j, tc), :]
        dux_ref[...] = jnp.where(rows_all >= PAD, dux, 0.0).astype(BF16)
        carry[0:1, :] = g_in
        carry[1:2, :] = ab[0:1, :]
        carry[8:16, :] = dxb[0:8, :]

    vec_out = pl.BlockSpec((1, RB), lambda c, k: (0, c))
    return pl.pallas_call(
        body, name="rnn_bwd", grid=(NBLK, nt),
        in_specs=[blk(0), halo(0), blk(NBLK), blk(0), halo(0), blk(0), cwv, vec, mat, vec, mat, vec, vec, mat, mat],
        out_specs=[blk(0), blk(0), cwv, vec_out, mat, vec_out, mat, vec_out, vec_out],
        out_shape=[jax.ShapeDtypeStruct((T, DR), BF16), jax.ShapeDtypeStruct((T, DR), BF16),
                   jax.ShapeDtypeStruct((CW, DR), F32), jax.ShapeDtypeStruct((1, DR), F32),
                   jax.ShapeDtypeStruct((NBLK, RB, RB), F32), jax.ShapeDtypeStruct((1, DR), F32),
                   jax.ShapeDtypeStruct((NBLK, RB, RB), F32), jax.ShapeDtypeStruct((1, DR), F32),
                   jax.ShapeDtypeStruct((1, DR), F32)],
        scratch_shapes=[pltpu.VMEM((tc + 8, RB), F32), pltpu.VMEM((tc + 8, RB), F32),
                        pltpu.VMEM((tc + 8, RB), F32), pltpu.VMEM((tc + 8, RB), F32),
                        pltpu.VMEM((tc, RB), F32), pltpu.VMEM((tc, RB), F32), pltpu.VMEM((tc, RB), F32),
                        pltpu.VMEM((tc, RB), F32), pltpu.VMEM((tc, RB), BF16), pltpu.VMEM((tc, RB), BF16),
                        pltpu.VMEM((16, RB), F32)],
        compiler_params=_cparams(("parallel", "arbitrary")),
    )(uxg, uxg, uxg, hs, hs, dy, cw, cb, wa, ba, wi, bi, lam, wat, wit)


def _attn_prep(q_all, kv_all, ukr, tab, *, tm=320):
    T = q_all.shape[0]

    def body(q_ref, kv_ref, kr_ref, tab_ref, qo_ref, ko_ref, vo_ref):
        tab_v = tab_ref[...]
        lane = lax.broadcasted_iota(jnp.int32, (tm, LANES), 1)
        t1 = kr_ref[...] * tab_v
        kro = jnp.where(lane < ROPE, t1 + pltpu.roll(t1, ROPE, 1), 0.0).astype(BF16)
        for h in range(NH):
            c0 = h * QW
            qo_ref[h, :, 0:NOPE] = (q_ref[:, c0:c0 + NOPE] * SCALE).astype(BF16)
            t2 = q_ref[:, c0 + NOPE:c0 + QW] * tab_v
            qo_ref[h, :, NOPE:QW] = ((t2 + pltpu.roll(t2, ROPE, 1)) * SCALE).astype(BF16)
            ko_ref[h, :, 0:NOPE] = kv_ref[:, c0:c0 + NOPE].astype(BF16)
            ko_ref[h, :, NOPE:QW] = kro
            vo_ref[h, :, :] = kv_ref[:, c0 + NOPE:c0 + QW].astype(BF16)

    return pl.pallas_call(
        body, name="attn_prep", grid=(T // tm,),
        in_specs=[_rows(tm, NH * QW), _rows(tm, NH * QW), _rows(tm, LANES), _rows(tm, LANES)],
        out_specs=[pl.BlockSpec((NH, tm, QW), lambda i: (0, i, 0)), pl.BlockSpec((NH, tm, QW), lambda i: (0, i, 0)),
                   pl.BlockSpec((NH, tm, VD), lambda i: (0, i, 0))],
        out_shape=[jax.ShapeDtypeStruct((NH, T, QW), BF16), jax.ShapeDtypeStruct((NH, T, QW), BF16),
                   jax.ShapeDtypeStruct((NH, T, VD), BF16)],
        compiler_params=_cparams(("parallel",)),
    )(q_all, kv_all, ukr, tab)


def _attn_prep_bwd(dq, dk, dv, tab, *, tm=320):
    T = dq.shape[1]

    def body(dq_ref, dk_ref, dv_ref, tab_ref, dqa_ref, dkva_ref, dkr_ref):
        tab_v = tab_ref[...]
        lane = lax.broadcasted_iota(jnp.int32, (tm, LANES), 1)
        dkro = jnp.zeros((tm, LANES), F32)
        for h in range(NH):
            c0 = h * QW
            dqa_ref[:, c0:c0 + NOPE] = (dq_ref[h, :, 0:NOPE] * SCALE).astype(BF16)
            d2 = dq_ref[h, :, NOPE:QW]
            dqa_ref[:, c0 + NOPE:c0 + QW] = ((d2 + pltpu.roll(d2, ROPE, 1)) * tab_v * SCALE).astype(BF16)
            dkva_ref[:, c0:c0 + NOPE] = dk_ref[h, :, 0:NOPE].astype(BF16)
            dkva_ref[:, c0 + NOPE:c0 + QW] = dv_ref[h, :, :].astype(BF16)
            dkro = dkro + dk_ref[h, :, NOPE:QW]
        dkro = jnp.where(lane < ROPE, dkro, 0.0)
        dkr_ref[...] = ((dkro + pltpu.roll(dkro, ROPE, 1)) * tab_v).astype(BF16)

    return pl.pallas_call(
        body, name="attn_prep_bwd", grid=(T // tm,),
        in_specs=[pl.BlockSpec((NH, tm, QW), lambda i: (0, i, 0)), pl.BlockSpec((NH, tm, QW), lambda i: (0, i, 0)),
                  pl.BlockSpec((NH, tm, VD), lambda i: (0, i, 0)), _rows(tm, LANES)],
        out_specs=[_rows(tm, NH * QW), _rows(tm, NH * QW), _rows(tm, LANES)],
        out_shape=[jax.ShapeDtypeStruct((T, NH * QW), BF16), jax.ShapeDtypeStruct((T, NH * QW), BF16),
                   jax.ShapeDtypeStruct((T, LANES), BF16)],
        compiler_params=_cparams(("parallel",)),
    )(dq, dk, dv, tab)


def _visible(q0, k0, nq, nk):
    rows = q0 + lax.broadcasted_iota(jnp.int32, (nq, nk), 0)
    cols = k0 + lax.broadcasted_iota(jnp.int32, (nq, nk), 1)
    return ((cols >> 6) <= (rows >> 6)) & (cols >= PAD)


def _visible_t(q0, k0, nq, nk):
    cols = k0 + lax.broadcasted_iota(jnp.int32, (nk, nq), 0)
    rows = q0 + lax.broadcasted_iota(jnp.int32, (nk, nq), 1)
    return ((cols >> 6) <= (rows >> 6)) & (cols >= PAD)


_NT = (((1,), (1,)), ((), ()))
ATTN_BLOCK = 1664


def _attn_block(T):
    return ATTN_BLOCK if T % ATTN_BLOCK == 0 else 640


def _round_up(n, m):
    return -(-n // m) * m


def _flash_fwd(q, k, v, *, gather=(), bq=None):
    T = q.shape[1]
    bq = bq or _attn_block(T)
    nq = T // bq
    rs = bq // CHAINS
    n = len(gather)

    def body(*refs):
        q_ref, k_ref, v_ref = refs[:3]
        g_src = refs[3:3 + n]
        o_ref, lse_ref = refs[3 + n:5 + n]
        g_out = refs[5 + n:5 + 2 * n]
        scr = refs[5 + 2 * n:]
        m_s, l_s, acc_s = scr[:CHAINS], scr[CHAINS:2 * CHAINS], scr[2 * CHAINS:3 * CHAINS]
        g_scr = scr[3 * CHAINS:]
        h = pl.program_id(0)
        i = pl.program_id(1)
        if n:
            @pl.when((h == 0) & (i == 0))
            def _():
                _gather_start(_gather_descs(g_src, g_out, g_scr))

        for r in range(CHAINS):
            m_s[r][...] = jnp.full_like(m_s[r], NEG)
            l_s[r][...] = jnp.zeros_like(l_s[r])
            acc_s[r][...] = jnp.zeros_like(acc_s[r])

        def step(j, masked, diag):
            off = pl.multiple_of(j * bq, bq)
            for r in range(CHAINS):
                rows = pl.ds(r * rs, rs)
                kw = min(bq, _round_up((r + 1) * rs, LANES)) if diag else bq
                kv_ = k_ref[pl.ds(off, kw), :]
                vv = v_ref[pl.ds(off, kw), :]
                s = lax.dot_general(q_ref[rows, :], kv_, _NT, preferred_element_type=F32)
                if masked:
                    s = jnp.where(_visible(i * bq + r * rs, j * bq, rs, kw), s, NEG)
                m_prev = m_s[r][...]
                m_new = jnp.maximum(m_prev, jnp.max(s, axis=-1, keepdims=True))
                p = jnp.exp(s - m_new)
                alpha = jnp.exp(m_prev - m_new)
                l_s[r][...] = alpha * l_s[r][...] + jnp.sum(p, axis=-1, keepdims=True)
                acc_s[r][...] = alpha * acc_s[r][...] + jnp.dot(p.astype(BF16), vv, preferred_element_type=F32)
                m_s[r][...] = m_new

        @pl.when(i == 0)
        def _():
            step(0, True, True)

        @pl.when(i > 0)
        def _():
            step(0, True, False)

            def loop(j, c):
                step(j, False, False)
                return c

            lax.fori_loop(1, i, loop, 0)
            step(i, True, True)

        for r in range(CHAINS):
            rows = pl.ds(r * rs, rs)
            o_ref[rows, :] = (acc_s[r][...] / l_s[r][...]).astype(BF16)
            lse_ref[rows, :] = m_s[r][...] + jnp.log(l_s[r][...])

        if n:
            @pl.when((h == NH - 1) & (i == nq - 1))
            def _():
                _gather_wait(_gather_descs(g_src, g_out, g_scr, with_loads=False))

    return pl.pallas_call(
        body, name="flash_fwd", grid=(NH, nq),
        in_specs=[pl.BlockSpec((None, bq, QW), lambda h, i: (h, i, 0)),
                  pl.BlockSpec((None, T, QW), lambda h, i: (h, 0, 0)),
                  pl.BlockSpec((None, T, VD), lambda h, i: (h, 0, 0))] + [HBM] * n,
        out_specs=[pl.BlockSpec((bq, VD), lambda h, i: (i, h)),
                   pl.BlockSpec((None, bq, 1), lambda h, i: (h, i, 0))] + [HBM] * n,
        out_shape=[jax.ShapeDtypeStruct((T, NH * VD), BF16), jax.ShapeDtypeStruct((NH, T, 1), F32)]
        + [jax.ShapeDtypeStruct((4,) + g.shape, g.dtype) for g in gather],
        scratch_shapes=[pltpu.VMEM((rs, 1), F32)] * (2 * CHAINS) + [pltpu.VMEM((rs, VD), F32)] * CHAINS
        + (_gather_scratch(gather) if n else []),
        compiler_params=_cparams(("arbitrary", "arbitrary")),
    )(q, k, v, *gather)


def _attn_delta(o, do, *, tm=640):
    T = o.shape[0]

    def body(o_ref, do_ref, d_ref):
        prod = o_ref[...].astype(F32) * do_ref[...].astype(F32)
        for h in range(NH):
            d_ref[h, :, :] = jnp.sum(prod[:, h * VD:(h + 1) * VD], axis=-1, keepdims=True)

    return pl.pallas_call(
        body, name="attn_delta", grid=(T // tm,),
        in_specs=[_rows(tm, NH * VD), _rows(tm, NH * VD)],
        out_specs=pl.BlockSpec((NH, tm, 1), lambda i: (0, i, 0)),
        out_shape=jax.ShapeDtypeStruct((NH, T, 1), F32),
        compiler_params=_cparams(("parallel",)),
    )(o, do)


_TN = (((0,), (0,)), ((), ()))


def _flash_bwd(q, k, v, do, lse_row, delta_row, *, scatter=(), bq=None):
    T = q.shape[1]
    bq = bq or _attn_block(T)
    nq = T // bq
    rs = bq // CHAINS
    n = len(scatter)

    def body(*refs):
        q_ref, k_ref, v_ref, do_ref, lse_ref, dl_ref = refs[:6]
        s_src = refs[6:6 + n]
        dq_ref, dk_ref, dv_ref = refs[6 + n:9 + n]
        s_out = refs[9 + n:9 + 2 * n]
        s_scr = refs[9 + 2 * n:]
        h = pl.program_id(0)
        j = pl.program_id(1)
        if n:
            @pl.when((h == 0) & (j == 0))
            def _():
                for cp in _scatter_descs(s_src, s_out, s_scr):
                    cp.start()

        @pl.when(j == 0)
        def _():
            dq_ref[...] = jnp.zeros_like(dq_ref)

        dk_ref[...] = jnp.zeros_like(dk_ref)
        dv_ref[...] = jnp.zeros_like(dv_ref)

        def step(i, masked, diag):
            for r in range(CHAINS):
                rows = pl.ds(r * rs, rs)
                q0 = (r * rs) // LANES * LANES if diag else 0
                qn = bq - q0
                off = pl.multiple_of(i * bq + q0, LANES)
                qv = q_ref[pl.ds(off, qn), :]
                dov = do_ref[pl.ds(off, qn), :]
                lse_v = lse_ref[:, pl.ds(off, qn)]
                dl_v = dl_ref[:, pl.ds(off, qn)]
                st = lax.dot_general(k_ref[rows, :], qv, _NT, preferred_element_type=F32)
                if masked:
                    st = jnp.where(_visible_t(i * bq + q0, j * bq + r * rs, qn, rs), st, NEG)
                pt = jnp.exp(st - lse_v)
                dv_ref[rows, :] += jnp.dot(pt.astype(BF16), dov, preferred_element_type=F32)
                dpt = lax.dot_general(v_ref[rows, :], dov, _NT, preferred_element_type=F32)
                dst = (pt * (dpt - dl_v)).astype(BF16)
                dk_ref[rows, :] += jnp.dot(dst, qv, preferred_element_type=F32)
                dq_ref[pl.ds(off, qn), :] += lax.dot_general(dst, k_ref[rows, :], _TN,
                                                             preferred_element_type=F32)

        step(j, True, True)

        @pl.when(j == 0)
        def _():
            def loop(i, c):
                step(i, True, False)
                return c
            lax.fori_loop(1, nq, loop, 0)

        @pl.when(j > 0)
        def _():
            def loop(i, c):
                step(i, False, False)
                return c
            lax.fori_loop(j + 1, nq, loop, 0)

        if n:
            @pl.when((h == NH - 1) & (j == nq - 1))
            def _():
                for cp in _scatter_descs(s_src, s_out, s_scr):
                    cp.wait()

    return pl.pallas_call(
        body, name="flash_bwd", grid=(NH, nq),
        in_specs=[pl.BlockSpec((None, T, QW), lambda h, j: (h, 0, 0)),
                  pl.BlockSpec((None, bq, QW), lambda h, j: (h, j, 0)),
                  pl.BlockSpec((None, bq, VD), lambda h, j: (h, j, 0)),
                  pl.BlockSpec((T, VD), lambda h, j: (0, h)),
                  pl.BlockSpec((None, 1, T), lambda h, j: (h, 0, 0)),
                  pl.BlockSpec((None, 1, T), lambda h, j: (h, 0, 0))] + [HBM] * n,
        out_specs=[pl.BlockSpec((None, T, QW), lambda h, j: (h, 0, 0)),
                   pl.BlockSpec((None, bq, QW), lambda h, j: (h, j, 0)),
                   pl.BlockSpec((None, bq, VD), lambda h, j: (h, j, 0))] + [HBM] * n,
        out_shape=[jax.ShapeDtypeStruct((NH, T, QW), F32), jax.ShapeDtypeStruct((NH, T, QW), F32),
                   jax.ShapeDtypeStruct((NH, T, VD), F32)]
        + [jax.ShapeDtypeStruct((3,) + s.shape[1:], s.dtype) for s in scatter],
        scratch_shapes=[_dma_sems(3 * n), _dma_sems(3 * n)] if n else [],
        compiler_params=_cparams(("arbitrary", "arbitrary")),
    )(q, k, v, do, lse_row, delta_row, *scatter)


def _rope_table(T):
    pos = (jnp.arange(T, dtype=jnp.int32) - PAD).astype(F32)
    inv_freq = ROPE_THETA ** (-jnp.arange(0, ROPE, 2, dtype=F32) / ROPE)
    ang = pos[:, None] * inv_freq[None, :]
    cos, sin = jnp.cos(ang), jnp.sin(ang)
    return jnp.concatenate([cos, cos, -sin, sin], axis=1)


def _swap_halves(w):
    return jnp.concatenate([w[..., ROPE // 2:], w[..., :ROPE // 2]], axis=-1)


O_UX, O_UG, O_UQ, O_UKV, O_UKR, O_UM = 0, DR, 2 * DR, 2 * DR + QR, 2 * DR + QR + KVR, 2 * DR + QR + KVR + ROPE


def _prep_weights(w):
    b = lambda a: a.astype(BF16)
    w_in = w["w_in"]
    kr = w_in[:, O_UKR:O_UM]
    p = {
        "w_xg": b(w_in[:, :O_UQ]),
        "w_q": b(w_in[:, O_UQ:O_UKV]),
        "w_kv": b(w_in[:, O_UKV:O_UKR]),
        "w_kr": b(jnp.concatenate([kr, _swap_halves(kr)], axis=1)),
        "w_m": b(w_in[:, O_UM:]),
    }
    wq = w["w_uq"].reshape(QR, NH, NOPE + ROPE)
    p["w_uq"] = b(jnp.concatenate([wq, _swap_halves(wq[..., NOPE:])], axis=-1).reshape(QR, NH * QW))
    p["w_ukv"] = b(w["w_ukv"])
    for n in ("w_xg", "w_q", "w_kv", "w_kr", "w_m", "w_uq", "w_ukv"):
        p[n + "_t"] = p[n].T
    p["wa"] = b(w["w_rec_a"])
    p["wi"] = b(w["w_rec_i"])
    p["wa_t"] = jnp.swapaxes(p["wa"], 1, 2)
    p["wi_t"] = jnp.swapaxes(p["wi"], 1, 2)
    return p


def _prep_late_weights(w):
    b = lambda a: a.astype(BF16)
    p = {"w_br": b(w["w_branch"][:DR]), "w_ba": b(w["w_branch"][DR:]), "w_out": b(w["w_out"]),
         "w_fi": b(w["w_ffn_in"]), "w_fo": b(w["w_ffn_out"])}
    for n in tuple(p):
        p[n + "_t"] = p[n].T
    return p


LATE = ("w_branch", "w_out", "w_ffn_in", "w_ffn_out")


def _local_step(x, tgt, w, late=None, reduce_first=None):
    S = x.shape[0]
    T = FRONT + S
    p = _prep_weights(w)
    tab = _rope_table(T)
    h0 = jnp.concatenate([jnp.zeros((PAD, D), F32), w["meta_tokens"], x], axis=0)
    row = lambda v: v.reshape(1, -1)

    z = _rmsnorm_fwd(h0, row(w["norm_mix_g"]), name="norm_mix")
    uxg = _mm(z, p["w_xg"], name="mm_uxg")
    uq = _mm(z, p["w_q"], name="mm_uq")
    ukv = _mm(z, p["w_kv"], name="mm_ukv")
    ukr = _mm(z, p["w_kr"], name="mm_ukr")
    um = _mm(z, p["w_m"], name="mm_um")
    rnn_w = (w["conv_w"], row(w["conv_b"]), p["wa"], row(w["b_rec_a"]), p["wi"], row(w["b_rec_i"]),
             row(w["lru_lambda"]))
    hs, y_rnn = _rnn_fwd(uxg, *rnn_w)
    qn = _rmsnorm_fwd(uq, row(w["q_norm_g"]), name="norm_q")
    kvn = _rmsnorm_fwd(ukv, row(w["kv_norm_g"]), name="norm_kv")
    q_all = _mm(qn, p["w_uq"], name="mm_q")
    kv_all = _mm(kvn, p["w_ukv"], name="mm_kv")
    qh, kh, vh = _attn_prep(q_all, kv_all, ukr, tab)
    y_att, lse, *stacks = _flash_fwd(qh, kh, vh, gather=late[0] if late else ())
    if late:
        w = {**w, **late[1](stacks)}
    p.update(_prep_late_weights(w))
    p_rnn = _mm(y_rnn, p["w_br"], name="mm_prnn")
    p_att = _mm(y_att, p["w_ba"], name="mm_patt")
    bg = row(w["b_gate"])
    mixed = _gate_mix_fwd(um, bg, p_rnn, p_att)
    h1 = _mm(mixed, p["w_out"], name="mm_out", res=h0)
    zf = _rmsnorm_fwd(h1, row(w["norm_ffn_g"]), name="norm_ffn")
    ff = _mm(zf, p["w_fi"], name="mm_ffn_in")
    act = _swiglu_fwd(ff)
    h2 = _mm(act, p["w_fo"], name="mm_ffn_out", res=h1)

    g = {}
    dh2, dh2b, dg_fin, lsum = _loss_head(h2, tgt, row(w["final_norm_g"]))
    loss = 0.5 * jnp.sum(lsum) / D
    g["final_norm_g"] = dg_fin.reshape(-1)
    dact = _mm(dh2b, p["w_fo_t"], name="mm_dact")
    g["w_ffn_out"] = _mm_tn(act, dh2b, name="mm_dw_ffn_out")
    dff = _swiglu_bwd(ff, dact)
    dzf = _mm(dff, p["w_fi_t"], name="mm_dzf")
    g["w_ffn_in"] = _mm_tn(zf, dff, name="mm_dw_ffn_in")
    dh1, dh1b, dg = _rmsnorm_bwd(h1, row(w["norm_ffn_g"]), dzf, dh2, name="norm_ffn_bwd")
    g["norm_ffn_g"] = dg
    dmixed = _mm(dh1b, p["w_out_t"], name="mm_dmixed")
    g["w_out"] = _mm_tn(mixed, dh1b, name="mm_dw_out")
    dp_rnn, dp_att, dum, dbg = _gate_mix_bwd(um, bg, p_rnn, p_att, dmixed)
    g["b_gate"] = dbg.reshape(2, D)
    dy_rnn = _mm(dp_rnn, p["w_br_t"], name="mm_dy_rnn")
    dy_att = _mm(dp_att, p["w_ba_t"], name="mm_dy_att", out_dtype=BF16)
    g["w_branch"] = jnp.concatenate([_mm_tn(y_rnn, dp_rnn, name="mm_dw_br"),
                                     _mm_tn(y_att, dp_att, name="mm_dw_ba")], axis=0)
    delta = _attn_delta(y_att, dy_att)
    first = reduce_first({n: g[n] for n in LATE}) if reduce_first else ()
    dq, dk, dv, *received = _flash_bwd(qh, kh, vh, dy_att, lse.reshape(NH, 1, T), delta.reshape(NH, 1, T),
                                       scatter=first)
    dq_all, dkv_all, dukr = _attn_prep_bwd(dq, dk, dv, tab)
    dqn = _mm(dq_all, p["w_uq_t"], name="mm_dqn")
    dkvn = _mm(dkv_all, p["w_ukv_t"], name="mm_dkvn")
    dwq = _mm_tn(qn, dq_all, name="mm_dw_uq").reshape(QR, NH, QW)
    dwq_rope = dwq[..., NOPE:NOPE + ROPE] + _swap_halves(dwq[..., NOPE + ROPE:])
    g["w_uq"] = jnp.concatenate([dwq[..., :NOPE], dwq_rope], axis=-1).reshape(QR, NH * (NOPE + ROPE))
    g["w_ukv"] = _mm_tn(kvn, dkv_all, name="mm_dw_ukv")
    duq, dg = _rmsnorm_bwd(uq, row(w["q_norm_g"]), dqn, None, name="norm_q_bwd", want_f32=False)
    g["q_norm_g"] = dg
    dukv, dg = _rmsnorm_bwd(ukv, row(w["kv_norm_g"]), dkvn, None, name="norm_kv_bwd", want_f32=False)
    g["kv_norm_g"] = dg
    (dux, dug, g["conv_w"], g["conv_b"], g["w_rec_a"], g["b_rec_a"], g["w_rec_i"], g["b_rec_i"],
     g["lru_lambda"]) = _rnn_bwd(uxg, hs, dy_rnn, *rnn_w, p["wa_t"], p["wi_t"])
    dz = _mm(dux, p["w_xg_t"][:DR], name="mm_dz_x")
    dz = _mm(dug, p["w_xg_t"][DR:], name="mm_dz_g", res=dz)
    dz = _mm(duq, p["w_q_t"], name="mm_dz_q", res=dz)
    dz = _mm(dukv, p["w_kv_t"], name="mm_dz_kv", res=dz)
    dz = _mm(dukr, p["w_kr_t"], name="mm_dz_kr", res=dz)
    dz = _mm(dum, p["w_m_t"], name="mm_dz_m", res=dz)
    dwkr = _mm_tn(z, dukr, name="mm_dw_kr")
    g["w_in"] = jnp.concatenate([
        _mm_tn(z, dux, name="mm_dw_x"), _mm_tn(z, dug, name="mm_dw_g"),
        _mm_tn(z, duq, name="mm_dw_q"), _mm_tn(z, dukv, name="mm_dw_kv"),
        dwkr[:, :ROPE] + _swap_halves(dwkr[:, ROPE:]),
        _mm_tn(z, dum, name="mm_dw_m")], axis=1)
    dh0, dg = _rmsnorm_bwd(h0, row(w["norm_mix_g"]), dz, dh1, name="norm_mix_bwd", want_bf16=False)
    g["norm_mix_g"] = dg
    g["meta_tokens"] = dh0[PAD:FRONT]
    return loss, dh0[FRONT:], g, (first, received)


HBM = pl.BlockSpec(memory_space=pltpu.HBM)
CHIP_FLIPS = ((1, 0), (0, 1), (1, 1))


def _place():
    return lax.axis_index("x"), lax.axis_index("y"), lax.axis_index("c")


def _flip(v, f):
    return 1 - v if f else v


def _dma_sems(n):
    return pltpu.SemaphoreType.DMA((n,))


def _gather_scratch(srcs):
    n = len(srcs)
    return [pltpu.VMEM(s.shape, s.dtype) for s in srcs] + [_dma_sems(3 * n), _dma_sems(3 * n), _dma_sems(n),
                                                            _dma_sems(n)]


def _gather_descs(src_refs, out_refs, scr, with_loads=True):
    n = len(src_refs)
    stage = scr[:n]
    send_sems, recv_sems, in_sems, local_sems = scr[n:]
    x, y, c = _place()
    me = 2 * x + y
    loads, sends, local = [], [], []
    for a in range(n):
        if with_loads:
            loads.append(pltpu.make_async_copy(src_refs[a], stage[a], in_sems.at[a]))
        for k, (fx, fy) in enumerate(CHIP_FLIPS):
            sends.append(pltpu.make_async_remote_copy(
                src_ref=stage[a], dst_ref=out_refs[a].at[me], send_sem=send_sems.at[3 * a + k],
                recv_sem=recv_sems.at[3 * a + k], device_id=(_flip(x, fx), _flip(y, fy), c),
                device_id_type=MESH))
        local.append(pltpu.make_async_copy(stage[a], out_refs[a].at[me], local_sems.at[a]))
    return loads, sends, local


def _gather_start(descs):
    loads, sends, local = descs
    for cp in loads:
        cp.start()
    for a, cp in enumerate(loads):
        cp.wait()
        for s in sends[3 * a:3 * a + 3]:
            s.start()
        local[a].start()


def _gather_wait(descs):
    _, sends, local = descs
    for cp in sends + local:
        cp.wait()


def _allgather_chips(srcs, *, name):
    n = len(srcs)

    def body(*refs):
        descs = _gather_descs(refs[:n], refs[n:2 * n], refs[2 * n:])
        _gather_start(descs)
        _gather_wait(descs)

    return pl.pallas_call(
        body, name=name, in_specs=[HBM] * n, out_specs=[HBM] * n,
        out_shape=[jax.ShapeDtypeStruct((4,) + s.shape, s.dtype) for s in srcs],
        scratch_shapes=_gather_scratch(srcs),
        compiler_params=pltpu.CompilerParams(vmem_limit_bytes=VMEM_LIMIT),
    )(*srcs)


def _scatter_descs(src_refs, out_refs, scr):
    send_sems, recv_sems = scr
    x, y, c = _place()
    copies = []
    for a in range(len(src_refs)):
        for k, (fx, fy) in enumerate(CHIP_FLIPS):
            px, py = _flip(x, fx), _flip(y, fy)
            copies.append(pltpu.make_async_remote_copy(
                src_ref=src_refs[a].at[2 * px + py], dst_ref=out_refs[a].at[k],
                send_sem=send_sems.at[3 * a + k], recv_sem=recv_sems.at[3 * a + k],
                device_id=(px, py, c), device_id_type=MESH))
    return copies


def _scatter_chips(srcs, *, name):
    n = len(srcs)

    def body(*refs):
        copies = _scatter_descs(refs[:n], refs[n:2 * n], refs[2 * n:])
        for cp in copies:
            cp.start()
        for cp in copies:
            cp.wait()

    return pl.pallas_call(
        body, name=name, in_specs=[HBM] * n, out_specs=[HBM] * n,
        out_shape=[jax.ShapeDtypeStruct((3,) + s.shape[1:], s.dtype) for s in srcs],
        scratch_shapes=[_dma_sems(3 * n), _dma_sems(3 * n)],
    )(*srcs)


def _sibling_take(srcs, *, name):
    n = len(srcs)

    def body(*refs):
        src_refs, out_refs = refs[:n], refs[n:2 * n]
        send_sems, recv_sems = refs[2 * n:]
        x, y, c = _place()
        copies = []
        for a in range(n):
            h = srcs[a].shape[1] // 2
            theirs = pl.ds(pl.multiple_of((1 - c) * h, 8), h)
            cp = pltpu.make_async_remote_copy(
                src_ref=src_refs[a].at[:, theirs, :], dst_ref=out_refs[a], send_sem=send_sems.at[a],
                recv_sem=recv_sems.at[a], device_id=(x, y, 1 - c), device_id_type=MESH)
            cp.start()
            copies.append(cp)
        for cp in copies:
            cp.wait()

    return pl.pallas_call(
        body, name=name, in_specs=[HBM] * n, out_specs=[HBM] * n,
        out_shape=[jax.ShapeDtypeStruct((4, s.shape[1] // 2, s.shape[2]), s.dtype) for s in srcs],
        scratch_shapes=[_dma_sems(n), _dma_sems(n)],
    )(*srcs)


def _sibling_swap(srcs, *, name):
    n = len(srcs)

    def body(*refs):
        src_refs, out_refs = refs[:n], refs[n:2 * n]
        send_sems, recv_sems = refs[2 * n:]
        x, y, c = _place()
        copies = []
        for a in range(n):
            cp = pltpu.make_async_remote_copy(
                src_ref=src_refs[a], dst_ref=out_refs[a], send_sem=send_sems.at[a],
                recv_sem=recv_sems.at[a], device_id=(x, y, 1 - c), device_id_type=MESH)
            cp.start()
            copies.append(cp)
        for cp in copies:
            cp.wait()

    return pl.pallas_call(
        body, name=name, in_specs=[HBM] * n, out_specs=[HBM] * n,
        out_shape=[jax.ShapeDtypeStruct(s.shape, s.dtype) for s in srcs],
        scratch_shapes=[_dma_sems(n), _dma_sems(n)],
    )(*srcs)


def _row_tile(rows, cols, n_arrays, step=16):
    budget = 24 * 1024 * 1024 // (2 * 4 * n_arrays * cols)
    best = step
    for t in range(step, rows + 1, step):
        if rows % t == 0 and t <= budget:
            best = t
    assert rows % best == 0, (rows, cols)
    return best


def _add_halves(mine, theirs, wire, *, name):
    _, h, c = mine.shape
    tm = _row_tile(h, c, 3)
    spec = pl.BlockSpec((None, tm, c), lambda s, i: (s, i, 0))

    def body(a_ref, b_ref, o_ref):
        o_ref[...] = (a_ref[...] + b_ref[...]).astype(wire)

    return pl.pallas_call(
        body, name=name, grid=(4, h // tm), in_specs=[spec, spec], out_specs=spec,
        out_shape=jax.ShapeDtypeStruct(mine.shape, wire), compiler_params=_cparams(("parallel", "parallel")),
    )(mine, theirs)


def _sum4(own, recv, *, name):
    h, c = own.shape
    tm = _row_tile(h, c, 5)

    def body(o_ref, r_ref, out_ref):
        f = lambda k: r_ref[k].astype(F32)
        out_ref[...] = ((o_ref[...].astype(F32) + f(0)) + f(1)) + f(2)

    return pl.pallas_call(
        body, name=name, grid=(h // tm,),
        in_specs=[_rows(tm, c), pl.BlockSpec((3, tm, c), lambda i: (0, i, 0))],
        out_specs=_rows(tm, c), out_shape=jax.ShapeDtypeStruct((h, c), F32),
        compiler_params=_cparams(("parallel",)),
    )(own, recv)


def _adamw(g, w, m, v, *, name):
    r, c = g.shape
    tm = _row_tile(r, c, 7, step=8)
    c1 = 1.0 / (1.0 - ADAM_B1 ** ADAM_STEP)
    c2 = 1.0 / (1.0 - ADAM_B2 ** ADAM_STEP)

    def body(g_ref, w_ref, m_ref, v_ref, d_ref, nm_ref, nv_ref):
        gv = g_ref[...]
        nm = ADAM_B1 * m_ref[...] + (1.0 - ADAM_B1) * gv
        nv = ADAM_B2 * v_ref[...] + (1.0 - ADAM_B2) * (gv * gv)
        nm_ref[...] = nm
        nv_ref[...] = nv
        d_ref[...] = -ADAM_LR * ((nm * c1) / (jnp.sqrt(nv * c2) + ADAM_EPS) + ADAM_WD * w_ref[...])

    spec = _rows(tm, c)
    shape = jax.ShapeDtypeStruct((r, c), F32)
    return pl.pallas_call(
        body, name=name, grid=(r // tm,), in_specs=[spec] * 4, out_specs=[spec] * 3,
        out_shape=[shape] * 3, compiler_params=_cparams(("parallel",)),
    )(g, w, m, v)


BIG = (("w_in", (D, 1328), 1), ("w_uq", (QR, 384), 1), ("w_ukv", (KVR, 512), 1), ("w_branch", (576, D), 0),
       ("w_out", (256, D), 0), ("w_ffn_in", (D, 1408), 1), ("w_ffn_out", (704, D), 0))
SMALL = (("meta_tokens", (NMETA, 256), 1), ("b_gate", (2, 256), 1), ("conv_w", (CW, 320), 1))
REPL = (("norm_mix_g", (D,)), ("conv_b", (DR,)), ("w_rec_a", (NBLK, RB, RB)), ("b_rec_a", (DR,)),
        ("w_rec_i", (NBLK, RB, RB)), ("b_rec_i", (DR,)), ("lru_lambda", (DR,)), ("q_norm_g", (QR,)),
        ("kv_norm_g", (KVR,)), ("norm_ffn_g", (D,)), ("final_norm_g", (D,)))
WEIGHTS = ("meta_tokens", "norm_mix_g", "w_in", "b_gate", "conv_w", "conv_b", "w_rec_a", "b_rec_a", "w_rec_i",
           "b_rec_i", "lru_lambda", "q_norm_g", "w_uq", "kv_norm_g", "w_ukv", "w_branch", "w_out", "norm_ffn_g",
           "w_ffn_in", "w_ffn_out", "final_norm_g")
W = 1024
SMALL_N = sum(math.prod(s) for _, s, _ in SMALL)
SMALL_ROWS = 8
REPL_N = sum(math.prod(s) for _, s in REPL)
QUART_ROWS = 88
assert SMALL_N <= SMALL_ROWS * W and REPL_N <= 4 * QUART_ROWS * W


def _flat_pad(parts, rows):
    v = jnp.concatenate([p.reshape(-1) for p in parts])
    return jnp.pad(v, (0, rows * W - v.shape[0])).reshape(rows, W)


def _shard_stack(full, shard_shape, axis):
    r, cs = shard_shape
    if axis == 0:
        return full.reshape(4, r, cs)
    return jnp.stack([full[:, s * cs:(s + 1) * cs] for s in range(4)])


def _unshard(stack, axis):
    if axis == 0:
        return stack.reshape(4 * stack.shape[1], stack.shape[2])
    return jnp.concatenate([stack[s] for s in range(4)], axis=1)


def _split(flat, table):
    out, off = {}, 0
    for name, shape, *_ in table:
        n = math.prod(shape)
        out[name] = flat[..., off:off + n].reshape(flat.shape[:-1] + tuple(shape))
        off += n
    return out


def kernel(x, meta_tokens, norm_mix_g, w_in, b_gate, conv_w, conv_b, w_rec_a, b_rec_a, w_rec_i, b_rec_i, lru_lambda, q_norm_g, w_uq, kv_norm_g, w_ukv, w_branch, w_out, norm_ffn_g, w_ffn_in, w_ffn_out, final_norm_g, loss_target, m_meta_tokens, m_norm_mix_g, m_w_in, m_b_gate, m_conv_w, m_conv_b, m_w_rec_a, m_b_rec_a, m_w_rec_i, m_b_rec_i, m_lru_lambda, m_q_norm_g, m_w_uq, m_kv_norm_g, m_w_ukv, m_w_branch, m_w_out, m_norm_ffn_g, m_w_ffn_in, m_w_ffn_out, m_final_norm_g, v_meta_tokens, v_norm_mix_g, v_w_in, v_b_gate, v_conv_w, v_conv_b, v_w_rec_a, v_b_rec_a, v_w_rec_i, v_b_rec_i, v_lru_lambda, v_q_norm_g, v_w_uq, v_kv_norm_g, v_w_ukv, v_w_branch, v_w_out, v_norm_ffn_g, v_w_ffn_in, v_w_ffn_out, v_final_norm_g):
    args = dict(locals())
    chip = 2 * lax.axis_index("x") + lax.axis_index("y")
    core = lax.axis_index("c")

    first_big = [b for b in BIG if b[0] not in LATE]
    late_big = [b for b in BIG if b[0] in LATE]
    bf16_shard = lambda n, s: args[n].reshape(s).astype(BF16)
    small = _flat_pad([args[n] for n, _, _ in SMALL], SMALL_ROWS)
    gathered = _allgather_chips([bf16_shard(n, s) for n, s, _ in first_big] + [small], name="gather_weights")
    w = {}
    for (name, _, axis), stack in zip(first_big, gathered):
        w[name] = _unshard(stack, axis)
    small_parts = _split(gathered[-1].reshape(4, SMALL_ROWS * W), SMALL)
    for name, _, axis in SMALL:
        w[name] = _unshard(small_parts[name], axis)
    for name, shape in REPL:
        w[name] = args[name].reshape(shape)
    finish_late = lambda stacks: {name: _unshard(st, axis) for (name, _, axis), st in zip(late_big, stacks)}

    def to_wire(red, tag, wires):
        theirs = _sibling_take(red, name="reduce_sibling_" + tag)
        parts = []
        for k, (a, t) in enumerate(zip(red, theirs)):
            h = a.shape[1] // 2
            mine = lax.dynamic_slice_in_dim(a, core * h, h, axis=1)
            parts.append(_add_halves(mine, t, wires[k], name=f"add_sibling_{tag}{k}"))
        return parts

    reduce_first = lambda gl: to_wire([_shard_stack(gl[n], s, a) for n, s, a in late_big], "a",
                                      [BF16] * len(late_big))
    loss, grad_x, g, (parts_a, recv_a) = _local_step(
        x[0], loss_target[0], w, late=([bf16_shard(n, s) for n, s, _ in late_big], finish_late),
        reduce_first=reduce_first)
    loss = lax.psum(loss, ("x", "y", "c"))

    red = [_shard_stack(g[n], s, a) for n, s, a in first_big]
    small_g = jnp.concatenate([_shard_stack(g[n], s, a).reshape(4, -1) for n, s, a in SMALL], axis=1)
    small_g = jnp.pad(small_g, ((0, 0), (0, SMALL_ROWS * W - SMALL_N))).reshape(4, SMALL_ROWS, W)
    repl_g = _flat_pad([g[n] for n, _ in REPL], 4 * QUART_ROWS).reshape(4, QUART_ROWS, W)
    red.append(jnp.concatenate([small_g, repl_g], axis=1))
    parts_b = to_wire(red, "b", [BF16] * len(first_big) + [F32])
    recv_b = _scatter_chips(parts_b, name="reduce_chips")
    order = [b[0] for b in late_big] + [b[0] for b in first_big] + ["misc"]
    halves = [_sum4(lax.dynamic_index_in_dim(p, chip, 0, keepdims=False), r, name="sum_chips_" + n)
              for n, p, r in zip(order, list(parts_a) + parts_b, list(recv_a) + list(recv_b))]
    others = _sibling_swap(halves, name="share_sibling")
    gred = [jnp.where(core == 0, jnp.concatenate([a, b], axis=0), jnp.concatenate([b, a], axis=0))
            for a, b in zip(halves, others)]

    results = {}
    shape_of = {name: shape for name, shape, _ in BIG}
    for name, gr in zip(order[:-1], gred):
        shape = shape_of[name]
        d, nm, nv = _adamw(gr, args[name].reshape(shape), args["m_" + name].reshape(shape),
                           args["v_" + name].reshape(shape), name="adamw_" + name)
        results[name] = (gr, d, nm, nv)
    g_repl = _allgather_chips([gred[-1][SMALL_ROWS:]], name="gather_repl")[0].reshape(4 * QUART_ROWS, W)
    g_misc = jnp.concatenate([gred[-1][:SMALL_ROWS], g_repl], axis=0)
    misc_state = lambda prefix: jnp.concatenate(
        [_flat_pad([args[prefix + n] for n, _, _ in SMALL], SMALL_ROWS),
         _flat_pad([args[prefix + n] for n, _ in REPL], 4 * QUART_ROWS)], axis=0)
    d, nm, nv = _adamw(g_misc, misc_state(""), misc_state("m_"), misc_state("v_"), name="adamw_misc")
    misc = (g_misc, d, nm, nv)

    outs = []
    for k in range(4):
        sm = _split(misc[k][:SMALL_ROWS].reshape(-1), SMALL)
        rp = _split(misc[k][SMALL_ROWS:].reshape(-1), REPL)
        for name in WEIGHTS:
            val = results[name][k] if name in results else (sm[name] if name in sm else rp[name])
            outs.append(val.reshape(args[name].shape))
    return (loss, grad_x[None], *outs)
```

```python
import functools
import math

import jax
import jax.numpy as jnp
from jax import lax
from jax.experimental import pallas as pl
from jax.experimental.pallas import tpu as pltpu

F32 = jnp.float32
BF16 = jnp.bfloat16

D = 1024
DR = 1280
NBLK = 10
RB = 128
CW = 4
NH = 8
NOPE = 128
ROPE = 64
VD = 128
QR = 384
KVR = 256
DFF = 2816
NMETA = 16
EPS = 1e-6
LRU_C = 8.0
ROPE_THETA = 10000.0
SCALE = 1.0 / math.sqrt(NOPE + ROPE)
NEG = -1e30
FRONT = 128
PAD = FRONT - NMETA
QW = 2 * NOPE
LANES = 128
SUB = 128
CHAINS = 4
VMEM_LIMIT = 52 * 1024 * 1024

ADAM_LR = 0.001
ADAM_B1 = 0.9
ADAM_B2 = 0.999
ADAM_EPS = 1e-08
ADAM_WD = 0.01
ADAM_STEP = 10

MESH = pl.DeviceIdType.MESH


def _cparams(sem):
    return pltpu.CompilerParams(dimension_semantics=sem, vmem_limit_bytes=VMEM_LIMIT)


def _sigmoid(x):
    return 1.0 / (1.0 + jnp.exp(-x))


def _gelu_parts(x):
    c = math.sqrt(2.0 / math.pi)
    inner = c * (x + 0.044715 * x * x * x)
    t = jnp.tanh(inner)
    g = 0.5 * x * (1.0 + t)
    dg = 0.5 * (1.0 + t) + 0.5 * x * (1.0 - t * t) * c * (1.0 + 3.0 * 0.044715 * x * x)
    return g, dg


def _divisors(n, step, cap):
    return [d for d in range(step, min(n, cap) + 1, step) if n % d == 0] or [n]


MM_VMEM_BUDGET = 40 * 1024 * 1024
MM_MAX_ROWS = 1664
MM_MAX_COLS = 1408


def _mm_tiles(M, K, N, a_item, out_item, has_res):
    best = None
    for tn in _divisors(N, LANES, MM_MAX_COLS):
        for tm in _divisors(M, 16, MM_MAX_ROWS):
            need = 2 * (tm * K * a_item + K * tn * 2 + tm * tn * (out_item + (4 if has_res else 0)))
            if need <= MM_VMEM_BUDGET and (best is None or tm * tn > best[0] * best[1]):
                best = (tm, tn)
    assert best is not None, (M, K, N)
    return best


def _mm(a, b, *, name, out_dtype=F32, res=None):
    M, K = a.shape
    N = b.shape[1]
    has_res = res is not None
    tm, tn = _mm_tiles(M, K, N, a.dtype.itemsize, jnp.dtype(out_dtype).itemsize, has_res)

    def body(*refs):
        if has_res:
            a_ref, b_ref, r_ref, o_ref = refs
        else:
            a_ref, b_ref, o_ref = refs
        acc = jnp.dot(a_ref[...].astype(BF16), b_ref[...].astype(BF16), preferred_element_type=F32)
        if has_res:
            acc = acc + r_ref[...].astype(F32)
        o_ref[...] = acc.astype(o_ref.dtype)

    a_bytes = M * K * a.dtype.itemsize
    b_bytes = K * N * b.dtype.itemsize
    rows_outer = a_bytes + (M // tm) * b_bytes <= b_bytes + (N // tn) * a_bytes
    if rows_outer:
        grid = (M // tm, N // tn)
        ia, ib, io = (lambda i, j: (i, 0)), (lambda i, j: (0, j)), (lambda i, j: (i, j))
    else:
        grid = (N // tn, M // tm)
        ia, ib, io = (lambda j, i: (i, 0)), (lambda j, i: (0, j)), (lambda j, i: (i, j))
    in_specs = [pl.BlockSpec((tm, K), ia), pl.BlockSpec((K, tn), ib)]
    args = [a, b]
    if has_res:
        in_specs.append(pl.BlockSpec((tm, tn), io))
        args.append(res)
    return pl.pallas_call(
        body, name=name, grid=grid, in_specs=in_specs,
        out_specs=pl.BlockSpec((tm, tn), io),
        out_shape=jax.ShapeDtypeStruct((M, N), out_dtype),
        compiler_params=_cparams(("parallel", "parallel")),
    )(*args)


def _mm_sum(pairs, *, name, res=None, out_dtype=F32):
    M = pairs[0][0].shape[0]
    N = pairs[0][1].shape[1]
    ks = [a.shape[1] for a, _ in pairs]
    has_res = res is not None
    tm, tn = _mm_tiles(M, sum(ks), N, 2, jnp.dtype(out_dtype).itemsize, has_res)
    n = len(pairs)

    def body(*refs):
        acc = None
        for k in range(n):
            d = jnp.dot(refs[2 * k][...].astype(BF16), refs[2 * k + 1][...].astype(BF16),
                        preferred_element_type=F32)
            acc = d if acc is None else acc + d
        if has_res:
            acc = acc + refs[2 * n][...].astype(F32)
        refs[-1][...] = acc.astype(refs[-1].dtype)

    in_specs, args = [], []
    for (a, b), kk in zip(pairs, ks):
        in_specs += [pl.BlockSpec((tm, kk), lambda i, j: (i, 0)), pl.BlockSpec((kk, tn), lambda i, j: (0, j))]
        args += [a, b]
    if has_res:
        in_specs.append(pl.BlockSpec((tm, tn), lambda i, j: (i, j)))
        args.append(res)
    return pl.pallas_call(
        body, name=name, grid=(M // tm, N // tn), in_specs=in_specs,
        out_specs=pl.BlockSpec((tm, tn), lambda i, j: (i, j)),
        out_shape=jax.ShapeDtypeStruct((M, N), out_dtype),
        compiler_params=_cparams(("parallel", "parallel")),
    )(*args)


def _mm_tn(a, b, *, name):
    T, K1 = a.shape
    N = b.shape[1]
    tt = _divisors(T, 16, MM_MAX_ROWS)[-1]
    tk = _divisors(K1, LANES, MM_MAX_COLS)[-1]
    tn = _divisors(N, LANES, MM_MAX_COLS)[-1]

    def body(a_ref, b_ref, o_ref):
        @pl.when(pl.program_id(2) == 0)
        def _():
            o_ref[...] = jnp.zeros_like(o_ref)

        o_ref[...] += lax.dot_general(a_ref[...].astype(BF16), b_ref[...].astype(BF16),
                                      (((0,), (0,)), ((), ())), preferred_element_type=F32)

    return pl.pallas_call(
        body, name=name, grid=(K1 // tk, N // tn, T // tt),
        in_specs=[pl.BlockSpec((tt, tk), lambda i, j, t: (t, i)),
                  pl.BlockSpec((tt, tn), lambda i, j, t: (t, j))],
        out_specs=pl.BlockSpec((tk, tn), lambda i, j, t: (i, j)),
        out_shape=jax.ShapeDtypeStruct((K1, N), F32),
        compiler_params=_cparams(("parallel", "parallel", "arbitrary")),
    )(a, b)


def _rows(tm, w, cb=0):
    return pl.BlockSpec((tm, w), lambda i: (i, cb))


def _const(shape):
    n = len(shape)
    return pl.BlockSpec(shape, lambda i: (0,) * n)


def _rmsnorm_fwd(x, g, *, name, tm=640):
    T, C = x.shape

    def body(x_ref, g_ref, o_ref):
        xv = x_ref[...]
        r = lax.rsqrt(jnp.mean(xv * xv, axis=-1, keepdims=True) + EPS)
        o_ref[...] = ((xv * r) * g_ref[...]).astype(BF16)

    return pl.pallas_call(
        body, name=name, grid=(T // tm,),
        in_specs=[_rows(tm, C), _const((1, C))],
        out_specs=_rows(tm, C),
        out_shape=jax.ShapeDtypeStruct((T, C), BF16),
        compiler_params=_cparams(("parallel",)),
    )(x, g)


def _rmsnorm_bwd(x, g, dy, res, *, name, tm=640, want_f32=True, want_bf16=True):
    T, C = x.shape
    has_res = res is not None

    def body(*refs):
        refs = list(refs)
        x_ref, g_ref, dy_ref = refs[:3]
        refs = refs[3:]
        r_ref = refs.pop(0) if has_res else None
        o32 = refs.pop(0) if want_f32 else None
        o16 = refs.pop(0) if want_bf16 else None
        dg_ref = refs.pop(0)

        @pl.when(pl.program_id(0) == 0)
        def _():
            dg_ref[...] = jnp.zeros_like(dg_ref)

        xv = x_ref[...]
        dyv = dy_ref[...].astype(F32)
        r = lax.rsqrt(jnp.mean(xv * xv, axis=-1, keepdims=True) + EPS)
        xn = xv * r
        dg_ref[...] += jnp.sum(dyv * xn, axis=0, keepdims=True)
        dxn = dyv * g_ref[...]
        dx = r * (dxn - xn * jnp.mean(dxn * xn, axis=-1, keepdims=True))
        if has_res:
            dx = dx + r_ref[...]
        if want_f32:
            o32[...] = dx
        if want_bf16:
            o16[...] = dx.astype(BF16)

    in_specs = [_rows(tm, C), _const((1, C)), _rows(tm, C)]
    args = [x, g, dy]
    if has_res:
        in_specs.append(_rows(tm, C))
        args.append(res)
    out_specs, out_shape = [], []
    if want_f32:
        out_specs.append(_rows(tm, C))
        out_shape.append(jax.ShapeDtypeStruct((T, C), F32))
    if want_bf16:
        out_specs.append(_rows(tm, C))
        out_shape.append(jax.ShapeDtypeStruct((T, C), BF16))
    out_specs.append(_const((1, C)))
    out_shape.append(jax.ShapeDtypeStruct((1, C), F32))
    return pl.pallas_call(
        body, name=name, grid=(T // tm,), in_specs=in_specs, out_specs=out_specs,
        out_shape=out_shape, compiler_params=_cparams(("arbitrary",)),
    )(*args)


def _gate_mix_fwd(um, bg, p_rnn, p_att, *, tm=320):
    T = um.shape[0]

    def body(um_ref, bg_ref, pr_ref, pa_ref, o_ref):
        g = _sigmoid(um_ref[...].astype(F32) + bg_ref[...])
        o_ref[...] = (g[:, :D] * pr_ref[...].astype(F32) + g[:, D:] * pa_ref[...].astype(F32)).astype(BF16)

    return pl.pallas_call(
        body, name="gate_mix_fwd", grid=(T // tm,),
        in_specs=[_rows(tm, 2 * D), _const((1, 2 * D)), _rows(tm, D), _rows(tm, D)],
        out_specs=_rows(tm, D),
        out_shape=jax.ShapeDtypeStruct((T, D), BF16),
        compiler_params=_cparams(("parallel",)),
    )(um, bg, p_rnn, p_att)


def _gate_mix_bwd(um, bg, p_rnn, p_att, dmixed, *, tm=320):
    T = um.shape[0]

    def body(um_ref, bg_ref, pr_ref, pa_ref, dm_ref, dpr_ref, dpa_ref, dum_ref, dbg_ref):
        @pl.when(pl.program_id(0) == 0)
        def _():
            dbg_ref[...] = jnp.zeros_like(dbg_ref)

        g = _sigmoid(um_ref[...].astype(F32) + bg_ref[...])
        g0, g1 = g[:, :D], g[:, D:]
        dm = dm_ref[...].astype(F32)
        dpr_ref[...] = (dm * g0).astype(BF16)
        dpa_ref[...] = (dm * g1).astype(BF16)
        d0 = dm * pr_ref[...].astype(F32) * g0 * (1.0 - g0)
        d1 = dm * pa_ref[...].astype(F32) * g1 * (1.0 - g1)
        dum_ref[:, :D] = d0.astype(BF16)
        dum_ref[:, D:] = d1.astype(BF16)
        dbg_ref[:, :D] += jnp.sum(d0, axis=0, keepdims=True)
        dbg_ref[:, D:] += jnp.sum(d1, axis=0, keepdims=True)

    return pl.pallas_call(
        body, name="gate_mix_bwd", grid=(T // tm,),
        in_specs=[_rows(tm, 2 * D), _const((1, 2 * D)), _rows(tm, D), _rows(tm, D), _rows(tm, D)],
        out_specs=[_rows(tm, D), _rows(tm, D), _rows(tm, 2 * D), _const((1, 2 * D))],
        out_shape=[jax.ShapeDtypeStruct((T, D), BF16), jax.ShapeDtypeStruct((T, D), BF16),
                   jax.ShapeDtypeStruct((T, 2 * D), BF16), jax.ShapeDtypeStruct((1, 2 * D), F32)],
        compiler_params=_cparams(("arbitrary",)),
    )(um, bg, p_rnn, p_att, dmixed)


def _swiglu_fwd(ff, *, tm=320):
    T = ff.shape[0]

    def body(g_ref, u_ref, o_ref):
        gv = g_ref[...].astype(F32)
        o_ref[...] = (gv * _sigmoid(gv) * u_ref[...].astype(F32)).astype(BF16)

    return pl.pallas_call(
        body, name="swiglu_fwd", grid=(T // tm,),
        in_specs=[_rows(tm, DFF, 0), _rows(tm, DFF, 1)],
        out_specs=_rows(tm, DFF),
        out_shape=jax.ShapeDtypeStruct((T, DFF), BF16),
        compiler_params=_cparams(("parallel",)),
    )(ff, ff)


def _swiglu_bwd(ff, dact, *, tm=320):
    T = ff.shape[0]

    def body(g_ref, u_ref, da_ref, o_ref):
        gv = g_ref[...].astype(F32)
        s = _sigmoid(gv)
        da = da_ref[...].astype(F32)
        o_ref[:, :DFF] = (da * u_ref[...].astype(F32) * s * (1.0 + gv * (1.0 - s))).astype(BF16)
        o_ref[:, DFF:] = (da * gv * s).astype(BF16)

    return pl.pallas_call(
        body, name="swiglu_bwd", grid=(T // tm,),
        in_specs=[_rows(tm, DFF, 0), _rows(tm, DFF, 1), _rows(tm, DFF)],
        out_specs=_rows(tm, 2 * DFF),
        out_shape=jax.ShapeDtypeStruct((T, 2 * DFF), BF16),
        compiler_params=_cparams(("parallel",)),
    )(ff, ff, dact)


def _loss_head(h2, tgt, g, *, tm=FRONT):
    T = h2.shape[0]
    front_blocks = FRONT // tm

    def body(h_ref, t_ref, g_ref, d32_ref, d16_ref, dg_ref, ls_ref):
        i = pl.program_id(0)

        @pl.when(i == 0)
        def _():
            dg_ref[...] = jnp.zeros_like(dg_ref)
            ls_ref[...] = jnp.zeros_like(ls_ref)

        xv = h_ref[...]
        r = lax.rsqrt(jnp.mean(xv * xv, axis=-1, keepdims=True) + EPS)
        xn = xv * r
        gv = g_ref[...]
        e = jnp.where(i >= front_blocks, xn * gv - t_ref[...], 0.0)
        ls_ref[...] += jnp.sum(e * e, axis=0, keepdims=True)
        dy = e * (1.0 / D)
        dg_ref[...] += jnp.sum(dy * xn, axis=0, keepdims=True)
        dxn = dy * gv
        dx = r * (dxn - xn * jnp.mean(dxn * xn, axis=-1, keepdims=True))
        d32_ref[...] = dx
        d16_ref[...] = dx.astype(BF16)

    return pl.pallas_call(
        body, name="loss_head", grid=(T // tm,),
        in_specs=[_rows(tm, D), pl.BlockSpec((tm, D), lambda i: (jnp.maximum(i - front_blocks, 0), 0)),
                  _const((1, D))],
        out_specs=[_rows(tm, D), _rows(tm, D), _const((1, D)), _const((1, D))],
        out_shape=[jax.ShapeDtypeStruct((T, D), F32), jax.ShapeDtypeStruct((T, D), BF16),
                   jax.ShapeDtypeStruct((1, D), F32), jax.ShapeDtypeStruct((1, D), F32)],
        compiler_params=_cparams(("arbitrary",)),
    )(h2, tgt, g)


def _scan_fwd(a, b, h_in):
    n = a.shape[0]
    row = lax.broadcasted_iota(jnp.int32, a.shape, 0)
    s = 1
    while s < n:
        if s % 8:
            a_sh = jnp.where(row >= s, pltpu.roll(a, s, 0), 1.0)
            b_sh = jnp.where(row >= s, pltpu.roll(b, s, 0), 0.0)
        else:
            a_sh = jnp.concatenate([jnp.ones((s, RB), F32), a[:n - s]], axis=0)
            b_sh = jnp.concatenate([jnp.zeros((s, RB), F32), b[:n - s]], axis=0)
        b = a * b_sh + b
        a = a * a_sh
        s *= 2
    return b + a * h_in


def _scan_rev(a, b, g_in):
    n = a.shape[0]
    row = lax.broadcasted_iota(jnp.int32, a.shape, 0)
    s = 1
    while s < n:
        if s % 8:
            a_sh = jnp.where(row < n - s, pltpu.roll(a, n - s, 0), 1.0)
            b_sh = jnp.where(row < n - s, pltpu.roll(b, n - s, 0), 0.0)
        else:
            a_sh = jnp.concatenate([a[s:], jnp.ones((s, RB), F32)], axis=0)
            b_sh = jnp.concatenate([b[s:], jnp.zeros((s, RB), F32)], axis=0)
        b = a * b_sh + b
        a = a * a_sh
        s *= 2
    return b + a * g_in


def _lru_gates(xc, wa, ba, wi, bi, lam):
    xcb = xc.astype(BF16)
    r = _sigmoid(jnp.dot(xcb, wa, preferred_element_type=F32) + ba)
    ig = _sigmoid(jnp.dot(xcb, wi, preferred_element_type=F32) + bi)
    log_sig = jnp.minimum(lam, 0.0) - jnp.log(1.0 + jnp.exp(-jnp.abs(lam)))
    log_a = LRU_C * r * log_sig
    a = jnp.exp(log_a)
    z = 2.0 * log_a
    poly = -z * (1.0 + z * (0.5 + z * (1.0 / 6.0 + z * (1.0 / 24.0))))
    m2 = jnp.where(z > -0.03, poly, 1.0 - jnp.exp(z))
    return r, ig, log_sig, a, jnp.sqrt(m2)


def _rnn_specs(tc, nblk_t, rev):
    def tmap(k):
        return (nblk_t - 1 - k) if rev else k

    hb = tc // 8
    blk = lambda off: pl.BlockSpec((tc, RB), lambda c, k: (tmap(k), c + off))
    halo = lambda off: pl.BlockSpec((8, RB), lambda c, k: (jnp.maximum(tmap(k) * hb - 1, 0), c + off))
    vec = pl.BlockSpec((1, RB), lambda c, k: (0, c))
    cwv = pl.BlockSpec((CW, RB), lambda c, k: (0, c))
    mat = pl.BlockSpec((None, RB, RB), lambda c, k: (c, 0, 0))
    return blk, halo, vec, cwv, mat


def _rnn_fwd(uxg, cw, cb, wa, ba, wi, bi, lam, *, tc=640):
    T = uxg.shape[0]
    nt = T // tc
    nsub = tc // SUB
    blk, halo, vec, cwv, mat = _rnn_specs(tc, nt, False)

    def body(x_ref, xh_ref, ug_ref, cw_ref, cb_ref, wa_ref, ba_ref, wi_ref, bi_ref, lam_ref,
             h_ref, y_ref, xb, hc):
        k = pl.program_id(1)

        @pl.when(k == 0)
        def _():
            hc[...] = jnp.zeros_like(hc)

        xb[0:8, :] = jnp.where(k > 0, xh_ref[...], 0.0)
        xb[8:, :] = x_ref[...]
        cwv_, cbv = cw_ref[...], cb_ref[...]
        wav, wiv = wa_ref[...], wi_ref[...]
        bav, biv, lamv = ba_ref[...], bi_ref[...], lam_ref[...]
        h_in = hc[0:1, :]
        for sc in range(nsub):
            r0 = sc * SUB
            xc = cbv + cwv_[0:1, :] * xb[pl.ds(5 + r0, SUB), :]
            for j in range(1, CW):
                xc = xc + cwv_[j:j + 1, :] * xb[pl.ds(5 + j + r0, SUB), :]
            r, ig, _, a, mm = _lru_gates(xc, wav, bav, wiv, biv, lamv)
            rows = k * tc + r0 + lax.broadcasted_iota(jnp.int32, (SUB, RB), 0)
            b = jnp.where(rows >= PAD, mm * (ig * xc), 0.0)
            h = _scan_fwd(a, b, h_in)
            h_in = h[SUB - 1:SUB, :]
            h_ref[pl.ds(r0, SUB), :] = h
            gl, _ = _gelu_parts(ug_ref[pl.ds(r0, SUB), :])
            y_ref[pl.ds(r0, SUB), :] = (h * gl).astype(BF16)
        hc[0:1, :] = h_in

    return pl.pallas_call(
        body, name="rnn_fwd", grid=(NBLK, nt),
        in_specs=[blk(0), halo(0), blk(NBLK), cwv, vec, mat, vec, mat, vec, vec],
        out_specs=[blk(0), blk(0)],
        out_shape=[jax.ShapeDtypeStruct((T, DR), F32), jax.ShapeDtypeStruct((T, DR), BF16)],
        scratch_shapes=[pltpu.VMEM((tc + 8, RB), F32), pltpu.VMEM((8, RB), F32)],
        compiler_params=_cparams(("parallel", "arbitrary")),
    )(uxg, uxg, uxg, cw, cb, wa, ba, wi, bi, lam)


def _rnn_bwd(uxg, hs, dy, cw, cb, wa, ba, wi, bi, lam, wat, wit, *, tc=640):
    T = uxg.shape[0]
    nt = T // tc
    nsub = tc // SUB
    blk, halo, vec, cwv, mat = _rnn_specs(tc, nt, True)

    def body(x_ref, xh_ref, ug_ref, h_ref, hh_ref, dy_ref, cw_ref, cb_ref, wa_ref, ba_ref, wi_ref,
             bi_ref, lam_ref, wat_ref, wit_ref,
             dux_ref, dug_ref, dcw_ref, dcb_ref, dwa_ref, dba_ref, dwi_ref, dbi_ref, dlam_ref,
             xb, hb, ab, dxb, xcs, rs, igs, mms, dgas, dgis, carry):
        k = pl.program_id(1)
        kt = nt - 1 - k

        @pl.when(k == 0)
        def _():
            carry[...] = jnp.zeros_like(carry)
            for ref in (dcw_ref, dcb_ref, dwa_ref, dba_ref, dwi_ref, dbi_ref, dlam_ref):
                ref[...] = jnp.zeros_like(ref)

        xb[0:8, :] = jnp.where(kt > 0, xh_ref[...], 0.0)
        xb[8:, :] = x_ref[...]
        hb[0:8, :] = jnp.where(kt > 0, hh_ref[...], 0.0)
        hb[8:, :] = h_ref[...]
        cwv_, cbv = cw_ref[...], cb_ref[...]
        wav, wiv = wa_ref[...], wi_ref[...]
        bav, biv, lamv = ba_ref[...], bi_ref[...], lam_ref[...]
        ab[tc:tc + 8, :] = jnp.broadcast_to(carry[1:2, :], (8, RB))
        dxb[tc:tc + 8, :] = carry[8:16, :]
        log_sig = None
        for sc in range(nsub):
            r0 = sc * SUB
            xc = cbv + cwv_[0:1, :] * xb[pl.ds(5 + r0, SUB), :]
            for j in range(1, CW):
                xc = xc + cwv_[j:j + 1, :] * xb[pl.ds(5 + j + r0, SUB), :]
            r, ig, log_sig, a, mm = _lru_gates(xc, wav, bav, wiv, biv, lamv)
            xcs[pl.ds(r0, SUB), :] = xc
            rs[pl.ds(r0, SUB), :] = r
            igs[pl.ds(r0, SUB), :] = ig
            mms[pl.ds(r0, SUB), :] = mm
            ab[pl.ds(r0, SUB), :] = a
        sig_neg = _sigmoid(-lamv)
        g_in = carry[0:1, :]
        dlam_acc = jnp.zeros((1, RB), F32)
        for sc in reversed(range(nsub)):
            r0 = sc * SUB
            xc, r, ig, mm = xcs[pl.ds(r0, SUB), :], rs[pl.ds(r0, SUB), :], igs[pl.ds(r0, SUB), :], mms[pl.ds(r0, SUB), :]
            a = ab[pl.ds(r0, SUB), :]
            a_next = ab[pl.ds(r0 + 1, SUB), :]
            hv = hb[pl.ds(8 + r0, SUB), :]
            hprev = hb[pl.ds(7 + r0, SUB), :]
            dyv = dy_ref[pl.ds(r0, SUB), :]
            gl, dgl = _gelu_parts(ug_ref[pl.ds(r0, SUB), :])
            dug_ref[pl.ds(r0, SUB), :] = (dyv * hv * dgl).astype(BF16)
            G = _scan_rev(a_next, dyv * gl, g_in)
            g_in = G[0:1, :]
            rows = kt * tc + r0 + lax.broadcasted_iota(jnp.int32, (SUB, RB), 0)
            db = jnp.where(rows >= PAD, G, 0.0)
            da = G * hprev
            dmm = db * (ig * xc)
            di = db * (mm * xc)
            dxc = db * (mm * ig)
            dlog_a = da * a - dmm * (a * a) / jnp.maximum(mm, 1e-30)
            dr = dlog_a * (LRU_C * log_sig)
            dlam_acc = dlam_acc + jnp.sum(dlog_a * (LRU_C * r), axis=0, keepdims=True)
            dga = dr * r * (1.0 - r)
            dgi = di * ig * (1.0 - ig)
            dgab, dgib = dga.astype(BF16), dgi.astype(BF16)
            dgas[pl.ds(r0, SUB), :] = dgab
            dgis[pl.ds(r0, SUB), :] = dgib
            dba_ref[...] += jnp.sum(dga, axis=0, keepdims=True)
            dbi_ref[...] += jnp.sum(dgi, axis=0, keepdims=True)
            dxc = dxc + jnp.dot(dgab, wat_ref[...], preferred_element_type=F32) \
                + jnp.dot(dgib, wit_ref[...], preferred_element_type=F32)
            dxb[pl.ds(r0, SUB), :] = dxc
        dlam_ref[...] += dlam_acc * sig_neg
        xcb = xcs[...].astype(BF16)
        tn = (((0,), (0,)), ((), ()))
        dwa_ref[...] += lax.dot_general(xcb, dgas[...], tn, preferred_element_type=F32)
        dwi_ref[...] += lax.dot_general(xcb, dgis[...], tn, preferred_element_type=F32)
        dxc_all = dxb[0:tc, :]
        dcb_ref[...] += jnp.sum(dxc_all, axis=0, keepdims=True)
        rows_all = kt * tc + lax.broadcasted_iota(jnp.int32, (tc, RB), 0)
        dux = jnp.zeros((tc, RB), F32)
        for j in range(CW):
            dcw_ref[j:j + 1, :] += jnp.sum(dxc_all * xb[pl.ds(5 + j, tc), :], axis=0, keepdims=True)
            dux = dux + cwv_[j:j + 1, :] * dxb[pl.ds(CW - 1 - j, tc), :]
        dux_ref[...] = jnp.where(rows_all >= PAD, dux, 0.0).astype(BF16)
        carry[0:1, :] = g_in
        carry[1:2, :] = ab[0:1, :]
        carry[8:16, :] = dxb[0:8, :]

    vec_out = pl.BlockSpec((1, RB), lambda c, k: (0, c))
    return pl.pallas_call(
        body, name="rnn_bwd", grid=(NBLK, nt),
        in_specs=[blk(0), halo(0), blk(NBLK), blk(0), halo(0), blk(0), cwv, vec, mat, vec, mat, vec, vec, mat, mat],
        out_specs=[blk(0), blk(0), cwv, vec_out, mat, vec_out, mat, vec_out, vec_out],
        out_shape=[jax.ShapeDtypeStruct((T, DR), BF16), jax.ShapeDtypeStruct((T, DR), BF16),
                   jax.ShapeDtypeStruct((CW, DR), F32), jax.ShapeDtypeStruct((1, DR), F32),
                   jax.ShapeDtypeStruct((NBLK, RB, RB), F32), jax.ShapeDtypeStruct((1, DR), F32),
                   jax.ShapeDtypeStruct((NBLK, RB, RB), F32), jax.ShapeDtypeStruct((1, DR), F32),
                   jax.ShapeDtypeStruct((1, DR), F32)],
        scratch_shapes=[pltpu.VMEM((tc + 8, RB), F32), pltpu.VMEM((tc + 8, RB), F32),
                        pltpu.VMEM((tc + 8, RB), F32), pltpu.VMEM((tc + 8, RB), F32),
                        pltpu.VMEM((tc, RB), F32), pltpu.VMEM((tc, RB), F32), pltpu.VMEM((tc, RB), F32),
                        pltpu.VMEM((tc, RB), F32), pltpu.VMEM((tc, RB), BF16), pltpu.VMEM((tc, RB), BF16),
                        pltpu.VMEM((16, RB), F32)],
        compiler_params=_cparams(("parallel", "arbitrary")),
    )(uxg, uxg, uxg, hs, hs, dy, cw, cb, wa, ba, wi, bi, lam, wat, wit)


def _attn_prep(q_all, kv_all, ukr, tab, *, tm=320):
    T = q_all.shape[0]

    def body(q_ref, kv_ref, kr_ref, tab_ref, qo_ref, ko_ref, vo_ref):
        tab_v = tab_ref[...]
        lane = lax.broadcasted_iota(jnp.int32, (tm, LANES), 1)
        t1 = kr_ref[...] * tab_v
        kro = jnp.where(lane < ROPE, t1 + pltpu.roll(t1, ROPE, 1), 0.0).astype(BF16)
        for h in range(NH):
            c0 = h * QW
            qo_ref[h, :, 0:NOPE] = (q_ref[:, c0:c0 + NOPE].astype(F32) * SCALE).astype(BF16)
            t2 = q_ref[:, c0 + NOPE:c0 + QW].astype(F32) * tab_v
            qo_ref[h, :, NOPE:QW] = ((t2 + pltpu.roll(t2, ROPE, 1)) * SCALE).astype(BF16)
            ko_ref[h, :, 0:NOPE] = kv_ref[:, c0:c0 + NOPE].astype(BF16)
            ko_ref[h, :, NOPE:QW] = kro
            vo_ref[h, :, :] = kv_ref[:, c0 + NOPE:c0 + QW].astype(BF16)

    return pl.pallas_call(
        body, name="attn_prep", grid=(T // tm,),
        in_specs=[_rows(tm, NH * QW), _rows(tm, NH * QW), _rows(tm, LANES), _rows(tm, LANES)],
        out_specs=[pl.BlockSpec((NH, tm, QW), lambda i: (0, i, 0)), pl.BlockSpec((NH, tm, QW), lambda i: (0, i, 0)),
                   pl.BlockSpec((NH, tm, VD), lambda i: (0, i, 0))],
        out_shape=[jax.ShapeDtypeStruct((NH, T, QW), BF16), jax.ShapeDtypeStruct((NH, T, QW), BF16),
                   jax.ShapeDtypeStruct((NH, T, VD), BF16)],
        compiler_params=_cparams(("parallel",)),
    )(q_all, kv_all, ukr, tab)


def _attn_prep_bwd(dq, dk, dv, tab, *, tm=320):
    T = dq.shape[1]

    def body(dq_ref, dk_ref, dv_ref, tab_ref, dqa_ref, dkva_ref, dkr_ref):
        tab_v = tab_ref[...]
        lane = lax.broadcasted_iota(jnp.int32, (tm, LANES), 1)
        dkro = jnp.zeros((tm, LANES), F32)
        for h in range(NH):
            c0 = h * QW
            dqa_ref[:, c0:c0 + NOPE] = (dq_ref[h, :, 0:NOPE] * SCALE).astype(BF16)
            d2 = dq_ref[h, :, NOPE:QW]
            dqa_ref[:, c0 + NOPE:c0 + QW] = ((d2 + pltpu.roll(d2, ROPE, 1)) * tab_v * SCALE).astype(BF16)
            dkva_ref[:, c0:c0 + NOPE] = dk_ref[h, :, 0:NOPE].astype(BF16)
            dkva_ref[:, c0 + NOPE:c0 + QW] = dv_ref[h, :, :].astype(BF16)
            dkro = dkro + dk_ref[h, :, NOPE:QW]
        dkro = jnp.where(lane < ROPE, dkro, 0.0)
        dkr_ref[...] = ((dkro + pltpu.roll(dkro, ROPE, 1)) * tab_v).astype(BF16)

    return pl.pallas_call(
        body, name="attn_prep_bwd", grid=(T // tm,),
        in_specs=[pl.BlockSpec((NH, tm, QW), lambda i: (0, i, 0)), pl.BlockSpec((NH, tm, QW), lambda i: (0, i, 0)),
                  pl.BlockSpec((NH, tm, VD), lambda i: (0, i, 0)), _rows(tm, LANES)],
        out_specs=[_rows(tm, NH * QW), _rows(tm, NH * QW), _rows(tm, LANES)],
        out_shape=[jax.ShapeDtypeStruct((T, NH * QW), BF16), jax.ShapeDtypeStruct((T, NH * QW), BF16),
                   jax.ShapeDtypeStruct((T, LANES), BF16)],
        compiler_params=_cparams(("parallel",)),
    )(dq, dk, dv, tab)


def _visible(q0, k0, nq, nk):
    rows = q0 + lax.broadcasted_iota(jnp.int32, (nq, nk), 0)
    cols = k0 + lax.broadcasted_iota(jnp.int32, (nq, nk), 1)
    return ((cols >> 6) <= (rows >> 6)) & (cols >= PAD)


def _visible_t(q0, k0, nq, nk):
    cols = k0 + lax.broadcasted_iota(jnp.int32, (nk, nq), 0)
    rows = q0 + lax.broadcasted_iota(jnp.int32, (nk, nq), 1)
    return ((cols >> 6) <= (rows >> 6)) & (cols >= PAD)


_NT = (((1,), (1,)), ((), ()))
ATTN_BLOCK = 1664


def _attn_block(T):
    return ATTN_BLOCK if T % ATTN_BLOCK == 0 else 640


def _round_up(n, m):
    return -(-n // m) * m


def _flash_fwd(q, k, v, *, gather=(), bq=None):
    T = q.shape[1]
    bq = bq or _attn_block(T)
    nq = T // bq
    rs = bq // CHAINS
    n = len(gather)

    def body(*refs):
        q_ref, k_ref, v_ref = refs[:3]
        g_src = refs[3:3 + n]
        o_ref, lse_ref = refs[3 + n:5 + n]
        g_out = refs[5 + n:5 + 2 * n]
        scr = refs[5 + 2 * n:]
        m_s, l_s, acc_s = scr[:CHAINS], scr[CHAINS:2 * CHAINS], scr[2 * CHAINS:3 * CHAINS]
        g_scr = scr[3 * CHAINS:]
        h = pl.program_id(0)
        i = pl.program_id(1)
        if n:
            @pl.when((h == 0) & (i == 0))
            def _():
                _gather_start(_gather_descs(g_src, g_out, g_scr))

        for r in range(CHAINS):
            m_s[r][...] = jnp.full_like(m_s[r], NEG)
            l_s[r][...] = jnp.zeros_like(l_s[r])
            acc_s[r][...] = jnp.zeros_like(acc_s[r])

        def step(j, masked, diag):
            off = pl.multiple_of(j * bq, bq)
            for r in range(CHAINS):
                rows = pl.ds(r * rs, rs)
                kw = min(bq, _round_up((r + 1) * rs, LANES)) if diag else bq
                kv_ = k_ref[pl.ds(off, kw), :]
                vv = v_ref[pl.ds(off, kw), :]
                s = lax.dot_general(q_ref[rows, :], kv_, _NT, preferred_element_type=F32)
                if masked:
                    s = jnp.where(_visible(i * bq + r * rs, j * bq, rs, kw), s, NEG)
                m_prev = m_s[r][...]
                m_new = jnp.maximum(m_prev, jnp.max(s, axis=-1, keepdims=True))
                p = jnp.exp(s - m_new)
                alpha = jnp.exp(m_prev - m_new)
                l_s[r][...] = alpha * l_s[r][...] + jnp.sum(p, axis=-1, keepdims=True)
                acc_s[r][...] = alpha * acc_s[r][...] + jnp.dot(p.astype(BF16), vv, preferred_element_type=F32)
                m_s[r][...] = m_new

        @pl.when(i == 0)
        def _():
            step(0, True, True)

        @pl.when(i > 0)
        def _():
            step(0, True, False)

            def loop(j, c):
                step(j, False, False)
                return c

            lax.fori_loop(1, i, loop, 0)
            step(i, True, True)

        for r in range(CHAINS):
            rows = pl.ds(r * rs, rs)
            o_ref[rows, :] = (acc_s[r][...] / l_s[r][...]).astype(BF16)
            lse_ref[rows, :] = m_s[r][...] + jnp.log(l_s[r][...])

        if n:
            @pl.when((h == NH - 1) & (i == nq - 1))
            def _():
                _gather_wait(_gather_descs(g_src, g_out, g_scr, with_loads=False))

    return pl.pallas_call(
        body, name="flash_fwd", grid=(NH, nq),
        in_specs=[pl.BlockSpec((None, bq, QW), lambda h, i: (h, i, 0)),
                  pl.BlockSpec((None, T, QW), lambda h, i: (h, 0, 0)),
                  pl.BlockSpec((None, T, VD), lambda h, i: (h, 0, 0))] + [HBM] * n,
        out_specs=[pl.BlockSpec((bq, VD), lambda h, i: (i, h)),
                   pl.BlockSpec((None, bq, 1), lambda h, i: (h, i, 0))] + [HBM] * n,
        out_shape=[jax.ShapeDtypeStruct((T, NH * VD), BF16), jax.ShapeDtypeStruct((NH, T, 1), F32)]
        + [jax.ShapeDtypeStruct((4,) + g.shape, g.dtype) for g in gather],
        scratch_shapes=[pltpu.VMEM((rs, 1), F32)] * (2 * CHAINS) + [pltpu.VMEM((rs, VD), F32)] * CHAINS
        + (_gather_scratch(gather) if n else []),
        compiler_params=_cparams(("arbitrary", "arbitrary")),
    )(q, k, v, *gather)


def _attn_delta(o, do, *, tm=640):
    T = o.shape[0]

    def body(o_ref, do_ref, d_ref):
        prod = o_ref[...].astype(F32) * do_ref[...].astype(F32)
        for h in range(NH):
            d_ref[h, :, :] = jnp.sum(prod[:, h * VD:(h + 1) * VD], axis=-1, keepdims=True)

    return pl.pallas_call(
        body, name="attn_delta", grid=(T // tm,),
        in_specs=[_rows(tm, NH * VD), _rows(tm, NH * VD)],
        out_specs=pl.BlockSpec((NH, tm, 1), lambda i: (0, i, 0)),
        out_shape=jax.ShapeDtypeStruct((NH, T, 1), F32),
        compiler_params=_cparams(("parallel",)),
    )(o, do)


_TN = (((0,), (0,)), ((), ()))


def _flash_bwd(q, k, v, do, lse_row, delta_row, *, scatter=(), bq=None):
    T = q.shape[1]
    bq = bq or _attn_block(T)
    nq = T // bq
    rs = bq // CHAINS
    n = len(scatter)

    def body(*refs):
        q_ref, k_ref, v_ref, do_ref, lse_ref, dl_ref = refs[:6]
        s_src = refs[6:6 + n]
        dq_ref, dk_ref, dv_ref = refs[6 + n:9 + n]
        s_out = refs[9 + n:9 + 2 * n]
        s_scr = refs[9 + 2 * n:]
        h = pl.program_id(0)
        j = pl.program_id(1)
        if n:
            @pl.when((h == 0) & (j == 0))
            def _():
                for cp in _scatter_descs(s_src, s_out, s_scr):
                    cp.start()

        @pl.when(j == 0)
        def _():
            dq_ref[...] = jnp.zeros_like(dq_ref)

        dk_ref[...] = jnp.zeros_like(dk_ref)
        dv_ref[...] = jnp.zeros_like(dv_ref)

        def step(i, masked, diag):
            for r in range(CHAINS):
                rows = pl.ds(r * rs, rs)
                q0 = (r * rs) // LANES * LANES if diag else 0
                qn = bq - q0
                off = pl.multiple_of(i * bq + q0, LANES)
                qv = q_ref[pl.ds(off, qn), :]
                dov = do_ref[pl.ds(off, qn), :]
                lse_v = lse_ref[:, pl.ds(off, qn)]
                dl_v = dl_ref[:, pl.ds(off, qn)]
                st = lax.dot_general(k_ref[rows, :], qv, _NT, preferred_element_type=F32)
                if masked:
                    st = jnp.where(_visible_t(i * bq + q0, j * bq + r * rs, qn, rs), st, NEG)
                pt = jnp.exp(st - lse_v)
                dv_ref[rows, :] += jnp.dot(pt.astype(BF16), dov, preferred_element_type=F32)
                dpt = lax.dot_general(v_ref[rows, :], dov, _NT, preferred_element_type=F32)
                dst = (pt * (dpt - dl_v)).astype(BF16)
                dk_ref[rows, :] += jnp.dot(dst, qv, preferred_element_type=F32)
                dq_ref[pl.ds(off, qn), :] += lax.dot_general(dst, k_ref[rows, :], _TN,
                                                             preferred_element_type=F32)

        step(j, True, True)

        @pl.when(j == 0)
        def _():
            def loop(i, c):
                step(i, True, False)
                return c
            lax.fori_loop(1, nq, loop, 0)

        @pl.when(j > 0)
        def _():
            def loop(i, c):
                step(i, False, False)
                return c
            lax.fori_loop(j + 1, nq, loop, 0)

        if n:
            @pl.when((h == NH - 1) & (j == nq - 1))
            def _():
                for cp in _scatter_descs(s_src, s_out, s_scr):
                    cp.wait()

    return pl.pallas_call(
        body, name="flash_bwd", grid=(NH, nq),
        in_specs=[pl.BlockSpec((None, T, QW), lambda h, j: (h, 0, 0)),
                  pl.BlockSpec((None, bq, QW), lambda h, j: (h, j, 0)),
                  pl.BlockSpec((None, bq, VD), lambda h, j: (h, j, 0)),
                  pl.BlockSpec((T, VD), lambda h, j: (0, h)),
                  pl.BlockSpec((None, 1, T), lambda h, j: (h, 0, 0)),
                  pl.BlockSpec((None, 1, T), lambda h, j: (h, 0, 0))] + [HBM] * n,
        out_specs=[pl.BlockSpec((None, T, QW), lambda h, j: (h, 0, 0)),
                   pl.BlockSpec((None, bq, QW), lambda h, j: (h, j, 0)),
                   pl.BlockSpec((None, bq, VD), lambda h, j: (h, j, 0))] + [HBM] * n,
        out_shape=[jax.ShapeDtypeStruct((NH, T, QW), F32), jax.ShapeDtypeStruct((NH, T, QW), F32),
                   jax.ShapeDtypeStruct((NH, T, VD), F32)]
        + [jax.ShapeDtypeStruct((3,) + s.shape[1:], s.dtype) for s in scatter],
        scratch_shapes=[_dma_sems(3 * n), _dma_sems(3 * n)] if n else [],
        compiler_params=_cparams(("arbitrary", "arbitrary")),
    )(q, k, v, do, lse_row, delta_row, *scatter)


def _rope_table(T):
    pos = (jnp.arange(T, dtype=jnp.int32) - PAD).astype(F32)
    inv_freq = ROPE_THETA ** (-jnp.arange(0, ROPE, 2, dtype=F32) / ROPE)
    ang = pos[:, None] * inv_freq[None, :]
    cos, sin = jnp.cos(ang), jnp.sin(ang)
    return jnp.concatenate([cos, cos, -sin, sin], axis=1)


def _swap_halves(w):
    return jnp.concatenate([w[..., ROPE // 2:], w[..., :ROPE // 2]], axis=-1)


O_UX, O_UG, O_UQ, O_UKV, O_UKR, O_UM = 0, DR, 2 * DR, 2 * DR + QR, 2 * DR + QR + KVR, 2 * DR + QR + KVR + ROPE


def _prep_weights(w):
    b = lambda a: a.astype(BF16)
    w_in = w["w_in"]
    kr = w_in[:, O_UKR:O_UM]
    p = {
        "w_xg": b(w_in[:, :O_UQ]),
        "w_q": b(w_in[:, O_UQ:O_UKV]),
        "w_kv": b(w_in[:, O_UKV:O_UKR]),
        "w_kr": b(jnp.concatenate([kr, _swap_halves(kr)], axis=1)),
        "w_m": b(w_in[:, O_UM:]),
    }
    wq = w["w_uq"].reshape(QR, NH, NOPE + ROPE)
    p["w_uq"] = b(jnp.concatenate([wq, _swap_halves(wq[..., NOPE:])], axis=-1).reshape(QR, NH * QW))
    p["w_ukv"] = b(w["w_ukv"])
    for n in ("w_xg", "w_q", "w_kv", "w_kr", "w_m", "w_uq", "w_ukv"):
        p[n + "_t"] = p[n].T
    p["wa"] = b(w["w_rec_a"])
    p["wi"] = b(w["w_rec_i"])
    p["wa_t"] = jnp.swapaxes(p["wa"], 1, 2)
    p["wi_t"] = jnp.swapaxes(p["wi"], 1, 2)
    return p


def _prep_late_weights(w):
    b = lambda a: a.astype(BF16)
    p = {"w_br": b(w["w_branch"][:DR]), "w_ba": b(w["w_branch"][DR:]), "w_out": b(w["w_out"]),
         "w_fi": b(w["w_ffn_in"]), "w_fo": b(w["w_ffn_out"])}
    for n in tuple(p):
        p[n + "_t"] = p[n].T
    return p


LATE = ("w_branch", "w_out", "w_ffn_in", "w_ffn_out")


def _local_step(x, tgt, w, late=None, reduce_first=None):
    S = x.shape[0]
    T = FRONT + S
    p = _prep_weights(w)
    tab = _rope_table(T)
    h0 = jnp.concatenate([jnp.zeros((PAD, D), F32), w["meta_tokens"], x], axis=0)
    row = lambda v: v.reshape(1, -1)

    z = _rmsnorm_fwd(h0, row(w["norm_mix_g"]), name="norm_mix")
    uxg = _mm(z, p["w_xg"], name="mm_uxg")
    uq = _mm(z, p["w_q"], name="mm_uq")
    ukv = _mm(z, p["w_kv"], name="mm_ukv")
    ukr = _mm(z, p["w_kr"], name="mm_ukr")
    um = _mm(z, p["w_m"], name="mm_um", out_dtype=BF16)
    rnn_w = (w["conv_w"], row(w["conv_b"]), p["wa"], row(w["b_rec_a"]), p["wi"], row(w["b_rec_i"]),
             row(w["lru_lambda"]))
    hs, y_rnn = _rnn_fwd(uxg, *rnn_w)
    qn = _rmsnorm_fwd(uq, row(w["q_norm_g"]), name="norm_q")
    kvn = _rmsnorm_fwd(ukv, row(w["kv_norm_g"]), name="norm_kv")
    q_all = _mm(qn, p["w_uq"], name="mm_q", out_dtype=BF16)
    kv_all = _mm(kvn, p["w_ukv"], name="mm_kv", out_dtype=BF16)
    qh, kh, vh = _attn_prep(q_all, kv_all, ukr, tab)
    y_att, lse, *stacks = _flash_fwd(qh, kh, vh, gather=late[0] if late else ())
    if late:
        w = {**w, **late[1](stacks)}
    p.update(_prep_late_weights(w))
    p_rnn = _mm(y_rnn, p["w_br"], name="mm_prnn", out_dtype=BF16)
    p_att = _mm(y_att, p["w_ba"], name="mm_patt", out_dtype=BF16)
    bg = row(w["b_gate"])
    mixed = _gate_mix_fwd(um, bg, p_rnn, p_att)
    h1 = _mm(mixed, p["w_out"], name="mm_out", res=h0)
    zf = _rmsnorm_fwd(h1, row(w["norm_ffn_g"]), name="norm_ffn")
    ff = _mm(zf, p["w_fi"], name="mm_ffn_in", out_dtype=BF16)
    act = _swiglu_fwd(ff)
    h2 = _mm(act, p["w_fo"], name="mm_ffn_out", res=h1)

    g = {}
    dh2, dh2b, dg_fin, lsum = _loss_head(h2, tgt, row(w["final_norm_g"]))
    loss = 0.5 * jnp.sum(lsum) / D
    g["final_norm_g"] = dg_fin.reshape(-1)
    dact = _mm(dh2b, p["w_fo_t"], name="mm_dact", out_dtype=BF16)
    g["w_ffn_out"] = _mm_tn(act, dh2b, name="mm_dw_ffn_out")
    dff = _swiglu_bwd(ff, dact)
    dzf = _mm(dff, p["w_fi_t"], name="mm_dzf")
    g["w_ffn_in"] = _mm_tn(zf, dff, name="mm_dw_ffn_in")
    dh1, dh1b, dg = _rmsnorm_bwd(h1, row(w["norm_ffn_g"]), dzf, dh2, name="norm_ffn_bwd")
    g["norm_ffn_g"] = dg
    dmixed = _mm(dh1b, p["w_out_t"], name="mm_dmixed", out_dtype=BF16)
    g["w_out"] = _mm_tn(mixed, dh1b, name="mm_dw_out")
    dp_rnn, dp_att, dum, dbg = _gate_mix_bwd(um, bg, p_rnn, p_att, dmixed)
    g["b_gate"] = dbg.reshape(2, D)
    dy_rnn = _mm(dp_rnn, p["w_br_t"], name="mm_dy_rnn")
    dy_att = _mm(dp_att, p["w_ba_t"], name="mm_dy_att", out_dtype=BF16)
    g["w_branch"] = jnp.concatenate([_mm_tn(y_rnn, dp_rnn, name="mm_dw_br"),
                                     _mm_tn(y_att, dp_att, name="mm_dw_ba")], axis=0)
    delta = _attn_delta(y_att, dy_att)
    first = reduce_first({n: g[n] for n in LATE}) if reduce_first else ()
    dq, dk, dv, *received = _flash_bwd(qh, kh, vh, dy_att, lse.reshape(NH, 1, T), delta.reshape(NH, 1, T),
                                       scatter=first)
    dq_all, dkv_all, dukr = _attn_prep_bwd(dq, dk, dv, tab)
    dqn = _mm(dq_all, p["w_uq_t"], name="mm_dqn")
    dkvn = _mm(dkv_all, p["w_ukv_t"], name="mm_dkvn")
    dwq = _mm_tn(qn, dq_all, name="mm_dw_uq").reshape(QR, NH, QW)
    dwq_rope = dwq[..., NOPE:NOPE + ROPE] + _swap_halves(dwq[..., NOPE + ROPE:])
    g["w_uq"] = jnp.concatenate([dwq[..., :NOPE], dwq_rope], axis=-1).reshape(QR, NH * (NOPE + ROPE))
    g["w_ukv"] = _mm_tn(kvn, dkv_all, name="mm_dw_ukv")
    duq, dg = _rmsnorm_bwd(uq, row(w["q_norm_g"]), dqn, None, name="norm_q_bwd", want_f32=False)
    g["q_norm_g"] = dg
    dukv, dg = _rmsnorm_bwd(ukv, row(w["kv_norm_g"]), dkvn, None, name="norm_kv_bwd", want_f32=False)
    g["kv_norm_g"] = dg
    (dux, dug, g["conv_w"], g["conv_b"], g["w_rec_a"], g["b_rec_a"], g["w_rec_i"], g["b_rec_i"],
     g["lru_lambda"]) = _rnn_bwd(uxg, hs, dy_rnn, *rnn_w, p["wa_t"], p["wi_t"])
    dz = _mm_sum([(dux, p["w_xg_t"][:DR]), (dug, p["w_xg_t"][DR:]), (duq, p["w_q_t"]), (dukv, p["w_kv_t"]),
                  (dukr, p["w_kr_t"]), (dum, p["w_m_t"])], name="mm_dz")
    dwkr = _mm_tn(z, dukr, name="mm_dw_kr")
    g["w_in"] = jnp.concatenate([
        _mm_tn(z, dux, name="mm_dw_x"), _mm_tn(z, dug, name="mm_dw_g"),
        _mm_tn(z, duq, name="mm_dw_q"), _mm_tn(z, dukv, name="mm_dw_kv"),
        dwkr[:, :ROPE] + _swap_halves(dwkr[:, ROPE:]),
        _mm_tn(z, dum, name="mm_dw_m")], axis=1)
    dh0, dg = _rmsnorm_bwd(h0, row(w["norm_mix_g"]), dz, dh1, name="norm_mix_bwd", want_bf16=False)
    g["norm_mix_g"] = dg
    g["meta_tokens"] = dh0[PAD:FRONT]
    return loss, dh0[FRONT:], g, (first, received)


HBM = pl.BlockSpec(memory_space=pltpu.HBM)
CHIP_FLIPS = ((1, 0), (0, 1), (1, 1))


def _place():
    return lax.axis_index("x"), lax.axis_index("y"), lax.axis_index("c")


def _flip(v, f):
    return 1 - v if f else v


def _dma_sems(n):
    return pltpu.SemaphoreType.DMA((n,))


def _gather_scratch(srcs):
    n = len(srcs)
    return [pltpu.VMEM(s.shape, s.dtype) for s in srcs] + [_dma_sems(3 * n), _dma_sems(3 * n), _dma_sems(n),
                                                            _dma_sems(n)]


def _gather_descs(src_refs, out_refs, scr, with_loads=True):
    n = len(src_refs)
    stage = scr[:n]
    send_sems, recv_sems, in_sems, local_sems = scr[n:]
    x, y, c = _place()
    me = 2 * x + y
    loads, sends, local = [], [], []
    for a in range(n):
        if with_loads:
            loads.append(pltpu.make_async_copy(src_refs[a], stage[a], in_sems.at[a]))
        for k, (fx, fy) in enumerate(CHIP_FLIPS):
            sends.append(pltpu.make_async_remote_copy(
                src_ref=stage[a], dst_ref=out_refs[a].at[me], send_sem=send_sems.at[3 * a + k],
                recv_sem=recv_sems.at[3 * a + k], device_id=(_flip(x, fx), _flip(y, fy), c),
                device_id_type=MESH))
        local.append(pltpu.make_async_copy(stage[a], out_refs[a].at[me], local_sems.at[a]))
    return loads, sends, local


def _gather_start(descs):
    loads, sends, local = descs
    for cp in loads:
        cp.start()
    for a, cp in enumerate(loads):
        cp.wait()
        for s in sends[3 * a:3 * a + 3]:
            s.start()
        local[a].start()


def _gather_wait(descs):
    _, sends, local = descs
    for cp in sends + local:
        cp.wait()


def _allgather_chips(srcs, *, name):
    n = len(srcs)

    def body(*refs):
        descs = _gather_descs(refs[:n], refs[n:2 * n], refs[2 * n:])
        _gather_start(descs)
        _gather_wait(descs)

    return pl.pallas_call(
        body, name=name, in_specs=[HBM] * n, out_specs=[HBM] * n,
        out_shape=[jax.ShapeDtypeStruct((4,) + s.shape, s.dtype) for s in srcs],
        scratch_shapes=_gather_scratch(srcs),
        compiler_params=pltpu.CompilerParams(vmem_limit_bytes=VMEM_LIMIT),
    )(*srcs)


def _scatter_descs(src_refs, out_refs, scr):
    send_sems, recv_sems = scr
    x, y, c = _place()
    copies = []
    for a in range(len(src_refs)):
        for k, (fx, fy) in enumerate(CHIP_FLIPS):
            px, py = _flip(x, fx), _flip(y, fy)
            copies.append(pltpu.make_async_remote_copy(
                src_ref=src_refs[a].at[2 * px + py], dst_ref=out_refs[a].at[k],
                send_sem=send_sems.at[3 * a + k], recv_sem=recv_sems.at[3 * a + k],
                device_id=(px, py, c), device_id_type=MESH))
    return copies


def _scatter_chips(srcs, *, name):
    n = len(srcs)

    def body(*refs):
        copies = _scatter_descs(refs[:n], refs[n:2 * n], refs[2 * n:])
        for cp in copies:
            cp.start()
        for cp in copies:
            cp.wait()

    return pl.pallas_call(
        body, name=name, in_specs=[HBM] * n, out_specs=[HBM] * n,
        out_shape=[jax.ShapeDtypeStruct((3,) + s.shape[1:], s.dtype) for s in srcs],
        scratch_shapes=[_dma_sems(3 * n), _dma_sems(3 * n)],
    )(*srcs)


def _sibling_take(srcs, *, name):
    n = len(srcs)

    def body(*refs):
        src_refs, out_refs = refs[:n], refs[n:2 * n]
        send_sems, recv_sems = refs[2 * n:]
        x, y, c = _place()
        copies = []
        for a in range(n):
            h = srcs[a].shape[1] // 2
            theirs = pl.ds(pl.multiple_of((1 - c) * h, 8), h)
            cp = pltpu.make_async_remote_copy(
                src_ref=src_refs[a].at[:, theirs, :], dst_ref=out_refs[a], send_sem=send_sems.at[a],
                recv_sem=recv_sems.at[a], device_id=(x, y, 1 - c), device_id_type=MESH)
            cp.start()
            copies.append(cp)
        for cp in copies:
            cp.wait()

    return pl.pallas_call(
        body, name=name, in_specs=[HBM] * n, out_specs=[HBM] * n,
        out_shape=[jax.ShapeDtypeStruct((4, s.shape[1] // 2, s.shape[2]), s.dtype) for s in srcs],
        scratch_shapes=[_dma_sems(n), _dma_sems(n)],
    )(*srcs)


def _sibling_swap(srcs, *, name):
    n = len(srcs)

    def body(*refs):
        src_refs, out_refs = refs[:n], refs[n:2 * n]
        send_sems, recv_sems = refs[2 * n:]
        x, y, c = _place()
        copies = []
        for a in range(n):
            cp = pltpu.make_async_remote_copy(
                src_ref=src_refs[a], dst_ref=out_refs[a], send_sem=send_sems.at[a],
                recv_sem=recv_sems.at[a], device_id=(x, y, 1 - c), device_id_type=MESH)
            cp.start()
            copies.append(cp)
        for cp in copies:
            cp.wait()

    return pl.pallas_call(
        body, name=name, in_specs=[HBM] * n, out_specs=[HBM] * n,
        out_shape=[jax.ShapeDtypeStruct(s.shape, s.dtype) for s in srcs],
        scratch_shapes=[_dma_sems(n), _dma_sems(n)],
    )(*srcs)


def _row_tile(rows, cols, n_arrays, step=16):
    budget = 24 * 1024 * 1024 // (2 * 4 * n_arrays * cols)
    best = step
    for t in range(step, rows + 1, step):
        if rows % t == 0 and t <= budget:
            best = t
    assert rows % best == 0, (rows, cols)
    return best


def _add_halves(mine, theirs, wire, *, name):
    _, h, c = mine.shape
    tm = _row_tile(h, c, 3)
    spec = pl.BlockSpec((None, tm, c), lambda s, i: (s, i, 0))

    def body(a_ref, b_ref, o_ref):
        o_ref[...] = (a_ref[...] + b_ref[...]).astype(wire)

    return pl.pallas_call(
        body, name=name, grid=(4, h // tm), in_specs=[spec, spec], out_specs=spec,
        out_shape=jax.ShapeDtypeStruct(mine.shape, wire), compiler_params=_cparams(("parallel", "parallel")),
    )(mine, theirs)


def _sum4(own, recv, *, name):
    h, c = own.shape
    tm = _row_tile(h, c, 5)

    def body(o_ref, r_ref, out_ref):
        f = lambda k: r_ref[k].astype(F32)
        out_ref[...] = ((o_ref[...].astype(F32) + f(0)) + f(1)) + f(2)

    return pl.pallas_call(
        body, name=name, grid=(h // tm,),
        in_specs=[_rows(tm, c), pl.BlockSpec((3, tm, c), lambda i: (0, i, 0))],
        out_specs=_rows(tm, c), out_shape=jax.ShapeDtypeStruct((h, c), F32),
        compiler_params=_cparams(("parallel",)),
    )(own, recv)


def _adamw(g, w, m, v, *, name):
    r, c = g.shape
    tm = _row_tile(r, c, 7, step=8)
    c1 = 1.0 / (1.0 - ADAM_B1 ** ADAM_STEP)
    c2 = 1.0 / (1.0 - ADAM_B2 ** ADAM_STEP)

    def body(g_ref, w_ref, m_ref, v_ref, d_ref, nm_ref, nv_ref):
        gv = g_ref[...]
        nm = ADAM_B1 * m_ref[...] + (1.0 - ADAM_B1) * gv
        nv = ADAM_B2 * v_ref[...] + (1.0 - ADAM_B2) * (gv * gv)
        nm_ref[...] = nm
        nv_ref[...] = nv
        d_ref[...] = -ADAM_LR * ((nm * c1) / (jnp.sqrt(nv * c2) + ADAM_EPS) + ADAM_WD * w_ref[...])

    spec = _rows(tm, c)
    shape = jax.ShapeDtypeStruct((r, c), F32)
    return pl.pallas_call(
        body, name=name, grid=(r // tm,), in_specs=[spec] * 4, out_specs=[spec] * 3,
        out_shape=[shape] * 3, compiler_params=_cparams(("parallel",)),
    )(g, w, m, v)


BIG = (("w_in", (D, 1328), 1), ("w_uq", (QR, 384), 1), ("w_ukv", (KVR, 512), 1), ("w_branch", (576, D), 0),
       ("w_out", (256, D), 0), ("w_ffn_in", (D, 1408), 1), ("w_ffn_out", (704, D), 0))
SMALL = (("meta_tokens", (NMETA, 256), 1), ("b_gate", (2, 256), 1), ("conv_w", (CW, 320), 1))
REPL = (("norm_mix_g", (D,)), ("conv_b", (DR,)), ("w_rec_a", (NBLK, RB, RB)), ("b_rec_a", (DR,)),
        ("w_rec_i", (NBLK, RB, RB)), ("b_rec_i", (DR,)), ("lru_lambda", (DR,)), ("q_norm_g", (QR,)),
        ("kv_norm_g", (KVR,)), ("norm_ffn_g", (D,)), ("final_norm_g", (D,)))
WEIGHTS = ("meta_tokens", "norm_mix_g", "w_in", "b_gate", "conv_w", "conv_b", "w_rec_a", "b_rec_a", "w_rec_i",
           "b_rec_i", "lru_lambda", "q_norm_g", "w_uq", "kv_norm_g", "w_ukv", "w_branch", "w_out", "norm_ffn_g",
           "w_ffn_in", "w_ffn_out", "final_norm_g")
W = 1024
SMALL_N = sum(math.prod(s) for _, s, _ in SMALL)
SMALL_ROWS = 8
REPL_N = sum(math.prod(s) for _, s in REPL)
QUART_ROWS = 88
assert SMALL_N <= SMALL_ROWS * W and REPL_N <= 4 * QUART_ROWS * W


def _flat_pad(parts, rows):
    v = jnp.concatenate([p.reshape(-1) for p in parts])
    return jnp.pad(v, (0, rows * W - v.shape[0])).reshape(rows, W)


def _shard_stack(full, shard_shape, axis):
    r, cs = shard_shape
    if axis == 0:
        return full.reshape(4, r, cs)
    return jnp.stack([full[:, s * cs:(s + 1) * cs] for s in range(4)])


def _unshard(stack, axis):
    if axis == 0:
        return stack.reshape(4 * stack.shape[1], stack.shape[2])
    return jnp.concatenate([stack[s] for s in range(4)], axis=1)


def _split(flat, table):
    out, off = {}, 0
    for name, shape, *_ in table:
        n = math.prod(shape)
        out[name] = flat[..., off:off + n].reshape(flat.shape[:-1] + tuple(shape))
        off += n
    return out


def kernel(x, meta_tokens, norm_mix_g, w_in, b_gate, conv_w, conv_b, w_rec_a, b_rec_a, w_rec_i, b_rec_i, lru_lambda, q_norm_g, w_uq, kv_norm_g, w_ukv, w_branch, w_out, norm_ffn_g, w_ffn_in, w_ffn_out, final_norm_g, loss_target, m_meta_tokens, m_norm_mix_g, m_w_in, m_b_gate, m_conv_w, m_conv_b, m_w_rec_a, m_b_rec_a, m_w_rec_i, m_b_rec_i, m_lru_lambda, m_q_norm_g, m_w_uq, m_kv_norm_g, m_w_ukv, m_w_branch, m_w_out, m_norm_ffn_g, m_w_ffn_in, m_w_ffn_out, m_final_norm_g, v_meta_tokens, v_norm_mix_g, v_w_in, v_b_gate, v_conv_w, v_conv_b, v_w_rec_a, v_b_rec_a, v_w_rec_i, v_b_rec_i, v_lru_lambda, v_q_norm_g, v_w_uq, v_kv_norm_g, v_w_ukv, v_w_branch, v_w_out, v_norm_ffn_g, v_w_ffn_in, v_w_ffn_out, v_final_norm_g):
    args = dict(locals())
    chip = 2 * lax.axis_index("x") + lax.axis_index("y")
    core = lax.axis_index("c")

    first_big = [b for b in BIG if b[0] not in LATE]
    late_big = [b for b in BIG if b[0] in LATE]
    bf16_shard = lambda n, s: args[n].reshape(s).astype(BF16)
    small = _flat_pad([args[n] for n, _, _ in SMALL], SMALL_ROWS)
    gathered = _allgather_chips([bf16_shard(n, s) for n, s, _ in first_big] + [small], name="gather_weights")
    w = {}
    for (name, _, axis), stack in zip(first_big, gathered):
        w[name] = _unshard(stack, axis)
    small_parts = _split(gathered[-1].reshape(4, SMALL_ROWS * W), SMALL)
    for name, _, axis in SMALL:
        w[name] = _unshard(small_parts[name], axis)
    for name, shape in REPL:
        w[name] = args[name].reshape(shape)
    finish_late = lambda stacks: {name: _unshard(st, axis) for (name, _, axis), st in zip(late_big, stacks)}

    def to_wire(red, tag, wires):
        theirs = _sibling_take(red, name="reduce_sibling_" + tag)
        parts = []
        for k, (a, t) in enumerate(zip(red, theirs)):
            h = a.shape[1] // 2
            mine = lax.dynamic_slice_in_dim(a, core * h, h, axis=1)
            parts.append(_add_halves(mine, t, wires[k], name=f"add_sibling_{tag}{k}"))
        return parts

    reduce_first = lambda gl: to_wire([_shard_stack(gl[n], s, a) for n, s, a in late_big], "a",
                                      [BF16] * len(late_big))
    loss, grad_x, g, (parts_a, recv_a) = _local_step(
        x[0], loss_target[0], w, late=([bf16_shard(n, s) for n, s, _ in late_big], finish_late),
        reduce_first=reduce_first)
    loss = lax.psum(loss, ("x", "y", "c"))

    red = [_shard_stack(g[n], s, a) for n, s, a in first_big]
    small_g = jnp.concatenate([_shard_stack(g[n], s, a).reshape(4, -1) for n, s, a in SMALL], axis=1)
    small_g = jnp.pad(small_g, ((0, 0), (0, SMALL_ROWS * W - SMALL_N))).reshape(4, SMALL_ROWS, W)
    repl_g = _flat_pad([g[n] for n, _ in REPL], 4 * QUART_ROWS).reshape(4, QUART_ROWS, W)
    red.append(jnp.concatenate([small_g, repl_g], axis=1))
    parts_b = to_wire(red, "b", [BF16] * len(first_big) + [F32])
    recv_b = _scatter_chips(parts_b, name="reduce_chips")
    order = [b[0] for b in late_big] + [b[0] for b in first_big] + ["misc"]
    halves = [_sum4(lax.dynamic_index_in_dim(p, chip, 0, keepdims=False), r, name="sum_chips_" + n)
              for n, p, r in zip(order, list(parts_a) + parts_b, list(recv_a) + list(recv_b))]
    others = _sibling_swap(halves, name="share_sibling")
    gred = [jnp.where(core == 0, jnp.concatenate([a, b], axis=0), jnp.concatenate([b, a], axis=0))
            for a, b in zip(halves, others)]

    results = {}
    shape_of = {name: shape for name, shape, _ in BIG}
    for name, gr in zip(order[:-1], gred):
        shape = shape_of[name]
        d, nm, nv = _adamw(gr, args[name].reshape(shape), args["m_" + name].reshape(shape),
                           args["v_" + name].reshape(shape), name="adamw_" + name)
        results[name] = (gr, d, nm, nv)
    g_repl = _allgather_chips([gred[-1][SMALL_ROWS:]], name="gather_repl")[0].reshape(4 * QUART_ROWS, W)
    g_misc = jnp.concatenate([gred[-1][:SMALL_ROWS], g_repl], axis=0)
    misc_state = lambda prefix: jnp.concatenate(
        [_flat_pad([args[prefix + n] for n, _, _ in SMALL], SMALL_ROWS),
         _flat_pad([args[prefix + n] for n, _ in REPL], 4 * QUART_ROWS)], axis=0)
    d, nm, nv = _adamw(g_misc, misc_state(""), misc_state("m_"), misc_state("v_"), name="adamw_misc")
    misc = (g_misc, d, nm, nv)

    outs = []
    for k in range(4):
        sm = _split(misc[k][:SMALL_ROWS].reshape(-1), SMALL)
        rp = _split(misc[k][SMALL_ROWS:].reshape(-1), REPL)
        for name in WEIGHTS:
            val = results[name][k] if name in results else (sm[name] if name in sm else rp[name])
            outs.append(val.reshape(args[name].shape))
    return (loss, grad_x[None], *outs)
```

```python
import functools
import math

import jax
import jax.numpy as jnp
from jax import lax
from jax.experimental import pallas as pl
from jax.experimental.pallas import tpu as pltpu

F32 = jnp.float32
BF16 = jnp.bfloat16

D = 1024
DR = 1280
NBLK = 10
RB = 128
CW = 4
NH = 8
NOPE = 128
ROPE = 64
VD = 128
QR = 384
KVR = 256
DFF = 2816
NMETA = 16
EPS = 1e-6
LRU_C = 8.0
ROPE_THETA = 10000.0
SCALE = 1.0 / math.sqrt(NOPE + ROPE)
NEG = -1e30
FRONT = 128
PAD = FRONT - NMETA
QW = 2 * NOPE
LANES = 128
SUB = 128
CHAINS = 4
VMEM_LIMIT = 52 * 1024 * 1024

ADAM_LR = 0.001
ADAM_B1 = 0.9
ADAM_B2 = 0.999
ADAM_EPS = 1e-08
ADAM_WD = 0.01
ADAM_STEP = 10

MESH = pl.DeviceIdType.MESH


def _cparams(sem):
    return pltpu.CompilerParams(dimension_semantics=sem, vmem_limit_bytes=VMEM_LIMIT)


def _sigmoid(x):
    return 1.0 / (1.0 + jnp.exp(-x))


def _gelu_parts(x):
    c = math.sqrt(2.0 / math.pi)
    inner = c * (x + 0.044715 * x * x * x)
    t = jnp.tanh(inner)
    g = 0.5 * x * (1.0 + t)
    dg = 0.5 * (1.0 + t) + 0.5 * x * (1.0 - t * t) * c * (1.0 + 3.0 * 0.044715 * x * x)
    return g, dg


def _divisors(n, step, cap):
    return [d for d in range(step, min(n, cap) + 1, step) if n % d == 0] or [n]


MM_VMEM_BUDGET = 40 * 1024 * 1024
MM_MAX_ROWS = 1664
MM_MAX_COLS = 1408


def _mm_tiles(M, K, N, a_item, out_item, has_res):
    best = None
    for tn in _divisors(N, LANES, MM_MAX_COLS):
        for tm in _divisors(M, 16, MM_MAX_ROWS):
            need = 2 * (tm * K * a_item + K * tn * 2 + tm * tn * (out_item + (4 if has_res else 0)))
            if need <= MM_VMEM_BUDGET and (best is None or tm * tn > best[0] * best[1]):
                best = (tm, tn)
    assert best is not None, (M, K, N)
    return best


def _mm(a, b, *, name, out_dtype=F32, res=None):
    M, K = a.shape
    N = b.shape[1]
    has_res = res is not None
    tm, tn = _mm_tiles(M, K, N, a.dtype.itemsize, jnp.dtype(out_dtype).itemsize, has_res)

    def body(*refs):
        if has_res:
            a_ref, b_ref, r_ref, o_ref = refs
        else:
            a_ref, b_ref, o_ref = refs
        acc = jnp.dot(a_ref[...].astype(BF16), b_ref[...].astype(BF16), preferred_element_type=F32)
        if has_res:
            acc = acc + r_ref[...].astype(F32)
        o_ref[...] = acc.astype(o_ref.dtype)

    a_bytes = M * K * a.dtype.itemsize
    b_bytes = K * N * b.dtype.itemsize
    rows_outer = a_bytes + (M // tm) * b_bytes <= b_bytes + (N // tn) * a_bytes
    if rows_outer:
        grid = (M // tm, N // tn)
        ia, ib, io = (lambda i, j: (i, 0)), (lambda i, j: (0, j)), (lambda i, j: (i, j))
    else:
        grid = (N // tn, M // tm)
        ia, ib, io = (lambda j, i: (i, 0)), (lambda j, i: (0, j)), (lambda j, i: (i, j))
    in_specs = [pl.BlockSpec((tm, K), ia), pl.BlockSpec((K, tn), ib)]
    args = [a, b]
    if has_res:
        in_specs.append(pl.BlockSpec((tm, tn), io))
        args.append(res)
    return pl.pallas_call(
        body, name=name, grid=grid, in_specs=in_specs,
        out_specs=pl.BlockSpec((tm, tn), io),
        out_shape=jax.ShapeDtypeStruct((M, N), out_dtype),
        compiler_params=_cparams(("parallel", "parallel")),
    )(*args)


def _mm_sum(pairs, *, name, res=None, out_dtype=F32):
    M = pairs[0][0].shape[0]
    N = pairs[0][1].shape[1]
    ks = [a.shape[1] for a, _ in pairs]
    has_res = res is not None
    tm, tn = _mm_tiles(M, sum(ks), N, 2, jnp.dtype(out_dtype).itemsize, has_res)
    n = len(pairs)

    def body(*refs):
        acc = None
        for k in range(n):
            d = jnp.dot(refs[2 * k][...].astype(BF16), refs[2 * k + 1][...].astype(BF16),
                        preferred_element_type=F32)
            acc = d if acc is None else acc + d
        if has_res:
            acc = acc + refs[2 * n][...].astype(F32)
        refs[-1][...] = acc.astype(refs[-1].dtype)

    in_specs, args = [], []
    for (a, b), kk in zip(pairs, ks):
        in_specs += [pl.BlockSpec((tm, kk), lambda i, j: (i, 0)), pl.BlockSpec((kk, tn), lambda i, j: (0, j))]
        args += [a, b]
    if has_res:
        in_specs.append(pl.BlockSpec((tm, tn), lambda i, j: (i, j)))
        args.append(res)
    return pl.pallas_call(
        body, name=name, grid=(M // tm, N // tn), in_specs=in_specs,
        out_specs=pl.BlockSpec((tm, tn), lambda i, j: (i, j)),
        out_shape=jax.ShapeDtypeStruct((M, N), out_dtype),
        compiler_params=_cparams(("parallel", "parallel")),
    )(*args)


def _mm_tn(a, b, *, name):
    T, K1 = a.shape
    N = b.shape[1]
    tt = _divisors(T, 16, MM_MAX_ROWS)[-1]
    tk = _divisors(K1, LANES, MM_MAX_COLS)[-1]
    tn = _divisors(N, LANES, MM_MAX_COLS)[-1]

    def body(a_ref, b_ref, o_ref):
        @pl.when(pl.program_id(2) == 0)
        def _():
            o_ref[...] = jnp.zeros_like(o_ref)

        o_ref[...] += lax.dot_general(a_ref[...].astype(BF16), b_ref[...].astype(BF16),
                                      (((0,), (0,)), ((), ())), preferred_element_type=F32)

    return pl.pallas_call(
        body, name=name, grid=(K1 // tk, N // tn, T // tt),
        in_specs=[pl.BlockSpec((tt, tk), lambda i, j, t: (t, i)),
                  pl.BlockSpec((tt, tn), lambda i, j, t: (t, j))],
        out_specs=pl.BlockSpec((tk, tn), lambda i, j, t: (i, j)),
        out_shape=jax.ShapeDtypeStruct((K1, N), F32),
        compiler_params=_cparams(("parallel", "parallel", "arbitrary")),
    )(a, b)


def _rows(tm, w, cb=0):
    return pl.BlockSpec((tm, w), lambda i: (i, cb))


def _const(shape):
    n = len(shape)
    return pl.BlockSpec(shape, lambda i: (0,) * n)


def _rmsnorm_fwd(x, g, *, name, tm=640):
    T, C = x.shape

    def body(x_ref, g_ref, o_ref):
        xv = x_ref[...]
        r = lax.rsqrt(jnp.mean(xv * xv, axis=-1, keepdims=True) + EPS)
        o_ref[...] = ((xv * r) * g_ref[...]).astype(BF16)

    return pl.pallas_call(
        body, name=name, grid=(T // tm,),
        in_specs=[_rows(tm, C), _const((1, C))],
        out_specs=_rows(tm, C),
        out_shape=jax.ShapeDtypeStruct((T, C), BF16),
        compiler_params=_cparams(("parallel",)),
    )(x, g)


def _rmsnorm_bwd(x, g, dy, res, *, name, tm=640, want_f32=True, want_bf16=True):
    T, C = x.shape
    has_res = res is not None

    def body(*refs):
        refs = list(refs)
        x_ref, g_ref, dy_ref = refs[:3]
        refs = refs[3:]
        r_ref = refs.pop(0) if has_res else None
        o32 = refs.pop(0) if want_f32 else None
        o16 = refs.pop(0) if want_bf16 else None
        dg_ref = refs.pop(0)

        @pl.when(pl.program_id(0) == 0)
        def _():
            dg_ref[...] = jnp.zeros_like(dg_ref)

        xv = x_ref[...]
        dyv = dy_ref[...].astype(F32)
        r = lax.rsqrt(jnp.mean(xv * xv, axis=-1, keepdims=True) + EPS)
        xn = xv * r
        dg_ref[...] += jnp.sum(dyv * xn, axis=0, keepdims=True)
        dxn = dyv * g_ref[...]
        dx = r * (dxn - xn * jnp.mean(dxn * xn, axis=-1, keepdims=True))
        if has_res:
            dx = dx + r_ref[...]
        if want_f32:
            o32[...] = dx
        if want_bf16:
            o16[...] = dx.astype(BF16)

    in_specs = [_rows(tm, C), _const((1, C)), _rows(tm, C)]
    args = [x, g, dy]
    if has_res:
        in_specs.append(_rows(tm, C))
        args.append(res)
    out_specs, out_shape = [], []
    if want_f32:
        out_specs.append(_rows(tm, C))
        out_shape.append(jax.ShapeDtypeStruct((T, C), F32))
    if want_bf16:
        out_specs.append(_rows(tm, C))
        out_shape.append(jax.ShapeDtypeStruct((T, C), BF16))
    out_specs.append(_const((1, C)))
    out_shape.append(jax.ShapeDtypeStruct((1, C), F32))
    return pl.pallas_call(
        body, name=name, grid=(T // tm,), in_specs=in_specs, out_specs=out_specs,
        out_shape=out_shape, compiler_params=_cparams(("arbitrary",)),
    )(*args)


def _gate_mix_fwd(um, bg, p_rnn, p_att, *, tm=320):
    T = um.shape[0]

    def body(um_ref, bg_ref, pr_ref, pa_ref, o_ref):
        g = _sigmoid(um_ref[...].astype(F32) + bg_ref[...])
        o_ref[...] = (g[:, :D] * pr_ref[...].astype(F32) + g[:, D:] * pa_ref[...].astype(F32)).astype(BF16)

    return pl.pallas_call(
        body, name="gate_mix_fwd", grid=(T // tm,),
        in_specs=[_rows(tm, 2 * D), _const((1, 2 * D)), _rows(tm, D), _rows(tm, D)],
        out_specs=_rows(tm, D),
        out_shape=jax.ShapeDtypeStruct((T, D), BF16),
        compiler_params=_cparams(("parallel",)),
    )(um, bg, p_rnn, p_att)


def _gate_mix_bwd(um, bg, p_rnn, p_att, dmixed, *, tm=320):
    T = um.shape[0]

    def body(um_ref, bg_ref, pr_ref, pa_ref, dm_ref, dpr_ref, dpa_ref, dum_ref, dbg_ref):
        @pl.when(pl.program_id(0) == 0)
        def _():
            dbg_ref[...] = jnp.zeros_like(dbg_ref)

        g = _sigmoid(um_ref[...].astype(F32) + bg_ref[...])
        g0, g1 = g[:, :D], g[:, D:]
        dm = dm_ref[...].astype(F32)
        dpr_ref[...] = (dm * g0).astype(BF16)
        dpa_ref[...] = (dm * g1).astype(BF16)
        d0 = dm * pr_ref[...].astype(F32) * g0 * (1.0 - g0)
        d1 = dm * pa_ref[...].astype(F32) * g1 * (1.0 - g1)
        dum_ref[:, :D] = d0.astype(BF16)
        dum_ref[:, D:] = d1.astype(BF16)
        dbg_ref[:, :D] += jnp.sum(d0, axis=0, keepdims=True)
        dbg_ref[:, D:] += jnp.sum(d1, axis=0, keepdims=True)

    return pl.pallas_call(
        body, name="gate_mix_bwd", grid=(T // tm,),
        in_specs=[_rows(tm, 2 * D), _const((1, 2 * D)), _rows(tm, D), _rows(tm, D), _rows(tm, D)],
        out_specs=[_rows(tm, D), _rows(tm, D), _rows(tm, 2 * D), _const((1, 2 * D))],
        out_shape=[jax.ShapeDtypeStruct((T, D), BF16), jax.ShapeDtypeStruct((T, D), BF16),
                   jax.ShapeDtypeStruct((T, 2 * D), BF16), jax.ShapeDtypeStruct((1, 2 * D), F32)],
        compiler_params=_cparams(("arbitrary",)),
    )(um, bg, p_rnn, p_att, dmixed)


def _swiglu_fwd(ff, *, tm=320):
    T = ff.shape[0]

    def body(g_ref, u_ref, o_ref):
        gv = g_ref[...].astype(F32)
        o_ref[...] = (gv * _sigmoid(gv) * u_ref[...].astype(F32)).astype(BF16)

    return pl.pallas_call(
        body, name="swiglu_fwd", grid=(T // tm,),
        in_specs=[_rows(tm, DFF, 0), _rows(tm, DFF, 1)],
        out_specs=_rows(tm, DFF),
        out_shape=jax.ShapeDtypeStruct((T, DFF), BF16),
        compiler_params=_cparams(("parallel",)),
    )(ff, ff)


def _swiglu_bwd(ff, dact, *, tm=320):
    T = ff.shape[0]

    def body(g_ref, u_ref, da_ref, o_ref):
        gv = g_ref[...].astype(F32)
        s = _sigmoid(gv)
        da = da_ref[...].astype(F32)
        o_ref[:, :DFF] = (da * u_ref[...].astype(F32) * s * (1.0 + gv * (1.0 - s))).astype(BF16)
        o_ref[:, DFF:] = (da * gv * s).astype(BF16)

    return pl.pallas_call(
        body, name="swiglu_bwd", grid=(T // tm,),
        in_specs=[_rows(tm, DFF, 0), _rows(tm, DFF, 1), _rows(tm, DFF)],
        out_specs=_rows(tm, 2 * DFF),
        out_shape=jax.ShapeDtypeStruct((T, 2 * DFF), BF16),
        compiler_params=_cparams(("parallel",)),
    )(ff, ff, dact)


def _loss_head(h2, tgt, g, *, tm=FRONT):
    T = h2.shape[0]
    front_blocks = FRONT // tm

    def body(h_ref, t_ref, g_ref, d32_ref, d16_ref, dg_ref, ls_ref):
        i = pl.program_id(0)

        @pl.when(i == 0)
        def _():
            dg_ref[...] = jnp.zeros_like(dg_ref)
            ls_ref[...] = jnp.zeros_like(ls_ref)

        xv = h_ref[...]
        r = lax.rsqrt(jnp.mean(xv * xv, axis=-1, keepdims=True) + EPS)
        xn = xv * r
        gv = g_ref[...]
        e = jnp.where(i >= front_blocks, xn * gv - t_ref[...], 0.0)
        ls_ref[...] += jnp.sum(e * e, axis=0, keepdims=True)
        dy = e * (1.0 / D)
        dg_ref[...] += jnp.sum(dy * xn, axis=0, keepdims=True)
        dxn = dy * gv
        dx = r * (dxn - xn * jnp.mean(dxn * xn, axis=-1, keepdims=True))
        d32_ref[...] = dx
        d16_ref[...] = dx.astype(BF16)

    return pl.pallas_call(
        body, name="loss_head", grid=(T // tm,),
        in_specs=[_rows(tm, D), pl.BlockSpec((tm, D), lambda i: (jnp.maximum(i - front_blocks, 0), 0)),
                  _const((1, D))],
        out_specs=[_rows(tm, D), _rows(tm, D), _const((1, D)), _const((1, D))],
        out_shape=[jax.ShapeDtypeStruct((T, D), F32), jax.ShapeDtypeStruct((T, D), BF16),
                   jax.ShapeDtypeStruct((1, D), F32), jax.ShapeDtypeStruct((1, D), F32)],
        compiler_params=_cparams(("arbitrary",)),
    )(h2, tgt, g)


def _scan_fwd(a, b, h_in):
    n = a.shape[0]
    row = lax.broadcasted_iota(jnp.int32, a.shape, 0)
    s = 1
    while s < n:
        if s % 8:
            a_sh = jnp.where(row >= s, pltpu.roll(a, s, 0), 1.0)
            b_sh = jnp.where(row >= s, pltpu.roll(b, s, 0), 0.0)
        else:
            a_sh = jnp.concatenate([jnp.ones((s, RB), F32), a[:n - s]], axis=0)
            b_sh = jnp.concatenate([jnp.zeros((s, RB), F32), b[:n - s]], axis=0)
        b = a * b_sh + b
        a = a * a_sh
        s *= 2
    return b + a * h_in


def _scan_rev(a, b, g_in):
    n = a.shape[0]
    row = lax.broadcasted_iota(jnp.int32, a.shape, 0)
    s = 1
    while s < n:
        if s % 8:
            a_sh = jnp.where(row < n - s, pltpu.roll(a, n - s, 0), 1.0)
            b_sh = jnp.where(row < n - s, pltpu.roll(b, n - s, 0), 0.0)
        else:
            a_sh = jnp.concatenate([a[s:], jnp.ones((s, RB), F32)], axis=0)
            b_sh = jnp.concatenate([b[s:], jnp.zeros((s, RB), F32)], axis=0)
        b = a * b_sh + b
        a = a * a_sh
        s *= 2
    return b + a * g_in


def _lru_gates(xc, wa, ba, wi, bi, lam):
    xcb = xc.astype(BF16)
    r = _sigmoid(jnp.dot(xcb, wa, preferred_element_type=F32) + ba)
    ig = _sigmoid(jnp.dot(xcb, wi, preferred_element_type=F32) + bi)
    log_sig = jnp.minimum(lam, 0.0) - jnp.log(1.0 + jnp.exp(-jnp.abs(lam)))
    log_a = LRU_C * r * log_sig
    a = jnp.exp(log_a)
    m2 = jnp.tanh(-log_a) * (1.0 + a * a)
    return r, ig, log_sig, a, m2 * lax.rsqrt(jnp.maximum(m2, 1e-37))


def _rnn_specs(tc, nblk_t, rev):
    def tmap(k):
        return (nblk_t - 1 - k) if rev else k

    hb = tc // 8
    blk = lambda off: pl.BlockSpec((tc, RB), lambda c, k: (tmap(k), c + off))
    halo = lambda off: pl.BlockSpec((8, RB), lambda c, k: (jnp.maximum(tmap(k) * hb - 1, 0), c + off))
    vec = pl.BlockSpec((1, RB), lambda c, k: (0, c))
    cwv = pl.BlockSpec((CW, RB), lambda c, k: (0, c))
    mat = pl.BlockSpec((None, RB, RB), lambda c, k: (c, 0, 0))
    return blk, halo, vec, cwv, mat


def _rnn_fwd(uxg, cw, cb, wa, ba, wi, bi, lam, *, tc=640):
    T = uxg.shape[0]
    nt = T // tc
    nsub = tc // SUB
    blk, halo, vec, cwv, mat = _rnn_specs(tc, nt, False)

    def body(x_ref, xh_ref, ug_ref, cw_ref, cb_ref, wa_ref, ba_ref, wi_ref, bi_ref, lam_ref,
             h_ref, y_ref, xb, hc):
        k = pl.program_id(1)

        @pl.when(k == 0)
        def _():
            hc[...] = jnp.zeros_like(hc)

        xb[0:8, :] = jnp.where(k > 0, xh_ref[...], 0.0)
        xb[8:, :] = x_ref[...]
        cwv_, cbv = cw_ref[...], cb_ref[...]
        wav, wiv = wa_ref[...], wi_ref[...]
        bav, biv, lamv = ba_ref[...], bi_ref[...], lam_ref[...]
        h_in = hc[0:1, :]
        for sc in range(nsub):
            r0 = sc * SUB
            xc = cbv + cwv_[0:1, :] * xb[pl.ds(5 + r0, SUB), :]
            for j in range(1, CW):
                xc = xc + cwv_[j:j + 1, :] * xb[pl.ds(5 + j + r0, SUB), :]
            r, ig, _, a, mm = _lru_gates(xc, wav, bav, wiv, biv, lamv)
            rows = k * tc + r0 + lax.broadcasted_iota(jnp.int32, (SUB, RB), 0)
            b = jnp.where(rows >= PAD, mm * (ig * xc), 0.0)
            h = _scan_fwd(a, b, h_in)
            h_in = h[SUB - 1:SUB, :]
            h_ref[pl.ds(r0, SUB), :] = h
            gl, _ = _gelu_parts(ug_ref[pl.ds(r0, SUB), :])
            y_ref[pl.ds(r0, SUB), :] = (h * gl).astype(BF16)
        hc[0:1, :] = h_in

    return pl.pallas_call(
        body, name="rnn_fwd", grid=(NBLK, nt),
        in_specs=[blk(0), halo(0), blk(NBLK), cwv, vec, mat, vec, mat, vec, vec],
        out_specs=[blk(0), blk(0)],
        out_shape=[jax.ShapeDtypeStruct((T, DR), F32), jax.ShapeDtypeStruct((T, DR), BF16)],
        scratch_shapes=[pltpu.VMEM((tc + 8, RB), F32), pltpu.VMEM((8, RB), F32)],
        compiler_params=_cparams(("parallel", "arbitrary")),
    )(uxg, uxg, uxg, cw, cb, wa, ba, wi, bi, lam)


def _rnn_bwd(uxg, hs, dy, cw, cb, wa, ba, wi, bi, lam, wat, wit, *, tc=640):
    T = uxg.shape[0]
    nt = T // tc
    nsub = tc // SUB
    blk, halo, vec, cwv, mat = _rnn_specs(tc, nt, True)

    def body(x_ref, xh_ref, ug_ref, h_ref, hh_ref, dy_ref, cw_ref, cb_ref, wa_ref, ba_ref, wi_ref,
             bi_ref, lam_ref, wat_ref, wit_ref,
             dux_ref, dug_ref, dcw_ref, dcb_ref, dwa_ref, dba_ref, dwi_ref, dbi_ref, dlam_ref,
             xb, hb, ab, dxb, xcs, rs, igs, mms, dgas, dgis, carry):
        k = pl.program_id(1)
        kt = nt - 1 - k

        @pl.when(k == 0)
        def _():
            carry[...] = jnp.zeros_like(carry)
            for ref in (dcw_ref, dcb_ref, dwa_ref, dba_ref, dwi_ref, dbi_ref, dlam_ref):
                ref[...] = jnp.zeros_like(ref)

        xb[0:8, :] = jnp.where(kt > 0, xh_ref[...], 0.0)
        xb[8:, :] = x_ref[...]
        hb[0:8, :] = jnp.where(kt > 0, hh_ref[...], 0.0)
        hb[8:, :] = h_ref[...]
        cwv_, cbv = cw_ref[...], cb_ref[...]
        wav, wiv = wa_ref[...], wi_ref[...]
        bav, biv, lamv = ba_ref[...], bi_ref[...], lam_ref[...]
        ab[tc:tc + 8, :] = jnp.broadcast_to(carry[1:2, :], (8, RB))
        dxb[tc:tc + 8, :] = carry[8:16, :]
        log_sig = None
        for sc in range(nsub):
            r0 = sc * SUB
            xc = cbv + cwv_[0:1, :] * xb[pl.ds(5 + r0, SUB), :]
            for j in range(1, CW):
                xc = xc + cwv_[j:j + 1, :] * xb[pl.ds(5 + j + r0, SUB), :]
            r, ig, log_sig, a, mm = _lru_gates(xc, wav, bav, wiv, biv, lamv)
            xcs[pl.ds(r0, SUB), :] = xc
            rs[pl.ds(r0, SUB), :] = r
            igs[pl.ds(r0, SUB), :] = ig
            mms[pl.ds(r0, SUB), :] = mm
            ab[pl.ds(r0, SUB), :] = a
        sig_neg = _sigmoid(-lamv)
        g_in = carry[0:1, :]
        dlam_acc = jnp.zeros((1, RB), F32)
        for sc in reversed(range(nsub)):
            r0 = sc * SUB
            xc, r, ig, mm = xcs[pl.ds(r0, SUB), :], rs[pl.ds(r0, SUB), :], igs[pl.ds(r0, SUB), :], mms[pl.ds(r0, SUB), :]
            a = ab[pl.ds(r0, SUB), :]
            a_next = ab[pl.ds(r0 + 1, SUB), :]
            hv = hb[pl.ds(8 + r0, SUB), :]
            hprev = hb[pl.ds(7 + r0, SUB), :]
            dyv = dy_ref[pl.ds(r0, SUB), :]
            gl, dgl = _gelu_parts(ug_ref[pl.ds(r0, SUB), :])
            dug_ref[pl.ds(r0, SUB), :] = (dyv * hv * dgl).astype(BF16)
            G = _scan_rev(a_next, dyv * gl, g_in)
            g_in = G[0:1, :]
            rows = kt * tc + r0 + lax.broadcasted_iota(jnp.int32, (SUB, RB), 0)
            db = jnp.where(rows >= PAD, G, 0.0)
            da = G * hprev
            dmm = db * (ig * xc)
            di = db * (mm * xc)
            dxc = db * (mm * ig)
            dlog_a = da * a - dmm * (a * a) / jnp.maximum(mm, 1e-30)
            dr = dlog_a * (LRU_C * log_sig)
            dlam_acc = dlam_acc + jnp.sum(dlog_a * (LRU_C * r), axis=0, keepdims=True)
            dga = dr * r * (1.0 - r)
            dgi = di * ig * (1.0 - ig)
            dgab, dgib = dga.astype(BF16), dgi.astype(BF16)
            dgas[pl.ds(r0, SUB), :] = dgab
            dgis[pl.ds(r0, SUB), :] = dgib
            dba_ref[...] += jnp.sum(dga, axis=0, keepdims=True)
            dbi_ref[...] += jnp.sum(dgi, axis=0, keepdims=True)
            dxc = dxc + jnp.dot(dgab, wat_ref[...], preferred_element_type=F32) \
                + jnp.dot(dgib, wit_ref[...], preferred_element_type=F32)
            dxb[pl.ds(r0, SUB), :] = dxc
        dlam_ref[...] += dlam_acc * sig_neg
        xcb = xcs[...].astype(BF16)
        tn = (((0,), (0,)), ((), ()))
        dwa_ref[...] += lax.dot_general(xcb, dgas[...], tn, preferred_element_type=F32)
        dwi_ref[...] += lax.dot_general(xcb, dgis[...], tn, preferred_element_type=F32)
        dxc_all = dxb[0:tc, :]
        dcb_ref[...] += jnp.sum(dxc_all, axis=0, keepdims=True)
        rows_all = kt * tc + lax.broadcasted_iota(jnp.int32, (tc, RB), 0)
        dux = jnp.zeros((tc, RB), F32)
        for j in range(CW):
            dcw_ref[j:j + 1, :] += jnp.sum(dxc_all * xb[pl.ds(5 + j, tc), :], axis=0, keepdims=True)
            dux = dux + cwv_[j:j + 1, :] * dxb[pl.ds(CW - 1 - j, tc), :]
        dux_ref[...] = jnp.where(rows_all >= PAD, dux, 0.0).astype(BF16)
        carry[0:1, :] = g_in
        carry[1:2, :] = ab[0:1, :]
        carry[8:16, :] = dxb[0:8, :]

    vec_out = pl.BlockSpec((1, RB), lambda c, k: (0, c))
    return pl.pallas_call(
        body, name="rnn_bwd", grid=(NBLK, nt),
        in_specs=[blk(0), halo(0), blk(NBLK), blk(0), halo(0), blk(0), cwv, vec, mat, vec, mat, vec, vec, mat, mat],
        out_specs=[blk(0), blk(0), cwv, vec_out, mat, vec_out, mat, vec_out, vec_out],
        out_shape=[jax.ShapeDtypeStruct((T, DR), BF16), jax.ShapeDtypeStruct((T, DR), BF16),
                   jax.ShapeDtypeStruct((CW, DR), F32), jax.ShapeDtypeStruct((1, DR), F32),
                   jax.ShapeDtypeStruct((NBLK, RB, RB), F32), jax.ShapeDtypeStruct((1, DR), F32),
                   jax.ShapeDtypeStruct((NBLK, RB, RB), F32), jax.ShapeDtypeStruct((1, DR), F32),
                   jax.ShapeDtypeStruct((1, DR), F32)],
        scratch_shapes=[pltpu.VMEM((tc + 8, RB), F32), pltpu.VMEM((tc + 8, RB), F32),
                        pltpu.VMEM((tc + 8, RB), F32), pltpu.VMEM((tc + 8, RB), F32),
                        pltpu.VMEM((tc, RB), F32), pltpu.VMEM((tc, RB), F32), pltpu.VMEM((tc, RB), F32),
                        pltpu.VMEM((tc, RB), F32), pltpu.VMEM((tc, RB), BF16), pltpu.VMEM((tc, RB), BF16),
                        pltpu.VMEM((16, RB), F32)],
        compiler_params=_cparams(("parallel", "arbitrary")),
    )(uxg, uxg, uxg, hs, hs, dy, cw, cb, wa, ba, wi, bi, lam, wat, wit)


def _attn_prep(q_all, kv_all, ukr, tab, *, tm=320):
    T = q_all.shape[0]

    def body(q_ref, kv_ref, kr_ref, tab_ref, qo_ref, ko_ref, vo_ref):
        tab_v = tab_ref[...]
        lane = lax.broadcasted_iota(jnp.int32, (tm, LANES), 1)
        t1 = kr_ref[...] * tab_v
        kro = jnp.where(lane < ROPE, t1 + pltpu.roll(t1, ROPE, 1), 0.0).astype(BF16)
        for h in range(NH):
            c0 = h * QW
            qo_ref[h, :, 0:NOPE] = (q_ref[:, c0:c0 + NOPE].astype(F32) * SCALE).astype(BF16)
            t2 = q_ref[:, c0 + NOPE:c0 + QW].astype(F32) * tab_v
            qo_ref[h, :, NOPE:QW] = ((t2 + pltpu.roll(t2, ROPE, 1)) * SCALE).astype(BF16)
            ko_ref[h, :, 0:NOPE] = kv_ref[:, c0:c0 + NOPE].astype(BF16)
            ko_ref[h, :, NOPE:QW] = kro
            vo_ref[h, :, :] = kv_ref[:, c0 + NOPE:c0 + QW].astype(BF16)

    return pl.pallas_call(
        body, name="attn_prep", grid=(T // tm,),
        in_specs=[_rows(tm, NH * QW), _rows(tm, NH * QW), _rows(tm, LANES), _rows(tm, LANES)],
        out_specs=[pl.BlockSpec((NH, tm, QW), lambda i: (0, i, 0)), pl.BlockSpec((NH, tm, QW), lambda i: (0, i, 0)),
                   pl.BlockSpec((NH, tm, VD), lambda i: (0, i, 0))],
        out_shape=[jax.ShapeDtypeStruct((NH, T, QW), BF16), jax.ShapeDtypeStruct((NH, T, QW), BF16),
                   jax.ShapeDtypeStruct((NH, T, VD), BF16)],
        compiler_params=_cparams(("parallel",)),
    )(q_all, kv_all, ukr, tab)


def _attn_prep_bwd(dq, dk, dv, tab, *, tm=320):
    T = dq.shape[1]

    def body(dq_ref, dk_ref, dv_ref, tab_ref, dqa_ref, dkva_ref, dkr_ref):
        tab_v = tab_ref[...]
        lane = lax.broadcasted_iota(jnp.int32, (tm, LANES), 1)
        dkro = jnp.zeros((tm, LANES), F32)
        for h in range(NH):
            c0 = h * QW
            dqa_ref[:, c0:c0 + NOPE] = (dq_ref[h, :, 0:NOPE] * SCALE).astype(BF16)
            d2 = dq_ref[h, :, NOPE:QW]
            dqa_ref[:, c0 + NOPE:c0 + QW] = ((d2 + pltpu.roll(d2, ROPE, 1)) * tab_v * SCALE).astype(BF16)
            dkva_ref[:, c0:c0 + NOPE] = dk_ref[h, :, 0:NOPE].astype(BF16)
            dkva_ref[:, c0 + NOPE:c0 + QW] = dv_ref[h, :, :].astype(BF16)
            dkro = dkro + dk_ref[h, :, NOPE:QW]
        dkro = jnp.where(lane < ROPE, dkro, 0.0)
        dkr_ref[...] = ((dkro + pltpu.roll(dkro, ROPE, 1)) * tab_v).astype(BF16)

    return pl.pallas_call(
        body, name="attn_prep_bwd", grid=(T // tm,),
        in_specs=[pl.BlockSpec((NH, tm, QW), lambda i: (0, i, 0)), pl.BlockSpec((NH, tm, QW), lambda i: (0, i, 0)),
                  pl.BlockSpec((NH, tm, VD), lambda i: (0, i, 0)), _rows(tm, LANES)],
        out_specs=[_rows(tm, NH * QW), _rows(tm, NH * QW), _rows(tm, LANES)],
        out_shape=[jax.ShapeDtypeStruct((T, NH * QW), BF16), jax.ShapeDtypeStruct((T, NH * QW), BF16),
                   jax.ShapeDtypeStruct((T, LANES), BF16)],
        compiler_params=_cparams(("parallel",)),
    )(dq, dk, dv, tab)


def _visible(q0, k0, nq, nk):
    rows = q0 + lax.broadcasted_iota(jnp.int32, (nq, nk), 0)
    cols = k0 + lax.broadcasted_iota(jnp.int32, (nq, nk), 1)
    return ((cols >> 6) <= (rows >> 6)) & (cols >= PAD)


def _visible_t(q0, k0, nq, nk):
    cols = k0 + lax.broadcasted_iota(jnp.int32, (nk, nq), 0)
    rows = q0 + lax.broadcasted_iota(jnp.int32, (nk, nq), 1)
    return ((cols >> 6) <= (rows >> 6)) & (cols >= PAD)


_NT = (((1,), (1,)), ((), ()))
ATTN_BLOCK = 1664


def _attn_block(T):
    return ATTN_BLOCK if T % ATTN_BLOCK == 0 else 640


def _round_up(n, m):
    return -(-n // m) * m


def _flash_fwd(q, k, v, *, gather=(), bq=None):
    T = q.shape[1]
    bq = bq or _attn_block(T)
    nq = T // bq
    rs = bq // CHAINS
    n = len(gather)

    def body(*refs):
        q_ref, k_ref, v_ref = refs[:3]
        g_src = refs[3:3 + n]
        o_ref, lse_ref = refs[3 + n:5 + n]
        g_out = refs[5 + n:5 + 2 * n]
        scr = refs[5 + 2 * n:]
        m_s, l_s, acc_s = scr[:CHAINS], scr[CHAINS:2 * CHAINS], scr[2 * CHAINS:3 * CHAINS]
        g_scr = scr[3 * CHAINS:]
        h = pl.program_id(0)
        i = pl.program_id(1)
        if n:
            @pl.when((h == 0) & (i == 0))
            def _():
                _gather_start(_gather_descs(g_src, g_out, g_scr))

        for r in range(CHAINS):
            m_s[r][...] = jnp.full_like(m_s[r], NEG)
            l_s[r][...] = jnp.zeros_like(l_s[r])
            acc_s[r][...] = jnp.zeros_like(acc_s[r])

        def step(j, masked, diag):
            off = pl.multiple_of(j * bq, bq)
            for r in range(CHAINS):
                rows = pl.ds(r * rs, rs)
                kw = min(bq, _round_up((r + 1) * rs, LANES)) if diag else bq
                kv_ = k_ref[pl.ds(off, kw), :]
                vv = v_ref[pl.ds(off, kw), :]
                s = lax.dot_general(q_ref[rows, :], kv_, _NT, preferred_element_type=F32)
                if masked:
                    s = jnp.where(_visible(i * bq + r * rs, j * bq, rs, kw), s, NEG)
                m_prev = m_s[r][...]
                m_new = jnp.maximum(m_prev, jnp.max(s, axis=-1, keepdims=True))
                p = jnp.exp(s - m_new)
                alpha = jnp.exp(m_prev - m_new)
                l_s[r][...] = alpha * l_s[r][...] + jnp.sum(p, axis=-1, keepdims=True)
                acc_s[r][...] = alpha * acc_s[r][...] + jnp.dot(p.astype(BF16), vv, preferred_element_type=F32)
                m_s[r][...] = m_new

        @pl.when(i == 0)
        def _():
            step(0, True, True)

        @pl.when(i > 0)
        def _():
            step(0, True, False)

            def loop(j, c):
                step(j, False, False)
                return c

            lax.fori_loop(1, i, loop, 0)
            step(i, True, True)

        for r in range(CHAINS):
            rows = pl.ds(r * rs, rs)
            o_ref[rows, :] = (acc_s[r][...] / l_s[r][...]).astype(BF16)
            lse_ref[rows, :] = m_s[r][...] + jnp.log(l_s[r][...])

        if n:
            @pl.when((h == NH - 1) & (i == nq - 1))
            def _():
                _gather_wait(_gather_descs(g_src, g_out, g_scr, with_loads=False))

    return pl.pallas_call(
        body, name="flash_fwd", grid=(NH, nq),
        in_specs=[pl.BlockSpec((None, bq, QW), lambda h, i: (h, i, 0)),
                  pl.BlockSpec((None, T, QW), lambda h, i: (h, 0, 0)),
                  pl.BlockSpec((None, T, VD), lambda h, i: (h, 0, 0))] + [HBM] * n,
        out_specs=[pl.BlockSpec((bq, VD), lambda h, i: (i, h)),
                   pl.BlockSpec((None, bq, 1), lambda h, i: (h, i, 0))] + [HBM] * n,
        out_shape=[jax.ShapeDtypeStruct((T, NH * VD), BF16), jax.ShapeDtypeStruct((NH, T, 1), F32)]
        + [jax.ShapeDtypeStruct((4,) + g.shape, g.dtype) for g in gather],
        scratch_shapes=[pltpu.VMEM((rs, 1), F32)] * (2 * CHAINS) + [pltpu.VMEM((rs, VD), F32)] * CHAINS
        + (_gather_scratch(gather) if n else []),
        compiler_params=_cparams(("arbitrary", "arbitrary")),
    )(q, k, v, *gather)


def _attn_delta(o, do, *, tm=640):
    T = o.shape[0]

    def body(o_ref, do_ref, d_ref):
        prod = o_ref[...].astype(F32) * do_ref[...].astype(F32)
        for h in range(NH):
            d_ref[h, :, :] = jnp.sum(prod[:, h * VD:(h + 1) * VD], axis=-1, keepdims=True)

    return pl.pallas_call(
        body, name="attn_delta", grid=(T // tm,),
        in_specs=[_rows(tm, NH * VD), _rows(tm, NH * VD)],
        out_specs=pl.BlockSpec((NH, tm, 1), lambda i: (0, i, 0)),
        out_shape=jax.ShapeDtypeStruct((NH, T, 1), F32),
        compiler_params=_cparams(("parallel",)),
    )(o, do)


_TN = (((0,), (0,)), ((), ()))


def _flash_bwd(q, k, v, do, lse_row, delta_row, *, scatter=(), bq=None):
    T = q.shape[1]
    bq = bq or _attn_block(T)
    nq = T // bq
    rs = bq // CHAINS
    n = len(scatter)

    def body(*refs):
        q_ref, k_ref, v_ref, do_ref, lse_ref, dl_ref = refs[:6]
        s_src = refs[6:6 + n]
        dq_ref, dk_ref, dv_ref = refs[6 + n:9 + n]
        s_out = refs[9 + n:9 + 2 * n]
        s_scr = refs[9 + 2 * n:]
        h = pl.program_id(0)
        j = pl.program_id(1)
        if n:
            @pl.when((h == 0) & (j == 0))
            def _():
                for cp in _scatter_descs(s_src, s_out, s_scr):
                    cp.start()

        @pl.when(j == 0)
        def _():
            dq_ref[...] = jnp.zeros_like(dq_ref)

        dk_ref[...] = jnp.zeros_like(dk_ref)
        dv_ref[...] = jnp.zeros_like(dv_ref)

        def step(i, masked, diag):
            for r in range(CHAINS):
                rows = pl.ds(r * rs, rs)
                q0 = (r * rs) // LANES * LANES if diag else 0
                qn = bq - q0
                off = pl.multiple_of(i * bq + q0, LANES)
                qv = q_ref[pl.ds(off, qn), :]
                dov = do_ref[pl.ds(off, qn), :]
                lse_v = lse_ref[:, pl.ds(off, qn)]
                dl_v = dl_ref[:, pl.ds(off, qn)]
                st = lax.dot_general(k_ref[rows, :], qv, _NT, preferred_element_type=F32)
                if masked:
                    st = jnp.where(_visible_t(i * bq + q0, j * bq + r * rs, qn, rs), st, NEG)
                pt = jnp.exp(st - lse_v)
                dv_ref[rows, :] += jnp.dot(pt.astype(BF16), dov, preferred_element_type=F32)
                dpt = lax.dot_general(v_ref[rows, :], dov, _NT, preferred_element_type=F32)
                dst = (pt * (dpt - dl_v)).astype(BF16)
                dk_ref[rows, :] += jnp.dot(dst, qv, preferred_element_type=F32)
                dq_ref[pl.ds(off, qn), :] += lax.dot_general(dst, k_ref[rows, :], _TN,
                                                             preferred_element_type=F32)

        step(j, True, True)

        @pl.when(j == 0)
        def _():
            def loop(i, c):
                step(i, True, False)
                return c
            lax.fori_loop(1, nq, loop, 0)

        @pl.when(j > 0)
        def _():
            def loop(i, c):
                step(i, False, False)
                return c
            lax.fori_loop(j + 1, nq, loop, 0)

        if n:
            @pl.when((h == NH - 1) & (j == nq - 1))
            def _():
                for cp in _scatter_descs(s_src, s_out, s_scr):
                    cp.wait()

    return pl.pallas_call(
        body, name="flash_bwd", grid=(NH, nq),
        in_specs=[pl.BlockSpec((None, T, QW), lambda h, j: (h, 0, 0)),
                  pl.BlockSpec((None, bq, QW), lambda h, j: (h, j, 0)),
                  pl.BlockSpec((None, bq, VD), lambda h, j: (h, j, 0)),
                  pl.BlockSpec((T, VD), lambda h, j: (0, h)),
                  pl.BlockSpec((None, 1, T), lambda h, j: (h, 0, 0)),
                  pl.BlockSpec((None, 1, T), lambda h, j: (h, 0, 0))] + [HBM] * n,
        out_specs=[pl.BlockSpec((None, T, QW), lambda h, j: (h, 0, 0)),
                   pl.BlockSpec((None, bq, QW), lambda h, j: (h, j, 0)),
                   pl.BlockSpec((None, bq, VD), lambda h, j: (h, j, 0))] + [HBM] * n,
        out_shape=[jax.ShapeDtypeStruct((NH, T, QW), F32), jax.ShapeDtypeStruct((NH, T, QW), F32),
                   jax.ShapeDtypeStruct((NH, T, VD), F32)]
        + [jax.ShapeDtypeStruct((3,) + s.shape[1:], s.dtype) for s in scatter],
        scratch_shapes=[_dma_sems(3 * n), _dma_sems(3 * n)] if n else [],
        compiler_params=_cparams(("arbitrary", "arbitrary")),
    )(q, k, v, do, lse_row, delta_row, *scatter)


def _rope_table(T):
    pos = (jnp.arange(T, dtype=jnp.int32) - PAD).astype(F32)
    inv_freq = ROPE_THETA ** (-jnp.arange(0, ROPE, 2, dtype=F32) / ROPE)
    ang = pos[:, None] * inv_freq[None, :]
    cos, sin = jnp.cos(ang), jnp.sin(ang)
    return jnp.concatenate([cos, cos, -sin, sin], axis=1)


def _swap_halves(w):
    return jnp.concatenate([w[..., ROPE // 2:], w[..., :ROPE // 2]], axis=-1)


O_UX, O_UG, O_UQ, O_UKV, O_UKR, O_UM = 0, DR, 2 * DR, 2 * DR + QR, 2 * DR + QR + KVR, 2 * DR + QR + KVR + ROPE


def _prep_weights(w):
    b = lambda a: a.astype(BF16)
    w_in = w["w_in"]
    kr = w_in[:, O_UKR:O_UM]
    p = {
        "w_xg": b(w_in[:, :O_UQ]),
        "w_q": b(w_in[:, O_UQ:O_UKV]),
        "w_kv": b(w_in[:, O_UKV:O_UKR]),
        "w_kr": b(jnp.concatenate([kr, _swap_halves(kr)], axis=1)),
        "w_m": b(w_in[:, O_UM:]),
    }
    wq = w["w_uq"].reshape(QR, NH, NOPE + ROPE)
    p["w_uq"] = b(jnp.concatenate([wq, _swap_halves(wq[..., NOPE:])], axis=-1).reshape(QR, NH * QW))
    p["w_ukv"] = b(w["w_ukv"])
    for n in ("w_xg", "w_q", "w_kv", "w_kr", "w_m", "w_uq", "w_ukv"):
        p[n + "_t"] = p[n].T
    p["wa"] = b(w["w_rec_a"])
    p["wi"] = b(w["w_rec_i"])
    p["wa_t"] = jnp.swapaxes(p["wa"], 1, 2)
    p["wi_t"] = jnp.swapaxes(p["wi"], 1, 2)
    return p


def _prep_late_weights(w):
    b = lambda a: a.astype(BF16)
    p = {"w_br": b(w["w_branch"][:DR]), "w_ba": b(w["w_branch"][DR:]), "w_out": b(w["w_out"]),
         "w_fi": b(w["w_ffn_in"]), "w_fo": b(w["w_ffn_out"])}
    for n in tuple(p):
        p[n + "_t"] = p[n].T
    return p


LATE = ("w_branch", "w_out", "w_ffn_in", "w_ffn_out")


def _local_step(x, tgt, w, late=None, reduce_first=None):
    S = x.shape[0]
    T = FRONT + S
    p = _prep_weights(w)
    tab = _rope_table(T)
    h0 = jnp.concatenate([jnp.zeros((PAD, D), F32), w["meta_tokens"], x], axis=0)
    row = lambda v: v.reshape(1, -1)

    z = _rmsnorm_fwd(h0, row(w["norm_mix_g"]), name="norm_mix")
    uxg = _mm(z, p["w_xg"], name="mm_uxg")
    uq = _mm(z, p["w_q"], name="mm_uq")
    ukv = _mm(z, p["w_kv"], name="mm_ukv")
    ukr = _mm(z, p["w_kr"], name="mm_ukr")
    um = _mm(z, p["w_m"], name="mm_um", out_dtype=BF16)
    rnn_w = (w["conv_w"], row(w["conv_b"]), p["wa"], row(w["b_rec_a"]), p["wi"], row(w["b_rec_i"]),
             row(w["lru_lambda"]))
    hs, y_rnn = _rnn_fwd(uxg, *rnn_w)
    qn = _rmsnorm_fwd(uq, row(w["q_norm_g"]), name="norm_q")
    kvn = _rmsnorm_fwd(ukv, row(w["kv_norm_g"]), name="norm_kv")
    q_all = _mm(qn, p["w_uq"], name="mm_q", out_dtype=BF16)
    kv_all = _mm(kvn, p["w_ukv"], name="mm_kv", out_dtype=BF16)
    qh, kh, vh = _attn_prep(q_all, kv_all, ukr, tab)
    y_att, lse, *stacks = _flash_fwd(qh, kh, vh, gather=late[0] if late else ())
    if late:
        w = {**w, **late[1](stacks)}
    p.update(_prep_late_weights(w))
    p_rnn = _mm(y_rnn, p["w_br"], name="mm_prnn", out_dtype=BF16)
    p_att = _mm(y_att, p["w_ba"], name="mm_patt", out_dtype=BF16)
    bg = row(w["b_gate"])
    mixed = _gate_mix_fwd(um, bg, p_rnn, p_att)
    h1 = _mm(mixed, p["w_out"], name="mm_out", res=h0)
    zf = _rmsnorm_fwd(h1, row(w["norm_ffn_g"]), name="norm_ffn")
    ff = _mm(zf, p["w_fi"], name="mm_ffn_in", out_dtype=BF16)
    act = _swiglu_fwd(ff)
    h2 = _mm(act, p["w_fo"], name="mm_ffn_out", res=h1)

    g = {}
    dh2, dh2b, dg_fin, lsum = _loss_head(h2, tgt, row(w["final_norm_g"]))
    loss = 0.5 * jnp.sum(lsum) / D
    g["final_norm_g"] = dg_fin.reshape(-1)
    dact = _mm(dh2b, p["w_fo_t"], name="mm_dact", out_dtype=BF16)
    g["w_ffn_out"] = _mm_tn(act, dh2b, name="mm_dw_ffn_out")
    dff = _swiglu_bwd(ff, dact)
    dzf = _mm(dff, p["w_fi_t"], name="mm_dzf")
    g["w_ffn_in"] = _mm_tn(zf, dff, name="mm_dw_ffn_in")
    dh1, dh1b, dg = _rmsnorm_bwd(h1, row(w["norm_ffn_g"]), dzf, dh2, name="norm_ffn_bwd")
    g["norm_ffn_g"] = dg
    dmixed = _mm(dh1b, p["w_out_t"], name="mm_dmixed", out_dtype=BF16)
    g["w_out"] = _mm_tn(mixed, dh1b, name="mm_dw_out")
    dp_rnn, dp_att, dum, dbg = _gate_mix_bwd(um, bg, p_rnn, p_att, dmixed)
    g["b_gate"] = dbg.reshape(2, D)
    dy_rnn = _mm(dp_rnn, p["w_br_t"], name="mm_dy_rnn")
    dy_att = _mm(dp_att, p["w_ba_t"], name="mm_dy_att", out_dtype=BF16)
    g["w_branch"] = jnp.concatenate([_mm_tn(y_rnn, dp_rnn, name="mm_dw_br"),
                                     _mm_tn(y_att, dp_att, name="mm_dw_ba")], axis=0)
    delta = _attn_delta(y_att, dy_att)
    first = reduce_first({n: g[n] for n in LATE}) if reduce_first else ()
    dq, dk, dv, *received = _flash_bwd(qh, kh, vh, dy_att, lse.reshape(NH, 1, T), delta.reshape(NH, 1, T),
                                       scatter=first)
    dq_all, dkv_all, dukr = _attn_prep_bwd(dq, dk, dv, tab)
    dqn = _mm(dq_all, p["w_uq_t"], name="mm_dqn")
    dkvn = _mm(dkv_all, p["w_ukv_t"], name="mm_dkvn")
    dwq = _mm_tn(qn, dq_all, name="mm_dw_uq").reshape(QR, NH, QW)
    dwq_rope = dwq[..., NOPE:NOPE + ROPE] + _swap_halves(dwq[..., NOPE + ROPE:])
    g["w_uq"] = jnp.concatenate([dwq[..., :NOPE], dwq_rope], axis=-1).reshape(QR, NH * (NOPE + ROPE))
    g["w_ukv"] = _mm_tn(kvn, dkv_all, name="mm_dw_ukv")
    duq, dg = _rmsnorm_bwd(uq, row(w["q_norm_g"]), dqn, None, name="norm_q_bwd", want_f32=False)
    g["q_norm_g"] = dg
    dukv, dg = _rmsnorm_bwd(ukv, row(w["kv_norm_g"]), dkvn, None, name="norm_kv_bwd", want_f32=False)
    g["kv_norm_g"] = dg
    (dux, dug, g["conv_w"], g["conv_b"], g["w_rec_a"], g["b_rec_a"], g["w_rec_i"], g["b_rec_i"],
     g["lru_lambda"]) = _rnn_bwd(uxg, hs, dy_rnn, *rnn_w, p["wa_t"], p["wi_t"])
    dz = _mm_sum([(dux, p["w_xg_t"][:DR]), (dug, p["w_xg_t"][DR:]), (duq, p["w_q_t"]), (dukv, p["w_kv_t"]),
                  (dukr, p["w_kr_t"]), (dum, p["w_m_t"])], name="mm_dz")
    dwkr = _mm_tn(z, dukr, name="mm_dw_kr")
    g["w_in"] = jnp.concatenate([
        _mm_tn(z, dux, name="mm_dw_x"), _mm_tn(z, dug, name="mm_dw_g"),
        _mm_tn(z, duq, name="mm_dw_q"), _mm_tn(z, dukv, name="mm_dw_kv"),
        dwkr[:, :ROPE] + _swap_halves(dwkr[:, ROPE:]),
        _mm_tn(z, dum, name="mm_dw_m")], axis=1)
    dh0, dg = _rmsnorm_bwd(h0, row(w["norm_mix_g"]), dz, dh1, name="norm_mix_bwd", want_bf16=False)
    g["norm_mix_g"] = dg
    g["meta_tokens"] = dh0[PAD:FRONT]
    return loss, dh0[FRONT:], g, (first, received)


HBM = pl.BlockSpec(memory_space=pltpu.HBM)
CHIP_FLIPS = ((1, 0), (0, 1), (1, 1))


def _place():
    return lax.axis_index("x"), lax.axis_index("y"), lax.axis_index("c")


def _flip(v, f):
    return 1 - v if f else v


def _dma_sems(n):
    return pltpu.SemaphoreType.DMA((n,))


def _gather_scratch(srcs):
    n = len(srcs)
    return [pltpu.VMEM(s.shape, s.dtype) for s in srcs] + [_dma_sems(3 * n), _dma_sems(3 * n), _dma_sems(n),
                                                            _dma_sems(n)]


def _gather_descs(src_refs, out_refs, scr, with_loads=True):
    n = len(src_refs)
    stage = scr[:n]
    send_sems, recv_sems, in_sems, local_sems = scr[n:]
    x, y, c = _place()
    me = 2 * x + y
    loads, sends, local = [], [], []
    for a in range(n):
        if with_loads:
            loads.append(pltpu.make_async_copy(src_refs[a], stage[a], in_sems.at[a]))
        for k, (fx, fy) in enumerate(CHIP_FLIPS):
            sends.append(pltpu.make_async_remote_copy(
                src_ref=stage[a], dst_ref=out_refs[a].at[me], send_sem=send_sems.at[3 * a + k],
                recv_sem=recv_sems.at[3 * a + k], device_id=(_flip(x, fx), _flip(y, fy), c),
                device_id_type=MESH))
        local.append(pltpu.make_async_copy(stage[a], out_refs[a].at[me], local_sems.at[a]))
    return loads, sends, local


def _gather_start(descs):
    loads, sends, local = descs
    for cp in loads:
        cp.start()
    for a, cp in enumerate(loads):
        cp.wait()
        for s in sends[3 * a:3 * a + 3]:
            s.start()
        local[a].start()


def _gather_wait(descs):
    _, sends, local = descs
    for cp in sends + local:
        cp.wait()


def _allgather_chips(srcs, *, name):
    n = len(srcs)

    def body(*refs):
        descs = _gather_descs(refs[:n], refs[n:2 * n], refs[2 * n:])
        _gather_start(descs)
        _gather_wait(descs)

    return pl.pallas_call(
        body, name=name, in_specs=[HBM] * n, out_specs=[HBM] * n,
        out_shape=[jax.ShapeDtypeStruct((4,) + s.shape, s.dtype) for s in srcs],
        scratch_shapes=_gather_scratch(srcs),
        compiler_params=pltpu.CompilerParams(vmem_limit_bytes=VMEM_LIMIT),
    )(*srcs)


def _scatter_descs(src_refs, out_refs, scr):
    send_sems, recv_sems = scr
    x, y, c = _place()
    copies = []
    for a in range(len(src_refs)):
        for k, (fx, fy) in enumerate(CHIP_FLIPS):
            px, py = _flip(x, fx), _flip(y, fy)
            copies.append(pltpu.make_async_remote_copy(
                src_ref=src_refs[a].at[2 * px + py], dst_ref=out_refs[a].at[k],
                send_sem=send_sems.at[3 * a + k], recv_sem=recv_sems.at[3 * a + k],
                device_id=(px, py, c), device_id_type=MESH))
    return copies


def _scatter_chips(srcs, *, name):
    n = len(srcs)

    def body(*refs):
        copies = _scatter_descs(refs[:n], refs[n:2 * n], refs[2 * n:])
        for cp in copies:
            cp.start()
        for cp in copies:
            cp.wait()

    return pl.pallas_call(
        body, name=name, in_specs=[HBM] * n, out_specs=[HBM] * n,
        out_shape=[jax.ShapeDtypeStruct((3,) + s.shape[1:], s.dtype) for s in srcs],
        scratch_shapes=[_dma_sems(3 * n), _dma_sems(3 * n)],
    )(*srcs)


def _sibling_take(srcs, *, name):
    n = len(srcs)

    def body(*refs):
        src_refs, out_refs = refs[:n], refs[n:2 * n]
        send_sems, recv_sems = refs[2 * n:]
        x, y, c = _place()
        copies = []
        for a in range(n):
            h = srcs[a].shape[1] // 2
            theirs = pl.ds(pl.multiple_of((1 - c) * h, 8), h)
            cp = pltpu.make_async_remote_copy(
                src_ref=src_refs[a].at[:, theirs, :], dst_ref=out_refs[a], send_sem=send_sems.at[a],
                recv_sem=recv_sems.at[a], device_id=(x, y, 1 - c), device_id_type=MESH)
            cp.start()
            copies.append(cp)
        for cp in copies:
            cp.wait()

    return pl.pallas_call(
        body, name=name, in_specs=[HBM] * n, out_specs=[HBM] * n,
        out_shape=[jax.ShapeDtypeStruct((4, s.shape[1] // 2, s.shape[2]), s.dtype) for s in srcs],
        scratch_shapes=[_dma_sems(n), _dma_sems(n)],
    )(*srcs)


def _sibling_swap(srcs, *, name):
    n = len(srcs)

    def body(*refs):
        src_refs, out_refs = refs[:n], refs[n:2 * n]
        send_sems, recv_sems = refs[2 * n:]
        x, y, c = _place()
        copies = []
        for a in range(n):
            cp = pltpu.make_async_remote_copy(
                src_ref=src_refs[a], dst_ref=out_refs[a], send_sem=send_sems.at[a],
                recv_sem=recv_sems.at[a], device_id=(x, y, 1 - c), device_id_type=MESH)
            cp.start()
            copies.append(cp)
        for cp in copies:
            cp.wait()

    return pl.pallas_call(
        body, name=name, in_specs=[HBM] * n, out_specs=[HBM] * n,
        out_shape=[jax.ShapeDtypeStruct(s.shape, s.dtype) for s in srcs],
        scratch_shapes=[_dma_sems(n), _dma_sems(n)],
    )(*srcs)


def _row_tile(rows, cols, n_arrays, step=16):
    budget = 24 * 1024 * 1024 // (2 * 4 * n_arrays * cols)
    best = step
    for t in range(step, rows + 1, step):
        if rows % t == 0 and t <= budget:
            best = t
    assert rows % best == 0, (rows, cols)
    return best


def _add_halves(full, theirs, core, wire, *, name):
    _, h, c = theirs.shape
    tm = _row_tile(h, c, 3)
    nb = h // tm

    def body(core_ref, a_ref, b_ref, o_ref):
        o_ref[...] = (a_ref[...] + b_ref[...]).astype(wire)

    spec = pl.BlockSpec((None, tm, c), lambda s, i, core_ref: (s, i, 0))
    grid_spec = pltpu.PrefetchScalarGridSpec(
        num_scalar_prefetch=1, grid=(4, nb),
        in_specs=[pl.BlockSpec((None, tm, c), lambda s, i, core_ref: (s, core_ref[0] * nb + i, 0)), spec],
        out_specs=spec)
    return pl.pallas_call(
        body, name=name, grid_spec=grid_spec, out_shape=jax.ShapeDtypeStruct(theirs.shape, wire),
        compiler_params=_cparams(("parallel", "parallel")),
    )(core.reshape(1), full, theirs)


def _sum4(own, recv, *, name):
    h, c = own.shape
    tm = _row_tile(h, c, 5)

    def body(o_ref, r_ref, out_ref):
        f = lambda k: r_ref[k].astype(F32)
        out_ref[...] = ((o_ref[...].astype(F32) + f(0)) + f(1)) + f(2)

    return pl.pallas_call(
        body, name=name, grid=(h // tm,),
        in_specs=[_rows(tm, c), pl.BlockSpec((3, tm, c), lambda i: (0, i, 0))],
        out_specs=_rows(tm, c), out_shape=jax.ShapeDtypeStruct((h, c), F32),
        compiler_params=_cparams(("parallel",)),
    )(own, recv)


def _adamw(g, w, m, v, *, name):
    r, c = g.shape
    tm = _row_tile(r, c, 7, step=8)
    c1 = 1.0 / (1.0 - ADAM_B1 ** ADAM_STEP)
    c2 = 1.0 / (1.0 - ADAM_B2 ** ADAM_STEP)

    def body(g_ref, w_ref, m_ref, v_ref, d_ref, nm_ref, nv_ref):
        gv = g_ref[...]
        nm = ADAM_B1 * m_ref[...] + (1.0 - ADAM_B1) * gv
        nv = ADAM_B2 * v_ref[...] + (1.0 - ADAM_B2) * (gv * gv)
        nm_ref[...] = nm
        nv_ref[...] = nv
        d_ref[...] = -ADAM_LR * ((nm * c1) / (jnp.sqrt(nv * c2) + ADAM_EPS) + ADAM_WD * w_ref[...])

    spec = _rows(tm, c)
    shape = jax.ShapeDtypeStruct((r, c), F32)
    return pl.pallas_call(
        body, name=name, grid=(r // tm,), in_specs=[spec] * 4, out_specs=[spec] * 3,
        out_shape=[shape] * 3, compiler_params=_cparams(("parallel",)),
    )(g, w, m, v)


def _adamw_halves(mine, theirs, core, w, m, v, *, name):
    h, c = mine.shape
    tm = _row_tile(h, c, 10, step=8)
    nb = h // tm
    c1 = 1.0 / (1.0 - ADAM_B1 ** ADAM_STEP)
    c2 = 1.0 / (1.0 - ADAM_B2 ** ADAM_STEP)

    def body(core_ref, a_ref, b_ref, w_ref, m_ref, v_ref, g_ref, d_ref, nm_ref, nv_ref):
        gv = jnp.where(pl.program_id(0) // nb == core_ref[0], a_ref[...], b_ref[...])
        nm = ADAM_B1 * m_ref[...] + (1.0 - ADAM_B1) * gv
        nv = ADAM_B2 * v_ref[...] + (1.0 - ADAM_B2) * (gv * gv)
        g_ref[...] = gv
        nm_ref[...] = nm
        nv_ref[...] = nv
        d_ref[...] = -ADAM_LR * ((nm * c1) / (jnp.sqrt(nv * c2) + ADAM_EPS) + ADAM_WD * w_ref[...])

    half = pl.BlockSpec((tm, c), lambda i, core_ref: (i % nb, 0))
    spec = pl.BlockSpec((tm, c), lambda i, core_ref: (i, 0))
    grid_spec = pltpu.PrefetchScalarGridSpec(
        num_scalar_prefetch=1, grid=(2 * nb,), in_specs=[half, half, spec, spec, spec], out_specs=[spec] * 4)
    return pl.pallas_call(
        body, name=name, grid_spec=grid_spec, out_shape=[jax.ShapeDtypeStruct((2 * h, c), F32)] * 4,
        compiler_params=_cparams(("parallel",)),
    )(core.reshape(1), mine, theirs, w, m, v)


BIG = (("w_in", (D, 1328), 1), ("w_uq", (QR, 384), 1), ("w_ukv", (KVR, 512), 1), ("w_branch", (576, D), 0),
       ("w_out", (256, D), 0), ("w_ffn_in", (D, 1408), 1), ("w_ffn_out", (704, D), 0))
SMALL = (("meta_tokens", (NMETA, 256), 1), ("b_gate", (2, 256), 1), ("conv_w", (CW, 320), 1))
REPL = (("norm_mix_g", (D,)), ("conv_b", (DR,)), ("w_rec_a", (NBLK, RB, RB)), ("b_rec_a", (DR,)),
        ("w_rec_i", (NBLK, RB, RB)), ("b_rec_i", (DR,)), ("lru_lambda", (DR,)), ("q_norm_g", (QR,)),
        ("kv_norm_g", (KVR,)), ("norm_ffn_g", (D,)), ("final_norm_g", (D,)))
WEIGHTS = ("meta_tokens", "norm_mix_g", "w_in", "b_gate", "conv_w", "conv_b", "w_rec_a", "b_rec_a", "w_rec_i",
           "b_rec_i", "lru_lambda", "q_norm_g", "w_uq", "kv_norm_g", "w_ukv", "w_branch", "w_out", "norm_ffn_g",
           "w_ffn_in", "w_ffn_out", "final_norm_g")
W = 1024
SMALL_N = sum(math.prod(s) for _, s, _ in SMALL)
SMALL_ROWS = 8
REPL_N = sum(math.prod(s) for _, s in REPL)
QUART_ROWS = 88
assert SMALL_N <= SMALL_ROWS * W and REPL_N <= 4 * QUART_ROWS * W


def _flat_pad(parts, rows):
    v = jnp.concatenate([p.reshape(-1) for p in parts])
    return jnp.pad(v, (0, rows * W - v.shape[0])).reshape(rows, W)


def _shard_stack(full, shard_shape, axis):
    r, cs = shard_shape
    if axis == 0:
        return full.reshape(4, r, cs)
    return jnp.stack([full[:, s * cs:(s + 1) * cs] for s in range(4)])


def _unshard(stack, axis):
    if axis == 0:
        return stack.reshape(4 * stack.shape[1], stack.shape[2])
    return jnp.concatenate([stack[s] for s in range(4)], axis=1)


def _split(flat, table):
    out, off = {}, 0
    for name, shape, *_ in table:
        n = math.prod(shape)
        out[name] = flat[..., off:off + n].reshape(flat.shape[:-1] + tuple(shape))
        off += n
    return out


def kernel(x, meta_tokens, norm_mix_g, w_in, b_gate, conv_w, conv_b, w_rec_a, b_rec_a, w_rec_i, b_rec_i, lru_lambda, q_norm_g, w_uq, kv_norm_g, w_ukv, w_branch, w_out, norm_ffn_g, w_ffn_in, w_ffn_out, final_norm_g, loss_target, m_meta_tokens, m_norm_mix_g, m_w_in, m_b_gate, m_conv_w, m_conv_b, m_w_rec_a, m_b_rec_a, m_w_rec_i, m_b_rec_i, m_lru_lambda, m_q_norm_g, m_w_uq, m_kv_norm_g, m_w_ukv, m_w_branch, m_w_out, m_norm_ffn_g, m_w_ffn_in, m_w_ffn_out, m_final_norm_g, v_meta_tokens, v_norm_mix_g, v_w_in, v_b_gate, v_conv_w, v_conv_b, v_w_rec_a, v_b_rec_a, v_w_rec_i, v_b_rec_i, v_lru_lambda, v_q_norm_g, v_w_uq, v_kv_norm_g, v_w_ukv, v_w_branch, v_w_out, v_norm_ffn_g, v_w_ffn_in, v_w_ffn_out, v_final_norm_g):
    args = dict(locals())
    chip = 2 * lax.axis_index("x") + lax.axis_index("y")
    core = lax.axis_index("c")

    first_big = [b for b in BIG if b[0] not in LATE]
    late_big = [b for b in BIG if b[0] in LATE]
    bf16_shard = lambda n, s: args[n].reshape(s).astype(BF16)
    small = _flat_pad([args[n] for n, _, _ in SMALL], SMALL_ROWS)
    gathered = _allgather_chips([bf16_shard(n, s) for n, s, _ in first_big] + [small], name="gather_weights")
    w = {}
    for (name, _, axis), stack in zip(first_big, gathered):
        w[name] = _unshard(stack, axis)
    small_parts = _split(gathered[-1].reshape(4, SMALL_ROWS * W), SMALL)
    for name, _, axis in SMALL:
        w[name] = _unshard(small_parts[name], axis)
    for name, shape in REPL:
        w[name] = args[name].reshape(shape)
    finish_late = lambda stacks: {name: _unshard(st, axis) for (name, _, axis), st in zip(late_big, stacks)}

    def to_wire(red, tag, wires):
        theirs = _sibling_take(red, name="reduce_sibling_" + tag)
        return [_add_halves(a, t, core, wires[k], name=f"add_sibling_{tag}{k}")
                for k, (a, t) in enumerate(zip(red, theirs))]

    reduce_first = lambda gl: to_wire([_shard_stack(gl[n], s, a) for n, s, a in late_big], "a",
                                      [BF16] * len(late_big))
    loss, grad_x, g, (parts_a, recv_a) = _local_step(
        x[0], loss_target[0], w, late=([bf16_shard(n, s) for n, s, _ in late_big], finish_late),
        reduce_first=reduce_first)
    loss = lax.psum(loss, ("x", "y", "c"))

    red = [_shard_stack(g[n], s, a) for n, s, a in first_big]
    small_g = jnp.concatenate([_shard_stack(g[n], s, a).reshape(4, -1) for n, s, a in SMALL], axis=1)
    small_g = jnp.pad(small_g, ((0, 0), (0, SMALL_ROWS * W - SMALL_N))).reshape(4, SMALL_ROWS, W)
    repl_g = _flat_pad([g[n] for n, _ in REPL], 4 * QUART_ROWS).reshape(4, QUART_ROWS, W)
    red.append(jnp.concatenate([small_g, repl_g], axis=1))
    parts_b = to_wire(red, "b", [BF16] * len(first_big) + [F32])
    recv_b = _scatter_chips(parts_b, name="reduce_chips")
    order = [b[0] for b in late_big] + [b[0] for b in first_big] + ["misc"]
    halves = [_sum4(lax.dynamic_index_in_dim(p, chip, 0, keepdims=False), r, name="sum_chips_" + n)
              for n, p, r in zip(order, list(parts_a) + parts_b, list(recv_a) + list(recv_b))]
    others = _sibling_swap(halves, name="share_sibling")

    results = {}
    shape_of = {name: shape for name, shape, _ in BIG}
    for name, mine, theirs in zip(order[:-1], halves, others):
        shape = shape_of[name]
        results[name] = _adamw_halves(mine, theirs, core, args[name].reshape(shape),
                                      args["m_" + name].reshape(shape), args["v_" + name].reshape(shape),
                                      name="adamw_" + name)

    a, b = halves[-1], others[-1]
    g_mine = jnp.where(core == 0, jnp.concatenate([a, b], axis=0), jnp.concatenate([b, a], axis=0))
    g_repl = _allgather_chips([g_mine[SMALL_ROWS:]], name="gather_repl")[0].reshape(4 * QUART_ROWS, W)
    g_misc = jnp.concatenate([g_mine[:SMALL_ROWS], g_repl], axis=0)
    misc_state = lambda prefix: jnp.concatenate(
        [_flat_pad([args[prefix + n] for n, _, _ in SMALL], SMALL_ROWS),
         _flat_pad([args[prefix + n] for n, _ in REPL], 4 * QUART_ROWS)], axis=0)
    d, nm, nv = _adamw(g_misc, misc_state(""), misc_state("m_"), misc_state("v_"), name="adamw_misc")
    misc = (g_misc, d, nm, nv)

    outs = []
    for k in range(4):
        sm = _split(misc[k][:SMALL_ROWS].reshape(-1), SMALL)
        rp = _split(misc[k][SMALL_ROWS:].reshape(-1), REPL)
        for name in WEIGHTS:
            val = results[name][k] if name in results else (sm[name] if name in sm else rp[name])
            outs.append(val.reshape(args[name].shape))
    return (loss, grad_x[None], *outs)
```

```python
import functools
import math

import jax
import jax.numpy as jnp
from jax import lax
from jax.experimental import pallas as pl
from jax.experimental.pallas import tpu as pltpu

F32 = jnp.float32
BF16 = jnp.bfloat16

D = 1024
DR = 1280
NBLK = 10
RB = 128
CW = 4
NH = 8
NOPE = 128
ROPE = 64
VD = 128
QR = 384
KVR = 256
DFF = 2816
NMETA = 16
EPS = 1e-6
LRU_C = 8.0
ROPE_THETA = 10000.0
SCALE = 1.0 / math.sqrt(NOPE + ROPE)
NEG = -1e30
FRONT = 128
PAD = FRONT - NMETA
QW = 2 * NOPE
LANES = 128
SUB = 128
CHAINS = 4
VMEM_LIMIT = 52 * 1024 * 1024

ADAM_LR = 0.001
ADAM_B1 = 0.9
ADAM_B2 = 0.999
ADAM_EPS = 1e-08
ADAM_WD = 0.01
ADAM_STEP = 10

MESH = pl.DeviceIdType.MESH


def _cparams(sem):
    return pltpu.CompilerParams(dimension_semantics=sem, vmem_limit_bytes=VMEM_LIMIT)


def _sigmoid(x):
    return 1.0 / (1.0 + jnp.exp(-x))


def _gelu_parts(x):
    c = math.sqrt(2.0 / math.pi)
    inner = c * (x + 0.044715 * x * x * x)
    t = jnp.tanh(inner)
    g = 0.5 * x * (1.0 + t)
    dg = 0.5 * (1.0 + t) + 0.5 * x * (1.0 - t * t) * c * (1.0 + 3.0 * 0.044715 * x * x)
    return g, dg


def _divisors(n, step, cap):
    return [d for d in range(step, min(n, cap) + 1, step) if n % d == 0] or [n]


MM_VMEM_BUDGET = 40 * 1024 * 1024
MM_MAX_ROWS = 1664
MM_MAX_COLS = 1408


def _mm_tiles(M, K, N, a_item, out_item, has_res):
    best = None
    for tn in _divisors(N, LANES, MM_MAX_COLS):
        for tm in _divisors(M, 16, MM_MAX_ROWS):
            need = 2 * (tm * K * a_item + K * tn * 2 + tm * tn * (out_item + (4 if has_res else 0)))
            if need <= MM_VMEM_BUDGET and (best is None or tm * tn > best[0] * best[1]):
                best = (tm, tn)
    assert best is not None, (M, K, N)
    return best


def _mm(a, b, *, name, out_dtype=F32, res=None):
    M, K = a.shape
    N = b.shape[1]
    has_res = res is not None
    tm, tn = _mm_tiles(M, K, N, a.dtype.itemsize, jnp.dtype(out_dtype).itemsize, has_res)

    def body(*refs):
        if has_res:
            a_ref, b_ref, r_ref, o_ref = refs
        else:
            a_ref, b_ref, o_ref = refs
        acc = jnp.dot(a_ref[...].astype(BF16), b_ref[...].astype(BF16), preferred_element_type=F32)
        if has_res:
            acc = acc + r_ref[...].astype(F32)
        o_ref[...] = acc.astype(o_ref.dtype)

    a_bytes = M * K * a.dtype.itemsize
    b_bytes = K * N * b.dtype.itemsize
    rows_outer = a_bytes + (M // tm) * b_bytes <= b_bytes + (N // tn) * a_bytes
    if rows_outer:
        grid = (M // tm, N // tn)
        ia, ib, io = (lambda i, j: (i, 0)), (lambda i, j: (0, j)), (lambda i, j: (i, j))
    else:
        grid = (N // tn, M // tm)
        ia, ib, io = (lambda j, i: (i, 0)), (lambda j, i: (0, j)), (lambda j, i: (i, j))
    in_specs = [pl.BlockSpec((tm, K), ia), pl.BlockSpec((K, tn), ib)]
    args = [a, b]
    if has_res:
        in_specs.append(pl.BlockSpec((tm, tn), io))
        args.append(res)
    return pl.pallas_call(
        body, name=name, grid=grid, in_specs=in_specs,
        out_specs=pl.BlockSpec((tm, tn), io),
        out_shape=jax.ShapeDtypeStruct((M, N), out_dtype),
        compiler_params=_cparams(("parallel", "parallel")),
    )(*args)


def _mm_sum(pairs, *, name, res=None, out_dtype=F32):
    M = pairs[0][0].shape[0]
    N = pairs[0][1].shape[1]
    ks = [a.shape[1] for a, _ in pairs]
    has_res = res is not None
    tm, tn = _mm_tiles(M, sum(ks), N, 2, jnp.dtype(out_dtype).itemsize, has_res)
    n = len(pairs)

    def body(*refs):
        acc = None
        for k in range(n):
            d = jnp.dot(refs[2 * k][...].astype(BF16), refs[2 * k + 1][...].astype(BF16),
                        preferred_element_type=F32)
            acc = d if acc is None else acc + d
        if has_res:
            acc = acc + refs[2 * n][...].astype(F32)
        refs[-1][...] = acc.astype(refs[-1].dtype)

    in_specs, args = [], []
    for (a, b), kk in zip(pairs, ks):
        in_specs += [pl.BlockSpec((tm, kk), lambda i, j: (i, 0)), pl.BlockSpec((kk, tn), lambda i, j: (0, j))]
        args += [a, b]
    if has_res:
        in_specs.append(pl.BlockSpec((tm, tn), lambda i, j: (i, j)))
        args.append(res)
    return pl.pallas_call(
        body, name=name, grid=(M // tm, N // tn), in_specs=in_specs,
        out_specs=pl.BlockSpec((tm, tn), lambda i, j: (i, j)),
        out_shape=jax.ShapeDtypeStruct((M, N), out_dtype),
        compiler_params=_cparams(("parallel", "parallel")),
    )(*args)


def _mm_tn(a, b, *, name):
    T, K1 = a.shape
    N = b.shape[1]
    tt = _divisors(T, 16, MM_MAX_ROWS)[-1]
    tk = _divisors(K1, LANES, MM_MAX_COLS)[-1]
    tn = _divisors(N, LANES, MM_MAX_COLS)[-1]

    def body(a_ref, b_ref, o_ref):
        @pl.when(pl.program_id(2) == 0)
        def _():
            o_ref[...] = jnp.zeros_like(o_ref)

        o_ref[...] += lax.dot_general(a_ref[...].astype(BF16), b_ref[...].astype(BF16),
                                      (((0,), (0,)), ((), ())), preferred_element_type=F32)

    return pl.pallas_call(
        body, name=name, grid=(K1 // tk, N // tn, T // tt),
        in_specs=[pl.BlockSpec((tt, tk), lambda i, j, t: (t, i)),
                  pl.BlockSpec((tt, tn), lambda i, j, t: (t, j))],
        out_specs=pl.BlockSpec((tk, tn), lambda i, j, t: (i, j)),
        out_shape=jax.ShapeDtypeStruct((K1, N), F32),
        compiler_params=_cparams(("parallel", "parallel", "arbitrary")),
    )(a, b)


def _rows(tm, w, cb=0):
    return pl.BlockSpec((tm, w), lambda i: (i, cb))


def _const(shape):
    n = len(shape)
    return pl.BlockSpec(shape, lambda i: (0,) * n)


def _rmsnorm_fwd(x, g, *, name, tm=640):
    T, C = x.shape

    def body(x_ref, g_ref, o_ref):
        xv = x_ref[...]
        r = lax.rsqrt(jnp.mean(xv * xv, axis=-1, keepdims=True) + EPS)
        o_ref[...] = ((xv * r) * g_ref[...]).astype(BF16)

    return pl.pallas_call(
        body, name=name, grid=(T // tm,),
        in_specs=[_rows(tm, C), _const((1, C))],
        out_specs=_rows(tm, C),
        out_shape=jax.ShapeDtypeStruct((T, C), BF16),
        compiler_params=_cparams(("parallel",)),
    )(x, g)


def _rmsnorm_bwd(x, g, dy, res, *, name, tm=640, want_f32=True, want_bf16=True):
    T, C = x.shape
    has_res = res is not None

    def body(*refs):
        refs = list(refs)
        x_ref, g_ref, dy_ref = refs[:3]
        refs = refs[3:]
        r_ref = refs.pop(0) if has_res else None
        o32 = refs.pop(0) if want_f32 else None
        o16 = refs.pop(0) if want_bf16 else None
        dg_ref = refs.pop(0)

        @pl.when(pl.program_id(0) == 0)
        def _():
            dg_ref[...] = jnp.zeros_like(dg_ref)

        xv = x_ref[...]
        dyv = dy_ref[...].astype(F32)
        r = lax.rsqrt(jnp.mean(xv * xv, axis=-1, keepdims=True) + EPS)
        xn = xv * r
        dg_ref[...] += jnp.sum(dyv * xn, axis=0, keepdims=True)
        dxn = dyv * g_ref[...]
        dx = r * (dxn - xn * jnp.mean(dxn * xn, axis=-1, keepdims=True))
        if has_res:
            dx = dx + r_ref[...]
        if want_f32:
            o32[...] = dx
        if want_bf16:
            o16[...] = dx.astype(BF16)

    in_specs = [_rows(tm, C), _const((1, C)), _rows(tm, C)]
    args = [x, g, dy]
    if has_res:
        in_specs.append(_rows(tm, C))
        args.append(res)
    out_specs, out_shape = [], []
    if want_f32:
        out_specs.append(_rows(tm, C))
        out_shape.append(jax.ShapeDtypeStruct((T, C), F32))
    if want_bf16:
        out_specs.append(_rows(tm, C))
        out_shape.append(jax.ShapeDtypeStruct((T, C), BF16))
    out_specs.append(_const((1, C)))
    out_shape.append(jax.ShapeDtypeStruct((1, C), F32))
    return pl.pallas_call(
        body, name=name, grid=(T // tm,), in_specs=in_specs, out_specs=out_specs,
        out_shape=out_shape, compiler_params=_cparams(("arbitrary",)),
    )(*args)


def _gate_mix_fwd(um, bg, p_rnn, p_att, *, tm=320):
    T = um.shape[0]

    def body(um_ref, bg_ref, pr_ref, pa_ref, o_ref):
        g = _sigmoid(um_ref[...].astype(F32) + bg_ref[...])
        o_ref[...] = (g[:, :D] * pr_ref[...].astype(F32) + g[:, D:] * pa_ref[...].astype(F32)).astype(BF16)

    return pl.pallas_call(
        body, name="gate_mix_fwd", grid=(T // tm,),
        in_specs=[_rows(tm, 2 * D), _const((1, 2 * D)), _rows(tm, D), _rows(tm, D)],
        out_specs=_rows(tm, D),
        out_shape=jax.ShapeDtypeStruct((T, D), BF16),
        compiler_params=_cparams(("parallel",)),
    )(um, bg, p_rnn, p_att)


def _gate_mix_bwd(um, bg, p_rnn, p_att, dmixed, *, tm=320):
    T = um.shape[0]

    def body(um_ref, bg_ref, pr_ref, pa_ref, dm_ref, dpr_ref, dpa_ref, dum_ref, dbg_ref):
        @pl.when(pl.program_id(0) == 0)
        def _():
            dbg_ref[...] = jnp.zeros_like(dbg_ref)

        g = _sigmoid(um_ref[...].astype(F32) + bg_ref[...])
        g0, g1 = g[:, :D], g[:, D:]
        dm = dm_ref[...].astype(F32)
        dpr_ref[...] = (dm * g0).astype(BF16)
        dpa_ref[...] = (dm * g1).astype(BF16)
        d0 = dm * pr_ref[...].astype(F32) * g0 * (1.0 - g0)
        d1 = dm * pa_ref[...].astype(F32) * g1 * (1.0 - g1)
        dum_ref[:, :D] = d0.astype(BF16)
        dum_ref[:, D:] = d1.astype(BF16)
        dbg_ref[:, :D] += jnp.sum(d0, axis=0, keepdims=True)
        dbg_ref[:, D:] += jnp.sum(d1, axis=0, keepdims=True)

    return pl.pallas_call(
        body, name="gate_mix_bwd", grid=(T // tm,),
        in_specs=[_rows(tm, 2 * D), _const((1, 2 * D)), _rows(tm, D), _rows(tm, D), _rows(tm, D)],
        out_specs=[_rows(tm, D), _rows(tm, D), _rows(tm, 2 * D), _const((1, 2 * D))],
        out_shape=[jax.ShapeDtypeStruct((T, D), BF16), jax.ShapeDtypeStruct((T, D), BF16),
                   jax.ShapeDtypeStruct((T, 2 * D), BF16), jax.ShapeDtypeStruct((1, 2 * D), F32)],
        compiler_params=_cparams(("arbitrary",)),
    )(um, bg, p_rnn, p_att, dmixed)


def _swiglu_fwd(ff, *, tm=320):
    T = ff.shape[0]

    def body(g_ref, u_ref, o_ref):
        gv = g_ref[...].astype(F32)
        o_ref[...] = (gv * _sigmoid(gv) * u_ref[...].astype(F32)).astype(BF16)

    return pl.pallas_call(
        body, name="swiglu_fwd", grid=(T // tm,),
        in_specs=[_rows(tm, DFF, 0), _rows(tm, DFF, 1)],
        out_specs=_rows(tm, DFF),
        out_shape=jax.ShapeDtypeStruct((T, DFF), BF16),
        compiler_params=_cparams(("parallel",)),
    )(ff, ff)


def _swiglu_bwd(ff, dact, *, tm=320):
    T = ff.shape[0]

    def body(g_ref, u_ref, da_ref, o_ref):
        gv = g_ref[...].astype(F32)
        s = _sigmoid(gv)
        da = da_ref[...].astype(F32)
        o_ref[:, :DFF] = (da * u_ref[...].astype(F32) * s * (1.0 + gv * (1.0 - s))).astype(BF16)
        o_ref[:, DFF:] = (da * gv * s).astype(BF16)

    return pl.pallas_call(
        body, name="swiglu_bwd", grid=(T // tm,),
        in_specs=[_rows(tm, DFF, 0), _rows(tm, DFF, 1), _rows(tm, DFF)],
        out_specs=_rows(tm, 2 * DFF),
        out_shape=jax.ShapeDtypeStruct((T, 2 * DFF), BF16),
        compiler_params=_cparams(("parallel",)),
    )(ff, ff, dact)


def _loss_head(h2, tgt, g, *, tm=640):
    T = h2.shape[0]
    nsub = tm // FRONT

    def body(h_ref, *refs):
        t_refs = refs[:nsub]
        g_ref, d32_ref, d16_ref, dg_ref, ls_ref = refs[nsub:]
        i = pl.program_id(0)

        @pl.when(i == 0)
        def _():
            dg_ref[...] = jnp.zeros_like(dg_ref)
            ls_ref[...] = jnp.zeros_like(ls_ref)

        gv = g_ref[...]
        for k in range(nsub):
            rows = pl.ds(k * FRONT, FRONT)
            xv = h_ref[rows, :]
            r = lax.rsqrt(jnp.mean(xv * xv, axis=-1, keepdims=True) + EPS)
            xn = xv * r
            e = jnp.where(i * nsub + k >= 1, xn * gv - t_refs[k][...], 0.0)
            ls_ref[...] += jnp.sum(e * e, axis=0, keepdims=True)
            dy = e * (1.0 / D)
            dg_ref[...] += jnp.sum(dy * xn, axis=0, keepdims=True)
            dxn = dy * gv
            dx = r * (dxn - xn * jnp.mean(dxn * xn, axis=-1, keepdims=True))
            d32_ref[rows, :] = dx
            d16_ref[rows, :] = dx.astype(BF16)

    def t_spec(k):
        return pl.BlockSpec((FRONT, D), lambda i: (jnp.maximum(i * nsub + k - 1, 0), 0))

    return pl.pallas_call(
        body, name="loss_head", grid=(T // tm,),
        in_specs=[_rows(tm, D)] + [t_spec(k) for k in range(nsub)] + [_const((1, D))],
        out_specs=[_rows(tm, D), _rows(tm, D), _const((1, D)), _const((1, D))],
        out_shape=[jax.ShapeDtypeStruct((T, D), F32), jax.ShapeDtypeStruct((T, D), BF16),
                   jax.ShapeDtypeStruct((1, D), F32), jax.ShapeDtypeStruct((1, D), F32)],
        compiler_params=_cparams(("arbitrary",)),
    )(h2, *([tgt] * nsub), g)


def _scan_fwd(a, b, h_in):
    n = a.shape[0]
    row = lax.broadcasted_iota(jnp.int32, a.shape, 0)
    s = 1
    while s < n:
        if s % 8:
            a_sh = jnp.where(row >= s, pltpu.roll(a, s, 0), 1.0)
            b_sh = jnp.where(row >= s, pltpu.roll(b, s, 0), 0.0)
        else:
            a_sh = jnp.concatenate([jnp.ones((s, RB), F32), a[:n - s]], axis=0)
            b_sh = jnp.concatenate([jnp.zeros((s, RB), F32), b[:n - s]], axis=0)
        b = a * b_sh + b
        a = a * a_sh
        s *= 2
    return b + a * h_in


def _scan_rev(a, b, g_in):
    n = a.shape[0]
    row = lax.broadcasted_iota(jnp.int32, a.shape, 0)
    s = 1
    while s < n:
        if s % 8:
            a_sh = jnp.where(row < n - s, pltpu.roll(a, n - s, 0), 1.0)
            b_sh = jnp.where(row < n - s, pltpu.roll(b, n - s, 0), 0.0)
        else:
            a_sh = jnp.concatenate([a[s:], jnp.ones((s, RB), F32)], axis=0)
            b_sh = jnp.concatenate([b[s:], jnp.zeros((s, RB), F32)], axis=0)
        b = a * b_sh + b
        a = a * a_sh
        s *= 2
    return b + a * g_in


def _lru_gates(xc, wa, ba, wi, bi, lam):
    xcb = xc.astype(BF16)
    r = _sigmoid(jnp.dot(xcb, wa, preferred_element_type=F32) + ba)
    ig = _sigmoid(jnp.dot(xcb, wi, preferred_element_type=F32) + bi)
    log_sig = jnp.minimum(lam, 0.0) - jnp.log(1.0 + jnp.exp(-jnp.abs(lam)))
    log_a = LRU_C * r * log_sig
    a = jnp.exp(log_a)
    m2 = jnp.tanh(-log_a) * (1.0 + a * a)
    return r, ig, log_sig, a, m2 * lax.rsqrt(jnp.maximum(m2, 1e-37))


def _rnn_specs(tc, nblk_t, rev):
    def tmap(k):
        return (nblk_t - 1 - k) if rev else k

    hb = tc // 8
    blk = lambda off: pl.BlockSpec((tc, RB), lambda c, k: (tmap(k), c + off))
    halo = lambda off: pl.BlockSpec((8, RB), lambda c, k: (jnp.maximum(tmap(k) * hb - 1, 0), c + off))
    vec = pl.BlockSpec((1, RB), lambda c, k: (0, c))
    cwv = pl.BlockSpec((CW, RB), lambda c, k: (0, c))
    mat = pl.BlockSpec((None, RB, RB), lambda c, k: (c, 0, 0))
    return blk, halo, vec, cwv, mat


def _rnn_fwd(uxg, cw, cb, wa, ba, wi, bi, lam, *, tc=640):
    T = uxg.shape[0]
    nt = T // tc
    nsub = tc // SUB
    blk, halo, vec, cwv, mat = _rnn_specs(tc, nt, False)

    def body(x_ref, xh_ref, ug_ref, cw_ref, cb_ref, wa_ref, ba_ref, wi_ref, bi_ref, lam_ref,
             h_ref, y_ref, xb, hc):
        k = pl.program_id(1)

        @pl.when(k == 0)
        def _():
            hc[...] = jnp.zeros_like(hc)

        xb[0:8, :] = jnp.where(k > 0, xh_ref[...], 0.0)
        xb[8:, :] = x_ref[...]
        cwv_, cbv = cw_ref[...], cb_ref[...]
        wav, wiv = wa_ref[...], wi_ref[...]
        bav, biv, lamv = ba_ref[...], bi_ref[...], lam_ref[...]
        h_in = hc[0:1, :]
        for sc in range(nsub):
            r0 = sc * SUB
            xc = cbv + cwv_[0:1, :] * xb[pl.ds(5 + r0, SUB), :]
            for j in range(1, CW):
                xc = xc + cwv_[j:j + 1, :] * xb[pl.ds(5 + j + r0, SUB), :]
            r, ig, _, a, mm = _lru_gates(xc, wav, bav, wiv, biv, lamv)
            rows = k * tc + r0 + lax.broadcasted_iota(jnp.int32, (SUB, RB), 0)
            b = jnp.where(rows >= PAD, mm * (ig * xc), 0.0)
            h = _scan_fwd(a, b, h_in)
            h_in = h[SUB - 1:SUB, :]
            h_ref[pl.ds(r0, SUB), :] = h
            gl, _ = _gelu_parts(ug_ref[pl.ds(r0, SUB), :])
            y_ref[pl.ds(r0, SUB), :] = (h * gl).astype(BF16)
        hc[0:1, :] = h_in

    return pl.pallas_call(
        body, name="rnn_fwd", grid=(NBLK, nt),
        in_specs=[blk(0), halo(0), blk(NBLK), cwv, vec, mat, vec, mat, vec, vec],
        out_specs=[blk(0), blk(0)],
        out_shape=[jax.ShapeDtypeStruct((T, DR), F32), jax.ShapeDtypeStruct((T, DR), BF16)],
        scratch_shapes=[pltpu.VMEM((tc + 8, RB), F32), pltpu.VMEM((8, RB), F32)],
        compiler_params=_cparams(("parallel", "arbitrary")),
    )(uxg, uxg, uxg, cw, cb, wa, ba, wi, bi, lam)


def _rnn_bwd(uxg, hs, dy, cw, cb, wa, ba, wi, bi, lam, wat, wit, *, tc=640):
    T = uxg.shape[0]
    nt = T // tc
    nsub = tc // SUB
    blk, halo, vec, cwv, mat = _rnn_specs(tc, nt, True)

    def body(x_ref, xh_ref, ug_ref, h_ref, hh_ref, dy_ref, cw_ref, cb_ref, wa_ref, ba_ref, wi_ref,
             bi_ref, lam_ref, wat_ref, wit_ref,
             dux_ref, dug_ref, dcw_ref, dcb_ref, dwa_ref, dba_ref, dwi_ref, dbi_ref, dlam_ref,
             xb, hb, ab, dxb, xcs, rs, igs, mms, dgas, dgis, carry):
        k = pl.program_id(1)
        kt = nt - 1 - k

        @pl.when(k == 0)
        def _():
            carry[...] = jnp.zeros_like(carry)
            for ref in (dcw_ref, dcb_ref, dwa_ref, dba_ref, dwi_ref, dbi_ref, dlam_ref):
                ref[...] = jnp.zeros_like(ref)

        xb[0:8, :] = jnp.where(kt > 0, xh_ref[...], 0.0)
        xb[8:, :] = x_ref[...]
        hb[0:8, :] = jnp.where(kt > 0, hh_ref[...], 0.0)
        hb[8:, :] = h_ref[...]
        cwv_, cbv = cw_ref[...], cb_ref[...]
        wav, wiv = wa_ref[...], wi_ref[...]
        bav, biv, lamv = ba_ref[...], bi_ref[...], lam_ref[...]
        ab[tc:tc + 8, :] = jnp.broadcast_to(carry[1:2, :], (8, RB))
        dxb[tc:tc + 8, :] = carry[8:16, :]
        log_sig = None
        for sc in range(nsub):
            r0 = sc * SUB
            xc = cbv + cwv_[0:1, :] * xb[pl.ds(5 + r0, SUB), :]
            for j in range(1, CW):
                xc = xc + cwv_[j:j + 1, :] * xb[pl.ds(5 + j + r0, SUB), :]
            r, ig, log_sig, a, mm = _lru_gates(xc, wav, bav, wiv, biv, lamv)
            xcs[pl.ds(r0, SUB), :] = xc
            rs[pl.ds(r0, SUB), :] = r
            igs[pl.ds(r0, SUB), :] = ig
            mms[pl.ds(r0, SUB), :] = mm
            ab[pl.ds(r0, SUB), :] = a
        sig_neg = _sigmoid(-lamv)
        g_in = carry[0:1, :]
        dlam_acc = jnp.zeros((1, RB), F32)
        for sc in reversed(range(nsub)):
            r0 = sc * SUB
            xc, r, ig, mm = xcs[pl.ds(r0, SUB), :], rs[pl.ds(r0, SUB), :], igs[pl.ds(r0, SUB), :], mms[pl.ds(r0, SUB), :]
            a = ab[pl.ds(r0, SUB), :]
            a_next = ab[pl.ds(r0 + 1, SUB), :]
            hv = hb[pl.ds(8 + r0, SUB), :]
            hprev = hb[pl.ds(7 + r0, SUB), :]
            dyv = dy_ref[pl.ds(r0, SUB), :]
            gl, dgl = _gelu_parts(ug_ref[pl.ds(r0, SUB), :])
            dug_ref[pl.ds(r0, SUB), :] = (dyv * hv * dgl).astype(BF16)
            G = _scan_rev(a_next, dyv * gl, g_in)
            g_in = G[0:1, :]
            rows = kt * tc + r0 + lax.broadcasted_iota(jnp.int32, (SUB, RB), 0)
            db = jnp.where(rows >= PAD, G, 0.0)
            da = G * hprev
            dmm = db * (ig * xc)
            di = db * (mm * xc)
            dxc = db * (mm * ig)
            dlog_a = da * a - dmm * (a * a) / jnp.maximum(mm, 1e-30)
            dr = dlog_a * (LRU_C * log_sig)
            dlam_acc = dlam_acc + jnp.sum(dlog_a * (LRU_C * r), axis=0, keepdims=True)
            dga = dr * r * (1.0 - r)
            dgi = di * ig * (1.0 - ig)
            dgab, dgib = dga.astype(BF16), dgi.astype(BF16)
            dgas[pl.ds(r0, SUB), :] = dgab
            dgis[pl.ds(r0, SUB), :] = dgib
            dba_ref[...] += jnp.sum(dga, axis=0, keepdims=True)
            dbi_ref[...] += jnp.sum(dgi, axis=0, keepdims=True)
            dxc = dxc + jnp.dot(dgab, wat_ref[...], preferred_element_type=F32) \
                + jnp.dot(dgib, wit_ref[...], preferred_element_type=F32)
            dxb[pl.ds(r0, SUB), :] = dxc
        dlam_ref[...] += dlam_acc * sig_neg
        xcb = xcs[...].astype(BF16)
        tn = (((0,), (0,)), ((), ()))
        dwa_ref[...] += lax.dot_general(xcb, dgas[...], tn, preferred_element_type=F32)
        dwi_ref[...] += lax.dot_general(xcb, dgis[...], tn, preferred_element_type=F32)
        dxc_all = dxb[0:tc, :]
        dcb_ref[...] += jnp.sum(dxc_all, axis=0, keepdims=True)
        rows_all = kt * tc + lax.broadcasted_iota(jnp.int32, (tc, RB), 0)
        dux = jnp.zeros((tc, RB), F32)
        for j in range(CW):
            dcw_ref[j:j + 1, :] += jnp.sum(dxc_all * xb[pl.ds(5 + j, tc), :], axis=0, keepdims=True)
            dux = dux + cwv_[j:j + 1, :] * dxb[pl.ds(CW - 1 - j, tc), :]
        dux_ref[...] = jnp.where(rows_all >= PAD, dux, 0.0).astype(BF16)
        carry[0:1, :] = g_in
        carry[1:2, :] = ab[0:1, :]
        carry[8:16, :] = dxb[0:8, :]

    vec_out = pl.BlockSpec((1, RB), lambda c, k: (0, c))
    return pl.pallas_call(
        body, name="rnn_bwd", grid=(NBLK, nt),
        in_specs=[blk(0), halo(0), blk(NBLK), blk(0), halo(0), blk(0), cwv, vec, mat, vec, mat, vec, vec, mat, mat],
        out_specs=[blk(0), blk(0), cwv, vec_out, mat, vec_out, mat, vec_out, vec_out],
        out_shape=[jax.ShapeDtypeStruct((T, DR), BF16), jax.ShapeDtypeStruct((T, DR), BF16),
                   jax.ShapeDtypeStruct((CW, DR), F32), jax.ShapeDtypeStruct((1, DR), F32),
                   jax.ShapeDtypeStruct((NBLK, RB, RB), F32), jax.ShapeDtypeStruct((1, DR), F32),
                   jax.ShapeDtypeStruct((NBLK, RB, RB), F32), jax.ShapeDtypeStruct((1, DR), F32),
                   jax.ShapeDtypeStruct((1, DR), F32)],
        scratch_shapes=[pltpu.VMEM((tc + 8, RB), F32), pltpu.VMEM((tc + 8, RB), F32),
                        pltpu.VMEM((tc + 8, RB), F32), pltpu.VMEM((tc + 8, RB), F32),
                        pltpu.VMEM((tc, RB), F32), pltpu.VMEM((tc, RB), F32), pltpu.VMEM((tc, RB), F32),
                        pltpu.VMEM((tc, RB), F32), pltpu.VMEM((tc, RB), BF16), pltpu.VMEM((tc, RB), BF16),
                        pltpu.VMEM((16, RB), F32)],
        compiler_params=_cparams(("parallel", "arbitrary")),
    )(uxg, uxg, uxg, hs, hs, dy, cw, cb, wa, ba, wi, bi, lam, wat, wit)


def _attn_prep(q_all, kv_all, ukr, tab, *, tm=320):
    T = q_all.shape[0]

    def body(q_ref, kv_ref, kr_ref, tab_ref, qo_ref, ko_ref, vo_ref):
        tab_v = tab_ref[...]
        lane = lax.broadcasted_iota(jnp.int32, (tm, LANES), 1)
        t1 = kr_ref[...] * tab_v
        kro = jnp.where(lane < ROPE, t1 + pltpu.roll(t1, ROPE, 1), 0.0).astype(BF16)
        for h in range(NH):
            c0 = h * QW
            qo_ref[h, :, 0:NOPE] = (q_ref[:, c0:c0 + NOPE].astype(F32) * SCALE).astype(BF16)
            t2 = q_ref[:, c0 + NOPE:c0 + QW].astype(F32) * tab_v
            qo_ref[h, :, NOPE:QW] = ((t2 + pltpu.roll(t2, ROPE, 1)) * SCALE).astype(BF16)
            ko_ref[h, :, 0:NOPE] = kv_ref[:, c0:c0 + NOPE].astype(BF16)
            ko_ref[h, :, NOPE:QW] = kro
            vo_ref[h, :, :] = kv_ref[:, c0 + NOPE:c0 + QW].astype(BF16)

    return pl.pallas_call(
        body, name="attn_prep", grid=(T // tm,),
        in_specs=[_rows(tm, NH * QW), _rows(tm, NH * QW), _rows(tm, LANES), _rows(tm, LANES)],
        out_specs=[pl.BlockSpec((NH, tm, QW), lambda i: (0, i, 0)), pl.BlockSpec((NH, tm, QW), lambda i: (0, i, 0)),
                   pl.BlockSpec((NH, tm, VD), lambda i: (0, i, 0))],
        out_shape=[jax.ShapeDtypeStruct((NH, T, QW), BF16), jax.ShapeDtypeStruct((NH, T, QW), BF16),
                   jax.ShapeDtypeStruct((NH, T, VD), BF16)],
        compiler_params=_cparams(("parallel",)),
    )(q_all, kv_all, ukr, tab)


def _attn_prep_bwd(dq, dk, dv, tab, *, tm=320):
    T = dq.shape[1]

    def body(dq_ref, dk_ref, dv_ref, tab_ref, dqa_ref, dkva_ref, dkr_ref):
        tab_v = tab_ref[...]
        lane = lax.broadcasted_iota(jnp.int32, (tm, LANES), 1)
        dkro = jnp.zeros((tm, LANES), F32)
        for h in range(NH):
            c0 = h * QW
            dqa_ref[:, c0:c0 + NOPE] = (dq_ref[h, :, 0:NOPE] * SCALE).astype(BF16)
            d2 = dq_ref[h, :, NOPE:QW]
            dqa_ref[:, c0 + NOPE:c0 + QW] = ((d2 + pltpu.roll(d2, ROPE, 1)) * tab_v * SCALE).astype(BF16)
            dkva_ref[:, c0:c0 + NOPE] = dk_ref[h, :, 0:NOPE].astype(BF16)
            dkva_ref[:, c0 + NOPE:c0 + QW] = dv_ref[h, :, :].astype(BF16)
            dkro = dkro + dk_ref[h, :, NOPE:QW]
        dkro = jnp.where(lane < ROPE, dkro, 0.0)
        dkr_ref[...] = ((dkro + pltpu.roll(dkro, ROPE, 1)) * tab_v).astype(BF16)

    return pl.pallas_call(
        body, name="attn_prep_bwd", grid=(T // tm,),
        in_specs=[pl.BlockSpec((NH, tm, QW), lambda i: (0, i, 0)), pl.BlockSpec((NH, tm, QW), lambda i: (0, i, 0)),
                  pl.BlockSpec((NH, tm, VD), lambda i: (0, i, 0)), _rows(tm, LANES)],
        out_specs=[_rows(tm, NH * QW), _rows(tm, NH * QW), _rows(tm, LANES)],
        out_shape=[jax.ShapeDtypeStruct((T, NH * QW), BF16), jax.ShapeDtypeStruct((T, NH * QW), BF16),
                   jax.ShapeDtypeStruct((T, LANES), BF16)],
        compiler_params=_cparams(("parallel",)),
    )(dq, dk, dv, tab)


def _visible(q0, k0, nq, nk):
    rows = q0 + lax.broadcasted_iota(jnp.int32, (nq, nk), 0)
    cols = k0 + lax.broadcasted_iota(jnp.int32, (nq, nk), 1)
    return ((cols >> 6) <= (rows >> 6)) & (cols >= PAD)


def _visible_t(q0, k0, nq, nk):
    cols = k0 + lax.broadcasted_iota(jnp.int32, (nk, nq), 0)
    rows = q0 + lax.broadcasted_iota(jnp.int32, (nk, nq), 1)
    return ((cols >> 6) <= (rows >> 6)) & (cols >= PAD)


_NT = (((1,), (1,)), ((), ()))
ATTN_BLOCK = 1664


def _attn_block(T):
    return ATTN_BLOCK if T % ATTN_BLOCK == 0 else 640


def _round_up(n, m):
    return -(-n // m) * m


def _flash_fwd(q, k, v, *, gather=(), bq=None):
    T = q.shape[1]
    bq = bq or _attn_block(T)
    nq = T // bq
    rs = bq // CHAINS
    n = len(gather)

    def body(*refs):
        q_ref, k_ref, v_ref = refs[:3]
        g_src = refs[3:3 + n]
        o_ref, lse_ref = refs[3 + n:5 + n]
        g_out = refs[5 + n:5 + 2 * n]
        scr = refs[5 + 2 * n:]
        m_s, l_s, acc_s = scr[:CHAINS], scr[CHAINS:2 * CHAINS], scr[2 * CHAINS:3 * CHAINS]
        g_scr = scr[3 * CHAINS:]
        h = pl.program_id(0)
        i = pl.program_id(1)
        if n:
            @pl.when((h == 0) & (i == 0))
            def _():
                _gather_start(_gather_descs(g_src, g_out, g_scr))

        for r in range(CHAINS):
            m_s[r][...] = jnp.full_like(m_s[r], NEG)
            l_s[r][...] = jnp.zeros_like(l_s[r])
            acc_s[r][...] = jnp.zeros_like(acc_s[r])

        def step(j, masked, diag):
            off = pl.multiple_of(j * bq, bq)
            for r in range(CHAINS):
                rows = pl.ds(r * rs, rs)
                kw = min(bq, _round_up((r + 1) * rs, LANES)) if diag else bq
                kv_ = k_ref[pl.ds(off, kw), :]
                vv = v_ref[pl.ds(off, kw), :]
                s = lax.dot_general(q_ref[rows, :], kv_, _NT, preferred_element_type=F32)
                if masked:
                    s = jnp.where(_visible(i * bq + r * rs, j * bq, rs, kw), s, NEG)
                m_prev = m_s[r][...]
                m_new = jnp.maximum(m_prev, jnp.max(s, axis=-1, keepdims=True))
                p = jnp.exp(s - m_new)
                alpha = jnp.exp(m_prev - m_new)
                l_s[r][...] = alpha * l_s[r][...] + jnp.sum(p, axis=-1, keepdims=True)
                acc_s[r][...] = alpha * acc_s[r][...] + jnp.dot(p.astype(BF16), vv, preferred_element_type=F32)
                m_s[r][...] = m_new

        @pl.when(i == 0)
        def _():
            step(0, True, True)

        @pl.when(i > 0)
        def _():
            step(0, True, False)

            def loop(j, c):
                step(j, False, False)
                return c

            lax.fori_loop(1, i, loop, 0)
            step(i, True, True)

        for r in range(CHAINS):
            rows = pl.ds(r * rs, rs)
            o_ref[rows, :] = (acc_s[r][...] / l_s[r][...]).astype(BF16)
            lse_ref[rows, :] = m_s[r][...] + jnp.log(l_s[r][...])

        if n:
            @pl.when((h == NH - 1) & (i == nq - 1))
            def _():
                _gather_wait(_gather_descs(g_src, g_out, g_scr, with_loads=False))

    return pl.pallas_call(
        body, name="flash_fwd", grid=(NH, nq),
        in_specs=[pl.BlockSpec((None, bq, QW), lambda h, i: (h, i, 0)),
                  pl.BlockSpec((None, T, QW), lambda h, i: (h, 0, 0)),
                  pl.BlockSpec((None, T, VD), lambda h, i: (h, 0, 0))] + [HBM] * n,
        out_specs=[pl.BlockSpec((bq, VD), lambda h, i: (i, h)),
                   pl.BlockSpec((None, bq, 1), lambda h, i: (h, i, 0))] + [HBM] * n,
        out_shape=[jax.ShapeDtypeStruct((T, NH * VD), BF16), jax.ShapeDtypeStruct((NH, T, 1), F32)]
        + [jax.ShapeDtypeStruct((4,) + g.shape, g.dtype) for g in gather],
        scratch_shapes=[pltpu.VMEM((rs, 1), F32)] * (2 * CHAINS) + [pltpu.VMEM((rs, VD), F32)] * CHAINS
        + (_gather_scratch(gather) if n else []),
        compiler_params=_cparams(("arbitrary", "arbitrary")),
    )(q, k, v, *gather)


def _attn_delta(o, do, *, tm=640):
    T = o.shape[0]

    def body(o_ref, do_ref, d_ref):
        prod = o_ref[...].astype(F32) * do_ref[...].astype(F32)
        for h in range(NH):
            d_ref[h, :, :] = jnp.sum(prod[:, h * VD:(h + 1) * VD], axis=-1, keepdims=True)

    return pl.pallas_call(
        body, name="attn_delta", grid=(T // tm,),
        in_specs=[_rows(tm, NH * VD), _rows(tm, NH * VD)],
        out_specs=pl.BlockSpec((NH, tm, 1), lambda i: (0, i, 0)),
        out_shape=jax.ShapeDtypeStruct((NH, T, 1), F32),
        compiler_params=_cparams(("parallel",)),
    )(o, do)


_TN = (((0,), (0,)), ((), ()))


def _flash_bwd(q, k, v, do, lse_row, delta_row, *, scatter=(), bq=None):
    T = q.shape[1]
    bq = bq or _attn_block(T)
    nq = T // bq
    rs = bq // CHAINS
    n = len(scatter)

    def body(*refs):
        q_ref, k_ref, v_ref, do_ref, lse_ref, dl_ref = refs[:6]
        s_src = refs[6:6 + n]
        dq_ref, dk_ref, dv_ref = refs[6 + n:9 + n]
        s_out = refs[9 + n:9 + 2 * n]
        s_scr = refs[9 + 2 * n:]
        h = pl.program_id(0)
        j = pl.program_id(1)
        if n:
            @pl.when((h == 0) & (j == 0))
            def _():
                for cp in _scatter_descs(s_src, s_out, s_scr):
                    cp.start()

        @pl.when(j == 0)
        def _():
            dq_ref[...] = jnp.zeros_like(dq_ref)

        dk_ref[...] = jnp.zeros_like(dk_ref)
        dv_ref[...] = jnp.zeros_like(dv_ref)

        def step(i, masked, diag):
            for r in range(CHAINS):
                rows = pl.ds(r * rs, rs)
                q0 = (r * rs) // LANES * LANES if diag else 0
                qn = bq - q0
                off = pl.multiple_of(i * bq + q0, LANES)
                qv = q_ref[pl.ds(off, qn), :]
                dov = do_ref[pl.ds(off, qn), :]
                lse_v = lse_ref[:, pl.ds(off, qn)]
                dl_v = dl_ref[:, pl.ds(off, qn)]
                st = lax.dot_general(k_ref[rows, :], qv, _NT, preferred_element_type=F32)
                if masked:
                    st = jnp.where(_visible_t(i * bq + q0, j * bq + r * rs, qn, rs), st, NEG)
                pt = jnp.exp(st - lse_v)
                dv_ref[rows, :] += jnp.dot(pt.astype(BF16), dov, preferred_element_type=F32)
                dpt = lax.dot_general(v_ref[rows, :], dov, _NT, preferred_element_type=F32)
                dst = (pt * (dpt - dl_v)).astype(BF16)
                dk_ref[rows, :] += jnp.dot(dst, qv, preferred_element_type=F32)
                dq_ref[pl.ds(off, qn), :] += lax.dot_general(dst, k_ref[rows, :], _TN,
                                                             preferred_element_type=F32)

        step(j, True, True)

        @pl.when(j == 0)
        def _():
            def loop(i, c):
                step(i, True, False)
                return c
            lax.fori_loop(1, nq, loop, 0)

        @pl.when(j > 0)
        def _():
            def loop(i, c):
                step(i, False, False)
                return c
            lax.fori_loop(j + 1, nq, loop, 0)

        if n:
            @pl.when((h == NH - 1) & (j == nq - 1))
            def _():
                for cp in _scatter_descs(s_src, s_out, s_scr):
                    cp.wait()

    return pl.pallas_call(
        body, name="flash_bwd", grid=(NH, nq),
        in_specs=[pl.BlockSpec((None, T, QW), lambda h, j: (h, 0, 0)),
                  pl.BlockSpec((None, bq, QW), lambda h, j: (h, j, 0)),
                  pl.BlockSpec((None, bq, VD), lambda h, j: (h, j, 0)),
                  pl.BlockSpec((T, VD), lambda h, j: (0, h)),
                  pl.BlockSpec((None, 1, T), lambda h, j: (h, 0, 0)),
                  pl.BlockSpec((None, 1, T), lambda h, j: (h, 0, 0))] + [HBM] * n,
        out_specs=[pl.BlockSpec((None, T, QW), lambda h, j: (h, 0, 0)),
                   pl.BlockSpec((None, bq, QW), lambda h, j: (h, j, 0)),
                   pl.BlockSpec((None, bq, VD), lambda h, j: (h, j, 0))] + [HBM] * n,
        out_shape=[jax.ShapeDtypeStruct((NH, T, QW), F32), jax.ShapeDtypeStruct((NH, T, QW), F32),
                   jax.ShapeDtypeStruct((NH, T, VD), F32)]
        + [jax.ShapeDtypeStruct((3,) + s.shape[1:], s.dtype) for s in scatter],
        scratch_shapes=[_dma_sems(3 * n), _dma_sems(3 * n)] if n else [],
        compiler_params=_cparams(("arbitrary", "arbitrary")),
    )(q, k, v, do, lse_row, delta_row, *scatter)


def _rope_table(T):
    pos = (jnp.arange(T, dtype=jnp.int32) - PAD).astype(F32)
    inv_freq = ROPE_THETA ** (-jnp.arange(0, ROPE, 2, dtype=F32) / ROPE)
    ang = pos[:, None] * inv_freq[None, :]
    cos, sin = jnp.cos(ang), jnp.sin(ang)
    return jnp.concatenate([cos, cos, -sin, sin], axis=1)


def _swap_halves(w):
    return jnp.concatenate([w[..., ROPE // 2:], w[..., :ROPE // 2]], axis=-1)


O_UX, O_UG, O_UQ, O_UKV, O_UKR, O_UM = 0, DR, 2 * DR, 2 * DR + QR, 2 * DR + QR + KVR, 2 * DR + QR + KVR + ROPE


def _prep_weights(w):
    b = lambda a: a.astype(BF16)
    w_in = w["w_in"]
    kr = w_in[:, O_UKR:O_UM]
    p = {
        "w_xg": b(w_in[:, :O_UQ]),
        "w_q": b(w_in[:, O_UQ:O_UKV]),
        "w_kv": b(w_in[:, O_UKV:O_UKR]),
        "w_kr": b(jnp.concatenate([kr, _swap_halves(kr)], axis=1)),
        "w_m": b(w_in[:, O_UM:]),
    }
    wq = w["w_uq"].reshape(QR, NH, NOPE + ROPE)
    p["w_uq"] = b(jnp.concatenate([wq, _swap_halves(wq[..., NOPE:])], axis=-1).reshape(QR, NH * QW))
    p["w_ukv"] = b(w["w_ukv"])
    for n in ("w_xg", "w_q", "w_kv", "w_kr", "w_m", "w_uq", "w_ukv"):
        p[n + "_t"] = p[n].T
    p["wa"] = b(w["w_rec_a"])
    p["wi"] = b(w["w_rec_i"])
    p["wa_t"] = jnp.swapaxes(p["wa"], 1, 2)
    p["wi_t"] = jnp.swapaxes(p["wi"], 1, 2)
    return p


def _prep_late_weights(w):
    b = lambda a: a.astype(BF16)
    p = {"w_br": b(w["w_branch"][:DR]), "w_ba": b(w["w_branch"][DR:]), "w_out": b(w["w_out"]),
         "w_fi": b(w["w_ffn_in"]), "w_fo": b(w["w_ffn_out"])}
    for n in tuple(p):
        p[n + "_t"] = p[n].T
    return p


LATE = ("w_branch", "w_out", "w_ffn_in", "w_ffn_out")


def _local_step(x, tgt, w, late=None, reduce_first=None):
    S = x.shape[0]
    T = FRONT + S
    p = _prep_weights(w)
    tab = _rope_table(T)
    h0 = jnp.concatenate([jnp.zeros((PAD, D), F32), w["meta_tokens"], x], axis=0)
    row = lambda v: v.reshape(1, -1)

    z = _rmsnorm_fwd(h0, row(w["norm_mix_g"]), name="norm_mix")
    uxg = _mm(z, p["w_xg"], name="mm_uxg")
    uq = _mm(z, p["w_q"], name="mm_uq")
    ukv = _mm(z, p["w_kv"], name="mm_ukv")
    ukr = _mm(z, p["w_kr"], name="mm_ukr")
    um = _mm(z, p["w_m"], name="mm_um", out_dtype=BF16)
    rnn_w = (w["conv_w"], row(w["conv_b"]), p["wa"], row(w["b_rec_a"]), p["wi"], row(w["b_rec_i"]),
             row(w["lru_lambda"]))
    hs, y_rnn = _rnn_fwd(uxg, *rnn_w)
    qn = _rmsnorm_fwd(uq, row(w["q_norm_g"]), name="norm_q")
    kvn = _rmsnorm_fwd(ukv, row(w["kv_norm_g"]), name="norm_kv")
    q_all = _mm(qn, p["w_uq"], name="mm_q", out_dtype=BF16)
    kv_all = _mm(kvn, p["w_ukv"], name="mm_kv", out_dtype=BF16)
    qh, kh, vh = _attn_prep(q_all, kv_all, ukr, tab)
    y_att, lse, *stacks = _flash_fwd(qh, kh, vh, gather=late[0] if late else ())
    if late:
        w = {**w, **late[1](stacks)}
    p.update(_prep_late_weights(w))
    p_rnn = _mm(y_rnn, p["w_br"], name="mm_prnn", out_dtype=BF16)
    p_att = _mm(y_att, p["w_ba"], name="mm_patt", out_dtype=BF16)
    bg = row(w["b_gate"])
    mixed = _gate_mix_fwd(um, bg, p_rnn, p_att)
    h1 = _mm(mixed, p["w_out"], name="mm_out", res=h0)
    zf = _rmsnorm_fwd(h1, row(w["norm_ffn_g"]), name="norm_ffn")
    ff = _mm(zf, p["w_fi"], name="mm_ffn_in", out_dtype=BF16)
    act = _swiglu_fwd(ff)
    h2 = _mm(act, p["w_fo"], name="mm_ffn_out", res=h1)

    g = {}
    dh2, dh2b, dg_fin, lsum = _loss_head(h2, tgt, row(w["final_norm_g"]))
    loss = 0.5 * jnp.sum(lsum) / D
    g["final_norm_g"] = dg_fin.reshape(-1)
    dact = _mm(dh2b, p["w_fo_t"], name="mm_dact", out_dtype=BF16)
    g["w_ffn_out"] = _mm_tn(act, dh2b, name="mm_dw_ffn_out")
    dff = _swiglu_bwd(ff, dact)
    dzf = _mm(dff, p["w_fi_t"], name="mm_dzf")
    g["w_ffn_in"] = _mm_tn(zf, dff, name="mm_dw_ffn_in")
    dh1, dh1b, dg = _rmsnorm_bwd(h1, row(w["norm_ffn_g"]), dzf, dh2, name="norm_ffn_bwd")
    g["norm_ffn_g"] = dg
    dmixed = _mm(dh1b, p["w_out_t"], name="mm_dmixed", out_dtype=BF16)
    g["w_out"] = _mm_tn(mixed, dh1b, name="mm_dw_out")
    dp_rnn, dp_att, dum, dbg = _gate_mix_bwd(um, bg, p_rnn, p_att, dmixed)
    g["b_gate"] = dbg.reshape(2, D)
    dy_rnn = _mm(dp_rnn, p["w_br_t"], name="mm_dy_rnn")
    dy_att = _mm(dp_att, p["w_ba_t"], name="mm_dy_att", out_dtype=BF16)
    g["w_branch"] = jnp.concatenate([_mm_tn(y_rnn, dp_rnn, name="mm_dw_br"),
                                     _mm_tn(y_att, dp_att, name="mm_dw_ba")], axis=0)
    delta = _attn_delta(y_att, dy_att)
    first = reduce_first({n: g[n] for n in LATE}) if reduce_first else ()
    dq, dk, dv, *received = _flash_bwd(qh, kh, vh, dy_att, lse.reshape(NH, 1, T), delta.reshape(NH, 1, T),
                                       scatter=first)
    dq_all, dkv_all, dukr = _attn_prep_bwd(dq, dk, dv, tab)
    dqn = _mm(dq_all, p["w_uq_t"], name="mm_dqn")
    dkvn = _mm(dkv_all, p["w_ukv_t"], name="mm_dkvn")
    dwq = _mm_tn(qn, dq_all, name="mm_dw_uq").reshape(QR, NH, QW)
    dwq_rope = dwq[..., NOPE:NOPE + ROPE] + _swap_halves(dwq[..., NOPE + ROPE:])
    g["w_uq"] = jnp.concatenate([dwq[..., :NOPE], dwq_rope], axis=-1).reshape(QR, NH * (NOPE + ROPE))
    g["w_ukv"] = _mm_tn(kvn, dkv_all, name="mm_dw_ukv")
    duq, dg = _rmsnorm_bwd(uq, row(w["q_norm_g"]), dqn, None, name="norm_q_bwd", want_f32=False)
    g["q_norm_g"] = dg
    dukv, dg = _rmsnorm_bwd(ukv, row(w["kv_norm_g"]), dkvn, None, name="norm_kv_bwd", want_f32=False)
    g["kv_norm_g"] = dg
    (dux, dug, g["conv_w"], g["conv_b"], g["w_rec_a"], g["b_rec_a"], g["w_rec_i"], g["b_rec_i"],
     g["lru_lambda"]) = _rnn_bwd(uxg, hs, dy_rnn, *rnn_w, p["wa_t"], p["wi_t"])
    dz = _mm_sum([(dux, p["w_xg_t"][:DR]), (dug, p["w_xg_t"][DR:]), (duq, p["w_q_t"]), (dukv, p["w_kv_t"]),
                  (dukr, p["w_kr_t"]), (dum, p["w_m_t"])], name="mm_dz")
    dwkr = _mm_tn(z, dukr, name="mm_dw_kr")
    g["w_in"] = jnp.concatenate([
        _mm_tn(z, dux, name="mm_dw_x"), _mm_tn(z, dug, name="mm_dw_g"),
        _mm_tn(z, duq, name="mm_dw_q"), _mm_tn(z, dukv, name="mm_dw_kv"),
        dwkr[:, :ROPE] + _swap_halves(dwkr[:, ROPE:]),
        _mm_tn(z, dum, name="mm_dw_m")], axis=1)
    dh0, dg = _rmsnorm_bwd(h0, row(w["norm_mix_g"]), dz, dh1, name="norm_mix_bwd", want_bf16=False)
    g["norm_mix_g"] = dg
    g["meta_tokens"] = dh0[PAD:FRONT]
    return loss, dh0[FRONT:], g, (first, received)


HBM = pl.BlockSpec(memory_space=pltpu.HBM)
CHIP_FLIPS = ((1, 0), (0, 1), (1, 1))


def _place():
    return lax.axis_index("x"), lax.axis_index("y"), lax.axis_index("c")


def _flip(v, f):
    return 1 - v if f else v


def _dma_sems(n):
    return pltpu.SemaphoreType.DMA((n,))


def _gather_scratch(srcs):
    n = len(srcs)
    return [pltpu.VMEM(s.shape, s.dtype) for s in srcs] + [_dma_sems(3 * n), _dma_sems(3 * n), _dma_sems(n),
                                                            _dma_sems(n)]


def _gather_descs(src_refs, out_refs, scr, with_loads=True):
    n = len(src_refs)
    stage = scr[:n]
    send_sems, recv_sems, in_sems, local_sems = scr[n:]
    x, y, c = _place()
    me = 2 * x + y
    loads, sends, local = [], [], []
    for a in range(n):
        if with_loads:
            loads.append(pltpu.make_async_copy(src_refs[a], stage[a], in_sems.at[a]))
        for k, (fx, fy) in enumerate(CHIP_FLIPS):
            sends.append(pltpu.make_async_remote_copy(
                src_ref=stage[a], dst_ref=out_refs[a].at[me], send_sem=send_sems.at[3 * a + k],
                recv_sem=recv_sems.at[3 * a + k], device_id=(_flip(x, fx), _flip(y, fy), c),
                device_id_type=MESH))
        local.append(pltpu.make_async_copy(stage[a], out_refs[a].at[me], local_sems.at[a]))
    return loads, sends, local


def _gather_start(descs):
    loads, sends, local = descs
    for cp in loads:
        cp.start()
    for a, cp in enumerate(loads):
        cp.wait()
        for s in sends[3 * a:3 * a + 3]:
            s.start()
        local[a].start()


def _gather_wait(descs):
    _, sends, local = descs
    for cp in sends + local:
        cp.wait()


def _allgather_chips(srcs, *, name):
    n = len(srcs)

    def body(*refs):
        descs = _gather_descs(refs[:n], refs[n:2 * n], refs[2 * n:])
        _gather_start(descs)
        _gather_wait(descs)

    return pl.pallas_call(
        body, name=name, in_specs=[HBM] * n, out_specs=[HBM] * n,
        out_shape=[jax.ShapeDtypeStruct((4,) + s.shape, s.dtype) for s in srcs],
        scratch_shapes=_gather_scratch(srcs),
        compiler_params=pltpu.CompilerParams(vmem_limit_bytes=VMEM_LIMIT),
    )(*srcs)


def _allgather_chips_split(srcs, split, *, name):
    n = len(srcs)

    def body(*refs):
        src, out, stage = refs[:n], refs[n:2 * n], refs[2 * n:3 * n]
        send_a, recv_a, send_b, recv_b, in_sems, local_sems = refs[3 * n:]
        x, y, c = _place()
        me = 2 * x + y
        loads = [pltpu.make_async_copy(src[a], stage[a], in_sems.at[a]) for a in range(n)]
        for cp in loads:
            cp.start()

        def half(a, core):
            h = srcs[a].shape[0] // 2
            return pl.ds(pl.multiple_of(core * h, 16), h)

        ici, local = [], []
        for a in range(n):
            loads[a].wait()
            for k, (fx, fy) in enumerate(CHIP_FLIPS):
                s_ref, d_ref = stage[a], out[a].at[me]
                if split[a]:
                    s_ref, d_ref = stage[a].at[half(a, c), :], out[a].at[me, half(a, c), :]
                cp = pltpu.make_async_remote_copy(
                    src_ref=s_ref, dst_ref=d_ref, send_sem=send_a.at[3 * a + k], recv_sem=recv_a.at[3 * a + k],
                    device_id=(_flip(x, fx), _flip(y, fy), c), device_id_type=MESH)
                cp.start()
                ici.append(cp)
            cp = pltpu.make_async_copy(stage[a], out[a].at[me], local_sems.at[a])
            cp.start()
            local.append(cp)
        passed = []
        for a in range(n):
            if not split[a]:
                continue
            for k, (fx, fy) in enumerate(CHIP_FLIPS):
                ici[3 * a + k].wait_recv()
                there = 2 * _flip(x, fx) + _flip(y, fy)
                cp = pltpu.make_async_remote_copy(
                    src_ref=out[a].at[there, half(a, c), :], dst_ref=out[a].at[there, half(a, c), :],
                    send_sem=send_b.at[3 * a + k], recv_sem=recv_b.at[3 * a + k],
                    device_id=(x, y, 1 - c), device_id_type=MESH)
                cp.start()
                passed.append(cp)
        for a in range(n):
            for k in range(3):
                ici[3 * a + k].wait_send()
                if not split[a]:
                    ici[3 * a + k].wait_recv()
        for cp in passed + local:
            cp.wait()

    return pl.pallas_call(
        body, name=name, in_specs=[HBM] * n, out_specs=[HBM] * n,
        out_shape=[jax.ShapeDtypeStruct((4,) + s.shape, s.dtype) for s in srcs],
        scratch_shapes=[pltpu.VMEM(s.shape, s.dtype) for s in srcs]
        + [_dma_sems(3 * n), _dma_sems(3 * n), _dma_sems(3 * n), _dma_sems(3 * n), _dma_sems(n), _dma_sems(n)],
        compiler_params=pltpu.CompilerParams(vmem_limit_bytes=VMEM_LIMIT),
    )(*srcs)


def _scatter_descs(src_refs, out_refs, scr):
    send_sems, recv_sems = scr
    x, y, c = _place()
    copies = []
    for a in range(len(src_refs)):
        for k, (fx, fy) in enumerate(CHIP_FLIPS):
            px, py = _flip(x, fx), _flip(y, fy)
            copies.append(pltpu.make_async_remote_copy(
                src_ref=src_refs[a].at[2 * px + py], dst_ref=out_refs[a].at[k],
                send_sem=send_sems.at[3 * a + k], recv_sem=recv_sems.at[3 * a + k],
                device_id=(px, py, c), device_id_type=MESH))
    return copies


def _scatter_chips(srcs, *, name):
    n = len(srcs)

    def body(*refs):
        copies = _scatter_descs(refs[:n], refs[n:2 * n], refs[2 * n:])
        for cp in copies:
            cp.start()
        for cp in copies:
            cp.wait()

    return pl.pallas_call(
        body, name=name, in_specs=[HBM] * n, out_specs=[HBM] * n,
        out_shape=[jax.ShapeDtypeStruct((3,) + s.shape[1:], s.dtype) for s in srcs],
        scratch_shapes=[_dma_sems(3 * n), _dma_sems(3 * n)],
    )(*srcs)


def _sibling_take(srcs, *, name):
    n = len(srcs)

    def body(*refs):
        src_refs, out_refs = refs[:n], refs[n:2 * n]
        send_sems, recv_sems = refs[2 * n:]
        x, y, c = _place()
        copies = []
        for a in range(n):
            h = srcs[a].shape[1] // 2
            theirs = pl.ds(pl.multiple_of((1 - c) * h, 8), h)
            cp = pltpu.make_async_remote_copy(
                src_ref=src_refs[a].at[:, theirs, :], dst_ref=out_refs[a], send_sem=send_sems.at[a],
                recv_sem=recv_sems.at[a], device_id=(x, y, 1 - c), device_id_type=MESH)
            cp.start()
            copies.append(cp)
        for cp in copies:
            cp.wait()

    return pl.pallas_call(
        body, name=name, in_specs=[HBM] * n, out_specs=[HBM] * n,
        out_shape=[jax.ShapeDtypeStruct((4, s.shape[1] // 2, s.shape[2]), s.dtype) for s in srcs],
        scratch_shapes=[_dma_sems(n), _dma_sems(n)],
    )(*srcs)


def _sibling_swap(srcs, *, name):
    n = len(srcs)

    def body(*refs):
        src_refs, out_refs = refs[:n], refs[n:2 * n]
        send_sems, recv_sems = refs[2 * n:]
        x, y, c = _place()
        copies = []
        for a in range(n):
            cp = pltpu.make_async_remote_copy(
                src_ref=src_refs[a], dst_ref=out_refs[a], send_sem=send_sems.at[a],
                recv_sem=recv_sems.at[a], device_id=(x, y, 1 - c), device_id_type=MESH)
            cp.start()
            copies.append(cp)
        for cp in copies:
            cp.wait()

    return pl.pallas_call(
        body, name=name, in_specs=[HBM] * n, out_specs=[HBM] * n,
        out_shape=[jax.ShapeDtypeStruct(s.shape, s.dtype) for s in srcs],
        scratch_shapes=[_dma_sems(n), _dma_sems(n)],
    )(*srcs)


def _row_tile(rows, cols, n_arrays, step=16):
    budget = 24 * 1024 * 1024 // (2 * 4 * n_arrays * cols)
    best = step
    for t in range(step, rows + 1, step):
        if rows % t == 0 and t <= budget:
            best = t
    assert rows % best == 0, (rows, cols)
    return best


def _add_halves(full, theirs, core, wire, *, name):
    _, h, c = theirs.shape
    tm = _row_tile(h, c, 3)
    nb = h // tm

    def body(core_ref, a_ref, b_ref, o_ref):
        o_ref[...] = (a_ref[...] + b_ref[...]).astype(wire)

    spec = pl.BlockSpec((None, tm, c), lambda s, i, core_ref: (s, i, 0))
    grid_spec = pltpu.PrefetchScalarGridSpec(
        num_scalar_prefetch=1, grid=(4, nb),
        in_specs=[pl.BlockSpec((None, tm, c), lambda s, i, core_ref: (s, core_ref[0] * nb + i, 0)), spec],
        out_specs=spec)
    return pl.pallas_call(
        body, name=name, grid_spec=grid_spec, out_shape=jax.ShapeDtypeStruct(theirs.shape, wire),
        compiler_params=_cparams(("parallel", "parallel")),
    )(core.reshape(1), full, theirs)


def _sum4(own, recv, *, name):
    h, c = own.shape
    tm = _row_tile(h, c, 5)

    def body(o_ref, r_ref, out_ref):
        f = lambda k: r_ref[k].astype(F32)
        out_ref[...] = ((o_ref[...].astype(F32) + f(0)) + f(1)) + f(2)

    return pl.pallas_call(
        body, name=name, grid=(h // tm,),
        in_specs=[_rows(tm, c), pl.BlockSpec((3, tm, c), lambda i: (0, i, 0))],
        out_specs=_rows(tm, c), out_shape=jax.ShapeDtypeStruct((h, c), F32),
        compiler_params=_cparams(("parallel",)),
    )(own, recv)


def _adamw(g, w, m, v, *, name):
    r, c = g.shape
    tm = _row_tile(r, c, 7, step=8)
    c1 = 1.0 / (1.0 - ADAM_B1 ** ADAM_STEP)
    c2 = 1.0 / (1.0 - ADAM_B2 ** ADAM_STEP)

    def body(g_ref, w_ref, m_ref, v_ref, d_ref, nm_ref, nv_ref):
        gv = g_ref[...]
        nm = ADAM_B1 * m_ref[...] + (1.0 - ADAM_B1) * gv
        nv = ADAM_B2 * v_ref[...] + (1.0 - ADAM_B2) * (gv * gv)
        nm_ref[...] = nm
        nv_ref[...] = nv
        d_ref[...] = -ADAM_LR * ((nm * c1) / (jnp.sqrt(nv * c2) + ADAM_EPS) + ADAM_WD * w_ref[...])

    spec = _rows(tm, c)
    shape = jax.ShapeDtypeStruct((r, c), F32)
    return pl.pallas_call(
        body, name=name, grid=(r // tm,), in_specs=[spec] * 4, out_specs=[spec] * 3,
        out_shape=[shape] * 3, compiler_params=_cparams(("parallel",)),
    )(g, w, m, v)


def _adamw_halves(mine, theirs, core, w, m, v, *, name):
    h, c = mine.shape
    tm = _row_tile(h, c, 10, step=8)
    nb = h // tm
    c1 = 1.0 / (1.0 - ADAM_B1 ** ADAM_STEP)
    c2 = 1.0 / (1.0 - ADAM_B2 ** ADAM_STEP)

    def body(core_ref, a_ref, b_ref, w_ref, m_ref, v_ref, g_ref, d_ref, nm_ref, nv_ref):
        gv = jnp.where(pl.program_id(0) // nb == core_ref[0], a_ref[...], b_ref[...])
        nm = ADAM_B1 * m_ref[...] + (1.0 - ADAM_B1) * gv
        nv = ADAM_B2 * v_ref[...] + (1.0 - ADAM_B2) * (gv * gv)
        g_ref[...] = gv
        nm_ref[...] = nm
        nv_ref[...] = nv
        d_ref[...] = -ADAM_LR * ((nm * c1) / (jnp.sqrt(nv * c2) + ADAM_EPS) + ADAM_WD * w_ref[...])

    half = pl.BlockSpec((tm, c), lambda i, core_ref: (i % nb, 0))
    spec = pl.BlockSpec((tm, c), lambda i, core_ref: (i, 0))
    grid_spec = pltpu.PrefetchScalarGridSpec(
        num_scalar_prefetch=1, grid=(2 * nb,), in_specs=[half, half, spec, spec, spec], out_specs=[spec] * 4)
    return pl.pallas_call(
        body, name=name, grid_spec=grid_spec, out_shape=[jax.ShapeDtypeStruct((2 * h, c), F32)] * 4,
        compiler_params=_cparams(("parallel",)),
    )(core.reshape(1), mine, theirs, w, m, v)


BIG = (("w_in", (D, 1328), 1), ("w_uq", (QR, 384), 1), ("w_ukv", (KVR, 512), 1), ("w_branch", (576, D), 0),
       ("w_out", (256, D), 0), ("w_ffn_in", (D, 1408), 1), ("w_ffn_out", (704, D), 0))
SMALL = (("meta_tokens", (NMETA, 256), 1), ("b_gate", (2, 256), 1), ("conv_w", (CW, 320), 1))
REPL = (("norm_mix_g", (D,)), ("conv_b", (DR,)), ("w_rec_a", (NBLK, RB, RB)), ("b_rec_a", (DR,)),
        ("w_rec_i", (NBLK, RB, RB)), ("b_rec_i", (DR,)), ("lru_lambda", (DR,)), ("q_norm_g", (QR,)),
        ("kv_norm_g", (KVR,)), ("norm_ffn_g", (D,)), ("final_norm_g", (D,)))
WEIGHTS = ("meta_tokens", "norm_mix_g", "w_in", "b_gate", "conv_w", "conv_b", "w_rec_a", "b_rec_a", "w_rec_i",
           "b_rec_i", "lru_lambda", "q_norm_g", "w_uq", "kv_norm_g", "w_ukv", "w_branch", "w_out", "norm_ffn_g",
           "w_ffn_in", "w_ffn_out", "final_norm_g")
W = 1024
SMALL_N = sum(math.prod(s) for _, s, _ in SMALL)
SMALL_ROWS = 8
REPL_N = sum(math.prod(s) for _, s in REPL)
QUART_ROWS = 88
assert SMALL_N <= SMALL_ROWS * W and REPL_N <= 4 * QUART_ROWS * W


def _flat_pad(parts, rows):
    v = jnp.concatenate([p.reshape(-1) for p in parts])
    return jnp.pad(v, (0, rows * W - v.shape[0])).reshape(rows, W)


def _shard_stack(full, shard_shape, axis):
    r, cs = shard_shape
    if axis == 0:
        return full.reshape(4, r, cs)
    return jnp.stack([full[:, s * cs:(s + 1) * cs] for s in range(4)])


def _unshard(stack, axis):
    if axis == 0:
        return stack.reshape(4 * stack.shape[1], stack.shape[2])
    return jnp.concatenate([stack[s] for s in range(4)], axis=1)


def _split(flat, table):
    out, off = {}, 0
    for name, shape, *_ in table:
        n = math.prod(shape)
        out[name] = flat[..., off:off + n].reshape(flat.shape[:-1] + tuple(shape))
        off += n
    return out


def kernel(x, meta_tokens, norm_mix_g, w_in, b_gate, conv_w, conv_b, w_rec_a, b_rec_a, w_rec_i, b_rec_i, lru_lambda, q_norm_g, w_uq, kv_norm_g, w_ukv, w_branch, w_out, norm_ffn_g, w_ffn_in, w_ffn_out, final_norm_g, loss_target, m_meta_tokens, m_norm_mix_g, m_w_in, m_b_gate, m_conv_w, m_conv_b, m_w_rec_a, m_b_rec_a, m_w_rec_i, m_b_rec_i, m_lru_lambda, m_q_norm_g, m_w_uq, m_kv_norm_g, m_w_ukv, m_w_branch, m_w_out, m_norm_ffn_g, m_w_ffn_in, m_w_ffn_out, m_final_norm_g, v_meta_tokens, v_norm_mix_g, v_w_in, v_b_gate, v_conv_w, v_conv_b, v_w_rec_a, v_b_rec_a, v_w_rec_i, v_b_rec_i, v_lru_lambda, v_q_norm_g, v_w_uq, v_kv_norm_g, v_w_ukv, v_w_branch, v_w_out, v_norm_ffn_g, v_w_ffn_in, v_w_ffn_out, v_final_norm_g):
    args = dict(locals())
    chip = 2 * lax.axis_index("x") + lax.axis_index("y")
    core = lax.axis_index("c")

    first_big = [b for b in BIG if b[0] not in LATE]
    late_big = [b for b in BIG if b[0] in LATE]
    bf16_shard = lambda n, s: args[n].reshape(s).astype(BF16)
    small = _flat_pad([args[n] for n, _, _ in SMALL], SMALL_ROWS)
    gathered = _allgather_chips_split([bf16_shard(n, s) for n, s, _ in first_big] + [small],
                                      [True] * len(first_big) + [False], name="gather_weights")
    w = {}
    for (name, _, axis), stack in zip(first_big, gathered):
        w[name] = _unshard(stack, axis)
    small_parts = _split(gathered[-1].reshape(4, SMALL_ROWS * W), SMALL)
    for name, _, axis in SMALL:
        w[name] = _unshard(small_parts[name], axis)
    for name, shape in REPL:
        w[name] = args[name].reshape(shape)
    finish_late = lambda stacks: {name: _unshard(st, axis) for (name, _, axis), st in zip(late_big, stacks)}

    def to_wire(red, tag, wires):
        theirs = _sibling_take(red, name="reduce_sibling_" + tag)
        return [_add_halves(a, t, core, wires[k], name=f"add_sibling_{tag}{k}")
                for k, (a, t) in enumerate(zip(red, theirs))]

    reduce_first = lambda gl: to_wire([_shard_stack(gl[n], s, a) for n, s, a in late_big], "a",
                                      [BF16] * len(late_big))
    loss, grad_x, g, (parts_a, recv_a) = _local_step(
        x[0], loss_target[0], w, late=([bf16_shard(n, s) for n, s, _ in late_big], finish_late),
        reduce_first=reduce_first)
    loss = lax.psum(loss, ("x", "y", "c"))

    red = [_shard_stack(g[n], s, a) for n, s, a in first_big]
    small_g = jnp.concatenate([_shard_stack(g[n], s, a).reshape(4, -1) for n, s, a in SMALL], axis=1)
    small_g = jnp.pad(small_g, ((0, 0), (0, SMALL_ROWS * W - SMALL_N))).reshape(4, SMALL_ROWS, W)
    repl_g = _flat_pad([g[n] for n, _ in REPL], 4 * QUART_ROWS).reshape(4, QUART_ROWS, W)
    red.append(jnp.concatenate([small_g, repl_g], axis=1))
    parts_b = to_wire(red, "b", [BF16] * len(first_big) + [F32])
    recv_b = _scatter_chips(parts_b, name="reduce_chips")
    order = [b[0] for b in late_big] + [b[0] for b in first_big] + ["misc"]
    halves = [_sum4(lax.dynamic_index_in_dim(p, chip, 0, keepdims=False), r, name="sum_chips_" + n)
              for n, p, r in zip(order, list(parts_a) + parts_b, list(recv_a) + list(recv_b))]
    others = _sibling_swap(halves, name="share_sibling")

    results = {}
    shape_of = {name: shape for name, shape, _ in BIG}
    for name, mine, theirs in zip(order[:-1], halves, others):
        shape = shape_of[name]
        results[name] = _adamw_halves(mine, theirs, core, args[name].reshape(shape),
                                      args["m_" + name].reshape(shape), args["v_" + name].reshape(shape),
                                      name="adamw_" + name)

    a, b = halves[-1], others[-1]
    g_mine = jnp.where(core == 0, jnp.concatenate([a, b], axis=0), jnp.concatenate([b, a], axis=0))
    g_repl = _allgather_chips([g_mine[SMALL_ROWS:]], name="gather_repl")[0].reshape(4 * QUART_ROWS, W)
    g_misc = jnp.concatenate([g_mine[:SMALL_ROWS], g_repl], axis=0)
    misc_state = lambda prefix: jnp.concatenate(
        [_flat_pad([args[prefix + n] for n, _, _ in SMALL], SMALL_ROWS),
         _flat_pad([args[prefix + n] for n, _ in REPL], 4 * QUART_ROWS)], axis=0)
    d, nm, nv = _adamw(g_misc, misc_state(""), misc_state("m_"), misc_state("v_"), name="adamw_misc")
    misc = (g_misc, d, nm, nv)

    outs = []
    for k in range(4):
        sm = _split(misc[k][:SMALL_ROWS].reshape(-1), SMALL)
        rp = _split(misc[k][SMALL_ROWS:].reshape(-1), REPL)
        for name in WEIGHTS:
            val = results[name][k] if name in results else (sm[name] if name in sm else rp[name])
            outs.append(val.reshape(args[name].shape))
    return (loss, grad_x[None], *outs)
```

```python
import functools
import math

import jax
import jax.numpy as jnp
from jax import lax
from jax.experimental import pallas as pl
from jax.experimental.pallas import tpu as pltpu

F32 = jnp.float32
BF16 = jnp.bfloat16

D = 1024
DR = 1280
NBLK = 10
RB = 128
CW = 4
NH = 8
NOPE = 128
ROPE = 64
VD = 128
QR = 384
KVR = 256
DFF = 2816
NMETA = 16
EPS = 1e-6
LRU_C = 8.0
ROPE_THETA = 10000.0
SCALE = 1.0 / math.sqrt(NOPE + ROPE)
NEG = -1e30
FRONT = 128
PAD = FRONT - NMETA
QW = 2 * NOPE
LANES = 128
SUB = 128
CHAINS = 4
VMEM_LIMIT = 52 * 1024 * 1024

ADAM_LR = 0.001
ADAM_B1 = 0.9
ADAM_B2 = 0.999
ADAM_EPS = 1e-08
ADAM_WD = 0.01
ADAM_STEP = 10

MESH = pl.DeviceIdType.MESH


def _cparams(sem):
    return pltpu.CompilerParams(dimension_semantics=sem, vmem_limit_bytes=VMEM_LIMIT)


def _sigmoid(x):
    return 1.0 / (1.0 + jnp.exp(-x))


def _gelu_parts(x):
    c = math.sqrt(2.0 / math.pi)
    inner = c * (x + 0.044715 * x * x * x)
    t = jnp.tanh(inner)
    g = 0.5 * x * (1.0 + t)
    dg = 0.5 * (1.0 + t) + 0.5 * x * (1.0 - t * t) * c * (1.0 + 3.0 * 0.044715 * x * x)
    return g, dg


def _divisors(n, step, cap):
    return [d for d in range(step, min(n, cap) + 1, step) if n % d == 0] or [n]


MM_VMEM_BUDGET = 40 * 1024 * 1024
MM_MAX_ROWS = 1664
MM_MAX_COLS = 1408


def _mm_tiles(M, K, N, a_item, out_item, has_res):
    best = None
    for tn in _divisors(N, LANES, MM_MAX_COLS):
        for tm in _divisors(M, 16, MM_MAX_ROWS):
            need = 2 * (tm * K * a_item + K * tn * 2 + tm * tn * (out_item + (4 if has_res else 0)))
            if need <= MM_VMEM_BUDGET and (best is None or tm * tn > best[0] * best[1]):
                best = (tm, tn)
    assert best is not None, (M, K, N)
    return best


def _mm(a, b, *, name, out_dtype=F32, res=None):
    M, K = a.shape
    N = b.shape[1]
    has_res = res is not None
    tm, tn = _mm_tiles(M, K, N, a.dtype.itemsize, jnp.dtype(out_dtype).itemsize, has_res)

    def body(*refs):
        if has_res:
            a_ref, b_ref, r_ref, o_ref = refs
        else:
            a_ref, b_ref, o_ref = refs
        acc = jnp.dot(a_ref[...].astype(BF16), b_ref[...].astype(BF16), preferred_element_type=F32)
        if has_res:
            acc = acc + r_ref[...].astype(F32)
        o_ref[...] = acc.astype(o_ref.dtype)

    a_bytes = M * K * a.dtype.itemsize
    b_bytes = K * N * b.dtype.itemsize
    rows_outer = a_bytes + (M // tm) * b_bytes <= b_bytes + (N // tn) * a_bytes
    if rows_outer:
        grid = (M // tm, N // tn)
        ia, ib, io = (lambda i, j: (i, 0)), (lambda i, j: (0, j)), (lambda i, j: (i, j))
    else:
        grid = (N // tn, M // tm)
        ia, ib, io = (lambda j, i: (i, 0)), (lambda j, i: (0, j)), (lambda j, i: (i, j))
    in_specs = [pl.BlockSpec((tm, K), ia), pl.BlockSpec((K, tn), ib)]
    args = [a, b]
    if has_res:
        in_specs.append(pl.BlockSpec((tm, tn), io))
        args.append(res)
    return pl.pallas_call(
        body, name=name, grid=grid, in_specs=in_specs,
        out_specs=pl.BlockSpec((tm, tn), io),
        out_shape=jax.ShapeDtypeStruct((M, N), out_dtype),
        compiler_params=_cparams(("parallel", "parallel")),
    )(*args)


def _mm_sum(pairs, *, name, scatter=(), out_dtype=F32):
    M = pairs[0][0].shape[0]
    N = pairs[0][1].shape[1]
    ks = [a.shape[1] for a, _ in pairs]
    tm, tn = _mm_tiles(M, sum(ks), N, 2, jnp.dtype(out_dtype).itemsize, False)
    n = len(pairs)
    ns = len(scatter)
    gm, gn = M // tm, N // tn

    def body(*refs):
        s_src = refs[2 * n:2 * n + ns]
        o_ref = refs[2 * n + ns]
        s_out = refs[2 * n + ns + 1:2 * n + 2 * ns + 1]
        s_scr = refs[2 * n + 2 * ns + 1:]
        i, j = pl.program_id(0), pl.program_id(1)
        if ns:
            @pl.when((i == 0) & (j == 0))
            def _():
                for cp in _scatter_descs(s_src, s_out, s_scr):
                    cp.start()

        acc = None
        for k in range(n):
            d = jnp.dot(refs[2 * k][...].astype(BF16), refs[2 * k + 1][...].astype(BF16),
                        preferred_element_type=F32)
            acc = d if acc is None else acc + d
        o_ref[...] = acc.astype(o_ref.dtype)

        if ns:
            @pl.when((i == gm - 1) & (j == gn - 1))
            def _():
                for cp in _scatter_descs(s_src, s_out, s_scr):
                    cp.wait()

    in_specs, args = [], []
    for (a, b), kk in zip(pairs, ks):
        in_specs += [pl.BlockSpec((tm, kk), lambda i, j: (i, 0)), pl.BlockSpec((kk, tn), lambda i, j: (0, j))]
        args += [a, b]
    return pl.pallas_call(
        body, name=name, grid=(gm, gn), in_specs=in_specs + [HBM] * ns,
        out_specs=[pl.BlockSpec((tm, tn), lambda i, j: (i, j))] + [HBM] * ns,
        out_shape=[jax.ShapeDtypeStruct((M, N), out_dtype)]
        + [jax.ShapeDtypeStruct((3,) + s.shape[1:], s.dtype) for s in scatter],
        scratch_shapes=[_dma_sems(3 * ns), _dma_sems(3 * ns)] if ns else [],
        compiler_params=_cparams(("arbitrary", "arbitrary")),
    )(*args, *scatter)


def _mm_tn(a, b, *, name):
    T, K1 = a.shape
    N = b.shape[1]
    tt = _divisors(T, 16, MM_MAX_ROWS)[-1]
    tk = _divisors(K1, LANES, MM_MAX_COLS)[-1]
    tn = _divisors(N, LANES, MM_MAX_COLS)[-1]

    def body(a_ref, b_ref, o_ref):
        @pl.when(pl.program_id(2) == 0)
        def _():
            o_ref[...] = jnp.zeros_like(o_ref)

        o_ref[...] += lax.dot_general(a_ref[...].astype(BF16), b_ref[...].astype(BF16),
                                      (((0,), (0,)), ((), ())), preferred_element_type=F32)

    return pl.pallas_call(
        body, name=name, grid=(K1 // tk, N // tn, T // tt),
        in_specs=[pl.BlockSpec((tt, tk), lambda i, j, t: (t, i)),
                  pl.BlockSpec((tt, tn), lambda i, j, t: (t, j))],
        out_specs=pl.BlockSpec((tk, tn), lambda i, j, t: (i, j)),
        out_shape=jax.ShapeDtypeStruct((K1, N), F32),
        compiler_params=_cparams(("parallel", "parallel", "arbitrary")),
    )(a, b)


def _rows(tm, w, cb=0):
    return pl.BlockSpec((tm, w), lambda i: (i, cb))


def _const(shape):
    n = len(shape)
    return pl.BlockSpec(shape, lambda i: (0,) * n)


def _rmsnorm_fwd(x, g, *, name, tm=640):
    T, C = x.shape

    def body(x_ref, g_ref, o_ref):
        xv = x_ref[...]
        r = lax.rsqrt(jnp.mean(xv * xv, axis=-1, keepdims=True) + EPS)
        o_ref[...] = ((xv * r) * g_ref[...]).astype(BF16)

    return pl.pallas_call(
        body, name=name, grid=(T // tm,),
        in_specs=[_rows(tm, C), _const((1, C))],
        out_specs=_rows(tm, C),
        out_shape=jax.ShapeDtypeStruct((T, C), BF16),
        compiler_params=_cparams(("parallel",)),
    )(x, g)


def _rmsnorm_bwd(x, g, dy, res, *, name, tm=640, want_f32=True, want_bf16=True):
    T, C = x.shape
    has_res = res is not None

    def body(*refs):
        refs = list(refs)
        x_ref, g_ref, dy_ref = refs[:3]
        refs = refs[3:]
        r_ref = refs.pop(0) if has_res else None
        o32 = refs.pop(0) if want_f32 else None
        o16 = refs.pop(0) if want_bf16 else None
        dg_ref = refs.pop(0)

        @pl.when(pl.program_id(0) == 0)
        def _():
            dg_ref[...] = jnp.zeros_like(dg_ref)

        xv = x_ref[...]
        dyv = dy_ref[...].astype(F32)
        r = lax.rsqrt(jnp.mean(xv * xv, axis=-1, keepdims=True) + EPS)
        xn = xv * r
        dg_ref[...] += jnp.sum(dyv * xn, axis=0, keepdims=True)
        dxn = dyv * g_ref[...]
        dx = r * (dxn - xn * jnp.mean(dxn * xn, axis=-1, keepdims=True))
        if has_res:
            dx = dx + r_ref[...]
        if want_f32:
            o32[...] = dx
        if want_bf16:
            o16[...] = dx.astype(BF16)

    in_specs = [_rows(tm, C), _const((1, C)), _rows(tm, C)]
    args = [x, g, dy]
    if has_res:
        in_specs.append(_rows(tm, C))
        args.append(res)
    out_specs, out_shape = [], []
    if want_f32:
        out_specs.append(_rows(tm, C))
        out_shape.append(jax.ShapeDtypeStruct((T, C), F32))
    if want_bf16:
        out_specs.append(_rows(tm, C))
        out_shape.append(jax.ShapeDtypeStruct((T, C), BF16))
    out_specs.append(_const((1, C)))
    out_shape.append(jax.ShapeDtypeStruct((1, C), F32))
    return pl.pallas_call(
        body, name=name, grid=(T // tm,), in_specs=in_specs, out_specs=out_specs,
        out_shape=out_shape, compiler_params=_cparams(("arbitrary",)),
    )(*args)


def _gate_mix_fwd(um, bg, p_rnn, p_att, *, tm=320):
    T = um.shape[0]

    def body(um_ref, bg_ref, pr_ref, pa_ref, o_ref):
        g = _sigmoid(um_ref[...].astype(F32) + bg_ref[...])
        o_ref[...] = (g[:, :D] * pr_ref[...].astype(F32) + g[:, D:] * pa_ref[...].astype(F32)).astype(BF16)

    return pl.pallas_call(
        body, name="gate_mix_fwd", grid=(T // tm,),
        in_specs=[_rows(tm, 2 * D), _const((1, 2 * D)), _rows(tm, D), _rows(tm, D)],
        out_specs=_rows(tm, D),
        out_shape=jax.ShapeDtypeStruct((T, D), BF16),
        compiler_params=_cparams(("parallel",)),
    )(um, bg, p_rnn, p_att)


def _gate_mix_bwd(um, bg, p_rnn, p_att, dmixed, *, tm=320):
    T = um.shape[0]

    def body(um_ref, bg_ref, pr_ref, pa_ref, dm_ref, dpr_ref, dpa_ref, dum_ref, dbg_ref):
        @pl.when(pl.program_id(0) == 0)
        def _():
            dbg_ref[...] = jnp.zeros_like(dbg_ref)

        g = _sigmoid(um_ref[...].astype(F32) + bg_ref[...])
        g0, g1 = g[:, :D], g[:, D:]
        dm = dm_ref[...].astype(F32)
        dpr_ref[...] = (dm * g0).astype(BF16)
        dpa_ref[...] = (dm * g1).astype(BF16)
        d0 = dm * pr_ref[...].astype(F32) * g0 * (1.0 - g0)
        d1 = dm * pa_ref[...].astype(F32) * g1 * (1.0 - g1)
        dum_ref[:, :D] = d0.astype(BF16)
        dum_ref[:, D:] = d1.astype(BF16)
        dbg_ref[:, :D] += jnp.sum(d0, axis=0, keepdims=True)
        dbg_ref[:, D:] += jnp.sum(d1, axis=0, keepdims=True)

    return pl.pallas_call(
        body, name="gate_mix_bwd", grid=(T // tm,),
        in_specs=[_rows(tm, 2 * D), _const((1, 2 * D)), _rows(tm, D), _rows(tm, D), _rows(tm, D)],
        out_specs=[_rows(tm, D), _rows(tm, D), _rows(tm, 2 * D), _const((1, 2 * D))],
        out_shape=[jax.ShapeDtypeStruct((T, D), BF16), jax.ShapeDtypeStruct((T, D), BF16),
                   jax.ShapeDtypeStruct((T, 2 * D), BF16), jax.ShapeDtypeStruct((1, 2 * D), F32)],
        compiler_params=_cparams(("arbitrary",)),
    )(um, bg, p_rnn, p_att, dmixed)


def _swiglu_fwd(ff, *, tm=320):
    T = ff.shape[0]

    def body(g_ref, u_ref, o_ref):
        gv = g_ref[...].astype(F32)
        o_ref[...] = (gv * _sigmoid(gv) * u_ref[...].astype(F32)).astype(BF16)

    return pl.pallas_call(
        body, name="swiglu_fwd", grid=(T // tm,),
        in_specs=[_rows(tm, DFF, 0), _rows(tm, DFF, 1)],
        out_specs=_rows(tm, DFF),
        out_shape=jax.ShapeDtypeStruct((T, DFF), BF16),
        compiler_params=_cparams(("parallel",)),
    )(ff, ff)


def _swiglu_bwd(ff, dact, *, tm=320):
    T = ff.shape[0]

    def body(g_ref, u_ref, da_ref, o_ref):
        gv = g_ref[...].astype(F32)
        s = _sigmoid(gv)
        da = da_ref[...].astype(F32)
        o_ref[:, :DFF] = (da * u_ref[...].astype(F32) * s * (1.0 + gv * (1.0 - s))).astype(BF16)
        o_ref[:, DFF:] = (da * gv * s).astype(BF16)

    return pl.pallas_call(
        body, name="swiglu_bwd", grid=(T // tm,),
        in_specs=[_rows(tm, DFF, 0), _rows(tm, DFF, 1), _rows(tm, DFF)],
        out_specs=_rows(tm, 2 * DFF),
        out_shape=jax.ShapeDtypeStruct((T, 2 * DFF), BF16),
        compiler_params=_cparams(("parallel",)),
    )(ff, ff, dact)


def _loss_head(h2, tgt, g, *, tm=640):
    T = h2.shape[0]
    nsub = tm // FRONT

    def body(h_ref, *refs):
        t_refs = refs[:nsub]
        g_ref, d32_ref, d16_ref, dg_ref, ls_ref = refs[nsub:]
        i = pl.program_id(0)

        @pl.when(i == 0)
        def _():
            dg_ref[...] = jnp.zeros_like(dg_ref)
            ls_ref[...] = jnp.zeros_like(ls_ref)

        gv = g_ref[...]
        for k in range(nsub):
            rows = pl.ds(k * FRONT, FRONT)
            xv = h_ref[rows, :]
            r = lax.rsqrt(jnp.mean(xv * xv, axis=-1, keepdims=True) + EPS)
            xn = xv * r
            e = jnp.where(i * nsub + k >= 1, xn * gv - t_refs[k][...], 0.0)
            ls_ref[...] += jnp.sum(e * e, axis=0, keepdims=True)
            dy = e * (1.0 / D)
            dg_ref[...] += jnp.sum(dy * xn, axis=0, keepdims=True)
            dxn = dy * gv
            dx = r * (dxn - xn * jnp.mean(dxn * xn, axis=-1, keepdims=True))
            d32_ref[rows, :] = dx
            d16_ref[rows, :] = dx.astype(BF16)

    def t_spec(k):
        return pl.BlockSpec((FRONT, D), lambda i: (jnp.maximum(i * nsub + k - 1, 0), 0))

    return pl.pallas_call(
        body, name="loss_head", grid=(T // tm,),
        in_specs=[_rows(tm, D)] + [t_spec(k) for k in range(nsub)] + [_const((1, D))],
        out_specs=[_rows(tm, D), _rows(tm, D), _const((1, D)), _const((1, D))],
        out_shape=[jax.ShapeDtypeStruct((T, D), F32), jax.ShapeDtypeStruct((T, D), BF16),
                   jax.ShapeDtypeStruct((1, D), F32), jax.ShapeDtypeStruct((1, D), F32)],
        compiler_params=_cparams(("arbitrary",)),
    )(h2, *([tgt] * nsub), g)


def _scan_fwd(a, b, h_in):
    n = a.shape[0]
    row = lax.broadcasted_iota(jnp.int32, a.shape, 0)
    s = 1
    while s < n:
        if s % 8:
            a_sh = jnp.where(row >= s, pltpu.roll(a, s, 0), 1.0)
            b_sh = jnp.where(row >= s, pltpu.roll(b, s, 0), 0.0)
        else:
            a_sh = jnp.concatenate([jnp.ones((s, RB), F32), a[:n - s]], axis=0)
            b_sh = jnp.concatenate([jnp.zeros((s, RB), F32), b[:n - s]], axis=0)
        b = a * b_sh + b
        a = a * a_sh
        s *= 2
    return b + a * h_in


def _scan_rev(a, b, g_in):
    n = a.shape[0]
    row = lax.broadcasted_iota(jnp.int32, a.shape, 0)
    s = 1
    while s < n:
        if s % 8:
            a_sh = jnp.where(row < n - s, pltpu.roll(a, n - s, 0), 1.0)
            b_sh = jnp.where(row < n - s, pltpu.roll(b, n - s, 0), 0.0)
        else:
            a_sh = jnp.concatenate([a[s:], jnp.ones((s, RB), F32)], axis=0)
            b_sh = jnp.concatenate([b[s:], jnp.zeros((s, RB), F32)], axis=0)
        b = a * b_sh + b
        a = a * a_sh
        s *= 2
    return b + a * g_in


def _lru_gates(xc, wa, ba, wi, bi, lam):
    xcb = xc.astype(BF16)
    r = _sigmoid(jnp.dot(xcb, wa, preferred_element_type=F32) + ba)
    ig = _sigmoid(jnp.dot(xcb, wi, preferred_element_type=F32) + bi)
    log_sig = jnp.minimum(lam, 0.0) - jnp.log(1.0 + jnp.exp(-jnp.abs(lam)))
    log_a = LRU_C * r * log_sig
    a = jnp.exp(log_a)
    m2 = jnp.tanh(-log_a) * (1.0 + a * a)
    return r, ig, log_sig, a, m2 * lax.rsqrt(jnp.maximum(m2, 1e-37))


def _rnn_specs(tc, nblk_t, rev):
    def tmap(k):
        return (nblk_t - 1 - k) if rev else k

    hb = tc // 8
    blk = lambda off: pl.BlockSpec((tc, RB), lambda c, k: (tmap(k), c + off))
    halo = lambda off: pl.BlockSpec((8, RB), lambda c, k: (jnp.maximum(tmap(k) * hb - 1, 0), c + off))
    vec = pl.BlockSpec((1, RB), lambda c, k: (0, c))
    cwv = pl.BlockSpec((CW, RB), lambda c, k: (0, c))
    mat = pl.BlockSpec((None, RB, RB), lambda c, k: (c, 0, 0))
    return blk, halo, vec, cwv, mat


def _rnn_fwd(uxg, cw, cb, wa, ba, wi, bi, lam, *, tc=640):
    T = uxg.shape[0]
    nt = T // tc
    nsub = tc // SUB
    blk, halo, vec, cwv, mat = _rnn_specs(tc, nt, False)

    def body(x_ref, xh_ref, ug_ref, cw_ref, cb_ref, wa_ref, ba_ref, wi_ref, bi_ref, lam_ref,
             h_ref, y_ref, xb, hc):
        k = pl.program_id(1)

        @pl.when(k == 0)
        def _():
            hc[...] = jnp.zeros_like(hc)

        xb[0:8, :] = jnp.where(k > 0, xh_ref[...], 0.0)
        xb[8:, :] = x_ref[...]
        cwv_, cbv = cw_ref[...], cb_ref[...]
        wav, wiv = wa_ref[...], wi_ref[...]
        bav, biv, lamv = ba_ref[...], bi_ref[...], lam_ref[...]
        h_in = hc[0:1, :]
        for sc in range(nsub):
            r0 = sc * SUB
            xc = cbv + cwv_[0:1, :] * xb[pl.ds(5 + r0, SUB), :]
            for j in range(1, CW):
                xc = xc + cwv_[j:j + 1, :] * xb[pl.ds(5 + j + r0, SUB), :]
            r, ig, _, a, mm = _lru_gates(xc, wav, bav, wiv, biv, lamv)
            rows = k * tc + r0 + lax.broadcasted_iota(jnp.int32, (SUB, RB), 0)
            b = jnp.where(rows >= PAD, mm * (ig * xc), 0.0)
            h = _scan_fwd(a, b, h_in)
            h_in = h[SUB - 1:SUB, :]
            h_ref[pl.ds(r0, SUB), :] = h
            gl, _ = _gelu_parts(ug_ref[pl.ds(r0, SUB), :])
            y_ref[pl.ds(r0, SUB), :] = (h * gl).astype(BF16)
        hc[0:1, :] = h_in

    return pl.pallas_call(
        body, name="rnn_fwd", grid=(NBLK, nt),
        in_specs=[blk(0), halo(0), blk(NBLK), cwv, vec, mat, vec, mat, vec, vec],
        out_specs=[blk(0), blk(0)],
        out_shape=[jax.ShapeDtypeStruct((T, DR), F32), jax.ShapeDtypeStruct((T, DR), BF16)],
        scratch_shapes=[pltpu.VMEM((tc + 8, RB), F32), pltpu.VMEM((8, RB), F32)],
        compiler_params=_cparams(("parallel", "arbitrary")),
    )(uxg, uxg, uxg, cw, cb, wa, ba, wi, bi, lam)


def _rnn_bwd(uxg, hs, dy, cw, cb, wa, ba, wi, bi, lam, wat, wit, *, tc=640):
    T = uxg.shape[0]
    nt = T // tc
    nsub = tc // SUB
    blk, halo, vec, cwv, mat = _rnn_specs(tc, nt, True)

    def body(x_ref, xh_ref, ug_ref, h_ref, hh_ref, dy_ref, cw_ref, cb_ref, wa_ref, ba_ref, wi_ref,
             bi_ref, lam_ref, wat_ref, wit_ref,
             dux_ref, dug_ref, dcw_ref, dcb_ref, dwa_ref, dba_ref, dwi_ref, dbi_ref, dlam_ref,
             xb, hb, ab, dxb, xcs, rs, igs, mms, dgas, dgis, carry):
        k = pl.program_id(1)
        kt = nt - 1 - k

        @pl.when(k == 0)
        def _():
            carry[...] = jnp.zeros_like(carry)
            for ref in (dcw_ref, dcb_ref, dwa_ref, dba_ref, dwi_ref, dbi_ref, dlam_ref):
                ref[...] = jnp.zeros_like(ref)

        xb[0:8, :] = jnp.where(kt > 0, xh_ref[...], 0.0)
        xb[8:, :] = x_ref[...]
        hb[0:8, :] = jnp.where(kt > 0, hh_ref[...], 0.0)
        hb[8:, :] = h_ref[...]
        cwv_, cbv = cw_ref[...], cb_ref[...]
        wav, wiv = wa_ref[...], wi_ref[...]
        bav, biv, lamv = ba_ref[...], bi_ref[...], lam_ref[...]
        ab[tc:tc + 8, :] = jnp.broadcast_to(carry[1:2, :], (8, RB))
        dxb[tc:tc + 8, :] = carry[8:16, :]
        log_sig = None
        for sc in range(nsub):
            r0 = sc * SUB
            xc = cbv + cwv_[0:1, :] * xb[pl.ds(5 + r0, SUB), :]
            for j in range(1, CW):
                xc = xc + cwv_[j:j + 1, :] * xb[pl.ds(5 + j + r0, SUB), :]
            r, ig, log_sig, a, mm = _lru_gates(xc, wav, bav, wiv, biv, lamv)
            xcs[pl.ds(r0, SUB), :] = xc
            rs[pl.ds(r0, SUB), :] = r
            igs[pl.ds(r0, SUB), :] = ig
            mms[pl.ds(r0, SUB), :] = mm
            ab[pl.ds(r0, SUB), :] = a
        sig_neg = _sigmoid(-lamv)
        g_in = carry[0:1, :]
        dlam_acc = jnp.zeros((1, RB), F32)
        for sc in reversed(range(nsub)):
            r0 = sc * SUB
            xc, r, ig, mm = xcs[pl.ds(r0, SUB), :], rs[pl.ds(r0, SUB), :], igs[pl.ds(r0, SUB), :], mms[pl.ds(r0, SUB), :]
            a = ab[pl.ds(r0, SUB), :]
            a_next = ab[pl.ds(r0 + 1, SUB), :]
            hv = hb[pl.ds(8 + r0, SUB), :]
            hprev = hb[pl.ds(7 + r0, SUB), :]
            dyv = dy_ref[pl.ds(r0, SUB), :]
            gl, dgl = _gelu_parts(ug_ref[pl.ds(r0, SUB), :])
            dug_ref[pl.ds(r0, SUB), :] = (dyv * hv * dgl).astype(BF16)
            G = _scan_rev(a_next, dyv * gl, g_in)
            g_in = G[0:1, :]
            rows = kt * tc + r0 + lax.broadcasted_iota(jnp.int32, (SUB, RB), 0)
            db = jnp.where(rows >= PAD, G, 0.0)
            da = G * hprev
            dmm = db * (ig * xc)
            di = db * (mm * xc)
            dxc = db * (mm * ig)
            dlog_a = da * a - dmm * (a * a) / jnp.maximum(mm, 1e-30)
            dr = dlog_a * (LRU_C * log_sig)
            dlam_acc = dlam_acc + jnp.sum(dlog_a * (LRU_C * r), axis=0, keepdims=True)
            dga = dr * r * (1.0 - r)
            dgi = di * ig * (1.0 - ig)
            dgab, dgib = dga.astype(BF16), dgi.astype(BF16)
            dgas[pl.ds(r0, SUB), :] = dgab
            dgis[pl.ds(r0, SUB), :] = dgib
            dba_ref[...] += jnp.sum(dga, axis=0, keepdims=True)
            dbi_ref[...] += jnp.sum(dgi, axis=0, keepdims=True)
            dxc = dxc + jnp.dot(dgab, wat_ref[...], preferred_element_type=F32) \
                + jnp.dot(dgib, wit_ref[...], preferred_element_type=F32)
            dxb[pl.ds(r0, SUB), :] = dxc
        dlam_ref[...] += dlam_acc * sig_neg
        xcb = xcs[...].astype(BF16)
        tn = (((0,), (0,)), ((), ()))
        dwa_ref[...] += lax.dot_general(xcb, dgas[...], tn, preferred_element_type=F32)
        dwi_ref[...] += lax.dot_general(xcb, dgis[...], tn, preferred_element_type=F32)
        dxc_all = dxb[0:tc, :]
        dcb_ref[...] += jnp.sum(dxc_all, axis=0, keepdims=True)
        rows_all = kt * tc + lax.broadcasted_iota(jnp.int32, (tc, RB), 0)
        dux = jnp.zeros((tc, RB), F32)
        for j in range(CW):
            dcw_ref[j:j + 1, :] += jnp.sum(dxc_all * xb[pl.ds(5 + j, tc), :], axis=0, keepdims=True)
            dux = dux + cwv_[j:j + 1, :] * dxb[pl.ds(CW - 1 - j, tc), :]
        dux_ref[...] = jnp.where(rows_all >= PAD, dux, 0.0).astype(BF16)
        carry[0:1, :] = g_in
        carry[1:2, :] = ab[0:1, :]
        carry[8:16, :] = dxb[0:8, :]

    vec_out = pl.BlockSpec((1, RB), lambda c, k: (0, c))
    return pl.pallas_call(
        body, name="rnn_bwd", grid=(NBLK, nt),
        in_specs=[blk(0), halo(0), blk(NBLK), blk(0), halo(0), blk(0), cwv, vec, mat, vec, mat, vec, vec, mat, mat],
        out_specs=[blk(0), blk(0), cwv, vec_out, mat, vec_out, mat, vec_out, vec_out],
        out_shape=[jax.ShapeDtypeStruct((T, DR), BF16), jax.ShapeDtypeStruct((T, DR), BF16),
                   jax.ShapeDtypeStruct((CW, DR), F32), jax.ShapeDtypeStruct((1, DR), F32),
                   jax.ShapeDtypeStruct((NBLK, RB, RB), F32), jax.ShapeDtypeStruct((1, DR), F32),
                   jax.ShapeDtypeStruct((NBLK, RB, RB), F32), jax.ShapeDtypeStruct((1, DR), F32),
                   jax.ShapeDtypeStruct((1, DR), F32)],
        scratch_shapes=[pltpu.VMEM((tc + 8, RB), F32), pltpu.VMEM((tc + 8, RB), F32),
                        pltpu.VMEM((tc + 8, RB), F32), pltpu.VMEM((tc + 8, RB), F32),
                        pltpu.VMEM((tc, RB), F32), pltpu.VMEM((tc, RB), F32), pltpu.VMEM((tc, RB), F32),
                        pltpu.VMEM((tc, RB), F32), pltpu.VMEM((tc, RB), BF16), pltpu.VMEM((tc, RB), BF16),
                        pltpu.VMEM((16, RB), F32)],
        compiler_params=_cparams(("parallel", "arbitrary")),
    )(uxg, uxg, uxg, hs, hs, dy, cw, cb, wa, ba, wi, bi, lam, wat, wit)


def _attn_prep(q_all, kv_all, ukr, tab, *, tm=320):
    T = q_all.shape[0]

    def body(q_ref, kv_ref, kr_ref, tab_ref, qo_ref, ko_ref, vo_ref):
        tab_v = tab_ref[...]
        lane = lax.broadcasted_iota(jnp.int32, (tm, LANES), 1)
        t1 = kr_ref[...] * tab_v
        kro = jnp.where(lane < ROPE, t1 + pltpu.roll(t1, ROPE, 1), 0.0).astype(BF16)
        for h in range(NH):
            c0 = h * QW
            qo_ref[h, :, 0:NOPE] = (q_ref[:, c0:c0 + NOPE].astype(F32) * SCALE).astype(BF16)
            t2 = q_ref[:, c0 + NOPE:c0 + QW].astype(F32) * tab_v
            qo_ref[h, :, NOPE:QW] = ((t2 + pltpu.roll(t2, ROPE, 1)) * SCALE).astype(BF16)
            ko_ref[h, :, 0:NOPE] = kv_ref[:, c0:c0 + NOPE].astype(BF16)
            ko_ref[h, :, NOPE:QW] = kro
            vo_ref[h, :, :] = kv_ref[:, c0 + NOPE:c0 + QW].astype(BF16)

    return pl.pallas_call(
        body, name="attn_prep", grid=(T // tm,),
        in_specs=[_rows(tm, NH * QW), _rows(tm, NH * QW), _rows(tm, LANES), _rows(tm, LANES)],
        out_specs=[pl.BlockSpec((NH, tm, QW), lambda i: (0, i, 0)), pl.BlockSpec((NH, tm, QW), lambda i: (0, i, 0)),
                   pl.BlockSpec((NH, tm, VD), lambda i: (0, i, 0))],
        out_shape=[jax.ShapeDtypeStruct((NH, T, QW), BF16), jax.ShapeDtypeStruct((NH, T, QW), BF16),
                   jax.ShapeDtypeStruct((NH, T, VD), BF16)],
        compiler_params=_cparams(("parallel",)),
    )(q_all, kv_all, ukr, tab)


def _attn_prep_bwd(dq, dk, dv, tab, *, tm=320):
    T = dq.shape[1]

    def body(dq_ref, dk_ref, dv_ref, tab_ref, dqa_ref, dkva_ref, dkr_ref):
        tab_v = tab_ref[...]
        lane = lax.broadcasted_iota(jnp.int32, (tm, LANES), 1)
        dkro = jnp.zeros((tm, LANES), F32)
        for h in range(NH):
            c0 = h * QW
            dqa_ref[:, c0:c0 + NOPE] = (dq_ref[h, :, 0:NOPE] * SCALE).astype(BF16)
            d2 = dq_ref[h, :, NOPE:QW]
            dqa_ref[:, c0 + NOPE:c0 + QW] = ((d2 + pltpu.roll(d2, ROPE, 1)) * tab_v * SCALE).astype(BF16)
            dkva_ref[:, c0:c0 + NOPE] = dk_ref[h, :, 0:NOPE].astype(BF16)
            dkva_ref[:, c0 + NOPE:c0 + QW] = dv_ref[h, :, :].astype(BF16)
            dkro = dkro + dk_ref[h, :, NOPE:QW]
        dkro = jnp.where(lane < ROPE, dkro, 0.0)
        dkr_ref[...] = ((dkro + pltpu.roll(dkro, ROPE, 1)) * tab_v).astype(BF16)

    return pl.pallas_call(
        body, name="attn_prep_bwd", grid=(T // tm,),
        in_specs=[pl.BlockSpec((NH, tm, QW), lambda i: (0, i, 0)), pl.BlockSpec((NH, tm, QW), lambda i: (0, i, 0)),
                  pl.BlockSpec((NH, tm, VD), lambda i: (0, i, 0)), _rows(tm, LANES)],
        out_specs=[_rows(tm, NH * QW), _rows(tm, NH * QW), _rows(tm, LANES)],
        out_shape=[jax.ShapeDtypeStruct((T, NH * QW), BF16), jax.ShapeDtypeStruct((T, NH * QW), BF16),
                   jax.ShapeDtypeStruct((T, LANES), BF16)],
        compiler_params=_cparams(("parallel",)),
    )(dq, dk, dv, tab)


def _visible(q0, k0, nq, nk):
    rows = q0 + lax.broadcasted_iota(jnp.int32, (nq, nk), 0)
    cols = k0 + lax.broadcasted_iota(jnp.int32, (nq, nk), 1)
    return ((cols >> 6) <= (rows >> 6)) & (cols >= PAD)


def _visible_t(q0, k0, nq, nk):
    cols = k0 + lax.broadcasted_iota(jnp.int32, (nk, nq), 0)
    rows = q0 + lax.broadcasted_iota(jnp.int32, (nk, nq), 1)
    return ((cols >> 6) <= (rows >> 6)) & (cols >= PAD)


_NT = (((1,), (1,)), ((), ()))
ATTN_BLOCK = 1664


def _attn_block(T):
    return ATTN_BLOCK if T % ATTN_BLOCK == 0 else 640


def _round_up(n, m):
    return -(-n // m) * m


def _flash_fwd(q, k, v, *, gather=(), bq=None):
    T = q.shape[1]
    bq = bq or _attn_block(T)
    nq = T // bq
    rs = bq // CHAINS
    n = len(gather)

    def body(*refs):
        q_ref, k_ref, v_ref = refs[:3]
        g_src = refs[3:3 + n]
        o_ref, lse_ref = refs[3 + n:5 + n]
        g_out = refs[5 + n:5 + 2 * n]
        scr = refs[5 + 2 * n:]
        m_s, l_s, acc_s = scr[:CHAINS], scr[CHAINS:2 * CHAINS], scr[2 * CHAINS:3 * CHAINS]
        g_scr = scr[3 * CHAINS:]
        h = pl.program_id(0)
        i = pl.program_id(1)
        if n:
            @pl.when((h == 0) & (i == 0))
            def _():
                _gather_start(_gather_descs(g_src, g_out, g_scr))

        for r in range(CHAINS):
            m_s[r][...] = jnp.full_like(m_s[r], NEG)
            l_s[r][...] = jnp.zeros_like(l_s[r])
            acc_s[r][...] = jnp.zeros_like(acc_s[r])

        def step(j, masked, diag):
            off = pl.multiple_of(j * bq, bq)
            for r in range(CHAINS):
                rows = pl.ds(r * rs, rs)
                kw = min(bq, _round_up((r + 1) * rs, LANES)) if diag else bq
                kv_ = k_ref[pl.ds(off, kw), :]
                vv = v_ref[pl.ds(off, kw), :]
                s = lax.dot_general(q_ref[rows, :], kv_, _NT, preferred_element_type=F32)
                if masked:
                    s = jnp.where(_visible(i * bq + r * rs, j * bq, rs, kw), s, NEG)
                m_prev = m_s[r][...]
                m_new = jnp.maximum(m_prev, jnp.max(s, axis=-1, keepdims=True))
                p = jnp.exp(s - m_new)
                alpha = jnp.exp(m_prev - m_new)
                l_s[r][...] = alpha * l_s[r][...] + jnp.sum(p, axis=-1, keepdims=True)
                acc_s[r][...] = alpha * acc_s[r][...] + jnp.dot(p.astype(BF16), vv, preferred_element_type=F32)
                m_s[r][...] = m_new

        @pl.when(i == 0)
        def _():
            step(0, True, True)

        @pl.when(i > 0)
        def _():
            step(0, True, False)

            def loop(j, c):
                step(j, False, False)
                return c

            lax.fori_loop(1, i, loop, 0)
            step(i, True, True)

        for r in range(CHAINS):
            rows = pl.ds(r * rs, rs)
            o_ref[rows, :] = (acc_s[r][...] / l_s[r][...]).astype(BF16)
            lse_ref[rows, :] = m_s[r][...] + jnp.log(l_s[r][...])

        if n:
            @pl.when((h == NH - 1) & (i == nq - 1))
            def _():
                _gather_wait(_gather_descs(g_src, g_out, g_scr, with_loads=False))

    return pl.pallas_call(
        body, name="flash_fwd", grid=(NH, nq),
        in_specs=[pl.BlockSpec((None, bq, QW), lambda h, i: (h, i, 0)),
                  pl.BlockSpec((None, T, QW), lambda h, i: (h, 0, 0)),
                  pl.BlockSpec((None, T, VD), lambda h, i: (h, 0, 0))] + [HBM] * n,
        out_specs=[pl.BlockSpec((bq, VD), lambda h, i: (i, h)),
                   pl.BlockSpec((None, bq, 1), lambda h, i: (h, i, 0))] + [HBM] * n,
        out_shape=[jax.ShapeDtypeStruct((T, NH * VD), BF16), jax.ShapeDtypeStruct((NH, T, 1), F32)]
        + [jax.ShapeDtypeStruct((4,) + g.shape, g.dtype) for g in gather],
        scratch_shapes=[pltpu.VMEM((rs, 1), F32)] * (2 * CHAINS) + [pltpu.VMEM((rs, VD), F32)] * CHAINS
        + (_gather_scratch(gather) if n else []),
        compiler_params=_cparams(("arbitrary", "arbitrary")),
    )(q, k, v, *gather)


def _attn_delta(o, do, *, tm=640):
    T = o.shape[0]

    def body(o_ref, do_ref, d_ref):
        prod = o_ref[...].astype(F32) * do_ref[...].astype(F32)
        for h in range(NH):
            d_ref[h, :, :] = jnp.sum(prod[:, h * VD:(h + 1) * VD], axis=-1, keepdims=True)

    return pl.pallas_call(
        body, name="attn_delta", grid=(T // tm,),
        in_specs=[_rows(tm, NH * VD), _rows(tm, NH * VD)],
        out_specs=pl.BlockSpec((NH, tm, 1), lambda i: (0, i, 0)),
        out_shape=jax.ShapeDtypeStruct((NH, T, 1), F32),
        compiler_params=_cparams(("parallel",)),
    )(o, do)


_TN = (((0,), (0,)), ((), ()))


def _flash_bwd(q, k, v, do, lse_row, delta_row, *, scatter=(), bq=None):
    T = q.shape[1]
    bq = bq or _attn_block(T)
    nq = T // bq
    rs = bq // CHAINS
    n = len(scatter)

    def body(*refs):
        q_ref, k_ref, v_ref, do_ref, lse_ref, dl_ref = refs[:6]
        s_src = refs[6:6 + n]
        dq_ref, dk_ref, dv_ref = refs[6 + n:9 + n]
        s_out = refs[9 + n:9 + 2 * n]
        s_scr = refs[9 + 2 * n:]
        h = pl.program_id(0)
        j = pl.program_id(1)
        if n:
            @pl.when((h == 0) & (j == 0))
            def _():
                for cp in _scatter_descs(s_src, s_out, s_scr):
                    cp.start()

        @pl.when(j == 0)
        def _():
            dq_ref[...] = jnp.zeros_like(dq_ref)

        dk_ref[...] = jnp.zeros_like(dk_ref)
        dv_ref[...] = jnp.zeros_like(dv_ref)

        def step(i, masked, diag):
            for r in range(CHAINS):
                rows = pl.ds(r * rs, rs)
                q0 = (r * rs) // LANES * LANES if diag else 0
                qn = bq - q0
                off = pl.multiple_of(i * bq + q0, LANES)
                qv = q_ref[pl.ds(off, qn), :]
                dov = do_ref[pl.ds(off, qn), :]
                lse_v = lse_ref[:, pl.ds(off, qn)]
                dl_v = dl_ref[:, pl.ds(off, qn)]
                st = lax.dot_general(k_ref[rows, :], qv, _NT, preferred_element_type=F32)
                if masked:
                    st = jnp.where(_visible_t(i * bq + q0, j * bq + r * rs, qn, rs), st, NEG)
                pt = jnp.exp(st - lse_v)
                dv_ref[rows, :] += jnp.dot(pt.astype(BF16), dov, preferred_element_type=F32)
                dpt = lax.dot_general(v_ref[rows, :], dov, _NT, preferred_element_type=F32)
                dst = (pt * (dpt - dl_v)).astype(BF16)
                dk_ref[rows, :] += jnp.dot(dst, qv, preferred_element_type=F32)
                dq_ref[pl.ds(off, qn), :] += lax.dot_general(dst, k_ref[rows, :], _TN,
                                                             preferred_element_type=F32)

        step(j, True, True)

        @pl.when(j == 0)
        def _():
            def loop(i, c):
                step(i, True, False)
                return c
            lax.fori_loop(1, nq, loop, 0)

        @pl.when(j > 0)
        def _():
            def loop(i, c):
                step(i, False, False)
                return c
            lax.fori_loop(j + 1, nq, loop, 0)

        if n:
            @pl.when((h == NH - 1) & (j == nq - 1))
            def _():
                for cp in _scatter_descs(s_src, s_out, s_scr):
                    cp.wait()

    return pl.pallas_call(
        body, name="flash_bwd", grid=(NH, nq),
        in_specs=[pl.BlockSpec((None, T, QW), lambda h, j: (h, 0, 0)),
                  pl.BlockSpec((None, bq, QW), lambda h, j: (h, j, 0)),
                  pl.BlockSpec((None, bq, VD), lambda h, j: (h, j, 0)),
                  pl.BlockSpec((T, VD), lambda h, j: (0, h)),
                  pl.BlockSpec((None, 1, T), lambda h, j: (h, 0, 0)),
                  pl.BlockSpec((None, 1, T), lambda h, j: (h, 0, 0))] + [HBM] * n,
        out_specs=[pl.BlockSpec((None, T, QW), lambda h, j: (h, 0, 0)),
                   pl.BlockSpec((None, bq, QW), lambda h, j: (h, j, 0)),
                   pl.BlockSpec((None, bq, VD), lambda h, j: (h, j, 0))] + [HBM] * n,
        out_shape=[jax.ShapeDtypeStruct((NH, T, QW), F32), jax.ShapeDtypeStruct((NH, T, QW), F32),
                   jax.ShapeDtypeStruct((NH, T, VD), F32)]
        + [jax.ShapeDtypeStruct((3,) + s.shape[1:], s.dtype) for s in scatter],
        scratch_shapes=[_dma_sems(3 * n), _dma_sems(3 * n)] if n else [],
        compiler_params=_cparams(("arbitrary", "arbitrary")),
    )(q, k, v, do, lse_row, delta_row, *scatter)


def _rope_table(T):
    pos = (jnp.arange(T, dtype=jnp.int32) - PAD).astype(F32)
    inv_freq = ROPE_THETA ** (-jnp.arange(0, ROPE, 2, dtype=F32) / ROPE)
    ang = pos[:, None] * inv_freq[None, :]
    cos, sin = jnp.cos(ang), jnp.sin(ang)
    return jnp.concatenate([cos, cos, -sin, sin], axis=1)


def _swap_halves(w):
    return jnp.concatenate([w[..., ROPE // 2:], w[..., :ROPE // 2]], axis=-1)


O_UX, O_UG, O_UQ, O_UKV, O_UKR, O_UM = 0, DR, 2 * DR, 2 * DR + QR, 2 * DR + QR + KVR, 2 * DR + QR + KVR + ROPE


def _prep_weights(w):
    b = lambda a: a.astype(BF16)
    w_in = w["w_in"]
    kr = w_in[:, O_UKR:O_UM]
    p = {
        "w_xg": b(w_in[:, :O_UQ]),
        "w_q": b(w_in[:, O_UQ:O_UKV]),
        "w_kv": b(w_in[:, O_UKV:O_UKR]),
        "w_kr": b(jnp.concatenate([kr, _swap_halves(kr)], axis=1)),
        "w_m": b(w_in[:, O_UM:]),
    }
    wq = w["w_uq"].reshape(QR, NH, NOPE + ROPE)
    p["w_uq"] = b(jnp.concatenate([wq, _swap_halves(wq[..., NOPE:])], axis=-1).reshape(QR, NH * QW))
    p["w_ukv"] = b(w["w_ukv"])
    for n in ("w_xg", "w_q", "w_kv", "w_kr", "w_m", "w_uq", "w_ukv"):
        p[n + "_t"] = p[n].T
    p["wa"] = b(w["w_rec_a"])
    p["wi"] = b(w["w_rec_i"])
    p["wa_t"] = jnp.swapaxes(p["wa"], 1, 2)
    p["wi_t"] = jnp.swapaxes(p["wi"], 1, 2)
    return p


def _prep_late_weights(w):
    b = lambda a: a.astype(BF16)
    p = {"w_br": b(w["w_branch"][:DR]), "w_ba": b(w["w_branch"][DR:]), "w_out": b(w["w_out"]),
         "w_fi": b(w["w_ffn_in"]), "w_fo": b(w["w_ffn_out"])}
    for n in tuple(p):
        p[n + "_t"] = p[n].T
    return p


LATE = ("w_branch", "w_out", "w_ffn_in", "w_ffn_out")


def _local_step(x, tgt, w, late=None, reduce_first=None, reduce_second=None):
    S = x.shape[0]
    T = FRONT + S
    p = _prep_weights(w)
    tab = _rope_table(T)
    h0 = jnp.concatenate([jnp.zeros((PAD, D), F32), w["meta_tokens"], x], axis=0)
    row = lambda v: v.reshape(1, -1)

    z = _rmsnorm_fwd(h0, row(w["norm_mix_g"]), name="norm_mix")
    uxg = _mm(z, p["w_xg"], name="mm_uxg")
    uq = _mm(z, p["w_q"], name="mm_uq")
    ukv = _mm(z, p["w_kv"], name="mm_ukv")
    ukr = _mm(z, p["w_kr"], name="mm_ukr")
    um = _mm(z, p["w_m"], name="mm_um", out_dtype=BF16)
    rnn_w = (w["conv_w"], row(w["conv_b"]), p["wa"], row(w["b_rec_a"]), p["wi"], row(w["b_rec_i"]),
             row(w["lru_lambda"]))
    hs, y_rnn = _rnn_fwd(uxg, *rnn_w)
    qn = _rmsnorm_fwd(uq, row(w["q_norm_g"]), name="norm_q")
    kvn = _rmsnorm_fwd(ukv, row(w["kv_norm_g"]), name="norm_kv")
    q_all = _mm(qn, p["w_uq"], name="mm_q", out_dtype=BF16)
    kv_all = _mm(kvn, p["w_ukv"], name="mm_kv", out_dtype=BF16)
    qh, kh, vh = _attn_prep(q_all, kv_all, ukr, tab)
    y_att, lse, *stacks = _flash_fwd(qh, kh, vh, gather=late[0] if late else ())
    if late:
        w = {**w, **late[1](stacks)}
    p.update(_prep_late_weights(w))
    p_rnn = _mm(y_rnn, p["w_br"], name="mm_prnn", out_dtype=BF16)
    p_att = _mm(y_att, p["w_ba"], name="mm_patt", out_dtype=BF16)
    bg = row(w["b_gate"])
    mixed = _gate_mix_fwd(um, bg, p_rnn, p_att)
    h1 = _mm(mixed, p["w_out"], name="mm_out", res=h0)
    zf = _rmsnorm_fwd(h1, row(w["norm_ffn_g"]), name="norm_ffn")
    ff = _mm(zf, p["w_fi"], name="mm_ffn_in", out_dtype=BF16)
    act = _swiglu_fwd(ff)
    h2 = _mm(act, p["w_fo"], name="mm_ffn_out", res=h1)

    g = {}
    dh2, dh2b, dg_fin, lsum = _loss_head(h2, tgt, row(w["final_norm_g"]))
    loss = 0.5 * jnp.sum(lsum) / D
    g["final_norm_g"] = dg_fin.reshape(-1)
    dact = _mm(dh2b, p["w_fo_t"], name="mm_dact", out_dtype=BF16)
    g["w_ffn_out"] = _mm_tn(act, dh2b, name="mm_dw_ffn_out")
    dff = _swiglu_bwd(ff, dact)
    dzf = _mm(dff, p["w_fi_t"], name="mm_dzf")
    g["w_ffn_in"] = _mm_tn(zf, dff, name="mm_dw_ffn_in")
    dh1, dh1b, dg = _rmsnorm_bwd(h1, row(w["norm_ffn_g"]), dzf, dh2, name="norm_ffn_bwd")
    g["norm_ffn_g"] = dg
    dmixed = _mm(dh1b, p["w_out_t"], name="mm_dmixed", out_dtype=BF16)
    g["w_out"] = _mm_tn(mixed, dh1b, name="mm_dw_out")
    dp_rnn, dp_att, dum, dbg = _gate_mix_bwd(um, bg, p_rnn, p_att, dmixed)
    g["b_gate"] = dbg.reshape(2, D)
    dy_rnn = _mm(dp_rnn, p["w_br_t"], name="mm_dy_rnn")
    dy_att = _mm(dp_att, p["w_ba_t"], name="mm_dy_att", out_dtype=BF16)
    g["w_branch"] = jnp.concatenate([_mm_tn(y_rnn, dp_rnn, name="mm_dw_br"),
                                     _mm_tn(y_att, dp_att, name="mm_dw_ba")], axis=0)
    delta = _attn_delta(y_att, dy_att)
    first = reduce_first({n: g[n] for n in LATE}) if reduce_first else ()
    dq, dk, dv, *received = _flash_bwd(qh, kh, vh, dy_att, lse.reshape(NH, 1, T), delta.reshape(NH, 1, T),
                                       scatter=first)
    dq_all, dkv_all, dukr = _attn_prep_bwd(dq, dk, dv, tab)
    dqn = _mm(dq_all, p["w_uq_t"], name="mm_dqn")
    dkvn = _mm(dkv_all, p["w_ukv_t"], name="mm_dkvn")
    dwq = _mm_tn(qn, dq_all, name="mm_dw_uq").reshape(QR, NH, QW)
    dwq_rope = dwq[..., NOPE:NOPE + ROPE] + _swap_halves(dwq[..., NOPE + ROPE:])
    g["w_uq"] = jnp.concatenate([dwq[..., :NOPE], dwq_rope], axis=-1).reshape(QR, NH * (NOPE + ROPE))
    g["w_ukv"] = _mm_tn(kvn, dkv_all, name="mm_dw_ukv")
    duq, dg = _rmsnorm_bwd(uq, row(w["q_norm_g"]), dqn, None, name="norm_q_bwd", want_f32=False)
    g["q_norm_g"] = dg
    dukv, dg = _rmsnorm_bwd(ukv, row(w["kv_norm_g"]), dkvn, None, name="norm_kv_bwd", want_f32=False)
    g["kv_norm_g"] = dg
    (dux, dug, g["conv_w"], g["conv_b"], g["w_rec_a"], g["b_rec_a"], g["w_rec_i"], g["b_rec_i"],
     g["lru_lambda"]) = _rnn_bwd(uxg, hs, dy_rnn, *rnn_w, p["wa_t"], p["wi_t"])
    dwkr = _mm_tn(z, dukr, name="mm_dw_kr")
    g["w_in"] = jnp.concatenate([
        _mm_tn(z, dux, name="mm_dw_x"), _mm_tn(z, dug, name="mm_dw_g"),
        _mm_tn(z, duq, name="mm_dw_q"), _mm_tn(z, dukv, name="mm_dw_kv"),
        dwkr[:, :ROPE] + _swap_halves(dwkr[:, ROPE:]),
        _mm_tn(z, dum, name="mm_dw_m")], axis=1)
    second = reduce_second({n: g[n] for n in ("w_in", "w_uq", "w_ukv")}) if reduce_second else ()
    dz, *received2 = _mm_sum(
        [(dux, p["w_xg_t"][:DR]), (dug, p["w_xg_t"][DR:]), (duq, p["w_q_t"]), (dukv, p["w_kv_t"]),
         (dukr, p["w_kr_t"]), (dum, p["w_m_t"])], name="mm_dz", scatter=second)
    dh0, dg = _rmsnorm_bwd(h0, row(w["norm_mix_g"]), dz, dh1, name="norm_mix_bwd", want_bf16=False)
    g["norm_mix_g"] = dg
    g["meta_tokens"] = dh0[PAD:FRONT]
    return loss, dh0[FRONT:], g, (list(first) + list(second), received + received2)


HBM = pl.BlockSpec(memory_space=pltpu.HBM)
CHIP_FLIPS = ((1, 0), (0, 1), (1, 1))


def _place():
    return lax.axis_index("x"), lax.axis_index("y"), lax.axis_index("c")


def _flip(v, f):
    return 1 - v if f else v


def _dma_sems(n):
    return pltpu.SemaphoreType.DMA((n,))


def _gather_scratch(srcs):
    n = len(srcs)
    return [pltpu.VMEM(s.shape, s.dtype) for s in srcs] + [_dma_sems(3 * n), _dma_sems(3 * n), _dma_sems(n),
                                                            _dma_sems(n)]


def _gather_descs(src_refs, out_refs, scr, with_loads=True):
    n = len(src_refs)
    stage = scr[:n]
    send_sems, recv_sems, in_sems, local_sems = scr[n:]
    x, y, c = _place()
    me = 2 * x + y
    loads, sends, local = [], [], []
    for a in range(n):
        if with_loads:
            loads.append(pltpu.make_async_copy(src_refs[a], stage[a], in_sems.at[a]))
        for k, (fx, fy) in enumerate(CHIP_FLIPS):
            sends.append(pltpu.make_async_remote_copy(
                src_ref=stage[a], dst_ref=out_refs[a].at[me], send_sem=send_sems.at[3 * a + k],
                recv_sem=recv_sems.at[3 * a + k], device_id=(_flip(x, fx), _flip(y, fy), c),
                device_id_type=MESH))
        local.append(pltpu.make_async_copy(stage[a], out_refs[a].at[me], local_sems.at[a]))
    return loads, sends, local


def _gather_start(descs):
    loads, sends, local = descs
    for cp in loads:
        cp.start()
    for a, cp in enumerate(loads):
        cp.wait()
        for s in sends[3 * a:3 * a + 3]:
            s.start()
        local[a].start()


def _gather_wait(descs):
    _, sends, local = descs
    for cp in sends + local:
        cp.wait()


def _allgather_chips(srcs, *, name):
    n = len(srcs)

    def body(*refs):
        descs = _gather_descs(refs[:n], refs[n:2 * n], refs[2 * n:])
        _gather_start(descs)
        _gather_wait(descs)

    return pl.pallas_call(
        body, name=name, in_specs=[HBM] * n, out_specs=[HBM] * n,
        out_shape=[jax.ShapeDtypeStruct((4,) + s.shape, s.dtype) for s in srcs],
        scratch_shapes=_gather_scratch(srcs),
        compiler_params=pltpu.CompilerParams(vmem_limit_bytes=VMEM_LIMIT),
    )(*srcs)


def _allgather_chips_split(srcs, split, *, name):
    n = len(srcs)

    def body(*refs):
        src, out, stage = refs[:n], refs[n:2 * n], refs[2 * n:3 * n]
        send_a, recv_a, send_b, recv_b, in_sems, local_sems = refs[3 * n:]
        x, y, c = _place()
        me = 2 * x + y
        loads = [pltpu.make_async_copy(src[a], stage[a], in_sems.at[a]) for a in range(n)]
        for cp in loads:
            cp.start()

        def half(a, core):
            h = srcs[a].shape[0] // 2
            return pl.ds(pl.multiple_of(core * h, 16), h)

        ici, local = [], []
        for a in range(n):
            loads[a].wait()
            for k, (fx, fy) in enumerate(CHIP_FLIPS):
                s_ref, d_ref = stage[a], out[a].at[me]
                if split[a]:
                    s_ref, d_ref = stage[a].at[half(a, c), :], out[a].at[me, half(a, c), :]
                cp = pltpu.make_async_remote_copy(
                    src_ref=s_ref, dst_ref=d_ref, send_sem=send_a.at[3 * a + k], recv_sem=recv_a.at[3 * a + k],
                    device_id=(_flip(x, fx), _flip(y, fy), c), device_id_type=MESH)
                cp.start()
                ici.append(cp)
            cp = pltpu.make_async_copy(stage[a], out[a].at[me], local_sems.at[a])
            cp.start()
            local.append(cp)
        passed = []
        for a in range(n):
            if not split[a]:
                continue
            for k, (fx, fy) in enumerate(CHIP_FLIPS):
                ici[3 * a + k].wait_recv()
                there = 2 * _flip(x, fx) + _flip(y, fy)
                cp = pltpu.make_async_remote_copy(
                    src_ref=out[a].at[there, half(a, c), :], dst_ref=out[a].at[there, half(a, c), :],
                    send_sem=send_b.at[3 * a + k], recv_sem=recv_b.at[3 * a + k],
                    device_id=(x, y, 1 - c), device_id_type=MESH)
                cp.start()
                passed.append(cp)
        for a in range(n):
            for k in range(3):
                ici[3 * a + k].wait_send()
                if not split[a]:
                    ici[3 * a + k].wait_recv()
        for cp in passed + local:
            cp.wait()

    return pl.pallas_call(
        body, name=name, in_specs=[HBM] * n, out_specs=[HBM] * n,
        out_shape=[jax.ShapeDtypeStruct((4,) + s.shape, s.dtype) for s in srcs],
        scratch_shapes=[pltpu.VMEM(s.shape, s.dtype) for s in srcs]
        + [_dma_sems(3 * n), _dma_sems(3 * n), _dma_sems(3 * n), _dma_sems(3 * n), _dma_sems(n), _dma_sems(n)],
        compiler_params=pltpu.CompilerParams(vmem_limit_bytes=VMEM_LIMIT),
    )(*srcs)


def _scatter_descs(src_refs, out_refs, scr):
    send_sems, recv_sems = scr
    x, y, c = _place()
    copies = []
    for a in range(len(src_refs)):
        for k, (fx, fy) in enumerate(CHIP_FLIPS):
            px, py = _flip(x, fx), _flip(y, fy)
            copies.append(pltpu.make_async_remote_copy(
                src_ref=src_refs[a].at[2 * px + py], dst_ref=out_refs[a].at[k],
                send_sem=send_sems.at[3 * a + k], recv_sem=recv_sems.at[3 * a + k],
                device_id=(px, py, c), device_id_type=MESH))
    return copies


def _scatter_chips(srcs, *, name):
    n = len(srcs)

    def body(*refs):
        copies = _scatter_descs(refs[:n], refs[n:2 * n], refs[2 * n:])
        for cp in copies:
            cp.start()
        for cp in copies:
            cp.wait()

    return pl.pallas_call(
        body, name=name, in_specs=[HBM] * n, out_specs=[HBM] * n,
        out_shape=[jax.ShapeDtypeStruct((3,) + s.shape[1:], s.dtype) for s in srcs],
        scratch_shapes=[_dma_sems(3 * n), _dma_sems(3 * n)],
    )(*srcs)


def _sibling_take(srcs, *, name):
    n = len(srcs)

    def body(*refs):
        src_refs, out_refs = refs[:n], refs[n:2 * n]
        send_sems, recv_sems = refs[2 * n:]
        x, y, c = _place()
        copies = []
        for a in range(n):
            h = srcs[a].shape[1] // 2
            theirs = pl.ds(pl.multiple_of((1 - c) * h, 8), h)
            cp = pltpu.make_async_remote_copy(
                src_ref=src_refs[a].at[:, theirs, :], dst_ref=out_refs[a], send_sem=send_sems.at[a],
                recv_sem=recv_sems.at[a], device_id=(x, y, 1 - c), device_id_type=MESH)
            cp.start()
            copies.append(cp)
        for cp in copies:
            cp.wait()

    return pl.pallas_call(
        body, name=name, in_specs=[HBM] * n, out_specs=[HBM] * n,
        out_shape=[jax.ShapeDtypeStruct((4, s.shape[1] // 2, s.shape[2]), s.dtype) for s in srcs],
        scratch_shapes=[_dma_sems(n), _dma_sems(n)],
    )(*srcs)


def _sibling_swap(srcs, *, name):
    n = len(srcs)

    def body(*refs):
        src_refs, out_refs = refs[:n], refs[n:2 * n]
        send_sems, recv_sems = refs[2 * n:]
        x, y, c = _place()
        copies = []
        for a in range(n):
            cp = pltpu.make_async_remote_copy(
                src_ref=src_refs[a], dst_ref=out_refs[a], send_sem=send_sems.at[a],
                recv_sem=recv_sems.at[a], device_id=(x, y, 1 - c), device_id_type=MESH)
            cp.start()
            copies.append(cp)
        for cp in copies:
            cp.wait()

    return pl.pallas_call(
        body, name=name, in_specs=[HBM] * n, out_specs=[HBM] * n,
        out_shape=[jax.ShapeDtypeStruct(s.shape, s.dtype) for s in srcs],
        scratch_shapes=[_dma_sems(n), _dma_sems(n)],
    )(*srcs)


def _row_tile(rows, cols, n_arrays, step=16):
    budget = 24 * 1024 * 1024 // (2 * 4 * n_arrays * cols)
    best = step
    for t in range(step, rows + 1, step):
        if rows % t == 0 and t <= budget:
            best = t
    assert rows % best == 0, (rows, cols)
    return best


def _add_halves(full, theirs, core, wire, *, name):
    _, h, c = theirs.shape
    tm = _row_tile(h, c, 3)
    nb = h // tm

    def body(core_ref, a_ref, b_ref, o_ref):
        o_ref[...] = (a_ref[...] + b_ref[...]).astype(wire)

    spec = pl.BlockSpec((None, tm, c), lambda s, i, core_ref: (s, i, 0))
    grid_spec = pltpu.PrefetchScalarGridSpec(
        num_scalar_prefetch=1, grid=(4, nb),
        in_specs=[pl.BlockSpec((None, tm, c), lambda s, i, core_ref: (s, core_ref[0] * nb + i, 0)), spec],
        out_specs=spec)
    return pl.pallas_call(
        body, name=name, grid_spec=grid_spec, out_shape=jax.ShapeDtypeStruct(theirs.shape, wire),
        compiler_params=_cparams(("parallel", "parallel")),
    )(core.reshape(1), full, theirs)


def _sum4(own, recv, *, name):
    h, c = own.shape
    tm = _row_tile(h, c, 5)

    def body(o_ref, r_ref, out_ref):
        f = lambda k: r_ref[k].astype(F32)
        out_ref[...] = ((o_ref[...].astype(F32) + f(0)) + f(1)) + f(2)

    return pl.pallas_call(
        body, name=name, grid=(h // tm,),
        in_specs=[_rows(tm, c), pl.BlockSpec((3, tm, c), lambda i: (0, i, 0))],
        out_specs=_rows(tm, c), out_shape=jax.ShapeDtypeStruct((h, c), F32),
        compiler_params=_cparams(("parallel",)),
    )(own, recv)


def _adamw(g, w, m, v, *, name):
    r, c = g.shape
    tm = _row_tile(r, c, 7, step=8)
    c1 = 1.0 / (1.0 - ADAM_B1 ** ADAM_STEP)
    c2 = 1.0 / (1.0 - ADAM_B2 ** ADAM_STEP)

    def body(g_ref, w_ref, m_ref, v_ref, d_ref, nm_ref, nv_ref):
        gv = g_ref[...]
        nm = ADAM_B1 * m_ref[...] + (1.0 - ADAM_B1) * gv
        nv = ADAM_B2 * v_ref[...] + (1.0 - ADAM_B2) * (gv * gv)
        nm_ref[...] = nm
        nv_ref[...] = nv
        d_ref[...] = -ADAM_LR * ((nm * c1) / (jnp.sqrt(nv * c2) + ADAM_EPS) + ADAM_WD * w_ref[...])

    spec = _rows(tm, c)
    shape = jax.ShapeDtypeStruct((r, c), F32)
    return pl.pallas_call(
        body, name=name, grid=(r // tm,), in_specs=[spec] * 4, out_specs=[spec] * 3,
        out_shape=[shape] * 3, compiler_params=_cparams(("parallel",)),
    )(g, w, m, v)


def _adamw_halves(mine, theirs, core, w, m, v, *, name):
    h, c = mine.shape
    tm = _row_tile(h, c, 10, step=8)
    nb = h // tm
    c1 = 1.0 / (1.0 - ADAM_B1 ** ADAM_STEP)
    c2 = 1.0 / (1.0 - ADAM_B2 ** ADAM_STEP)

    def body(core_ref, a_ref, b_ref, w_ref, m_ref, v_ref, g_ref, d_ref, nm_ref, nv_ref):
        gv = jnp.where(pl.program_id(0) // nb == core_ref[0], a_ref[...], b_ref[...])
        nm = ADAM_B1 * m_ref[...] + (1.0 - ADAM_B1) * gv
        nv = ADAM_B2 * v_ref[...] + (1.0 - ADAM_B2) * (gv * gv)
        g_ref[...] = gv
        nm_ref[...] = nm
        nv_ref[...] = nv
        d_ref[...] = -ADAM_LR * ((nm * c1) / (jnp.sqrt(nv * c2) + ADAM_EPS) + ADAM_WD * w_ref[...])

    half = pl.BlockSpec((tm, c), lambda i, core_ref: (i % nb, 0))
    spec = pl.BlockSpec((tm, c), lambda i, core_ref: (i, 0))
    grid_spec = pltpu.PrefetchScalarGridSpec(
        num_scalar_prefetch=1, grid=(2 * nb,), in_specs=[half, half, spec, spec, spec], out_specs=[spec] * 4)
    return pl.pallas_call(
        body, name=name, grid_spec=grid_spec, out_shape=[jax.ShapeDtypeStruct((2 * h, c), F32)] * 4,
        compiler_params=_cparams(("parallel",)),
    )(core.reshape(1), mine, theirs, w, m, v)


BIG = (("w_in", (D, 1328), 1), ("w_uq", (QR, 384), 1), ("w_ukv", (KVR, 512), 1), ("w_branch", (576, D), 0),
       ("w_out", (256, D), 0), ("w_ffn_in", (D, 1408), 1), ("w_ffn_out", (704, D), 0))
SMALL = (("meta_tokens", (NMETA, 256), 1), ("b_gate", (2, 256), 1), ("conv_w", (CW, 320), 1))
REPL = (("norm_mix_g", (D,)), ("conv_b", (DR,)), ("w_rec_a", (NBLK, RB, RB)), ("b_rec_a", (DR,)),
        ("w_rec_i", (NBLK, RB, RB)), ("b_rec_i", (DR,)), ("lru_lambda", (DR,)), ("q_norm_g", (QR,)),
        ("kv_norm_g", (KVR,)), ("norm_ffn_g", (D,)), ("final_norm_g", (D,)))
WEIGHTS = ("meta_tokens", "norm_mix_g", "w_in", "b_gate", "conv_w", "conv_b", "w_rec_a", "b_rec_a", "w_rec_i",
           "b_rec_i", "lru_lambda", "q_norm_g", "w_uq", "kv_norm_g", "w_ukv", "w_branch", "w_out", "norm_ffn_g",
           "w_ffn_in", "w_ffn_out", "final_norm_g")
W = 1024
SMALL_N = sum(math.prod(s) for _, s, _ in SMALL)
SMALL_ROWS = 8
REPL_N = sum(math.prod(s) for _, s in REPL)
QUART_ROWS = 88
assert SMALL_N <= SMALL_ROWS * W and REPL_N <= 4 * QUART_ROWS * W


def _flat_pad(parts, rows):
    v = jnp.concatenate([p.reshape(-1) for p in parts])
    return jnp.pad(v, (0, rows * W - v.shape[0])).reshape(rows, W)


def _shard_stack(full, shard_shape, axis):
    r, cs = shard_shape
    if axis == 0:
        return full.reshape(4, r, cs)
    return jnp.stack([full[:, s * cs:(s + 1) * cs] for s in range(4)])


def _unshard(stack, axis):
    if axis == 0:
        return stack.reshape(4 * stack.shape[1], stack.shape[2])
    return jnp.concatenate([stack[s] for s in range(4)], axis=1)


def _split(flat, table):
    out, off = {}, 0
    for name, shape, *_ in table:
        n = math.prod(shape)
        out[name] = flat[..., off:off + n].reshape(flat.shape[:-1] + tuple(shape))
        off += n
    return out


def kernel(x, meta_tokens, norm_mix_g, w_in, b_gate, conv_w, conv_b, w_rec_a, b_rec_a, w_rec_i, b_rec_i, lru_lambda, q_norm_g, w_uq, kv_norm_g, w_ukv, w_branch, w_out, norm_ffn_g, w_ffn_in, w_ffn_out, final_norm_g, loss_target, m_meta_tokens, m_norm_mix_g, m_w_in, m_b_gate, m_conv_w, m_conv_b, m_w_rec_a, m_b_rec_a, m_w_rec_i, m_b_rec_i, m_lru_lambda, m_q_norm_g, m_w_uq, m_kv_norm_g, m_w_ukv, m_w_branch, m_w_out, m_norm_ffn_g, m_w_ffn_in, m_w_ffn_out, m_final_norm_g, v_meta_tokens, v_norm_mix_g, v_w_in, v_b_gate, v_conv_w, v_conv_b, v_w_rec_a, v_b_rec_a, v_w_rec_i, v_b_rec_i, v_lru_lambda, v_q_norm_g, v_w_uq, v_kv_norm_g, v_w_ukv, v_w_branch, v_w_out, v_norm_ffn_g, v_w_ffn_in, v_w_ffn_out, v_final_norm_g):
    args = dict(locals())
    chip = 2 * lax.axis_index("x") + lax.axis_index("y")
    core = lax.axis_index("c")

    first_big = [b for b in BIG if b[0] not in LATE]
    late_big = [b for b in BIG if b[0] in LATE]
    bf16_shard = lambda n, s: args[n].reshape(s).astype(BF16)
    small = _flat_pad([args[n] for n, _, _ in SMALL], SMALL_ROWS)
    gathered = _allgather_chips_split([bf16_shard(n, s) for n, s, _ in first_big] + [small],
                                      [True] * len(first_big) + [False], name="gather_weights")
    w = {}
    for (name, _, axis), stack in zip(first_big, gathered):
        w[name] = _unshard(stack, axis)
    small_parts = _split(gathered[-1].reshape(4, SMALL_ROWS * W), SMALL)
    for name, _, axis in SMALL:
        w[name] = _unshard(small_parts[name], axis)
    for name, shape in REPL:
        w[name] = args[name].reshape(shape)
    finish_late = lambda stacks: {name: _unshard(st, axis) for (name, _, axis), st in zip(late_big, stacks)}

    def to_wire(red, tag, wires):
        theirs = _sibling_take(red, name="reduce_sibling_" + tag)
        return [_add_halves(a, t, core, wires[k], name=f"add_sibling_{tag}{k}")
                for k, (a, t) in enumerate(zip(red, theirs))]

    reduce_first = lambda gl: to_wire([_shard_stack(gl[n], s, a) for n, s, a in late_big], "a",
                                      [BF16] * len(late_big))
    reduce_second = lambda gl: to_wire([_shard_stack(gl[n], s, a) for n, s, a in first_big], "b",
                                       [BF16] * len(first_big))
    loss, grad_x, g, (parts_ab, recv_ab) = _local_step(
        x[0], loss_target[0], w, late=([bf16_shard(n, s) for n, s, _ in late_big], finish_late),
        reduce_first=reduce_first, reduce_second=reduce_second)
    loss = lax.psum(loss, ("x", "y", "c"))

    small_g = jnp.concatenate([_shard_stack(g[n], s, a).reshape(4, -1) for n, s, a in SMALL], axis=1)
    small_g = jnp.pad(small_g, ((0, 0), (0, SMALL_ROWS * W - SMALL_N))).reshape(4, SMALL_ROWS, W)
    repl_g = _flat_pad([g[n] for n, _ in REPL], 4 * QUART_ROWS).reshape(4, QUART_ROWS, W)
    parts_c = to_wire([jnp.concatenate([small_g, repl_g], axis=1)], "c", [F32])
    recv_c = _scatter_chips(parts_c, name="reduce_chips")
    order = [b[0] for b in late_big] + [b[0] for b in first_big] + ["misc"]
    halves = [_sum4(lax.dynamic_index_in_dim(p, chip, 0, keepdims=False), r, name="sum_chips_" + n)
              for n, p, r in zip(order, list(parts_ab) + parts_c, list(recv_ab) + list(recv_c))]
    others = _sibling_swap(halves, name="share_sibling")

    results = {}
    shape_of = {name: shape for name, shape, _ in BIG}
    for name, mine, theirs in zip(order[:-1], halves, others):
        shape = shape_of[name]
        results[name] = _adamw_halves(mine, theirs, core, args[name].reshape(shape),
                                      args["m_" + name].reshape(shape), args["v_" + name].reshape(shape),
                                      name="adamw_" + name)

    a, b = halves[-1], others[-1]
    g_mine = jnp.where(core == 0, jnp.concatenate([a, b], axis=0), jnp.concatenate([b, a], axis=0))
    g_repl = _allgather_chips([g_mine[SMALL_ROWS:]], name="gather_repl")[0].reshape(4 * QUART_ROWS, W)
    g_misc = jnp.concatenate([g_mine[:SMALL_ROWS], g_repl], axis=0)
    misc_state = lambda prefix: jnp.concatenate(
        [_flat_pad([args[prefix + n] for n, _, _ in SMALL], SMALL_ROWS),
         _flat_pad([args[prefix + n] for n, _ in REPL], 4 * QUART_ROWS)], axis=0)
    d, nm, nv = _adamw(g_misc, misc_state(""), misc_state("m_"), misc_state("v_"), name="adamw_misc")
    misc = (g_misc, d, nm, nv)

    outs = []
    for k in range(4):
        sm = _split(misc[k][:SMALL_ROWS].reshape(-1), SMALL)
        rp = _split(misc[k][SMALL_ROWS:].reshape(-1), REPL)
        for name in WEIGHTS:
            val = results[name][k] if name in results else (sm[name] if name in sm else rp[name])
            outs.append(val.reshape(args[name].shape))
    return (loss, grad_x[None], *outs)
```

```python
import functools
import math

import jax
import jax.numpy as jnp
from jax import lax
from jax.experimental import pallas as pl
from jax.experimental.pallas import tpu as pltpu

F32 = jnp.float32
BF16 = jnp.bfloat16

D = 1024
DR = 1280
NBLK = 10
RB = 128
CW = 4
NH = 8
NOPE = 128
ROPE = 64
VD = 128
QR = 384
KVR = 256
DFF = 2816
NMETA = 16
EPS = 1e-6
LRU_C = 8.0
ROPE_THETA = 10000.0
SCALE = 1.0 / math.sqrt(NOPE + ROPE)
NEG = -1e30
FRONT = 128
PAD = FRONT - NMETA
QW = 2 * NOPE
LANES = 128
SUB = 128
CHAINS = 4
VMEM_LIMIT = 52 * 1024 * 1024

ADAM_LR = 0.001
ADAM_B1 = 0.9
ADAM_B2 = 0.999
ADAM_EPS = 1e-08
ADAM_WD = 0.01
ADAM_STEP = 10

MESH = pl.DeviceIdType.MESH


def _cparams(sem):
    return pltpu.CompilerParams(dimension_semantics=sem, vmem_limit_bytes=VMEM_LIMIT)


def _sigmoid(x):
    return 1.0 / (1.0 + jnp.exp(-x))


def _gelu_parts(x):
    c = math.sqrt(2.0 / math.pi)
    inner = c * (x + 0.044715 * x * x * x)
    t = jnp.tanh(inner)
    g = 0.5 * x * (1.0 + t)
    dg = 0.5 * (1.0 + t) + 0.5 * x * (1.0 - t * t) * c * (1.0 + 3.0 * 0.044715 * x * x)
    return g, dg


def _divisors(n, step, cap):
    return [d for d in range(step, min(n, cap) + 1, step) if n % d == 0] or [n]


MM_VMEM_BUDGET = 40 * 1024 * 1024
MM_MAX_ROWS = 1664
MM_MAX_COLS = 1408


def _mm_tiles(M, K, N, a_item, out_item, has_res):
    best = None
    for tn in _divisors(N, LANES, MM_MAX_COLS):
        for tm in _divisors(M, 16, MM_MAX_ROWS):
            need = 2 * (tm * K * a_item + K * tn * 2 + tm * tn * (out_item + (4 if has_res else 0)))
            if need <= MM_VMEM_BUDGET and (best is None or tm * tn > best[0] * best[1]):
                best = (tm, tn)
    assert best is not None, (M, K, N)
    return best


_NT_DIMS = (((1,), (1,)), ((), ()))


def _mm(a, b, *, name, out_dtype=F32, res=None, bt=False):
    M, K = a.shape
    N = b.shape[0] if bt else b.shape[1]
    has_res = res is not None
    tm, tn = _mm_tiles(M, K, N, a.dtype.itemsize, jnp.dtype(out_dtype).itemsize, has_res)

    def body(*refs):
        if has_res:
            a_ref, b_ref, r_ref, o_ref = refs
        else:
            a_ref, b_ref, o_ref = refs
        av, bv = a_ref[...].astype(BF16), b_ref[...].astype(BF16)
        if bt:
            acc = lax.dot_general(av, bv, _NT_DIMS, preferred_element_type=F32)
        else:
            acc = jnp.dot(av, bv, preferred_element_type=F32)
        if has_res:
            acc = acc + r_ref[...].astype(F32)
        o_ref[...] = acc.astype(o_ref.dtype)

    a_bytes = M * K * a.dtype.itemsize
    b_bytes = K * N * b.dtype.itemsize
    rows_outer = a_bytes + (M // tm) * b_bytes <= b_bytes + (N // tn) * a_bytes
    if rows_outer:
        grid = (M // tm, N // tn)
        ia, ib, io = (lambda i, j: (i, 0)), (lambda i, j: (0, j)), (lambda i, j: (i, j))
        ibt = lambda i, j: (j, 0)
    else:
        grid = (N // tn, M // tm)
        ia, ib, io = (lambda j, i: (i, 0)), (lambda j, i: (0, j)), (lambda j, i: (i, j))
        ibt = lambda j, i: (j, 0)
    in_specs = [pl.BlockSpec((tm, K), ia), pl.BlockSpec((tn, K), ibt) if bt else pl.BlockSpec((K, tn), ib)]
    args = [a, b]
    if has_res:
        in_specs.append(pl.BlockSpec((tm, tn), io))
        args.append(res)
    return pl.pallas_call(
        body, name=name, grid=grid, in_specs=in_specs,
        out_specs=pl.BlockSpec((tm, tn), io),
        out_shape=jax.ShapeDtypeStruct((M, N), out_dtype),
        compiler_params=_cparams(("parallel", "parallel")),
    )(*args)


def _mm_sum(pairs, *, name, scatter=(), out_dtype=F32):
    M = pairs[0][0].shape[0]
    N = pairs[0][1].shape[0]
    ks = [a.shape[1] for a, _ in pairs]
    tm, tn = _mm_tiles(M, sum(ks), N, 2, jnp.dtype(out_dtype).itemsize, False)
    n = len(pairs)
    ns = len(scatter)
    gm, gn = M // tm, N // tn

    def body(*refs):
        s_src = refs[2 * n:2 * n + ns]
        o_ref = refs[2 * n + ns]
        s_out = refs[2 * n + ns + 1:2 * n + 2 * ns + 1]
        s_scr = refs[2 * n + 2 * ns + 1:]
        i, j = pl.program_id(0), pl.program_id(1)
        if ns:
            @pl.when((i == 0) & (j == 0))
            def _():
                for cp in _scatter_descs(s_src, s_out, s_scr):
                    cp.start()

        acc = None
        for k in range(n):
            d = lax.dot_general(refs[2 * k][...].astype(BF16), refs[2 * k + 1][...].astype(BF16), _NT_DIMS,
                                preferred_element_type=F32)
            acc = d if acc is None else acc + d
        o_ref[...] = acc.astype(o_ref.dtype)

        if ns:
            @pl.when((i == gm - 1) & (j == gn - 1))
            def _():
                for cp in _scatter_descs(s_src, s_out, s_scr):
                    cp.wait()

    in_specs, args = [], []
    for (a, b), kk in zip(pairs, ks):
        in_specs += [pl.BlockSpec((tm, kk), lambda i, j: (i, 0)), pl.BlockSpec((tn, kk), lambda i, j: (j, 0))]
        args += [a, b]
    return pl.pallas_call(
        body, name=name, grid=(gm, gn), in_specs=in_specs + [HBM] * ns,
        out_specs=[pl.BlockSpec((tm, tn), lambda i, j: (i, j))] + [HBM] * ns,
        out_shape=[jax.ShapeDtypeStruct((M, N), out_dtype)]
        + [jax.ShapeDtypeStruct((3,) + s.shape[1:], s.dtype) for s in scatter],
        scratch_shapes=[_dma_sems(3 * ns), _dma_sems(3 * ns)] if ns else [],
        compiler_params=_cparams(("arbitrary", "arbitrary")),
    )(*args, *scatter)


def _mm_tn(a, b, *, name):
    T, K1 = a.shape
    N = b.shape[1]
    tt = _divisors(T, 16, MM_MAX_ROWS)[-1]
    tk = _divisors(K1, LANES, MM_MAX_COLS)[-1]
    tn = _divisors(N, LANES, MM_MAX_COLS)[-1]

    def body(a_ref, b_ref, o_ref):
        @pl.when(pl.program_id(2) == 0)
        def _():
            o_ref[...] = jnp.zeros_like(o_ref)

        o_ref[...] += lax.dot_general(a_ref[...].astype(BF16), b_ref[...].astype(BF16),
                                      (((0,), (0,)), ((), ())), preferred_element_type=F32)

    return pl.pallas_call(
        body, name=name, grid=(K1 // tk, N // tn, T // tt),
        in_specs=[pl.BlockSpec((tt, tk), lambda i, j, t: (t, i)),
                  pl.BlockSpec((tt, tn), lambda i, j, t: (t, j))],
        out_specs=pl.BlockSpec((tk, tn), lambda i, j, t: (i, j)),
        out_shape=jax.ShapeDtypeStruct((K1, N), F32),
        compiler_params=_cparams(("parallel", "parallel", "arbitrary")),
    )(a, b)


def _rows(tm, w, cb=0):
    return pl.BlockSpec((tm, w), lambda i: (i, cb))


def _const(shape):
    n = len(shape)
    return pl.BlockSpec(shape, lambda i: (0,) * n)


def _rmsnorm_fwd(x, g, *, name, tm=640):
    T, C = x.shape

    def body(x_ref, g_ref, o_ref):
        xv = x_ref[...]
        r = lax.rsqrt(jnp.mean(xv * xv, axis=-1, keepdims=True) + EPS)
        o_ref[...] = ((xv * r) * g_ref[...]).astype(BF16)

    return pl.pallas_call(
        body, name=name, grid=(T // tm,),
        in_specs=[_rows(tm, C), _const((1, C))],
        out_specs=_rows(tm, C),
        out_shape=jax.ShapeDtypeStruct((T, C), BF16),
        compiler_params=_cparams(("parallel",)),
    )(x, g)


def _rmsnorm_bwd(x, g, dy, res, *, name, tm=640, want_f32=True, want_bf16=True):
    T, C = x.shape
    has_res = res is not None

    def body(*refs):
        refs = list(refs)
        x_ref, g_ref, dy_ref = refs[:3]
        refs = refs[3:]
        r_ref = refs.pop(0) if has_res else None
        o32 = refs.pop(0) if want_f32 else None
        o16 = refs.pop(0) if want_bf16 else None
        dg_ref = refs.pop(0)

        @pl.when(pl.program_id(0) == 0)
        def _():
            dg_ref[...] = jnp.zeros_like(dg_ref)

        xv = x_ref[...]
        dyv = dy_ref[...].astype(F32)
        r = lax.rsqrt(jnp.mean(xv * xv, axis=-1, keepdims=True) + EPS)
        xn = xv * r
        dg_ref[...] += jnp.sum(dyv * xn, axis=0, keepdims=True)
        dxn = dyv * g_ref[...]
        dx = r * (dxn - xn * jnp.mean(dxn * xn, axis=-1, keepdims=True))
        if has_res:
            dx = dx + r_ref[...]
        if want_f32:
            o32[...] = dx
        if want_bf16:
            o16[...] = dx.astype(BF16)

    in_specs = [_rows(tm, C), _const((1, C)), _rows(tm, C)]
    args = [x, g, dy]
    if has_res:
        in_specs.append(_rows(tm, C))
        args.append(res)
    out_specs, out_shape = [], []
    if want_f32:
        out_specs.append(_rows(tm, C))
        out_shape.append(jax.ShapeDtypeStruct((T, C), F32))
    if want_bf16:
        out_specs.append(_rows(tm, C))
        out_shape.append(jax.ShapeDtypeStruct((T, C), BF16))
    out_specs.append(_const((1, C)))
    out_shape.append(jax.ShapeDtypeStruct((1, C), F32))
    return pl.pallas_call(
        body, name=name, grid=(T // tm,), in_specs=in_specs, out_specs=out_specs,
        out_shape=out_shape, compiler_params=_cparams(("arbitrary",)),
    )(*args)


def _gate_mix_fwd(um, bg, p_rnn, p_att, *, tm=320):
    T = um.shape[0]

    def body(um_ref, bg_ref, pr_ref, pa_ref, o_ref):
        g = _sigmoid(um_ref[...].astype(F32) + bg_ref[...])
        o_ref[...] = (g[:, :D] * pr_ref[...].astype(F32) + g[:, D:] * pa_ref[...].astype(F32)).astype(BF16)

    return pl.pallas_call(
        body, name="gate_mix_fwd", grid=(T // tm,),
        in_specs=[_rows(tm, 2 * D), _const((1, 2 * D)), _rows(tm, D), _rows(tm, D)],
        out_specs=_rows(tm, D),
        out_shape=jax.ShapeDtypeStruct((T, D), BF16),
        compiler_params=_cparams(("parallel",)),
    )(um, bg, p_rnn, p_att)


def _gate_mix_bwd(um, bg, p_rnn, p_att, dmixed, *, tm=320):
    T = um.shape[0]

    def body(um_ref, bg_ref, pr_ref, pa_ref, dm_ref, dpr_ref, dpa_ref, dum_ref, dbg_ref):
        @pl.when(pl.program_id(0) == 0)
        def _():
            dbg_ref[...] = jnp.zeros_like(dbg_ref)

        g = _sigmoid(um_ref[...].astype(F32) + bg_ref[...])
        g0, g1 = g[:, :D], g[:, D:]
        dm = dm_ref[...].astype(F32)
        dpr_ref[...] = (dm * g0).astype(BF16)
        dpa_ref[...] = (dm * g1).astype(BF16)
        d0 = dm * pr_ref[...].astype(F32) * g0 * (1.0 - g0)
        d1 = dm * pa_ref[...].astype(F32) * g1 * (1.0 - g1)
        dum_ref[:, :D] = d0.astype(BF16)
        dum_ref[:, D:] = d1.astype(BF16)
        dbg_ref[:, :D] += jnp.sum(d0, axis=0, keepdims=True)
        dbg_ref[:, D:] += jnp.sum(d1, axis=0, keepdims=True)

    return pl.pallas_call(
        body, name="gate_mix_bwd", grid=(T // tm,),
        in_specs=[_rows(tm, 2 * D), _const((1, 2 * D)), _rows(tm, D), _rows(tm, D), _rows(tm, D)],
        out_specs=[_rows(tm, D), _rows(tm, D), _rows(tm, 2 * D), _const((1, 2 * D))],
        out_shape=[jax.ShapeDtypeStruct((T, D), BF16), jax.ShapeDtypeStruct((T, D), BF16),
                   jax.ShapeDtypeStruct((T, 2 * D), BF16), jax.ShapeDtypeStruct((1, 2 * D), F32)],
        compiler_params=_cparams(("arbitrary",)),
    )(um, bg, p_rnn, p_att, dmixed)


def _swiglu_fwd(ff, *, tm=320):
    T = ff.shape[0]

    def body(g_ref, u_ref, o_ref):
        gv = g_ref[...].astype(F32)
        o_ref[...] = (gv * _sigmoid(gv) * u_ref[...].astype(F32)).astype(BF16)

    return pl.pallas_call(
        body, name="swiglu_fwd", grid=(T // tm,),
        in_specs=[_rows(tm, DFF, 0), _rows(tm, DFF, 1)],
        out_specs=_rows(tm, DFF),
        out_shape=jax.ShapeDtypeStruct((T, DFF), BF16),
        compiler_params=_cparams(("parallel",)),
    )(ff, ff)


def _swiglu_bwd(ff, dact, *, tm=320):
    T = ff.shape[0]

    def body(g_ref, u_ref, da_ref, o_ref):
        gv = g_ref[...].astype(F32)
        s = _sigmoid(gv)
        da = da_ref[...].astype(F32)
        o_ref[:, :DFF] = (da * u_ref[...].astype(F32) * s * (1.0 + gv * (1.0 - s))).astype(BF16)
        o_ref[:, DFF:] = (da * gv * s).astype(BF16)

    return pl.pallas_call(
        body, name="swiglu_bwd", grid=(T // tm,),
        in_specs=[_rows(tm, DFF, 0), _rows(tm, DFF, 1), _rows(tm, DFF)],
        out_specs=_rows(tm, 2 * DFF),
        out_shape=jax.ShapeDtypeStruct((T, 2 * DFF), BF16),
        compiler_params=_cparams(("parallel",)),
    )(ff, ff, dact)


def _loss_head(h2, tgt, g, *, tm=640):
    T = h2.shape[0]
    nsub = tm // FRONT

    def body(h_ref, *refs):
        t_refs = refs[:nsub]
        g_ref, d32_ref, d16_ref, dg_ref, ls_ref = refs[nsub:]
        i = pl.program_id(0)

        @pl.when(i == 0)
        def _():
            dg_ref[...] = jnp.zeros_like(dg_ref)
            ls_ref[...] = jnp.zeros_like(ls_ref)

        gv = g_ref[...]
        for k in range(nsub):
            rows = pl.ds(k * FRONT, FRONT)
            xv = h_ref[rows, :]
            r = lax.rsqrt(jnp.mean(xv * xv, axis=-1, keepdims=True) + EPS)
            xn = xv * r
            e = jnp.where(i * nsub + k >= 1, xn * gv - t_refs[k][...], 0.0)
            ls_ref[...] += jnp.sum(e * e, axis=0, keepdims=True)
            dy = e * (1.0 / D)
            dg_ref[...] += jnp.sum(dy * xn, axis=0, keepdims=True)
            dxn = dy * gv
            dx = r * (dxn - xn * jnp.mean(dxn * xn, axis=-1, keepdims=True))
            d32_ref[rows, :] = dx
            d16_ref[rows, :] = dx.astype(BF16)

    def t_spec(k):
        return pl.BlockSpec((FRONT, D), lambda i: (jnp.maximum(i * nsub + k - 1, 0), 0))

    return pl.pallas_call(
        body, name="loss_head", grid=(T // tm,),
        in_specs=[_rows(tm, D)] + [t_spec(k) for k in range(nsub)] + [_const((1, D))],
        out_specs=[_rows(tm, D), _rows(tm, D), _const((1, D)), _const((1, D))],
        out_shape=[jax.ShapeDtypeStruct((T, D), F32), jax.ShapeDtypeStruct((T, D), BF16),
                   jax.ShapeDtypeStruct((1, D), F32), jax.ShapeDtypeStruct((1, D), F32)],
        compiler_params=_cparams(("arbitrary",)),
    )(h2, *([tgt] * nsub), g)


def _scan_fwd(a, b, h_in):
    n = a.shape[0]
    row = lax.broadcasted_iota(jnp.int32, a.shape, 0)
    s = 1
    while s < n:
        if s % 8:
            a_sh = jnp.where(row >= s, pltpu.roll(a, s, 0), 1.0)
            b_sh = jnp.where(row >= s, pltpu.roll(b, s, 0), 0.0)
        else:
            a_sh = jnp.concatenate([jnp.ones((s, RB), F32), a[:n - s]], axis=0)
            b_sh = jnp.concatenate([jnp.zeros((s, RB), F32), b[:n - s]], axis=0)
        b = a * b_sh + b
        a = a * a_sh
        s *= 2
    return b + a * h_in


def _scan_rev(a, b, g_in):
    n = a.shape[0]
    row = lax.broadcasted_iota(jnp.int32, a.shape, 0)
    s = 1
    while s < n:
        if s % 8:
            a_sh = jnp.where(row < n - s, pltpu.roll(a, n - s, 0), 1.0)
            b_sh = jnp.where(row < n - s, pltpu.roll(b, n - s, 0), 0.0)
        else:
            a_sh = jnp.concatenate([a[s:], jnp.ones((s, RB), F32)], axis=0)
            b_sh = jnp.concatenate([b[s:], jnp.zeros((s, RB), F32)], axis=0)
        b = a * b_sh + b
        a = a * a_sh
        s *= 2
    return b + a * g_in


def _lru_gates(xc, wa, ba, wi, bi, lam):
    xcb = xc.astype(BF16)
    r = _sigmoid(jnp.dot(xcb, wa, preferred_element_type=F32) + ba)
    ig = _sigmoid(jnp.dot(xcb, wi, preferred_element_type=F32) + bi)
    log_sig = jnp.minimum(lam, 0.0) - jnp.log(1.0 + jnp.exp(-jnp.abs(lam)))
    log_a = LRU_C * r * log_sig
    a = jnp.exp(log_a)
    m2 = jnp.tanh(-log_a) * (1.0 + a * a)
    return r, ig, log_sig, a, m2 * lax.rsqrt(jnp.maximum(m2, 1e-37))


def _rnn_specs(tc, nblk_t, rev):
    def tmap(k):
        return (nblk_t - 1 - k) if rev else k

    hb = tc // 8
    blk = lambda off: pl.BlockSpec((tc, RB), lambda c, k: (tmap(k), c + off))
    halo = lambda off: pl.BlockSpec((8, RB), lambda c, k: (jnp.maximum(tmap(k) * hb - 1, 0), c + off))
    vec = pl.BlockSpec((1, RB), lambda c, k: (0, c))
    cwv = pl.BlockSpec((CW, RB), lambda c, k: (0, c))
    mat = pl.BlockSpec((None, RB, RB), lambda c, k: (c, 0, 0))
    return blk, halo, vec, cwv, mat


def _rnn_fwd(uxg, cw, cb, wa, ba, wi, bi, lam, *, tc=640):
    T = uxg.shape[0]
    nt = T // tc
    nsub = tc // SUB
    blk, halo, vec, cwv, mat = _rnn_specs(tc, nt, False)

    def body(x_ref, xh_ref, ug_ref, cw_ref, cb_ref, wa_ref, ba_ref, wi_ref, bi_ref, lam_ref,
             h_ref, y_ref, xb, hc):
        k = pl.program_id(1)

        @pl.when(k == 0)
        def _():
            hc[...] = jnp.zeros_like(hc)

        xb[0:8, :] = jnp.where(k > 0, xh_ref[...], 0.0)
        xb[8:, :] = x_ref[...]
        cwv_, cbv = cw_ref[...], cb_ref[...]
        wav, wiv = wa_ref[...], wi_ref[...]
        bav, biv, lamv = ba_ref[...], bi_ref[...], lam_ref[...]
        h_in = hc[0:1, :]
        for sc in range(nsub):
            r0 = sc * SUB
            xc = cbv + cwv_[0:1, :] * xb[pl.ds(5 + r0, SUB), :]
            for j in range(1, CW):
                xc = xc + cwv_[j:j + 1, :] * xb[pl.ds(5 + j + r0, SUB), :]
            r, ig, _, a, mm = _lru_gates(xc, wav, bav, wiv, biv, lamv)
            rows = k * tc + r0 + lax.broadcasted_iota(jnp.int32, (SUB, RB), 0)
            b = jnp.where(rows >= PAD, mm * (ig * xc), 0.0)
            h = _scan_fwd(a, b, h_in)
            h_in = h[SUB - 1:SUB, :]
            h_ref[pl.ds(r0, SUB), :] = h
            gl, _ = _gelu_parts(ug_ref[pl.ds(r0, SUB), :])
            y_ref[pl.ds(r0, SUB), :] = (h * gl).astype(BF16)
        hc[0:1, :] = h_in

    return pl.pallas_call(
        body, name="rnn_fwd", grid=(NBLK, nt),
        in_specs=[blk(0), halo(0), blk(NBLK), cwv, vec, mat, vec, mat, vec, vec],
        out_specs=[blk(0), blk(0)],
        out_shape=[jax.ShapeDtypeStruct((T, DR), F32), jax.ShapeDtypeStruct((T, DR), BF16)],
        scratch_shapes=[pltpu.VMEM((tc + 8, RB), F32), pltpu.VMEM((8, RB), F32)],
        compiler_params=_cparams(("parallel", "arbitrary")),
    )(uxg, uxg, uxg, cw, cb, wa, ba, wi, bi, lam)


def _rnn_bwd(uxg, hs, dy, cw, cb, wa, ba, wi, bi, lam, wat, wit, *, tc=640):
    T = uxg.shape[0]
    nt = T // tc
    nsub = tc // SUB
    blk, halo, vec, cwv, mat = _rnn_specs(tc, nt, True)

    def body(x_ref, xh_ref, ug_ref, h_ref, hh_ref, dy_ref, cw_ref, cb_ref, wa_ref, ba_ref, wi_ref,
             bi_ref, lam_ref, wat_ref, wit_ref,
             dux_ref, dug_ref, dcw_ref, dcb_ref, dwa_ref, dba_ref, dwi_ref, dbi_ref, dlam_ref,
             xb, hb, ab, dxb, xcs, rs, igs, mms, dgas, dgis, carry):
        k = pl.program_id(1)
        kt = nt - 1 - k

        @pl.when(k == 0)
        def _():
            carry[...] = jnp.zeros_like(carry)
            for ref in (dcw_ref, dcb_ref, dwa_ref, dba_ref, dwi_ref, dbi_ref, dlam_ref):
                ref[...] = jnp.zeros_like(ref)

        xb[0:8, :] = jnp.where(kt > 0, xh_ref[...], 0.0)
        xb[8:, :] = x_ref[...]
        hb[0:8, :] = jnp.where(kt > 0, hh_ref[...], 0.0)
        hb[8:, :] = h_ref[...]
        cwv_, cbv = cw_ref[...], cb_ref[...]
        wav, wiv = wa_ref[...], wi_ref[...]
        bav, biv, lamv = ba_ref[...], bi_ref[...], lam_ref[...]
        ab[tc:tc + 8, :] = jnp.broadcast_to(carry[1:2, :], (8, RB))
        dxb[tc:tc + 8, :] = carry[8:16, :]
        log_sig = None
        for sc in range(nsub):
            r0 = sc * SUB
            xc = cbv + cwv_[0:1, :] * xb[pl.ds(5 + r0, SUB), :]
            for j in range(1, CW):
                xc = xc + cwv_[j:j + 1, :] * xb[pl.ds(5 + j + r0, SUB), :]
            r, ig, log_sig, a, mm = _lru_gates(xc, wav, bav, wiv, biv, lamv)
            xcs[pl.ds(r0, SUB), :] = xc
            rs[pl.ds(r0, SUB), :] = r
            igs[pl.ds(r0, SUB), :] = ig
            mms[pl.ds(r0, SUB), :] = mm
            ab[pl.ds(r0, SUB), :] = a
        sig_neg = _sigmoid(-lamv)
        g_in = carry[0:1, :]
        dlam_acc = jnp.zeros((1, RB), F32)
        for sc in reversed(range(nsub)):
            r0 = sc * SUB
            xc, r, ig, mm = xcs[pl.ds(r0, SUB), :], rs[pl.ds(r0, SUB), :], igs[pl.ds(r0, SUB), :], mms[pl.ds(r0, SUB), :]
            a = ab[pl.ds(r0, SUB), :]
            a_next = ab[pl.ds(r0 + 1, SUB), :]
            hv = hb[pl.ds(8 + r0, SUB), :]
            hprev = hb[pl.ds(7 + r0, SUB), :]
            dyv = dy_ref[pl.ds(r0, SUB), :]
            gl, dgl = _gelu_parts(ug_ref[pl.ds(r0, SUB), :])
            dug_ref[pl.ds(r0, SUB), :] = (dyv * hv * dgl).astype(BF16)
            G = _scan_rev(a_next, dyv * gl, g_in)
            g_in = G[0:1, :]
            rows = kt * tc + r0 + lax.broadcasted_iota(jnp.int32, (SUB, RB), 0)
            db = jnp.where(rows >= PAD, G, 0.0)
            da = G * hprev
            dmm = db * (ig * xc)
            di = db * (mm * xc)
            dxc = db * (mm * ig)
            dlog_a = da * a - dmm * (a * a) / jnp.maximum(mm, 1e-30)
            dr = dlog_a * (LRU_C * log_sig)
            dlam_acc = dlam_acc + jnp.sum(dlog_a * (LRU_C * r), axis=0, keepdims=True)
            dga = dr * r * (1.0 - r)
            dgi = di * ig * (1.0 - ig)
            dgab, dgib = dga.astype(BF16), dgi.astype(BF16)
            dgas[pl.ds(r0, SUB), :] = dgab
            dgis[pl.ds(r0, SUB), :] = dgib
            dba_ref[...] += jnp.sum(dga, axis=0, keepdims=True)
            dbi_ref[...] += jnp.sum(dgi, axis=0, keepdims=True)
            dxc = dxc + jnp.dot(dgab, wat_ref[...], preferred_element_type=F32) \
                + jnp.dot(dgib, wit_ref[...], preferred_element_type=F32)
            dxb[pl.ds(r0, SUB), :] = dxc
        dlam_ref[...] += dlam_acc * sig_neg
        xcb = xcs[...].astype(BF16)
        tn = (((0,), (0,)), ((), ()))
        dwa_ref[...] += lax.dot_general(xcb, dgas[...], tn, preferred_element_type=F32)
        dwi_ref[...] += lax.dot_general(xcb, dgis[...], tn, preferred_element_type=F32)
        dxc_all = dxb[0:tc, :]
        dcb_ref[...] += jnp.sum(dxc_all, axis=0, keepdims=True)
        rows_all = kt * tc + lax.broadcasted_iota(jnp.int32, (tc, RB), 0)
        dux = jnp.zeros((tc, RB), F32)
        for j in range(CW):
            dcw_ref[j:j + 1, :] += jnp.sum(dxc_all * xb[pl.ds(5 + j, tc), :], axis=0, keepdims=True)
            dux = dux + cwv_[j:j + 1, :] * dxb[pl.ds(CW - 1 - j, tc), :]
        dux_ref[...] = jnp.where(rows_all >= PAD, dux, 0.0).astype(BF16)
        carry[0:1, :] = g_in
        carry[1:2, :] = ab[0:1, :]
        carry[8:16, :] = dxb[0:8, :]

    vec_out = pl.BlockSpec((1, RB), lambda c, k: (0, c))
    return pl.pallas_call(
        body, name="rnn_bwd", grid=(NBLK, nt),
        in_specs=[blk(0), halo(0), blk(NBLK), blk(0), halo(0), blk(0), cwv, vec, mat, vec, mat, vec, vec, mat, mat],
        out_specs=[blk(0), blk(0), cwv, vec_out, mat, vec_out, mat, vec_out, vec_out],
        out_shape=[jax.ShapeDtypeStruct((T, DR), BF16), jax.ShapeDtypeStruct((T, DR), BF16),
                   jax.ShapeDtypeStruct((CW, DR), F32), jax.ShapeDtypeStruct((1, DR), F32),
                   jax.ShapeDtypeStruct((NBLK, RB, RB), F32), jax.ShapeDtypeStruct((1, DR), F32),
                   jax.ShapeDtypeStruct((NBLK, RB, RB), F32), jax.ShapeDtypeStruct((1, DR), F32),
                   jax.ShapeDtypeStruct((1, DR), F32)],
        scratch_shapes=[pltpu.VMEM((tc + 8, RB), F32), pltpu.VMEM((tc + 8, RB), F32),
                        pltpu.VMEM((tc + 8, RB), F32), pltpu.VMEM((tc + 8, RB), F32),
                        pltpu.VMEM((tc, RB), F32), pltpu.VMEM((tc, RB), F32), pltpu.VMEM((tc, RB), F32),
                        pltpu.VMEM((tc, RB), F32), pltpu.VMEM((tc, RB), BF16), pltpu.VMEM((tc, RB), BF16),
                        pltpu.VMEM((16, RB), F32)],
        compiler_params=_cparams(("parallel", "arbitrary")),
    )(uxg, uxg, uxg, hs, hs, dy, cw, cb, wa, ba, wi, bi, lam, wat, wit)


def _attn_prep(q_all, kv_all, ukr, tab, *, tm=320):
    T = q_all.shape[0]

    def body(q_ref, kv_ref, kr_ref, tab_ref, qo_ref, ko_ref, vo_ref):
        tab_v = tab_ref[...]
        lane = lax.broadcasted_iota(jnp.int32, (tm, LANES), 1)
        t1 = kr_ref[...] * tab_v
        kro = jnp.where(lane < ROPE, t1 + pltpu.roll(t1, ROPE, 1), 0.0).astype(BF16)
        for h in range(NH):
            c0 = h * QW
            qo_ref[h, :, 0:NOPE] = (q_ref[:, c0:c0 + NOPE].astype(F32) * SCALE).astype(BF16)
            t2 = q_ref[:, c0 + NOPE:c0 + QW].astype(F32) * tab_v
            qo_ref[h, :, NOPE:QW] = ((t2 + pltpu.roll(t2, ROPE, 1)) * SCALE).astype(BF16)
            ko_ref[h, :, 0:NOPE] = kv_ref[:, c0:c0 + NOPE].astype(BF16)
            ko_ref[h, :, NOPE:QW] = kro
            vo_ref[h, :, :] = kv_ref[:, c0 + NOPE:c0 + QW].astype(BF16)

    return pl.pallas_call(
        body, name="attn_prep", grid=(T // tm,),
        in_specs=[_rows(tm, NH * QW), _rows(tm, NH * QW), _rows(tm, LANES), _rows(tm, LANES)],
        out_specs=[pl.BlockSpec((NH, tm, QW), lambda i: (0, i, 0)), pl.BlockSpec((NH, tm, QW), lambda i: (0, i, 0)),
                   pl.BlockSpec((NH, tm, VD), lambda i: (0, i, 0))],
        out_shape=[jax.ShapeDtypeStruct((NH, T, QW), BF16), jax.ShapeDtypeStruct((NH, T, QW), BF16),
                   jax.ShapeDtypeStruct((NH, T, VD), BF16)],
        compiler_params=_cparams(("parallel",)),
    )(q_all, kv_all, ukr, tab)


def _attn_prep_bwd(dq, dk, dv, tab, *, tm=320):
    T = dq.shape[1]

    def body(dq_ref, dk_ref, dv_ref, tab_ref, dqa_ref, dkva_ref, dkr_ref):
        tab_v = tab_ref[...]
        lane = lax.broadcasted_iota(jnp.int32, (tm, LANES), 1)
        dkro = jnp.zeros((tm, LANES), F32)
        for h in range(NH):
            c0 = h * QW
            dqa_ref[:, c0:c0 + NOPE] = (dq_ref[h, :, 0:NOPE] * SCALE).astype(BF16)
            d2 = dq_ref[h, :, NOPE:QW]
            dqa_ref[:, c0 + NOPE:c0 + QW] = ((d2 + pltpu.roll(d2, ROPE, 1)) * tab_v * SCALE).astype(BF16)
            dkva_ref[:, c0:c0 + NOPE] = dk_ref[h, :, 0:NOPE].astype(BF16)
            dkva_ref[:, c0 + NOPE:c0 + QW] = dv_ref[h, :, :].astype(BF16)
            dkro = dkro + dk_ref[h, :, NOPE:QW].astype(F32)
        dkro = jnp.where(lane < ROPE, dkro, 0.0)
        dkr_ref[...] = ((dkro + pltpu.roll(dkro, ROPE, 1)) * tab_v).astype(BF16)

    return pl.pallas_call(
        body, name="attn_prep_bwd", grid=(T // tm,),
        in_specs=[pl.BlockSpec((NH, tm, QW), lambda i: (0, i, 0)), pl.BlockSpec((NH, tm, QW), lambda i: (0, i, 0)),
                  pl.BlockSpec((NH, tm, VD), lambda i: (0, i, 0)), _rows(tm, LANES)],
        out_specs=[_rows(tm, NH * QW), _rows(tm, NH * QW), _rows(tm, LANES)],
        out_shape=[jax.ShapeDtypeStruct((T, NH * QW), BF16), jax.ShapeDtypeStruct((T, NH * QW), BF16),
                   jax.ShapeDtypeStruct((T, LANES), BF16)],
        compiler_params=_cparams(("parallel",)),
    )(dq, dk, dv, tab)


def _visible(q0, k0, nq, nk):
    rows = q0 + lax.broadcasted_iota(jnp.int32, (nq, nk), 0)
    cols = k0 + lax.broadcasted_iota(jnp.int32, (nq, nk), 1)
    return ((cols >> 6) <= (rows >> 6)) & (cols >= PAD)


def _visible_t(q0, k0, nq, nk):
    cols = k0 + lax.broadcasted_iota(jnp.int32, (nk, nq), 0)
    rows = q0 + lax.broadcasted_iota(jnp.int32, (nk, nq), 1)
    return ((cols >> 6) <= (rows >> 6)) & (cols >= PAD)


_NT = (((1,), (1,)), ((), ()))
ATTN_BLOCK = 1664


def _attn_block(T):
    return ATTN_BLOCK if T % ATTN_BLOCK == 0 else 640


def _round_up(n, m):
    return -(-n // m) * m


def _flash_fwd(q, k, v, *, gather=(), bq=None):
    T = q.shape[1]
    bq = bq or _attn_block(T)
    nq = T // bq
    rs = bq // CHAINS
    n = len(gather)

    def body(*refs):
        q_ref, k_ref, v_ref = refs[:3]
        g_src = refs[3:3 + n]
        o_ref, lse_ref = refs[3 + n:5 + n]
        g_out = refs[5 + n:5 + 2 * n]
        scr = refs[5 + 2 * n:]
        m_s, l_s, acc_s = scr[:CHAINS], scr[CHAINS:2 * CHAINS], scr[2 * CHAINS:3 * CHAINS]
        g_scr = scr[3 * CHAINS:]
        h = pl.program_id(0)
        i = pl.program_id(1)
        if n:
            @pl.when((h == 0) & (i == 0))
            def _():
                _gather_start(_gather_descs(g_src, g_out, g_scr))

        for r in range(CHAINS):
            m_s[r][...] = jnp.full_like(m_s[r], NEG)
            l_s[r][...] = jnp.zeros_like(l_s[r])
            acc_s[r][...] = jnp.zeros_like(acc_s[r])

        def step(j, masked, diag):
            off = pl.multiple_of(j * bq, bq)
            for r in range(CHAINS):
                rows = pl.ds(r * rs, rs)
                kw = min(bq, _round_up((r + 1) * rs, LANES)) if diag else bq
                kv_ = k_ref[pl.ds(off, kw), :]
                vv = v_ref[pl.ds(off, kw), :]
                s = lax.dot_general(q_ref[rows, :], kv_, _NT, preferred_element_type=F32)
                if masked:
                    s = jnp.where(_visible(i * bq + r * rs, j * bq, rs, kw), s, NEG)
                m_prev = m_s[r][...]
                m_new = jnp.maximum(m_prev, jnp.max(s, axis=-1, keepdims=True))
                p = jnp.exp(s - m_new)
                alpha = jnp.exp(m_prev - m_new)
                l_s[r][...] = alpha * l_s[r][...] + jnp.sum(p, axis=-1, keepdims=True)
                acc_s[r][...] = alpha * acc_s[r][...] + jnp.dot(p.astype(BF16), vv, preferred_element_type=F32)
                m_s[r][...] = m_new

        @pl.when(i == 0)
        def _():
            step(0, True, True)

        @pl.when(i > 0)
        def _():
            step(0, True, False)

            def loop(j, c):
                step(j, False, False)
                return c

            lax.fori_loop(1, i, loop, 0)
            step(i, True, True)

        for r in range(CHAINS):
            rows = pl.ds(r * rs, rs)
            o_ref[rows, :] = (acc_s[r][...] / l_s[r][...]).astype(BF16)
        lse_col = jnp.concatenate([m_s[r][...] + jnp.log(l_s[r][...]) for r in range(CHAINS)], axis=0)
        lse_ref[...] = jnp.broadcast_to(lse_col, (bq, LANES)).T[0:1, :]

        if n:
            @pl.when((h == NH - 1) & (i == nq - 1))
            def _():
                _gather_wait(_gather_descs(g_src, g_out, g_scr, with_loads=False))

    return pl.pallas_call(
        body, name="flash_fwd", grid=(NH, nq),
        in_specs=[pl.BlockSpec((None, bq, QW), lambda h, i: (h, i, 0)),
                  pl.BlockSpec((None, T, QW), lambda h, i: (h, 0, 0)),
                  pl.BlockSpec((None, T, VD), lambda h, i: (h, 0, 0))] + [HBM] * n,
        out_specs=[pl.BlockSpec((bq, VD), lambda h, i: (i, h)),
                   pl.BlockSpec((None, 1, bq), lambda h, i: (h, 0, i))] + [HBM] * n,
        out_shape=[jax.ShapeDtypeStruct((T, NH * VD), BF16), jax.ShapeDtypeStruct((NH, 1, T), F32)]
        + [jax.ShapeDtypeStruct((4,) + g.shape, g.dtype) for g in gather],
        scratch_shapes=[pltpu.VMEM((rs, 1), F32)] * (2 * CHAINS) + [pltpu.VMEM((rs, VD), F32)] * CHAINS
        + (_gather_scratch(gather) if n else []),
        compiler_params=_cparams(("arbitrary", "arbitrary")),
    )(q, k, v, *gather)


def _attn_delta(o, do, *, tm=640):
    T = o.shape[0]

    def body(o_ref, do_ref, d_ref):
        prod = o_ref[...].astype(F32) * do_ref[...].astype(F32)
        for h in range(NH):
            col = jnp.sum(prod[:, h * VD:(h + 1) * VD], axis=-1, keepdims=True)
            d_ref[h, :, :] = jnp.broadcast_to(col, (tm, LANES)).T[0:1, :]

    return pl.pallas_call(
        body, name="attn_delta", grid=(T // tm,),
        in_specs=[_rows(tm, NH * VD), _rows(tm, NH * VD)],
        out_specs=pl.BlockSpec((NH, 1, tm), lambda i: (0, 0, i)),
        out_shape=jax.ShapeDtypeStruct((NH, 1, T), F32),
        compiler_params=_cparams(("parallel",)),
    )(o, do)


_TN = (((0,), (0,)), ((), ()))


def _flash_bwd(q, k, v, do, lse_row, delta_row, *, scatter=(), bq=None):
    T = q.shape[1]
    bq = bq or _attn_block(T)
    nq = T // bq
    rs = bq // CHAINS
    n = len(scatter)

    def body(*refs):
        q_ref, k_ref, v_ref, do_ref, lse_ref, dl_ref = refs[:6]
        s_src = refs[6:6 + n]
        dq_ref, dk_out, dv_out = refs[6 + n:9 + n]
        s_out = refs[9 + n:9 + 2 * n]
        dk_ref, dv_ref = refs[9 + 2 * n:11 + 2 * n]
        s_scr = refs[11 + 2 * n:]
        h = pl.program_id(0)
        j = pl.program_id(1)
        if n:
            @pl.when((h == 0) & (j == 0))
            def _():
                for cp in _scatter_descs(s_src, s_out, s_scr):
                    cp.start()

        @pl.when(j == 0)
        def _():
            dq_ref[...] = jnp.zeros_like(dq_ref)

        dk_ref[...] = jnp.zeros_like(dk_ref)
        dv_ref[...] = jnp.zeros_like(dv_ref)

        def step(i, masked, diag):
            for r in range(CHAINS):
                rows = pl.ds(r * rs, rs)
                q0 = (r * rs) // LANES * LANES if diag else 0
                qn = bq - q0
                off = pl.multiple_of(i * bq + q0, LANES)
                qv = q_ref[pl.ds(off, qn), :]
                dov = do_ref[pl.ds(off, qn), :]
                lse_v = lse_ref[:, pl.ds(off, qn)]
                dl_v = dl_ref[:, pl.ds(off, qn)]
                st = lax.dot_general(k_ref[rows, :], qv, _NT, preferred_element_type=F32)
                if masked:
                    st = jnp.where(_visible_t(i * bq + q0, j * bq + r * rs, qn, rs), st, NEG)
                pt = jnp.exp(st - lse_v)
                dv_ref[rows, :] += jnp.dot(pt.astype(BF16), dov, preferred_element_type=F32)
                dpt = lax.dot_general(v_ref[rows, :], dov, _NT, preferred_element_type=F32)
                dst = (pt * (dpt - dl_v)).astype(BF16)
                dk_ref[rows, :] += jnp.dot(dst, qv, preferred_element_type=F32)
                dq_ref[pl.ds(off, qn), :] += lax.dot_general(dst, k_ref[rows, :], _TN,
                                                             preferred_element_type=F32)

        step(j, True, True)

        @pl.when(j == 0)
        def _():
            def loop(i, c):
                step(i, True, False)
                return c
            lax.fori_loop(1, nq, loop, 0)

        @pl.when(j > 0)
        def _():
            def loop(i, c):
                step(i, False, False)
                return c
            lax.fori_loop(j + 1, nq, loop, 0)

        dk_out[...] = dk_ref[...].astype(BF16)
        dv_out[...] = dv_ref[...].astype(BF16)

        if n:
            @pl.when((h == NH - 1) & (j == nq - 1))
            def _():
                for cp in _scatter_descs(s_src, s_out, s_scr):
                    cp.wait()

    return pl.pallas_call(
        body, name="flash_bwd", grid=(NH, nq),
        in_specs=[pl.BlockSpec((None, T, QW), lambda h, j: (h, 0, 0)),
                  pl.BlockSpec((None, bq, QW), lambda h, j: (h, j, 0)),
                  pl.BlockSpec((None, bq, VD), lambda h, j: (h, j, 0)),
                  pl.BlockSpec((T, VD), lambda h, j: (0, h)),
                  pl.BlockSpec((None, 1, T), lambda h, j: (h, 0, 0)),
                  pl.BlockSpec((None, 1, T), lambda h, j: (h, 0, 0))] + [HBM] * n,
        out_specs=[pl.BlockSpec((None, T, QW), lambda h, j: (h, 0, 0)),
                   pl.BlockSpec((None, bq, QW), lambda h, j: (h, j, 0)),
                   pl.BlockSpec((None, bq, VD), lambda h, j: (h, j, 0))] + [HBM] * n,
        out_shape=[jax.ShapeDtypeStruct((NH, T, QW), F32), jax.ShapeDtypeStruct((NH, T, QW), BF16),
                   jax.ShapeDtypeStruct((NH, T, VD), BF16)]
        + [jax.ShapeDtypeStruct((3,) + s.shape[1:], s.dtype) for s in scatter],
        scratch_shapes=[pltpu.VMEM((bq, QW), F32), pltpu.VMEM((bq, VD), F32)]
        + ([_dma_sems(3 * n), _dma_sems(3 * n)] if n else []),
        compiler_params=_cparams(("arbitrary", "arbitrary")),
    )(q, k, v, do, lse_row, delta_row, *scatter)


def _rope_table(T):
    pos = (jnp.arange(T, dtype=jnp.int32) - PAD).astype(F32)
    inv_freq = ROPE_THETA ** (-jnp.arange(0, ROPE, 2, dtype=F32) / ROPE)
    ang = pos[:, None] * inv_freq[None, :]
    cos, sin = jnp.cos(ang), jnp.sin(ang)
    return jnp.concatenate([cos, cos, -sin, sin], axis=1)


def _swap_halves(w):
    return jnp.concatenate([w[..., ROPE // 2:], w[..., :ROPE // 2]], axis=-1)


O_UX, O_UG, O_UQ, O_UKV, O_UKR, O_UM = 0, DR, 2 * DR, 2 * DR + QR, 2 * DR + QR + KVR, 2 * DR + QR + KVR + ROPE


def _prep_weights(w):
    b = lambda a: a.astype(BF16)
    w_in = w["w_in"]
    kr = w_in[:, O_UKR:O_UM]
    p = {
        "w_xg": b(w_in[:, :O_UQ]),
        "w_q": b(w_in[:, O_UQ:O_UKV]),
        "w_kv": b(w_in[:, O_UKV:O_UKR]),
        "w_kr": b(jnp.concatenate([kr, _swap_halves(kr)], axis=1)),
        "w_m": b(w_in[:, O_UM:]),
    }
    wq = w["w_uq"].reshape(QR, NH, NOPE + ROPE)
    p["w_uq"] = b(jnp.concatenate([wq, _swap_halves(wq[..., NOPE:])], axis=-1).reshape(QR, NH * QW))
    p["w_ukv"] = b(w["w_ukv"])
    p["w_x"], p["w_g"] = p["w_xg"][:, :DR], p["w_xg"][:, DR:]
    p["wa"] = b(w["w_rec_a"])
    p["wi"] = b(w["w_rec_i"])
    p["wa_t"] = jnp.swapaxes(p["wa"], 1, 2)
    p["wi_t"] = jnp.swapaxes(p["wi"], 1, 2)
    return p


def _prep_late_weights(w):
    b = lambda a: a.astype(BF16)
    p = {"w_br": b(w["w_branch"][:DR]), "w_ba": b(w["w_branch"][DR:]), "w_out": b(w["w_out"]),
         "w_fi": b(w["w_ffn_in"]), "w_fo": b(w["w_ffn_out"])}
    return p


LATE = ("w_branch", "w_out", "w_ffn_in", "w_ffn_out")


def _local_step(x, tgt, w, late=None, reduce_first=None, reduce_second=None):
    S = x.shape[0]
    T = FRONT + S
    p = _prep_weights(w)
    tab = _rope_table(T)
    h0 = jnp.concatenate([jnp.zeros((PAD, D), F32), w["meta_tokens"], x], axis=0)
    row = lambda v: v.reshape(1, -1)

    z = _rmsnorm_fwd(h0, row(w["norm_mix_g"]), name="norm_mix")
    uxg = _mm(z, p["w_xg"], name="mm_uxg")
    uq = _mm(z, p["w_q"], name="mm_uq")
    ukv = _mm(z, p["w_kv"], name="mm_ukv")
    ukr = _mm(z, p["w_kr"], name="mm_ukr")
    um = _mm(z, p["w_m"], name="mm_um", out_dtype=BF16)
    rnn_w = (w["conv_w"], row(w["conv_b"]), p["wa"], row(w["b_rec_a"]), p["wi"], row(w["b_rec_i"]),
             row(w["lru_lambda"]))
    hs, y_rnn = _rnn_fwd(uxg, *rnn_w)
    qn = _rmsnorm_fwd(uq, row(w["q_norm_g"]), name="norm_q")
    kvn = _rmsnorm_fwd(ukv, row(w["kv_norm_g"]), name="norm_kv")
    q_all = _mm(qn, p["w_uq"], name="mm_q", out_dtype=BF16)
    kv_all = _mm(kvn, p["w_ukv"], name="mm_kv", out_dtype=BF16)
    qh, kh, vh = _attn_prep(q_all, kv_all, ukr, tab)
    y_att, lse, *stacks = _flash_fwd(qh, kh, vh, gather=late[0] if late else ())
    if late:
        w = {**w, **late[1](stacks)}
    p.update(_prep_late_weights(w))
    p_rnn = _mm(y_rnn, p["w_br"], name="mm_prnn", out_dtype=BF16)
    p_att = _mm(y_att, p["w_ba"], name="mm_patt", out_dtype=BF16)
    bg = row(w["b_gate"])
    mixed = _gate_mix_fwd(um, bg, p_rnn, p_att)
    h1 = _mm(mixed, p["w_out"], name="mm_out", res=h0)
    zf = _rmsnorm_fwd(h1, row(w["norm_ffn_g"]), name="norm_ffn")
    ff = _mm(zf, p["w_fi"], name="mm_ffn_in", out_dtype=BF16)
    act = _swiglu_fwd(ff)
    h2 = _mm(act, p["w_fo"], name="mm_ffn_out", res=h1)

    g = {}
    dh2, dh2b, dg_fin, lsum = _loss_head(h2, tgt, row(w["final_norm_g"]))
    loss = 0.5 * jnp.sum(lsum) / D
    g["final_norm_g"] = dg_fin.reshape(-1)
    dact = _mm(dh2b, p["w_fo"], name="mm_dact", out_dtype=BF16, bt=True)
    g["w_ffn_out"] = _mm_tn(act, dh2b, name="mm_dw_ffn_out")
    dff = _swiglu_bwd(ff, dact)
    dzf = _mm(dff, p["w_fi"], name="mm_dzf", bt=True)
    g["w_ffn_in"] = _mm_tn(zf, dff, name="mm_dw_ffn_in")
    dh1, dh1b, dg = _rmsnorm_bwd(h1, row(w["norm_ffn_g"]), dzf, dh2, name="norm_ffn_bwd")
    g["norm_ffn_g"] = dg
    dmixed = _mm(dh1b, p["w_out"], name="mm_dmixed", out_dtype=BF16, bt=True)
    g["w_out"] = _mm_tn(mixed, dh1b, name="mm_dw_out")
    dp_rnn, dp_att, dum, dbg = _gate_mix_bwd(um, bg, p_rnn, p_att, dmixed)
    g["b_gate"] = dbg.reshape(2, D)
    dy_rnn = _mm(dp_rnn, p["w_br"], name="mm_dy_rnn", bt=True)
    dy_att = _mm(dp_att, p["w_ba"], name="mm_dy_att", out_dtype=BF16, bt=True)
    g["w_branch"] = jnp.concatenate([_mm_tn(y_rnn, dp_rnn, name="mm_dw_br"),
                                     _mm_tn(y_att, dp_att, name="mm_dw_ba")], axis=0)
    delta = _attn_delta(y_att, dy_att)
    first = reduce_first({n: g[n] for n in LATE}) if reduce_first else ()
    dq, dk, dv, *received = _flash_bwd(qh, kh, vh, dy_att, lse.reshape(NH, 1, T), delta.reshape(NH, 1, T),
                                       scatter=first)
    dq_all, dkv_all, dukr = _attn_prep_bwd(dq, dk, dv, tab)
    dqn = _mm(dq_all, p["w_uq"], name="mm_dqn", bt=True)
    dkvn = _mm(dkv_all, p["w_ukv"], name="mm_dkvn", bt=True)
    dwq = _mm_tn(qn, dq_all, name="mm_dw_uq").reshape(QR, NH, QW)
    dwq_rope = dwq[..., NOPE:NOPE + ROPE] + _swap_halves(dwq[..., NOPE + ROPE:])
    g["w_uq"] = jnp.concatenate([dwq[..., :NOPE], dwq_rope], axis=-1).reshape(QR, NH * (NOPE + ROPE))
    g["w_ukv"] = _mm_tn(kvn, dkv_all, name="mm_dw_ukv")
    duq, dg = _rmsnorm_bwd(uq, row(w["q_norm_g"]), dqn, None, name="norm_q_bwd", want_f32=False)
    g["q_norm_g"] = dg
    dukv, dg = _rmsnorm_bwd(ukv, row(w["kv_norm_g"]), dkvn, None, name="norm_kv_bwd", want_f32=False)
    g["kv_norm_g"] = dg
    (dux, dug, g["conv_w"], g["conv_b"], g["w_rec_a"], g["b_rec_a"], g["w_rec_i"], g["b_rec_i"],
     g["lru_lambda"]) = _rnn_bwd(uxg, hs, dy_rnn, *rnn_w, p["wa_t"], p["wi_t"])
    dwkr = _mm_tn(z, dukr, name="mm_dw_kr")
    g["w_in"] = jnp.concatenate([
        _mm_tn(z, dux, name="mm_dw_x"), _mm_tn(z, dug, name="mm_dw_g"),
        _mm_tn(z, duq, name="mm_dw_q"), _mm_tn(z, dukv, name="mm_dw_kv"),
        dwkr[:, :ROPE] + _swap_halves(dwkr[:, ROPE:]),
        _mm_tn(z, dum, name="mm_dw_m")], axis=1)
    second = reduce_second({n: g[n] for n in ("w_in", "w_uq", "w_ukv")}) if reduce_second else ()
    dz, *received2 = _mm_sum(
        [(dux, p["w_x"]), (dug, p["w_g"]), (duq, p["w_q"]), (dukv, p["w_kv"]), (dukr, p["w_kr"]),
         (dum, p["w_m"])], name="mm_dz", scatter=second)
    dh0, dg = _rmsnorm_bwd(h0, row(w["norm_mix_g"]), dz, dh1, name="norm_mix_bwd", want_bf16=False)
    g["norm_mix_g"] = dg
    g["meta_tokens"] = dh0[PAD:FRONT]
    return loss, dh0[FRONT:], g, (list(first) + list(second), received + received2)


HBM = pl.BlockSpec(memory_space=pltpu.HBM)
CHIP_FLIPS = ((1, 0), (0, 1), (1, 1))


def _place():
    return lax.axis_index("x"), lax.axis_index("y"), lax.axis_index("c")


def _flip(v, f):
    return 1 - v if f else v


def _dma_sems(n):
    return pltpu.SemaphoreType.DMA((n,))


def _gather_scratch(srcs):
    n = len(srcs)
    return [pltpu.VMEM(s.shape, s.dtype) for s in srcs] + [_dma_sems(3 * n), _dma_sems(3 * n), _dma_sems(n),
                                                            _dma_sems(n)]


def _gather_descs(src_refs, out_refs, scr, with_loads=True):
    n = len(src_refs)
    stage = scr[:n]
    send_sems, recv_sems, in_sems, local_sems = scr[n:]
    x, y, c = _place()
    me = 2 * x + y
    loads, sends, local = [], [], []
    for a in range(n):
        if with_loads:
            loads.append(pltpu.make_async_copy(src_refs[a], stage[a], in_sems.at[a]))
        for k, (fx, fy) in enumerate(CHIP_FLIPS):
            sends.append(pltpu.make_async_remote_copy(
                src_ref=stage[a], dst_ref=out_refs[a].at[me], send_sem=send_sems.at[3 * a + k],
                recv_sem=recv_sems.at[3 * a + k], device_id=(_flip(x, fx), _flip(y, fy), c),
                device_id_type=MESH))
        local.append(pltpu.make_async_copy(stage[a], out_refs[a].at[me], local_sems.at[a]))
    return loads, sends, local


def _gather_start(descs):
    loads, sends, local = descs
    for cp in loads:
        cp.start()
    for a, cp in enumerate(loads):
        cp.wait()
        for s in sends[3 * a:3 * a + 3]:
            s.start()
        local[a].start()


def _gather_wait(descs):
    _, sends, local = descs
    for cp in sends + local:
        cp.wait()


def _allgather_chips(srcs, *, name):
    n = len(srcs)

    def body(*refs):
        descs = _gather_descs(refs[:n], refs[n:2 * n], refs[2 * n:])
        _gather_start(descs)
        _gather_wait(descs)

    return pl.pallas_call(
        body, name=name, in_specs=[HBM] * n, out_specs=[HBM] * n,
        out_shape=[jax.ShapeDtypeStruct((4,) + s.shape, s.dtype) for s in srcs],
        scratch_shapes=_gather_scratch(srcs),
        compiler_params=pltpu.CompilerParams(vmem_limit_bytes=VMEM_LIMIT),
    )(*srcs)


def _allgather_chips_split(srcs, split, *, name):
    n = len(srcs)

    def body(*refs):
        src, out, stage = refs[:n], refs[n:2 * n], refs[2 * n:3 * n]
        send_a, recv_a, send_b, recv_b, in_sems, local_sems = refs[3 * n:]
        x, y, c = _place()
        me = 2 * x + y
        loads = [pltpu.make_async_copy(src[a], stage[a], in_sems.at[a]) for a in range(n)]
        for cp in loads:
            cp.start()

        def half(a, core):
            h = srcs[a].shape[0] // 2
            return pl.ds(pl.multiple_of(core * h, 16), h)

        ici, local = [], []
        for a in range(n):
            loads[a].wait()
            for k, (fx, fy) in enumerate(CHIP_FLIPS):
                s_ref, d_ref = stage[a], out[a].at[me]
                if split[a]:
                    s_ref, d_ref = stage[a].at[half(a, c), :], out[a].at[me, half(a, c), :]
                cp = pltpu.make_async_remote_copy(
                    src_ref=s_ref, dst_ref=d_ref, send_sem=send_a.at[3 * a + k], recv_sem=recv_a.at[3 * a + k],
                    device_id=(_flip(x, fx), _flip(y, fy), c), device_id_type=MESH)
                cp.start()
                ici.append(cp)
            cp = pltpu.make_async_copy(stage[a], out[a].at[me], local_sems.at[a])
            cp.start()
            local.append(cp)
        passed = []
        for a in range(n):
            if not split[a]:
                continue
            for k, (fx, fy) in enumerate(CHIP_FLIPS):
                ici[3 * a + k].wait_recv()
                there = 2 * _flip(x, fx) + _flip(y, fy)
                cp = pltpu.make_async_remote_copy(
                    src_ref=out[a].at[there, half(a, c), :], dst_ref=out[a].at[there, half(a, c), :],
                    send_sem=send_b.at[3 * a + k], recv_sem=recv_b.at[3 * a + k],
                    device_id=(x, y, 1 - c), device_id_type=MESH)
                cp.start()
                passed.append(cp)
        for a in range(n):
            for k in range(3):
                ici[3 * a + k].wait_send()
                if not split[a]:
                    ici[3 * a + k].wait_recv()
        for cp in passed + local:
            cp.wait()

    return pl.pallas_call(
        body, name=name, in_specs=[HBM] * n, out_specs=[HBM] * n,
        out_shape=[jax.ShapeDtypeStruct((4,) + s.shape, s.dtype) for s in srcs],
        scratch_shapes=[pltpu.VMEM(s.shape, s.dtype) for s in srcs]
        + [_dma_sems(3 * n), _dma_sems(3 * n), _dma_sems(3 * n), _dma_sems(3 * n), _dma_sems(n), _dma_sems(n)],
        compiler_params=pltpu.CompilerParams(vmem_limit_bytes=VMEM_LIMIT),
    )(*srcs)


def _scatter_descs(src_refs, out_refs, scr):
    send_sems, recv_sems = scr
    x, y, c = _place()
    copies = []
    for a in range(len(src_refs)):
        for k, (fx, fy) in enumerate(CHIP_FLIPS):
            px, py = _flip(x, fx), _flip(y, fy)
            copies.append(pltpu.make_async_remote_copy(
                src_ref=src_refs[a].at[2 * px + py], dst_ref=out_refs[a].at[k],
                send_sem=send_sems.at[3 * a + k], recv_sem=recv_sems.at[3 * a + k],
                device_id=(px, py, c), device_id_type=MESH))
    return copies


def _scatter_chips(srcs, *, name):
    n = len(srcs)

    def body(*refs):
        copies = _scatter_descs(refs[:n], refs[n:2 * n], refs[2 * n:])
        for cp in copies:
            cp.start()
        for cp in copies:
            cp.wait()

    return pl.pallas_call(
        body, name=name, in_specs=[HBM] * n, out_specs=[HBM] * n,
        out_shape=[jax.ShapeDtypeStruct((3,) + s.shape[1:], s.dtype) for s in srcs],
        scratch_shapes=[_dma_sems(3 * n), _dma_sems(3 * n)],
    )(*srcs)


def _sibling_take(srcs, *, name):
    n = len(srcs)

    def body(*refs):
        src_refs, out_refs = refs[:n], refs[n:2 * n]
        send_sems, recv_sems = refs[2 * n:]
        x, y, c = _place()
        copies = []
        for a in range(n):
            h = srcs[a].shape[1] // 2
            theirs = pl.ds(pl.multiple_of((1 - c) * h, 8), h)
            cp = pltpu.make_async_remote_copy(
                src_ref=src_refs[a].at[:, theirs, :], dst_ref=out_refs[a], send_sem=send_sems.at[a],
                recv_sem=recv_sems.at[a], device_id=(x, y, 1 - c), device_id_type=MESH)
            cp.start()
            copies.append(cp)
        for cp in copies:
            cp.wait()

    return pl.pallas_call(
        body, name=name, in_specs=[HBM] * n, out_specs=[HBM] * n,
        out_shape=[jax.ShapeDtypeStruct((4, s.shape[1] // 2, s.shape[2]), s.dtype) for s in srcs],
        scratch_shapes=[_dma_sems(n), _dma_sems(n)],
    )(*srcs)


def _sibling_swap(srcs, *, name):
    n = len(srcs)

    def body(*refs):
        src_refs, out_refs = refs[:n], refs[n:2 * n]
        send_sems, recv_sems = refs[2 * n:]
        x, y, c = _place()
        copies = []
        for a in range(n):
            cp = pltpu.make_async_remote_copy(
                src_ref=src_refs[a], dst_ref=out_refs[a], send_sem=send_sems.at[a],
                recv_sem=recv_sems.at[a], device_id=(x, y, 1 - c), device_id_type=MESH)
            cp.start()
            copies.append(cp)
        for cp in copies:
            cp.wait()

    return pl.pallas_call(
        body, name=name, in_specs=[HBM] * n, out_specs=[HBM] * n,
        out_shape=[jax.ShapeDtypeStruct(s.shape, s.dtype) for s in srcs],
        scratch_shapes=[_dma_sems(n), _dma_sems(n)],
    )(*srcs)


def _row_tile(rows, cols, n_arrays, step=16):
    budget = 24 * 1024 * 1024 // (2 * 4 * n_arrays * cols)
    best = step
    for t in range(step, rows + 1, step):
        if rows % t == 0 and t <= budget:
            best = t
    assert rows % best == 0, (rows, cols)
    return best


def _add_halves(full, theirs, core, wire, *, name):
    _, h, c = theirs.shape
    tm = _row_tile(h, c, 3)
    nb = h // tm

    def body(core_ref, a_ref, b_ref, o_ref):
        o_ref[...] = (a_ref[...] + b_ref[...]).astype(wire)

    spec = pl.BlockSpec((None, tm, c), lambda s, i, core_ref: (s, i, 0))
    grid_spec = pltpu.PrefetchScalarGridSpec(
        num_scalar_prefetch=1, grid=(4, nb),
        in_specs=[pl.BlockSpec((None, tm, c), lambda s, i, core_ref: (s, core_ref[0] * nb + i, 0)), spec],
        out_specs=spec)
    return pl.pallas_call(
        body, name=name, grid_spec=grid_spec, out_shape=jax.ShapeDtypeStruct(theirs.shape, wire),
        compiler_params=_cparams(("parallel", "parallel")),
    )(core.reshape(1), full, theirs)


def _sum4(own, recv, *, name):
    h, c = own.shape
    tm = _row_tile(h, c, 5)

    def body(o_ref, r_ref, out_ref):
        f = lambda k: r_ref[k].astype(F32)
        out_ref[...] = ((o_ref[...].astype(F32) + f(0)) + f(1)) + f(2)

    return pl.pallas_call(
        body, name=name, grid=(h // tm,),
        in_specs=[_rows(tm, c), pl.BlockSpec((3, tm, c), lambda i: (0, i, 0))],
        out_specs=_rows(tm, c), out_shape=jax.ShapeDtypeStruct((h, c), F32),
        compiler_params=_cparams(("parallel",)),
    )(own, recv)


def _adamw(g, w, m, v, *, name):
    r, c = g.shape
    tm = _row_tile(r, c, 7, step=8)
    c1 = 1.0 / (1.0 - ADAM_B1 ** ADAM_STEP)
    c2 = 1.0 / (1.0 - ADAM_B2 ** ADAM_STEP)

    def body(g_ref, w_ref, m_ref, v_ref, d_ref, nm_ref, nv_ref):
        gv = g_ref[...]
        nm = ADAM_B1 * m_ref[...] + (1.0 - ADAM_B1) * gv
        nv = ADAM_B2 * v_ref[...] + (1.0 - ADAM_B2) * (gv * gv)
        nm_ref[...] = nm
        nv_ref[...] = nv
        d_ref[...] = -ADAM_LR * ((nm * c1) / (jnp.sqrt(nv * c2) + ADAM_EPS) + ADAM_WD * w_ref[...])

    spec = _rows(tm, c)
    shape = jax.ShapeDtypeStruct((r, c), F32)
    return pl.pallas_call(
        body, name=name, grid=(r // tm,), in_specs=[spec] * 4, out_specs=[spec] * 3,
        out_shape=[shape] * 3, compiler_params=_cparams(("parallel",)),
    )(g, w, m, v)


def _adamw_halves(mine, theirs, core, w, m, v, *, name):
    h, c = mine.shape
    tm = _row_tile(h, c, 10, step=8)
    nb = h // tm
    c1 = 1.0 / (1.0 - ADAM_B1 ** ADAM_STEP)
    c2 = 1.0 / (1.0 - ADAM_B2 ** ADAM_STEP)

    def body(core_ref, a_ref, b_ref, w_ref, m_ref, v_ref, g_ref, d_ref, nm_ref, nv_ref):
        gv = jnp.where(pl.program_id(0) // nb == core_ref[0], a_ref[...], b_ref[...])
        nm = ADAM_B1 * m_ref[...] + (1.0 - ADAM_B1) * gv
        nv = ADAM_B2 * v_ref[...] + (1.0 - ADAM_B2) * (gv * gv)
        g_ref[...] = gv
        nm_ref[...] = nm
        nv_ref[...] = nv
        d_ref[...] = -ADAM_LR * ((nm * c1) / (jnp.sqrt(nv * c2) + ADAM_EPS) + ADAM_WD * w_ref[...])

    half = pl.BlockSpec((tm, c), lambda i, core_ref: (i % nb, 0))
    spec = pl.BlockSpec((tm, c), lambda i, core_ref: (i, 0))
    grid_spec = pltpu.PrefetchScalarGridSpec(
        num_scalar_prefetch=1, grid=(2 * nb,), in_specs=[half, half, spec, spec, spec], out_specs=[spec] * 4)
    return pl.pallas_call(
        body, name=name, grid_spec=grid_spec, out_shape=[jax.ShapeDtypeStruct((2 * h, c), F32)] * 4,
        compiler_params=_cparams(("parallel",)),
    )(core.reshape(1), mine, theirs, w, m, v)


BIG = (("w_in", (D, 1328), 1), ("w_uq", (QR, 384), 1), ("w_ukv", (KVR, 512), 1), ("w_branch", (576, D), 0),
       ("w_out", (256, D), 0), ("w_ffn_in", (D, 1408), 1), ("w_ffn_out", (704, D), 0))
SMALL = (("meta_tokens", (NMETA, 256), 1), ("b_gate", (2, 256), 1), ("conv_w", (CW, 320), 1))
REPL = (("norm_mix_g", (D,)), ("conv_b", (DR,)), ("w_rec_a", (NBLK, RB, RB)), ("b_rec_a", (DR,)),
        ("w_rec_i", (NBLK, RB, RB)), ("b_rec_i", (DR,)), ("lru_lambda", (DR,)), ("q_norm_g", (QR,)),
        ("kv_norm_g", (KVR,)), ("norm_ffn_g", (D,)), ("final_norm_g", (D,)))
WEIGHTS = ("meta_tokens", "norm_mix_g", "w_in", "b_gate", "conv_w", "conv_b", "w_rec_a", "b_rec_a", "w_rec_i",
           "b_rec_i", "lru_lambda", "q_norm_g", "w_uq", "kv_norm_g", "w_ukv", "w_branch", "w_out", "norm_ffn_g",
           "w_ffn_in", "w_ffn_out", "final_norm_g")
W = 1024
SMALL_N = sum(math.prod(s) for _, s, _ in SMALL)
SMALL_ROWS = 8
REPL_N = sum(math.prod(s) for _, s in REPL)
QUART_ROWS = 88
assert SMALL_N <= SMALL_ROWS * W and REPL_N <= 4 * QUART_ROWS * W


def _flat_pad(parts, rows):
    v = jnp.concatenate([p.reshape(-1) for p in parts])
    return jnp.pad(v, (0, rows * W - v.shape[0])).reshape(rows, W)


def _shard_stack(full, shard_shape, axis):
    r, cs = shard_shape
    if axis == 0:
        return full.reshape(4, r, cs)
    return jnp.stack([full[:, s * cs:(s + 1) * cs] for s in range(4)])


def _unshard(stack, axis):
    if axis == 0:
        return stack.reshape(4 * stack.shape[1], stack.shape[2])
    return jnp.concatenate([stack[s] for s in range(4)], axis=1)


def _split(flat, table):
    out, off = {}, 0
    for name, shape, *_ in table:
        n = math.prod(shape)
        out[name] = flat[..., off:off + n].reshape(flat.shape[:-1] + tuple(shape))
        off += n
    return out


def kernel(x, meta_tokens, norm_mix_g, w_in, b_gate, conv_w, conv_b, w_rec_a, b_rec_a, w_rec_i, b_rec_i, lru_lambda, q_norm_g, w_uq, kv_norm_g, w_ukv, w_branch, w_out, norm_ffn_g, w_ffn_in, w_ffn_out, final_norm_g, loss_target, m_meta_tokens, m_norm_mix_g, m_w_in, m_b_gate, m_conv_w, m_conv_b, m_w_rec_a, m_b_rec_a, m_w_rec_i, m_b_rec_i, m_lru_lambda, m_q_norm_g, m_w_uq, m_kv_norm_g, m_w_ukv, m_w_branch, m_w_out, m_norm_ffn_g, m_w_ffn_in, m_w_ffn_out, m_final_norm_g, v_meta_tokens, v_norm_mix_g, v_w_in, v_b_gate, v_conv_w, v_conv_b, v_w_rec_a, v_b_rec_a, v_w_rec_i, v_b_rec_i, v_lru_lambda, v_q_norm_g, v_w_uq, v_kv_norm_g, v_w_ukv, v_w_branch, v_w_out, v_norm_ffn_g, v_w_ffn_in, v_w_ffn_out, v_final_norm_g):
    args = dict(locals())
    chip = 2 * lax.axis_index("x") + lax.axis_index("y")
    core = lax.axis_index("c")

    first_big = [b for b in BIG if b[0] not in LATE]
    late_big = [b for b in BIG if b[0] in LATE]
    bf16_shard = lambda n, s: args[n].reshape(s).astype(BF16)
    small = _flat_pad([args[n] for n, _, _ in SMALL], SMALL_ROWS)
    gathered = _allgather_chips_split([bf16_shard(n, s) for n, s, _ in first_big] + [small],
                                      [True] * len(first_big) + [False], name="gather_weights")
    w = {}
    for (name, _, axis), stack in zip(first_big, gathered):
        w[name] = _unshard(stack, axis)
    small_parts = _split(gathered[-1].reshape(4, SMALL_ROWS * W), SMALL)
    for name, _, axis in SMALL:
        w[name] = _unshard(small_parts[name], axis)
    for name, shape in REPL:
        w[name] = args[name].reshape(shape)
    finish_late = lambda stacks: {name: _unshard(st, axis) for (name, _, axis), st in zip(late_big, stacks)}

    def to_wire(red, tag, wires):
        theirs = _sibling_take(red, name="reduce_sibling_" + tag)
        return [_add_halves(a, t, core, wires[k], name=f"add_sibling_{tag}{k}")
                for k, (a, t) in enumerate(zip(red, theirs))]

    reduce_first = lambda gl: to_wire([_shard_stack(gl[n], s, a) for n, s, a in late_big], "a",
                                      [BF16] * len(late_big))
    reduce_second = lambda gl: to_wire([_shard_stack(gl[n], s, a) for n, s, a in first_big], "b",
                                       [BF16] * len(first_big))
    loss, grad_x, g, (parts_ab, recv_ab) = _local_step(
        x[0], loss_target[0], w, late=([bf16_shard(n, s) for n, s, _ in late_big], finish_late),
        reduce_first=reduce_first, reduce_second=reduce_second)
    loss = lax.psum(loss, ("x", "y", "c"))

    small_g = jnp.concatenate([_shard_stack(g[n], s, a).reshape(4, -1) for n, s, a in SMALL], axis=1)
    small_g = jnp.pad(small_g, ((0, 0), (0, SMALL_ROWS * W - SMALL_N))).reshape(4, SMALL_ROWS, W)
    repl_g = _flat_pad([g[n] for n, _ in REPL], 4 * QUART_ROWS).reshape(4, QUART_ROWS, W)
    parts_c = to_wire([jnp.concatenate([small_g, repl_g], axis=1)], "c", [F32])
    recv_c = _scatter_chips(parts_c, name="reduce_chips")
    order = [b[0] for b in late_big] + [b[0] for b in first_big] + ["misc"]
    halves = [_sum4(lax.dynamic_index_in_dim(p, chip, 0, keepdims=False), r, name="sum_chips_" + n)
              for n, p, r in zip(order, list(parts_ab) + parts_c, list(recv_ab) + list(recv_c))]
    others = _sibling_swap(halves, name="share_sibling")

    results = {}
    shape_of = {name: shape for name, shape, _ in BIG}
    for name, mine, theirs in zip(order[:-1], halves, others):
        shape = shape_of[name]
        results[name] = _adamw_halves(mine, theirs, core, args[name].reshape(shape),
                                      args["m_" + name].reshape(shape), args["v_" + name].reshape(shape),
                                      name="adamw_" + name)

    a, b = halves[-1], others[-1]
    g_mine = jnp.where(core == 0, jnp.concatenate([a, b], axis=0), jnp.concatenate([b, a], axis=0))
    g_repl = _allgather_chips([g_mine[SMALL_ROWS:]], name="gather_repl")[0].reshape(4 * QUART_ROWS, W)
    g_misc = jnp.concatenate([g_mine[:SMALL_ROWS], g_repl], axis=0)
    misc_state = lambda prefix: jnp.concatenate(
        [_flat_pad([args[prefix + n] for n, _, _ in SMALL], SMALL_ROWS),
         _flat_pad([args[prefix + n] for n, _ in REPL], 4 * QUART_ROWS)], axis=0)
    d, nm, nv = _adamw(g_misc, misc_state(""), misc_state("m_"), misc_state("v_"), name="adamw_misc")
    misc = (g_misc, d, nm, nv)

    outs = []
    for k in range(4):
        sm = _split(misc[k][:SMALL_ROWS].reshape(-1), SMALL)
        rp = _split(misc[k][SMALL_ROWS:].reshape(-1), REPL)
        for name in WEIGHTS:
            val = results[name][k] if name in results else (sm[name] if name in sm else rp[name])
            outs.append(val.reshape(args[name].shape))
    return (loss, grad_x[None], *outs)
```

```python
import functools
import math

import jax
import jax.numpy as jnp
from jax import lax
from jax.experimental import pallas as pl
from jax.experimental.pallas import tpu as pltpu

F32 = jnp.float32
BF16 = jnp.bfloat16

D = 1024
DR = 1280
NBLK = 10
RB = 128
CW = 4
NH = 8
NOPE = 128
ROPE = 64
VD = 128
QR = 384
KVR = 256
DFF = 2816
NMETA = 16
EPS = 1e-6
LRU_C = 8.0
ROPE_THETA = 10000.0
SCALE = 1.0 / math.sqrt(NOPE + ROPE)
NEG = -1e30
FRONT = 128
PAD = FRONT - NMETA
QW = 2 * NOPE
LANES = 128
SUB = 128
CHAINS = 4
VMEM_LIMIT = 52 * 1024 * 1024

ADAM_LR = 0.001
ADAM_B1 = 0.9
ADAM_B2 = 0.999
ADAM_EPS = 1e-08
ADAM_WD = 0.01
ADAM_STEP = 10

MESH = pl.DeviceIdType.MESH


def _cparams(sem):
    return pltpu.CompilerParams(dimension_semantics=sem, vmem_limit_bytes=VMEM_LIMIT)


def _sigmoid(x):
    return 1.0 / (1.0 + jnp.exp(-x))


def _gelu_parts(x):
    c = math.sqrt(2.0 / math.pi)
    inner = c * (x + 0.044715 * x * x * x)
    t = jnp.tanh(inner)
    g = 0.5 * x * (1.0 + t)
    dg = 0.5 * (1.0 + t) + 0.5 * x * (1.0 - t * t) * c * (1.0 + 3.0 * 0.044715 * x * x)
    return g, dg


def _divisors(n, step, cap):
    return [d for d in range(step, min(n, cap) + 1, step) if n % d == 0] or [n]


MM_VMEM_BUDGET = 40 * 1024 * 1024
MM_MAX_ROWS = 1664
MM_MAX_COLS = 1408


def _mm_tiles(M, K, N, a_item, out_item, has_res):
    best = None
    for tn in _divisors(N, LANES, MM_MAX_COLS):
        for tm in _divisors(M, 16, MM_MAX_ROWS):
            need = 2 * (tm * K * a_item + K * tn * 2 + tm * tn * (out_item + (4 if has_res else 0)))
            if need <= MM_VMEM_BUDGET and (best is None or tm * tn > best[0] * best[1]):
                best = (tm, tn)
    assert best is not None, (M, K, N)
    return best


_NT_DIMS = (((1,), (1,)), ((), ()))


def _mm(a, b, *, name, out_dtype=F32, res=None, bt=False):
    M, K = a.shape
    N = b.shape[0] if bt else b.shape[1]
    has_res = res is not None
    tm, tn = _mm_tiles(M, K, N, a.dtype.itemsize, jnp.dtype(out_dtype).itemsize, has_res)

    def body(*refs):
        if has_res:
            a_ref, b_ref, r_ref, o_ref = refs
        else:
            a_ref, b_ref, o_ref = refs
        av, bv = a_ref[...].astype(BF16), b_ref[...].astype(BF16)
        if bt:
            acc = lax.dot_general(av, bv, _NT_DIMS, preferred_element_type=F32)
        else:
            acc = jnp.dot(av, bv, preferred_element_type=F32)
        if has_res:
            acc = acc + r_ref[...].astype(F32)
        o_ref[...] = acc.astype(o_ref.dtype)

    a_bytes = M * K * a.dtype.itemsize
    b_bytes = K * N * b.dtype.itemsize
    rows_outer = a_bytes + (M // tm) * b_bytes <= b_bytes + (N // tn) * a_bytes
    if rows_outer:
        grid = (M // tm, N // tn)
        ia, ib, io = (lambda i, j: (i, 0)), (lambda i, j: (0, j)), (lambda i, j: (i, j))
        ibt = lambda i, j: (j, 0)
    else:
        grid = (N // tn, M // tm)
        ia, ib, io = (lambda j, i: (i, 0)), (lambda j, i: (0, j)), (lambda j, i: (i, j))
        ibt = lambda j, i: (j, 0)
    in_specs = [pl.BlockSpec((tm, K), ia), pl.BlockSpec((tn, K), ibt) if bt else pl.BlockSpec((K, tn), ib)]
    args = [a, b]
    if has_res:
        in_specs.append(pl.BlockSpec((tm, tn), io))
        args.append(res)
    return pl.pallas_call(
        body, name=name, grid=grid, in_specs=in_specs,
        out_specs=pl.BlockSpec((tm, tn), io),
        out_shape=jax.ShapeDtypeStruct((M, N), out_dtype),
        compiler_params=_cparams(("parallel", "parallel")),
    )(*args)


def _mm_take(a, b, take, *, name):
    M, K = a.shape
    N = b.shape[0]
    tm, tn = _mm_tiles(M, K, N, a.dtype.itemsize, 4, False)
    gm, gn = M // tm, N // tn
    n = len(take)
    heights = [t.shape[1] // 2 for t in take]

    def body(*refs):
        a_ref, b_ref = refs[:2]
        t_src = refs[2:2 + n]
        o_ref = refs[2 + n]
        t_out = refs[3 + n:3 + 2 * n]
        t_scr = refs[3 + 2 * n:]
        i, j = pl.program_id(0), pl.program_id(1)

        @pl.when((i == 0) & (j == 0))
        def _():
            for cp in _take_descs(t_src, t_out, t_scr, heights):
                cp.start()

        o_ref[...] = lax.dot_general(a_ref[...].astype(BF16), b_ref[...].astype(BF16), _NT_DIMS,
                                     preferred_element_type=F32)

        @pl.when((i == gm - 1) & (j == gn - 1))
        def _():
            for cp in _take_descs(t_src, t_out, t_scr, heights):
                cp.wait()

    return pl.pallas_call(
        body, name=name, grid=(gm, gn),
        in_specs=[pl.BlockSpec((tm, K), lambda i, j: (i, 0)), pl.BlockSpec((tn, K), lambda i, j: (j, 0))]
        + [HBM] * n,
        out_specs=[pl.BlockSpec((tm, tn), lambda i, j: (i, j))] + [HBM] * n,
        out_shape=[jax.ShapeDtypeStruct((M, N), F32)]
        + [jax.ShapeDtypeStruct((4, h, t.shape[2]), t.dtype) for t, h in zip(take, heights)],
        scratch_shapes=[_dma_sems(n), _dma_sems(n)],
        compiler_params=_cparams(("arbitrary", "arbitrary")),
    )(a, b, *take)


def _mm_sum(pairs, *, name, scatter=(), out_dtype=F32):
    M = pairs[0][0].shape[0]
    N = pairs[0][1].shape[0]
    ks = [a.shape[1] for a, _ in pairs]
    tm, tn = _mm_tiles(M, sum(ks), N, 2, jnp.dtype(out_dtype).itemsize, False)
    n = len(pairs)
    ns = len(scatter)
    gm, gn = M // tm, N // tn

    def body(*refs):
        s_src = refs[2 * n:2 * n + ns]
        o_ref = refs[2 * n + ns]
        s_out = refs[2 * n + ns + 1:2 * n + 2 * ns + 1]
        s_scr = refs[2 * n + 2 * ns + 1:]
        i, j = pl.program_id(0), pl.program_id(1)
        if ns:
            @pl.when((i == 0) & (j == 0))
            def _():
                for cp in _scatter_descs(s_src, s_out, s_scr):
                    cp.start()

        acc = None
        for k in range(n):
            d = lax.dot_general(refs[2 * k][...].astype(BF16), refs[2 * k + 1][...].astype(BF16), _NT_DIMS,
                                preferred_element_type=F32)
            acc = d if acc is None else acc + d
        o_ref[...] = acc.astype(o_ref.dtype)

        if ns:
            @pl.when((i == gm - 1) & (j == gn - 1))
            def _():
                for cp in _scatter_descs(s_src, s_out, s_scr):
                    cp.wait()

    in_specs, args = [], []
    for (a, b), kk in zip(pairs, ks):
        in_specs += [pl.BlockSpec((tm, kk), lambda i, j: (i, 0)), pl.BlockSpec((tn, kk), lambda i, j: (j, 0))]
        args += [a, b]
    return pl.pallas_call(
        body, name=name, grid=(gm, gn), in_specs=in_specs + [HBM] * ns,
        out_specs=[pl.BlockSpec((tm, tn), lambda i, j: (i, j))] + [HBM] * ns,
        out_shape=[jax.ShapeDtypeStruct((M, N), out_dtype)]
        + [jax.ShapeDtypeStruct((3,) + s.shape[1:], s.dtype) for s in scatter],
        scratch_shapes=[_dma_sems(3 * ns), _dma_sems(3 * ns)] if ns else [],
        compiler_params=_cparams(("arbitrary", "arbitrary")),
    )(*args, *scatter)


def _mm_tn(a, b, *, name):
    T, K1 = a.shape
    N = b.shape[1]
    tt = _divisors(T, 16, MM_MAX_ROWS)[-1]
    tk = _divisors(K1, LANES, MM_MAX_COLS)[-1]
    tn = _divisors(N, LANES, MM_MAX_COLS)[-1]

    def body(a_ref, b_ref, o_ref):
        @pl.when(pl.program_id(2) == 0)
        def _():
            o_ref[...] = jnp.zeros_like(o_ref)

        o_ref[...] += lax.dot_general(a_ref[...].astype(BF16), b_ref[...].astype(BF16),
                                      (((0,), (0,)), ((), ())), preferred_element_type=F32)

    return pl.pallas_call(
        body, name=name, grid=(K1 // tk, N // tn, T // tt),
        in_specs=[pl.BlockSpec((tt, tk), lambda i, j, t: (t, i)),
                  pl.BlockSpec((tt, tn), lambda i, j, t: (t, j))],
        out_specs=pl.BlockSpec((tk, tn), lambda i, j, t: (i, j)),
        out_shape=jax.ShapeDtypeStruct((K1, N), F32),
        compiler_params=_cparams(("parallel", "parallel", "arbitrary")),
    )(a, b)


def _rows(tm, w, cb=0):
    return pl.BlockSpec((tm, w), lambda i: (i, cb))


def _const(shape):
    n = len(shape)
    return pl.BlockSpec(shape, lambda i: (0,) * n)


def _rmsnorm_fwd(x, g, *, name, tm=640):
    T, C = x.shape

    def body(x_ref, g_ref, o_ref):
        xv = x_ref[...]
        r = lax.rsqrt(jnp.mean(xv * xv, axis=-1, keepdims=True) + EPS)
        o_ref[...] = ((xv * r) * g_ref[...]).astype(BF16)

    return pl.pallas_call(
        body, name=name, grid=(T // tm,),
        in_specs=[_rows(tm, C), _const((1, C))],
        out_specs=_rows(tm, C),
        out_shape=jax.ShapeDtypeStruct((T, C), BF16),
        compiler_params=_cparams(("parallel",)),
    )(x, g)


def _rmsnorm_bwd(x, g, dy, res, *, name, tm=640, want_f32=True, want_bf16=True):
    T, C = x.shape
    has_res = res is not None

    def body(*refs):
        refs = list(refs)
        x_ref, g_ref, dy_ref = refs[:3]
        refs = refs[3:]
        r_ref = refs.pop(0) if has_res else None
        o32 = refs.pop(0) if want_f32 else None
        o16 = refs.pop(0) if want_bf16 else None
        dg_ref = refs.pop(0)

        @pl.when(pl.program_id(0) == 0)
        def _():
            dg_ref[...] = jnp.zeros_like(dg_ref)

        xv = x_ref[...]
        dyv = dy_ref[...].astype(F32)
        r = lax.rsqrt(jnp.mean(xv * xv, axis=-1, keepdims=True) + EPS)
        xn = xv * r
        dg_ref[...] += jnp.sum(dyv * xn, axis=0, keepdims=True)
        dxn = dyv * g_ref[...]
        dx = r * (dxn - xn * jnp.mean(dxn * xn, axis=-1, keepdims=True))
        if has_res:
            dx = dx + r_ref[...]
        if want_f32:
            o32[...] = dx
        if want_bf16:
            o16[...] = dx.astype(BF16)

    in_specs = [_rows(tm, C), _const((1, C)), _rows(tm, C)]
    args = [x, g, dy]
    if has_res:
        in_specs.append(_rows(tm, C))
        args.append(res)
    out_specs, out_shape = [], []
    if want_f32:
        out_specs.append(_rows(tm, C))
        out_shape.append(jax.ShapeDtypeStruct((T, C), F32))
    if want_bf16:
        out_specs.append(_rows(tm, C))
        out_shape.append(jax.ShapeDtypeStruct((T, C), BF16))
    out_specs.append(_const((1, C)))
    out_shape.append(jax.ShapeDtypeStruct((1, C), F32))
    return pl.pallas_call(
        body, name=name, grid=(T // tm,), in_specs=in_specs, out_specs=out_specs,
        out_shape=out_shape, compiler_params=_cparams(("arbitrary",)),
    )(*args)


def _gate_mix_fwd(um, bg, p_rnn, p_att, *, tm=320):
    T = um.shape[0]

    def body(um_ref, bg_ref, pr_ref, pa_ref, o_ref):
        g = _sigmoid(um_ref[...].astype(F32) + bg_ref[...])
        o_ref[...] = (g[:, :D] * pr_ref[...].astype(F32) + g[:, D:] * pa_ref[...].astype(F32)).astype(BF16)

    return pl.pallas_call(
        body, name="gate_mix_fwd", grid=(T // tm,),
        in_specs=[_rows(tm, 2 * D), _const((1, 2 * D)), _rows(tm, D), _rows(tm, D)],
        out_specs=_rows(tm, D),
        out_shape=jax.ShapeDtypeStruct((T, D), BF16),
        compiler_params=_cparams(("parallel",)),
    )(um, bg, p_rnn, p_att)


def _gate_mix_bwd(um, bg, p_rnn, p_att, dmixed, *, tm=320):
    T = um.shape[0]

    def body(um_ref, bg_ref, pr_ref, pa_ref, dm_ref, dpr_ref, dpa_ref, dum_ref, dbg_ref):
        @pl.when(pl.program_id(0) == 0)
        def _():
            dbg_ref[...] = jnp.zeros_like(dbg_ref)

        g = _sigmoid(um_ref[...].astype(F32) + bg_ref[...])
        g0, g1 = g[:, :D], g[:, D:]
        dm = dm_ref[...].astype(F32)
        dpr_ref[...] = (dm * g0).astype(BF16)
        dpa_ref[...] = (dm * g1).astype(BF16)
        d0 = dm * pr_ref[...].astype(F32) * g0 * (1.0 - g0)
        d1 = dm * pa_ref[...].astype(F32) * g1 * (1.0 - g1)
        dum_ref[:, :D] = d0.astype(BF16)
        dum_ref[:, D:] = d1.astype(BF16)
        dbg_ref[:, :D] += jnp.sum(d0, axis=0, keepdims=True)
        dbg_ref[:, D:] += jnp.sum(d1, axis=0, keepdims=True)

    return pl.pallas_call(
        body, name="gate_mix_bwd", grid=(T // tm,),
        in_specs=[_rows(tm, 2 * D), _const((1, 2 * D)), _rows(tm, D), _rows(tm, D), _rows(tm, D)],
        out_specs=[_rows(tm, D), _rows(tm, D), _rows(tm, 2 * D), _const((1, 2 * D))],
        out_shape=[jax.ShapeDtypeStruct((T, D), BF16), jax.ShapeDtypeStruct((T, D), BF16),
                   jax.ShapeDtypeStruct((T, 2 * D), BF16), jax.ShapeDtypeStruct((1, 2 * D), F32)],
        compiler_params=_cparams(("arbitrary",)),
    )(um, bg, p_rnn, p_att, dmixed)


def _swiglu_fwd(ff, *, tm=320):
    T = ff.shape[0]

    def body(g_ref, u_ref, o_ref):
        gv = g_ref[...].astype(F32)
        o_ref[...] = (gv * _sigmoid(gv) * u_ref[...].astype(F32)).astype(BF16)

    return pl.pallas_call(
        body, name="swiglu_fwd", grid=(T // tm,),
        in_specs=[_rows(tm, DFF, 0), _rows(tm, DFF, 1)],
        out_specs=_rows(tm, DFF),
        out_shape=jax.ShapeDtypeStruct((T, DFF), BF16),
        compiler_params=_cparams(("parallel",)),
    )(ff, ff)


def _swiglu_bwd(ff, dact, *, tm=320):
    T = ff.shape[0]

    def body(g_ref, u_ref, da_ref, o_ref):
        gv = g_ref[...].astype(F32)
        s = _sigmoid(gv)
        da = da_ref[...].astype(F32)
        o_ref[:, :DFF] = (da * u_ref[...].astype(F32) * s * (1.0 + gv * (1.0 - s))).astype(BF16)
        o_ref[:, DFF:] = (da * gv * s).astype(BF16)

    return pl.pallas_call(
        body, name="swiglu_bwd", grid=(T // tm,),
        in_specs=[_rows(tm, DFF, 0), _rows(tm, DFF, 1), _rows(tm, DFF)],
        out_specs=_rows(tm, 2 * DFF),
        out_shape=jax.ShapeDtypeStruct((T, 2 * DFF), BF16),
        compiler_params=_cparams(("parallel",)),
    )(ff, ff, dact)


def _loss_head(h2, tgt, g, *, tm=640):
    T = h2.shape[0]
    nsub = tm // FRONT

    def body(h_ref, *refs):
        t_refs = refs[:nsub]
        g_ref, d32_ref, d16_ref, dg_ref, ls_ref = refs[nsub:]
        i = pl.program_id(0)

        @pl.when(i == 0)
        def _():
            dg_ref[...] = jnp.zeros_like(dg_ref)
            ls_ref[...] = jnp.zeros_like(ls_ref)

        gv = g_ref[...]
        for k in range(nsub):
            rows = pl.ds(k * FRONT, FRONT)
            xv = h_ref[rows, :]
            r = lax.rsqrt(jnp.mean(xv * xv, axis=-1, keepdims=True) + EPS)
            xn = xv * r
            e = jnp.where(i * nsub + k >= 1, xn * gv - t_refs[k][...], 0.0)
            ls_ref[...] += jnp.sum(e * e, axis=0, keepdims=True)
            dy = e * (1.0 / D)
            dg_ref[...] += jnp.sum(dy * xn, axis=0, keepdims=True)
            dxn = dy * gv
            dx = r * (dxn - xn * jnp.mean(dxn * xn, axis=-1, keepdims=True))
            d32_ref[rows, :] = dx
            d16_ref[rows, :] = dx.astype(BF16)

    def t_spec(k):
        return pl.BlockSpec((FRONT, D), lambda i: (jnp.maximum(i * nsub + k - 1, 0), 0))

    return pl.pallas_call(
        body, name="loss_head", grid=(T // tm,),
        in_specs=[_rows(tm, D)] + [t_spec(k) for k in range(nsub)] + [_const((1, D))],
        out_specs=[_rows(tm, D), _rows(tm, D), _const((1, D)), _const((1, D))],
        out_shape=[jax.ShapeDtypeStruct((T, D), F32), jax.ShapeDtypeStruct((T, D), BF16),
                   jax.ShapeDtypeStruct((1, D), F32), jax.ShapeDtypeStruct((1, D), F32)],
        compiler_params=_cparams(("arbitrary",)),
    )(h2, *([tgt] * nsub), g)


def _scan_fwd(a, b, h_in):
    n = a.shape[0]
    row = lax.broadcasted_iota(jnp.int32, a.shape, 0)
    s = 1
    while s < n:
        if s % 8:
            a_sh = jnp.where(row >= s, pltpu.roll(a, s, 0), 1.0)
            b_sh = jnp.where(row >= s, pltpu.roll(b, s, 0), 0.0)
        else:
            a_sh = jnp.concatenate([jnp.ones((s, RB), F32), a[:n - s]], axis=0)
            b_sh = jnp.concatenate([jnp.zeros((s, RB), F32), b[:n - s]], axis=0)
        b = a * b_sh + b
        a = a * a_sh
        s *= 2
    return b + a * h_in


def _scan_rev(a, b, g_in):
    n = a.shape[0]
    row = lax.broadcasted_iota(jnp.int32, a.shape, 0)
    s = 1
    while s < n:
        if s % 8:
            a_sh = jnp.where(row < n - s, pltpu.roll(a, n - s, 0), 1.0)
            b_sh = jnp.where(row < n - s, pltpu.roll(b, n - s, 0), 0.0)
        else:
            a_sh = jnp.concatenate([a[s:], jnp.ones((s, RB), F32)], axis=0)
            b_sh = jnp.concatenate([b[s:], jnp.zeros((s, RB), F32)], axis=0)
        b = a * b_sh + b
        a = a * a_sh
        s *= 2
    return b + a * g_in


def _lru_gates(xc, wa, ba, wi, bi, lam):
    xcb = xc.astype(BF16)
    r = _sigmoid(jnp.dot(xcb, wa, preferred_element_type=F32) + ba)
    ig = _sigmoid(jnp.dot(xcb, wi, preferred_element_type=F32) + bi)
    log_sig = jnp.minimum(lam, 0.0) - jnp.log(1.0 + jnp.exp(-jnp.abs(lam)))
    log_a = LRU_C * r * log_sig
    a = jnp.exp(log_a)
    m2 = jnp.tanh(-log_a) * (1.0 + a * a)
    return r, ig, log_sig, a, m2 * lax.rsqrt(jnp.maximum(m2, 1e-37))


def _rnn_specs(tc, nblk_t, rev):
    def tmap(k):
        return (nblk_t - 1 - k) if rev else k

    hb = tc // 8
    blk = lambda off: pl.BlockSpec((tc, RB), lambda c, k: (tmap(k), c + off))
    halo = lambda off: pl.BlockSpec((8, RB), lambda c, k: (jnp.maximum(tmap(k) * hb - 1, 0), c + off))
    vec = pl.BlockSpec((1, RB), lambda c, k: (0, c))
    cwv = pl.BlockSpec((CW, RB), lambda c, k: (0, c))
    mat = pl.BlockSpec((None, RB, RB), lambda c, k: (c, 0, 0))
    return blk, halo, vec, cwv, mat


def _rnn_fwd(uxg, cw, cb, wa, ba, wi, bi, lam, *, tc=640):
    T = uxg.shape[0]
    nt = T // tc
    nsub = tc // SUB
    blk, halo, vec, cwv, mat = _rnn_specs(tc, nt, False)

    def body(x_ref, xh_ref, ug_ref, cw_ref, cb_ref, wa_ref, ba_ref, wi_ref, bi_ref, lam_ref,
             h_ref, y_ref, xb, hc):
        k = pl.program_id(1)

        @pl.when(k == 0)
        def _():
            hc[...] = jnp.zeros_like(hc)

        xb[0:8, :] = jnp.where(k > 0, xh_ref[...], 0.0)
        xb[8:, :] = x_ref[...]
        cwv_, cbv = cw_ref[...], cb_ref[...]
        wav, wiv = wa_ref[...], wi_ref[...]
        bav, biv, lamv = ba_ref[...], bi_ref[...], lam_ref[...]
        h_in = hc[0:1, :]
        for sc in range(nsub):
            r0 = sc * SUB
            xc = cbv + cwv_[0:1, :] * xb[pl.ds(5 + r0, SUB), :]
            for j in range(1, CW):
                xc = xc + cwv_[j:j + 1, :] * xb[pl.ds(5 + j + r0, SUB), :]
            r, ig, _, a, mm = _lru_gates(xc, wav, bav, wiv, biv, lamv)
            rows = k * tc + r0 + lax.broadcasted_iota(jnp.int32, (SUB, RB), 0)
            b = jnp.where(rows >= PAD, mm * (ig * xc), 0.0)
            h = _scan_fwd(a, b, h_in)
            h_in = h[SUB - 1:SUB, :]
            h_ref[pl.ds(r0, SUB), :] = h
            gl, _ = _gelu_parts(ug_ref[pl.ds(r0, SUB), :])
            y_ref[pl.ds(r0, SUB), :] = (h * gl).astype(BF16)
        hc[0:1, :] = h_in

    return pl.pallas_call(
        body, name="rnn_fwd", grid=(NBLK, nt),
        in_specs=[blk(0), halo(0), blk(NBLK), cwv, vec, mat, vec, mat, vec, vec],
        out_specs=[blk(0), blk(0)],
        out_shape=[jax.ShapeDtypeStruct((T, DR), F32), jax.ShapeDtypeStruct((T, DR), BF16)],
        scratch_shapes=[pltpu.VMEM((tc + 8, RB), F32), pltpu.VMEM((8, RB), F32)],
        compiler_params=_cparams(("parallel", "arbitrary")),
    )(uxg, uxg, uxg, cw, cb, wa, ba, wi, bi, lam)


def _rnn_bwd(uxg, hs, dy, cw, cb, wa, ba, wi, bi, lam, wat, wit, *, tc=640):
    T = uxg.shape[0]
    nt = T // tc
    nsub = tc // SUB
    blk, halo, vec, cwv, mat = _rnn_specs(tc, nt, True)

    def body(x_ref, xh_ref, ug_ref, h_ref, hh_ref, dy_ref, cw_ref, cb_ref, wa_ref, ba_ref, wi_ref,
             bi_ref, lam_ref, wat_ref, wit_ref,
             dux_ref, dug_ref, dcw_ref, dcb_ref, dwa_ref, dba_ref, dwi_ref, dbi_ref, dlam_ref,
             xb, hb, ab, dxb, xcs, rs, igs, mms, dgas, dgis, carry):
        k = pl.program_id(1)
        kt = nt - 1 - k

        @pl.when(k == 0)
        def _():
            carry[...] = jnp.zeros_like(carry)
            for ref in (dcw_ref, dcb_ref, dwa_ref, dba_ref, dwi_ref, dbi_ref, dlam_ref):
                ref[...] = jnp.zeros_like(ref)

        xb[0:8, :] = jnp.where(kt > 0, xh_ref[...], 0.0)
        xb[8:, :] = x_ref[...]
        hb[0:8, :] = jnp.where(kt > 0, hh_ref[...], 0.0)
        hb[8:, :] = h_ref[...]
        cwv_, cbv = cw_ref[...], cb_ref[...]
        wav, wiv = wa_ref[...], wi_ref[...]
        bav, biv, lamv = ba_ref[...], bi_ref[...], lam_ref[...]
        ab[tc:tc + 8, :] = jnp.broadcast_to(carry[1:2, :], (8, RB))
        dxb[tc:tc + 8, :] = carry[8:16, :]
        log_sig = None
        for sc in range(nsub):
            r0 = sc * SUB
            xc = cbv + cwv_[0:1, :] * xb[pl.ds(5 + r0, SUB), :]
            for j in range(1, CW):
                xc = xc + cwv_[j:j + 1, :] * xb[pl.ds(5 + j + r0, SUB), :]
            r, ig, log_sig, a, mm = _lru_gates(xc, wav, bav, wiv, biv, lamv)
            xcs[pl.ds(r0, SUB), :] = xc
            rs[pl.ds(r0, SUB), :] = r
            igs[pl.ds(r0, SUB), :] = ig
            mms[pl.ds(r0, SUB), :] = mm
            ab[pl.ds(r0, SUB), :] = a
        sig_neg = _sigmoid(-lamv)
        g_in = carry[0:1, :]
        dlam_acc = jnp.zeros((1, RB), F32)
        for sc in reversed(range(nsub)):
            r0 = sc * SUB
            xc, r, ig, mm = xcs[pl.ds(r0, SUB), :], rs[pl.ds(r0, SUB), :], igs[pl.ds(r0, SUB), :], mms[pl.ds(r0, SUB), :]
            a = ab[pl.ds(r0, SUB), :]
            a_next = ab[pl.ds(r0 + 1, SUB), :]
            hv = hb[pl.ds(8 + r0, SUB), :]
            hprev = hb[pl.ds(7 + r0, SUB), :]
            dyv = dy_ref[pl.ds(r0, SUB), :]
            gl, dgl = _gelu_parts(ug_ref[pl.ds(r0, SUB), :])
            dug_ref[pl.ds(r0, SUB), :] = (dyv * hv * dgl).astype(BF16)
            G = _scan_rev(a_next, dyv * gl, g_in)
            g_in = G[0:1, :]
            rows = kt * tc + r0 + lax.broadcasted_iota(jnp.int32, (SUB, RB), 0)
            db = jnp.where(rows >= PAD, G, 0.0)
            da = G * hprev
            dmm = db * (ig * xc)
            di = db * (mm * xc)
            dxc = db * (mm * ig)
            dlog_a = da * a - dmm * (a * a) / jnp.maximum(mm, 1e-30)
            dr = dlog_a * (LRU_C * log_sig)
            dlam_acc = dlam_acc + jnp.sum(dlog_a * (LRU_C * r), axis=0, keepdims=True)
            dga = dr * r * (1.0 - r)
            dgi = di * ig * (1.0 - ig)
            dgab, dgib = dga.astype(BF16), dgi.astype(BF16)
            dgas[pl.ds(r0, SUB), :] = dgab
            dgis[pl.ds(r0, SUB), :] = dgib
            dba_ref[...] += jnp.sum(dga, axis=0, keepdims=True)
            dbi_ref[...] += jnp.sum(dgi, axis=0, keepdims=True)
            dxc = dxc + jnp.dot(dgab, wat_ref[...], preferred_element_type=F32) \
                + jnp.dot(dgib, wit_ref[...], preferred_element_type=F32)
            dxb[pl.ds(r0, SUB), :] = dxc
        dlam_ref[...] += dlam_acc * sig_neg
        xcb = xcs[...].astype(BF16)
        tn = (((0,), (0,)), ((), ()))
        dwa_ref[...] += lax.dot_general(xcb, dgas[...], tn, preferred_element_type=F32)
        dwi_ref[...] += lax.dot_general(xcb, dgis[...], tn, preferred_element_type=F32)
        dxc_all = dxb[0:tc, :]
        dcb_ref[...] += jnp.sum(dxc_all, axis=0, keepdims=True)
        rows_all = kt * tc + lax.broadcasted_iota(jnp.int32, (tc, RB), 0)
        dux = jnp.zeros((tc, RB), F32)
        for j in range(CW):
            dcw_ref[j:j + 1, :] += jnp.sum(dxc_all * xb[pl.ds(5 + j, tc), :], axis=0, keepdims=True)
            dux = dux + cwv_[j:j + 1, :] * dxb[pl.ds(CW - 1 - j, tc), :]
        dux_ref[...] = jnp.where(rows_all >= PAD, dux, 0.0).astype(BF16)
        carry[0:1, :] = g_in
        carry[1:2, :] = ab[0:1, :]
        carry[8:16, :] = dxb[0:8, :]

    vec_out = pl.BlockSpec((1, RB), lambda c, k: (0, c))
    return pl.pallas_call(
        body, name="rnn_bwd", grid=(NBLK, nt),
        in_specs=[blk(0), halo(0), blk(NBLK), blk(0), halo(0), blk(0), cwv, vec, mat, vec, mat, vec, vec, mat, mat],
        out_specs=[blk(0), blk(0), cwv, vec_out, mat, vec_out, mat, vec_out, vec_out],
        out_shape=[jax.ShapeDtypeStruct((T, DR), BF16), jax.ShapeDtypeStruct((T, DR), BF16),
                   jax.ShapeDtypeStruct((CW, DR), F32), jax.ShapeDtypeStruct((1, DR), F32),
                   jax.ShapeDtypeStruct((NBLK, RB, RB), F32), jax.ShapeDtypeStruct((1, DR), F32),
                   jax.ShapeDtypeStruct((NBLK, RB, RB), F32), jax.ShapeDtypeStruct((1, DR), F32),
                   jax.ShapeDtypeStruct((1, DR), F32)],
        scratch_shapes=[pltpu.VMEM((tc + 8, RB), F32), pltpu.VMEM((tc + 8, RB), F32),
                        pltpu.VMEM((tc + 8, RB), F32), pltpu.VMEM((tc + 8, RB), F32),
                        pltpu.VMEM((tc, RB), F32), pltpu.VMEM((tc, RB), F32), pltpu.VMEM((tc, RB), F32),
                        pltpu.VMEM((tc, RB), F32), pltpu.VMEM((tc, RB), BF16), pltpu.VMEM((tc, RB), BF16),
                        pltpu.VMEM((16, RB), F32)],
        compiler_params=_cparams(("parallel", "arbitrary")),
    )(uxg, uxg, uxg, hs, hs, dy, cw, cb, wa, ba, wi, bi, lam, wat, wit)


def _attn_prep(q_all, kv_all, ukr, tab, *, tm=320):
    T = q_all.shape[0]

    def body(q_ref, kv_ref, kr_ref, tab_ref, qo_ref, ko_ref, vo_ref):
        tab_v = tab_ref[...]
        lane = lax.broadcasted_iota(jnp.int32, (tm, LANES), 1)
        t1 = kr_ref[...] * tab_v
        kro = jnp.where(lane < ROPE, t1 + pltpu.roll(t1, ROPE, 1), 0.0).astype(BF16)
        for h in range(NH):
            c0 = h * QW
            qo_ref[h, :, 0:NOPE] = (q_ref[:, c0:c0 + NOPE].astype(F32) * SCALE).astype(BF16)
            t2 = q_ref[:, c0 + NOPE:c0 + QW].astype(F32) * tab_v
            qo_ref[h, :, NOPE:QW] = ((t2 + pltpu.roll(t2, ROPE, 1)) * SCALE).astype(BF16)
            ko_ref[h, :, 0:NOPE] = kv_ref[:, c0:c0 + NOPE].astype(BF16)
            ko_ref[h, :, NOPE:QW] = kro
            vo_ref[h, :, :] = kv_ref[:, c0 + NOPE:c0 + QW].astype(BF16)

    return pl.pallas_call(
        body, name="attn_prep", grid=(T // tm,),
        in_specs=[_rows(tm, NH * QW), _rows(tm, NH * QW), _rows(tm, LANES), _rows(tm, LANES)],
        out_specs=[pl.BlockSpec((NH, tm, QW), lambda i: (0, i, 0)), pl.BlockSpec((NH, tm, QW), lambda i: (0, i, 0)),
                   pl.BlockSpec((NH, tm, VD), lambda i: (0, i, 0))],
        out_shape=[jax.ShapeDtypeStruct((NH, T, QW), BF16), jax.ShapeDtypeStruct((NH, T, QW), BF16),
                   jax.ShapeDtypeStruct((NH, T, VD), BF16)],
        compiler_params=_cparams(("parallel",)),
    )(q_all, kv_all, ukr, tab)


def _attn_prep_bwd(dq, dk, dv, tab, *, tm=320):
    T = dq.shape[1]

    def body(dq_ref, dk_ref, dv_ref, tab_ref, dqa_ref, dkva_ref, dkr_ref):
        tab_v = tab_ref[...]
        lane = lax.broadcasted_iota(jnp.int32, (tm, LANES), 1)
        dkro = jnp.zeros((tm, LANES), F32)
        for h in range(NH):
            c0 = h * QW
            dqa_ref[:, c0:c0 + NOPE] = (dq_ref[h, :, 0:NOPE] * SCALE).astype(BF16)
            d2 = dq_ref[h, :, NOPE:QW]
            dqa_ref[:, c0 + NOPE:c0 + QW] = ((d2 + pltpu.roll(d2, ROPE, 1)) * tab_v * SCALE).astype(BF16)
            dkva_ref[:, c0:c0 + NOPE] = dk_ref[h, :, 0:NOPE].astype(BF16)
            dkva_ref[:, c0 + NOPE:c0 + QW] = dv_ref[h, :, :].astype(BF16)
            dkro = dkro + dk_ref[h, :, NOPE:QW].astype(F32)
        dkro = jnp.where(lane < ROPE, dkro, 0.0)
        dkr_ref[...] = ((dkro + pltpu.roll(dkro, ROPE, 1)) * tab_v).astype(BF16)

    return pl.pallas_call(
        body, name="attn_prep_bwd", grid=(T // tm,),
        in_specs=[pl.BlockSpec((NH, tm, QW), lambda i: (0, i, 0)), pl.BlockSpec((NH, tm, QW), lambda i: (0, i, 0)),
                  pl.BlockSpec((NH, tm, VD), lambda i: (0, i, 0)), _rows(tm, LANES)],
        out_specs=[_rows(tm, NH * QW), _rows(tm, NH * QW), _rows(tm, LANES)],
        out_shape=[jax.ShapeDtypeStruct((T, NH * QW), BF16), jax.ShapeDtypeStruct((T, NH * QW), BF16),
                   jax.ShapeDtypeStruct((T, LANES), BF16)],
        compiler_params=_cparams(("parallel",)),
    )(dq, dk, dv, tab)


def _visible(q0, k0, nq, nk):
    rows = q0 + lax.broadcasted_iota(jnp.int32, (nq, nk), 0)
    cols = k0 + lax.broadcasted_iota(jnp.int32, (nq, nk), 1)
    return ((cols >> 6) <= (rows >> 6)) & (cols >= PAD)


def _visible_t(q0, k0, nq, nk):
    cols = k0 + lax.broadcasted_iota(jnp.int32, (nk, nq), 0)
    rows = q0 + lax.broadcasted_iota(jnp.int32, (nk, nq), 1)
    return ((cols >> 6) <= (rows >> 6)) & (cols >= PAD)


_NT = (((1,), (1,)), ((), ()))
ATTN_BLOCK = 1664


def _attn_block(T):
    return ATTN_BLOCK if T % ATTN_BLOCK == 0 else 640


def _round_up(n, m):
    return -(-n // m) * m


def _flash_fwd(q, k, v, *, gather=(), bq=None):
    T = q.shape[1]
    bq = bq or _attn_block(T)
    nq = T // bq
    rs = bq // CHAINS
    n = len(gather)

    def body(*refs):
        q_ref, k_ref, v_ref = refs[:3]
        g_src = refs[3:3 + n]
        o_ref, lse_ref = refs[3 + n:5 + n]
        g_out = refs[5 + n:5 + 2 * n]
        scr = refs[5 + 2 * n:]
        m_s, l_s, acc_s = scr[:CHAINS], scr[CHAINS:2 * CHAINS], scr[2 * CHAINS:3 * CHAINS]
        g_scr = scr[3 * CHAINS:]
        h = pl.program_id(0)
        i = pl.program_id(1)
        if n:
            @pl.when((h == 0) & (i == 0))
            def _():
                _gather_start(_gather_descs(g_src, g_out, g_scr))

        for r in range(CHAINS):
            m_s[r][...] = jnp.full_like(m_s[r], NEG)
            l_s[r][...] = jnp.zeros_like(l_s[r])
            acc_s[r][...] = jnp.zeros_like(acc_s[r])

        def step(j, masked, diag):
            off = pl.multiple_of(j * bq, bq)
            for r in range(CHAINS):
                rows = pl.ds(r * rs, rs)
                kw = min(bq, _round_up((r + 1) * rs, LANES)) if diag else bq
                kv_ = k_ref[pl.ds(off, kw), :]
                vv = v_ref[pl.ds(off, kw), :]
                s = lax.dot_general(q_ref[rows, :], kv_, _NT, preferred_element_type=F32)
                if masked:
                    s = jnp.where(_visible(i * bq + r * rs, j * bq, rs, kw), s, NEG)
                m_prev = m_s[r][...]
                m_new = jnp.maximum(m_prev, jnp.max(s, axis=-1, keepdims=True))
                p = jnp.exp(s - m_new)
                alpha = jnp.exp(m_prev - m_new)
                l_s[r][...] = alpha * l_s[r][...] + jnp.sum(p, axis=-1, keepdims=True)
                acc_s[r][...] = alpha * acc_s[r][...] + jnp.dot(p.astype(BF16), vv, preferred_element_type=F32)
                m_s[r][...] = m_new

        @pl.when(i == 0)
        def _():
            step(0, True, True)

        @pl.when(i > 0)
        def _():
            step(0, True, False)

            def loop(j, c):
                step(j, False, False)
                return c

            lax.fori_loop(1, i, loop, 0)
            step(i, True, True)

        for r in range(CHAINS):
            rows = pl.ds(r * rs, rs)
            o_ref[rows, :] = (acc_s[r][...] / l_s[r][...]).astype(BF16)
            lse_ref[rows, :] = m_s[r][...] + jnp.log(l_s[r][...])

        if n:
            @pl.when((h == NH - 1) & (i == nq - 1))
            def _():
                _gather_wait(_gather_descs(g_src, g_out, g_scr, with_loads=False))

    return pl.pallas_call(
        body, name="flash_fwd", grid=(NH, nq),
        in_specs=[pl.BlockSpec((None, bq, QW), lambda h, i: (h, i, 0)),
                  pl.BlockSpec((None, T, QW), lambda h, i: (h, 0, 0)),
                  pl.BlockSpec((None, T, VD), lambda h, i: (h, 0, 0))] + [HBM] * n,
        out_specs=[pl.BlockSpec((bq, VD), lambda h, i: (i, h)),
                   pl.BlockSpec((None, bq, 1), lambda h, i: (h, i, 0))] + [HBM] * n,
        out_shape=[jax.ShapeDtypeStruct((T, NH * VD), BF16), jax.ShapeDtypeStruct((NH, T, 1), F32)]
        + [jax.ShapeDtypeStruct((4,) + g.shape, g.dtype) for g in gather],
        scratch_shapes=[pltpu.VMEM((rs, 1), F32)] * (2 * CHAINS) + [pltpu.VMEM((rs, VD), F32)] * CHAINS
        + (_gather_scratch(gather) if n else []),
        compiler_params=_cparams(("arbitrary", "arbitrary")),
    )(q, k, v, *gather)


def _attn_delta(o, do, *, tm=640):
    T = o.shape[0]

    def body(o_ref, do_ref, d_ref):
        prod = o_ref[...].astype(F32) * do_ref[...].astype(F32)
        for h in range(NH):
            col = jnp.sum(prod[:, h * VD:(h + 1) * VD], axis=-1, keepdims=True)
            d_ref[h, :, :] = jnp.broadcast_to(col, (tm, LANES)).T[0:1, :]

    return pl.pallas_call(
        body, name="attn_delta", grid=(T // tm,),
        in_specs=[_rows(tm, NH * VD), _rows(tm, NH * VD)],
        out_specs=pl.BlockSpec((NH, 1, tm), lambda i: (0, 0, i)),
        out_shape=jax.ShapeDtypeStruct((NH, 1, T), F32),
        compiler_params=_cparams(("parallel",)),
    )(o, do)


_TN = (((0,), (0,)), ((), ()))


def _flash_bwd(q, k, v, do, lse_row, delta_row, *, scatter=(), bq=None):
    T = q.shape[1]
    bq = bq or _attn_block(T)
    nq = T // bq
    rs = bq // CHAINS
    n = len(scatter)

    def body(*refs):
        q_ref, k_ref, v_ref, do_ref, lse_ref, dl_ref = refs[:6]
        s_src = refs[6:6 + n]
        dq_ref, dk_out, dv_out = refs[6 + n:9 + n]
        s_out = refs[9 + n:9 + 2 * n]
        dk_ref, dv_ref = refs[9 + 2 * n:11 + 2 * n]
        s_scr = refs[11 + 2 * n:]
        h = pl.program_id(0)
        j = pl.program_id(1)
        if n:
            @pl.when((h == 0) & (j == 0))
            def _():
                for cp in _scatter_descs(s_src, s_out, s_scr):
                    cp.start()

        @pl.when(j == 0)
        def _():
            dq_ref[...] = jnp.zeros_like(dq_ref)

        dk_ref[...] = jnp.zeros_like(dk_ref)
        dv_ref[...] = jnp.zeros_like(dv_ref)

        def step(i, masked, diag):
            for r in range(CHAINS):
                rows = pl.ds(r * rs, rs)
                q0 = (r * rs) // LANES * LANES if diag else 0
                qn = bq - q0
                off = pl.multiple_of(i * bq + q0, LANES)
                qv = q_ref[pl.ds(off, qn), :]
                dov = do_ref[pl.ds(off, qn), :]
                lse_v = lse_ref[:, pl.ds(off, qn)]
                dl_v = dl_ref[:, pl.ds(off, qn)]
                st = lax.dot_general(k_ref[rows, :], qv, _NT, preferred_element_type=F32)
                if masked:
                    st = jnp.where(_visible_t(i * bq + q0, j * bq + r * rs, qn, rs), st, NEG)
                pt = jnp.exp(st - lse_v)
                dv_ref[rows, :] += jnp.dot(pt.astype(BF16), dov, preferred_element_type=F32)
                dpt = lax.dot_general(v_ref[rows, :], dov, _NT, preferred_element_type=F32)
                dst = (pt * (dpt - dl_v)).astype(BF16)
                dk_ref[rows, :] += jnp.dot(dst, qv, preferred_element_type=F32)
                dq_ref[pl.ds(off, qn), :] += lax.dot_general(dst, k_ref[rows, :], _TN,
                                                             preferred_element_type=F32)

        step(j, True, True)

        @pl.when(j == 0)
        def _():
            def loop(i, c):
                step(i, True, False)
                return c
            lax.fori_loop(1, nq, loop, 0)

        @pl.when(j > 0)
        def _():
            def loop(i, c):
                step(i, False, False)
                return c
            lax.fori_loop(j + 1, nq, loop, 0)

        dk_out[...] = dk_ref[...].astype(BF16)
        dv_out[...] = dv_ref[...].astype(BF16)

        if n:
            @pl.when((h == NH - 1) & (j == nq - 1))
            def _():
                for cp in _scatter_descs(s_src, s_out, s_scr):
                    cp.wait()

    return pl.pallas_call(
        body, name="flash_bwd", grid=(NH, nq),
        in_specs=[pl.BlockSpec((None, T, QW), lambda h, j: (h, 0, 0)),
                  pl.BlockSpec((None, bq, QW), lambda h, j: (h, j, 0)),
                  pl.BlockSpec((None, bq, VD), lambda h, j: (h, j, 0)),
                  pl.BlockSpec((T, VD), lambda h, j: (0, h)),
                  pl.BlockSpec((None, 1, T), lambda h, j: (h, 0, 0)),
                  pl.BlockSpec((None, 1, T), lambda h, j: (h, 0, 0))] + [HBM] * n,
        out_specs=[pl.BlockSpec((None, T, QW), lambda h, j: (h, 0, 0)),
                   pl.BlockSpec((None, bq, QW), lambda h, j: (h, j, 0)),
                   pl.BlockSpec((None, bq, VD), lambda h, j: (h, j, 0))] + [HBM] * n,
        out_shape=[jax.ShapeDtypeStruct((NH, T, QW), F32), jax.ShapeDtypeStruct((NH, T, QW), BF16),
                   jax.ShapeDtypeStruct((NH, T, VD), BF16)]
        + [jax.ShapeDtypeStruct((3,) + s.shape[1:], s.dtype) for s in scatter],
        scratch_shapes=[pltpu.VMEM((bq, QW), F32), pltpu.VMEM((bq, VD), F32)]
        + ([_dma_sems(3 * n), _dma_sems(3 * n)] if n else []),
        compiler_params=_cparams(("arbitrary", "arbitrary")),
    )(q, k, v, do, lse_row, delta_row, *scatter)


def _rope_table(T):
    pos = (jnp.arange(T, dtype=jnp.int32) - PAD).astype(F32)
    inv_freq = ROPE_THETA ** (-jnp.arange(0, ROPE, 2, dtype=F32) / ROPE)
    ang = pos[:, None] * inv_freq[None, :]
    cos, sin = jnp.cos(ang), jnp.sin(ang)
    return jnp.concatenate([cos, cos, -sin, sin], axis=1)


def _swap_halves(w):
    return jnp.concatenate([w[..., ROPE // 2:], w[..., :ROPE // 2]], axis=-1)


O_UX, O_UG, O_UQ, O_UKV, O_UKR, O_UM = 0, DR, 2 * DR, 2 * DR + QR, 2 * DR + QR + KVR, 2 * DR + QR + KVR + ROPE


def _prep_weights(w):
    b = lambda a: a.astype(BF16)
    w_in = w["w_in"]
    kr = w_in[:, O_UKR:O_UM]
    p = {
        "w_xg": b(w_in[:, :O_UQ]),
        "w_q": b(w_in[:, O_UQ:O_UKV]),
        "w_kv": b(w_in[:, O_UKV:O_UKR]),
        "w_kr": b(jnp.concatenate([kr, _swap_halves(kr)], axis=1)),
        "w_m": b(w_in[:, O_UM:]),
    }
    wq = w["w_uq"].reshape(QR, NH, NOPE + ROPE)
    p["w_uq"] = b(jnp.concatenate([wq, _swap_halves(wq[..., NOPE:])], axis=-1).reshape(QR, NH * QW))
    p["w_ukv"] = b(w["w_ukv"])
    p["w_x"], p["w_g"] = p["w_xg"][:, :DR], p["w_xg"][:, DR:]
    p["wa"] = b(w["w_rec_a"])
    p["wi"] = b(w["w_rec_i"])
    p["wa_t"] = jnp.swapaxes(p["wa"], 1, 2)
    p["wi_t"] = jnp.swapaxes(p["wi"], 1, 2)
    return p


def _prep_late_weights(w):
    b = lambda a: a.astype(BF16)
    p = {"w_br": b(w["w_branch"][:DR]), "w_ba": b(w["w_branch"][DR:]), "w_out": b(w["w_out"]),
         "w_fi": b(w["w_ffn_in"]), "w_fo": b(w["w_ffn_out"])}
    return p


LATE = ("w_branch", "w_out", "w_ffn_in", "w_ffn_out")


def _local_step(x, tgt, w, late=None, reduce_first=None, reduce_second=None):
    S = x.shape[0]
    T = FRONT + S
    p = _prep_weights(w)
    tab = _rope_table(T)
    h0 = jnp.concatenate([jnp.zeros((PAD, D), F32), w["meta_tokens"], x], axis=0)
    row = lambda v: v.reshape(1, -1)

    z = _rmsnorm_fwd(h0, row(w["norm_mix_g"]), name="norm_mix")
    uxg = _mm(z, p["w_xg"], name="mm_uxg")
    uq = _mm(z, p["w_q"], name="mm_uq")
    ukv = _mm(z, p["w_kv"], name="mm_ukv")
    ukr = _mm(z, p["w_kr"], name="mm_ukr")
    um = _mm(z, p["w_m"], name="mm_um", out_dtype=BF16)
    rnn_w = (w["conv_w"], row(w["conv_b"]), p["wa"], row(w["b_rec_a"]), p["wi"], row(w["b_rec_i"]),
             row(w["lru_lambda"]))
    hs, y_rnn = _rnn_fwd(uxg, *rnn_w)
    qn = _rmsnorm_fwd(uq, row(w["q_norm_g"]), name="norm_q")
    kvn = _rmsnorm_fwd(ukv, row(w["kv_norm_g"]), name="norm_kv")
    q_all = _mm(qn, p["w_uq"], name="mm_q", out_dtype=BF16)
    kv_all = _mm(kvn, p["w_ukv"], name="mm_kv", out_dtype=BF16)
    qh, kh, vh = _attn_prep(q_all, kv_all, ukr, tab)
    y_att, lse, *stacks = _flash_fwd(qh, kh, vh, gather=late[0] if late else ())
    if late:
        w = {**w, **late[1](stacks)}
    p.update(_prep_late_weights(w))
    p_rnn = _mm(y_rnn, p["w_br"], name="mm_prnn", out_dtype=BF16)
    p_att = _mm(y_att, p["w_ba"], name="mm_patt", out_dtype=BF16)
    bg = row(w["b_gate"])
    mixed = _gate_mix_fwd(um, bg, p_rnn, p_att)
    h1 = _mm(mixed, p["w_out"], name="mm_out", res=h0)
    zf = _rmsnorm_fwd(h1, row(w["norm_ffn_g"]), name="norm_ffn")
    ff = _mm(zf, p["w_fi"], name="mm_ffn_in", out_dtype=BF16)
    act = _swiglu_fwd(ff)
    h2 = _mm(act, p["w_fo"], name="mm_ffn_out", res=h1)

    g = {}
    dh2, dh2b, dg_fin, lsum = _loss_head(h2, tgt, row(w["final_norm_g"]))
    loss = 0.5 * jnp.sum(lsum) / D
    g["final_norm_g"] = dg_fin.reshape(-1)
    dact = _mm(dh2b, p["w_fo"], name="mm_dact", out_dtype=BF16, bt=True)
    g["w_ffn_out"] = _mm_tn(act, dh2b, name="mm_dw_ffn_out")
    dff = _swiglu_bwd(ff, dact)
    dzf = _mm(dff, p["w_fi"], name="mm_dzf", bt=True)
    g["w_ffn_in"] = _mm_tn(zf, dff, name="mm_dw_ffn_in")
    dh1, dh1b, dg = _rmsnorm_bwd(h1, row(w["norm_ffn_g"]), dzf, dh2, name="norm_ffn_bwd")
    g["norm_ffn_g"] = dg
    dmixed = _mm(dh1b, p["w_out"], name="mm_dmixed", out_dtype=BF16, bt=True)
    g["w_out"] = _mm_tn(mixed, dh1b, name="mm_dw_out")
    dp_rnn, dp_att, dum, dbg = _gate_mix_bwd(um, bg, p_rnn, p_att, dmixed)
    g["b_gate"] = dbg.reshape(2, D)
    g["w_branch"] = jnp.concatenate([_mm_tn(y_rnn, dp_rnn, name="mm_dw_br"),
                                     _mm_tn(y_att, dp_att, name="mm_dw_ba")], axis=0)
    if reduce_first:
        stacks = reduce_first[0]({n: g[n] for n in LATE})
        dy_rnn, *theirs = _mm_take(dp_rnn, p["w_br"], stacks, name="mm_dy_rnn")
        first = reduce_first[1](stacks, theirs)
    else:
        dy_rnn, first = _mm(dp_rnn, p["w_br"], name="mm_dy_rnn", bt=True), ()
    dy_att = _mm(dp_att, p["w_ba"], name="mm_dy_att", out_dtype=BF16, bt=True)
    delta = _attn_delta(y_att, dy_att)
    dq, dk, dv, *received = _flash_bwd(qh, kh, vh, dy_att, lse.reshape(NH, 1, T), delta.reshape(NH, 1, T),
                                       scatter=first)
    dq_all, dkv_all, dukr = _attn_prep_bwd(dq, dk, dv, tab)
    dqn = _mm(dq_all, p["w_uq"], name="mm_dqn", bt=True)
    dkvn = _mm(dkv_all, p["w_ukv"], name="mm_dkvn", bt=True)
    dwq = _mm_tn(qn, dq_all, name="mm_dw_uq").reshape(QR, NH, QW)
    dwq_rope = dwq[..., NOPE:NOPE + ROPE] + _swap_halves(dwq[..., NOPE + ROPE:])
    g["w_uq"] = jnp.concatenate([dwq[..., :NOPE], dwq_rope], axis=-1).reshape(QR, NH * (NOPE + ROPE))
    g["w_ukv"] = _mm_tn(kvn, dkv_all, name="mm_dw_ukv")
    duq, dg = _rmsnorm_bwd(uq, row(w["q_norm_g"]), dqn, None, name="norm_q_bwd", want_f32=False)
    g["q_norm_g"] = dg
    dukv, dg = _rmsnorm_bwd(ukv, row(w["kv_norm_g"]), dkvn, None, name="norm_kv_bwd", want_f32=False)
    g["kv_norm_g"] = dg
    (dux, dug, g["conv_w"], g["conv_b"], g["w_rec_a"], g["b_rec_a"], g["w_rec_i"], g["b_rec_i"],
     g["lru_lambda"]) = _rnn_bwd(uxg, hs, dy_rnn, *rnn_w, p["wa_t"], p["wi_t"])
    dwkr = _mm_tn(z, dukr, name="mm_dw_kr")
    g["w_in"] = jnp.concatenate([
        _mm_tn(z, dux, name="mm_dw_x"), _mm_tn(z, dug, name="mm_dw_g"),
        _mm_tn(z, duq, name="mm_dw_q"), _mm_tn(z, dukv, name="mm_dw_kv"),
        dwkr[:, :ROPE] + _swap_halves(dwkr[:, ROPE:]),
        _mm_tn(z, dum, name="mm_dw_m")], axis=1)
    second = reduce_second({n: g[n] for n in ("w_in", "w_uq", "w_ukv")}) if reduce_second else ()
    dz, *received2 = _mm_sum(
        [(dux, p["w_x"]), (dug, p["w_g"]), (duq, p["w_q"]), (dukv, p["w_kv"]), (dukr, p["w_kr"]),
         (dum, p["w_m"])], name="mm_dz", scatter=second)
    dh0, dg = _rmsnorm_bwd(h0, row(w["norm_mix_g"]), dz, dh1, name="norm_mix_bwd", want_bf16=False)
    g["norm_mix_g"] = dg
    g["meta_tokens"] = dh0[PAD:FRONT]
    return loss, dh0[FRONT:], g, (list(first) + list(second), received + received2)


HBM = pl.BlockSpec(memory_space=pltpu.HBM)
CHIP_FLIPS = ((1, 0), (0, 1), (1, 1))


def _place():
    return lax.axis_index("x"), lax.axis_index("y"), lax.axis_index("c")


def _flip(v, f):
    return 1 - v if f else v


def _dma_sems(n):
    return pltpu.SemaphoreType.DMA((n,))


def _gather_scratch(srcs):
    n = len(srcs)
    return [pltpu.VMEM(s.shape, s.dtype) for s in srcs] + [_dma_sems(3 * n), _dma_sems(3 * n), _dma_sems(n),
                                                            _dma_sems(n)]


def _gather_descs(src_refs, out_refs, scr, with_loads=True):
    n = len(src_refs)
    stage = scr[:n]
    send_sems, recv_sems, in_sems, local_sems = scr[n:]
    x, y, c = _place()
    me = 2 * x + y
    loads, sends, local = [], [], []
    for a in range(n):
        if with_loads:
            loads.append(pltpu.make_async_copy(src_refs[a], stage[a], in_sems.at[a]))
        for k, (fx, fy) in enumerate(CHIP_FLIPS):
            sends.append(pltpu.make_async_remote_copy(
                src_ref=stage[a], dst_ref=out_refs[a].at[me], send_sem=send_sems.at[3 * a + k],
                recv_sem=recv_sems.at[3 * a + k], device_id=(_flip(x, fx), _flip(y, fy), c),
                device_id_type=MESH))
        local.append(pltpu.make_async_copy(stage[a], out_refs[a].at[me], local_sems.at[a]))
    return loads, sends, local


def _gather_start(descs):
    loads, sends, local = descs
    for cp in loads:
        cp.start()
    for a, cp in enumerate(loads):
        cp.wait()
        for s in sends[3 * a:3 * a + 3]:
            s.start()
        local[a].start()


def _gather_wait(descs):
    _, sends, local = descs
    for cp in sends + local:
        cp.wait()


def _allgather_chips(srcs, *, name):
    n = len(srcs)

    def body(*refs):
        descs = _gather_descs(refs[:n], refs[n:2 * n], refs[2 * n:])
        _gather_start(descs)
        _gather_wait(descs)

    return pl.pallas_call(
        body, name=name, in_specs=[HBM] * n, out_specs=[HBM] * n,
        out_shape=[jax.ShapeDtypeStruct((4,) + s.shape, s.dtype) for s in srcs],
        scratch_shapes=_gather_scratch(srcs),
        compiler_params=pltpu.CompilerParams(vmem_limit_bytes=VMEM_LIMIT),
    )(*srcs)


def _allgather_chips_split(srcs, split, *, name):
    n = len(srcs)

    def body(*refs):
        src, out, stage = refs[:n], refs[n:2 * n], refs[2 * n:3 * n]
        send_a, recv_a, send_b, recv_b, in_sems, local_sems = refs[3 * n:]
        x, y, c = _place()
        me = 2 * x + y
        loads = [pltpu.make_async_copy(src[a], stage[a], in_sems.at[a]) for a in range(n)]
        for cp in loads:
            cp.start()

        def half(a, core):
            h = srcs[a].shape[0] // 2
            return pl.ds(pl.multiple_of(core * h, 16), h)

        ici, local = [], []
        for a in range(n):
            loads[a].wait()
            for k, (fx, fy) in enumerate(CHIP_FLIPS):
                s_ref, d_ref = stage[a], out[a].at[me]
                if split[a]:
                    s_ref, d_ref = stage[a].at[half(a, c), :], out[a].at[me, half(a, c), :]
                cp = pltpu.make_async_remote_copy(
                    src_ref=s_ref, dst_ref=d_ref, send_sem=send_a.at[3 * a + k], recv_sem=recv_a.at[3 * a + k],
                    device_id=(_flip(x, fx), _flip(y, fy), c), device_id_type=MESH)
                cp.start()
                ici.append(cp)
            cp = pltpu.make_async_copy(stage[a], out[a].at[me], local_sems.at[a])
            cp.start()
            local.append(cp)
        passed = []
        for a in range(n):
            if not split[a]:
                continue
            for k, (fx, fy) in enumerate(CHIP_FLIPS):
                ici[3 * a + k].wait_recv()
                there = 2 * _flip(x, fx) + _flip(y, fy)
                cp = pltpu.make_async_remote_copy(
                    src_ref=out[a].at[there, half(a, c), :], dst_ref=out[a].at[there, half(a, c), :],
                    send_sem=send_b.at[3 * a + k], recv_sem=recv_b.at[3 * a + k],
                    device_id=(x, y, 1 - c), device_id_type=MESH)
                cp.start()
                passed.append(cp)
        for a in range(n):
            for k in range(3):
                ici[3 * a + k].wait_send()
                if not split[a]:
                    ici[3 * a + k].wait_recv()
        for cp in passed + local:
            cp.wait()

    return pl.pallas_call(
        body, name=name, in_specs=[HBM] * n, out_specs=[HBM] * n,
        out_shape=[jax.ShapeDtypeStruct((4,) + s.shape, s.dtype) for s in srcs],
        scratch_shapes=[pltpu.VMEM(s.shape, s.dtype) for s in srcs]
        + [_dma_sems(3 * n), _dma_sems(3 * n), _dma_sems(3 * n), _dma_sems(3 * n), _dma_sems(n), _dma_sems(n)],
        compiler_params=pltpu.CompilerParams(vmem_limit_bytes=VMEM_LIMIT),
    )(*srcs)


def _scatter_descs(src_refs, out_refs, scr):
    send_sems, recv_sems = scr
    x, y, c = _place()
    copies = []
    for a in range(len(src_refs)):
        for k, (fx, fy) in enumerate(CHIP_FLIPS):
            px, py = _flip(x, fx), _flip(y, fy)
            copies.append(pltpu.make_async_remote_copy(
                src_ref=src_refs[a].at[2 * px + py], dst_ref=out_refs[a].at[k],
                send_sem=send_sems.at[3 * a + k], recv_sem=recv_sems.at[3 * a + k],
                device_id=(px, py, c), device_id_type=MESH))
    return copies


def _scatter_chips(srcs, *, name):
    n = len(srcs)

    def body(*refs):
        copies = _scatter_descs(refs[:n], refs[n:2 * n], refs[2 * n:])
        for cp in copies:
            cp.start()
        for cp in copies:
            cp.wait()

    return pl.pallas_call(
        body, name=name, in_specs=[HBM] * n, out_specs=[HBM] * n,
        out_shape=[jax.ShapeDtypeStruct((3,) + s.shape[1:], s.dtype) for s in srcs],
        scratch_shapes=[_dma_sems(3 * n), _dma_sems(3 * n)],
    )(*srcs)


def _take_descs(src_refs, out_refs, scr, heights):
    send_sems, recv_sems = scr
    x, y, c = _place()
    copies = []
    for a, h in enumerate(heights):
        theirs = pl.ds(pl.multiple_of((1 - c) * h, 8), h)
        copies.append(pltpu.make_async_remote_copy(
            src_ref=src_refs[a].at[:, theirs, :], dst_ref=out_refs[a], send_sem=send_sems.at[a],
            recv_sem=recv_sems.at[a], device_id=(x, y, 1 - c), device_id_type=MESH))
    return copies


def _sibling_take(srcs, *, name):
    n = len(srcs)
    heights = [s.shape[1] // 2 for s in srcs]

    def body(*refs):
        copies = _take_descs(refs[:n], refs[n:2 * n], refs[2 * n:], heights)
        for cp in copies:
            cp.start()
        for cp in copies:
            cp.wait()

    return pl.pallas_call(
        body, name=name, in_specs=[HBM] * n, out_specs=[HBM] * n,
        out_shape=[jax.ShapeDtypeStruct((4, s.shape[1] // 2, s.shape[2]), s.dtype) for s in srcs],
        scratch_shapes=[_dma_sems(n), _dma_sems(n)],
    )(*srcs)


def _sibling_swap(srcs, *, name):
    n = len(srcs)

    def body(*refs):
        src_refs, out_refs = refs[:n], refs[n:2 * n]
        send_sems, recv_sems = refs[2 * n:]
        x, y, c = _place()
        copies = []
        for a in range(n):
            cp = pltpu.make_async_remote_copy(
                src_ref=src_refs[a], dst_ref=out_refs[a], send_sem=send_sems.at[a],
                recv_sem=recv_sems.at[a], device_id=(x, y, 1 - c), device_id_type=MESH)
            cp.start()
            copies.append(cp)
        for cp in copies:
            cp.wait()

    return pl.pallas_call(
        body, name=name, in_specs=[HBM] * n, out_specs=[HBM] * n,
        out_shape=[jax.ShapeDtypeStruct(s.shape, s.dtype) for s in srcs],
        scratch_shapes=[_dma_sems(n), _dma_sems(n)],
    )(*srcs)


def _row_tile(rows, cols, n_arrays, step=16):
    budget = 24 * 1024 * 1024 // (2 * 4 * n_arrays * cols)
    best = step
    for t in range(step, rows + 1, step):
        if rows % t == 0 and t <= budget:
            best = t
    assert rows % best == 0, (rows, cols)
    return best


def _add_halves(full, theirs, core, wire, *, name):
    _, h, c = theirs.shape
    tm = _row_tile(h, c, 3)
    nb = h // tm

    def body(core_ref, a_ref, b_ref, o_ref):
        o_ref[...] = (a_ref[...] + b_ref[...]).astype(wire)

    spec = pl.BlockSpec((None, tm, c), lambda s, i, core_ref: (s, i, 0))
    grid_spec = pltpu.PrefetchScalarGridSpec(
        num_scalar_prefetch=1, grid=(4, nb),
        in_specs=[pl.BlockSpec((None, tm, c), lambda s, i, core_ref: (s, core_ref[0] * nb + i, 0)), spec],
        out_specs=spec)
    return pl.pallas_call(
        body, name=name, grid_spec=grid_spec, out_shape=jax.ShapeDtypeStruct(theirs.shape, wire),
        compiler_params=_cparams(("parallel", "parallel")),
    )(core.reshape(1), full, theirs)


def _sum4(own, recv, *, name):
    h, c = own.shape
    tm = _row_tile(h, c, 5)

    def body(o_ref, r_ref, out_ref):
        f = lambda k: r_ref[k].astype(F32)
        out_ref[...] = ((o_ref[...].astype(F32) + f(0)) + f(1)) + f(2)

    return pl.pallas_call(
        body, name=name, grid=(h // tm,),
        in_specs=[_rows(tm, c), pl.BlockSpec((3, tm, c), lambda i: (0, i, 0))],
        out_specs=_rows(tm, c), out_shape=jax.ShapeDtypeStruct((h, c), F32),
        compiler_params=_cparams(("parallel",)),
    )(own, recv)


def _adamw(g, w, m, v, *, name):
    r, c = g.shape
    tm = _row_tile(r, c, 7, step=8)
    c1 = 1.0 / (1.0 - ADAM_B1 ** ADAM_STEP)
    c2 = 1.0 / (1.0 - ADAM_B2 ** ADAM_STEP)

    def body(g_ref, w_ref, m_ref, v_ref, d_ref, nm_ref, nv_ref):
        gv = g_ref[...]
        nm = ADAM_B1 * m_ref[...] + (1.0 - ADAM_B1) * gv
        nv = ADAM_B2 * v_ref[...] + (1.0 - ADAM_B2) * (gv * gv)
        nm_ref[...] = nm
        nv_ref[...] = nv
        d_ref[...] = -ADAM_LR * ((nm * c1) / (jnp.sqrt(nv * c2) + ADAM_EPS) + ADAM_WD * w_ref[...])

    spec = _rows(tm, c)
    shape = jax.ShapeDtypeStruct((r, c), F32)
    return pl.pallas_call(
        body, name=name, grid=(r // tm,), in_specs=[spec] * 4, out_specs=[spec] * 3,
        out_shape=[shape] * 3, compiler_params=_cparams(("parallel",)),
    )(g, w, m, v)


def _adamw_halves(mine, theirs, core, w, m, v, *, name):
    h, c = mine.shape
    tm = _row_tile(h, c, 10, step=8)
    nb = h // tm
    c1 = 1.0 / (1.0 - ADAM_B1 ** ADAM_STEP)
    c2 = 1.0 / (1.0 - ADAM_B2 ** ADAM_STEP)

    def body(core_ref, a_ref, b_ref, w_ref, m_ref, v_ref, g_ref, d_ref, nm_ref, nv_ref):
        gv = jnp.where(pl.program_id(0) // nb == core_ref[0], a_ref[...], b_ref[...])
        nm = ADAM_B1 * m_ref[...] + (1.0 - ADAM_B1) * gv
        nv = ADAM_B2 * v_ref[...] + (1.0 - ADAM_B2) * (gv * gv)
        g_ref[...] = gv
        nm_ref[...] = nm
        nv_ref[...] = nv
        d_ref[...] = -ADAM_LR * ((nm * c1) / (jnp.sqrt(nv * c2) + ADAM_EPS) + ADAM_WD * w_ref[...])

    half = pl.BlockSpec((tm, c), lambda i, core_ref: (i % nb, 0))
    spec = pl.BlockSpec((tm, c), lambda i, core_ref: (i, 0))
    grid_spec = pltpu.PrefetchScalarGridSpec(
        num_scalar_prefetch=1, grid=(2 * nb,), in_specs=[half, half, spec, spec, spec], out_specs=[spec] * 4)
    return pl.pallas_call(
        body, name=name, grid_spec=grid_spec, out_shape=[jax.ShapeDtypeStruct((2 * h, c), F32)] * 4,
        compiler_params=_cparams(("parallel",)),
    )(core.reshape(1), mine, theirs, w, m, v)


BIG = (("w_in", (D, 1328), 1), ("w_uq", (QR, 384), 1), ("w_ukv", (KVR, 512), 1), ("w_branch", (576, D), 0),
       ("w_out", (256, D), 0), ("w_ffn_in", (D, 1408), 1), ("w_ffn_out", (704, D), 0))
SMALL = (("meta_tokens", (NMETA, 256), 1), ("b_gate", (2, 256), 1), ("conv_w", (CW, 320), 1))
REPL = (("norm_mix_g", (D,)), ("conv_b", (DR,)), ("w_rec_a", (NBLK, RB, RB)), ("b_rec_a", (DR,)),
        ("w_rec_i", (NBLK, RB, RB)), ("b_rec_i", (DR,)), ("lru_lambda", (DR,)), ("q_norm_g", (QR,)),
        ("kv_norm_g", (KVR,)), ("norm_ffn_g", (D,)), ("final_norm_g", (D,)))
WEIGHTS = ("meta_tokens", "norm_mix_g", "w_in", "b_gate", "conv_w", "conv_b", "w_rec_a", "b_rec_a", "w_rec_i",
           "b_rec_i", "lru_lambda", "q_norm_g", "w_uq", "kv_norm_g", "w_ukv", "w_branch", "w_out", "norm_ffn_g",
           "w_ffn_in", "w_ffn_out", "final_norm_g")
W = 1024
SMALL_N = sum(math.prod(s) for _, s, _ in SMALL)
SMALL_ROWS = 8
REPL_N = sum(math.prod(s) for _, s in REPL)
QUART_ROWS = 88
assert SMALL_N <= SMALL_ROWS * W and REPL_N <= 4 * QUART_ROWS * W


def _flat_pad(parts, rows):
    v = jnp.concatenate([p.reshape(-1) for p in parts])
    return jnp.pad(v, (0, rows * W - v.shape[0])).reshape(rows, W)


def _shard_stack(full, shard_shape, axis):
    r, cs = shard_shape
    if axis == 0:
        return full.reshape(4, r, cs)
    return jnp.stack([full[:, s * cs:(s + 1) * cs] for s in range(4)])


def _unshard(stack, axis):
    if axis == 0:
        return stack.reshape(4 * stack.shape[1], stack.shape[2])
    return jnp.concatenate([stack[s] for s in range(4)], axis=1)


def _split(flat, table):
    out, off = {}, 0
    for name, shape, *_ in table:
        n = math.prod(shape)
        out[name] = flat[..., off:off + n].reshape(flat.shape[:-1] + tuple(shape))
        off += n
    return out


def kernel(x, meta_tokens, norm_mix_g, w_in, b_gate, conv_w, conv_b, w_rec_a, b_rec_a, w_rec_i, b_rec_i, lru_lambda, q_norm_g, w_uq, kv_norm_g, w_ukv, w_branch, w_out, norm_ffn_g, w_ffn_in, w_ffn_out, final_norm_g, loss_target, m_meta_tokens, m_norm_mix_g, m_w_in, m_b_gate, m_conv_w, m_conv_b, m_w_rec_a, m_b_rec_a, m_w_rec_i, m_b_rec_i, m_lru_lambda, m_q_norm_g, m_w_uq, m_kv_norm_g, m_w_ukv, m_w_branch, m_w_out, m_norm_ffn_g, m_w_ffn_in, m_w_ffn_out, m_final_norm_g, v_meta_tokens, v_norm_mix_g, v_w_in, v_b_gate, v_conv_w, v_conv_b, v_w_rec_a, v_b_rec_a, v_w_rec_i, v_b_rec_i, v_lru_lambda, v_q_norm_g, v_w_uq, v_kv_norm_g, v_w_ukv, v_w_branch, v_w_out, v_norm_ffn_g, v_w_ffn_in, v_w_ffn_out, v_final_norm_g):
    args = dict(locals())
    chip = 2 * lax.axis_index("x") + lax.axis_index("y")
    core = lax.axis_index("c")

    first_big = [b for b in BIG if b[0] not in LATE]
    late_big = [b for b in BIG if b[0] in LATE]
    bf16_shard = lambda n, s: args[n].reshape(s).astype(BF16)
    small = _flat_pad([args[n] for n, _, _ in SMALL], SMALL_ROWS)
    gathered = _allgather_chips_split([bf16_shard(n, s) for n, s, _ in first_big] + [small],
                                      [True] * len(first_big) + [False], name="gather_weights")
    w = {}
    for (name, _, axis), stack in zip(first_big, gathered):
        w[name] = _unshard(stack, axis)
    small_parts = _split(gathered[-1].reshape(4, SMALL_ROWS * W), SMALL)
    for name, _, axis in SMALL:
        w[name] = _unshard(small_parts[name], axis)
    for name, shape in REPL:
        w[name] = args[name].reshape(shape)
    finish_late = lambda stacks: {name: _unshard(st, axis) for (name, _, axis), st in zip(late_big, stacks)}

    def add_theirs(red, theirs, tag, wires):
        return [_add_halves(a, t, core, wires[k], name=f"add_sibling_{tag}{k}")
                for k, (a, t) in enumerate(zip(red, theirs))]

    def to_wire(red, tag, wires):
        return add_theirs(red, _sibling_take(red, name="reduce_sibling_" + tag), tag, wires)

    reduce_first = (lambda gl: [_shard_stack(gl[n], s, a) for n, s, a in late_big],
                    lambda red, theirs: add_theirs(red, theirs, "a", [BF16] * len(late_big)))
    reduce_second = lambda gl: to_wire([_shard_stack(gl[n], s, a) for n, s, a in first_big], "b",
                                       [BF16] * len(first_big))
    loss, grad_x, g, (parts_ab, recv_ab) = _local_step(
        x[0], loss_target[0], w, late=([bf16_shard(n, s) for n, s, _ in late_big], finish_late),
        reduce_first=reduce_first, reduce_second=reduce_second)
    loss = lax.psum(loss, ("x", "y", "c"))

    small_g = jnp.concatenate([_shard_stack(g[n], s, a).reshape(4, -1) for n, s, a in SMALL], axis=1)
    small_g = jnp.pad(small_g, ((0, 0), (0, SMALL_ROWS * W - SMALL_N))).reshape(4, SMALL_ROWS, W)
    repl_g = _flat_pad([g[n] for n, _ in REPL], 4 * QUART_ROWS).reshape(4, QUART_ROWS, W)
    parts_c = to_wire([jnp.concatenate([small_g, repl_g], axis=1)], "c", [F32])
    recv_c = _scatter_chips(parts_c, name="reduce_chips")
    order = [b[0] for b in late_big] + [b[0] for b in first_big] + ["misc"]
    halves = [_sum4(lax.dynamic_index_in_dim(p, chip, 0, keepdims=False), r, name="sum_chips_" + n)
              for n, p, r in zip(order, list(parts_ab) + parts_c, list(recv_ab) + list(recv_c))]
    others = _sibling_swap(halves, name="share_sibling")

    results = {}
    shape_of = {name: shape for name, shape, _ in BIG}
    for name, mine, theirs in zip(order[:-1], halves, others):
        shape = shape_of[name]
        results[name] = _adamw_halves(mine, theirs, core, args[name].reshape(shape),
                                      args["m_" + name].reshape(shape), args["v_" + name].reshape(shape),
                                      name="adamw_" + name)

    a, b = halves[-1], others[-1]
    g_mine = jnp.where(core == 0, jnp.concatenate([a, b], axis=0), jnp.concatenate([b, a], axis=0))
    g_repl = _allgather_chips([g_mine[SMALL_ROWS:]], name="gather_repl")[0].reshape(4 * QUART_ROWS, W)
    g_misc = jnp.concatenate([g_mine[:SMALL_ROWS], g_repl], axis=0)
    misc_state = lambda prefix: jnp.concatenate(
        [_flat_pad([args[prefix + n] for n, _, _ in SMALL], SMALL_ROWS),
         _flat_pad([args[prefix + n] for n, _ in REPL], 4 * QUART_ROWS)], axis=0)
    d, nm, nv = _adamw(g_misc, misc_state(""), misc_state("m_"), misc_state("v_"), name="adamw_misc")
    misc = (g_misc, d, nm, nv)

    outs = []
    for k in range(4):
        sm = _split(misc[k][:SMALL_ROWS].reshape(-1), SMALL)
        rp = _split(misc[k][SMALL_ROWS:].reshape(-1), REPL)
        for name in WEIGHTS:
            val = results[name][k] if name in results else (sm[name] if name in sm else rp[name])
            outs.append(val.reshape(args[name].shape))
    return (loss, grad_x[None], *outs)
```

```python
import functools
import math

import jax
import jax.numpy as jnp
from jax import lax
from jax.experimental import pallas as pl
from jax.experimental.pallas import tpu as pltpu

F32 = jnp.float32
BF16 = jnp.bfloat16

D = 1024
DR = 1280
NBLK = 10
RB = 128
CW = 4
NH = 8
NOPE = 128
ROPE = 64
VD = 128
QR = 384
KVR = 256
DFF = 2816
NMETA = 16
EPS = 1e-6
LRU_C = 8.0
ROPE_THETA = 10000.0
SCALE = 1.0 / math.sqrt(NOPE + ROPE)
NEG = -1e30
FRONT = 128
PAD = FRONT - NMETA
QW = 2 * NOPE
LANES = 128
SUB = 128
CHAINS = 4
VMEM_LIMIT = 52 * 1024 * 1024

ADAM_LR = 0.001
ADAM_B1 = 0.9
ADAM_B2 = 0.999
ADAM_EPS = 1e-08
ADAM_WD = 0.01
ADAM_STEP = 10

MESH = pl.DeviceIdType.MESH


def _cparams(sem):
    return pltpu.CompilerParams(dimension_semantics=sem, vmem_limit_bytes=VMEM_LIMIT)


def _sigmoid(x):
    return 1.0 / (1.0 + jnp.exp(-x))


def _gelu_parts(x):
    c = math.sqrt(2.0 / math.pi)
    inner = c * (x + 0.044715 * x * x * x)
    t = jnp.tanh(inner)
    g = 0.5 * x * (1.0 + t)
    dg = 0.5 * (1.0 + t) + 0.5 * x * (1.0 - t * t) * c * (1.0 + 3.0 * 0.044715 * x * x)
    return g, dg


def _divisors(n, step, cap):
    return [d for d in range(step, min(n, cap) + 1, step) if n % d == 0] or [n]


MM_VMEM_BUDGET = 40 * 1024 * 1024
MM_MAX_ROWS = 1664
MM_MAX_COLS = 1408


def _mm_tiles(M, K, N, a_item, out_item, has_res):
    best = None
    for tn in _divisors(N, LANES, MM_MAX_COLS):
        for tm in _divisors(M, 16, MM_MAX_ROWS):
            need = 2 * (tm * K * a_item + K * tn * 2 + tm * tn * (out_item + (4 if has_res else 0)))
            if need <= MM_VMEM_BUDGET and (best is None or tm * tn > best[0] * best[1]):
                best = (tm, tn)
    assert best is not None, (M, K, N)
    return best


_NT_DIMS = (((1,), (1,)), ((), ()))


def _mm(a, b, *, name, out_dtype=F32, res=None, bt=False):
    M, K = a.shape
    N = b.shape[0] if bt else b.shape[1]
    has_res = res is not None
    tm, tn = _mm_tiles(M, K, N, a.dtype.itemsize, jnp.dtype(out_dtype).itemsize, has_res)

    def body(*refs):
        if has_res:
            a_ref, b_ref, r_ref, o_ref = refs
        else:
            a_ref, b_ref, o_ref = refs
        av, bv = a_ref[...].astype(BF16), b_ref[...].astype(BF16)
        if bt:
            acc = lax.dot_general(av, bv, _NT_DIMS, preferred_element_type=F32)
        else:
            acc = jnp.dot(av, bv, preferred_element_type=F32)
        if has_res:
            acc = acc + r_ref[...].astype(F32)
        o_ref[...] = acc.astype(o_ref.dtype)

    a_bytes = M * K * a.dtype.itemsize
    b_bytes = K * N * b.dtype.itemsize
    rows_outer = a_bytes + (M // tm) * b_bytes <= b_bytes + (N // tn) * a_bytes
    if rows_outer:
        grid = (M // tm, N // tn)
        ia, ib, io = (lambda i, j: (i, 0)), (lambda i, j: (0, j)), (lambda i, j: (i, j))
        ibt = lambda i, j: (j, 0)
    else:
        grid = (N // tn, M // tm)
        ia, ib, io = (lambda j, i: (i, 0)), (lambda j, i: (0, j)), (lambda j, i: (i, j))
        ibt = lambda j, i: (j, 0)
    in_specs = [pl.BlockSpec((tm, K), ia), pl.BlockSpec((tn, K), ibt) if bt else pl.BlockSpec((K, tn), ib)]
    args = [a, b]
    if has_res:
        in_specs.append(pl.BlockSpec((tm, tn), io))
        args.append(res)
    return pl.pallas_call(
        body, name=name, grid=grid, in_specs=in_specs,
        out_specs=pl.BlockSpec((tm, tn), io),
        out_shape=jax.ShapeDtypeStruct((M, N), out_dtype),
        compiler_params=_cparams(("parallel", "parallel")),
    )(*args)


def _mm_take(a, b, take, *, name, out_dtype=F32):
    M, K = a.shape
    N = b.shape[0]
    tm, tn = _mm_tiles(M, K, N, a.dtype.itemsize, jnp.dtype(out_dtype).itemsize, False)
    gm, gn = M // tm, N // tn
    n = len(take)
    heights = [t.shape[1] // 2 for t in take]

    def body(*refs):
        a_ref, b_ref = refs[:2]
        t_src = refs[2:2 + n]
        o_ref = refs[2 + n]
        t_out = refs[3 + n:3 + 2 * n]
        t_scr = refs[3 + 2 * n:]
        i, j = pl.program_id(0), pl.program_id(1)

        @pl.when((i == 0) & (j == 0))
        def _():
            for cp in _take_descs(t_src, t_out, t_scr, heights):
                cp.start()

        o_ref[...] = lax.dot_general(a_ref[...].astype(BF16), b_ref[...].astype(BF16), _NT_DIMS,
                                     preferred_element_type=F32).astype(o_ref.dtype)

        @pl.when((i == gm - 1) & (j == gn - 1))
        def _():
            for cp in _take_descs(t_src, t_out, t_scr, heights):
                cp.wait()

    return pl.pallas_call(
        body, name=name, grid=(gm, gn),
        in_specs=[pl.BlockSpec((tm, K), lambda i, j: (i, 0)), pl.BlockSpec((tn, K), lambda i, j: (j, 0))]
        + [HBM] * n,
        out_specs=[pl.BlockSpec((tm, tn), lambda i, j: (i, j))] + [HBM] * n,
        out_shape=[jax.ShapeDtypeStruct((M, N), out_dtype)]
        + [jax.ShapeDtypeStruct((4, h, t.shape[2]), t.dtype) for t, h in zip(take, heights)],
        scratch_shapes=[_dma_sems(n), _dma_sems(n)],
        compiler_params=_cparams(("arbitrary", "arbitrary")),
    )(a, b, *take)


def _mm_sum(pairs, *, name, scatter=(), out_dtype=F32):
    M = pairs[0][0].shape[0]
    N = pairs[0][1].shape[0]
    ks = [a.shape[1] for a, _ in pairs]
    tm, tn = _mm_tiles(M, sum(ks), N, 2, jnp.dtype(out_dtype).itemsize, False)
    n = len(pairs)
    ns = len(scatter)
    gm, gn = M // tm, N // tn

    def body(*refs):
        s_src = refs[2 * n:2 * n + ns]
        o_ref = refs[2 * n + ns]
        s_out = refs[2 * n + ns + 1:2 * n + 2 * ns + 1]
        s_scr = refs[2 * n + 2 * ns + 1:]
        i, j = pl.program_id(0), pl.program_id(1)
        if ns:
            @pl.when((i == 0) & (j == 0))
            def _():
                for cp in _scatter_descs(s_src, s_out, s_scr):
                    cp.start()

        acc = None
        for k in range(n):
            d = lax.dot_general(refs[2 * k][...].astype(BF16), refs[2 * k + 1][...].astype(BF16), _NT_DIMS,
                                preferred_element_type=F32)
            acc = d if acc is None else acc + d
        o_ref[...] = acc.astype(o_ref.dtype)

        if ns:
            @pl.when((i == gm - 1) & (j == gn - 1))
            def _():
                for cp in _scatter_descs(s_src, s_out, s_scr):
                    cp.wait()

    in_specs, args = [], []
    for (a, b), kk in zip(pairs, ks):
        in_specs += [pl.BlockSpec((tm, kk), lambda i, j: (i, 0)), pl.BlockSpec((tn, kk), lambda i, j: (j, 0))]
        args += [a, b]
    return pl.pallas_call(
        body, name=name, grid=(gm, gn), in_specs=in_specs + [HBM] * ns,
        out_specs=[pl.BlockSpec((tm, tn), lambda i, j: (i, j))] + [HBM] * ns,
        out_shape=[jax.ShapeDtypeStruct((M, N), out_dtype)]
        + [jax.ShapeDtypeStruct((3,) + s.shape[1:], s.dtype) for s in scatter],
        scratch_shapes=[_dma_sems(3 * ns), _dma_sems(3 * ns)] if ns else [],
        compiler_params=_cparams(("arbitrary", "arbitrary")),
    )(*args, *scatter)


def _mm_tn(a, b, *, name):
    T, K1 = a.shape
    N = b.shape[1]
    tt = _divisors(T, 16, MM_MAX_ROWS)[-1]
    tk = _divisors(K1, LANES, MM_MAX_COLS)[-1]
    tn = _divisors(N, LANES, MM_MAX_COLS)[-1]

    def body(a_ref, b_ref, o_ref):
        @pl.when(pl.program_id(2) == 0)
        def _():
            o_ref[...] = jnp.zeros_like(o_ref)

        o_ref[...] += lax.dot_general(a_ref[...].astype(BF16), b_ref[...].astype(BF16),
                                      (((0,), (0,)), ((), ())), preferred_element_type=F32)

    return pl.pallas_call(
        body, name=name, grid=(K1 // tk, N // tn, T // tt),
        in_specs=[pl.BlockSpec((tt, tk), lambda i, j, t: (t, i)),
                  pl.BlockSpec((tt, tn), lambda i, j, t: (t, j))],
        out_specs=pl.BlockSpec((tk, tn), lambda i, j, t: (i, j)),
        out_shape=jax.ShapeDtypeStruct((K1, N), F32),
        compiler_params=_cparams(("parallel", "parallel", "arbitrary")),
    )(a, b)


def _rows(tm, w, cb=0):
    return pl.BlockSpec((tm, w), lambda i: (i, cb))


def _const(shape):
    n = len(shape)
    return pl.BlockSpec(shape, lambda i: (0,) * n)


def _rmsnorm_fwd(x, g, *, name, tm=640):
    T, C = x.shape

    def body(x_ref, g_ref, o_ref):
        xv = x_ref[...]
        r = lax.rsqrt(jnp.mean(xv * xv, axis=-1, keepdims=True) + EPS)
        o_ref[...] = ((xv * r) * g_ref[...]).astype(BF16)

    return pl.pallas_call(
        body, name=name, grid=(T // tm,),
        in_specs=[_rows(tm, C), _const((1, C))],
        out_specs=_rows(tm, C),
        out_shape=jax.ShapeDtypeStruct((T, C), BF16),
        compiler_params=_cparams(("parallel",)),
    )(x, g)


def _rmsnorm_bwd(x, g, dy, res, *, name, tm=640, want_f32=True, want_bf16=True):
    T, C = x.shape
    has_res = res is not None

    def body(*refs):
        refs = list(refs)
        x_ref, g_ref, dy_ref = refs[:3]
        refs = refs[3:]
        r_ref = refs.pop(0) if has_res else None
        o32 = refs.pop(0) if want_f32 else None
        o16 = refs.pop(0) if want_bf16 else None
        dg_ref = refs.pop(0)

        @pl.when(pl.program_id(0) == 0)
        def _():
            dg_ref[...] = jnp.zeros_like(dg_ref)

        xv = x_ref[...]
        dyv = dy_ref[...].astype(F32)
        r = lax.rsqrt(jnp.mean(xv * xv, axis=-1, keepdims=True) + EPS)
        xn = xv * r
        dg_ref[...] += jnp.sum(dyv * xn, axis=0, keepdims=True)
        dxn = dyv * g_ref[...]
        dx = r * (dxn - xn * jnp.mean(dxn * xn, axis=-1, keepdims=True))
        if has_res:
            dx = dx + r_ref[...]
        if want_f32:
            o32[...] = dx
        if want_bf16:
            o16[...] = dx.astype(BF16)

    in_specs = [_rows(tm, C), _const((1, C)), _rows(tm, C)]
    args = [x, g, dy]
    if has_res:
        in_specs.append(_rows(tm, C))
        args.append(res)
    out_specs, out_shape = [], []
    if want_f32:
        out_specs.append(_rows(tm, C))
        out_shape.append(jax.ShapeDtypeStruct((T, C), F32))
    if want_bf16:
        out_specs.append(_rows(tm, C))
        out_shape.append(jax.ShapeDtypeStruct((T, C), BF16))
    out_specs.append(_const((1, C)))
    out_shape.append(jax.ShapeDtypeStruct((1, C), F32))
    return pl.pallas_call(
        body, name=name, grid=(T // tm,), in_specs=in_specs, out_specs=out_specs,
        out_shape=out_shape, compiler_params=_cparams(("arbitrary",)),
    )(*args)


def _gate_mix_fwd(um, bg, p_rnn, p_att, *, tm=320):
    T = um.shape[0]

    def body(um_ref, bg_ref, pr_ref, pa_ref, o_ref):
        g = _sigmoid(um_ref[...].astype(F32) + bg_ref[...])
        o_ref[...] = (g[:, :D] * pr_ref[...].astype(F32) + g[:, D:] * pa_ref[...].astype(F32)).astype(BF16)

    return pl.pallas_call(
        body, name="gate_mix_fwd", grid=(T // tm,),
        in_specs=[_rows(tm, 2 * D), _const((1, 2 * D)), _rows(tm, D), _rows(tm, D)],
        out_specs=_rows(tm, D),
        out_shape=jax.ShapeDtypeStruct((T, D), BF16),
        compiler_params=_cparams(("parallel",)),
    )(um, bg, p_rnn, p_att)


def _gate_mix_bwd(um, bg, p_rnn, p_att, dmixed, *, tm=320):
    T = um.shape[0]

    def body(um_ref, bg_ref, pr_ref, pa_ref, dm_ref, dpr_ref, dpa_ref, dum_ref, dbg_ref):
        @pl.when(pl.program_id(0) == 0)
        def _():
            dbg_ref[...] = jnp.zeros_like(dbg_ref)

        g = _sigmoid(um_ref[...].astype(F32) + bg_ref[...])
        g0, g1 = g[:, :D], g[:, D:]
        dm = dm_ref[...].astype(F32)
        dpr_ref[...] = (dm * g0).astype(BF16)
        dpa_ref[...] = (dm * g1).astype(BF16)
        d0 = dm * pr_ref[...].astype(F32) * g0 * (1.0 - g0)
        d1 = dm * pa_ref[...].astype(F32) * g1 * (1.0 - g1)
        dum_ref[:, :D] = d0.astype(BF16)
        dum_ref[:, D:] = d1.astype(BF16)
        dbg_ref[:, :D] += jnp.sum(d0, axis=0, keepdims=True)
        dbg_ref[:, D:] += jnp.sum(d1, axis=0, keepdims=True)

    return pl.pallas_call(
        body, name="gate_mix_bwd", grid=(T // tm,),
        in_specs=[_rows(tm, 2 * D), _const((1, 2 * D)), _rows(tm, D), _rows(tm, D), _rows(tm, D)],
        out_specs=[_rows(tm, D), _rows(tm, D), _rows(tm, 2 * D), _const((1, 2 * D))],
        out_shape=[jax.ShapeDtypeStruct((T, D), BF16), jax.ShapeDtypeStruct((T, D), BF16),
                   jax.ShapeDtypeStruct((T, 2 * D), BF16), jax.ShapeDtypeStruct((1, 2 * D), F32)],
        compiler_params=_cparams(("arbitrary",)),
    )(um, bg, p_rnn, p_att, dmixed)


def _swiglu_fwd(ff, *, tm=320):
    T = ff.shape[0]

    def body(g_ref, u_ref, o_ref):
        gv = g_ref[...].astype(F32)
        o_ref[...] = (gv * _sigmoid(gv) * u_ref[...].astype(F32)).astype(BF16)

    return pl.pallas_call(
        body, name="swiglu_fwd", grid=(T // tm,),
        in_specs=[_rows(tm, DFF, 0), _rows(tm, DFF, 1)],
        out_specs=_rows(tm, DFF),
        out_shape=jax.ShapeDtypeStruct((T, DFF), BF16),
        compiler_params=_cparams(("parallel",)),
    )(ff, ff)


def _swiglu_bwd(ff, dact, *, tm=320):
    T = ff.shape[0]

    def body(g_ref, u_ref, da_ref, o_ref):
        gv = g_ref[...].astype(F32)
        s = _sigmoid(gv)
        da = da_ref[...].astype(F32)
        o_ref[:, :DFF] = (da * u_ref[...].astype(F32) * s * (1.0 + gv * (1.0 - s))).astype(BF16)
        o_ref[:, DFF:] = (da * gv * s).astype(BF16)

    return pl.pallas_call(
        body, name="swiglu_bwd", grid=(T // tm,),
        in_specs=[_rows(tm, DFF, 0), _rows(tm, DFF, 1), _rows(tm, DFF)],
        out_specs=_rows(tm, 2 * DFF),
        out_shape=jax.ShapeDtypeStruct((T, 2 * DFF), BF16),
        compiler_params=_cparams(("parallel",)),
    )(ff, ff, dact)


def _loss_head(h2, tgt, g, *, tm=640):
    T = h2.shape[0]
    nsub = tm // FRONT

    def body(h_ref, *refs):
        t_refs = refs[:nsub]
        g_ref, d32_ref, d16_ref, dg_ref, ls_ref = refs[nsub:]
        i = pl.program_id(0)

        @pl.when(i == 0)
        def _():
            dg_ref[...] = jnp.zeros_like(dg_ref)
            ls_ref[...] = jnp.zeros_like(ls_ref)

        gv = g_ref[...]
        for k in range(nsub):
            rows = pl.ds(k * FRONT, FRONT)
            xv = h_ref[rows, :]
            r = lax.rsqrt(jnp.mean(xv * xv, axis=-1, keepdims=True) + EPS)
            xn = xv * r
            e = jnp.where(i * nsub + k >= 1, xn * gv - t_refs[k][...], 0.0)
            ls_ref[...] += jnp.sum(e * e, axis=0, keepdims=True)
            dy = e * (1.0 / D)
            dg_ref[...] += jnp.sum(dy * xn, axis=0, keepdims=True)
            dxn = dy * gv
            dx = r * (dxn - xn * jnp.mean(dxn * xn, axis=-1, keepdims=True))
            d32_ref[rows, :] = dx
            d16_ref[rows, :] = dx.astype(BF16)

    def t_spec(k):
        return pl.BlockSpec((FRONT, D), lambda i: (jnp.maximum(i * nsub + k - 1, 0), 0))

    return pl.pallas_call(
        body, name="loss_head", grid=(T // tm,),
        in_specs=[_rows(tm, D)] + [t_spec(k) for k in range(nsub)] + [_const((1, D))],
        out_specs=[_rows(tm, D), _rows(tm, D), _const((1, D)), _const((1, D))],
        out_shape=[jax.ShapeDtypeStruct((T, D), F32), jax.ShapeDtypeStruct((T, D), BF16),
                   jax.ShapeDtypeStruct((1, D), F32), jax.ShapeDtypeStruct((1, D), F32)],
        compiler_params=_cparams(("arbitrary",)),
    )(h2, *([tgt] * nsub), g)


def _scan_fwd(a, b, h_in):
    n = a.shape[0]
    row = lax.broadcasted_iota(jnp.int32, a.shape, 0)
    s = 1
    while s < n:
        if s % 8:
            a_sh = jnp.where(row >= s, pltpu.roll(a, s, 0), 1.0)
            b_sh = jnp.where(row >= s, pltpu.roll(b, s, 0), 0.0)
        else:
            a_sh = jnp.concatenate([jnp.ones((s, RB), F32), a[:n - s]], axis=0)
            b_sh = jnp.concatenate([jnp.zeros((s, RB), F32), b[:n - s]], axis=0)
        b = a * b_sh + b
        a = a * a_sh
        s *= 2
    return b + a * h_in


def _scan_rev(a, b, g_in):
    n = a.shape[0]
    row = lax.broadcasted_iota(jnp.int32, a.shape, 0)
    s = 1
    while s < n:
        if s % 8:
            a_sh = jnp.where(row < n - s, pltpu.roll(a, n - s, 0), 1.0)
            b_sh = jnp.where(row < n - s, pltpu.roll(b, n - s, 0), 0.0)
        else:
            a_sh = jnp.concatenate([a[s:], jnp.ones((s, RB), F32)], axis=0)
            b_sh = jnp.concatenate([b[s:], jnp.zeros((s, RB), F32)], axis=0)
        b = a * b_sh + b
        a = a * a_sh
        s *= 2
    return b + a * g_in


def _lru_gates(xc, wa, ba, wi, bi, lam):
    xcb = xc.astype(BF16)
    r = _sigmoid(jnp.dot(xcb, wa, preferred_element_type=F32) + ba)
    ig = _sigmoid(jnp.dot(xcb, wi, preferred_element_type=F32) + bi)
    log_sig = jnp.minimum(lam, 0.0) - jnp.log(1.0 + jnp.exp(-jnp.abs(lam)))
    log_a = LRU_C * r * log_sig
    a = jnp.exp(log_a)
    m2 = jnp.tanh(-log_a) * (1.0 + a * a)
    return r, ig, log_sig, a, m2 * lax.rsqrt(jnp.maximum(m2, 1e-37))


def _rnn_specs(tc, nblk_t, rev):
    def tmap(k):
        return (nblk_t - 1 - k) if rev else k

    hb = tc // 8
    blk = lambda off: pl.BlockSpec((tc, RB), lambda c, k: (tmap(k), c + off))
    halo = lambda off: pl.BlockSpec((8, RB), lambda c, k: (jnp.maximum(tmap(k) * hb - 1, 0), c + off))
    vec = pl.BlockSpec((1, RB), lambda c, k: (0, c))
    cwv = pl.BlockSpec((CW, RB), lambda c, k: (0, c))
    mat = pl.BlockSpec((None, RB, RB), lambda c, k: (c, 0, 0))
    return blk, halo, vec, cwv, mat


def _rnn_fwd(uxg, cw, cb, wa, ba, wi, bi, lam, *, tc=640):
    T = uxg.shape[0]
    nt = T // tc
    nsub = tc // SUB
    blk, halo, vec, cwv, mat = _rnn_specs(tc, nt, False)

    def body(x_ref, xh_ref, ug_ref, cw_ref, cb_ref, wa_ref, ba_ref, wi_ref, bi_ref, lam_ref,
             h_ref, y_ref, xb, hc):
        k = pl.program_id(1)

        @pl.when(k == 0)
        def _():
            hc[...] = jnp.zeros_like(hc)

        xb[0:8, :] = jnp.where(k > 0, xh_ref[...], 0.0)
        xb[8:, :] = x_ref[...]
        cwv_, cbv = cw_ref[...], cb_ref[...]
        wav, wiv = wa_ref[...], wi_ref[...]
        bav, biv, lamv = ba_ref[...], bi_ref[...], lam_ref[...]
        h_in = hc[0:1, :]
        for sc in range(nsub):
            r0 = sc * SUB
            xc = cbv + cwv_[0:1, :] * xb[pl.ds(5 + r0, SUB), :]
            for j in range(1, CW):
                xc = xc + cwv_[j:j + 1, :] * xb[pl.ds(5 + j + r0, SUB), :]
            r, ig, _, a, mm = _lru_gates(xc, wav, bav, wiv, biv, lamv)
            rows = k * tc + r0 + lax.broadcasted_iota(jnp.int32, (SUB, RB), 0)
            b = jnp.where(rows >= PAD, mm * (ig * xc), 0.0)
            h = _scan_fwd(a, b, h_in)
            h_in = h[SUB - 1:SUB, :]
            h_ref[pl.ds(r0, SUB), :] = h
            gl, _ = _gelu_parts(ug_ref[pl.ds(r0, SUB), :])
            y_ref[pl.ds(r0, SUB), :] = (h * gl).astype(BF16)
        hc[0:1, :] = h_in

    return pl.pallas_call(
        body, name="rnn_fwd", grid=(NBLK, nt),
        in_specs=[blk(0), halo(0), blk(NBLK), cwv, vec, mat, vec, mat, vec, vec],
        out_specs=[blk(0), blk(0)],
        out_shape=[jax.ShapeDtypeStruct((T, DR), F32), jax.ShapeDtypeStruct((T, DR), BF16)],
        scratch_shapes=[pltpu.VMEM((tc + 8, RB), F32), pltpu.VMEM((8, RB), F32)],
        compiler_params=_cparams(("parallel", "arbitrary")),
    )(uxg, uxg, uxg, cw, cb, wa, ba, wi, bi, lam)


def _rnn_bwd(uxg, hs, dy, cw, cb, wa, ba, wi, bi, lam, wat, wit, *, tc=640):
    T = uxg.shape[0]
    nt = T // tc
    nsub = tc // SUB
    blk, halo, vec, cwv, mat = _rnn_specs(tc, nt, True)

    def body(x_ref, xh_ref, ug_ref, h_ref, hh_ref, dy_ref, cw_ref, cb_ref, wa_ref, ba_ref, wi_ref,
             bi_ref, lam_ref, wat_ref, wit_ref,
             dux_ref, dug_ref, dcw_ref, dcb_ref, dwa_ref, dba_ref, dwi_ref, dbi_ref, dlam_ref,
             xb, hb, ab, dxb, xcs, rs, igs, mms, dgas, dgis, carry):
        k = pl.program_id(1)
        kt = nt - 1 - k

        @pl.when(k == 0)
        def _():
            carry[...] = jnp.zeros_like(carry)
            for ref in (dcw_ref, dcb_ref, dwa_ref, dba_ref, dwi_ref, dbi_ref, dlam_ref):
                ref[...] = jnp.zeros_like(ref)

        xb[0:8, :] = jnp.where(kt > 0, xh_ref[...], 0.0)
        xb[8:, :] = x_ref[...]
        hb[0:8, :] = jnp.where(kt > 0, hh_ref[...], 0.0)
        hb[8:, :] = h_ref[...]
        cwv_, cbv = cw_ref[...], cb_ref[...]
        wav, wiv = wa_ref[...], wi_ref[...]
        bav, biv, lamv = ba_ref[...], bi_ref[...], lam_ref[...]
        ab[tc:tc + 8, :] = jnp.broadcast_to(carry[1:2, :], (8, RB))
        dxb[tc:tc + 8, :] = carry[8:16, :]
        log_sig = None
        for sc in range(nsub):
            r0 = sc * SUB
            xc = cbv + cwv_[0:1, :] * xb[pl.ds(5 + r0, SUB), :]
            for j in range(1, CW):
                xc = xc + cwv_[j:j + 1, :] * xb[pl.ds(5 + j + r0, SUB), :]
            r, ig, log_sig, a, mm = _lru_gates(xc, wav, bav, wiv, biv, lamv)
            xcs[pl.ds(r0, SUB), :] = xc
            rs[pl.ds(r0, SUB), :] = r
            igs[pl.ds(r0, SUB), :] = ig
            mms[pl.ds(r0, SUB), :] = mm
            ab[pl.ds(r0, SUB), :] = a
        sig_neg = _sigmoid(-lamv)
        g_in = carry[0:1, :]
        dlam_acc = jnp.zeros((1, RB), F32)
        for sc in reversed(range(nsub)):
            r0 = sc * SUB
            xc, r, ig, mm = xcs[pl.ds(r0, SUB), :], rs[pl.ds(r0, SUB), :], igs[pl.ds(r0, SUB), :], mms[pl.ds(r0, SUB), :]
            a = ab[pl.ds(r0, SUB), :]
            a_next = ab[pl.ds(r0 + 1, SUB), :]
            hv = hb[pl.ds(8 + r0, SUB), :]
            hprev = hb[pl.ds(7 + r0, SUB), :]
            dyv = dy_ref[pl.ds(r0, SUB), :]
            gl, dgl = _gelu_parts(ug_ref[pl.ds(r0, SUB), :])
            dug_ref[pl.ds(r0, SUB), :] = (dyv * hv * dgl).astype(BF16)
            G = _scan_rev(a_next, dyv * gl, g_in)
            g_in = G[0:1, :]
            rows = kt * tc + r0 + lax.broadcasted_iota(jnp.int32, (SUB, RB), 0)
            db = jnp.where(rows >= PAD, G, 0.0)
            da = G * hprev
            dmm = db * (ig * xc)
            di = db * (mm * xc)
            dxc = db * (mm * ig)
            dlog_a = da * a - dmm * (a * a) / jnp.maximum(mm, 1e-30)
            dr = dlog_a * (LRU_C * log_sig)
            dlam_acc = dlam_acc + jnp.sum(dlog_a * (LRU_C * r), axis=0, keepdims=True)
            dga = dr * r * (1.0 - r)
            dgi = di * ig * (1.0 - ig)
            dgab, dgib = dga.astype(BF16), dgi.astype(BF16)
            dgas[pl.ds(r0, SUB), :] = dgab
            dgis[pl.ds(r0, SUB), :] = dgib
            dba_ref[...] += jnp.sum(dga, axis=0, keepdims=True)
            dbi_ref[...] += jnp.sum(dgi, axis=0, keepdims=True)
            dxc = dxc + jnp.dot(dgab, wat_ref[...], preferred_element_type=F32) \
                + jnp.dot(dgib, wit_ref[...], preferred_element_type=F32)
            dxb[pl.ds(r0, SUB), :] = dxc
        dlam_ref[...] += dlam_acc * sig_neg
        xcb = xcs[...].astype(BF16)
        tn = (((0,), (0,)), ((), ()))
        dwa_ref[...] += lax.dot_general(xcb, dgas[...], tn, preferred_element_type=F32)
        dwi_ref[...] += lax.dot_general(xcb, dgis[...], tn, preferred_element_type=F32)
        dxc_all = dxb[0:tc, :]
        dcb_ref[...] += jnp.sum(dxc_all, axis=0, keepdims=True)
        rows_all = kt * tc + lax.broadcasted_iota(jnp.int32, (tc, RB), 0)
        dux = jnp.zeros((tc, RB), F32)
        for j in range(CW):
            dcw_ref[j:j + 1, :] += jnp.sum(dxc_all * xb[pl.ds(5 + j, tc), :], axis=0, keepdims=True)
            dux = dux + cwv_[j:j + 1, :] * dxb[pl.ds(CW - 1 - j, tc), :]
        dux_ref[...] = jnp.where(rows_all >= PAD, dux, 0.0).astype(BF16)
        carry[0:1, :] = g_in
        carry[1:2, :] = ab[0:1, :]
        carry[8:16, :] = dxb[0:8, :]

    vec_out = pl.BlockSpec((1, RB), lambda c, k: (0, c))
    return pl.pallas_call(
        body, name="rnn_bwd", grid=(NBLK, nt),
        in_specs=[blk(0), halo(0), blk(NBLK), blk(0), halo(0), blk(0), cwv, vec, mat, vec, mat, vec, vec, mat, mat],
        out_specs=[blk(0), blk(0), cwv, vec_out, mat, vec_out, mat, vec_out, vec_out],
        out_shape=[jax.ShapeDtypeStruct((T, DR), BF16), jax.ShapeDtypeStruct((T, DR), BF16),
                   jax.ShapeDtypeStruct((CW, DR), F32), jax.ShapeDtypeStruct((1, DR), F32),
                   jax.ShapeDtypeStruct((NBLK, RB, RB), F32), jax.ShapeDtypeStruct((1, DR), F32),
                   jax.ShapeDtypeStruct((NBLK, RB, RB), F32), jax.ShapeDtypeStruct((1, DR), F32),
                   jax.ShapeDtypeStruct((1, DR), F32)],
        scratch_shapes=[pltpu.VMEM((tc + 8, RB), F32), pltpu.VMEM((tc + 8, RB), F32),
                        pltpu.VMEM((tc + 8, RB), F32), pltpu.VMEM((tc + 8, RB), F32),
                        pltpu.VMEM((tc, RB), F32), pltpu.VMEM((tc, RB), F32), pltpu.VMEM((tc, RB), F32),
                        pltpu.VMEM((tc, RB), F32), pltpu.VMEM((tc, RB), BF16), pltpu.VMEM((tc, RB), BF16),
                        pltpu.VMEM((16, RB), F32)],
        compiler_params=_cparams(("parallel", "arbitrary")),
    )(uxg, uxg, uxg, hs, hs, dy, cw, cb, wa, ba, wi, bi, lam, wat, wit)


def _attn_prep(q_all, kv_all, ukr, tab, *, tm=320):
    T = q_all.shape[0]

    def body(q_ref, kv_ref, kr_ref, tab_ref, qo_ref, ko_ref, vo_ref):
        tab_v = tab_ref[...]
        lane = lax.broadcasted_iota(jnp.int32, (tm, LANES), 1)
        t1 = kr_ref[...] * tab_v
        kro = jnp.where(lane < ROPE, t1 + pltpu.roll(t1, ROPE, 1), 0.0).astype(BF16)
        for h in range(NH):
            c0 = h * QW
            qo_ref[h, :, 0:NOPE] = (q_ref[:, c0:c0 + NOPE].astype(F32) * SCALE).astype(BF16)
            t2 = q_ref[:, c0 + NOPE:c0 + QW].astype(F32) * tab_v
            qo_ref[h, :, NOPE:QW] = ((t2 + pltpu.roll(t2, ROPE, 1)) * SCALE).astype(BF16)
            ko_ref[h, :, 0:NOPE] = kv_ref[:, c0:c0 + NOPE].astype(BF16)
            ko_ref[h, :, NOPE:QW] = kro
            vo_ref[h, :, :] = kv_ref[:, c0 + NOPE:c0 + QW].astype(BF16)

    return pl.pallas_call(
        body, name="attn_prep", grid=(T // tm,),
        in_specs=[_rows(tm, NH * QW), _rows(tm, NH * QW), _rows(tm, LANES), _rows(tm, LANES)],
        out_specs=[pl.BlockSpec((NH, tm, QW), lambda i: (0, i, 0)), pl.BlockSpec((NH, tm, QW), lambda i: (0, i, 0)),
                   pl.BlockSpec((NH, tm, VD), lambda i: (0, i, 0))],
        out_shape=[jax.ShapeDtypeStruct((NH, T, QW), BF16), jax.ShapeDtypeStruct((NH, T, QW), BF16),
                   jax.ShapeDtypeStruct((NH, T, VD), BF16)],
        compiler_params=_cparams(("parallel",)),
    )(q_all, kv_all, ukr, tab)


def _attn_prep_bwd(dq, dk, dv, tab, *, tm=320):
    T = dq.shape[1]

    def body(dq_ref, dk_ref, dv_ref, tab_ref, dqa_ref, dkva_ref, dkr_ref):
        tab_v = tab_ref[...]
        lane = lax.broadcasted_iota(jnp.int32, (tm, LANES), 1)
        dkro = jnp.zeros((tm, LANES), F32)
        for h in range(NH):
            c0 = h * QW
            dqa_ref[:, c0:c0 + NOPE] = (dq_ref[h, :, 0:NOPE] * SCALE).astype(BF16)
            d2 = dq_ref[h, :, NOPE:QW]
            dqa_ref[:, c0 + NOPE:c0 + QW] = ((d2 + pltpu.roll(d2, ROPE, 1)) * tab_v * SCALE).astype(BF16)
            dkva_ref[:, c0:c0 + NOPE] = dk_ref[h, :, 0:NOPE].astype(BF16)
            dkva_ref[:, c0 + NOPE:c0 + QW] = dv_ref[h, :, :].astype(BF16)
            dkro = dkro + dk_ref[h, :, NOPE:QW].astype(F32)
        dkro = jnp.where(lane < ROPE, dkro, 0.0)
        dkr_ref[...] = ((dkro + pltpu.roll(dkro, ROPE, 1)) * tab_v).astype(BF16)

    return pl.pallas_call(
        body, name="attn_prep_bwd", grid=(T // tm,),
        in_specs=[pl.BlockSpec((NH, tm, QW), lambda i: (0, i, 0)), pl.BlockSpec((NH, tm, QW), lambda i: (0, i, 0)),
                  pl.BlockSpec((NH, tm, VD), lambda i: (0, i, 0)), _rows(tm, LANES)],
        out_specs=[_rows(tm, NH * QW), _rows(tm, NH * QW), _rows(tm, LANES)],
        out_shape=[jax.ShapeDtypeStruct((T, NH * QW), BF16), jax.ShapeDtypeStruct((T, NH * QW), BF16),
                   jax.ShapeDtypeStruct((T, LANES), BF16)],
        compiler_params=_cparams(("parallel",)),
    )(dq, dk, dv, tab)


def _visible(q0, k0, nq, nk):
    rows = q0 + lax.broadcasted_iota(jnp.int32, (nq, nk), 0)
    cols = k0 + lax.broadcasted_iota(jnp.int32, (nq, nk), 1)
    return ((cols >> 6) <= (rows >> 6)) & (cols >= PAD)


def _visible_t(q0, k0, nq, nk):
    cols = k0 + lax.broadcasted_iota(jnp.int32, (nk, nq), 0)
    rows = q0 + lax.broadcasted_iota(jnp.int32, (nk, nq), 1)
    return ((cols >> 6) <= (rows >> 6)) & (cols >= PAD)


_NT = (((1,), (1,)), ((), ()))
ATTN_BLOCK = 1664


def _attn_block(T):
    return ATTN_BLOCK if T % ATTN_BLOCK == 0 else 640


def _round_up(n, m):
    return -(-n // m) * m


def _flash_fwd(q, k, v, *, gather=(), bq=None):
    T = q.shape[1]
    bq = bq or _attn_block(T)
    nq = T // bq
    rs = bq // CHAINS
    n = len(gather)

    def body(*refs):
        q_ref, k_ref, v_ref = refs[:3]
        g_src = refs[3:3 + n]
        o_ref, lse_ref = refs[3 + n:5 + n]
        g_out = refs[5 + n:5 + 2 * n]
        scr = refs[5 + 2 * n:]
        m_s, l_s, acc_s = scr[:CHAINS], scr[CHAINS:2 * CHAINS], scr[2 * CHAINS:3 * CHAINS]
        g_scr = scr[3 * CHAINS:]
        h = pl.program_id(0)
        i = pl.program_id(1)
        if n:
            @pl.when((h == 0) & (i == 0))
            def _():
                _gather_start(_gather_descs(g_src, g_out, g_scr))

        for r in range(CHAINS):
            m_s[r][...] = jnp.full_like(m_s[r], NEG)
            l_s[r][...] = jnp.zeros_like(l_s[r])
            acc_s[r][...] = jnp.zeros_like(acc_s[r])

        def step(j, masked, diag):
            off = pl.multiple_of(j * bq, bq)
            for r in range(CHAINS):
                rows = pl.ds(r * rs, rs)
                kw = min(bq, _round_up((r + 1) * rs, LANES)) if diag else bq
                kv_ = k_ref[pl.ds(off, kw), :]
                vv = v_ref[pl.ds(off, kw), :]
                s = lax.dot_general(q_ref[rows, :], kv_, _NT, preferred_element_type=F32)
                if masked:
                    s = jnp.where(_visible(i * bq + r * rs, j * bq, rs, kw), s, NEG)
                m_prev = m_s[r][...]
                m_new = jnp.maximum(m_prev, jnp.max(s, axis=-1, keepdims=True))
                p = jnp.exp(s - m_new)
                alpha = jnp.exp(m_prev - m_new)
                l_s[r][...] = alpha * l_s[r][...] + jnp.sum(p, axis=-1, keepdims=True)
                acc_s[r][...] = alpha * acc_s[r][...] + jnp.dot(p.astype(BF16), vv, preferred_element_type=F32)
                m_s[r][...] = m_new

        @pl.when(i == 0)
        def _():
            step(0, True, True)

        @pl.when(i > 0)
        def _():
            step(0, True, False)

            def loop(j, c):
                step(j, False, False)
                return c

            lax.fori_loop(1, i, loop, 0)
            step(i, True, True)

        for r in range(CHAINS):
            rows = pl.ds(r * rs, rs)
            o_ref[rows, :] = (acc_s[r][...] / l_s[r][...]).astype(BF16)
            lse_ref[rows, :] = m_s[r][...] + jnp.log(l_s[r][...])

        if n:
            @pl.when((h == NH - 1) & (i == nq - 1))
            def _():
                _gather_wait(_gather_descs(g_src, g_out, g_scr, with_loads=False))

    return pl.pallas_call(
        body, name="flash_fwd", grid=(NH, nq),
        in_specs=[pl.BlockSpec((None, bq, QW), lambda h, i: (h, i, 0)),
                  pl.BlockSpec((None, T, QW), lambda h, i: (h, 0, 0)),
                  pl.BlockSpec((None, T, VD), lambda h, i: (h, 0, 0))] + [HBM] * n,
        out_specs=[pl.BlockSpec((bq, VD), lambda h, i: (i, h)),
                   pl.BlockSpec((None, bq, 1), lambda h, i: (h, i, 0))] + [HBM] * n,
        out_shape=[jax.ShapeDtypeStruct((T, NH * VD), BF16), jax.ShapeDtypeStruct((NH, T, 1), F32)]
        + [jax.ShapeDtypeStruct((4,) + g.shape, g.dtype) for g in gather],
        scratch_shapes=[pltpu.VMEM((rs, 1), F32)] * (2 * CHAINS) + [pltpu.VMEM((rs, VD), F32)] * CHAINS
        + (_gather_scratch(gather) if n else []),
        compiler_params=_cparams(("arbitrary", "arbitrary")),
    )(q, k, v, *gather)


def _attn_delta(o, do, *, tm=640):
    T = o.shape[0]

    def body(o_ref, do_ref, d_ref):
        prod = o_ref[...].astype(F32) * do_ref[...].astype(F32)
        for h in range(NH):
            col = jnp.sum(prod[:, h * VD:(h + 1) * VD], axis=-1, keepdims=True)
            d_ref[h, :, :] = jnp.broadcast_to(col, (tm, LANES)).T[0:1, :]

    return pl.pallas_call(
        body, name="attn_delta", grid=(T // tm,),
        in_specs=[_rows(tm, NH * VD), _rows(tm, NH * VD)],
        out_specs=pl.BlockSpec((NH, 1, tm), lambda i: (0, 0, i)),
        out_shape=jax.ShapeDtypeStruct((NH, 1, T), F32),
        compiler_params=_cparams(("parallel",)),
    )(o, do)


_TN = (((0,), (0,)), ((), ()))


def _flash_bwd(q, k, v, do, lse_row, delta_row, *, scatter=(), bq=None):
    T = q.shape[1]
    bq = bq or _attn_block(T)
    nq = T // bq
    rs = bq // CHAINS
    n = len(scatter)

    def body(*refs):
        q_ref, k_ref, v_ref, do_ref, lse_ref, dl_ref = refs[:6]
        s_src = refs[6:6 + n]
        dq_ref, dk_out, dv_out = refs[6 + n:9 + n]
        s_out = refs[9 + n:9 + 2 * n]
        dk_ref, dv_ref = refs[9 + 2 * n:11 + 2 * n]
        s_scr = refs[11 + 2 * n:]
        h = pl.program_id(0)
        j = pl.program_id(1)
        if n:
            @pl.when((h == 0) & (j == 0))
            def _():
                for cp in _scatter_descs(s_src, s_out, s_scr):
                    cp.start()

        @pl.when(j == 0)
        def _():
            dq_ref[...] = jnp.zeros_like(dq_ref)

        dk_ref[...] = jnp.zeros_like(dk_ref)
        dv_ref[...] = jnp.zeros_like(dv_ref)

        def step(i, masked, diag):
            for r in range(CHAINS):
                rows = pl.ds(r * rs, rs)
                q0 = (r * rs) // LANES * LANES if diag else 0
                qn = bq - q0
                off = pl.multiple_of(i * bq + q0, LANES)
                qv = q_ref[pl.ds(off, qn), :]
                dov = do_ref[pl.ds(off, qn), :]
                lse_v = lse_ref[:, pl.ds(off, qn)]
                dl_v = dl_ref[:, pl.ds(off, qn)]
                st = lax.dot_general(k_ref[rows, :], qv, _NT, preferred_element_type=F32)
                if masked:
                    st = jnp.where(_visible_t(i * bq + q0, j * bq + r * rs, qn, rs), st, NEG)
                pt = jnp.exp(st - lse_v)
                dv_ref[rows, :] += jnp.dot(pt.astype(BF16), dov, preferred_element_type=F32)
                dpt = lax.dot_general(v_ref[rows, :], dov, _NT, preferred_element_type=F32)
                dst = (pt * (dpt - dl_v)).astype(BF16)
                dk_ref[rows, :] += jnp.dot(dst, qv, preferred_element_type=F32)
                dq_ref[pl.ds(off, qn), :] += lax.dot_general(dst, k_ref[rows, :], _TN,
                                                             preferred_element_type=F32)

        step(j, True, True)

        @pl.when(j == 0)
        def _():
            def loop(i, c):
                step(i, True, False)
                return c
            lax.fori_loop(1, nq, loop, 0)

        @pl.when(j > 0)
        def _():
            def loop(i, c):
                step(i, False, False)
                return c
            lax.fori_loop(j + 1, nq, loop, 0)

        dk_out[...] = dk_ref[...].astype(BF16)
        dv_out[...] = dv_ref[...].astype(BF16)

        if n:
            @pl.when((h == NH - 1) & (j == nq - 1))
            def _():
                for cp in _scatter_descs(s_src, s_out, s_scr):
                    cp.wait()

    return pl.pallas_call(
        body, name="flash_bwd", grid=(NH, nq),
        in_specs=[pl.BlockSpec((None, T, QW), lambda h, j: (h, 0, 0)),
                  pl.BlockSpec((None, bq, QW), lambda h, j: (h, j, 0)),
                  pl.BlockSpec((None, bq, VD), lambda h, j: (h, j, 0)),
                  pl.BlockSpec((T, VD), lambda h, j: (0, h)),
                  pl.BlockSpec((None, 1, T), lambda h, j: (h, 0, 0)),
                  pl.BlockSpec((None, 1, T), lambda h, j: (h, 0, 0))] + [HBM] * n,
        out_specs=[pl.BlockSpec((None, T, QW), lambda h, j: (h, 0, 0)),
                   pl.BlockSpec((None, bq, QW), lambda h, j: (h, j, 0)),
                   pl.BlockSpec((None, bq, VD), lambda h, j: (h, j, 0))] + [HBM] * n,
        out_shape=[jax.ShapeDtypeStruct((NH, T, QW), F32), jax.ShapeDtypeStruct((NH, T, QW), BF16),
                   jax.ShapeDtypeStruct((NH, T, VD), BF16)]
        + [jax.ShapeDtypeStruct((3,) + s.shape[1:], s.dtype) for s in scatter],
        scratch_shapes=[pltpu.VMEM((bq, QW), F32), pltpu.VMEM((bq, VD), F32)]
        + ([_dma_sems(3 * n), _dma_sems(3 * n)] if n else []),
        compiler_params=_cparams(("arbitrary", "arbitrary")),
    )(q, k, v, do, lse_row, delta_row, *scatter)


def _rope_table(T):
    pos = (jnp.arange(T, dtype=jnp.int32) - PAD).astype(F32)
    inv_freq = ROPE_THETA ** (-jnp.arange(0, ROPE, 2, dtype=F32) / ROPE)
    ang = pos[:, None] * inv_freq[None, :]
    cos, sin = jnp.cos(ang), jnp.sin(ang)
    return jnp.concatenate([cos, cos, -sin, sin], axis=1)


def _swap_halves(w):
    return jnp.concatenate([w[..., ROPE // 2:], w[..., :ROPE // 2]], axis=-1)


O_UX, O_UG, O_UQ, O_UKV, O_UKR, O_UM = 0, DR, 2 * DR, 2 * DR + QR, 2 * DR + QR + KVR, 2 * DR + QR + KVR + ROPE


def _prep_weights(w):
    b = lambda a: a.astype(BF16)
    w_in = w["w_in"]
    kr = w_in[:, O_UKR:O_UM]
    p = {
        "w_xg": b(w_in[:, :O_UQ]),
        "w_q": b(w_in[:, O_UQ:O_UKV]),
        "w_kv": b(w_in[:, O_UKV:O_UKR]),
        "w_kr": b(jnp.concatenate([kr, _swap_halves(kr)], axis=1)),
        "w_m": b(w_in[:, O_UM:]),
    }
    wq = w["w_uq"].reshape(QR, NH, NOPE + ROPE)
    p["w_uq"] = b(jnp.concatenate([wq, _swap_halves(wq[..., NOPE:])], axis=-1).reshape(QR, NH * QW))
    p["w_ukv"] = b(w["w_ukv"])
    p["w_x"], p["w_g"] = p["w_xg"][:, :DR], p["w_xg"][:, DR:]
    p["wa"] = b(w["w_rec_a"])
    p["wi"] = b(w["w_rec_i"])
    p["wa_t"] = jnp.swapaxes(p["wa"], 1, 2)
    p["wi_t"] = jnp.swapaxes(p["wi"], 1, 2)
    return p


def _prep_late_weights(w):
    b = lambda a: a.astype(BF16)
    p = {"w_br": b(w["w_branch"][:DR]), "w_ba": b(w["w_branch"][DR:]), "w_out": b(w["w_out"]),
         "w_fi": b(w["w_ffn_in"]), "w_fo": b(w["w_ffn_out"])}
    return p


LATE = ("w_branch", "w_out", "w_ffn_in", "w_ffn_out")


def _local_step(x, tgt, w, late=None, reduce_first=None, reduce_second=None):
    S = x.shape[0]
    T = FRONT + S
    p = _prep_weights(w)
    tab = _rope_table(T)
    h0 = jnp.concatenate([jnp.zeros((PAD, D), F32), w["meta_tokens"], x], axis=0)
    row = lambda v: v.reshape(1, -1)

    z = _rmsnorm_fwd(h0, row(w["norm_mix_g"]), name="norm_mix")
    uxg = _mm(z, p["w_xg"], name="mm_uxg")
    uq = _mm(z, p["w_q"], name="mm_uq")
    ukv = _mm(z, p["w_kv"], name="mm_ukv")
    ukr = _mm(z, p["w_kr"], name="mm_ukr")
    um = _mm(z, p["w_m"], name="mm_um", out_dtype=BF16)
    rnn_w = (w["conv_w"], row(w["conv_b"]), p["wa"], row(w["b_rec_a"]), p["wi"], row(w["b_rec_i"]),
             row(w["lru_lambda"]))
    hs, y_rnn = _rnn_fwd(uxg, *rnn_w)
    qn = _rmsnorm_fwd(uq, row(w["q_norm_g"]), name="norm_q")
    kvn = _rmsnorm_fwd(ukv, row(w["kv_norm_g"]), name="norm_kv")
    q_all = _mm(qn, p["w_uq"], name="mm_q", out_dtype=BF16)
    kv_all = _mm(kvn, p["w_ukv"], name="mm_kv", out_dtype=BF16)
    qh, kh, vh = _attn_prep(q_all, kv_all, ukr, tab)
    y_att, lse, *stacks = _flash_fwd(qh, kh, vh, gather=late[0] if late else ())
    if late:
        w = {**w, **late[1](stacks)}
    p.update(_prep_late_weights(w))
    p_rnn = _mm(y_rnn, p["w_br"], name="mm_prnn", out_dtype=BF16)
    p_att = _mm(y_att, p["w_ba"], name="mm_patt", out_dtype=BF16)
    bg = row(w["b_gate"])
    mixed = _gate_mix_fwd(um, bg, p_rnn, p_att)
    h1 = _mm(mixed, p["w_out"], name="mm_out", res=h0)
    zf = _rmsnorm_fwd(h1, row(w["norm_ffn_g"]), name="norm_ffn")
    ff = _mm(zf, p["w_fi"], name="mm_ffn_in", out_dtype=BF16)
    act = _swiglu_fwd(ff)
    h2 = _mm(act, p["w_fo"], name="mm_ffn_out", res=h1)

    g = {}
    dh2, dh2b, dg_fin, lsum = _loss_head(h2, tgt, row(w["final_norm_g"]))
    loss = 0.5 * jnp.sum(lsum) / D
    g["final_norm_g"] = dg_fin.reshape(-1)
    dact = _mm(dh2b, p["w_fo"], name="mm_dact", out_dtype=BF16, bt=True)
    g["w_ffn_out"] = _mm_tn(act, dh2b, name="mm_dw_ffn_out")
    dff = _swiglu_bwd(ff, dact)
    dzf = _mm(dff, p["w_fi"], name="mm_dzf", bt=True)
    g["w_ffn_in"] = _mm_tn(zf, dff, name="mm_dw_ffn_in")
    dh1, dh1b, dg = _rmsnorm_bwd(h1, row(w["norm_ffn_g"]), dzf, dh2, name="norm_ffn_bwd")
    g["norm_ffn_g"] = dg
    dmixed = _mm(dh1b, p["w_out"], name="mm_dmixed", out_dtype=BF16, bt=True)
    g["w_out"] = _mm_tn(mixed, dh1b, name="mm_dw_out")
    dp_rnn, dp_att, dum, dbg = _gate_mix_bwd(um, bg, p_rnn, p_att, dmixed)
    g["b_gate"] = dbg.reshape(2, D)
    g["w_branch"] = jnp.concatenate([_mm_tn(y_rnn, dp_rnn, name="mm_dw_br"),
                                     _mm_tn(y_att, dp_att, name="mm_dw_ba")], axis=0)
    if reduce_first:
        stacks = reduce_first[0]({n: g[n] for n in LATE})
        big = max(range(len(stacks)), key=lambda k: stacks[k].size)
        rest = [k for k in range(len(stacks)) if k != big]
        dy_rnn, their_big = _mm_take(dp_rnn, p["w_br"], [stacks[big]], name="mm_dy_rnn")
        dy_att, *their_rest = _mm_take(dp_att, p["w_ba"], [stacks[k] for k in rest], name="mm_dy_att",
                                       out_dtype=BF16)
        theirs = [None] * len(stacks)
        theirs[big] = their_big
        for k, t in zip(rest, their_rest):
            theirs[k] = t
        first = reduce_first[1](stacks, theirs)
    else:
        dy_rnn, first = _mm(dp_rnn, p["w_br"], name="mm_dy_rnn", bt=True), ()
        dy_att = _mm(dp_att, p["w_ba"], name="mm_dy_att", out_dtype=BF16, bt=True)
    delta = _attn_delta(y_att, dy_att)
    dq, dk, dv, *received = _flash_bwd(qh, kh, vh, dy_att, lse.reshape(NH, 1, T), delta.reshape(NH, 1, T),
                                       scatter=first)
    dq_all, dkv_all, dukr = _attn_prep_bwd(dq, dk, dv, tab)
    dqn = _mm(dq_all, p["w_uq"], name="mm_dqn", bt=True)
    dkvn = _mm(dkv_all, p["w_ukv"], name="mm_dkvn", bt=True)
    dwq = _mm_tn(qn, dq_all, name="mm_dw_uq").reshape(QR, NH, QW)
    dwq_rope = dwq[..., NOPE:NOPE + ROPE] + _swap_halves(dwq[..., NOPE + ROPE:])
    g["w_uq"] = jnp.concatenate([dwq[..., :NOPE], dwq_rope], axis=-1).reshape(QR, NH * (NOPE + ROPE))
    g["w_ukv"] = _mm_tn(kvn, dkv_all, name="mm_dw_ukv")
    duq, dg = _rmsnorm_bwd(uq, row(w["q_norm_g"]), dqn, None, name="norm_q_bwd", want_f32=False)
    g["q_norm_g"] = dg
    dukv, dg = _rmsnorm_bwd(ukv, row(w["kv_norm_g"]), dkvn, None, name="norm_kv_bwd", want_f32=False)
    g["kv_norm_g"] = dg
    (dux, dug, g["conv_w"], g["conv_b"], g["w_rec_a"], g["b_rec_a"], g["w_rec_i"], g["b_rec_i"],
     g["lru_lambda"]) = _rnn_bwd(uxg, hs, dy_rnn, *rnn_w, p["wa_t"], p["wi_t"])
    dwkr = _mm_tn(z, dukr, name="mm_dw_kr")
    g["w_in"] = jnp.concatenate([
        _mm_tn(z, dux, name="mm_dw_x"), _mm_tn(z, dug, name="mm_dw_g"),
        _mm_tn(z, duq, name="mm_dw_q"), _mm_tn(z, dukv, name="mm_dw_kv"),
        dwkr[:, :ROPE] + _swap_halves(dwkr[:, ROPE:]),
        _mm_tn(z, dum, name="mm_dw_m")], axis=1)
    second = reduce_second({n: g[n] for n in ("w_in", "w_uq", "w_ukv")}) if reduce_second else ()
    dz, *received2 = _mm_sum(
        [(dux, p["w_x"]), (dug, p["w_g"]), (duq, p["w_q"]), (dukv, p["w_kv"]), (dukr, p["w_kr"]),
         (dum, p["w_m"])], name="mm_dz", scatter=second)
    dh0, dg = _rmsnorm_bwd(h0, row(w["norm_mix_g"]), dz, dh1, name="norm_mix_bwd", want_bf16=False)
    g["norm_mix_g"] = dg
    g["meta_tokens"] = dh0[PAD:FRONT]
    return loss, dh0[FRONT:], g, (list(first) + list(second), received + received2)


HBM = pl.BlockSpec(memory_space=pltpu.HBM)
CHIP_FLIPS = ((1, 0), (0, 1), (1, 1))


def _place():
    return lax.axis_index("x"), lax.axis_index("y"), lax.axis_index("c")


def _flip(v, f):
    return 1 - v if f else v


def _dma_sems(n):
    return pltpu.SemaphoreType.DMA((n,))


def _gather_scratch(srcs):
    n = len(srcs)
    return [pltpu.VMEM(s.shape, s.dtype) for s in srcs] + [_dma_sems(3 * n), _dma_sems(3 * n), _dma_sems(n),
                                                            _dma_sems(n)]


def _gather_descs(src_refs, out_refs, scr, with_loads=True):
    n = len(src_refs)
    stage = scr[:n]
    send_sems, recv_sems, in_sems, local_sems = scr[n:]
    x, y, c = _place()
    me = 2 * x + y
    loads, sends, local = [], [], []
    for a in range(n):
        if with_loads:
            loads.append(pltpu.make_async_copy(src_refs[a], stage[a], in_sems.at[a]))
        for k, (fx, fy) in enumerate(CHIP_FLIPS):
            sends.append(pltpu.make_async_remote_copy(
                src_ref=stage[a], dst_ref=out_refs[a].at[me], send_sem=send_sems.at[3 * a + k],
                recv_sem=recv_sems.at[3 * a + k], device_id=(_flip(x, fx), _flip(y, fy), c),
                device_id_type=MESH))
        local.append(pltpu.make_async_copy(stage[a], out_refs[a].at[me], local_sems.at[a]))
    return loads, sends, local


def _gather_start(descs):
    loads, sends, local = descs
    for cp in loads:
        cp.start()
    for a, cp in enumerate(loads):
        cp.wait()
        for s in sends[3 * a:3 * a + 3]:
            s.start()
        local[a].start()


def _gather_wait(descs):
    _, sends, local = descs
    for cp in sends + local:
        cp.wait()


def _allgather_chips(srcs, *, name):
    n = len(srcs)

    def body(*refs):
        descs = _gather_descs(refs[:n], refs[n:2 * n], refs[2 * n:])
        _gather_start(descs)
        _gather_wait(descs)

    return pl.pallas_call(
        body, name=name, in_specs=[HBM] * n, out_specs=[HBM] * n,
        out_shape=[jax.ShapeDtypeStruct((4,) + s.shape, s.dtype) for s in srcs],
        scratch_shapes=_gather_scratch(srcs),
        compiler_params=pltpu.CompilerParams(vmem_limit_bytes=VMEM_LIMIT),
    )(*srcs)


def _allgather_chips_split(srcs, split, *, name):
    n = len(srcs)

    def body(*refs):
        src, out, stage = refs[:n], refs[n:2 * n], refs[2 * n:3 * n]
        send_a, recv_a, send_b, recv_b, in_sems, local_sems = refs[3 * n:]
        x, y, c = _place()
        me = 2 * x + y
        loads = [pltpu.make_async_copy(src[a], stage[a], in_sems.at[a]) for a in range(n)]
        for cp in loads:
            cp.start()

        def half(a, core):
            h = srcs[a].shape[0] // 2
            return pl.ds(pl.multiple_of(core * h, 16), h)

        ici, local = [], []
        for a in range(n):
            loads[a].wait()
            for k, (fx, fy) in enumerate(CHIP_FLIPS):
                s_ref, d_ref = stage[a], out[a].at[me]
                if split[a]:
                    s_ref, d_ref = stage[a].at[half(a, c), :], out[a].at[me, half(a, c), :]
                cp = pltpu.make_async_remote_copy(
                    src_ref=s_ref, dst_ref=d_ref, send_sem=send_a.at[3 * a + k], recv_sem=recv_a.at[3 * a + k],
                    device_id=(_flip(x, fx), _flip(y, fy), c), device_id_type=MESH)
                cp.start()
                ici.append(cp)
            cp = pltpu.make_async_copy(stage[a], out[a].at[me], local_sems.at[a])
            cp.start()
            local.append(cp)
        passed = []
        for a in range(n):
            if not split[a]:
                continue
            for k, (fx, fy) in enumerate(CHIP_FLIPS):
                ici[3 * a + k].wait_recv()
                there = 2 * _flip(x, fx) + _flip(y, fy)
                cp = pltpu.make_async_remote_copy(
                    src_ref=out[a].at[there, half(a, c), :], dst_ref=out[a].at[there, half(a, c), :],
                    send_sem=send_b.at[3 * a + k], recv_sem=recv_b.at[3 * a + k],
                    device_id=(x, y, 1 - c), device_id_type=MESH)
                cp.start()
                passed.append(cp)
        for a in range(n):
            for k in range(3):
                ici[3 * a + k].wait_send()
                if not split[a]:
                    ici[3 * a + k].wait_recv()
        for cp in passed + local:
            cp.wait()

    return pl.pallas_call(
        body, name=name, in_specs=[HBM] * n, out_specs=[HBM] * n,
        out_shape=[jax.ShapeDtypeStruct((4,) + s.shape, s.dtype) for s in srcs],
        scratch_shapes=[pltpu.VMEM(s.shape, s.dtype) for s in srcs]
        + [_dma_sems(3 * n), _dma_sems(3 * n), _dma_sems(3 * n), _dma_sems(3 * n), _dma_sems(n), _dma_sems(n)],
        compiler_params=pltpu.CompilerParams(vmem_limit_bytes=VMEM_LIMIT),
    )(*srcs)


def _scatter_descs(src_refs, out_refs, scr):
    send_sems, recv_sems = scr
    x, y, c = _place()
    copies = []
    for a in range(len(src_refs)):
        for k, (fx, fy) in enumerate(CHIP_FLIPS):
            px, py = _flip(x, fx), _flip(y, fy)
            copies.append(pltpu.make_async_remote_copy(
                src_ref=src_refs[a].at[2 * px + py], dst_ref=out_refs[a].at[k],
                send_sem=send_sems.at[3 * a + k], recv_sem=recv_sems.at[3 * a + k],
                device_id=(px, py, c), device_id_type=MESH))
    return copies


def _scatter_chips(srcs, *, name):
    n = len(srcs)

    def body(*refs):
        copies = _scatter_descs(refs[:n], refs[n:2 * n], refs[2 * n:])
        for cp in copies:
            cp.start()
        for cp in copies:
            cp.wait()

    return pl.pallas_call(
        body, name=name, in_specs=[HBM] * n, out_specs=[HBM] * n,
        out_shape=[jax.ShapeDtypeStruct((3,) + s.shape[1:], s.dtype) for s in srcs],
        scratch_shapes=[_dma_sems(3 * n), _dma_sems(3 * n)],
    )(*srcs)


def _take_descs(src_refs, out_refs, scr, heights):
    send_sems, recv_sems = scr
    x, y, c = _place()
    copies = []
    for a, h in enumerate(heights):
        theirs = pl.ds(pl.multiple_of((1 - c) * h, 8), h)
        copies.append(pltpu.make_async_remote_copy(
            src_ref=src_refs[a].at[:, theirs, :], dst_ref=out_refs[a], send_sem=send_sems.at[a],
            recv_sem=recv_sems.at[a], device_id=(x, y, 1 - c), device_id_type=MESH))
    return copies


def _sibling_take(srcs, *, name):
    n = len(srcs)
    heights = [s.shape[1] // 2 for s in srcs]

    def body(*refs):
        copies = _take_descs(refs[:n], refs[n:2 * n], refs[2 * n:], heights)
        for cp in copies:
            cp.start()
        for cp in copies:
            cp.wait()

    return pl.pallas_call(
        body, name=name, in_specs=[HBM] * n, out_specs=[HBM] * n,
        out_shape=[jax.ShapeDtypeStruct((4, s.shape[1] // 2, s.shape[2]), s.dtype) for s in srcs],
        scratch_shapes=[_dma_sems(n), _dma_sems(n)],
    )(*srcs)


def _sibling_swap(srcs, *, name):
    n = len(srcs)

    def body(*refs):
        src_refs, out_refs = refs[:n], refs[n:2 * n]
        send_sems, recv_sems = refs[2 * n:]
        x, y, c = _place()
        copies = []
        for a in range(n):
            cp = pltpu.make_async_remote_copy(
                src_ref=src_refs[a], dst_ref=out_refs[a], send_sem=send_sems.at[a],
                recv_sem=recv_sems.at[a], device_id=(x, y, 1 - c), device_id_type=MESH)
            cp.start()
            copies.append(cp)
        for cp in copies:
            cp.wait()

    return pl.pallas_call(
        body, name=name, in_specs=[HBM] * n, out_specs=[HBM] * n,
        out_shape=[jax.ShapeDtypeStruct(s.shape, s.dtype) for s in srcs],
        scratch_shapes=[_dma_sems(n), _dma_sems(n)],
    )(*srcs)


def _row_tile(rows, cols, n_arrays, step=16):
    budget = 24 * 1024 * 1024 // (2 * 4 * n_arrays * cols)
    best = step
    for t in range(step, rows + 1, step):
        if rows % t == 0 and t <= budget:
            best = t
    assert rows % best == 0, (rows, cols)
    return best


def _add_halves(full, theirs, core, wire, *, name):
    _, h, c = theirs.shape
    tm = _row_tile(h, c, 3)
    nb = h // tm

    def body(core_ref, a_ref, b_ref, o_ref):
        o_ref[...] = (a_ref[...] + b_ref[...]).astype(wire)

    spec = pl.BlockSpec((None, tm, c), lambda s, i, core_ref: (s, i, 0))
    grid_spec = pltpu.PrefetchScalarGridSpec(
        num_scalar_prefetch=1, grid=(4, nb),
        in_specs=[pl.BlockSpec((None, tm, c), lambda s, i, core_ref: (s, core_ref[0] * nb + i, 0)), spec],
        out_specs=spec)
    return pl.pallas_call(
        body, name=name, grid_spec=grid_spec, out_shape=jax.ShapeDtypeStruct(theirs.shape, wire),
        compiler_params=_cparams(("parallel", "parallel")),
    )(core.reshape(1), full, theirs)


def _sum4(own, recv, *, name):
    h, c = own.shape
    tm = _row_tile(h, c, 5)

    def body(o_ref, r_ref, out_ref):
        f = lambda k: r_ref[k].astype(F32)
        out_ref[...] = ((o_ref[...].astype(F32) + f(0)) + f(1)) + f(2)

    return pl.pallas_call(
        body, name=name, grid=(h // tm,),
        in_specs=[_rows(tm, c), pl.BlockSpec((3, tm, c), lambda i: (0, i, 0))],
        out_specs=_rows(tm, c), out_shape=jax.ShapeDtypeStruct((h, c), F32),
        compiler_params=_cparams(("parallel",)),
    )(own, recv)


def _adamw(g, w, m, v, *, name):
    r, c = g.shape
    tm = _row_tile(r, c, 7, step=8)
    c1 = 1.0 / (1.0 - ADAM_B1 ** ADAM_STEP)
    c2 = 1.0 / (1.0 - ADAM_B2 ** ADAM_STEP)

    def body(g_ref, w_ref, m_ref, v_ref, d_ref, nm_ref, nv_ref):
        gv = g_ref[...]
        nm = ADAM_B1 * m_ref[...] + (1.0 - ADAM_B1) * gv
        nv = ADAM_B2 * v_ref[...] + (1.0 - ADAM_B2) * (gv * gv)
        nm_ref[...] = nm
        nv_ref[...] = nv
        d_ref[...] = -ADAM_LR * ((nm * c1) / (jnp.sqrt(nv * c2) + ADAM_EPS) + ADAM_WD * w_ref[...])

    spec = _rows(tm, c)
    shape = jax.ShapeDtypeStruct((r, c), F32)
    return pl.pallas_call(
        body, name=name, grid=(r // tm,), in_specs=[spec] * 4, out_specs=[spec] * 3,
        out_shape=[shape] * 3, compiler_params=_cparams(("parallel",)),
    )(g, w, m, v)


def _adamw_halves(mine, theirs, core, w, m, v, *, name):
    h, c = mine.shape
    tm = _row_tile(h, c, 10, step=8)
    nb = h // tm
    c1 = 1.0 / (1.0 - ADAM_B1 ** ADAM_STEP)
    c2 = 1.0 / (1.0 - ADAM_B2 ** ADAM_STEP)

    def body(core_ref, a_ref, b_ref, w_ref, m_ref, v_ref, g_ref, d_ref, nm_ref, nv_ref):
        gv = jnp.where(pl.program_id(0) // nb == core_ref[0], a_ref[...], b_ref[...])
        nm = ADAM_B1 * m_ref[...] + (1.0 - ADAM_B1) * gv
        nv = ADAM_B2 * v_ref[...] + (1.0 - ADAM_B2) * (gv * gv)
        g_ref[...] = gv
        nm_ref[...] = nm
        nv_ref[...] = nv
        d_ref[...] = -ADAM_LR * ((nm * c1) / (jnp.sqrt(nv * c2) + ADAM_EPS) + ADAM_WD * w_ref[...])

    half = pl.BlockSpec((tm, c), lambda i, core_ref: (i % nb, 0))
    spec = pl.BlockSpec((tm, c), lambda i, core_ref: (i, 0))
    grid_spec = pltpu.PrefetchScalarGridSpec(
        num_scalar_prefetch=1, grid=(2 * nb,), in_specs=[half, half, spec, spec, spec], out_specs=[spec] * 4)
    return pl.pallas_call(
        body, name=name, grid_spec=grid_spec, out_shape=[jax.ShapeDtypeStruct((2 * h, c), F32)] * 4,
        compiler_params=_cparams(("parallel",)),
    )(core.reshape(1), mine, theirs, w, m, v)


BIG = (("w_in", (D, 1328), 1), ("w_uq", (QR, 384), 1), ("w_ukv", (KVR, 512), 1), ("w_branch", (576, D), 0),
       ("w_out", (256, D), 0), ("w_ffn_in", (D, 1408), 1), ("w_ffn_out", (704, D), 0))
SMALL = (("meta_tokens", (NMETA, 256), 1), ("b_gate", (2, 256), 1), ("conv_w", (CW, 320), 1))
REPL = (("norm_mix_g", (D,)), ("conv_b", (DR,)), ("w_rec_a", (NBLK, RB, RB)), ("b_rec_a", (DR,)),
        ("w_rec_i", (NBLK, RB, RB)), ("b_rec_i", (DR,)), ("lru_lambda", (DR,)), ("q_norm_g", (QR,)),
        ("kv_norm_g", (KVR,)), ("norm_ffn_g", (D,)), ("final_norm_g", (D,)))
WEIGHTS = ("meta_tokens", "norm_mix_g", "w_in", "b_gate", "conv_w", "conv_b", "w_rec_a", "b_rec_a", "w_rec_i",
           "b_rec_i", "lru_lambda", "q_norm_g", "w_uq", "kv_norm_g", "w_ukv", "w_branch", "w_out", "norm_ffn_g",
           "w_ffn_in", "w_ffn_out", "final_norm_g")
W = 1024
SMALL_N = sum(math.prod(s) for _, s, _ in SMALL)
SMALL_ROWS = 8
REPL_N = sum(math.prod(s) for _, s in REPL)
QUART_ROWS = 88
assert SMALL_N <= SMALL_ROWS * W and REPL_N <= 4 * QUART_ROWS * W


def _flat_pad(parts, rows):
    v = jnp.concatenate([p.reshape(-1) for p in parts])
    return jnp.pad(v, (0, rows * W - v.shape[0])).reshape(rows, W)


def _shard_stack(full, shard_shape, axis):
    r, cs = shard_shape
    if axis == 0:
        return full.reshape(4, r, cs)
    return jnp.stack([full[:, s * cs:(s + 1) * cs] for s in range(4)])


def _unshard(stack, axis):
    if axis == 0:
        return stack.reshape(4 * stack.shape[1], stack.shape[2])
    return jnp.concatenate([stack[s] for s in range(4)], axis=1)


def _split(flat, table):
    out, off = {}, 0
    for name, shape, *_ in table:
        n = math.prod(shape)
        out[name] = flat[..., off:off + n].reshape(flat.shape[:-1] + tuple(shape))
        off += n
    return out


def kernel(x, meta_tokens, norm_mix_g, w_in, b_gate, conv_w, conv_b, w_rec_a, b_rec_a, w_rec_i, b_rec_i, lru_lambda, q_norm_g, w_uq, kv_norm_g, w_ukv, w_branch, w_out, norm_ffn_g, w_ffn_in, w_ffn_out, final_norm_g, loss_target, m_meta_tokens, m_norm_mix_g, m_w_in, m_b_gate, m_conv_w, m_conv_b, m_w_rec_a, m_b_rec_a, m_w_rec_i, m_b_rec_i, m_lru_lambda, m_q_norm_g, m_w_uq, m_kv_norm_g, m_w_ukv, m_w_branch, m_w_out, m_norm_ffn_g, m_w_ffn_in, m_w_ffn_out, m_final_norm_g, v_meta_tokens, v_norm_mix_g, v_w_in, v_b_gate, v_conv_w, v_conv_b, v_w_rec_a, v_b_rec_a, v_w_rec_i, v_b_rec_i, v_lru_lambda, v_q_norm_g, v_w_uq, v_kv_norm_g, v_w_ukv, v_w_branch, v_w_out, v_norm_ffn_g, v_w_ffn_in, v_w_ffn_out, v_final_norm_g):
    args = dict(locals())
    chip = 2 * lax.axis_index("x") + lax.axis_index("y")
    core = lax.axis_index("c")

    first_big = [b for b in BIG if b[0] not in LATE]
    late_big = [b for b in BIG if b[0] in LATE]
    bf16_shard = lambda n, s: args[n].reshape(s).astype(BF16)
    small = _flat_pad([args[n] for n, _, _ in SMALL], SMALL_ROWS)
    gathered = _allgather_chips_split([bf16_shard(n, s) for n, s, _ in first_big] + [small],
                                      [True] * len(first_big) + [False], name="gather_weights")
    w = {}
    for (name, _, axis), stack in zip(first_big, gathered):
        w[name] = _unshard(stack, axis)
    small_parts = _split(gathered[-1].reshape(4, SMALL_ROWS * W), SMALL)
    for name, _, axis in SMALL:
        w[name] = _unshard(small_parts[name], axis)
    for name, shape in REPL:
        w[name] = args[name].reshape(shape)
    finish_late = lambda stacks: {name: _unshard(st, axis) for (name, _, axis), st in zip(late_big, stacks)}

    def add_theirs(red, theirs, tag, wires):
        return [_add_halves(a, t, core, wires[k], name=f"add_sibling_{tag}{k}")
                for k, (a, t) in enumerate(zip(red, theirs))]

    def to_wire(red, tag, wires):
        return add_theirs(red, _sibling_take(red, name="reduce_sibling_" + tag), tag, wires)

    reduce_first = (lambda gl: [_shard_stack(gl[n], s, a) for n, s, a in late_big],
                    lambda red, theirs: add_theirs(red, theirs, "a", [BF16] * len(late_big)))
    reduce_second = lambda gl: to_wire([_shard_stack(gl[n], s, a) for n, s, a in first_big], "b",
                                       [BF16] * len(first_big))
    loss, grad_x, g, (parts_ab, recv_ab) = _local_step(
        x[0], loss_target[0], w, late=([bf16_shard(n, s) for n, s, _ in late_big], finish_late),
        reduce_first=reduce_first, reduce_second=reduce_second)
    loss = lax.psum(loss, ("x", "y", "c"))

    small_g = jnp.concatenate([_shard_stack(g[n], s, a).reshape(4, -1) for n, s, a in SMALL], axis=1)
    small_g = jnp.pad(small_g, ((0, 0), (0, SMALL_ROWS * W - SMALL_N))).reshape(4, SMALL_ROWS, W)
    repl_g = _flat_pad([g[n] for n, _ in REPL], 4 * QUART_ROWS).reshape(4, QUART_ROWS, W)
    parts_c = to_wire([jnp.concatenate([small_g, repl_g], axis=1)], "c", [F32])
    recv_c = _scatter_chips(parts_c, name="reduce_chips")
    order = [b[0] for b in late_big] + [b[0] for b in first_big] + ["misc"]
    halves = [_sum4(lax.dynamic_index_in_dim(p, chip, 0, keepdims=False), r, name="sum_chips_" + n)
              for n, p, r in zip(order, list(parts_ab) + parts_c, list(recv_ab) + list(recv_c))]
    others = _sibling_swap(halves, name="share_sibling")

    results = {}
    shape_of = {name: shape for name, shape, _ in BIG}
    for name, mine, theirs in zip(order[:-1], halves, others):
        shape = shape_of[name]
        results[name] = _adamw_halves(mine, theirs, core, args[name].reshape(shape),
                                      args["m_" + name].reshape(shape), args["v_" + name].reshape(shape),
                                      name="adamw_" + name)

    a, b = halves[-1], others[-1]
    g_mine = jnp.where(core == 0, jnp.concatenate([a, b], axis=0), jnp.concatenate([b, a], axis=0))
    g_repl = _allgather_chips([g_mine[SMALL_ROWS:]], name="gather_repl")[0].reshape(4 * QUART_ROWS, W)
    g_misc = jnp.concatenate([g_mine[:SMALL_ROWS], g_repl], axis=0)
    misc_state = lambda prefix: jnp.concatenate(
        [_flat_pad([args[prefix + n] for n, _, _ in SMALL], SMALL_ROWS),
         _flat_pad([args[prefix + n] for n, _ in REPL], 4 * QUART_ROWS)], axis=0)
    d, nm, nv = _adamw(g_misc, misc_state(""), misc_state("m_"), misc_state("v_"), name="adamw_misc")
    misc = (g_misc, d, nm, nv)

    outs = []
    for k in range(4):
        sm = _split(misc[k][:SMALL_ROWS].reshape(-1), SMALL)
        rp = _split(misc[k][SMALL_ROWS:].reshape(-1), REPL)
        for name in WEIGHTS:
            val = results[name][k] if name in results else (sm[name] if name in sm else rp[name])
            outs.append(val.reshape(args[name].shape))
    return (loss, grad_x[None], *outs)
```

```python
import functools
import math

import jax
import jax.numpy as jnp
from jax import lax
from jax.experimental import pallas as pl
from jax.experimental.pallas import tpu as pltpu

F32 = jnp.float32
BF16 = jnp.bfloat16

D = 1024
DR = 1280
NBLK = 10
RB = 128
CW = 4
NH = 8
NOPE = 128
ROPE = 64
VD = 128
QR = 384
KVR = 256
DFF = 2816
NMETA = 16
EPS = 1e-6
LRU_C = 8.0
ROPE_THETA = 10000.0
SCALE = 1.0 / math.sqrt(NOPE + ROPE)
NEG = -1e30
FRONT = 128
PAD = FRONT - NMETA
QW = 2 * NOPE
LANES = 128
SUB = 128
CHAINS = 4
VMEM_LIMIT = 52 * 1024 * 1024

ADAM_LR = 0.001
ADAM_B1 = 0.9
ADAM_B2 = 0.999
ADAM_EPS = 1e-08
ADAM_WD = 0.01
ADAM_STEP = 10

MESH = pl.DeviceIdType.MESH


def _cparams(sem):
    return pltpu.CompilerParams(dimension_semantics=sem, vmem_limit_bytes=VMEM_LIMIT)


def _sigmoid(x):
    return 1.0 / (1.0 + jnp.exp(-x))


def _gelu_parts(x):
    c = math.sqrt(2.0 / math.pi)
    inner = c * (x + 0.044715 * x * x * x)
    t = jnp.tanh(inner)
    g = 0.5 * x * (1.0 + t)
    dg = 0.5 * (1.0 + t) + 0.5 * x * (1.0 - t * t) * c * (1.0 + 3.0 * 0.044715 * x * x)
    return g, dg


def _divisors(n, step, cap):
    return [d for d in range(step, min(n, cap) + 1, step) if n % d == 0] or [n]


MM_VMEM_BUDGET = 40 * 1024 * 1024
MM_MAX_ROWS = 1664
MM_MAX_COLS = 1408


def _mm_tiles(M, K, N, a_item, out_item, has_res):
    best = None
    for tn in _divisors(N, LANES, MM_MAX_COLS):
        for tm in _divisors(M, 16, MM_MAX_ROWS):
            need = 2 * (tm * K * a_item + K * tn * 2 + tm * tn * (out_item + (4 if has_res else 0)))
            if need <= MM_VMEM_BUDGET and (best is None or tm * tn > best[0] * best[1]):
                best = (tm, tn)
    assert best is not None, (M, K, N)
    return best


_NT_DIMS = (((1,), (1,)), ((), ()))


def _mm(a, b, *, name, out_dtype=F32, res=None, bt=False):
    M, K = a.shape
    N = b.shape[0] if bt else b.shape[1]
    has_res = res is not None
    tm, tn = _mm_tiles(M, K, N, a.dtype.itemsize, jnp.dtype(out_dtype).itemsize, has_res)

    def body(*refs):
        if has_res:
            a_ref, b_ref, r_ref, o_ref = refs
        else:
            a_ref, b_ref, o_ref = refs
        av, bv = a_ref[...].astype(BF16), b_ref[...].astype(BF16)
        if bt:
            acc = lax.dot_general(av, bv, _NT_DIMS, preferred_element_type=F32)
        else:
            acc = jnp.dot(av, bv, preferred_element_type=F32)
        if has_res:
            acc = acc + r_ref[...].astype(F32)
        o_ref[...] = acc.astype(o_ref.dtype)

    a_bytes = M * K * a.dtype.itemsize
    b_bytes = K * N * b.dtype.itemsize
    rows_outer = a_bytes + (M // tm) * b_bytes <= b_bytes + (N // tn) * a_bytes
    if rows_outer:
        grid = (M // tm, N // tn)
        ia, ib, io = (lambda i, j: (i, 0)), (lambda i, j: (0, j)), (lambda i, j: (i, j))
        ibt = lambda i, j: (j, 0)
    else:
        grid = (N // tn, M // tm)
        ia, ib, io = (lambda j, i: (i, 0)), (lambda j, i: (0, j)), (lambda j, i: (i, j))
        ibt = lambda j, i: (j, 0)
    in_specs = [pl.BlockSpec((tm, K), ia), pl.BlockSpec((tn, K), ibt) if bt else pl.BlockSpec((K, tn), ib)]
    args = [a, b]
    if has_res:
        in_specs.append(pl.BlockSpec((tm, tn), io))
        args.append(res)
    return pl.pallas_call(
        body, name=name, grid=grid, in_specs=in_specs,
        out_specs=pl.BlockSpec((tm, tn), io),
        out_shape=jax.ShapeDtypeStruct((M, N), out_dtype),
        compiler_params=_cparams(("parallel", "parallel")),
    )(*args)


def _mm_take(a, b, take, *, name, out_dtype=F32):
    M, K = a.shape
    N = b.shape[0]
    tm, tn = _mm_tiles(M, K, N, a.dtype.itemsize, jnp.dtype(out_dtype).itemsize, False)
    gm, gn = M // tm, N // tn
    n = len(take)
    heights = [t.shape[1] // 2 for t in take]

    def body(*refs):
        a_ref, b_ref = refs[:2]
        t_src = refs[2:2 + n]
        o_ref = refs[2 + n]
        t_out = refs[3 + n:3 + 2 * n]
        t_scr = refs[3 + 2 * n:]
        i, j = pl.program_id(0), pl.program_id(1)

        @pl.when((i == 0) & (j == 0))
        def _():
            for cp in _take_descs(t_src, t_out, t_scr, heights):
                cp.start()

        o_ref[...] = lax.dot_general(a_ref[...].astype(BF16), b_ref[...].astype(BF16), _NT_DIMS,
                                     preferred_element_type=F32).astype(o_ref.dtype)

        @pl.when((i == gm - 1) & (j == gn - 1))
        def _():
            for cp in _take_descs(t_src, t_out, t_scr, heights):
                cp.wait()

    return pl.pallas_call(
        body, name=name, grid=(gm, gn),
        in_specs=[pl.BlockSpec((tm, K), lambda i, j: (i, 0)), pl.BlockSpec((tn, K), lambda i, j: (j, 0))]
        + [HBM] * n,
        out_specs=[pl.BlockSpec((tm, tn), lambda i, j: (i, j))] + [HBM] * n,
        out_shape=[jax.ShapeDtypeStruct((M, N), out_dtype)]
        + [jax.ShapeDtypeStruct((4, h, t.shape[2]), t.dtype) for t, h in zip(take, heights)],
        scratch_shapes=[_dma_sems(n), _dma_sems(n)],
        compiler_params=_cparams(("arbitrary", "arbitrary")),
    )(a, b, *take)


def _mm_sum(pairs, *, name, scatter=(), out_dtype=F32):
    M = pairs[0][0].shape[0]
    N = pairs[0][1].shape[0]
    ks = [a.shape[1] for a, _ in pairs]
    tm, tn = _mm_tiles(M, sum(ks), N, 2, jnp.dtype(out_dtype).itemsize, False)
    n = len(pairs)
    ns = len(scatter)
    gm, gn = M // tm, N // tn

    def body(*refs):
        s_src = refs[2 * n:2 * n + ns]
        o_ref = refs[2 * n + ns]
        s_out = refs[2 * n + ns + 1:2 * n + 2 * ns + 1]
        s_scr = refs[2 * n + 2 * ns + 1:]
        i, j = pl.program_id(0), pl.program_id(1)
        if ns:
            @pl.when((i == 0) & (j == 0))
            def _():
                for cp in _scatter_descs(s_src, s_out, s_scr):
                    cp.start()

        acc = None
        for k in range(n):
            d = lax.dot_general(refs[2 * k][...].astype(BF16), refs[2 * k + 1][...].astype(BF16), _NT_DIMS,
                                preferred_element_type=F32)
            acc = d if acc is None else acc + d
        o_ref[...] = acc.astype(o_ref.dtype)

        if ns:
            @pl.when((i == gm - 1) & (j == gn - 1))
            def _():
                for cp in _scatter_descs(s_src, s_out, s_scr):
                    cp.wait()

    in_specs, args = [], []
    for (a, b), kk in zip(pairs, ks):
        in_specs += [pl.BlockSpec((tm, kk), lambda i, j: (i, 0)), pl.BlockSpec((tn, kk), lambda i, j: (j, 0))]
        args += [a, b]
    return pl.pallas_call(
        body, name=name, grid=(gm, gn), in_specs=in_specs + [HBM] * ns,
        out_specs=[pl.BlockSpec((tm, tn), lambda i, j: (i, j))] + [HBM] * ns,
        out_shape=[jax.ShapeDtypeStruct((M, N), out_dtype)]
        + [jax.ShapeDtypeStruct((3,) + s.shape[1:], s.dtype) for s in scatter],
        scratch_shapes=[_dma_sems(3 * ns), _dma_sems(3 * ns)] if ns else [],
        compiler_params=_cparams(("arbitrary", "arbitrary")),
    )(*args, *scatter)


def _mm_tn(a, b, *, name):
    T, K1 = a.shape
    N = b.shape[1]
    tt = _divisors(T, 16, MM_MAX_ROWS)[-1]
    tk = _divisors(K1, LANES, MM_MAX_COLS)[-1]
    tn = _divisors(N, LANES, MM_MAX_COLS)[-1]

    def body(a_ref, b_ref, o_ref):
        @pl.when(pl.program_id(2) == 0)
        def _():
            o_ref[...] = jnp.zeros_like(o_ref)

        o_ref[...] += lax.dot_general(a_ref[...].astype(BF16), b_ref[...].astype(BF16),
                                      (((0,), (0,)), ((), ())), preferred_element_type=F32)

    return pl.pallas_call(
        body, name=name, grid=(K1 // tk, N // tn, T // tt),
        in_specs=[pl.BlockSpec((tt, tk), lambda i, j, t: (t, i)),
                  pl.BlockSpec((tt, tn), lambda i, j, t: (t, j))],
        out_specs=pl.BlockSpec((tk, tn), lambda i, j, t: (i, j)),
        out_shape=jax.ShapeDtypeStruct((K1, N), F32),
        compiler_params=_cparams(("parallel", "parallel", "arbitrary")),
    )(a, b)


def _rows(tm, w, cb=0):
    return pl.BlockSpec((tm, w), lambda i: (i, cb))


def _const(shape):
    n = len(shape)
    return pl.BlockSpec(shape, lambda i: (0,) * n)


def _rmsnorm_fwd(x, g, *, name, tm=640):
    T, C = x.shape

    def body(x_ref, g_ref, o_ref):
        xv = x_ref[...]
        r = lax.rsqrt(jnp.mean(xv * xv, axis=-1, keepdims=True) + EPS)
        o_ref[...] = ((xv * r) * g_ref[...]).astype(BF16)

    return pl.pallas_call(
        body, name=name, grid=(T // tm,),
        in_specs=[_rows(tm, C), _const((1, C))],
        out_specs=_rows(tm, C),
        out_shape=jax.ShapeDtypeStruct((T, C), BF16),
        compiler_params=_cparams(("parallel",)),
    )(x, g)


def _rmsnorm_bwd(x, g, dy, res, *, name, tm=640, want_f32=True, want_bf16=True):
    T, C = x.shape
    has_res = res is not None

    def body(*refs):
        refs = list(refs)
        x_ref, g_ref, dy_ref = refs[:3]
        refs = refs[3:]
        r_ref = refs.pop(0) if has_res else None
        o32 = refs.pop(0) if want_f32 else None
        o16 = refs.pop(0) if want_bf16 else None
        dg_ref = refs.pop(0)

        @pl.when(pl.program_id(0) == 0)
        def _():
            dg_ref[...] = jnp.zeros_like(dg_ref)

        xv = x_ref[...]
        dyv = dy_ref[...].astype(F32)
        r = lax.rsqrt(jnp.mean(xv * xv, axis=-1, keepdims=True) + EPS)
        xn = xv * r
        dg_ref[...] += jnp.sum(dyv * xn, axis=0, keepdims=True)
        dxn = dyv * g_ref[...]
        dx = r * (dxn - xn * jnp.mean(dxn * xn, axis=-1, keepdims=True))
        if has_res:
            dx = dx + r_ref[...]
        if want_f32:
            o32[...] = dx
        if want_bf16:
            o16[...] = dx.astype(BF16)

    in_specs = [_rows(tm, C), _const((1, C)), _rows(tm, C)]
    args = [x, g, dy]
    if has_res:
        in_specs.append(_rows(tm, C))
        args.append(res)
    out_specs, out_shape = [], []
    if want_f32:
        out_specs.append(_rows(tm, C))
        out_shape.append(jax.ShapeDtypeStruct((T, C), F32))
    if want_bf16:
        out_specs.append(_rows(tm, C))
        out_shape.append(jax.ShapeDtypeStruct((T, C), BF16))
    out_specs.append(_const((1, C)))
    out_shape.append(jax.ShapeDtypeStruct((1, C), F32))
    return pl.pallas_call(
        body, name=name, grid=(T // tm,), in_specs=in_specs, out_specs=out_specs,
        out_shape=out_shape, compiler_params=_cparams(("arbitrary",)),
    )(*args)


def _gate_mix_fwd(um, bg, p_rnn, p_att, *, tm=320):
    T = um.shape[0]

    def body(um_ref, bg_ref, pr_ref, pa_ref, o_ref):
        g = _sigmoid(um_ref[...].astype(F32) + bg_ref[...])
        o_ref[...] = (g[:, :D] * pr_ref[...].astype(F32) + g[:, D:] * pa_ref[...].astype(F32)).astype(BF16)

    return pl.pallas_call(
        body, name="gate_mix_fwd", grid=(T // tm,),
        in_specs=[_rows(tm, 2 * D), _const((1, 2 * D)), _rows(tm, D), _rows(tm, D)],
        out_specs=_rows(tm, D),
        out_shape=jax.ShapeDtypeStruct((T, D), BF16),
        compiler_params=_cparams(("parallel",)),
    )(um, bg, p_rnn, p_att)


def _gate_mix_bwd(um, bg, p_rnn, p_att, dmixed, *, tm=320):
    T = um.shape[0]

    def body(um_ref, bg_ref, pr_ref, pa_ref, dm_ref, dpr_ref, dpa_ref, dum_ref, dbg_ref):
        @pl.when(pl.program_id(0) == 0)
        def _():
            dbg_ref[...] = jnp.zeros_like(dbg_ref)

        g = _sigmoid(um_ref[...].astype(F32) + bg_ref[...])
        g0, g1 = g[:, :D], g[:, D:]
        dm = dm_ref[...].astype(F32)
        dpr_ref[...] = (dm * g0).astype(BF16)
        dpa_ref[...] = (dm * g1).astype(BF16)
        d0 = dm * pr_ref[...].astype(F32) * g0 * (1.0 - g0)
        d1 = dm * pa_ref[...].astype(F32) * g1 * (1.0 - g1)
        dum_ref[:, :D] = d0.astype(BF16)
        dum_ref[:, D:] = d1.astype(BF16)
        dbg_ref[:, :D] += jnp.sum(d0, axis=0, keepdims=True)
        dbg_ref[:, D:] += jnp.sum(d1, axis=0, keepdims=True)

    return pl.pallas_call(
        body, name="gate_mix_bwd", grid=(T // tm,),
        in_specs=[_rows(tm, 2 * D), _const((1, 2 * D)), _rows(tm, D), _rows(tm, D), _rows(tm, D)],
        out_specs=[_rows(tm, D), _rows(tm, D), _rows(tm, 2 * D), _const((1, 2 * D))],
        out_shape=[jax.ShapeDtypeStruct((T, D), BF16), jax.ShapeDtypeStruct((T, D), BF16),
                   jax.ShapeDtypeStruct((T, 2 * D), BF16), jax.ShapeDtypeStruct((1, 2 * D), F32)],
        compiler_params=_cparams(("arbitrary",)),
    )(um, bg, p_rnn, p_att, dmixed)


def _swiglu_fwd(ff, *, tm=320):
    T = ff.shape[0]

    def body(g_ref, u_ref, o_ref):
        gv = g_ref[...].astype(F32)
        o_ref[...] = (gv * _sigmoid(gv) * u_ref[...].astype(F32)).astype(BF16)

    return pl.pallas_call(
        body, name="swiglu_fwd", grid=(T // tm,),
        in_specs=[_rows(tm, DFF, 0), _rows(tm, DFF, 1)],
        out_specs=_rows(tm, DFF),
        out_shape=jax.ShapeDtypeStruct((T, DFF), BF16),
        compiler_params=_cparams(("parallel",)),
    )(ff, ff)


def _swiglu_bwd(ff, dact, *, tm=320):
    T = ff.shape[0]

    def body(g_ref, u_ref, da_ref, o_ref):
        gv = g_ref[...].astype(F32)
        s = _sigmoid(gv)
        da = da_ref[...].astype(F32)
        o_ref[:, :DFF] = (da * u_ref[...].astype(F32) * s * (1.0 + gv * (1.0 - s))).astype(BF16)
        o_ref[:, DFF:] = (da * gv * s).astype(BF16)

    return pl.pallas_call(
        body, name="swiglu_bwd", grid=(T // tm,),
        in_specs=[_rows(tm, DFF, 0), _rows(tm, DFF, 1), _rows(tm, DFF)],
        out_specs=_rows(tm, 2 * DFF),
        out_shape=jax.ShapeDtypeStruct((T, 2 * DFF), BF16),
        compiler_params=_cparams(("parallel",)),
    )(ff, ff, dact)


def _mix_out(um, bg, p_rnn, p_att, w_out, res, *, tm=320):
    T = um.shape[0]

    def body(um_ref, bg_ref, pr_ref, pa_ref, w_ref, r_ref, o_ref, m_ref):
        g = _sigmoid(um_ref[...].astype(F32) + bg_ref[...])
        mixed = (g[:, :D] * pr_ref[...].astype(F32) + g[:, D:] * pa_ref[...].astype(F32)).astype(BF16)
        m_ref[...] = mixed
        o_ref[...] = jnp.dot(mixed, w_ref[...], preferred_element_type=F32) + r_ref[...]

    return pl.pallas_call(
        body, name="mix_out", grid=(T // tm,),
        in_specs=[_rows(tm, 2 * D), _const((1, 2 * D)), _rows(tm, D), _rows(tm, D), _const((D, D)), _rows(tm, D)],
        out_specs=[_rows(tm, D), _rows(tm, D)],
        out_shape=[jax.ShapeDtypeStruct((T, D), F32), jax.ShapeDtypeStruct((T, D), BF16)],
        compiler_params=_cparams(("parallel",)),
    )(um, bg, p_rnn, p_att, w_out, res)


def _ffn_out(ff, w_fo, res, *, tm=320):
    T = ff.shape[0]

    def body(g_ref, u_ref, w_ref, r_ref, o_ref, a_ref):
        gv = g_ref[...].astype(F32)
        act = (gv * _sigmoid(gv) * u_ref[...].astype(F32)).astype(BF16)
        a_ref[...] = act
        o_ref[...] = jnp.dot(act, w_ref[...], preferred_element_type=F32) + r_ref[...]

    return pl.pallas_call(
        body, name="ffn_out", grid=(T // tm,),
        in_specs=[_rows(tm, DFF, 0), _rows(tm, DFF, 1), _const((DFF, D)), _rows(tm, D)],
        out_specs=[_rows(tm, D), _rows(tm, DFF)],
        out_shape=[jax.ShapeDtypeStruct((T, D), F32), jax.ShapeDtypeStruct((T, DFF), BF16)],
        compiler_params=_cparams(("parallel",)),
    )(ff, ff, w_fo, res)


def _loss_head(h2, tgt, g, *, tm=640):
    T = h2.shape[0]
    nsub = tm // FRONT

    def body(h_ref, *refs):
        t_refs = refs[:nsub]
        g_ref, d32_ref, d16_ref, dg_ref, ls_ref = refs[nsub:]
        i = pl.program_id(0)

        @pl.when(i == 0)
        def _():
            dg_ref[...] = jnp.zeros_like(dg_ref)
            ls_ref[...] = jnp.zeros_like(ls_ref)

        gv = g_ref[...]
        for k in range(nsub):
            rows = pl.ds(k * FRONT, FRONT)
            xv = h_ref[rows, :]
            r = lax.rsqrt(jnp.mean(xv * xv, axis=-1, keepdims=True) + EPS)
            xn = xv * r
            e = jnp.where(i * nsub + k >= 1, xn * gv - t_refs[k][...], 0.0)
            ls_ref[...] += jnp.sum(e * e, axis=0, keepdims=True)
            dy = e * (1.0 / D)
            dg_ref[...] += jnp.sum(dy * xn, axis=0, keepdims=True)
            dxn = dy * gv
            dx = r * (dxn - xn * jnp.mean(dxn * xn, axis=-1, keepdims=True))
            d32_ref[rows, :] = dx
            d16_ref[rows, :] = dx.astype(BF16)

    def t_spec(k):
        return pl.BlockSpec((FRONT, D), lambda i: (jnp.maximum(i * nsub + k - 1, 0), 0))

    return pl.pallas_call(
        body, name="loss_head", grid=(T // tm,),
        in_specs=[_rows(tm, D)] + [t_spec(k) for k in range(nsub)] + [_const((1, D))],
        out_specs=[_rows(tm, D), _rows(tm, D), _const((1, D)), _const((1, D))],
        out_shape=[jax.ShapeDtypeStruct((T, D), F32), jax.ShapeDtypeStruct((T, D), BF16),
                   jax.ShapeDtypeStruct((1, D), F32), jax.ShapeDtypeStruct((1, D), F32)],
        compiler_params=_cparams(("arbitrary",)),
    )(h2, *([tgt] * nsub), g)


def _scan_fwd(a, b, h_in):
    n = a.shape[0]
    row = lax.broadcasted_iota(jnp.int32, a.shape, 0)
    s = 1
    while s < n:
        if s % 8:
            a_sh = jnp.where(row >= s, pltpu.roll(a, s, 0), 1.0)
            b_sh = jnp.where(row >= s, pltpu.roll(b, s, 0), 0.0)
        else:
            a_sh = jnp.concatenate([jnp.ones((s, RB), F32), a[:n - s]], axis=0)
            b_sh = jnp.concatenate([jnp.zeros((s, RB), F32), b[:n - s]], axis=0)
        b = a * b_sh + b
        a = a * a_sh
        s *= 2
    return b + a * h_in


def _scan_rev(a, b, g_in):
    n = a.shape[0]
    row = lax.broadcasted_iota(jnp.int32, a.shape, 0)
    s = 1
    while s < n:
        if s % 8:
            a_sh = jnp.where(row < n - s, pltpu.roll(a, n - s, 0), 1.0)
            b_sh = jnp.where(row < n - s, pltpu.roll(b, n - s, 0), 0.0)
        else:
            a_sh = jnp.concatenate([a[s:], jnp.ones((s, RB), F32)], axis=0)
            b_sh = jnp.concatenate([b[s:], jnp.zeros((s, RB), F32)], axis=0)
        b = a * b_sh + b
        a = a * a_sh
        s *= 2
    return b + a * g_in


def _lru_gates(xc, wa, ba, wi, bi, lam):
    xcb = xc.astype(BF16)
    r = _sigmoid(jnp.dot(xcb, wa, preferred_element_type=F32) + ba)
    ig = _sigmoid(jnp.dot(xcb, wi, preferred_element_type=F32) + bi)
    log_sig = jnp.minimum(lam, 0.0) - jnp.log(1.0 + jnp.exp(-jnp.abs(lam)))
    log_a = LRU_C * r * log_sig
    a = jnp.exp(log_a)
    m2 = jnp.tanh(-log_a) * (1.0 + a * a)
    return r, ig, log_sig, a, m2 * lax.rsqrt(jnp.maximum(m2, 1e-37))


def _rnn_specs(tc, nblk_t, rev):
    def tmap(k):
        return (nblk_t - 1 - k) if rev else k

    hb = tc // 8
    blk = lambda off: pl.BlockSpec((tc, RB), lambda c, k: (tmap(k), c + off))
    halo = lambda off: pl.BlockSpec((8, RB), lambda c, k: (jnp.maximum(tmap(k) * hb - 1, 0), c + off))
    vec = pl.BlockSpec((1, RB), lambda c, k: (0, c))
    cwv = pl.BlockSpec((CW, RB), lambda c, k: (0, c))
    mat = pl.BlockSpec((None, RB, RB), lambda c, k: (c, 0, 0))
    return blk, halo, vec, cwv, mat


def _rnn_fwd(uxg, cw, cb, wa, ba, wi, bi, lam, *, tc=640):
    T = uxg.shape[0]
    nt = T // tc
    nsub = tc // SUB
    blk, halo, vec, cwv, mat = _rnn_specs(tc, nt, False)

    def body(x_ref, xh_ref, ug_ref, cw_ref, cb_ref, wa_ref, ba_ref, wi_ref, bi_ref, lam_ref,
             h_ref, y_ref, xb, hc):
        k = pl.program_id(1)

        @pl.when(k == 0)
        def _():
            hc[...] = jnp.zeros_like(hc)

        xb[0:8, :] = jnp.where(k > 0, xh_ref[...], 0.0)
        xb[8:, :] = x_ref[...]
        cwv_, cbv = cw_ref[...], cb_ref[...]
        wav, wiv = wa_ref[...], wi_ref[...]
        bav, biv, lamv = ba_ref[...], bi_ref[...], lam_ref[...]
        h_in = hc[0:1, :]
        for sc in range(nsub):
            r0 = sc * SUB
            xc = cbv + cwv_[0:1, :] * xb[pl.ds(5 + r0, SUB), :]
            for j in range(1, CW):
                xc = xc + cwv_[j:j + 1, :] * xb[pl.ds(5 + j + r0, SUB), :]
            r, ig, _, a, mm = _lru_gates(xc, wav, bav, wiv, biv, lamv)
            rows = k * tc + r0 + lax.broadcasted_iota(jnp.int32, (SUB, RB), 0)
            b = jnp.where(rows >= PAD, mm * (ig * xc), 0.0)
            h = _scan_fwd(a, b, h_in)
            h_in = h[SUB - 1:SUB, :]
            h_ref[pl.ds(r0, SUB), :] = h
            gl, _ = _gelu_parts(ug_ref[pl.ds(r0, SUB), :])
            y_ref[pl.ds(r0, SUB), :] = (h * gl).astype(BF16)
        hc[0:1, :] = h_in

    return pl.pallas_call(
        body, name="rnn_fwd", grid=(NBLK, nt),
        in_specs=[blk(0), halo(0), blk(NBLK), cwv, vec, mat, vec, mat, vec, vec],
        out_specs=[blk(0), blk(0)],
        out_shape=[jax.ShapeDtypeStruct((T, DR), F32), jax.ShapeDtypeStruct((T, DR), BF16)],
        scratch_shapes=[pltpu.VMEM((tc + 8, RB), F32), pltpu.VMEM((8, RB), F32)],
        compiler_params=_cparams(("parallel", "arbitrary")),
    )(uxg, uxg, uxg, cw, cb, wa, ba, wi, bi, lam)


def _rnn_bwd(uxg, hs, dy, cw, cb, wa, ba, wi, bi, lam, wat, wit, *, tc=640):
    T = uxg.shape[0]
    nt = T // tc
    nsub = tc // SUB
    blk, halo, vec, cwv, mat = _rnn_specs(tc, nt, True)

    def body(x_ref, xh_ref, ug_ref, h_ref, hh_ref, dy_ref, cw_ref, cb_ref, wa_ref, ba_ref, wi_ref,
             bi_ref, lam_ref, wat_ref, wit_ref,
             dux_ref, dug_ref, dcw_ref, dcb_ref, dwa_ref, dba_ref, dwi_ref, dbi_ref, dlam_ref,
             xb, hb, ab, dxb, xcs, rs, igs, mms, dgas, dgis, carry):
        k = pl.program_id(1)
        kt = nt - 1 - k

        @pl.when(k == 0)
        def _():
            carry[...] = jnp.zeros_like(carry)
            for ref in (dcw_ref, dcb_ref, dwa_ref, dba_ref, dwi_ref, dbi_ref, dlam_ref):
                ref[...] = jnp.zeros_like(ref)

        xb[0:8, :] = jnp.where(kt > 0, xh_ref[...], 0.0)
        xb[8:, :] = x_ref[...]
        hb[0:8, :] = jnp.where(kt > 0, hh_ref[...], 0.0)
        hb[8:, :] = h_ref[...]
        cwv_, cbv = cw_ref[...], cb_ref[...]
        wav, wiv = wa_ref[...], wi_ref[...]
        bav, biv, lamv = ba_ref[...], bi_ref[...], lam_ref[...]
        ab[tc:tc + 8, :] = jnp.broadcast_to(carry[1:2, :], (8, RB))
        dxb[tc:tc + 8, :] = carry[8:16, :]
        log_sig = None
        for sc in range(nsub):
            r0 = sc * SUB
            xc = cbv + cwv_[0:1, :] * xb[pl.ds(5 + r0, SUB), :]
            for j in range(1, CW):
                xc = xc + cwv_[j:j + 1, :] * xb[pl.ds(5 + j + r0, SUB), :]
            r, ig, log_sig, a, mm = _lru_gates(xc, wav, bav, wiv, biv, lamv)
            xcs[pl.ds(r0, SUB), :] = xc
            rs[pl.ds(r0, SUB), :] = r
            igs[pl.ds(r0, SUB), :] = ig
            mms[pl.ds(r0, SUB), :] = mm
            ab[pl.ds(r0, SUB), :] = a
        sig_neg = _sigmoid(-lamv)
        g_in = carry[0:1, :]
        dlam_acc = jnp.zeros((1, RB), F32)
        for sc in reversed(range(nsub)):
            r0 = sc * SUB
            xc, r, ig, mm = xcs[pl.ds(r0, SUB), :], rs[pl.ds(r0, SUB), :], igs[pl.ds(r0, SUB), :], mms[pl.ds(r0, SUB), :]
            a = ab[pl.ds(r0, SUB), :]
            a_next = ab[pl.ds(r0 + 1, SUB), :]
            hv = hb[pl.ds(8 + r0, SUB), :]
            hprev = hb[pl.ds(7 + r0, SUB), :]
            dyv = dy_ref[pl.ds(r0, SUB), :]
            gl, dgl = _gelu_parts(ug_ref[pl.ds(r0, SUB), :])
            dug_ref[pl.ds(r0, SUB), :] = (dyv * hv * dgl).astype(BF16)
            G = _scan_rev(a_next, dyv * gl, g_in)
            g_in = G[0:1, :]
            rows = kt * tc + r0 + lax.broadcasted_iota(jnp.int32, (SUB, RB), 0)
            db = jnp.where(rows >= PAD, G, 0.0)
            da = G * hprev
            dmm = db * (ig * xc)
            di = db * (mm * xc)
            dxc = db * (mm * ig)
            dlog_a = da * a - dmm * (a * a) / jnp.maximum(mm, 1e-30)
            dr = dlog_a * (LRU_C * log_sig)
            dlam_acc = dlam_acc + jnp.sum(dlog_a * (LRU_C * r), axis=0, keepdims=True)
            dga = dr * r * (1.0 - r)
            dgi = di * ig * (1.0 - ig)
            dgab, dgib = dga.astype(BF16), dgi.astype(BF16)
            dgas[pl.ds(r0, SUB), :] = dgab
            dgis[pl.ds(r0, SUB), :] = dgib
            dba_ref[...] += jnp.sum(dga, axis=0, keepdims=True)
            dbi_ref[...] += jnp.sum(dgi, axis=0, keepdims=True)
            dxc = dxc + jnp.dot(dgab, wat_ref[...], preferred_element_type=F32) \
                + jnp.dot(dgib, wit_ref[...], preferred_element_type=F32)
            dxb[pl.ds(r0, SUB), :] = dxc
        dlam_ref[...] += dlam_acc * sig_neg
        xcb = xcs[...].astype(BF16)
        tn = (((0,), (0,)), ((), ()))
        dwa_ref[...] += lax.dot_general(xcb, dgas[...], tn, preferred_element_type=F32)
        dwi_ref[...] += lax.dot_general(xcb, dgis[...], tn, preferred_element_type=F32)
        dxc_all = dxb[0:tc, :]
        dcb_ref[...] += jnp.sum(dxc_all, axis=0, keepdims=True)
        rows_all = kt * tc + lax.broadcasted_iota(jnp.int32, (tc, RB), 0)
        dux = jnp.zeros((tc, RB), F32)
        for j in range(CW):
            dcw_ref[j:j + 1, :] += jnp.sum(dxc_all * xb[pl.ds(5 + j, tc), :], axis=0, keepdims=True)
            dux = dux + cwv_[j:j + 1, :] * dxb[pl.ds(CW - 1 - j, tc), :]
        dux_ref[...] = jnp.where(rows_all >= PAD, dux, 0.0).astype(BF16)
        carry[0:1, :] = g_in
        carry[1:2, :] = ab[0:1, :]
        carry[8:16, :] = dxb[0:8, :]

    vec_out = pl.BlockSpec((1, RB), lambda c, k: (0, c))
    return pl.pallas_call(
        body, name="rnn_bwd", grid=(NBLK, nt),
        in_specs=[blk(0), halo(0), blk(NBLK), blk(0), halo(0), blk(0), cwv, vec, mat, vec, mat, vec, vec, mat, mat],
        out_specs=[blk(0), blk(0), cwv, vec_out, mat, vec_out, mat, vec_out, vec_out],
        out_shape=[jax.ShapeDtypeStruct((T, DR), BF16), jax.ShapeDtypeStruct((T, DR), BF16),
                   jax.ShapeDtypeStruct((CW, DR), F32), jax.ShapeDtypeStruct((1, DR), F32),
                   jax.ShapeDtypeStruct((NBLK, RB, RB), F32), jax.ShapeDtypeStruct((1, DR), F32),
                   jax.ShapeDtypeStruct((NBLK, RB, RB), F32), jax.ShapeDtypeStruct((1, DR), F32),
                   jax.ShapeDtypeStruct((1, DR), F32)],
        scratch_shapes=[pltpu.VMEM((tc + 8, RB), F32), pltpu.VMEM((tc + 8, RB), F32),
                        pltpu.VMEM((tc + 8, RB), F32), pltpu.VMEM((tc + 8, RB), F32),
                        pltpu.VMEM((tc, RB), F32), pltpu.VMEM((tc, RB), F32), pltpu.VMEM((tc, RB), F32),
                        pltpu.VMEM((tc, RB), F32), pltpu.VMEM((tc, RB), BF16), pltpu.VMEM((tc, RB), BF16),
                        pltpu.VMEM((16, RB), F32)],
        compiler_params=_cparams(("parallel", "arbitrary")),
    )(uxg, uxg, uxg, hs, hs, dy, cw, cb, wa, ba, wi, bi, lam, wat, wit)


def _attn_prep(q_all, kv_all, ukr, tab, *, tm=320):
    T = q_all.shape[0]

    def body(q_ref, kv_ref, kr_ref, tab_ref, qo_ref, ko_ref, vo_ref):
        tab_v = tab_ref[...]
        lane = lax.broadcasted_iota(jnp.int32, (tm, LANES), 1)
        t1 = kr_ref[...] * tab_v
        kro = jnp.where(lane < ROPE, t1 + pltpu.roll(t1, ROPE, 1), 0.0).astype(BF16)
        for h in range(NH):
            c0 = h * QW
            qo_ref[h, :, 0:NOPE] = (q_ref[:, c0:c0 + NOPE].astype(F32) * SCALE).astype(BF16)
            t2 = q_ref[:, c0 + NOPE:c0 + QW].astype(F32) * tab_v
            qo_ref[h, :, NOPE:QW] = ((t2 + pltpu.roll(t2, ROPE, 1)) * SCALE).astype(BF16)
            ko_ref[h, :, 0:NOPE] = kv_ref[:, c0:c0 + NOPE].astype(BF16)
            ko_ref[h, :, NOPE:QW] = kro
            vo_ref[h, :, :] = kv_ref[:, c0 + NOPE:c0 + QW].astype(BF16)

    return pl.pallas_call(
        body, name="attn_prep", grid=(T // tm,),
        in_specs=[_rows(tm, NH * QW), _rows(tm, NH * QW), _rows(tm, LANES), _rows(tm, LANES)],
        out_specs=[pl.BlockSpec((NH, tm, QW), lambda i: (0, i, 0)), pl.BlockSpec((NH, tm, QW), lambda i: (0, i, 0)),
                   pl.BlockSpec((NH, tm, VD), lambda i: (0, i, 0))],
        out_shape=[jax.ShapeDtypeStruct((NH, T, QW), BF16), jax.ShapeDtypeStruct((NH, T, QW), BF16),
                   jax.ShapeDtypeStruct((NH, T, VD), BF16)],
        compiler_params=_cparams(("parallel",)),
    )(q_all, kv_all, ukr, tab)


def _attn_prep_bwd(dq, dk, dv, tab, *, tm=320):
    T = dq.shape[1]

    def body(dq_ref, dk_ref, dv_ref, tab_ref, dqa_ref, dkva_ref, dkr_ref):
        tab_v = tab_ref[...]
        lane = lax.broadcasted_iota(jnp.int32, (tm, LANES), 1)
        dkro = jnp.zeros((tm, LANES), F32)
        for h in range(NH):
            c0 = h * QW
            dqa_ref[:, c0:c0 + NOPE] = (dq_ref[h, :, 0:NOPE] * SCALE).astype(BF16)
            d2 = dq_ref[h, :, NOPE:QW]
            dqa_ref[:, c0 + NOPE:c0 + QW] = ((d2 + pltpu.roll(d2, ROPE, 1)) * tab_v * SCALE).astype(BF16)
            dkva_ref[:, c0:c0 + NOPE] = dk_ref[h, :, 0:NOPE].astype(BF16)
            dkva_ref[:, c0 + NOPE:c0 + QW] = dv_ref[h, :, :].astype(BF16)
            dkro = dkro + dk_ref[h, :, NOPE:QW].astype(F32)
        dkro = jnp.where(lane < ROPE, dkro, 0.0)
        dkr_ref[...] = ((dkro + pltpu.roll(dkro, ROPE, 1)) * tab_v).astype(BF16)

    return pl.pallas_call(
        body, name="attn_prep_bwd", grid=(T // tm,),
        in_specs=[pl.BlockSpec((NH, tm, QW), lambda i: (0, i, 0)), pl.BlockSpec((NH, tm, QW), lambda i: (0, i, 0)),
                  pl.BlockSpec((NH, tm, VD), lambda i: (0, i, 0)), _rows(tm, LANES)],
        out_specs=[_rows(tm, NH * QW), _rows(tm, NH * QW), _rows(tm, LANES)],
        out_shape=[jax.ShapeDtypeStruct((T, NH * QW), BF16), jax.ShapeDtypeStruct((T, NH * QW), BF16),
                   jax.ShapeDtypeStruct((T, LANES), BF16)],
        compiler_params=_cparams(("parallel",)),
    )(dq, dk, dv, tab)


def _visible(q0, k0, nq, nk):
    rows = q0 + lax.broadcasted_iota(jnp.int32, (nq, nk), 0)
    cols = k0 + lax.broadcasted_iota(jnp.int32, (nq, nk), 1)
    return ((cols >> 6) <= (rows >> 6)) & (cols >= PAD)


def _visible_t(q0, k0, nq, nk):
    cols = k0 + lax.broadcasted_iota(jnp.int32, (nk, nq), 0)
    rows = q0 + lax.broadcasted_iota(jnp.int32, (nk, nq), 1)
    return ((cols >> 6) <= (rows >> 6)) & (cols >= PAD)


_NT = (((1,), (1,)), ((), ()))
ATTN_BLOCK = 1664


def _attn_block(T):
    return ATTN_BLOCK if T % ATTN_BLOCK == 0 else 640


def _round_up(n, m):
    return -(-n // m) * m


def _flash_fwd(q, k, v, *, gather=(), bq=None):
    T = q.shape[1]
    bq = bq or _attn_block(T)
    nq = T // bq
    rs = bq // CHAINS
    n = len(gather)

    def body(*refs):
        q_ref, k_ref, v_ref = refs[:3]
        g_src = refs[3:3 + n]
        o_ref, lse_ref = refs[3 + n:5 + n]
        g_out = refs[5 + n:5 + 2 * n]
        scr = refs[5 + 2 * n:]
        m_s, l_s, acc_s = scr[:CHAINS], scr[CHAINS:2 * CHAINS], scr[2 * CHAINS:3 * CHAINS]
        g_scr = scr[3 * CHAINS:]
        h = pl.program_id(0)
        i = pl.program_id(1)
        if n:
            @pl.when((h == 0) & (i == 0))
            def _():
                _gather_start(_gather_descs(g_src, g_out, g_scr))

        for r in range(CHAINS):
            m_s[r][...] = jnp.full_like(m_s[r], NEG)
            l_s[r][...] = jnp.zeros_like(l_s[r])
            acc_s[r][...] = jnp.zeros_like(acc_s[r])

        def step(j, masked, diag):
            off = pl.multiple_of(j * bq, bq)
            for r in range(CHAINS):
                rows = pl.ds(r * rs, rs)
                kw = min(bq, _round_up((r + 1) * rs, LANES)) if diag else bq
                kv_ = k_ref[pl.ds(off, kw), :]
                vv = v_ref[pl.ds(off, kw), :]
                s = lax.dot_general(q_ref[rows, :], kv_, _NT, preferred_element_type=F32)
                if masked:
                    s = jnp.where(_visible(i * bq + r * rs, j * bq, rs, kw), s, NEG)
                m_prev = m_s[r][...]
                m_new = jnp.maximum(m_prev, jnp.max(s, axis=-1, keepdims=True))
                p = jnp.exp(s - m_new)
                alpha = jnp.exp(m_prev - m_new)
                l_s[r][...] = alpha * l_s[r][...] + jnp.sum(p, axis=-1, keepdims=True)
                acc_s[r][...] = alpha * acc_s[r][...] + jnp.dot(p.astype(BF16), vv, preferred_element_type=F32)
                m_s[r][...] = m_new

        @pl.when(i == 0)
        def _():
            step(0, True, True)

        @pl.when(i > 0)
        def _():
            step(0, True, False)

            def loop(j, c):
                step(j, False, False)
                return c

            lax.fori_loop(1, i, loop, 0)
            step(i, True, True)

        for r in range(CHAINS):
            rows = pl.ds(r * rs, rs)
            o_ref[rows, :] = (acc_s[r][...] / l_s[r][...]).astype(BF16)
            lse_ref[rows, :] = m_s[r][...] + jnp.log(l_s[r][...])

        if n:
            @pl.when((h == NH - 1) & (i == nq - 1))
            def _():
                _gather_wait(_gather_descs(g_src, g_out, g_scr, with_loads=False))

    return pl.pallas_call(
        body, name="flash_fwd", grid=(NH, nq),
        in_specs=[pl.BlockSpec((None, bq, QW), lambda h, i: (h, i, 0)),
                  pl.BlockSpec((None, T, QW), lambda h, i: (h, 0, 0)),
                  pl.BlockSpec((None, T, VD), lambda h, i: (h, 0, 0))] + [HBM] * n,
        out_specs=[pl.BlockSpec((bq, VD), lambda h, i: (i, h)),
                   pl.BlockSpec((None, bq, 1), lambda h, i: (h, i, 0))] + [HBM] * n,
        out_shape=[jax.ShapeDtypeStruct((T, NH * VD), BF16), jax.ShapeDtypeStruct((NH, T, 1), F32)]
        + [jax.ShapeDtypeStruct((4,) + g.shape, g.dtype) for g in gather],
        scratch_shapes=[pltpu.VMEM((rs, 1), F32)] * (2 * CHAINS) + [pltpu.VMEM((rs, VD), F32)] * CHAINS
        + (_gather_scratch(gather) if n else []),
        compiler_params=_cparams(("arbitrary", "arbitrary")),
    )(q, k, v, *gather)


def _attn_delta(o, do, *, tm=640):
    T = o.shape[0]

    def body(o_ref, do_ref, d_ref):
        prod = o_ref[...].astype(F32) * do_ref[...].astype(F32)
        for h in range(NH):
            col = jnp.sum(prod[:, h * VD:(h + 1) * VD], axis=-1, keepdims=True)
            d_ref[h, :, :] = jnp.broadcast_to(col, (tm, LANES)).T[0:1, :]

    return pl.pallas_call(
        body, name="attn_delta", grid=(T // tm,),
        in_specs=[_rows(tm, NH * VD), _rows(tm, NH * VD)],
        out_specs=pl.BlockSpec((NH, 1, tm), lambda i: (0, 0, i)),
        out_shape=jax.ShapeDtypeStruct((NH, 1, T), F32),
        compiler_params=_cparams(("parallel",)),
    )(o, do)


_TN = (((0,), (0,)), ((), ()))


def _flash_bwd(q, k, v, do, lse_row, delta_row, *, scatter=(), bq=None):
    T = q.shape[1]
    bq = bq or _attn_block(T)
    nq = T // bq
    rs = bq // CHAINS
    n = len(scatter)

    def body(*refs):
        q_ref, k_ref, v_ref, do_ref, lse_ref, dl_ref = refs[:6]
        s_src = refs[6:6 + n]
        dq_ref, dk_out, dv_out = refs[6 + n:9 + n]
        s_out = refs[9 + n:9 + 2 * n]
        dk_ref, dv_ref = refs[9 + 2 * n:11 + 2 * n]
        s_scr = refs[11 + 2 * n:]
        h = pl.program_id(0)
        j = pl.program_id(1)
        if n:
            @pl.when((h == 0) & (j == 0))
            def _():
                for cp in _scatter_descs(s_src, s_out, s_scr):
                    cp.start()

        @pl.when(j == 0)
        def _():
            dq_ref[...] = jnp.zeros_like(dq_ref)

        dk_ref[...] = jnp.zeros_like(dk_ref)
        dv_ref[...] = jnp.zeros_like(dv_ref)

        def step(i, masked, diag):
            for r in range(CHAINS):
                rows = pl.ds(r * rs, rs)
                q0 = (r * rs) // LANES * LANES if diag else 0
                qn = bq - q0
                off = pl.multiple_of(i * bq + q0, LANES)
                qv = q_ref[pl.ds(off, qn), :]
                dov = do_ref[pl.ds(off, qn), :]
                lse_v = lse_ref[:, pl.ds(off, qn)]
                dl_v = dl_ref[:, pl.ds(off, qn)]
                st = lax.dot_general(k_ref[rows, :], qv, _NT, preferred_element_type=F32)
                if masked:
                    st = jnp.where(_visible_t(i * bq + q0, j * bq + r * rs, qn, rs), st, NEG)
                pt = jnp.exp(st - lse_v)
                dv_ref[rows, :] += jnp.dot(pt.astype(BF16), dov, preferred_element_type=F32)
                dpt = lax.dot_general(v_ref[rows, :], dov, _NT, preferred_element_type=F32)
                dst = (pt * (dpt - dl_v)).astype(BF16)
                dk_ref[rows, :] += jnp.dot(dst, qv, preferred_element_type=F32)
                dq_ref[pl.ds(off, qn), :] += lax.dot_general(dst, k_ref[rows, :], _TN,
                                                             preferred_element_type=F32)

        step(j, True, True)

        @pl.when(j == 0)
        def _():
            def loop(i, c):
                step(i, True, False)
                return c
            lax.fori_loop(1, nq, loop, 0)

        @pl.when(j > 0)
        def _():
            def loop(i, c):
                step(i, False, False)
                return c
            lax.fori_loop(j + 1, nq, loop, 0)

        dk_out[...] = dk_ref[...].astype(BF16)
        dv_out[...] = dv_ref[...].astype(BF16)

        if n:
            @pl.when((h == NH - 1) & (j == nq - 1))
            def _():
                for cp in _scatter_descs(s_src, s_out, s_scr):
                    cp.wait()

    return pl.pallas_call(
        body, name="flash_bwd", grid=(NH, nq),
        in_specs=[pl.BlockSpec((None, T, QW), lambda h, j: (h, 0, 0)),
                  pl.BlockSpec((None, bq, QW), lambda h, j: (h, j, 0)),
                  pl.BlockSpec((None, bq, VD), lambda h, j: (h, j, 0)),
                  pl.BlockSpec((T, VD), lambda h, j: (0, h)),
                  pl.BlockSpec((None, 1, T), lambda h, j: (h, 0, 0)),
                  pl.BlockSpec((None, 1, T), lambda h, j: (h, 0, 0))] + [HBM] * n,
        out_specs=[pl.BlockSpec((None, T, QW), lambda h, j: (h, 0, 0)),
                   pl.BlockSpec((None, bq, QW), lambda h, j: (h, j, 0)),
                   pl.BlockSpec((None, bq, VD), lambda h, j: (h, j, 0))] + [HBM] * n,
        out_shape=[jax.ShapeDtypeStruct((NH, T, QW), F32), jax.ShapeDtypeStruct((NH, T, QW), BF16),
                   jax.ShapeDtypeStruct((NH, T, VD), BF16)]
        + [jax.ShapeDtypeStruct((3,) + s.shape[1:], s.dtype) for s in scatter],
        scratch_shapes=[pltpu.VMEM((bq, QW), F32), pltpu.VMEM((bq, VD), F32)]
        + ([_dma_sems(3 * n), _dma_sems(3 * n)] if n else []),
        compiler_params=_cparams(("arbitrary", "arbitrary")),
    )(q, k, v, do, lse_row, delta_row, *scatter)


def _rope_table(T):
    pos = (jnp.arange(T, dtype=jnp.int32) - PAD).astype(F32)
    inv_freq = ROPE_THETA ** (-jnp.arange(0, ROPE, 2, dtype=F32) / ROPE)
    ang = pos[:, None] * inv_freq[None, :]
    cos, sin = jnp.cos(ang), jnp.sin(ang)
    return jnp.concatenate([cos, cos, -sin, sin], axis=1)


def _swap_halves(w):
    return jnp.concatenate([w[..., ROPE // 2:], w[..., :ROPE // 2]], axis=-1)


O_UX, O_UG, O_UQ, O_UKV, O_UKR, O_UM = 0, DR, 2 * DR, 2 * DR + QR, 2 * DR + QR + KVR, 2 * DR + QR + KVR + ROPE


def _prep_weights(w):
    b = lambda a: a.astype(BF16)
    w_in = w["w_in"]
    kr = w_in[:, O_UKR:O_UM]
    p = {
        "w_xg": b(w_in[:, :O_UQ]),
        "w_q": b(w_in[:, O_UQ:O_UKV]),
        "w_kv": b(w_in[:, O_UKV:O_UKR]),
        "w_kr": b(jnp.concatenate([kr, _swap_halves(kr)], axis=1)),
        "w_m": b(w_in[:, O_UM:]),
    }
    wq = w["w_uq"].reshape(QR, NH, NOPE + ROPE)
    p["w_uq"] = b(jnp.concatenate([wq, _swap_halves(wq[..., NOPE:])], axis=-1).reshape(QR, NH * QW))
    p["w_ukv"] = b(w["w_ukv"])
    p["w_x"], p["w_g"] = p["w_xg"][:, :DR], p["w_xg"][:, DR:]
    p["wa"] = b(w["w_rec_a"])
    p["wi"] = b(w["w_rec_i"])
    p["wa_t"] = jnp.swapaxes(p["wa"], 1, 2)
    p["wi_t"] = jnp.swapaxes(p["wi"], 1, 2)
    return p


def _prep_late_weights(w):
    b = lambda a: a.astype(BF16)
    p = {"w_br": b(w["w_branch"][:DR]), "w_ba": b(w["w_branch"][DR:]), "w_out": b(w["w_out"]),
         "w_fi": b(w["w_ffn_in"]), "w_fo": b(w["w_ffn_out"])}
    return p


LATE = ("w_branch", "w_out", "w_ffn_in", "w_ffn_out")


def _local_step(x, tgt, w, late=None, reduce_first=None, reduce_second=None):
    S = x.shape[0]
    T = FRONT + S
    p = _prep_weights(w)
    tab = _rope_table(T)
    h0 = jnp.concatenate([jnp.zeros((PAD, D), F32), w["meta_tokens"], x], axis=0)
    row = lambda v: v.reshape(1, -1)

    z = _rmsnorm_fwd(h0, row(w["norm_mix_g"]), name="norm_mix")
    uxg = _mm(z, p["w_xg"], name="mm_uxg")
    uq = _mm(z, p["w_q"], name="mm_uq")
    ukv = _mm(z, p["w_kv"], name="mm_ukv")
    ukr = _mm(z, p["w_kr"], name="mm_ukr")
    um = _mm(z, p["w_m"], name="mm_um", out_dtype=BF16)
    rnn_w = (w["conv_w"], row(w["conv_b"]), p["wa"], row(w["b_rec_a"]), p["wi"], row(w["b_rec_i"]),
             row(w["lru_lambda"]))
    hs, y_rnn = _rnn_fwd(uxg, *rnn_w)
    qn = _rmsnorm_fwd(uq, row(w["q_norm_g"]), name="norm_q")
    kvn = _rmsnorm_fwd(ukv, row(w["kv_norm_g"]), name="norm_kv")
    q_all = _mm(qn, p["w_uq"], name="mm_q", out_dtype=BF16)
    kv_all = _mm(kvn, p["w_ukv"], name="mm_kv", out_dtype=BF16)
    qh, kh, vh = _attn_prep(q_all, kv_all, ukr, tab)
    y_att, lse, *stacks = _flash_fwd(qh, kh, vh, gather=late[0] if late else ())
    if late:
        w = {**w, **late[1](stacks)}
    p.update(_prep_late_weights(w))
    p_rnn = _mm(y_rnn, p["w_br"], name="mm_prnn", out_dtype=BF16)
    p_att = _mm(y_att, p["w_ba"], name="mm_patt", out_dtype=BF16)
    bg = row(w["b_gate"])
    h1, mixed = _mix_out(um, bg, p_rnn, p_att, p["w_out"], h0)
    zf = _rmsnorm_fwd(h1, row(w["norm_ffn_g"]), name="norm_ffn")
    ff = _mm(zf, p["w_fi"], name="mm_ffn_in", out_dtype=BF16)
    h2, act = _ffn_out(ff, p["w_fo"], h1)

    g = {}
    dh2, dh2b, dg_fin, lsum = _loss_head(h2, tgt, row(w["final_norm_g"]))
    loss = 0.5 * jnp.sum(lsum) / D
    g["final_norm_g"] = dg_fin.reshape(-1)
    dact = _mm(dh2b, p["w_fo"], name="mm_dact", out_dtype=BF16, bt=True)
    g["w_ffn_out"] = _mm_tn(act, dh2b, name="mm_dw_ffn_out")
    dff = _swiglu_bwd(ff, dact)
    dzf = _mm(dff, p["w_fi"], name="mm_dzf", bt=True)
    g["w_ffn_in"] = _mm_tn(zf, dff, name="mm_dw_ffn_in")
    dh1, dh1b, dg = _rmsnorm_bwd(h1, row(w["norm_ffn_g"]), dzf, dh2, name="norm_ffn_bwd")
    g["norm_ffn_g"] = dg
    dmixed = _mm(dh1b, p["w_out"], name="mm_dmixed", out_dtype=BF16, bt=True)
    g["w_out"] = _mm_tn(mixed, dh1b, name="mm_dw_out")
    dp_rnn, dp_att, dum, dbg = _gate_mix_bwd(um, bg, p_rnn, p_att, dmixed)
    g["b_gate"] = dbg.reshape(2, D)
    g["w_branch"] = jnp.concatenate([_mm_tn(y_rnn, dp_rnn, name="mm_dw_br"),
                                     _mm_tn(y_att, dp_att, name="mm_dw_ba")], axis=0)
    if reduce_first:
        stacks = reduce_first[0]({n: g[n] for n in LATE})
        big = max(range(len(stacks)), key=lambda k: stacks[k].size)
        rest = [k for k in range(len(stacks)) if k != big]
        dy_rnn, their_big = _mm_take(dp_rnn, p["w_br"], [stacks[big]], name="mm_dy_rnn")
        dy_att, *their_rest = _mm_take(dp_att, p["w_ba"], [stacks[k] for k in rest], name="mm_dy_att",
                                       out_dtype=BF16)
        theirs = [None] * len(stacks)
        theirs[big] = their_big
        for k, t in zip(rest, their_rest):
            theirs[k] = t
        first = reduce_first[1](stacks, theirs)
    else:
        dy_rnn, first = _mm(dp_rnn, p["w_br"], name="mm_dy_rnn", bt=True), ()
        dy_att = _mm(dp_att, p["w_ba"], name="mm_dy_att", out_dtype=BF16, bt=True)
    delta = _attn_delta(y_att, dy_att)
    dq, dk, dv, *received = _flash_bwd(qh, kh, vh, dy_att, lse.reshape(NH, 1, T), delta.reshape(NH, 1, T),
                                       scatter=first)
    dq_all, dkv_all, dukr = _attn_prep_bwd(dq, dk, dv, tab)
    dqn = _mm(dq_all, p["w_uq"], name="mm_dqn", bt=True)
    dkvn = _mm(dkv_all, p["w_ukv"], name="mm_dkvn", bt=True)
    dwq = _mm_tn(qn, dq_all, name="mm_dw_uq").reshape(QR, NH, QW)
    dwq_rope = dwq[..., NOPE:NOPE + ROPE] + _swap_halves(dwq[..., NOPE + ROPE:])
    g["w_uq"] = jnp.concatenate([dwq[..., :NOPE], dwq_rope], axis=-1).reshape(QR, NH * (NOPE + ROPE))
    g["w_ukv"] = _mm_tn(kvn, dkv_all, name="mm_dw_ukv")
    duq, dg = _rmsnorm_bwd(uq, row(w["q_norm_g"]), dqn, None, name="norm_q_bwd", want_f32=False)
    g["q_norm_g"] = dg
    dukv, dg = _rmsnorm_bwd(ukv, row(w["kv_norm_g"]), dkvn, None, name="norm_kv_bwd", want_f32=False)
    g["kv_norm_g"] = dg
    (dux, dug, g["conv_w"], g["conv_b"], g["w_rec_a"], g["b_rec_a"], g["w_rec_i"], g["b_rec_i"],
     g["lru_lambda"]) = _rnn_bwd(uxg, hs, dy_rnn, *rnn_w, p["wa_t"], p["wi_t"])
    dwkr = _mm_tn(z, dukr, name="mm_dw_kr")
    g["w_in"] = jnp.concatenate([
        _mm_tn(z, dux, name="mm_dw_x"), _mm_tn(z, dug, name="mm_dw_g"),
        _mm_tn(z, duq, name="mm_dw_q"), _mm_tn(z, dukv, name="mm_dw_kv"),
        dwkr[:, :ROPE] + _swap_halves(dwkr[:, ROPE:]),
        _mm_tn(z, dum, name="mm_dw_m")], axis=1)
    second = reduce_second({n: g[n] for n in ("w_in", "w_uq", "w_ukv")}) if reduce_second else ()
    dz, *received2 = _mm_sum(
        [(dux, p["w_x"]), (dug, p["w_g"]), (duq, p["w_q"]), (dukv, p["w_kv"]), (dukr, p["w_kr"]),
         (dum, p["w_m"])], name="mm_dz", scatter=second)
    dh0, dg = _rmsnorm_bwd(h0, row(w["norm_mix_g"]), dz, dh1, name="norm_mix_bwd", want_bf16=False)
    g["norm_mix_g"] = dg
    g["meta_tokens"] = dh0[PAD:FRONT]
    return loss, dh0[FRONT:], g, (list(first) + list(second), received + received2)


HBM = pl.BlockSpec(memory_space=pltpu.HBM)
CHIP_FLIPS = ((1, 0), (0, 1), (1, 1))


def _place():
    return lax.axis_index("x"), lax.axis_index("y"), lax.axis_index("c")


def _flip(v, f):
    return 1 - v if f else v


def _dma_sems(n):
    return pltpu.SemaphoreType.DMA((n,))


def _gather_scratch(srcs):
    n = len(srcs)
    return [pltpu.VMEM(s.shape, s.dtype) for s in srcs] + [_dma_sems(3 * n), _dma_sems(3 * n), _dma_sems(n),
                                                            _dma_sems(n)]


def _gather_descs(src_refs, out_refs, scr, with_loads=True):
    n = len(src_refs)
    stage = scr[:n]
    send_sems, recv_sems, in_sems, local_sems = scr[n:]
    x, y, c = _place()
    me = 2 * x + y
    loads, sends, local = [], [], []
    for a in range(n):
        if with_loads:
            loads.append(pltpu.make_async_copy(src_refs[a], stage[a], in_sems.at[a]))
        for k, (fx, fy) in enumerate(CHIP_FLIPS):
            sends.append(pltpu.make_async_remote_copy(
                src_ref=stage[a], dst_ref=out_refs[a].at[me], send_sem=send_sems.at[3 * a + k],
                recv_sem=recv_sems.at[3 * a + k], device_id=(_flip(x, fx), _flip(y, fy), c),
                device_id_type=MESH))
        local.append(pltpu.make_async_copy(stage[a], out_refs[a].at[me], local_sems.at[a]))
    return loads, sends, local


def _gather_start(descs):
    loads, sends, local = descs
    for cp in loads:
        cp.start()
    for a, cp in enumerate(loads):
        cp.wait()
        for s in sends[3 * a:3 * a + 3]:
            s.start()
        local[a].start()


def _gather_wait(descs):
    _, sends, local = descs
    for cp in sends + local:
        cp.wait()


def _allgather_chips(srcs, *, name):
    n = len(srcs)

    def body(*refs):
        descs = _gather_descs(refs[:n], refs[n:2 * n], refs[2 * n:])
        _gather_start(descs)
        _gather_wait(descs)

    return pl.pallas_call(
        body, name=name, in_specs=[HBM] * n, out_specs=[HBM] * n,
        out_shape=[jax.ShapeDtypeStruct((4,) + s.shape, s.dtype) for s in srcs],
        scratch_shapes=_gather_scratch(srcs),
        compiler_params=pltpu.CompilerParams(vmem_limit_bytes=VMEM_LIMIT),
    )(*srcs)


def _allgather_chips_split(srcs, split, *, name):
    n = len(srcs)

    def body(*refs):
        src, out, stage = refs[:n], refs[n:2 * n], refs[2 * n:3 * n]
        send_a, recv_a, send_b, recv_b, in_sems, local_sems = refs[3 * n:]
        x, y, c = _place()
        me = 2 * x + y
        loads = [pltpu.make_async_copy(src[a], stage[a], in_sems.at[a]) for a in range(n)]
        for cp in loads:
            cp.start()

        def half(a, core):
            h = srcs[a].shape[0] // 2
            return pl.ds(pl.multiple_of(core * h, 16), h)

        ici, local = [], []
        for a in range(n):
            loads[a].wait()
            for k, (fx, fy) in enumerate(CHIP_FLIPS):
                s_ref, d_ref = stage[a], out[a].at[me]
                if split[a]:
                    s_ref, d_ref = stage[a].at[half(a, c), :], out[a].at[me, half(a, c), :]
                cp = pltpu.make_async_remote_copy(
                    src_ref=s_ref, dst_ref=d_ref, send_sem=send_a.at[3 * a + k], recv_sem=recv_a.at[3 * a + k],
                    device_id=(_flip(x, fx), _flip(y, fy), c), device_id_type=MESH)
                cp.start()
                ici.append(cp)
            cp = pltpu.make_async_copy(stage[a], out[a].at[me], local_sems.at[a])
            cp.start()
            local.append(cp)
        passed = []
        for a in range(n):
            if not split[a]:
                continue
            for k, (fx, fy) in enumerate(CHIP_FLIPS):
                ici[3 * a + k].wait_recv()
                there = 2 * _flip(x, fx) + _flip(y, fy)
                cp = pltpu.make_async_remote_copy(
                    src_ref=out[a].at[there, half(a, c), :], dst_ref=out[a].at[there, half(a, c), :],
                    send_sem=send_b.at[3 * a + k], recv_sem=recv_b.at[3 * a + k],
                    device_id=(x, y, 1 - c), device_id_type=MESH)
                cp.start()
                passed.append(cp)
        for a in range(n):
            for k in range(3):
                ici[3 * a + k].wait_send()
                if not split[a]:
                    ici[3 * a + k].wait_recv()
        for cp in passed + local:
            cp.wait()

    return pl.pallas_call(
        body, name=name, in_specs=[HBM] * n, out_specs=[HBM] * n,
        out_shape=[jax.ShapeDtypeStruct((4,) + s.shape, s.dtype) for s in srcs],
        scratch_shapes=[pltpu.VMEM(s.shape, s.dtype) for s in srcs]
        + [_dma_sems(3 * n), _dma_sems(3 * n), _dma_sems(3 * n), _dma_sems(3 * n), _dma_sems(n), _dma_sems(n)],
        compiler_params=pltpu.CompilerParams(vmem_limit_bytes=VMEM_LIMIT),
    )(*srcs)


def _scatter_descs(src_refs, out_refs, scr):
    send_sems, recv_sems = scr
    x, y, c = _place()
    copies = []
    for a in range(len(src_refs)):
        for k, (fx, fy) in enumerate(CHIP_FLIPS):
            px, py = _flip(x, fx), _flip(y, fy)
            copies.append(pltpu.make_async_remote_copy(
                src_ref=src_refs[a].at[2 * px + py], dst_ref=out_refs[a].at[k],
                send_sem=send_sems.at[3 * a + k], recv_sem=recv_sems.at[3 * a + k],
                device_id=(px, py, c), device_id_type=MESH))
    return copies


def _scatter_chips(srcs, *, name):
    n = len(srcs)

    def body(*refs):
        copies = _scatter_descs(refs[:n], refs[n:2 * n], refs[2 * n:])
        for cp in copies:
            cp.start()
        for cp in copies:
            cp.wait()

    return pl.pallas_call(
        body, name=name, in_specs=[HBM] * n, out_specs=[HBM] * n,
        out_shape=[jax.ShapeDtypeStruct((3,) + s.shape[1:], s.dtype) for s in srcs],
        scratch_shapes=[_dma_sems(3 * n), _dma_sems(3 * n)],
    )(*srcs)


def _take_descs(src_refs, out_refs, scr, heights):
    send_sems, recv_sems = scr
    x, y, c = _place()
    copies = []
    for a, h in enumerate(heights):
        theirs = pl.ds(pl.multiple_of((1 - c) * h, 8), h)
        copies.append(pltpu.make_async_remote_copy(
            src_ref=src_refs[a].at[:, theirs, :], dst_ref=out_refs[a], send_sem=send_sems.at[a],
            recv_sem=recv_sems.at[a], device_id=(x, y, 1 - c), device_id_type=MESH))
    return copies


def _sibling_take(srcs, *, name):
    n = len(srcs)
    heights = [s.shape[1] // 2 for s in srcs]

    def body(*refs):
        copies = _take_descs(refs[:n], refs[n:2 * n], refs[2 * n:], heights)
        for cp in copies:
            cp.start()
        for cp in copies:
            cp.wait()

    return pl.pallas_call(
        body, name=name, in_specs=[HBM] * n, out_specs=[HBM] * n,
        out_shape=[jax.ShapeDtypeStruct((4, s.shape[1] // 2, s.shape[2]), s.dtype) for s in srcs],
        scratch_shapes=[_dma_sems(n), _dma_sems(n)],
    )(*srcs)


def _sibling_swap(srcs, *, name):
    n = len(srcs)

    def body(*refs):
        src_refs, out_refs = refs[:n], refs[n:2 * n]
        send_sems, recv_sems = refs[2 * n:]
        x, y, c = _place()
        copies = []
        for a in range(n):
            cp = pltpu.make_async_remote_copy(
                src_ref=src_refs[a], dst_ref=out_refs[a], send_sem=send_sems.at[a],
                recv_sem=recv_sems.at[a], device_id=(x, y, 1 - c), device_id_type=MESH)
            cp.start()
            copies.append(cp)
        for cp in copies:
            cp.wait()

    return pl.pallas_call(
        body, name=name, in_specs=[HBM] * n, out_specs=[HBM] * n,
        out_shape=[jax.ShapeDtypeStruct(s.shape, s.dtype) for s in srcs],
        scratch_shapes=[_dma_sems(n), _dma_sems(n)],
    )(*srcs)


def _row_tile(rows, cols, n_arrays, step=16):
    budget = 24 * 1024 * 1024 // (2 * 4 * n_arrays * cols)
    best = step
    for t in range(step, rows + 1, step):
        if rows % t == 0 and t <= budget:
            best = t
    assert rows % best == 0, (rows, cols)
    return best


def _add_halves(full, theirs, core, wire, *, name):
    _, h, c = theirs.shape
    tm = _row_tile(h, c, 3)
    nb = h // tm

    def body(core_ref, a_ref, b_ref, o_ref):
        o_ref[...] = (a_ref[...] + b_ref[...]).astype(wire)

    spec = pl.BlockSpec((None, tm, c), lambda s, i, core_ref: (s, i, 0))
    grid_spec = pltpu.PrefetchScalarGridSpec(
        num_scalar_prefetch=1, grid=(4, nb),
        in_specs=[pl.BlockSpec((None, tm, c), lambda s, i, core_ref: (s, core_ref[0] * nb + i, 0)), spec],
        out_specs=spec)
    return pl.pallas_call(
        body, name=name, grid_spec=grid_spec, out_shape=jax.ShapeDtypeStruct(theirs.shape, wire),
        compiler_params=_cparams(("parallel", "parallel")),
    )(core.reshape(1), full, theirs)


def _sum4(own, recv, *, name):
    h, c = own.shape
    tm = _row_tile(h, c, 5)

    def body(o_ref, r_ref, out_ref):
        f = lambda k: r_ref[k].astype(F32)
        out_ref[...] = ((o_ref[...].astype(F32) + f(0)) + f(1)) + f(2)

    return pl.pallas_call(
        body, name=name, grid=(h // tm,),
        in_specs=[_rows(tm, c), pl.BlockSpec((3, tm, c), lambda i: (0, i, 0))],
        out_specs=_rows(tm, c), out_shape=jax.ShapeDtypeStruct((h, c), F32),
        compiler_params=_cparams(("parallel",)),
    )(own, recv)


def _adamw(g, w, m, v, *, name):
    r, c = g.shape
    tm = _row_tile(r, c, 7, step=8)
    c1 = 1.0 / (1.0 - ADAM_B1 ** ADAM_STEP)
    c2 = 1.0 / (1.0 - ADAM_B2 ** ADAM_STEP)

    def body(g_ref, w_ref, m_ref, v_ref, d_ref, nm_ref, nv_ref):
        gv = g_ref[...]
        nm = ADAM_B1 * m_ref[...] + (1.0 - ADAM_B1) * gv
        nv = ADAM_B2 * v_ref[...] + (1.0 - ADAM_B2) * (gv * gv)
        nm_ref[...] = nm
        nv_ref[...] = nv
        d_ref[...] = -ADAM_LR * ((nm * c1) / (jnp.sqrt(nv * c2) + ADAM_EPS) + ADAM_WD * w_ref[...])

    spec = _rows(tm, c)
    shape = jax.ShapeDtypeStruct((r, c), F32)
    return pl.pallas_call(
        body, name=name, grid=(r // tm,), in_specs=[spec] * 4, out_specs=[spec] * 3,
        out_shape=[shape] * 3, compiler_params=_cparams(("parallel",)),
    )(g, w, m, v)


def _adamw_halves(mine, theirs, core, w, m, v, *, name):
    h, c = mine.shape
    tm = _row_tile(h, c, 10, step=8)
    nb = h // tm
    c1 = 1.0 / (1.0 - ADAM_B1 ** ADAM_STEP)
    c2 = 1.0 / (1.0 - ADAM_B2 ** ADAM_STEP)

    def body(core_ref, a_ref, b_ref, w_ref, m_ref, v_ref, g_ref, d_ref, nm_ref, nv_ref):
        gv = jnp.where(pl.program_id(0) // nb == core_ref[0], a_ref[...], b_ref[...])
        nm = ADAM_B1 * m_ref[...] + (1.0 - ADAM_B1) * gv
        nv = ADAM_B2 * v_ref[...] + (1.0 - ADAM_B2) * (gv * gv)
        g_ref[...] = gv
        nm_ref[...] = nm
        nv_ref[...] = nv
        d_ref[...] = -ADAM_LR * ((nm * c1) / (jnp.sqrt(nv * c2) + ADAM_EPS) + ADAM_WD * w_ref[...])

    half = pl.BlockSpec((tm, c), lambda i, core_ref: (i % nb, 0))
    spec = pl.BlockSpec((tm, c), lambda i, core_ref: (i, 0))
    grid_spec = pltpu.PrefetchScalarGridSpec(
        num_scalar_prefetch=1, grid=(2 * nb,), in_specs=[half, half, spec, spec, spec], out_specs=[spec] * 4)
    return pl.pallas_call(
        body, name=name, grid_spec=grid_spec, out_shape=[jax.ShapeDtypeStruct((2 * h, c), F32)] * 4,
        compiler_params=_cparams(("parallel",)),
    )(core.reshape(1), mine, theirs, w, m, v)


BIG = (("w_in", (D, 1328), 1), ("w_uq", (QR, 384), 1), ("w_ukv", (KVR, 512), 1), ("w_branch", (576, D), 0),
       ("w_out", (256, D), 0), ("w_ffn_in", (D, 1408), 1), ("w_ffn_out", (704, D), 0))
SMALL = (("meta_tokens", (NMETA, 256), 1), ("b_gate", (2, 256), 1), ("conv_w", (CW, 320), 1))
REPL = (("norm_mix_g", (D,)), ("conv_b", (DR,)), ("w_rec_a", (NBLK, RB, RB)), ("b_rec_a", (DR,)),
        ("w_rec_i", (NBLK, RB, RB)), ("b_rec_i", (DR,)), ("lru_lambda", (DR,)), ("q_norm_g", (QR,)),
        ("kv_norm_g", (KVR,)), ("norm_ffn_g", (D,)), ("final_norm_g", (D,)))
WEIGHTS = ("meta_tokens", "norm_mix_g", "w_in", "b_gate", "conv_w", "conv_b", "w_rec_a", "b_rec_a", "w_rec_i",
           "b_rec_i", "lru_lambda", "q_norm_g", "w_uq", "kv_norm_g", "w_ukv", "w_branch", "w_out", "norm_ffn_g",
           "w_ffn_in", "w_ffn_out", "final_norm_g")
W = 1024
SMALL_N = sum(math.prod(s) for _, s, _ in SMALL)
SMALL_ROWS = 8
REPL_N = sum(math.prod(s) for _, s in REPL)
QUART_ROWS = 88
assert SMALL_N <= SMALL_ROWS * W and REPL_N <= 4 * QUART_ROWS * W


def _flat_pad(parts, rows):
    v = jnp.concatenate([p.reshape(-1) for p in parts])
    return jnp.pad(v, (0, rows * W - v.shape[0])).reshape(rows, W)


def _shard_stack(full, shard_shape, axis):
    r, cs = shard_shape
    if axis == 0:
        return full.reshape(4, r, cs)
    return jnp.stack([full[:, s * cs:(s + 1) * cs] for s in range(4)])


def _unshard(stack, axis):
    if axis == 0:
        return stack.reshape(4 * stack.shape[1], stack.shape[2])
    return jnp.concatenate([stack[s] for s in range(4)], axis=1)


def _split(flat, table):
    out, off = {}, 0
    for name, shape, *_ in table:
        n = math.prod(shape)
        out[name] = flat[..., off:off + n].reshape(flat.shape[:-1] + tuple(shape))
        off += n
    return out


def kernel(x, meta_tokens, norm_mix_g, w_in, b_gate, conv_w, conv_b, w_rec_a, b_rec_a, w_rec_i, b_rec_i, lru_lambda, q_norm_g, w_uq, kv_norm_g, w_ukv, w_branch, w_out, norm_ffn_g, w_ffn_in, w_ffn_out, final_norm_g, loss_target, m_meta_tokens, m_norm_mix_g, m_w_in, m_b_gate, m_conv_w, m_conv_b, m_w_rec_a, m_b_rec_a, m_w_rec_i, m_b_rec_i, m_lru_lambda, m_q_norm_g, m_w_uq, m_kv_norm_g, m_w_ukv, m_w_branch, m_w_out, m_norm_ffn_g, m_w_ffn_in, m_w_ffn_out, m_final_norm_g, v_meta_tokens, v_norm_mix_g, v_w_in, v_b_gate, v_conv_w, v_conv_b, v_w_rec_a, v_b_rec_a, v_w_rec_i, v_b_rec_i, v_lru_lambda, v_q_norm_g, v_w_uq, v_kv_norm_g, v_w_ukv, v_w_branch, v_w_out, v_norm_ffn_g, v_w_ffn_in, v_w_ffn_out, v_final_norm_g):
    args = dict(locals())
    chip = 2 * lax.axis_index("x") + lax.axis_index("y")
    core = lax.axis_index("c")

    first_big = [b for b in BIG if b[0] not in LATE]
    late_big = [b for b in BIG if b[0] in LATE]
    bf16_shard = lambda n, s: args[n].reshape(s).astype(BF16)
    small = _flat_pad([args[n] for n, _, _ in SMALL], SMALL_ROWS)
    gathered = _allgather_chips_split([bf16_shard(n, s) for n, s, _ in first_big] + [small],
                                      [True] * len(first_big) + [False], name="gather_weights")
    w = {}
    for (name, _, axis), stack in zip(first_big, gathered):
        w[name] = _unshard(stack, axis)
    small_parts = _split(gathered[-1].reshape(4, SMALL_ROWS * W), SMALL)
    for name, _, axis in SMALL:
        w[name] = _unshard(small_parts[name], axis)
    for name, shape in REPL:
        w[name] = args[name].reshape(shape)
    finish_late = lambda stacks: {name: _unshard(st, axis) for (name, _, axis), st in zip(late_big, stacks)}

    def add_theirs(red, theirs, tag, wires):
        return [_add_halves(a, t, core, wires[k], name=f"add_sibling_{tag}{k}")
                for k, (a, t) in enumerate(zip(red, theirs))]

    def to_wire(red, tag, wires):
        return add_theirs(red, _sibling_take(red, name="reduce_sibling_" + tag), tag, wires)

    reduce_first = (lambda gl: [_shard_stack(gl[n], s, a) for n, s, a in late_big],
                    lambda red, theirs: add_theirs(red, theirs, "a", [BF16] * len(late_big)))
    reduce_second = lambda gl: to_wire([_shard_stack(gl[n], s, a) for n, s, a in first_big], "b",
                                       [BF16] * len(first_big))
    loss, grad_x, g, (parts_ab, recv_ab) = _local_step(
        x[0], loss_target[0], w, late=([bf16_shard(n, s) for n, s, _ in late_big], finish_late),
        reduce_first=reduce_first, reduce_second=reduce_second)
    loss = lax.psum(loss, ("x", "y", "c"))

    small_g = jnp.concatenate([_shard_stack(g[n], s, a).reshape(4, -1) for n, s, a in SMALL], axis=1)
    small_g = jnp.pad(small_g, ((0, 0), (0, SMALL_ROWS * W - SMALL_N))).reshape(4, SMALL_ROWS, W)
    repl_g = _flat_pad([g[n] for n, _ in REPL], 4 * QUART_ROWS).reshape(4, QUART_ROWS, W)
    parts_c = to_wire([jnp.concatenate([small_g, repl_g], axis=1)], "c", [F32])
    recv_c = _scatter_chips(parts_c, name="reduce_chips")
    order = [b[0] for b in late_big] + [b[0] for b in first_big] + ["misc"]
    halves = [_sum4(lax.dynamic_index_in_dim(p, chip, 0, keepdims=False), r, name="sum_chips_" + n)
              for n, p, r in zip(order, list(parts_ab) + parts_c, list(recv_ab) + list(recv_c))]
    others = _sibling_swap(halves, name="share_sibling")

    results = {}
    shape_of = {name: shape for name, shape, _ in BIG}
    for name, mine, theirs in zip(order[:-1], halves, others):
        shape = shape_of[name]
        results[name] = _adamw_halves(mine, theirs, core, args[name].reshape(shape),
                                      args["m_" + name].reshape(shape), args["v_" + name].reshape(shape),
                                      name="adamw_" + name)

    a, b = halves[-1], others[-1]
    g_mine = jnp.where(core == 0, jnp.concatenate([a, b], axis=0), jnp.concatenate([b, a], axis=0))
    g_repl = _allgather_chips([g_mine[SMALL_ROWS:]], name="gather_repl")[0].reshape(4 * QUART_ROWS, W)
    g_misc = jnp.concatenate([g_mine[:SMALL_ROWS], g_repl], axis=0)
    misc_state = lambda prefix: jnp.concatenate(
        [_flat_pad([args[prefix + n] for n, _, _ in SMALL], SMALL_ROWS),
         _flat_pad([args[prefix + n] for n, _ in REPL], 4 * QUART_ROWS)], axis=0)
    d, nm, nv = _adamw(g_misc, misc_state(""), misc_state("m_"), misc_state("v_"), name="adamw_misc")
    misc = (g_misc, d, nm, nv)

    outs = []
    for k in range(4):
        sm = _split(misc[k][:SMALL_ROWS].reshape(-1), SMALL)
        rp = _split(misc[k][SMALL_ROWS:].reshape(-1), REPL)
        for name in WEIGHTS:
            val = results[name][k] if name in results else (sm[name] if name in sm else rp[name])
            outs.append(val.reshape(args[name].shape))
    return (loss, grad_x[None], *outs)
```
